```python
import math
import jax, jax.numpy as jnp
from jax import lax
import numpy as np

D_MODEL = 1024
BATCH = 8
SEQ = 4096
DEPTH = 4

N_MIXERS = 3
BLOCK = 128
NORM_EPS = 1e-6

SB_HEADS = 16
SB_HEAD_DIM = 64
SB_WIDTH = SB_HEADS * SB_HEAD_DIM

MLA_HEADS = 8
MLA_NOPE_DIM = 128
MLA_ROPE_DIM = 64
MLA_QK_DIM = MLA_NOPE_DIM + MLA_ROPE_DIM
MLA_V_DIM = 128
MLA_Q_RANK = 256
MLA_KV_RANK = 128
MLA_WIDTH = MLA_HEADS * MLA_V_DIM
ROPE_THETA = 10000.0

SWA_HEADS = 16
SWA_KV_HEADS = 4
SWA_GROUP = SWA_HEADS // SWA_KV_HEADS
SWA_HEAD_DIM = 64
SWA_WINDOW = 128
SWA_WIDTH = SWA_HEADS * SWA_HEAD_DIM

kernel_name = 'hybrid_sb_mla_swa_gated_trunk'


def rms_norm(x, g):
    xf = x.astype(jnp.float32)
    y = xf * lax.rsqrt(jnp.mean(xf * xf, axis=-1, keepdims=True) + NORM_EPS)
    return (y * g.astype(jnp.float32)).astype(x.dtype)


def rope(x, pos):
    half = x.shape[-1] // 2
    inv_freq = ROPE_THETA ** (-jnp.arange(half, dtype=jnp.float32) / half)
    ang = pos.astype(jnp.float32)[:, None] * inv_freq[None, :]
    cos = jnp.cos(ang)[None, :, None, :]
    sin = jnp.sin(ang)[None, :, None, :]
    xf = x.astype(jnp.float32)
    x1, x2 = xf[..., :half], xf[..., half:]
    out = jnp.concatenate([x1 * cos - x2 * sin, x2 * cos + x1 * sin], axis=-1)
    return out.astype(x.dtype)


def alibi_slopes(n_heads):
    return 2.0 ** (-8.0 * jnp.arange(1, n_heads + 1, dtype=jnp.float32) / n_heads)


def stick_breaking_attention(q, k, v):
    B, S, H, d = q.shape
    scale = 1.0 / math.sqrt(d)
    outs = []
    for i in range(S // BLOCK):
        t0 = i * BLOCK
        kl = t0 + BLOCK
        z = jnp.einsum('bthd,bshd->bhts', q[:, t0:kl], k[:, :kl]).astype(jnp.float32) * scale
        t_idx = t0 + jnp.arange(BLOCK)[:, None]
        s_idx = jnp.arange(kl)[None, :]
        mask = s_idx < t_idx
        log_fail = jnp.where(mask, jax.nn.log_sigmoid(-z), 0.0)
        later = lax.cumsum(log_fail, axis=3, reverse=True) - log_fail
        a = jnp.where(mask, jnp.exp(jax.nn.log_sigmoid(z) + later), 0.0)
        outs.append(jnp.einsum('bhts,bshd->bthd', a.astype(v.dtype), v[:, :kl]))
    return jnp.concatenate(outs, axis=1)


def causal_softmax_attention(q, k, v, scale):
    B, S, H, _ = q.shape
    outs = []
    for i in range(S // BLOCK):
        t0 = i * BLOCK
        kl = t0 + BLOCK
        s = jnp.einsum('bthd,bshd->bhts', q[:, t0:kl], k[:, :kl]).astype(jnp.float32) * scale
        mask = jnp.arange(kl)[None, :] <= (t0 + jnp.arange(BLOCK)[:, None])
        p = jax.nn.softmax(jnp.where(mask, s, -jnp.inf), axis=-1)
        outs.append(jnp.einsum('bhts,bshd->bthd', p.astype(v.dtype), v[:, :kl]))
    return jnp.concatenate(outs, axis=1)


def sliding_window_sink_attention(q, k, v, sinks):
    B, S, H, d = q.shape
    nb = S // BLOCK
    qb = q.reshape(B, nb, BLOCK, SWA_KV_HEADS, SWA_GROUP, d)

    def band(t):
        prev = jnp.pad(t, ((0, 0), (BLOCK, 0), (0, 0), (0, 0)))[:, :S]
        return jnp.concatenate([prev.reshape(B, nb, BLOCK, SWA_KV_HEADS, d),
                                t.reshape(B, nb, BLOCK, SWA_KV_HEADS, d)], axis=2)

    kb, vb = band(k), band(v)
    s = jnp.einsum('bnqkgd,bnskd->bnkgqs', qb, kb).astype(jnp.float32) / math.sqrt(d)
    rel = jnp.arange(BLOCK)[:, None] + BLOCK - jnp.arange(2 * BLOCK)[None, :]
    abs_s = jnp.arange(nb)[:, None] * BLOCK - BLOCK + jnp.arange(2 * BLOCK)[None, :]
    valid = ((rel >= 0) & (rel < SWA_WINDOW))[None, :, :] & (abs_s >= 0)[:, None, :]
    slopes = alibi_slopes(SWA_HEADS).reshape(SWA_KV_HEADS, SWA_GROUP)
    s = s - slopes[:, :, None, None] * rel.astype(jnp.float32)
    s = jnp.where(valid[None, :, None, None, :, :], s, -jnp.inf)
    sink = sinks.astype(jnp.float32).reshape(SWA_KV_HEADS, SWA_GROUP)[None, None, :, :, None, None]
    m = jnp.maximum(jnp.max(s, axis=-1, keepdims=True), sink)
    e = jnp.exp(s - m)
    p = e / (jnp.sum(e, axis=-1, keepdims=True) + jnp.exp(sink - m))
    o = jnp.einsum('bnkgqs,bnskd->bnqkgd', p.astype(v.dtype), vb)
    return o.reshape(B, S, H, d)


def stick_breaking_layer(x, norm_g, w_in, w_out):
    B, S, _ = x.shape
    proj = rms_norm(x, norm_g) @ w_in
    q, k, v, gate = jnp.split(proj, 4, axis=-1)
    q = q.reshape(B, S, SB_HEADS, SB_HEAD_DIM)
    k = k.reshape(B, S, SB_HEADS, SB_HEAD_DIM)
    v = v.reshape(B, S, SB_HEADS, SB_HEAD_DIM)
    o = stick_breaking_attention(q, k, v).reshape(B, S, SB_WIDTH)
    return x + (o * jax.nn.silu(gate)) @ w_out


def mla_layer(x, norm_g, w_in, q_a_norm, w_uq, kv_a_norm, w_ukv, q_head_norm, k_head_norm, w_out):
    B, S, _ = x.shape
    proj = rms_norm(x, norm_g) @ w_in
    c1 = MLA_Q_RANK
    c2 = c1 + MLA_KV_RANK
    c3 = c2 + MLA_ROPE_DIM
    q_lat, kv_lat, k_pe, gate = proj[..., :c1], proj[..., c1:c2], proj[..., c2:c3], proj[..., c3:]
    q = (rms_norm(q_lat, q_a_norm) @ w_uq).reshape(B, S, MLA_HEADS, MLA_QK_DIM)
    kv = (rms_norm(kv_lat, kv_a_norm) @ w_ukv).reshape(B, S, MLA_HEADS, MLA_NOPE_DIM + MLA_V_DIM)
    k_nope, v = kv[..., :MLA_NOPE_DIM], kv[..., MLA_NOPE_DIM:]
    k_pe = jnp.broadcast_to(k_pe[:, :, None, :], (B, S, MLA_HEADS, MLA_ROPE_DIM))
    k = jnp.concatenate([k_nope, k_pe], axis=-1)
    q = rms_norm(q, q_head_norm)
    k = rms_norm(k, k_head_norm)
    pos = jnp.arange(S)
    q = jnp.concatenate([q[..., :MLA_NOPE_DIM], rope(q[..., MLA_NOPE_DIM:], pos)], axis=-1)
    k = jnp.concatenate([k[..., :MLA_NOPE_DIM], rope(k[..., MLA_NOPE_DIM:], pos)], axis=-1)
    o = causal_softmax_attention(q, k, v, 1.0 / math.sqrt(MLA_QK_DIM)).reshape(B, S, MLA_WIDTH)
    return x + (o * jax.nn.silu(gate)) @ w_out


def swa_layer(x, norm_g, w_in, q_head_norm, k_head_norm, sinks, w_out):
    B, S, _ = x.shape
    proj = rms_norm(x, norm_g) @ w_in
    kv_w = SWA_KV_HEADS * SWA_HEAD_DIM
    c1 = SWA_WIDTH
    c2 = c1 + kv_w
    c3 = c2 + kv_w
    q = proj[..., :c1].reshape(B, S, SWA_HEADS, SWA_HEAD_DIM)
    k = proj[..., c1:c2].reshape(B, S, SWA_KV_HEADS, SWA_HEAD_DIM)
    v = proj[..., c2:c3].reshape(B, S, SWA_KV_HEADS, SWA_HEAD_DIM)
    gate = proj[..., c3:]
    q = rms_norm(q, q_head_norm)
    k = rms_norm(k, k_head_norm)
    o = sliding_window_sink_attention(q, k, v, sinks).reshape(B, S, SWA_WIDTH)
    return x + (o * jax.nn.silu(gate)) @ w_out


def _dense(key, fan_in, fan_out):
    return jax.random.normal(key, (fan_in, fan_out), jnp.float32) * fan_in ** -0.5


def _gain(key, n):
    return 1.0 + 0.02 * jax.random.normal(key, (n,), jnp.float32)


def _fwd_setup_inputs(seed: int = 0) -> dict:
    key = jax.random.key(seed)
    ks = jax.random.split(key, 24)
    sb_in = 4 * SB_WIDTH
    mla_in = MLA_Q_RANK + MLA_KV_RANK + MLA_ROPE_DIM + MLA_WIDTH
    swa_in = SWA_WIDTH + 2 * SWA_KV_HEADS * SWA_HEAD_DIM + SWA_WIDTH
    return {
        'x': jax.random.normal(ks[0], (BATCH, SEQ, D_MODEL), jnp.float32),
        'l0_norm': _gain(ks[1], D_MODEL),
        'l0_w_in': _dense(ks[2], D_MODEL, sb_in),
        'l0_w_out': _dense(ks[3], SB_WIDTH, D_MODEL),
        'l1_norm': _gain(ks[4], D_MODEL),
        'l1_w_in': _dense(ks[5], D_MODEL, mla_in),
        'l1_q_a_norm': _gain(ks[6], MLA_Q_RANK),
        'l1_w_uq': _dense(ks[7], MLA_Q_RANK, MLA_HEADS * MLA_QK_DIM),
        'l1_kv_a_norm': _gain(ks[8], MLA_KV_RANK),
        'l1_w_ukv': _dense(ks[9], MLA_KV_RANK, MLA_HEADS * (MLA_NOPE_DIM + MLA_V_DIM)),
        'l1_q_head_norm': _gain(ks[10], MLA_QK_DIM),
        'l1_k_head_norm': _gain(ks[11], MLA_QK_DIM),
        'l1_w_out': _dense(ks[12], MLA_WIDTH, D_MODEL),
        'l2_norm': _gain(ks[13], D_MODEL),
        'l2_w_in': _dense(ks[14], D_MODEL, swa_in),
        'l2_q_head_norm': _gain(ks[15], SWA_HEAD_DIM),
        'l2_k_head_norm': _gain(ks[16], SWA_HEAD_DIM),
        'l2_sinks': 0.5 * jax.random.normal(ks[17], (SWA_HEADS,), jnp.float32),
        'l2_w_out': _dense(ks[18], SWA_WIDTH, D_MODEL),
        'l3_norm': _gain(ks[19], D_MODEL),
        'l3_w_in': _dense(ks[20], D_MODEL, sb_in),
        'l3_w_out': _dense(ks[21], SB_WIDTH, D_MODEL),
    }


def _fwd_reference(x, l0_norm, l0_w_in, l0_w_out,
              l1_norm, l1_w_in, l1_q_a_norm, l1_w_uq, l1_kv_a_norm, l1_w_ukv,
              l1_q_head_norm, l1_k_head_norm, l1_w_out,
              l2_norm, l2_w_in, l2_q_head_norm, l2_k_head_norm, l2_sinks, l2_w_out,
              l3_norm, l3_w_in, l3_w_out):
    layer_params = [
        (l0_norm, l0_w_in, l0_w_out),
        (l1_norm, l1_w_in, l1_q_a_norm, l1_w_uq, l1_kv_a_norm, l1_w_ukv,
         l1_q_head_norm, l1_k_head_norm, l1_w_out),
        (l2_norm, l2_w_in, l2_q_head_norm, l2_k_head_norm, l2_sinks, l2_w_out),
        (l3_norm, l3_w_in, l3_w_out),
    ]
    mixers = (stick_breaking_layer, mla_layer, swa_layer)
    for i in range(DEPTH):
        x = mixers[i % N_MIXERS](x, *layer_params[i])
    return x


import jax as _jax
import jax.numpy as _jnp

TWIN_FORMAT = 'train_step'
FWD_PARAMS = ['x', 'l0_norm', 'l0_w_in', 'l0_w_out', 'l1_norm', 'l1_w_in', 'l1_q_a_norm', 'l1_w_uq', 'l1_kv_a_norm', 'l1_w_ukv', 'l1_q_head_norm', 'l1_k_head_norm', 'l1_w_out', 'l2_norm', 'l2_w_in', 'l2_q_head_norm', 'l2_k_head_norm', 'l2_sinks', 'l2_w_out', 'l3_norm', 'l3_w_in', 'l3_w_out']
TWIN_WEIGHTS = ['l0_norm', 'l0_w_in', 'l0_w_out', 'l1_norm', 'l1_w_in', 'l1_q_a_norm', 'l1_w_uq', 'l1_kv_a_norm', 'l1_w_ukv', 'l1_q_head_norm', 'l1_k_head_norm', 'l1_w_out', 'l2_norm', 'l2_w_in', 'l2_q_head_norm', 'l2_k_head_norm', 'l2_sinks', 'l2_w_out', 'l3_norm', 'l3_w_in', 'l3_w_out']
TWIN_DIFF_INPUT = 'x'
TWIN_INPUTS = ['x', 'l0_norm', 'l0_w_in', 'l0_w_out', 'l1_norm', 'l1_w_in', 'l1_q_a_norm', 'l1_w_uq', 'l1_kv_a_norm', 'l1_w_ukv', 'l1_q_head_norm', 'l1_k_head_norm', 'l1_w_out', 'l2_norm', 'l2_w_in', 'l2_q_head_norm', 'l2_k_head_norm', 'l2_sinks', 'l2_w_out', 'l3_norm', 'l3_w_in', 'l3_w_out', 'loss_target', 'm_l0_norm', 'm_l0_w_in', 'm_l0_w_out', 'm_l1_norm', 'm_l1_w_in', 'm_l1_q_a_norm', 'm_l1_w_uq', 'm_l1_kv_a_norm', 'm_l1_w_ukv', 'm_l1_q_head_norm', 'm_l1_k_head_norm', 'm_l1_w_out', 'm_l2_norm', 'm_l2_w_in', 'm_l2_q_head_norm', 'm_l2_k_head_norm', 'm_l2_sinks', 'm_l2_w_out', 'm_l3_norm', 'm_l3_w_in', 'm_l3_w_out', 'v_l0_norm', 'v_l0_w_in', 'v_l0_w_out', 'v_l1_norm', 'v_l1_w_in', 'v_l1_q_a_norm', 'v_l1_w_uq', 'v_l1_kv_a_norm', 'v_l1_w_ukv', 'v_l1_q_head_norm', 'v_l1_k_head_norm', 'v_l1_w_out', 'v_l2_norm', 'v_l2_w_in', 'v_l2_q_head_norm', 'v_l2_k_head_norm', 'v_l2_sinks', 'v_l2_w_out', 'v_l3_norm', 'v_l3_w_in', 'v_l3_w_out']
TWIN_OUTPUTS = ['loss', 'grad_x', 'grad_l0_norm', 'grad_l0_w_in', 'grad_l0_w_out', 'grad_l1_norm', 'grad_l1_w_in', 'grad_l1_q_a_norm', 'grad_l1_w_uq', 'grad_l1_kv_a_norm', 'grad_l1_w_ukv', 'grad_l1_q_head_norm', 'grad_l1_k_head_norm', 'grad_l1_w_out', 'grad_l2_norm', 'grad_l2_w_in', 'grad_l2_q_head_norm', 'grad_l2_k_head_norm', 'grad_l2_sinks', 'grad_l2_w_out', 'grad_l3_norm', 'grad_l3_w_in', 'grad_l3_w_out', 'delta_l0_norm', 'delta_l0_w_in', 'delta_l0_w_out', 'delta_l1_norm', 'delta_l1_w_in', 'delta_l1_q_a_norm', 'delta_l1_w_uq', 'delta_l1_kv_a_norm', 'delta_l1_w_ukv', 'delta_l1_q_head_norm', 'delta_l1_k_head_norm', 'delta_l1_w_out', 'delta_l2_norm', 'delta_l2_w_in', 'delta_l2_q_head_norm', 'delta_l2_k_head_norm', 'delta_l2_sinks', 'delta_l2_w_out', 'delta_l3_norm', 'delta_l3_w_in', 'delta_l3_w_out', 'new_m_l0_norm', 'new_m_l0_w_in', 'new_m_l0_w_out', 'new_m_l1_norm', 'new_m_l1_w_in', 'new_m_l1_q_a_norm', 'new_m_l1_w_uq', 'new_m_l1_kv_a_norm', 'new_m_l1_w_ukv', 'new_m_l1_q_head_norm', 'new_m_l1_k_head_norm', 'new_m_l1_w_out', 'new_m_l2_norm', 'new_m_l2_w_in', 'new_m_l2_q_head_norm', 'new_m_l2_k_head_norm', 'new_m_l2_sinks', 'new_m_l2_w_out', 'new_m_l3_norm', 'new_m_l3_w_in', 'new_m_l3_w_out', 'new_v_l0_norm', 'new_v_l0_w_in', 'new_v_l0_w_out', 'new_v_l1_norm', 'new_v_l1_w_in', 'new_v_l1_q_a_norm', 'new_v_l1_w_uq', 'new_v_l1_kv_a_norm', 'new_v_l1_w_ukv', 'new_v_l1_q_head_norm', 'new_v_l1_k_head_norm', 'new_v_l1_w_out', 'new_v_l2_norm', 'new_v_l2_w_in', 'new_v_l2_q_head_norm', 'new_v_l2_k_head_norm', 'new_v_l2_sinks', 'new_v_l2_w_out', 'new_v_l3_norm', 'new_v_l3_w_in', 'new_v_l3_w_out']
TWIN_LEAF_KINDS = {'loss': 'loss', 'grad_x': 'grad_x', 'grad_l0_norm': 'grad_w', 'grad_l0_w_in': 'grad_w', 'grad_l0_w_out': 'grad_w', 'grad_l1_norm': 'grad_w', 'grad_l1_w_in': 'grad_w', 'grad_l1_q_a_norm': 'grad_w', 'grad_l1_w_uq': 'grad_w', 'grad_l1_kv_a_norm': 'grad_w', 'grad_l1_w_ukv': 'grad_w', 'grad_l1_q_head_norm': 'grad_w', 'grad_l1_k_head_norm': 'grad_w', 'grad_l1_w_out': 'grad_w', 'grad_l2_norm': 'grad_w', 'grad_l2_w_in': 'grad_w', 'grad_l2_q_head_norm': 'grad_w', 'grad_l2_k_head_norm': 'grad_w', 'grad_l2_sinks': 'grad_w', 'grad_l2_w_out': 'grad_w', 'grad_l3_norm': 'grad_w', 'grad_l3_w_in': 'grad_w', 'grad_l3_w_out': 'grad_w', 'delta_l0_norm': 'delta_w', 'delta_l0_w_in': 'delta_w', 'delta_l0_w_out': 'delta_w', 'delta_l1_norm': 'delta_w', 'delta_l1_w_in': 'delta_w', 'delta_l1_q_a_norm': 'delta_w', 'delta_l1_w_uq': 'delta_w', 'delta_l1_kv_a_norm': 'delta_w', 'delta_l1_w_ukv': 'delta_w', 'delta_l1_q_head_norm': 'delta_w', 'delta_l1_k_head_norm': 'delta_w', 'delta_l1_w_out': 'delta_w', 'delta_l2_norm': 'delta_w', 'delta_l2_w_in': 'delta_w', 'delta_l2_q_head_norm': 'delta_w', 'delta_l2_k_head_norm': 'delta_w', 'delta_l2_sinks': 'delta_w', 'delta_l2_w_out': 'delta_w', 'delta_l3_norm': 'delta_w', 'delta_l3_w_in': 'delta_w', 'delta_l3_w_out': 'delta_w', 'new_m_l0_norm': 'new_m', 'new_m_l0_w_in': 'new_m', 'new_m_l0_w_out': 'new_m', 'new_m_l1_norm': 'new_m', 'new_m_l1_w_in': 'new_m', 'new_m_l1_q_a_norm': 'new_m', 'new_m_l1_w_uq': 'new_m', 'new_m_l1_kv_a_norm': 'new_m', 'new_m_l1_w_ukv': 'new_m', 'new_m_l1_q_head_norm': 'new_m', 'new_m_l1_k_head_norm': 'new_m', 'new_m_l1_w_out': 'new_m', 'new_m_l2_norm': 'new_m', 'new_m_l2_w_in': 'new_m', 'new_m_l2_q_head_norm': 'new_m', 'new_m_l2_k_head_norm': 'new_m', 'new_m_l2_sinks': 'new_m', 'new_m_l2_w_out': 'new_m', 'new_m_l3_norm': 'new_m', 'new_m_l3_w_in': 'new_m', 'new_m_l3_w_out': 'new_m', 'new_v_l0_norm': 'new_v', 'new_v_l0_w_in': 'new_v', 'new_v_l0_w_out': 'new_v', 'new_v_l1_norm': 'new_v', 'new_v_l1_w_in': 'new_v', 'new_v_l1_q_a_norm': 'new_v', 'new_v_l1_w_uq': 'new_v', 'new_v_l1_kv_a_norm': 'new_v', 'new_v_l1_w_ukv': 'new_v', 'new_v_l1_q_head_norm': 'new_v', 'new_v_l1_k_head_norm': 'new_v', 'new_v_l1_w_out': 'new_v', 'new_v_l2_norm': 'new_v', 'new_v_l2_w_in': 'new_v', 'new_v_l2_q_head_norm': 'new_v', 'new_v_l2_k_head_norm': 'new_v', 'new_v_l2_sinks': 'new_v', 'new_v_l2_w_out': 'new_v', 'new_v_l3_norm': 'new_v', 'new_v_l3_w_in': 'new_v', 'new_v_l3_w_out': 'new_v'}


def _forward(args):
    return _fwd_reference(*[args[k] for k in FWD_PARAMS])


def _output_shape():
    out = _jax.eval_shape(lambda: _forward(_fwd_setup_inputs(0)))
    return out.shape, out.dtype

N_MICROBATCH = 1
ADAM_LR = 0.001
ADAM_B1 = 0.9
ADAM_B2 = 0.999
ADAM_EPS = 1e-08
ADAM_WD = 0.01
ADAM_STEP = 10
PER_EXAMPLE_BATCH_AXIS = {'x': 0, 'loss_target': 0}
SHARED_INPUTS = []
_WEIGHT_DTYPES = {'l0_norm': _jnp.float32, 'l0_w_in': _jnp.float32, 'l0_w_out': _jnp.float32, 'l1_norm': _jnp.float32, 'l1_w_in': _jnp.float32, 'l1_q_a_norm': _jnp.float32, 'l1_w_uq': _jnp.float32, 'l1_kv_a_norm': _jnp.float32, 'l1_w_ukv': _jnp.float32, 'l1_q_head_norm': _jnp.float32, 'l1_k_head_norm': _jnp.float32, 'l1_w_out': _jnp.float32, 'l2_norm': _jnp.float32, 'l2_w_in': _jnp.float32, 'l2_q_head_norm': _jnp.float32, 'l2_k_head_norm': _jnp.float32, 'l2_sinks': _jnp.float32, 'l2_w_out': _jnp.float32, 'l3_norm': _jnp.float32, 'l3_w_in': _jnp.float32, 'l3_w_out': _jnp.float32}
MOMENT_SCALE = {'l0_norm': 1.166807e+01, 'l0_w_in': 1.404567e-01, 'l0_w_out': 1.568785e-01, 'l1_norm': 1.573472e-01, 'l1_w_in': 8.455730e-02, 'l1_q_a_norm': 9.145296e-02, 'l1_w_uq': 3.618717e-02, 'l1_kv_a_norm': 8.845263e-01, 'l1_w_ukv': 4.775379e-02, 'l1_q_head_norm': 3.371516e-01, 'l1_k_head_norm': 3.374606e-01, 'l1_w_out': 5.100406e-02, 'l2_norm': 1.843875e+00, 'l2_w_in': 8.098590e-02, 'l2_q_head_norm': 7.094494e+00, 'l2_k_head_norm': 7.087071e+00, 'l2_sinks': 1.408942e+01, 'l2_w_out': 6.348087e-02, 'l3_norm': 1.170260e+01, 'l3_w_in': 1.256538e-01, 'l3_w_out': 1.329425e-01}


def _to_microbatches(a, axis):
    t = _jnp.moveaxis(a, axis, 0)
    t = t.reshape((N_MICROBATCH, t.shape[0] // N_MICROBATCH) + t.shape[1:])
    return _jnp.moveaxis(t, 1, axis + 1)


def setup_inputs(seed: int = 0) -> dict:
    inp = _fwd_setup_inputs(seed)
    key = _jax.random.fold_in(_jax.random.key(seed), 7919)
    shape, _ = _output_shape()
    out = dict(inp)
    out["loss_target"] = _jax.random.normal(_jax.random.fold_in(key, 0), shape, _jnp.float32)
    for i, name in enumerate(TWIN_WEIGHTS):
        w = inp[name].astype(_jnp.float32)
        if MOMENT_SCALE is None:
            s = _jnp.sqrt(_jnp.mean(_jnp.square(w)) + 1e-30)
        else:
            s = MOMENT_SCALE[name]
        km, kv = _jax.random.split(_jax.random.fold_in(key, i + 1))
        out[name] = w
        out["m_" + name] = s * _jax.random.normal(km, w.shape, _jnp.float32)
        out["v_" + name] = (s * s) * _jax.random.uniform(kv, w.shape, _jnp.float32, 0.5, 1.5)
    if N_MICROBATCH > 1:
        for name, axis in PER_EXAMPLE_BATCH_AXIS.items():
            out[name] = _to_microbatches(out[name], axis)
    return {'x': out['x'], 'l0_norm': out['l0_norm'], 'l0_w_in': out['l0_w_in'], 'l0_w_out': out['l0_w_out'], 'l1_norm': out['l1_norm'], 'l1_w_in': out['l1_w_in'], 'l1_q_a_norm': out['l1_q_a_norm'], 'l1_w_uq': out['l1_w_uq'], 'l1_kv_a_norm': out['l1_kv_a_norm'], 'l1_w_ukv': out['l1_w_ukv'], 'l1_q_head_norm': out['l1_q_head_norm'], 'l1_k_head_norm': out['l1_k_head_norm'], 'l1_w_out': out['l1_w_out'], 'l2_norm': out['l2_norm'], 'l2_w_in': out['l2_w_in'], 'l2_q_head_norm': out['l2_q_head_norm'], 'l2_k_head_norm': out['l2_k_head_norm'], 'l2_sinks': out['l2_sinks'], 'l2_w_out': out['l2_w_out'], 'l3_norm': out['l3_norm'], 'l3_w_in': out['l3_w_in'], 'l3_w_out': out['l3_w_out'], 'loss_target': out['loss_target'], 'm_l0_norm': out['m_l0_norm'], 'm_l0_w_in': out['m_l0_w_in'], 'm_l0_w_out': out['m_l0_w_out'], 'm_l1_norm': out['m_l1_norm'], 'm_l1_w_in': out['m_l1_w_in'], 'm_l1_q_a_norm': out['m_l1_q_a_norm'], 'm_l1_w_uq': out['m_l1_w_uq'], 'm_l1_kv_a_norm': out['m_l1_kv_a_norm'], 'm_l1_w_ukv': out['m_l1_w_ukv'], 'm_l1_q_head_norm': out['m_l1_q_head_norm'], 'm_l1_k_head_norm': out['m_l1_k_head_norm'], 'm_l1_w_out': out['m_l1_w_out'], 'm_l2_norm': out['m_l2_norm'], 'm_l2_w_in': out['m_l2_w_in'], 'm_l2_q_head_norm': out['m_l2_q_head_norm'], 'm_l2_k_head_norm': out['m_l2_k_head_norm'], 'm_l2_sinks': out['m_l2_sinks'], 'm_l2_w_out': out['m_l2_w_out'], 'm_l3_norm': out['m_l3_norm'], 'm_l3_w_in': out['m_l3_w_in'], 'm_l3_w_out': out['m_l3_w_out'], 'v_l0_norm': out['v_l0_norm'], 'v_l0_w_in': out['v_l0_w_in'], 'v_l0_w_out': out['v_l0_w_out'], 'v_l1_norm': out['v_l1_norm'], 'v_l1_w_in': out['v_l1_w_in'], 'v_l1_q_a_norm': out['v_l1_q_a_norm'], 'v_l1_w_uq': out['v_l1_w_uq'], 'v_l1_kv_a_norm': out['v_l1_kv_a_norm'], 'v_l1_w_ukv': out['v_l1_w_ukv'], 'v_l1_q_head_norm': out['v_l1_q_head_norm'], 'v_l1_k_head_norm': out['v_l1_k_head_norm'], 'v_l1_w_out': out['v_l1_w_out'], 'v_l2_norm': out['v_l2_norm'], 'v_l2_w_in': out['v_l2_w_in'], 'v_l2_q_head_norm': out['v_l2_q_head_norm'], 'v_l2_k_head_norm': out['v_l2_k_head_norm'], 'v_l2_sinks': out['v_l2_sinks'], 'v_l2_w_out': out['v_l2_w_out'], 'v_l3_norm': out['v_l3_norm'], 'v_l3_w_in': out['v_l3_w_in'], 'v_l3_w_out': out['v_l3_w_out']}


def _loss(weights, diff, rest, loss_target):
    with _jax.named_scope("forward"):
        args = {**rest, TWIN_DIFF_INPUT: diff, **{k: w.astype(_WEIGHT_DTYPES[k]) for k, w in weights.items()}}
        y = _forward(args)
    with _jax.named_scope("loss_head"):
        err = _jnp.square(y.astype(_jnp.float32) - loss_target)
        return 0.5 * _jnp.sum(_jnp.mean(err, axis=-1)) if err.ndim else 0.5 * err


def _adamw(w, g, m, v):
    m = ADAM_B1 * m + (1.0 - ADAM_B1) * g
    v = ADAM_B2 * v + (1.0 - ADAM_B2) * _jnp.square(g)
    m_hat = m / (1.0 - ADAM_B1 ** ADAM_STEP)
    v_hat = v / (1.0 - ADAM_B2 ** ADAM_STEP)
    delta = -ADAM_LR * (m_hat / (_jnp.sqrt(v_hat) + ADAM_EPS) + ADAM_WD * w)
    return delta, m, v


def reference(x, l0_norm, l0_w_in, l0_w_out, l1_norm, l1_w_in, l1_q_a_norm, l1_w_uq, l1_kv_a_norm, l1_w_ukv, l1_q_head_norm, l1_k_head_norm, l1_w_out, l2_norm, l2_w_in, l2_q_head_norm, l2_k_head_norm, l2_sinks, l2_w_out, l3_norm, l3_w_in, l3_w_out, loss_target, m_l0_norm, m_l0_w_in, m_l0_w_out, m_l1_norm, m_l1_w_in, m_l1_q_a_norm, m_l1_w_uq, m_l1_kv_a_norm, m_l1_w_ukv, m_l1_q_head_norm, m_l1_k_head_norm, m_l1_w_out, m_l2_norm, m_l2_w_in, m_l2_q_head_norm, m_l2_k_head_norm, m_l2_sinks, m_l2_w_out, m_l3_norm, m_l3_w_in, m_l3_w_out, v_l0_norm, v_l0_w_in, v_l0_w_out, v_l1_norm, v_l1_w_in, v_l1_q_a_norm, v_l1_w_uq, v_l1_kv_a_norm, v_l1_w_ukv, v_l1_q_head_norm, v_l1_k_head_norm, v_l1_w_out, v_l2_norm, v_l2_w_in, v_l2_q_head_norm, v_l2_k_head_norm, v_l2_sinks, v_l2_w_out, v_l3_norm, v_l3_w_in, v_l3_w_out):
    given = dict(x=x, l0_norm=l0_norm, l0_w_in=l0_w_in, l0_w_out=l0_w_out, l1_norm=l1_norm, l1_w_in=l1_w_in, l1_q_a_norm=l1_q_a_norm, l1_w_uq=l1_w_uq, l1_kv_a_norm=l1_kv_a_norm, l1_w_ukv=l1_w_ukv, l1_q_head_norm=l1_q_head_norm, l1_k_head_norm=l1_k_head_norm, l1_w_out=l1_w_out, l2_norm=l2_norm, l2_w_in=l2_w_in, l2_q_head_norm=l2_q_head_norm, l2_k_head_norm=l2_k_head_norm, l2_sinks=l2_sinks, l2_w_out=l2_w_out, l3_norm=l3_norm, l3_w_in=l3_w_in, l3_w_out=l3_w_out, loss_target=loss_target, m_l0_norm=m_l0_norm, m_l0_w_in=m_l0_w_in, m_l0_w_out=m_l0_w_out, m_l1_norm=m_l1_norm, m_l1_w_in=m_l1_w_in, m_l1_q_a_norm=m_l1_q_a_norm, m_l1_w_uq=m_l1_w_uq, m_l1_kv_a_norm=m_l1_kv_a_norm, m_l1_w_ukv=m_l1_w_ukv, m_l1_q_head_norm=m_l1_q_head_norm, m_l1_k_head_norm=m_l1_k_head_norm, m_l1_w_out=m_l1_w_out, m_l2_norm=m_l2_norm, m_l2_w_in=m_l2_w_in, m_l2_q_head_norm=m_l2_q_head_norm, m_l2_k_head_norm=m_l2_k_head_norm, m_l2_sinks=m_l2_sinks, m_l2_w_out=m_l2_w_out, m_l3_norm=m_l3_norm, m_l3_w_in=m_l3_w_in, m_l3_w_out=m_l3_w_out, v_l0_norm=v_l0_norm, v_l0_w_in=v_l0_w_in, v_l0_w_out=v_l0_w_out, v_l1_norm=v_l1_norm, v_l1_w_in=v_l1_w_in, v_l1_q_a_norm=v_l1_q_a_norm, v_l1_w_uq=v_l1_w_uq, v_l1_kv_a_norm=v_l1_kv_a_norm, v_l1_w_ukv=v_l1_w_ukv, v_l1_q_head_norm=v_l1_q_head_norm, v_l1_k_head_norm=v_l1_k_head_norm, v_l1_w_out=v_l1_w_out, v_l2_norm=v_l2_norm, v_l2_w_in=v_l2_w_in, v_l2_q_head_norm=v_l2_q_head_norm, v_l2_k_head_norm=v_l2_k_head_norm, v_l2_sinks=v_l2_sinks, v_l2_w_out=v_l2_w_out, v_l3_norm=v_l3_norm, v_l3_w_in=v_l3_w_in, v_l3_w_out=v_l3_w_out)
    weights = {n: given[n] for n in TWIN_WEIGHTS}
    shared = {n: given[n] for n in SHARED_INPUTS}
    per_example = {n: given[n] for n in ['x']}
    grad_fn = _jax.value_and_grad(_loss, argnums=(0, 1))

    def one_microbatch(ex, loss_target):
        ex = dict(ex)
        diff = ex.pop(TWIN_DIFF_INPUT)
        return grad_fn(weights, diff, {**shared, **ex}, loss_target)

    if N_MICROBATCH == 1:
        loss, (grad_w, grad_x) = one_microbatch(per_example, given["loss_target"])
    else:
        def body(carry, xs):
            loss_sum, grad_sum = carry
            l_k, (gw_k, gx_k) = one_microbatch(xs[0], xs[1])
            with _jax.named_scope("update"):
                return (loss_sum + l_k, _jax.tree.map(_jnp.add, grad_sum, gw_k)), gx_k

        init = (_jnp.zeros((), _jnp.float32), _jax.tree.map(_jnp.zeros_like, weights))
        (loss, grad_w), grad_x = _jax.lax.scan(body, init, (per_example, given["loss_target"]))
    with _jax.named_scope("update"):
        delta_w, new_m, new_v = {}, {}, {}
        for n in TWIN_WEIGHTS:
            delta_w[n], new_m[n], new_v[n] = _adamw(weights[n], grad_w[n], given["m_" + n], given["v_" + n])
    return (loss, grad_x, *[grad_w[n] for n in TWIN_WEIGHTS], *[delta_w[n] for n in TWIN_WEIGHTS],
            *[new_m[n] for n in TWIN_WEIGHTS], *[new_v[n] for n in TWIN_WEIGHTS])
```

```python
import math

import jax
import jax.numpy as jnp
from jax import lax
from jax.experimental import pallas as pl
from jax.experimental.pallas import tpu as pltpu

F32 = jnp.float32
BF16 = jnp.bfloat16
I32 = jnp.int32
MESH = pl.DeviceIdType.MESH

NORM_EPS = 1e-6
D_MODEL = 1024
HEAD64 = 64
LANES = 128
BLK = 128
MLA_HEADS = 8
MLA_QK = 192
MLA_PAD = 256
ROPE_THETA = 10000.0
SWA_HEADS = 16
SWA_KV = 4
VMEM_CAP = 56 * 1024 * 1024

ADAM_LR, ADAM_B1, ADAM_B2, ADAM_EPS, ADAM_WD, ADAM_STEP = 0.001, 0.9, 0.999, 1e-08, 0.01, 10

NT = (((1,), (1,)), ((), ()))
NN = (((1,), (0,)), ((), ()))
TN = (((0,), (0,)), ((), ()))


def _dot(a, b, dims=NN):
    return lax.dot_general(a, b, dims, preferred_element_type=F32)


def _tile(n, pref):
    for t in (pref, 512, 256, 128):
        if t <= pref and n % t == 0:
            return t
    return n


def _params(sem, vmem_bytes):
    limit = int(min(max(2 * vmem_bytes, 24 * 1024 * 1024), VMEM_CAP))
    return pltpu.CompilerParams(dimension_semantics=sem, vmem_limit_bytes=limit)


def _split(v):
    hi = v.astype(BF16)
    return hi, (v - hi.astype(F32)).astype(BF16)


def _dot2(v, m):
    hi, lo = _split(v)
    return _dot(hi, m) + _dot(lo, m)


def _first_half(shape):
    return lax.broadcasted_iota(I32, shape, 1) < HEAD64


def _sigmoid(g):
    return 1.0 / (1.0 + jnp.exp(-g))


def matmul(a, b, mode, *, name, out_dtype=F32, add=None, tm=512, tn=1024, tk=512):
    if mode == "nn":
        (M, K), (K2, N) = a.shape, b.shape
    elif mode == "nt":
        (M, K), (N, K2) = a.shape, b.shape
    else:
        (K, M), (K2, N) = a.shape, b.shape
    assert K == K2, (a.shape, b.shape, mode)
    tm, tn, tk = _tile(M, tm), _tile(N, tn), _tile(K, tk)
    nk = K // tk
    dims = {"nn": NN, "nt": NT, "tn": TN}[mode]

    def body(*refs):
        if add is None:
            a_ref, b_ref, o_ref, acc_ref = refs
        else:
            a_ref, b_ref, add_ref, o_ref, acc_ref = refs
        k = pl.program_id(2)

        @pl.when(k == 0)
        def _():
            acc_ref[...] = jnp.zeros_like(acc_ref)

        acc_ref[...] += _dot(a_ref[...].astype(BF16), b_ref[...].astype(BF16), dims)

        @pl.when(k == nk - 1)
        def _():
            r = acc_ref[...]
            if add is not None:
                r = r + add_ref[...]
            o_ref[...] = r.astype(out_dtype)

    a_spec = (pl.BlockSpec((tk, tm), lambda i, j, k: (k, i)) if mode == "tn"
              else pl.BlockSpec((tm, tk), lambda i, j, k: (i, k)))
    b_spec = (pl.BlockSpec((tn, tk), lambda i, j, k: (j, k)) if mode == "nt"
              else pl.BlockSpec((tk, tn), lambda i, j, k: (k, j)))
    o_spec = pl.BlockSpec((tm, tn), lambda i, j, k: (i, j))
    in_specs, args = [a_spec, b_spec], [a, b]
    if add is not None:
        in_specs.append(o_spec)
        args.append(add)
    vm = 2 * (tm * tk * a.dtype.itemsize + tk * tn * b.dtype.itemsize) + 4 * tm * tn * 4
    return pl.pallas_call(
        body, name=name, grid=(M // tm, N // tn, nk),
        in_specs=in_specs, out_specs=o_spec,
        out_shape=jax.ShapeDtypeStruct((M, N), out_dtype),
        scratch_shapes=[pltpu.VMEM((tm, tn), F32)],
        compiler_params=_params(("parallel", "parallel", "arbitrary"), vm),
    )(*args)


def rmsnorm_fwd(x, g, *, col_off, width, out_dtype, name, tm=256):
    S = x.shape[0]
    assert col_off % width == 0
    cb = col_off // width
    tm = _tile(S, tm)

    def body(x_ref, g_ref, o_ref):
        v = x_ref[...]
        r = lax.rsqrt(jnp.mean(v * v, axis=1, keepdims=True) + NORM_EPS)
        o_ref[...] = (v * r * g_ref[...]).astype(out_dtype)

    return pl.pallas_call(
        body, name=name, grid=(S // tm,),
        in_specs=[pl.BlockSpec((tm, width), lambda i: (i, cb)), pl.BlockSpec((1, width), lambda i: (0, 0))],
        out_specs=pl.BlockSpec((tm, width), lambda i: (i, 0)),
        out_shape=jax.ShapeDtypeStruct((S, width), out_dtype),
        compiler_params=_params(("parallel",), 4 * tm * width * 4),
    )(x, g.reshape(1, width))


def rmsnorm_bwd(x, g, dh, *, col_off, width, out_dtype, name, res=None, tm=256):
    S = x.shape[0]
    cb = col_off // width
    tm = _tile(S, tm)

    def body(*refs):
        if res is None:
            x_ref, g_ref, dh_ref, dx_ref, dg_ref = refs
        else:
            x_ref, g_ref, dh_ref, res_ref, dx_ref, dg_ref = refs

        @pl.when(pl.program_id(0) == 0)
        def _():
            dg_ref[...] = jnp.zeros_like(dg_ref)

        v = x_ref[...]
        dhv = dh_ref[...].astype(F32)
        r = lax.rsqrt(jnp.mean(v * v, axis=1, keepdims=True) + NORM_EPS)
        y = v * r
        dy = dhv * g_ref[...]
        dx = r * (dy - y * jnp.mean(dy * y, axis=1, keepdims=True))
        if res is not None:
            dx = dx + res_ref[...]
        dx_ref[...] = dx.astype(out_dtype)
        dg_ref[...] += jnp.sum(dhv * y, axis=0, keepdims=True)

    row = pl.BlockSpec((tm, width), lambda i: (i, 0))
    in_specs = [pl.BlockSpec((tm, width), lambda i: (i, cb)), pl.BlockSpec((1, width), lambda i: (0, 0)), row]
    args = [x, g.reshape(1, width), dh]
    if res is not None:
        in_specs.append(row)
        args.append(res)
    return pl.pallas_call(
        body, name=name, grid=(S // tm,),
        in_specs=in_specs,
        out_specs=[row, pl.BlockSpec((1, width), lambda i: (0, 0))],
        out_shape=[jax.ShapeDtypeStruct((S, width), out_dtype), jax.ShapeDtypeStruct((1, width), F32)],
        compiler_params=_params(("arbitrary",), 8 * tm * width * 4),
    )(*args)


def _group_ones():
    r = lax.broadcasted_iota(I32, (LANES, LANES), 0) // HEAD64
    c = lax.broadcasted_iota(I32, (LANES, LANES), 1) // HEAD64
    return (r == c).astype(BF16)


def _segmean64(v):
    return _dot2(v, _group_ones()) * (1.0 / HEAD64)


def headnorm_fwd(x, g64, *, col_off, width, name, tm=512):
    S = x.shape[0]
    cb = col_off // LANES
    tm = _tile(S, tm)

    def body(x_ref, g_ref, o_ref):
        v = x_ref[...]
        r = lax.rsqrt(_segmean64(v * v) + NORM_EPS)
        o_ref[...] = (v * r * g_ref[...]).astype(BF16)

    return pl.pallas_call(
        body, name=name, grid=(S // tm, width // LANES),
        in_specs=[pl.BlockSpec((tm, LANES), lambda i, j: (i, cb + j)), pl.BlockSpec((1, LANES), lambda i, j: (0, 0))],
        out_specs=pl.BlockSpec((tm, LANES), lambda i, j: (i, j)),
        out_shape=jax.ShapeDtypeStruct((S, width), BF16),
        compiler_params=_params(("parallel", "parallel"), 8 * tm * LANES * 4),
    )(x, jnp.tile(g64, 2).reshape(1, LANES))


def headnorm_bwd(x, g64, dh, *, col_off, width, name, tm=512):
    S = x.shape[0]
    cb = col_off // LANES
    tm = _tile(S, tm)

    def body(x_ref, g_ref, dh_ref, dx_ref, dg_ref):
        @pl.when((pl.program_id(0) == 0) & (pl.program_id(1) == 0))
        def _():
            dg_ref[...] = jnp.zeros_like(dg_ref)

        v = x_ref[...]
        dhv = dh_ref[...]
        r = lax.rsqrt(_segmean64(v * v) + NORM_EPS)
        y = v * r
        dy = dhv * g_ref[...]
        dx_ref[...] = (r * (dy - y * _segmean64(dy * y))).astype(BF16)
        dg_ref[...] += jnp.sum(dhv * y, axis=0, keepdims=True)

    return pl.pallas_call(
        body, name=name, grid=(width // LANES, S // tm),
        in_specs=[pl.BlockSpec((tm, LANES), lambda j, i: (i, cb + j)), pl.BlockSpec((1, LANES), lambda j, i: (0, 0)),
                  pl.BlockSpec((tm, LANES), lambda j, i: (i, j))],
        out_specs=[pl.BlockSpec((tm, LANES), lambda j, i: (i, j)), pl.BlockSpec((1, LANES), lambda j, i: (0, 0))],
        out_shape=[jax.ShapeDtypeStruct((S, width), BF16), jax.ShapeDtypeStruct((1, LANES), F32)],
        compiler_params=_params(("arbitrary", "arbitrary"), 10 * tm * LANES * 4),
    )(x, jnp.tile(g64, 2).reshape(1, LANES), dh)


def gate_fwd(o, proj, *, gate_off, name, tm=256):
    S, W = o.shape
    cb = gate_off // W
    tm = _tile(S, tm)

    def body(o_ref, g_ref, out_ref):
        g = g_ref[...]
        out_ref[...] = (o_ref[...] * (g * _sigmoid(g))).astype(BF16)

    return pl.pallas_call(
        body, name=name, grid=(S // tm,),
        in_specs=[pl.BlockSpec((tm, W), lambda i: (i, 0)), pl.BlockSpec((tm, W), lambda i: (i, cb))],
        out_specs=pl.BlockSpec((tm, W), lambda i: (i, 0)),
        out_shape=jax.ShapeDtypeStruct((S, W), BF16),
        compiler_params=_params(("parallel",), 6 * tm * W * 4),
    )(o, proj)


def gate_bwd(dog, o, proj, *, gate_off, name, tm=256):
    S, W = o.shape
    cb = gate_off // W
    tm = _tile(S, tm)

    def body(d_ref, o_ref, g_ref, do_ref, dg_ref):
        g = g_ref[...]
        d = d_ref[...]
        s = _sigmoid(g)
        do_ref[...] = d * (g * s)
        dg_ref[...] = (d * o_ref[...] * (s * (1.0 + g * (1.0 - s)))).astype(BF16)

    row = pl.BlockSpec((tm, W), lambda i: (i, 0))
    return pl.pallas_call(
        body, name=name, grid=(S // tm,),
        in_specs=[row, row, pl.BlockSpec((tm, W), lambda i: (i, cb))],
        out_specs=[row, row],
        out_shape=[jax.ShapeDtypeStruct((S, W), F32), jax.ShapeDtypeStruct((S, W), BF16)],
        compiler_params=_params(("parallel",), 10 * tm * W * 4),
    )(dog, o, proj)


def loss_head(y, target, *, name, tm=256):
    S, W = y.shape
    tm = _tile(S, tm)
    n = S // tm

    def body(y_ref, t_ref, dy_ref, l_ref, acc_ref):
        i = pl.program_id(0)

        @pl.when(i == 0)
        def _():
            acc_ref[...] = jnp.zeros_like(acc_ref)

        e = y_ref[...] - t_ref[...]
        dy_ref[...] = e * (1.0 / W)
        acc_ref[...] += jnp.sum(e * e, axis=0, keepdims=True)

        @pl.when(i == n - 1)
        def _():
            l_ref[...] = jnp.full(l_ref.shape, (0.5 / W) * jnp.sum(acc_ref[...]), F32)

    row = pl.BlockSpec((tm, W), lambda i: (i, 0))
    return pl.pallas_call(
        body, name=name, grid=(n,),
        in_specs=[row, row],
        out_specs=[row, pl.BlockSpec((8, LANES), lambda i: (0, 0))],
        out_shape=[jax.ShapeDtypeStruct((S, W), F32), jax.ShapeDtypeStruct((8, LANES), F32)],
        scratch_shapes=[pltpu.VMEM((1, W), F32)],
        compiler_params=_params(("arbitrary",), 8 * tm * W * 4),
    )(y, target)


def _sb_scores(qh, k2, mask, upper, rf):
    z = _dot(qh, k2, NT) * (1.0 / math.sqrt(HEAD64))
    l = jnp.log(1.0 + jnp.exp(-jnp.abs(z)))
    sp = jnp.maximum(z, 0.0) + l
    f = jnp.where(mask, -sp, 0.0)
    g = z - sp
    later = _dot2(f, upper) + rf
    a = jnp.where(mask, jnp.exp(g + later), 0.0)
    return a, f, g


def sb_attn_fwd(proj, *, name):
    S = proj.shape[0]
    W = SB_W = 1024
    npair, nq = W // LANES, S // BLK

    def body(q_ref, k_ref, v_ref, o_ref):
        i = pl.program_id(1)
        first = _first_half((BLK, LANES))
        q2 = q_ref[...]
        qa = jnp.where(first, q2, 0.0).astype(BF16)
        qb = jnp.where(first, 0.0, q2).astype(BF16)
        row = lax.broadcasted_iota(I32, (BLK, BLK), 0)
        col = lax.broadcasted_iota(I32, (BLK, BLK), 1)
        upper = (row > col).astype(BF16)

        def step(jj, carry):
            acc, rfa, rfb = carry
            j = i - jj
            rows = pl.ds(pl.multiple_of(j * BLK, BLK), BLK)
            k2 = k_ref[rows, :].astype(BF16)
            v2 = v_ref[rows, :].astype(BF16)
            mask = (col + j * BLK) < (row + i * BLK)
            aa, fa, _ = _sb_scores(qa, k2, mask, upper, rfa)
            ab, fb, _ = _sb_scores(qb, k2, mask, upper, rfb)
            acc = acc + jnp.where(first, _dot(aa.astype(BF16), v2), _dot(ab.astype(BF16), v2))
            return acc, rfa + jnp.sum(fa, axis=1, keepdims=True), rfb + jnp.sum(fb, axis=1, keepdims=True)

        zero = jnp.zeros((BLK, 1), F32)
        acc, _, _ = lax.fori_loop(0, i + 1, step, (jnp.zeros((BLK, LANES), F32), zero, zero))
        o_ref[...] = acc

    return pl.pallas_call(
        body, name=name, grid=(npair, nq),
        in_specs=[pl.BlockSpec((BLK, LANES), lambda p, i: (i, p)),
                  pl.BlockSpec((S, LANES), lambda p, i: (0, npair + p)),
                  pl.BlockSpec((S, LANES), lambda p, i: (0, 2 * npair + p))],
        out_specs=pl.BlockSpec((BLK, LANES), lambda p, i: (i, p)),
        out_shape=jax.ShapeDtypeStruct((S, W), F32),
        compiler_params=_params(("parallel", "arbitrary"), 4 * S * LANES * 4),
    )(proj, proj, proj)


def sb_attn_bwd(proj, o, do, *, name):
    S = proj.shape[0]
    W = 1024
    npair, nq = W // LANES, S // BLK
    scale = 1.0 / math.sqrt(HEAD64)

    def body(q_ref, k_ref, v_ref, o_ref, do_ref, dq_ref, dk_ref, dv_ref):
        i = pl.program_id(1)

        @pl.when(i == 0)
        def _():
            dk_ref[...] = jnp.zeros_like(dk_ref)
            dv_ref[...] = jnp.zeros_like(dv_ref)

        first = _first_half((BLK, LANES))
        q2 = q_ref[...]
        do2 = do_ref[...]
        do2b = do2.astype(BF16)
        prod = do2b.astype(F32) * o_ref[...]
        tot_a = jnp.sum(jnp.where(first, prod, 0.0), axis=1, keepdims=True)
        tot_b = jnp.sum(jnp.where(first, 0.0, prod), axis=1, keepdims=True)
        qa = jnp.where(first, q2, 0.0).astype(BF16)
        qb = jnp.where(first, 0.0, q2).astype(BF16)
        doa = jnp.where(first, do2, 0.0).astype(BF16)
        dob = jnp.where(first, 0.0, do2).astype(BF16)
        q2b = q2.astype(BF16)
        row = lax.broadcasted_iota(I32, (BLK, BLK), 0)
        col = lax.broadcasted_iota(I32, (BLK, BLK), 1)
        upper = (row > col).astype(BF16)
        upper_incl = (row >= col).astype(BF16)

        def head(qh, doh, tot, rf, re, k2, v2, mask):
            a, f, g = _sb_scores(qh, k2, mask, upper, rf)
            ab = a.astype(BF16)
            e = ab.astype(F32) * _dot(doh, v2, NT)
            left = tot - (_dot2(e, upper_incl) + re)
            dz = jnp.where(mask, e - jnp.exp(g) * (e + left), 0.0) * scale
            dzb = dz.astype(BF16)
            return (_dot(dzb, k2), _dot(dzb, q2b, TN), _dot(ab, do2b, TN),
                    rf + jnp.sum(f, axis=1, keepdims=True), re + jnp.sum(e, axis=1, keepdims=True))

        def step(jj, carry):
            dq, rfa, rea, rfb, reb = carry
            j = i - jj
            rows = pl.ds(pl.multiple_of(j * BLK, BLK), BLK)
            k2 = k_ref[rows, :].astype(BF16)
            v2 = v_ref[rows, :].astype(BF16)
            mask = (col + j * BLK) < (row + i * BLK)
            dqa, dka, dva, rfa, rea = head(qa, doa, tot_a, rfa, rea, k2, v2, mask)
            dqb, dkb, dvb, rfb, reb = head(qb, dob, tot_b, rfb, reb, k2, v2, mask)
            dk_ref[rows, :] += jnp.where(first, dka, dkb)
            dv_ref[rows, :] += jnp.where(first, dva, dvb)
            return dq + jnp.where(first, dqa, dqb), rfa, rea, rfb, reb

        zero = jnp.zeros((BLK, 1), F32)
        dq, _, _, _, _ = lax.fori_loop(0, i + 1, step, (jnp.zeros((BLK, LANES), F32), zero, zero, zero, zero))
        dq_ref[...] = dq

    tile = pl.BlockSpec((BLK, LANES), lambda p, i: (i, p))
    full = pl.BlockSpec((S, LANES), lambda p, i: (0, p))
    shape = jax.ShapeDtypeStruct((S, W), F32)
    return pl.pallas_call(
        body, name=name, grid=(npair, nq),
        in_specs=[tile,
                  pl.BlockSpec((S, LANES), lambda p, i: (0, npair + p)),
                  pl.BlockSpec((S, LANES), lambda p, i: (0, 2 * npair + p)),
                  tile, tile],
        out_specs=[tile, full, full],
        out_shape=[shape, shape, shape],
        compiler_params=_params(("parallel", "arbitrary"), 8 * S * LANES * 4),
    )(proj, proj, proj, o, do)


def flash_fwd(qn, kn, kv, *, name):
    S = qn.shape[0]
    H, DK, DV = MLA_HEADS, MLA_PAD, LANES
    nq = S // BLK
    scale = 1.0 / math.sqrt(MLA_QK)

    def body(q_ref, k_ref, v_ref, o_ref, lse_ref):
        i = pl.program_id(1)
        q = q_ref[...]
        row = lax.broadcasted_iota(I32, (BLK, BLK), 0)
        col = lax.broadcasted_iota(I32, (BLK, BLK), 1)

        def step(j, carry):
            acc, m, l = carry
            rows = pl.ds(pl.multiple_of(j * BLK, BLK), BLK)
            s = _dot(q, k_ref[rows, :], NT) * scale
            s = jnp.where((col + j * BLK) <= (row + i * BLK), s, -1e30)
            m_new = jnp.maximum(m, jnp.max(s, axis=1, keepdims=True))
            p = jnp.exp(s - m_new)
            w = jnp.exp(m - m_new)
            acc = acc * w + _dot(p.astype(BF16), v_ref[rows, :].astype(BF16))
            return acc, m_new, l * w + jnp.sum(p, axis=1, keepdims=True)

        acc, m, l = lax.fori_loop(0, i + 1, step, (jnp.zeros((BLK, DV), F32), jnp.full((BLK, 1), -1e30, F32),
                                                   jnp.zeros((BLK, 1), F32)))
        o_ref[...] = acc / l
        lse_ref[0] = m + jnp.log(l)

    return pl.pallas_call(
        body, name=name, grid=(H, nq),
        in_specs=[pl.BlockSpec((BLK, DK), lambda h, i: (i, h)),
                  pl.BlockSpec((S, DK), lambda h, i: (0, h)),
                  pl.BlockSpec((S, DV), lambda h, i: (0, 2 * h + 1))],
        out_specs=[pl.BlockSpec((BLK, DV), lambda h, i: (i, h)), pl.BlockSpec((1, BLK, 1), lambda h, i: (h, i, 0))],
        out_shape=[jax.ShapeDtypeStruct((S, H * DV), F32), jax.ShapeDtypeStruct((H, S, 1), F32)],
        compiler_params=_params(("parallel", "arbitrary"), 2 * S * (DK * 2 + DV * 4)),
    )(qn, kn, kv)


def flash_bwd(qn, kn, kv, o, lse, do, *, name):
    S = qn.shape[0]
    H, DK, DV = MLA_HEADS, MLA_PAD, LANES
    nq = S // BLK
    scale = 1.0 / math.sqrt(MLA_QK)

    def body(q_ref, k_ref, v_ref, o_ref, lse_ref, do_ref, dq_ref, dk_ref, dv_ref):
        i = pl.program_id(1)

        @pl.when(i == 0)
        def _():
            dk_ref[...] = jnp.zeros_like(dk_ref)
            dv_ref[...] = jnp.zeros_like(dv_ref)

        q = q_ref[...]
        do = do_ref[...]
        dob = do.astype(BF16)
        delta = jnp.sum(do * o_ref[...], axis=1, keepdims=True)
        lse = lse_ref[0]
        row = lax.broadcasted_iota(I32, (BLK, BLK), 0)
        col = lax.broadcasted_iota(I32, (BLK, BLK), 1)

        def step(j, dq):
            rows = pl.ds(pl.multiple_of(j * BLK, BLK), BLK)
            k = k_ref[rows, :]
            v = v_ref[rows, :].astype(BF16)
            s = _dot(q, k, NT) * scale
            p = jnp.where((col + j * BLK) <= (row + i * BLK), jnp.exp(s - lse), 0.0)
            ds = (p * (_dot(dob, v, NT) - delta) * scale).astype(BF16)
            dv_ref[rows, :] += _dot(p.astype(BF16), dob, TN)
            dk_ref[rows, :] += _dot(ds, q, TN)
            return dq + _dot(ds, k)

        dq_ref[...] = lax.fori_loop(0, i + 1, step, jnp.zeros((BLK, DK), F32))

    qtile = pl.BlockSpec((BLK, DK), lambda h, i: (i, h))
    otile = pl.BlockSpec((BLK, DV), lambda h, i: (i, h))
    return pl.pallas_call(
        body, name=name, grid=(H, nq),
        in_specs=[qtile, pl.BlockSpec((S, DK), lambda h, i: (0, h)), pl.BlockSpec((S, DV), lambda h, i: (0, 2 * h + 1)),
                  otile, pl.BlockSpec((1, BLK, 1), lambda h, i: (h, i, 0)), otile],
        out_specs=[qtile, pl.BlockSpec((S, DK), lambda h, i: (0, h)), pl.BlockSpec((S, DV), lambda h, i: (0, h))],
        out_shape=[jax.ShapeDtypeStruct((S, H * DK), F32), jax.ShapeDtypeStruct((S, H * DK), F32),
                   jax.ShapeDtypeStruct((S, H * DV), F32)],
        compiler_params=_params(("parallel", "arbitrary"), 2 * S * (DK * 6 + DV * 8)),
    )(qn, kn, kv, o, lse, do)


def _rope_tables(S):
    half = 32
    inv_freq = ROPE_THETA ** (-jnp.arange(half, dtype=F32) / half)
    ang = jnp.arange(S).astype(F32)[:, None] * inv_freq[None, :]
    cos, sin = jnp.cos(ang), jnp.sin(ang)
    ones, zeros = jnp.ones((S, LANES), F32), jnp.zeros((S, LANES), F32)
    pad = jnp.zeros((S, 64), F32)
    return (jnp.concatenate([ones, cos, cos, pad + 1.0], axis=1),
            jnp.concatenate([zeros, -sin, sin, pad], axis=1))


def _rope_partner(u):
    lane = lax.broadcasted_iota(I32, u.shape, 1)
    return jnp.where((lane % HEAD64) < 32, pltpu.roll(u, LANES - 32, 1), pltpu.roll(u, 32, 1))


def _normrope(raw, g, cos, sgn):
    r = lax.rsqrt(jnp.sum(raw * raw, axis=1, keepdims=True) * (1.0 / MLA_QK) + NORM_EPS)
    y = raw * r
    u = y * g
    pe = u[:, LANES:]
    out = jnp.concatenate([u[:, :LANES], pe * cos[:, LANES:] + _rope_partner(pe) * sgn[:, LANES:]], axis=1)
    return out, y, r


def _normrope_bwd(dout, g, cos, sgn, y, r):
    dpe = dout[:, LANES:]
    du = jnp.concatenate([dout[:, :LANES], dpe * cos[:, LANES:] + _rope_partner(dpe * sgn[:, LANES:])], axis=1)
    dy = du * g
    draw = r * (dy - y * (jnp.sum(dy * y, axis=1, keepdims=True) * (1.0 / MLA_QK)))
    return draw, jnp.sum(du * y, axis=0, keepdims=True)


def mla_prep_fwd(qraw, kv, proj, gq, gk, cos, sgn, *, kpe_off, name, tm=256):
    S = qraw.shape[0]
    tm = _tile(S, tm)
    kb = kpe_off // LANES

    def body(q_ref, kn_ref, kpe_ref, gq_ref, gk_ref, c_ref, s_ref, qo_ref, ko_ref):
        cos, sgn = c_ref[...], s_ref[...]
        qo_ref[...] = _normrope(q_ref[...], gq_ref[...], cos, sgn)[0].astype(BF16)
        kraw = jnp.concatenate([kn_ref[...], kpe_ref[...]], axis=1)
        ko_ref[...] = _normrope(kraw, gk_ref[...], cos, sgn)[0].astype(BF16)

    head = pl.BlockSpec((tm, MLA_PAD), lambda i, h: (i, h))
    gain = pl.BlockSpec((1, MLA_PAD), lambda i, h: (0, 0))
    tab = pl.BlockSpec((tm, MLA_PAD), lambda i, h: (i, 0))
    return pl.pallas_call(
        body, name=name, grid=(S // tm, MLA_HEADS),
        in_specs=[head, pl.BlockSpec((tm, LANES), lambda i, h: (i, 2 * h)), pl.BlockSpec((tm, LANES), lambda i, h: (i, kb)),
                  gain, gain, tab, tab],
        out_specs=[head, head],
        out_shape=[jax.ShapeDtypeStruct(qraw.shape, BF16), jax.ShapeDtypeStruct(qraw.shape, BF16)],
        compiler_params=_params(("parallel", "arbitrary"), 16 * tm * MLA_PAD * 4),
    )(qraw, kv, proj, gq, gk, cos, sgn)


def mla_prep_bwd(dqn, dkn, dv, qraw, kv, proj, gq, gk, cos, sgn, *, kpe_off, name, tm=256):
    S = qraw.shape[0]
    tm = _tile(S, tm)
    kb = kpe_off // LANES

    def body(dq_ref, dk_ref, dv_ref, q_ref, kn_ref, kpe_ref, gq_ref, gk_ref, c_ref, s_ref,
             dqo_ref, dkv_ref, dkpe_ref, dgq_ref, dgk_ref, acc_ref):
        i, h = pl.program_id(0), pl.program_id(1)

        @pl.when((i == 0) & (h == 0))
        def _():
            dgq_ref[...] = jnp.zeros_like(dgq_ref)
            dgk_ref[...] = jnp.zeros_like(dgk_ref)

        @pl.when(h == 0)
        def _():
            acc_ref[...] = jnp.zeros_like(acc_ref)

        cos, sgn = c_ref[...], s_ref[...]
        _, yq, rq = _normrope(q_ref[...], gq_ref[...], cos, sgn)
        dq, dgq = _normrope_bwd(dq_ref[...], gq_ref[...], cos, sgn, yq, rq)
        dqo_ref[...] = dq.astype(BF16)
        dgq_ref[...] += dgq
        kraw = jnp.concatenate([kn_ref[...], kpe_ref[...]], axis=1)
        _, yk, rk = _normrope(kraw, gk_ref[...], cos, sgn)
        dk, dgk = _normrope_bwd(dk_ref[...], gk_ref[...], cos, sgn, yk, rk)
        dgk_ref[...] += dgk
        dkv_ref[...] = jnp.concatenate([dk[:, :LANES], dv_ref[...]], axis=1).astype(BF16)
        acc_ref[...] += dk[:, LANES:]

        @pl.when(h == MLA_HEADS - 1)
        def _():
            dkpe_ref[...] = acc_ref[...].astype(BF16)

    head = pl.BlockSpec((tm, MLA_PAD), lambda i, h: (i, h))
    gain = pl.BlockSpec((1, MLA_PAD), lambda i, h: (0, 0))
    tab = pl.BlockSpec((tm, MLA_PAD), lambda i, h: (i, 0))
    return pl.pallas_call(
        body, name=name, grid=(S // tm, MLA_HEADS),
        in_specs=[head, head, pl.BlockSpec((tm, LANES), lambda i, h: (i, h)), head,
                  pl.BlockSpec((tm, LANES), lambda i, h: (i, 2 * h)), pl.BlockSpec((tm, LANES), lambda i, h: (i, kb)),
                  gain, gain, tab, tab],
        out_specs=[head, head, pl.BlockSpec((tm, LANES), lambda i, h: (i, 0)), gain, gain],
        out_shape=[jax.ShapeDtypeStruct(qraw.shape, BF16), jax.ShapeDtypeStruct(qraw.shape, BF16),
                   jax.ShapeDtypeStruct((S, LANES), BF16), jax.ShapeDtypeStruct((1, MLA_PAD), F32),
                   jax.ShapeDtypeStruct((1, MLA_PAD), F32)],
        scratch_shapes=[pltpu.VMEM((tm, LANES), F32)],
        compiler_params=_params(("arbitrary", "arbitrary"), 24 * tm * MLA_PAD * 4),
    )(dqn, dkn, dv, qraw, kv, proj, gq, gk, cos, sgn)


def swa_kv_prep(proj, gk64, *, k_off, v_off, name, tm=512):
    S = proj.shape[0]
    tm = _tile(S, tm)
    W = SWA_KV * HEAD64

    def body(k_ref, v_ref, g_ref, ko_ref, vo_ref):
        first = _first_half((tm, LANES))

        def dup(n):
            nr = pltpu.roll(n, HEAD64, 1)
            return jnp.where(first, n, nr), jnp.where(first, nr, n)

        for t in range(W // LANES):
            x = k_ref[:, t * LANES:(t + 1) * LANES]
            n = x * lax.rsqrt(_segmean64(x * x) + NORM_EPS) * g_ref[...]
            d0, d1 = dup(n)
            ko_ref[:, 2 * t * LANES:(2 * t + 1) * LANES] = d0.astype(BF16)
            ko_ref[:, (2 * t + 1) * LANES:(2 * t + 2) * LANES] = d1.astype(BF16)
            d0, d1 = dup(v_ref[:, t * LANES:(t + 1) * LANES])
            vo_ref[:, 2 * t * LANES:(2 * t + 1) * LANES] = d0.astype(BF16)
            vo_ref[:, (2 * t + 1) * LANES:(2 * t + 2) * LANES] = d1.astype(BF16)

    out = pl.BlockSpec((tm, SWA_KV * LANES), lambda i: (i, 0))
    shape = jax.ShapeDtypeStruct((S, SWA_KV * LANES), BF16)
    return pl.pallas_call(
        body, name=name, grid=(S // tm,),
        in_specs=[pl.BlockSpec((tm, W), lambda i: (i, k_off // W)), pl.BlockSpec((tm, W), lambda i: (i, v_off // W)),
                  pl.BlockSpec((1, LANES), lambda i: (0, 0))],
        out_specs=[out, out], out_shape=[shape, shape],
        compiler_params=_params(("parallel",), 12 * tm * W * 4),
    )(proj, proj, jnp.tile(gk64, 2).reshape(1, LANES))


def swa_kv_prep_bwd(dkdup, dvdup, proj, gk64, *, k_off, name, tm=512):
    S = proj.shape[0]
    tm = _tile(S, tm)
    W = SWA_KV * HEAD64

    def body(dk_ref, dv_ref, k_ref, g_ref, dko_ref, dvo_ref, dg_ref):
        @pl.when(pl.program_id(0) == 0)
        def _():
            dg_ref[...] = jnp.zeros_like(dg_ref)

        first = _first_half((tm, LANES))

        def fold(ref, t):
            d0 = ref[:, 2 * t * LANES:(2 * t + 1) * LANES]
            d1 = ref[:, (2 * t + 1) * LANES:(2 * t + 2) * LANES]
            return jnp.where(first, d0 + pltpu.roll(d0, HEAD64, 1), d1 + pltpu.roll(d1, HEAD64, 1))

        for t in range(W // LANES):
            dvo_ref[:, t * LANES:(t + 1) * LANES] = fold(dv_ref, t).astype(BF16)
            dh = fold(dk_ref, t)
            x = k_ref[:, t * LANES:(t + 1) * LANES]
            r = lax.rsqrt(_segmean64(x * x) + NORM_EPS)
            y = x * r
            dy = dh * g_ref[...]
            dko_ref[:, t * LANES:(t + 1) * LANES] = (r * (dy - y * _segmean64(dy * y))).astype(BF16)
            dg_ref[...] += jnp.sum(dh * y, axis=0, keepdims=True)

    dup = pl.BlockSpec((tm, SWA_KV * LANES), lambda i: (i, 0))
    out = pl.BlockSpec((tm, W), lambda i: (i, 0))
    shape = jax.ShapeDtypeStruct((S, W), BF16)
    return pl.pallas_call(
        body, name=name, grid=(S // tm,),
        in_specs=[dup, dup, pl.BlockSpec((tm, W), lambda i: (i, k_off // W)), pl.BlockSpec((1, LANES), lambda i: (0, 0))],
        out_specs=[out, out, pl.BlockSpec((1, LANES), lambda i: (0, 0))],
        out_shape=[shape, shape, jax.ShapeDtypeStruct((1, LANES), F32)],
        compiler_params=_params(("arbitrary",), 16 * tm * W * 4),
    )(dkdup, dvdup, proj, jnp.tile(gk64, 2).reshape(1, LANES))


def _swa_geometry(i):
    r = lax.broadcasted_iota(I32, (BLK, 2 * BLK), 0)
    c = lax.broadcasted_iota(I32, (BLK, 2 * BLK), 1)
    rel = r + BLK - c
    valid = (rel >= 0) & (rel < BLK) & ((c >= BLK) | (i > 0))
    return valid, rel.astype(F32)


def _swa_slope(h):
    return 2.0 ** (-8.0 * (h + 1) / SWA_HEADS)


def swa_attn_fwd(qn, kdup, vdup, sinks, *, name):
    S = qn.shape[0]
    nq = S // BLK
    group = SWA_HEADS // SWA_KV

    def body(q_ref, kp_ref, kc_ref, vp_ref, vc_ref, sink_ref, o_ref, lse_ref):
        i = pl.program_id(0)
        valid, rel = _swa_geometry(i)
        first = _first_half((BLK, LANES))
        lane = lax.broadcasted_iota(I32, (BLK, LANES), 1)
        lse_all = jnp.zeros((BLK, LANES), F32)
        for g in range(SWA_KV):
            cols = slice(g * LANES, (g + 1) * LANES)
            kk = jnp.concatenate([kp_ref[:, cols], kc_ref[:, cols]], axis=0)
            vv = jnp.concatenate([vp_ref[:, cols], vc_ref[:, cols]], axis=0)
            for pair in range(group // 2):
                tile = (g * group) // 2 + pair
                q2 = q_ref[:, tile * LANES:(tile + 1) * LANES]
                outs = []
                for a in range(2):
                    h = 2 * tile + a
                    qh = jnp.where(first if a == 0 else ~first, q2, jnp.zeros_like(q2))
                    s = _dot(qh, kk, NT) * (1.0 / math.sqrt(HEAD64)) - _swa_slope(h) * rel
                    s = jnp.where(valid, s, -1e30)
                    sink = sink_ref[h]
                    m = jnp.maximum(jnp.max(s, axis=1, keepdims=True), sink)
                    e = jnp.exp(s - m)
                    den = jnp.sum(e, axis=1, keepdims=True) + jnp.exp(sink - m)
                    outs.append(_dot((e / den).astype(BF16), vv))
                    lse_all = jnp.where(lane == h, m + jnp.log(den), lse_all)
                o_ref[:, tile * LANES:(tile + 1) * LANES] = jnp.where(first, outs[0], outs[1])
        lse_ref[...] = lse_all

    prev = lambda i: (jnp.maximum(i - 1, 0), 0)
    cur = lambda i: (i, 0)
    kvw = SWA_KV * LANES
    return pl.pallas_call(
        body, name=name, grid=(nq,),
        in_specs=[pl.BlockSpec((BLK, 1024), cur), pl.BlockSpec((BLK, kvw), prev), pl.BlockSpec((BLK, kvw), cur),
                  pl.BlockSpec((BLK, kvw), prev), pl.BlockSpec((BLK, kvw), cur),
                  pl.BlockSpec(memory_space=pltpu.SMEM)],
        out_specs=[pl.BlockSpec((BLK, 1024), cur), pl.BlockSpec((BLK, LANES), cur)],
        out_shape=[jax.ShapeDtypeStruct((S, 1024), F32), jax.ShapeDtypeStruct((S, LANES), F32)],
        compiler_params=_params(("parallel",), 16 * BLK * 1024 * 4),
    )(qn, kdup, kdup, vdup, vdup, sinks)


def swa_attn_bwd(qn, kdup, vdup, sinks, o, lse, do, *, name):
    S = qn.shape[0]
    nq = S // BLK
    group = SWA_HEADS // SWA_KV
    scale = 1.0 / math.sqrt(HEAD64)

    def body(q_ref, kp_ref, kc_ref, vp_ref, vc_ref, sink_ref, o_ref, lse_ref, do_ref,
             dq_ref, dk_ref, dv_ref, ds_ref):
        i = pl.program_id(0)

        @pl.when(i == 0)
        def _():
            dk_ref[...] = jnp.zeros_like(dk_ref)
            dv_ref[...] = jnp.zeros_like(dv_ref)
            ds_ref[...] = jnp.zeros_like(ds_ref)

        valid, rel = _swa_geometry(i)
        first = _first_half((BLK, LANES))
        lane1 = lax.broadcasted_iota(I32, (1, LANES), 1)
        lane = lax.broadcasted_iota(I32, (BLK, LANES), 1)
        lse_all = lse_ref[...]
        prow = pl.ds(pl.multiple_of(jnp.maximum(i - 1, 0) * BLK, BLK), BLK)
        crow = pl.ds(pl.multiple_of(i * BLK, BLK), BLK)
        dsink = jnp.zeros((1, LANES), F32)
        for g in range(SWA_KV):
            cols = slice(g * LANES, (g + 1) * LANES)
            kk = jnp.concatenate([kp_ref[:, cols], kc_ref[:, cols]], axis=0)
            vv = jnp.concatenate([vp_ref[:, cols], vc_ref[:, cols]], axis=0)
            dkk = jnp.zeros((2 * BLK, LANES), F32)
            dvv = jnp.zeros((2 * BLK, LANES), F32)
            for pair in range(group // 2):
                tile = (g * group) // 2 + pair
                tcols = slice(tile * LANES, (tile + 1) * LANES)
                q2 = q_ref[:, tcols]
                do2 = do_ref[:, tcols]
                prod = do2 * o_ref[:, tcols]
                dqs = []
                for a in range(2):
                    h = 2 * tile + a
                    mine = first if a == 0 else ~first
                    qh = jnp.where(mine, q2, jnp.zeros_like(q2))
                    doh = jnp.where(mine, do2, 0.0).astype(BF16)
                    delta = jnp.sum(jnp.where(mine, prod, 0.0), axis=1, keepdims=True)
                    lse_h = jnp.sum(jnp.where(lane == h, lse_all, 0.0), axis=1, keepdims=True)
                    s = _dot(qh, kk, NT) * scale - _swa_slope(h) * rel
                    p = jnp.where(valid, jnp.exp(s - lse_h), 0.0)
                    dsc = (p * (_dot(doh, vv, NT) - delta) * scale).astype(BF16)
                    dqs.append(_dot(dsc, kk))
                    dkk = dkk + _dot(dsc, qh, TN)
                    dvv = dvv + _dot(p.astype(BF16), doh, TN)
                    psink = jnp.exp(sink_ref[h] - lse_h)
                    dsink = dsink + jnp.where(lane1 == h, -jnp.sum(psink * delta), 0.0)
                dq_ref[:, tcols] = jnp.where(first, dqs[0], dqs[1])
            dk_ref[prow, cols] += dkk[:BLK]
            dv_ref[prow, cols] += dvv[:BLK]
            dk_ref[crow, cols] += dkk[BLK:]
            dv_ref[crow, cols] += dvv[BLK:]
        ds_ref[...] += dsink

    prev = lambda i: (jnp.maximum(i - 1, 0), 0)
    cur = lambda i: (i, 0)
    kvw = SWA_KV * LANES
    whole = pl.BlockSpec((S, kvw), lambda i: (0, 0))
    return pl.pallas_call(
        body, name=name, grid=(nq,),
        in_specs=[pl.BlockSpec((BLK, 1024), cur), pl.BlockSpec((BLK, kvw), prev), pl.BlockSpec((BLK, kvw), cur),
                  pl.BlockSpec((BLK, kvw), prev), pl.BlockSpec((BLK, kvw), cur),
                  pl.BlockSpec(memory_space=pltpu.SMEM),
                  pl.BlockSpec((BLK, 1024), cur), pl.BlockSpec((BLK, LANES), cur), pl.BlockSpec((BLK, 1024), cur)],
        out_specs=[pl.BlockSpec((BLK, 1024), cur), whole, whole, pl.BlockSpec((1, LANES), lambda i: (0, 0))],
        out_shape=[jax.ShapeDtypeStruct((S, 1024), F32), jax.ShapeDtypeStruct((S, kvw), F32),
                   jax.ShapeDtypeStruct((S, kvw), F32), jax.ShapeDtypeStruct((1, LANES), F32)],
        compiler_params=_params(("arbitrary",), 4 * S * kvw * 4 + 24 * BLK * 1024 * 4),
    )(qn, kdup, kdup, vdup, vdup, sinks, o, lse, do)


def _in_bwd(x, h, dproj, dy, g, w_in, tag):
    dh = matmul(dproj, w_in, "nt", name=f"{tag}_dh")
    dw_in = matmul(h, dproj, "tn", name=f"{tag}_dwin")
    dx, dg = rmsnorm_bwd(x, g, dh, col_off=0, width=D_MODEL, out_dtype=F32, res=dy, name=f"{tag}_dnorm")
    return dx, dw_in, dg


def _out_bwd(dy, og, o, proj, w_out, gate_off, tag):
    dog = matmul(dy, w_out, "nt", name=f"{tag}_dog")
    dw_out = matmul(og, dy, "tn", name=f"{tag}_dwout")
    do, dgate = gate_bwd(dog, o, proj, gate_off=gate_off, name=f"{tag}_dgate")
    return do, dgate, dw_out


def sb_fwd(x, p, tag):
    h = rmsnorm_fwd(x, p["norm"], col_off=0, width=D_MODEL, out_dtype=BF16, name=f"{tag}_norm")
    proj = matmul(h, p["w_in"], "nn", name=f"{tag}_proj")
    o = sb_attn_fwd(proj, name=f"{tag}_attn")
    og = gate_fwd(o, proj, gate_off=3 * D_MODEL, name=f"{tag}_gate")
    y = matmul(og, p["w_out"], "nn", add=x, name=f"{tag}_out")
    return y, (x, h, proj, o, og)


def sb_bwd(dy, saved, p, tag):
    x, h, proj, o, og = saved
    do, dgate, dw_out = _out_bwd(dy, og, o, proj, p["w_out"], 3 * D_MODEL, tag)
    dq, dk, dv = sb_attn_bwd(proj, o, do, name=f"{tag}_dattn")
    dproj = jnp.concatenate([dq.astype(BF16), dk.astype(BF16), dv.astype(BF16), dgate], axis=1)
    dx, dw_in, dg = _in_bwd(x, h, dproj, dy, p["norm"], p["w_in"], tag)
    return dx, {"norm": dg[0], "w_in": dw_in, "w_out": dw_out}


MLA_GATE, MLA_QLAT, MLA_KVLAT, MLA_KPE, MLA_IN = 0, 1024, 1280, 1408, 1536


def mla_fwd(x, p, tabs, tag):
    cos, sgn = tabs
    h = rmsnorm_fwd(x, p["norm"], col_off=0, width=D_MODEL, out_dtype=BF16, name=f"{tag}_norm")
    proj = matmul(h, p["w_in"], "nn", name=f"{tag}_proj")
    ql = rmsnorm_fwd(proj, p["q_a_norm"], col_off=MLA_QLAT, width=256, out_dtype=BF16, name=f"{tag}_qanorm")
    kvl = rmsnorm_fwd(proj, p["kv_a_norm"], col_off=MLA_KVLAT, width=128, out_dtype=BF16, name=f"{tag}_kvanorm")
    qraw = matmul(ql, p["w_uq"], "nn", name=f"{tag}_uq")
    kv = matmul(kvl, p["w_ukv"], "nn", name=f"{tag}_ukv")
    qn, kn = mla_prep_fwd(qraw, kv, proj, p["gq"], p["gk"], cos, sgn, kpe_off=MLA_KPE, name=f"{tag}_prep")
    o, lse = flash_fwd(qn, kn, kv, name=f"{tag}_attn")
    og = gate_fwd(o, proj, gate_off=MLA_GATE, name=f"{tag}_gate")
    y = matmul(og, p["w_out"], "nn", add=x, name=f"{tag}_out")
    return y, (x, h, proj, ql, kvl, qraw, kv, qn, kn, o, lse, og)


def mla_bwd(dy, saved, p, tabs, tag):
    cos, sgn = tabs
    x, h, proj, ql, kvl, qraw, kv, qn, kn, o, lse, og = saved
    do, dgate, dw_out = _out_bwd(dy, og, o, proj, p["w_out"], MLA_GATE, tag)
    dqn, dkn, dv = flash_bwd(qn, kn, kv, o, lse, do, name=f"{tag}_dattn")
    dqraw, dkv, dkpe, dgq, dgk = mla_prep_bwd(dqn, dkn, dv, qraw, kv, proj, p["gq"], p["gk"], cos, sgn,
                                              kpe_off=MLA_KPE, name=f"{tag}_dprep")
    dql = matmul(dqraw, p["w_uq"], "nt", name=f"{tag}_dql")
    dw_uq = matmul(ql, dqraw, "tn", name=f"{tag}_dwuq")
    dkvl = matmul(dkv, p["w_ukv"], "nt", name=f"{tag}_dkvl")
    dw_ukv = matmul(kvl, dkv, "tn", name=f"{tag}_dwukv")
    dqlat, dgqa = rmsnorm_bwd(proj, p["q_a_norm"], dql, col_off=MLA_QLAT, width=256, out_dtype=BF16, name=f"{tag}_dqanorm")
    dkvlat, dgkva = rmsnorm_bwd(proj, p["kv_a_norm"], dkvl, col_off=MLA_KVLAT, width=128, out_dtype=BF16,
                                name=f"{tag}_dkvanorm")
    dproj = jnp.concatenate([dgate, dqlat, dkvlat, dkpe], axis=1)
    dx, dw_in, dg = _in_bwd(x, h, dproj, dy, p["norm"], p["w_in"], tag)
    return dx, {"norm": dg[0], "w_in": dw_in, "q_a_norm": dgqa[0], "w_uq": dw_uq, "kv_a_norm": dgkva[0],
                "w_ukv": dw_ukv, "gq": dgq[0], "gk": dgk[0], "w_out": dw_out}


SWA_Q, SWA_GATE, SWA_K, SWA_V = 0, 1024, 2048, 2304


def swa_fwd(x, p, tag):
    h = rmsnorm_fwd(x, p["norm"], col_off=0, width=D_MODEL, out_dtype=BF16, name=f"{tag}_norm")
    proj = matmul(h, p["w_in"], "nn", name=f"{tag}_proj")
    qn = headnorm_fwd(proj, p["q_head_norm"], col_off=SWA_Q, width=1024, name=f"{tag}_qnorm")
    kdup, vdup = swa_kv_prep(proj, p["k_head_norm"], k_off=SWA_K, v_off=SWA_V, name=f"{tag}_kvprep")
    o, lse = swa_attn_fwd(qn, kdup, vdup, p["sinks"], name=f"{tag}_attn")
    og = gate_fwd(o, proj, gate_off=SWA_GATE, name=f"{tag}_gate")
    y = matmul(og, p["w_out"], "nn", add=x, name=f"{tag}_out")
    return y, (x, h, proj, qn, kdup, vdup, o, lse, og)


def swa_bwd(dy, saved, p, tag):
    x, h, proj, qn, kdup, vdup, o, lse, og = saved
    do, dgate, dw_out = _out_bwd(dy, og, o, proj, p["w_out"], SWA_GATE, tag)
    dqn, dkdup, dvdup, dsinks = swa_attn_bwd(qn, kdup, vdup, p["sinks"], o, lse, do, name=f"{tag}_dattn")
    dq, dgq = headnorm_bwd(proj, p["q_head_norm"], dqn, col_off=SWA_Q, width=1024, name=f"{tag}_dqnorm")
    dk, dv, dgk = swa_kv_prep_bwd(dkdup, dvdup, proj, p["k_head_norm"], k_off=SWA_K, name=f"{tag}_dkvprep")
    dproj = jnp.concatenate([dq, dgate, dk, dv], axis=1)
    dx, dw_in, dg = _in_bwd(x, h, dproj, dy, p["norm"], p["w_in"], tag)
    return dx, {"norm": dg[0], "w_in": dw_in, "q_head_norm": dgq[0, :HEAD64] + dgq[0, HEAD64:],
                "k_head_norm": dgk[0, :HEAD64] + dgk[0, HEAD64:], "sinks": dsinks[0, :SWA_HEADS], "w_out": dw_out}


def prepare_weights(w):
    l1_in, l2_in = w["l1_w_in"], w["l2_w_in"]
    pad64 = lambda v: jnp.pad(v, (0, MLA_PAD - MLA_QK)).reshape(1, MLA_PAD)
    return [
        {"norm": w["l0_norm"], "w_in": w["l0_w_in"], "w_out": w["l0_w_out"]},
        {"norm": w["l1_norm"],
         "w_in": jnp.concatenate([l1_in[:, 448:], l1_in[:, :448], jnp.zeros((D_MODEL, 64), l1_in.dtype)], axis=1),
         "q_a_norm": w["l1_q_a_norm"], "kv_a_norm": w["l1_kv_a_norm"],
         "w_uq": jnp.pad(w["l1_w_uq"].reshape(256, MLA_HEADS, MLA_QK), ((0, 0), (0, 0), (0, MLA_PAD - MLA_QK))
                         ).reshape(256, MLA_HEADS * MLA_PAD),
         "w_ukv": w["l1_w_ukv"], "gq": pad64(w["l1_q_head_norm"]), "gk": pad64(w["l1_k_head_norm"]),
         "w_out": w["l1_w_out"]},
        {"norm": w["l2_norm"],
         "w_in": jnp.concatenate([l2_in[:, :1024], l2_in[:, 1536:], l2_in[:, 1024:1536]], axis=1),
         "q_head_norm": w["l2_q_head_norm"], "k_head_norm": w["l2_k_head_norm"], "sinks": w["l2_sinks"],
         "w_out": w["l2_w_out"]},
        {"norm": w["l3_norm"], "w_in": w["l3_w_in"], "w_out": w["l3_w_out"]},
    ]


def unprepare_grads(gs):
    g0, g1, g2, g3 = gs
    d1, d2 = g1["w_in"], g2["w_in"]
    return {
        "l0_norm": g0["norm"], "l0_w_in": g0["w_in"], "l0_w_out": g0["w_out"],
        "l1_norm": g1["norm"], "l1_w_in": jnp.concatenate([d1[:, 1024:1472], d1[:, :1024]], axis=1),
        "l1_q_a_norm": g1["q_a_norm"],
        "l1_w_uq": g1["w_uq"].reshape(256, MLA_HEADS, MLA_PAD)[:, :, :MLA_QK].reshape(256, MLA_HEADS * MLA_QK),
        "l1_kv_a_norm": g1["kv_a_norm"], "l1_w_ukv": g1["w_ukv"],
        "l1_q_head_norm": g1["gq"][:MLA_QK], "l1_k_head_norm": g1["gk"][:MLA_QK], "l1_w_out": g1["w_out"],
        "l2_norm": g2["norm"], "l2_w_in": jnp.concatenate([d2[:, :1024], d2[:, 2048:], d2[:, 1024:2048]], axis=1),
        "l2_q_head_norm": g2["q_head_norm"], "l2_k_head_norm": g2["k_head_norm"], "l2_sinks": g2["sinks"],
        "l2_w_out": g2["w_out"],
        "l3_norm": g3["norm"], "l3_w_in": g3["w_in"], "l3_w_out": g3["w_out"],
    }


def local_step(x, target, w):
    ps = prepare_weights(w)
    tabs = _rope_tables(x.shape[0])
    y0, s0 = sb_fwd(x, ps[0], "l0")
    y1, s1 = mla_fwd(y0, ps[1], tabs, "l1")
    y2, s2 = swa_fwd(y1, ps[2], "l2")
    y3, s3 = sb_fwd(y2, ps[3], "l3")
    dy, loss = loss_head(y3, target, name="loss")
    d3, g3 = sb_bwd(dy, s3, ps[3], "l3")
    d2, g2 = swa_bwd(d3, s2, ps[2], "l2")
    d1, g1 = mla_bwd(d2, s1, ps[1], tabs, "l1")
    d0, g0 = sb_bwd(d1, s0, ps[0], "l0")
    return loss, d0, unprepare_grads([g0, g1, g2, g3])


MATS = (("l0_w_in", "col", 1024, 4096), ("l0_w_out", "row", 1024, 1024), ("l1_w_in", "col", 1024, 1472),
        ("l1_w_uq", "col", 256, 1536), ("l1_w_ukv", "col", 128, 2048), ("l1_w_out", "row", 1024, 1024),
        ("l2_w_in", "col", 1024, 2560), ("l2_w_out", "row", 1024, 1024), ("l3_w_in", "col", 1024, 4096),
        ("l3_w_out", "row", 1024, 1024))
N_CHIPS = 4
PACK_W = 1024
HALF_ROWS = 2176
PACK_ROWS = 2 * HALF_ROWS
VECS = (("l0_norm", 0, 0, 1024), ("l1_norm", 1, 0, 1024), ("l2_norm", 2, 0, 1024), ("l3_norm", 3, 0, 1024),
        ("l1_q_a_norm", 4, 0, 256), ("l1_kv_a_norm", 4, 256, 128), ("l1_q_head_norm", 4, 384, 192),
        ("l1_k_head_norm", 4, 576, 192), ("l2_q_head_norm", 4, 768, 64), ("l2_k_head_norm", 4, 832, 64),
        ("l2_sinks", 4, 896, 16))
LOSS_SLOT = (4, 912)
VEC_ROWS = 8


def _shard_rows(k, n):
    return k * n // N_CHIPS // PACK_W


def pack_shards(shards):
    parts = [shards[name].reshape(-1, PACK_W) for name, _, _, _ in MATS]
    used = sum(p.shape[0] for p in parts)
    return jnp.concatenate(parts + [jnp.zeros((PACK_ROWS - used, PACK_W), parts[0].dtype)], axis=0)


def unpack_shards(flat):
    out, r0 = {}, 0
    for name, kind, k, n in MATS:
        rows = _shard_rows(k, n)
        shape = (k, n // N_CHIPS) if kind == "col" else (k // N_CHIPS, n)
        out[name] = flat[r0:r0 + rows].reshape(shape)
        r0 += rows
    return out


def pack_full(mats):
    parts = []
    for name, kind, k, n in MATS:
        m = mats[name]
        if kind == "col":
            m = m.reshape(k, N_CHIPS, n // N_CHIPS).transpose(1, 0, 2)
        parts.append(m.reshape(N_CHIPS, -1, PACK_W))
    used = sum(p.shape[1] for p in parts)
    return jnp.concatenate(parts + [jnp.zeros((N_CHIPS, PACK_ROWS - used, PACK_W), parts[0].dtype)], axis=1)


def unpack_full(stacked):
    out, r0 = {}, 0
    for name, kind, k, n in MATS:
        rows = _shard_rows(k, n)
        seg = stacked[:, r0:r0 + rows]
        if kind == "col":
            out[name] = seg.reshape(N_CHIPS, k, n // N_CHIPS).transpose(1, 0, 2).reshape(k, n)
        else:
            out[name] = seg.reshape(k, n)
        r0 += rows
    return out


def pack_vecs(vecs, loss=None):
    rows = []
    for r in range(VEC_ROWS):
        items = [(off, vecs[name]) for name, rr, off, _ in VECS if rr == r]
        if loss is not None and r == LOSS_SLOT[0]:
            items.append((LOSS_SLOT[1], loss.reshape(1)))
        pos, parts = 0, []
        for off, v in sorted(items, key=lambda t: t[0]):
            assert off == pos
            parts.append(v.astype(F32))
            pos += v.shape[0]
        parts.append(jnp.zeros((PACK_W - pos,), F32))
        rows.append(jnp.concatenate(parts))
    return jnp.stack(rows)


def unpack_vecs(block):
    return {name: block[r, off:off + n] for name, r, off, n in VECS}


def _me():
    return lax.axis_index("x"), lax.axis_index("y"), lax.axis_index("c")


OTHER_CHIPS = ((1, 0), (0, 1), (1, 1))


def _remote(src, dst, send_sem, recv_sem, to):
    return pltpu.make_async_remote_copy(src_ref=src, dst_ref=dst, send_sem=send_sem, recv_sem=recv_sem,
                                        device_id=to, device_id_type=MESH)


def gather_weights(block):
    def body(in_ref, out_ref, send_sems, recv_sems, local_sem):
        x, y, c = _me()
        sibling = (x, y, 1 - c)

        def half(px, py, pc):
            return out_ref.at[2 * px + py, pl.ds(pc * HALF_ROWS, HALF_ROWS), :]

        mine = pltpu.make_async_copy(in_ref, out_ref.at[2 * x + y], local_sem)
        mine.start()
        chips = [(x ^ dx, y ^ dy) for dx, dy in OTHER_CHIPS]
        first = [_remote(in_ref.at[pl.ds(c * HALF_ROWS, HALF_ROWS), :], half(x, y, c), send_sems.at[j], recv_sems.at[j],
                         (*chip, c)) for j, chip in enumerate(chips)]
        for cp in first:
            cp.start()
        passed = [_remote(half(*chip, c), half(*chip, c), send_sems.at[3 + j], recv_sems.at[3 + j], sibling)
                  for j, chip in enumerate(chips)]
        for j, chip in enumerate(chips):
            _remote(half(*chip, c), half(*chip, c), send_sems.at[j], recv_sems.at[j], (*chip, c)).wait_recv()
            passed[j].start()
        for j, chip in enumerate(chips):
            _remote(half(*chip, 1 - c), half(*chip, 1 - c), send_sems.at[3 + j], recv_sems.at[3 + j], sibling).wait_recv()
        for cp in first + passed:
            cp.wait_send()
        mine.wait()

    hbm = pl.BlockSpec(memory_space=pltpu.HBM)
    return pl.pallas_call(
        body, name="gather_weights",
        out_shape=jax.ShapeDtypeStruct((N_CHIPS, PACK_ROWS, PACK_W), block.dtype),
        in_specs=[hbm], out_specs=hbm,
        scratch_shapes=[pltpu.SemaphoreType.DMA((6,)), pltpu.SemaphoreType.DMA((6,)), pltpu.SemaphoreType.DMA],
    )(block)


def pair_exchange(g):
    def body(g_ref, out_ref, send_sems, recv_sems):
        x, y, c = _me()
        sibling = (x, y, 1 - c)
        copies = [_remote(g_ref.at[k, pl.ds((1 - c) * HALF_ROWS, HALF_ROWS), :], out_ref.at[k], send_sems.at[k],
                          recv_sems.at[k], sibling) for k in range(N_CHIPS)]
        for cp in copies:
            cp.start()
        for cp in copies:
            cp.wait_recv()
        for cp in copies:
            cp.wait_send()

    hbm = pl.BlockSpec(memory_space=pltpu.HBM)
    return pl.pallas_call(
        body, name="pair_exchange",
        out_shape=jax.ShapeDtypeStruct((N_CHIPS, HALF_ROWS, PACK_W), g.dtype),
        in_specs=[hbm], out_specs=hbm,
        scratch_shapes=[pltpu.SemaphoreType.DMA((N_CHIPS,)), pltpu.SemaphoreType.DMA((N_CHIPS,))],
    )(g)


def chip_exchange(part):
    def body(p_ref, out_ref, send_sems, recv_sems):
        x, y, c = _me()
        copies = [_remote(p_ref.at[2 * (x ^ dx) + (y ^ dy)], out_ref.at[j], send_sems.at[j], recv_sems.at[j],
                          (x ^ dx, y ^ dy, c)) for j, (dx, dy) in enumerate(OTHER_CHIPS)]
        for cp in copies:
            cp.start()
        for cp in copies:
            cp.wait_recv()
        for cp in copies:
            cp.wait_send()

    hbm = pl.BlockSpec(memory_space=pltpu.HBM)
    return pl.pallas_call(
        body, name="chip_exchange",
        out_shape=jax.ShapeDtypeStruct((len(OTHER_CHIPS), HALF_ROWS, PACK_W), part.dtype),
        in_specs=[hbm], out_specs=hbm,
        scratch_shapes=[pltpu.SemaphoreType.DMA((3,)), pltpu.SemaphoreType.DMA((3,))],
    )(part)


def join_halves(half):
    def body(h_ref, out_ref, send_sem, recv_sem, local_sem):
        x, y, c = _me()
        mine = pltpu.make_async_copy(h_ref, out_ref.at[pl.ds(c * HALF_ROWS, HALF_ROWS), :], local_sem)
        mine.start()
        cp = _remote(h_ref, out_ref.at[pl.ds(c * HALF_ROWS, HALF_ROWS), :], send_sem, recv_sem, (x, y, 1 - c))
        cp.start()
        _remote(h_ref, out_ref.at[pl.ds((1 - c) * HALF_ROWS, HALF_ROWS), :], send_sem, recv_sem, (x, y, 1 - c)).wait_recv()
        cp.wait_send()
        mine.wait()

    hbm = pl.BlockSpec(memory_space=pltpu.HBM)
    return pl.pallas_call(
        body, name="join_halves",
        out_shape=jax.ShapeDtypeStruct((PACK_ROWS, PACK_W), half.dtype),
        in_specs=[hbm], out_specs=hbm,
        scratch_shapes=[pltpu.SemaphoreType.DMA, pltpu.SemaphoreType.DMA, pltpu.SemaphoreType.DMA],
    )(half)


def sum_over_devices(block):
    def body(in_ref, out_ref, all_ref, send_sems, recv_sems):
        x, y, c = _me()
        me = 4 * x + 2 * y + c
        all_ref[me] = in_ref[...]
        copies = []
        for r in range(1, 8):
            to = (x ^ (r >> 2), y ^ ((r >> 1) & 1), c ^ (r & 1))
            copies.append(_remote(in_ref, all_ref.at[me], send_sems.at[r - 1], recv_sems.at[r - 1], to))
        for cp in copies:
            cp.start()
        for r in range(1, 8):
            frm = (x ^ (r >> 2), y ^ ((r >> 1) & 1), c ^ (r & 1))
            _remote(in_ref, all_ref.at[4 * frm[0] + 2 * frm[1] + frm[2]], send_sems.at[r - 1], recv_sems.at[r - 1],
                    frm).wait_recv()
        for cp in copies:
            cp.wait_send()
        acc = all_ref[0]
        for d in range(1, 8):
            acc = acc + all_ref[d]
        out_ref[...] = acc

    vmem = pl.BlockSpec(memory_space=pltpu.VMEM)
    return pl.pallas_call(
        body, name="sum_over_devices",
        out_shape=jax.ShapeDtypeStruct(block.shape, F32),
        in_specs=[vmem], out_specs=vmem,
        scratch_shapes=[pltpu.VMEM((8,) + block.shape, F32), pltpu.SemaphoreType.DMA((7,)), pltpu.SemaphoreType.DMA((7,))],
    )(block)


SUM_ROWS = 272


def pair_sum(g, got, core):
    steps = HALF_ROWS // SUM_ROWS

    def body(s_ref, g_ref, r_ref, o_ref):
        o_ref[...] = g_ref[...] + r_ref[...]

    blk = pl.BlockSpec((1, SUM_ROWS, PACK_W), lambda k, i, s: (k, i, 0))
    return pl.pallas_call(
        body, name="pair_sum",
        grid_spec=pltpu.PrefetchScalarGridSpec(
            num_scalar_prefetch=1, grid=(N_CHIPS, steps),
            in_specs=[pl.BlockSpec((1, SUM_ROWS, PACK_W), lambda k, i, s: (k, s[0] * steps + i, 0)), blk],
            out_specs=blk),
        out_shape=jax.ShapeDtypeStruct((N_CHIPS, HALF_ROWS, PACK_W), F32),
        compiler_params=_params(("parallel", "parallel"), 6 * SUM_ROWS * PACK_W * 4),
    )(core, g, got)


def chip_sum(part, got, chip):
    steps = HALF_ROWS // SUM_ROWS

    def body(s_ref, p_ref, a_ref, b_ref, c_ref, o_ref):
        o_ref[...] = ((p_ref[0] + a_ref[0]) + b_ref[0]) + c_ref[0]

    def got_spec(j):
        return pl.BlockSpec((1, SUM_ROWS, PACK_W), lambda i, s: (j, i, 0))

    return pl.pallas_call(
        body, name="chip_sum",
        grid_spec=pltpu.PrefetchScalarGridSpec(
            num_scalar_prefetch=1, grid=(steps,),
            in_specs=[pl.BlockSpec((1, SUM_ROWS, PACK_W), lambda i, s: (s[0], i, 0)), got_spec(0), got_spec(1), got_spec(2)],
            out_specs=pl.BlockSpec((SUM_ROWS, PACK_W), lambda i, s: (i, 0))),
        out_shape=jax.ShapeDtypeStruct((HALF_ROWS, PACK_W), F32),
        compiler_params=_params(("parallel",), 10 * SUM_ROWS * PACK_W * 4),
    )(chip, part, got, got, got)


def reduce_scatter(g):
    x, y, c = _me()
    part = pair_sum(g, pair_exchange(g), jnp.reshape(c, (1,)).astype(I32))
    half = chip_sum(part, chip_exchange(part), jnp.reshape(2 * x + y, (1,)).astype(I32))
    return join_halves(half)


def adamw(w, g, m, v, *, name):
    rows, cols = w.shape
    tm = _tile(rows, 256)
    c1 = 1.0 - ADAM_B1 ** ADAM_STEP
    c2 = 1.0 - ADAM_B2 ** ADAM_STEP

    def body(w_ref, g_ref, m_ref, v_ref, d_ref, mo_ref, vo_ref):
        gv = g_ref[...]
        mn = ADAM_B1 * m_ref[...] + (1.0 - ADAM_B1) * gv
        vn = ADAM_B2 * v_ref[...] + (1.0 - ADAM_B2) * (gv * gv)
        d_ref[...] = -ADAM_LR * ((mn / c1) / (jnp.sqrt(vn / c2) + ADAM_EPS) + ADAM_WD * w_ref[...])
        mo_ref[...] = mn
        vo_ref[...] = vn

    blk = pl.BlockSpec((tm, cols), lambda i: (i, 0))
    shape = jax.ShapeDtypeStruct(w.shape, F32)
    return pl.pallas_call(
        body, name=name, grid=(rows // tm,),
        in_specs=[blk] * 4, out_specs=[blk] * 3, out_shape=[shape] * 3,
        compiler_params=_params(("parallel",), 16 * tm * cols * 4),
    )(w, g, m, v)


WEIGHTS = ("l0_norm", "l0_w_in", "l0_w_out", "l1_norm", "l1_w_in", "l1_q_a_norm", "l1_w_uq", "l1_kv_a_norm", "l1_w_ukv",
           "l1_q_head_norm", "l1_k_head_norm", "l1_w_out", "l2_norm", "l2_w_in", "l2_q_head_norm", "l2_k_head_norm",
           "l2_sinks", "l2_w_out", "l3_norm", "l3_w_in", "l3_w_out")


def kernel(x, l0_norm, l0_w_in, l0_w_out, l1_norm, l1_w_in, l1_q_a_norm, l1_w_uq, l1_kv_a_norm, l1_w_ukv, l1_q_head_norm, l1_k_head_norm, l1_w_out, l2_norm, l2_w_in, l2_q_head_norm, l2_k_head_norm, l2_sinks, l2_w_out, l3_norm, l3_w_in, l3_w_out, loss_target, m_l0_norm, m_l0_w_in, m_l0_w_out, m_l1_norm, m_l1_w_in, m_l1_q_a_norm, m_l1_w_uq, m_l1_kv_a_norm, m_l1_w_ukv, m_l1_q_head_norm, m_l1_k_head_norm, m_l1_w_out, m_l2_norm, m_l2_w_in, m_l2_q_head_norm, m_l2_k_head_norm, m_l2_sinks, m_l2_w_out, m_l3_norm, m_l3_w_in, m_l3_w_out, v_l0_norm, v_l0_w_in, v_l0_w_out, v_l1_norm, v_l1_w_in, v_l1_q_a_norm, v_l1_w_uq, v_l1_kv_a_norm, v_l1_w_ukv, v_l1_q_head_norm, v_l1_k_head_norm, v_l1_w_out, v_l2_norm, v_l2_w_in, v_l2_q_head_norm, v_l2_k_head_norm, v_l2_sinks, v_l2_w_out, v_l3_norm, v_l3_w_in, v_l3_w_out):
    given = dict(locals())
    w = {n: given[n] for n in WEIGHTS}
    m = {n: given["m_" + n] for n in WEIGHTS}
    v = {n: given["v_" + n] for n in WEIGHTS}
    mat_names = [t[0] for t in MATS]
    vec_names = [t[0] for t in VECS]

    gathered = gather_weights(pack_shards({n: w[n] for n in mat_names}).astype(BF16))
    full = unpack_full(gathered)
    full.update({n: w[n] for n in vec_names})
    loss_tile, grad_x, grads = local_step(x[0], loss_target[0], full)

    block = reduce_scatter(pack_full({n: grads[n] for n in mat_names}))
    g = unpack_shards(block)
    vec_sum = sum_over_devices(pack_vecs({n: grads[n] for n in vec_names}, loss=loss_tile[0, 0]))
    loss = vec_sum[LOSS_SLOT[0], LOSS_SLOT[1]]

    delta, new_m, new_v = {}, {}, {}
    for n in mat_names:
        delta[n], new_m[n], new_v[n] = adamw(w[n], g[n], m[n], v[n], name=f"adamw_{n}")
    dv, mv, vv = adamw(pack_vecs(w), vec_sum, pack_vecs(m), pack_vecs(v), name="adamw_vecs")
    g.update(unpack_vecs(vec_sum))
    delta.update(unpack_vecs(dv))
    new_m.update(unpack_vecs(mv))
    new_v.update(unpack_vecs(vv))
    return (loss, grad_x[None], *[g[n] for n in WEIGHTS], *[delta[n] for n in WEIGHTS],
            *[new_m[n] for n in WEIGHTS], *[new_v[n] for n in WEIGHTS])
```

```python
import math

import jax
import jax.numpy as jnp
from jax import lax
from jax.experimental import pallas as pl
from jax.experimental.pallas import tpu as pltpu

F32 = jnp.float32
BF16 = jnp.bfloat16
I32 = jnp.int32
MESH = pl.DeviceIdType.MESH

NORM_EPS = 1e-6
D_MODEL = 1024
HEAD64 = 64
LANES = 128
BLK = 128
MLA_HEADS = 8
MLA_QK = 192
MLA_PAD = 256
ROPE_THETA = 10000.0
SWA_HEADS = 16
SWA_KV = 4
VMEM_CAP = 56 * 1024 * 1024

ADAM_LR, ADAM_B1, ADAM_B2, ADAM_EPS, ADAM_WD, ADAM_STEP = 0.001, 0.9, 0.999, 1e-08, 0.01, 10

NT = (((1,), (1,)), ((), ()))
NN = (((1,), (0,)), ((), ()))
TN = (((0,), (0,)), ((), ()))


def _dot(a, b, dims=NN):
    return lax.dot_general(a, b, dims, preferred_element_type=F32)


def _tile(n, pref):
    for t in (pref, 512, 256, 128):
        if t <= pref and n % t == 0:
            return t
    return n


def _params(sem, vmem_bytes):
    limit = int(min(max(2 * vmem_bytes, 24 * 1024 * 1024), VMEM_CAP))
    return pltpu.CompilerParams(dimension_semantics=sem, vmem_limit_bytes=limit)


def _split(v):
    hi = v.astype(BF16)
    return hi, (v - hi.astype(F32)).astype(BF16)


def _dot2(v, m):
    hi, lo = _split(v)
    return _dot(hi, m) + _dot(lo, m)


def _first_half(shape):
    return lax.broadcasted_iota(I32, shape, 1) < HEAD64


def _sigmoid(g):
    return 1.0 / (1.0 + jnp.exp(-g))


def matmul(a, b, mode, *, name, out_dtype=F32, add=None, tm=512, tn=1024, tk=512):
    if mode == "nn":
        (M, K), (K2, N) = a.shape, b.shape
    elif mode == "nt":
        (M, K), (N, K2) = a.shape, b.shape
    else:
        (K, M), (K2, N) = a.shape, b.shape
    assert K == K2, (a.shape, b.shape, mode)
    tm, tn, tk = _tile(M, tm), _tile(N, tn), _tile(K, tk)
    nk = K // tk
    dims = {"nn": NN, "nt": NT, "tn": TN}[mode]

    n_in = 2 if add is None else 3

    def body(*refs):
        a_ref, b_ref = refs[:2]
        out_refs, acc_ref = refs[n_in:-1], refs[-1]
        k = pl.program_id(2)

        @pl.when(k == 0)
        def _():
            acc_ref[...] = jnp.zeros_like(acc_ref)

        acc_ref[...] += _dot(a_ref[...].astype(BF16), b_ref[...].astype(BF16), dims)

        @pl.when(k == nk - 1)
        def _():
            r = acc_ref[...]
            if add is not None:
                r = r + refs[2][...]
            for o_ref in out_refs:
                o_ref[...] = r.astype(o_ref.dtype)

    a_spec = (pl.BlockSpec((tk, tm), lambda i, j, k: (k, i)) if mode == "tn"
              else pl.BlockSpec((tm, tk), lambda i, j, k: (i, k)))
    b_spec = (pl.BlockSpec((tn, tk), lambda i, j, k: (j, k)) if mode == "nt"
              else pl.BlockSpec((tk, tn), lambda i, j, k: (k, j)))
    o_spec = pl.BlockSpec((tm, tn), lambda i, j, k: (i, j))
    in_specs, args = [a_spec, b_spec], [a, b]
    if add is not None:
        in_specs.append(o_spec)
        args.append(add)
    vm = 2 * (tm * tk * a.dtype.itemsize + tk * tn * b.dtype.itemsize) + 5 * tm * tn * 4
    out_dtypes = list(out_dtype) if isinstance(out_dtype, (tuple, list)) else [out_dtype]
    res = pl.pallas_call(
        body, name=name, grid=(M // tm, N // tn, nk),
        in_specs=in_specs, out_specs=[o_spec] * len(out_dtypes),
        out_shape=[jax.ShapeDtypeStruct((M, N), d) for d in out_dtypes],
        scratch_shapes=[pltpu.VMEM((tm, tn), F32)],
        compiler_params=_params(("parallel", "parallel", "arbitrary"), vm),
    )(*args)
    return res[0] if len(out_dtypes) == 1 else res


def rmsnorm_fwd(x, g, *, col_off, width, out_dtype, name, tm=256):
    S = x.shape[0]
    assert col_off % width == 0
    cb = col_off // width
    tm = _tile(S, tm)

    def body(x_ref, g_ref, o_ref):
        v = x_ref[...]
        r = lax.rsqrt(jnp.mean(v * v, axis=1, keepdims=True) + NORM_EPS)
        o_ref[...] = (v * r * g_ref[...]).astype(out_dtype)

    return pl.pallas_call(
        body, name=name, grid=(S // tm,),
        in_specs=[pl.BlockSpec((tm, width), lambda i: (i, cb)), pl.BlockSpec((1, width), lambda i: (0, 0))],
        out_specs=pl.BlockSpec((tm, width), lambda i: (i, 0)),
        out_shape=jax.ShapeDtypeStruct((S, width), out_dtype),
        compiler_params=_params(("parallel",), 4 * tm * width * 4),
    )(x, g.reshape(1, width))


def rmsnorm_bwd(x, g, dh, *, col_off, width, out_dtype, name, res=None, tm=256):
    S = x.shape[0]
    cb = col_off // width
    tm = _tile(S, tm)

    def body(*refs):
        if res is None:
            x_ref, g_ref, dh_ref, dx_ref, dg_ref = refs
        else:
            x_ref, g_ref, dh_ref, res_ref, dx_ref, dg_ref = refs

        @pl.when(pl.program_id(0) == 0)
        def _():
            dg_ref[...] = jnp.zeros_like(dg_ref)

        v = x_ref[...]
        dhv = dh_ref[...].astype(F32)
        r = lax.rsqrt(jnp.mean(v * v, axis=1, keepdims=True) + NORM_EPS)
        y = v * r
        dy = dhv * g_ref[...]
        dx = r * (dy - y * jnp.mean(dy * y, axis=1, keepdims=True))
        if res is not None:
            dx = dx + res_ref[...]
        dx_ref[...] = dx.astype(out_dtype)
        dg_ref[...] += jnp.sum(dhv * y, axis=0, keepdims=True)

    row = pl.BlockSpec((tm, width), lambda i: (i, 0))
    in_specs = [pl.BlockSpec((tm, width), lambda i: (i, cb)), pl.BlockSpec((1, width), lambda i: (0, 0)), row]
    args = [x, g.reshape(1, width), dh]
    if res is not None:
        in_specs.append(row)
        args.append(res)
    return pl.pallas_call(
        body, name=name, grid=(S // tm,),
        in_specs=in_specs,
        out_specs=[row, pl.BlockSpec((1, width), lambda i: (0, 0))],
        out_shape=[jax.ShapeDtypeStruct((S, width), out_dtype), jax.ShapeDtypeStruct((1, width), F32)],
        compiler_params=_params(("arbitrary",), 8 * tm * width * 4),
    )(*args)


def _group_ones():
    r = lax.broadcasted_iota(I32, (LANES, LANES), 0) // HEAD64
    c = lax.broadcasted_iota(I32, (LANES, LANES), 1) // HEAD64
    return (r == c).astype(BF16)


def _segmean64(v):
    return _dot2(v, _group_ones()) * (1.0 / HEAD64)


def headnorm_fwd(x, g64, *, col_off, width, name, tm=512):
    S = x.shape[0]
    cb = col_off // LANES
    tm = _tile(S, tm)

    def body(x_ref, g_ref, o_ref):
        v = x_ref[...]
        r = lax.rsqrt(_segmean64(v * v) + NORM_EPS)
        o_ref[...] = (v * r * g_ref[...]).astype(BF16)

    return pl.pallas_call(
        body, name=name, grid=(S // tm, width // LANES),
        in_specs=[pl.BlockSpec((tm, LANES), lambda i, j: (i, cb + j)), pl.BlockSpec((1, LANES), lambda i, j: (0, 0))],
        out_specs=pl.BlockSpec((tm, LANES), lambda i, j: (i, j)),
        out_shape=jax.ShapeDtypeStruct((S, width), BF16),
        compiler_params=_params(("parallel", "parallel"), 8 * tm * LANES * 4),
    )(x, jnp.tile(g64, 2).reshape(1, LANES))


def headnorm_bwd(x, g64, dh, *, col_off, width, name, tm=512):
    S = x.shape[0]
    cb = col_off // LANES
    tm = _tile(S, tm)

    def body(x_ref, g_ref, dh_ref, dx_ref, dg_ref):
        @pl.when((pl.program_id(0) == 0) & (pl.program_id(1) == 0))
        def _():
            dg_ref[...] = jnp.zeros_like(dg_ref)

        v = x_ref[...]
        dhv = dh_ref[...]
        r = lax.rsqrt(_segmean64(v * v) + NORM_EPS)
        y = v * r
        dy = dhv * g_ref[...]
        dx_ref[...] = (r * (dy - y * _segmean64(dy * y))).astype(BF16)
        dg_ref[...] += jnp.sum(dhv * y, axis=0, keepdims=True)

    return pl.pallas_call(
        body, name=name, grid=(width // LANES, S // tm),
        in_specs=[pl.BlockSpec((tm, LANES), lambda j, i: (i, cb + j)), pl.BlockSpec((1, LANES), lambda j, i: (0, 0)),
                  pl.BlockSpec((tm, LANES), lambda j, i: (i, j))],
        out_specs=[pl.BlockSpec((tm, LANES), lambda j, i: (i, j)), pl.BlockSpec((1, LANES), lambda j, i: (0, 0))],
        out_shape=[jax.ShapeDtypeStruct((S, width), BF16), jax.ShapeDtypeStruct((1, LANES), F32)],
        compiler_params=_params(("arbitrary", "arbitrary"), 10 * tm * LANES * 4),
    )(x, jnp.tile(g64, 2).reshape(1, LANES), dh)


def gate_fwd(o, proj, *, gate_off, name, tm=256):
    S, W = o.shape
    cb = gate_off // W
    tm = _tile(S, tm)

    def body(o_ref, g_ref, out_ref):
        g = g_ref[...]
        out_ref[...] = (o_ref[...] * (g * _sigmoid(g))).astype(BF16)

    return pl.pallas_call(
        body, name=name, grid=(S // tm,),
        in_specs=[pl.BlockSpec((tm, W), lambda i: (i, 0)), pl.BlockSpec((tm, W), lambda i: (i, cb))],
        out_specs=pl.BlockSpec((tm, W), lambda i: (i, 0)),
        out_shape=jax.ShapeDtypeStruct((S, W), BF16),
        compiler_params=_params(("parallel",), 6 * tm * W * 4),
    )(o, proj)


def gate_bwd(dog, o, proj, *, gate_off, name, tm=256):
    S, W = o.shape
    cb = gate_off // W
    tm = _tile(S, tm)

    def body(d_ref, o_ref, g_ref, do_ref, dg_ref):
        g = g_ref[...]
        d = d_ref[...]
        s = _sigmoid(g)
        do_ref[...] = d * (g * s)
        dg_ref[...] = (d * o_ref[...] * (s * (1.0 + g * (1.0 - s)))).astype(BF16)

    row = pl.BlockSpec((tm, W), lambda i: (i, 0))
    return pl.pallas_call(
        body, name=name, grid=(S // tm,),
        in_specs=[row, row, pl.BlockSpec((tm, W), lambda i: (i, cb))],
        out_specs=[row, row],
        out_shape=[jax.ShapeDtypeStruct((S, W), F32), jax.ShapeDtypeStruct((S, W), BF16)],
        compiler_params=_params(("parallel",), 10 * tm * W * 4),
    )(dog, o, proj)


def loss_head(y, target, *, name, tm=256):
    S, W = y.shape
    tm = _tile(S, tm)
    n = S // tm

    def body(y_ref, t_ref, dy_ref, l_ref, acc_ref):
        i = pl.program_id(0)

        @pl.when(i == 0)
        def _():
            acc_ref[...] = jnp.zeros_like(acc_ref)

        e = y_ref[...] - t_ref[...]
        dy_ref[...] = e * (1.0 / W)
        acc_ref[...] += jnp.sum(e * e, axis=0, keepdims=True)

        @pl.when(i == n - 1)
        def _():
            l_ref[...] = jnp.full(l_ref.shape, (0.5 / W) * jnp.sum(acc_ref[...]), F32)

    row = pl.BlockSpec((tm, W), lambda i: (i, 0))
    return pl.pallas_call(
        body, name=name, grid=(n,),
        in_specs=[row, row],
        out_specs=[row, pl.BlockSpec((8, LANES), lambda i: (0, 0))],
        out_shape=[jax.ShapeDtypeStruct((S, W), F32), jax.ShapeDtypeStruct((8, LANES), F32)],
        scratch_shapes=[pltpu.VMEM((1, W), F32)],
        compiler_params=_params(("arbitrary",), 8 * tm * W * 4),
    )(y, target)


def _stack_heads(t, zero):
    first = _first_half(t.shape)
    return jnp.concatenate([jnp.where(first, t, zero), jnp.where(first, zero, t)], axis=0)


def _sb_weights(qs, ks, mask, upper, rss):
    zs = [_dot(q, k, NT) for q, k in zip(qs, ks)]
    sps = [jnp.maximum(z, 0.0) + jnp.log(1.0 + jnp.exp(-jnp.abs(z))) for z in zs]
    gs = [z - sp for z, sp in zip(zs, sps)]
    if mask is not None:
        sps = [jnp.where(mask, sp, 0.0) for sp in sps]
    splits = [_split(sp) for sp in sps]
    his = [_dot(hi, upper) for hi, _ in splits]
    los = [_dot(lo, upper) for _, lo in splits]
    avs = [jnp.exp(g - ((hi + lo) + rs)) for g, hi, lo, rs in zip(gs, his, los, rss)]
    if mask is not None:
        avs = [jnp.where(mask, a, 0.0) for a in avs]
    return avs, sps, gs


def _sb_consts():
    row = lax.broadcasted_iota(I32, (BLK, BLK), 0)
    col = lax.broadcasted_iota(I32, (BLK, BLK), 1)
    diag = col < row
    return row, col, jnp.concatenate([diag, diag], axis=0)


SB_W = 1024


def sb_attn_fwd(qkv, *, name, pairs=8):
    S = qkv.shape[0]
    W = SB_W
    PW = pairs * LANES
    ngrp, nq = W // PW, S // BLK

    def body(q_ref, k_ref, v_ref, o_ref):
        i = pl.program_id(1)
        row, col, diag = _sb_consts()
        upper = (row > col).astype(BF16)
        zero = jnp.zeros((BLK, LANES), BF16)
        qs = [_stack_heads(q_ref[:, p * LANES:(p + 1) * LANES] * 0.125, zero) for p in range(pairs)]

        def block(j, carry, mask):
            rows = pl.ds(pl.multiple_of(j * BLK, BLK), BLK)
            cols = [slice(p * LANES, (p + 1) * LANES) for p in range(pairs)]
            avs, sps, _ = _sb_weights(qs, [k_ref[rows, c] for c in cols], mask, upper, [c[1] for c in carry])
            abs_ = [a.astype(BF16) for a in avs]
            outs = [_dot(jnp.concatenate([ab[:BLK], ab[BLK:]], axis=1), _stack_heads(v_ref[rows, c], zero))
                    for ab, c in zip(abs_, cols)]
            return tuple((carry[p][0] + outs[p], carry[p][1] + jnp.sum(sps[p], axis=1, keepdims=True))
                         for p in range(pairs))

        init = tuple((jnp.zeros((BLK, LANES), F32), jnp.zeros((2 * BLK, 1), F32)) for _ in range(pairs))
        carry = block(i, init, diag)
        carry = lax.fori_loop(0, i, lambda jj, c: block(i - 1 - jj, c, None), carry)
        for p in range(pairs):
            o_ref[:, p * LANES:(p + 1) * LANES] = carry[p][0]

    once = pl.Buffered(1)
    return pl.pallas_call(
        body, name=name, grid=(ngrp, nq),
        in_specs=[pl.BlockSpec((BLK, PW), lambda p, i: (i, p)),
                  pl.BlockSpec((S, PW), lambda p, i: (0, ngrp + p), pipeline_mode=once),
                  pl.BlockSpec((S, PW), lambda p, i: (0, 2 * ngrp + p), pipeline_mode=once)],
        out_specs=pl.BlockSpec((BLK, PW), lambda p, i: (i, p)),
        out_shape=jax.ShapeDtypeStruct((S, W), F32),
        compiler_params=_params(("parallel", "arbitrary"), 2 * S * PW * 2 + 16 * BLK * PW * 4),
    )(qkv, qkv, qkv)


def sb_attn_bwd(qkv, o, do, *, name, pairs=4):
    S = qkv.shape[0]
    W = SB_W
    PW = pairs * LANES
    ngrp, nq = W // PW, S // BLK

    def body(q_ref, k_ref, v_ref, o_ref, do_ref, dq_ref, dk_ref, dv_ref):
        i = pl.program_id(1)

        @pl.when(i == 0)
        def _():
            dk_ref[...] = jnp.zeros_like(dk_ref)
            dv_ref[...] = jnp.zeros_like(dv_ref)

        row, col, diag = _sb_consts()
        upper = (row > col).astype(BF16)
        upper_incl = (row >= col).astype(BF16)
        first = _first_half((BLK, LANES))
        zero = jnp.zeros((BLK, LANES), BF16)
        qs, dos, tots = [], [], []
        for p in range(pairs):
            cols = slice(p * LANES, (p + 1) * LANES)
            qs.append(_stack_heads(q_ref[:, cols] * 0.125, zero))
            dob = do_ref[:, cols].astype(BF16)
            dos.append(_stack_heads(dob, zero))
            prod = dob.astype(F32) * o_ref[:, cols]
            tots.append(jnp.concatenate([jnp.sum(jnp.where(first, prod, 0.0), axis=1, keepdims=True),
                                         jnp.sum(jnp.where(first, 0.0, prod), axis=1, keepdims=True)], axis=0))

        def block(j, carry, mask):
            rows = pl.ds(pl.multiple_of(j * BLK, BLK), BLK)
            P = range(pairs)
            cols = [slice(p * LANES, (p + 1) * LANES) for p in P]
            ks = [k_ref[rows, c] for c in cols]
            das = [_dot(dos[p], v_ref[rows, cols[p]], NT) for p in P]
            avs, sps, gs = _sb_weights(qs, ks, mask, upper, [c[1] for c in carry])
            abs_ = [a.astype(BF16) for a in avs]
            es = [ab.astype(F32) * da for ab, da in zip(abs_, das)]
            splits = [_split(e) for e in es]
            his = [_dot(hi, upper_incl) for hi, _ in splits]
            los = [_dot(lo, upper_incl) for _, lo in splits]
            lefts = [tots[p] - ((his[p] + los[p]) + carry[p][2]) for p in P]
            dzs = [es[p] - jnp.exp(gs[p]) * (es[p] + lefts[p]) for p in P]
            if mask is not None:
                dzs = [jnp.where(mask, dz, 0.0) for dz in dzs]
            dzbs = [dz.astype(BF16) for dz in dzs]
            dks = [_dot(dzbs[p], qs[p], TN) for p in P]
            dvs = [_dot(abs_[p], dos[p], TN) for p in P]
            dqs = [_dot(jnp.concatenate([dzbs[p][:BLK], dzbs[p][BLK:]], axis=1), _stack_heads(ks[p], zero)) for p in P]
            for p in P:
                dk_ref[rows, cols[p]] += dks[p]
                dv_ref[rows, cols[p]] += dvs[p]
            return tuple((carry[p][0] + dqs[p], carry[p][1] + jnp.sum(sps[p], axis=1, keepdims=True),
                          carry[p][2] + jnp.sum(es[p], axis=1, keepdims=True)) for p in P)

        col0 = jnp.zeros((2 * BLK, 1), F32)
        init = tuple((jnp.zeros((BLK, LANES), F32), col0, col0) for _ in range(pairs))
        carry = block(i, init, diag)
        carry = lax.fori_loop(0, i, lambda jj, c: block(i - 1 - jj, c, None), carry)
        for p in range(pairs):
            dq_ref[:, p * LANES:(p + 1) * LANES] = (carry[p][0] * 0.125).astype(BF16)

    once = pl.Buffered(1)
    tile = pl.BlockSpec((BLK, PW), lambda p, i: (i, p))
    full = pl.BlockSpec((S, PW), lambda p, i: (0, p), pipeline_mode=once)
    shape = jax.ShapeDtypeStruct((S, W), F32)
    return pl.pallas_call(
        body, name=name, grid=(ngrp, nq),
        in_specs=[tile,
                  pl.BlockSpec((S, PW), lambda p, i: (0, ngrp + p), pipeline_mode=once),
                  pl.BlockSpec((S, PW), lambda p, i: (0, 2 * ngrp + p), pipeline_mode=once),
                  tile, tile],
        out_specs=[tile, full, full],
        out_shape=[jax.ShapeDtypeStruct((S, W), BF16), shape, shape],
        compiler_params=_params(("parallel", "arbitrary"), 2 * S * PW * 2 + 2 * S * PW * 4 + 16 * BLK * PW * 4),
    )(qkv, qkv, qkv, o, do)


FB = 256


def flash_fwd(qn, kn, vb, *, name, heads=2):
    S = qn.shape[0]
    H, DK, DV = MLA_HEADS, MLA_PAD, LANES
    nq, ngrp = S // FB, H // heads
    scale = 1.0 / math.sqrt(MLA_QK)

    def body(q_ref, k_ref, v_ref, o_ref, lse_ref):
        i = pl.program_id(1)
        diag = lax.broadcasted_iota(I32, (FB, FB), 1) <= lax.broadcasted_iota(I32, (FB, FB), 0)
        qs = [q_ref[:, h * DK:(h + 1) * DK] for h in range(heads)]

        def block(j, carry, mask):
            rows = pl.ds(pl.multiple_of(j * FB, FB), FB)
            out = []
            for h in range(heads):
                acc, m, l = carry[h]
                s = _dot(qs[h], k_ref[rows, h * DK:(h + 1) * DK], NT) * scale
                if mask is not None:
                    s = jnp.where(mask, s, -1e30)
                m_new = jnp.maximum(m, jnp.max(s, axis=1, keepdims=True))
                p = jnp.exp(s - m_new)
                w = jnp.exp(m - m_new)
                acc = acc * w + _dot(p.astype(BF16), v_ref[rows, h * DV:(h + 1) * DV])
                out.append((acc, m_new, l * w + jnp.sum(p, axis=1, keepdims=True)))
            return tuple(out)

        init = tuple((jnp.zeros((FB, DV), F32), jnp.full((FB, 1), -1e30, F32), jnp.zeros((FB, 1), F32))
                     for _ in range(heads))
        carry = lax.fori_loop(0, i, lambda j, c: block(j, c, None), init)
        carry = block(i, carry, diag)
        for h in range(heads):
            acc, m, l = carry[h]
            o_ref[:, h * DV:(h + 1) * DV] = acc / l
            lse_ref[h] = m + jnp.log(l)

    return pl.pallas_call(
        body, name=name, grid=(ngrp, nq),
        in_specs=[pl.BlockSpec((FB, heads * DK), lambda g, i: (i, g)),
                  pl.BlockSpec((S, heads * DK), lambda g, i: (0, g)),
                  pl.BlockSpec((S, heads * DV), lambda g, i: (0, g))],
        out_specs=[pl.BlockSpec((FB, heads * DV), lambda g, i: (i, g)), pl.BlockSpec((heads, FB, 1), lambda g, i: (g, i, 0))],
        out_shape=[jax.ShapeDtypeStruct((S, H * DV), F32), jax.ShapeDtypeStruct((H, S, 1), F32)],
        compiler_params=_params(("parallel", "arbitrary"), 2 * S * heads * (DK + DV) * 2 + 16 * FB * FB * 4),
    )(qn, kn, vb)


def flash_bwd(qn, kn, vb, o, lse, do, *, name, heads=2):
    S = qn.shape[0]
    H, DK, DV = MLA_HEADS, MLA_PAD, LANES
    nq, ngrp = S // FB, H // heads
    scale = 1.0 / math.sqrt(MLA_QK)

    def body(q_ref, k_ref, v_ref, o_ref, lse_ref, do_ref, dq_ref, dk_ref, dv_ref):
        i = pl.program_id(1)

        @pl.when(i == 0)
        def _():
            dk_ref[...] = jnp.zeros_like(dk_ref)
            dv_ref[...] = jnp.zeros_like(dv_ref)

        diag = lax.broadcasted_iota(I32, (FB, FB), 1) <= lax.broadcasted_iota(I32, (FB, FB), 0)
        qs, dobs, deltas, lses = [], [], [], []
        for h in range(heads):
            do = do_ref[:, h * DV:(h + 1) * DV]
            qs.append(q_ref[:, h * DK:(h + 1) * DK])
            dobs.append(do.astype(BF16))
            deltas.append(jnp.sum(do * o_ref[:, h * DV:(h + 1) * DV], axis=1, keepdims=True))
            lses.append(lse_ref[h])

        def block(j, carry, mask):
            rows = pl.ds(pl.multiple_of(j * FB, FB), FB)
            out = []
            for h in range(heads):
                kc, vc = slice(h * DK, (h + 1) * DK), slice(h * DV, (h + 1) * DV)
                k = k_ref[rows, kc]
                p = jnp.exp(_dot(qs[h], k, NT) * scale - lses[h])
                if mask is not None:
                    p = jnp.where(mask, p, 0.0)
                ds = (p * (_dot(dobs[h], v_ref[rows, vc], NT) - deltas[h]) * scale).astype(BF16)
                dv_ref[rows, vc] += _dot(p.astype(BF16), dobs[h], TN)
                dk_ref[rows, kc] += _dot(ds, qs[h], TN)
                out.append(carry[h] + _dot(ds, k))
            return tuple(out)

        carry = lax.fori_loop(0, i, lambda j, c: block(j, c, None), tuple(jnp.zeros((FB, DK), F32) for _ in range(heads)))
        carry = block(i, carry, diag)
        for h in range(heads):
            dq_ref[:, h * DK:(h + 1) * DK] = carry[h]

    qtile = pl.BlockSpec((FB, heads * DK), lambda g, i: (i, g))
    otile = pl.BlockSpec((FB, heads * DV), lambda g, i: (i, g))
    kfull = pl.BlockSpec((S, heads * DK), lambda g, i: (0, g))
    vfull = pl.BlockSpec((S, heads * DV), lambda g, i: (0, g))
    return pl.pallas_call(
        body, name=name, grid=(ngrp, nq),
        in_specs=[qtile, kfull, vfull, otile, pl.BlockSpec((heads, FB, 1), lambda g, i: (g, i, 0)), otile],
        out_specs=[qtile, kfull, vfull],
        out_shape=[jax.ShapeDtypeStruct((S, H * DK), F32), jax.ShapeDtypeStruct((S, H * DK), F32),
                   jax.ShapeDtypeStruct((S, H * DV), F32)],
        compiler_params=_params(("parallel", "arbitrary"), S * heads * (DK + DV) * 6 + 16 * FB * FB * 4),
    )(qn, kn, vb, o, lse, do)


def _rope_tables(S):
    half = 32
    inv_freq = ROPE_THETA ** (-jnp.arange(half, dtype=F32) / half)
    ang = jnp.arange(S).astype(F32)[:, None] * inv_freq[None, :]
    cos, sin = jnp.cos(ang), jnp.sin(ang)
    ones, zeros = jnp.ones((S, LANES), F32), jnp.zeros((S, LANES), F32)
    pad = jnp.zeros((S, 64), F32)
    return (jnp.concatenate([ones, cos, cos, pad + 1.0], axis=1),
            jnp.concatenate([zeros, -sin, sin, pad], axis=1))


def _rope_partner(u):
    lane = lax.broadcasted_iota(I32, u.shape, 1)
    return jnp.where((lane % HEAD64) < 32, pltpu.roll(u, LANES - 32, 1), pltpu.roll(u, 32, 1))


def _normrope(raw, g, cos, sgn):
    r = lax.rsqrt(jnp.sum(raw * raw, axis=1, keepdims=True) * (1.0 / MLA_QK) + NORM_EPS)
    y = raw * r
    u = y * g
    pe = u[:, LANES:]
    out = jnp.concatenate([u[:, :LANES], pe * cos[:, LANES:] + _rope_partner(pe) * sgn[:, LANES:]], axis=1)
    return out, y, r


def _normrope_bwd(dout, g, cos, sgn, y, r):
    dpe = dout[:, LANES:]
    du = jnp.concatenate([dout[:, :LANES], dpe * cos[:, LANES:] + _rope_partner(dpe * sgn[:, LANES:])], axis=1)
    dy = du * g
    draw = r * (dy - y * (jnp.sum(dy * y, axis=1, keepdims=True) * (1.0 / MLA_QK)))
    return draw, jnp.sum(du * y, axis=0, keepdims=True)


def mla_prep_fwd(qraw, kv, proj, gq, gk, cos, sgn, *, kpe_off, name, tm=256):
    S = qraw.shape[0]
    tm = _tile(S, tm)
    kb = kpe_off // LANES

    def body(q_ref, kn_ref, v_ref, kpe_ref, gq_ref, gk_ref, c_ref, s_ref, qo_ref, ko_ref, vo_ref):
        cos, sgn = c_ref[...], s_ref[...]
        qo_ref[...] = _normrope(q_ref[...], gq_ref[...], cos, sgn)[0].astype(BF16)
        kraw = jnp.concatenate([kn_ref[...], kpe_ref[...]], axis=1)
        ko_ref[...] = _normrope(kraw, gk_ref[...], cos, sgn)[0].astype(BF16)
        vo_ref[...] = v_ref[...].astype(BF16)

    head = pl.BlockSpec((tm, MLA_PAD), lambda i, h: (i, h))
    gain = pl.BlockSpec((1, MLA_PAD), lambda i, h: (0, 0))
    tab = pl.BlockSpec((tm, MLA_PAD), lambda i, h: (i, 0))
    return pl.pallas_call(
        body, name=name, grid=(S // tm, MLA_HEADS),
        in_specs=[head, pl.BlockSpec((tm, LANES), lambda i, h: (i, 2 * h)), pl.BlockSpec((tm, LANES), lambda i, h: (i, 2 * h + 1)),
                  pl.BlockSpec((tm, LANES), lambda i, h: (i, kb)), gain, gain, tab, tab],
        out_specs=[head, head, pl.BlockSpec((tm, LANES), lambda i, h: (i, h))],
        out_shape=[jax.ShapeDtypeStruct(qraw.shape, BF16), jax.ShapeDtypeStruct(qraw.shape, BF16),
                   jax.ShapeDtypeStruct((S, MLA_HEADS * LANES), BF16)],
        compiler_params=_params(("parallel", "arbitrary"), 16 * tm * MLA_PAD * 4),
    )(qraw, kv, kv, proj, gq, gk, cos, sgn)


def mla_prep_bwd(dqn, dkn, dv, qraw, kv, proj, gq, gk, cos, sgn, *, kpe_off, name, tm=256):
    S = qraw.shape[0]
    tm = _tile(S, tm)
    kb = kpe_off // LANES

    def body(dq_ref, dk_ref, dv_ref, q_ref, kn_ref, kpe_ref, gq_ref, gk_ref, c_ref, s_ref,
             dqo_ref, dkv_ref, dkpe_ref, dgq_ref, dgk_ref, acc_ref):
        i, h = pl.program_id(0), pl.program_id(1)

        @pl.when((i == 0) & (h == 0))
        def _():
            dgq_ref[...] = jnp.zeros_like(dgq_ref)
            dgk_ref[...] = jnp.zeros_like(dgk_ref)

        @pl.when(h == 0)
        def _():
            acc_ref[...] = jnp.zeros_like(acc_ref)

        cos, sgn = c_ref[...], s_ref[...]
        _, yq, rq = _normrope(q_ref[...], gq_ref[...], cos, sgn)
        dq, dgq = _normrope_bwd(dq_ref[...], gq_ref[...], cos, sgn, yq, rq)
        dqo_ref[...] = dq.astype(BF16)
        dgq_ref[...] += dgq
        kraw = jnp.concatenate([kn_ref[...], kpe_ref[...]], axis=1)
        _, yk, rk = _normrope(kraw, gk_ref[...], cos, sgn)
        dk, dgk = _normrope_bwd(dk_ref[...], gk_ref[...], cos, sgn, yk, rk)
        dgk_ref[...] += dgk
        dkv_ref[...] = jnp.concatenate([dk[:, :LANES], dv_ref[...]], axis=1).astype(BF16)
        acc_ref[...] += dk[:, LANES:]

        @pl.when(h == MLA_HEADS - 1)
        def _():
            dkpe_ref[...] = acc_ref[...].astype(BF16)

    head = pl.BlockSpec((tm, MLA_PAD), lambda i, h: (i, h))
    gain = pl.BlockSpec((1, MLA_PAD), lambda i, h: (0, 0))
    tab = pl.BlockSpec((tm, MLA_PAD), lambda i, h: (i, 0))
    return pl.pallas_call(
        body, name=name, grid=(S // tm, MLA_HEADS),
        in_specs=[head, head, pl.BlockSpec((tm, LANES), lambda i, h: (i, h)), head,
                  pl.BlockSpec((tm, LANES), lambda i, h: (i, 2 * h)), pl.BlockSpec((tm, LANES), lambda i, h: (i, kb)),
                  gain, gain, tab, tab],
        out_specs=[head, head, pl.BlockSpec((tm, LANES), lambda i, h: (i, 0)), gain, gain],
        out_shape=[jax.ShapeDtypeStruct(qraw.shape, BF16), jax.ShapeDtypeStruct(qraw.shape, BF16),
                   jax.ShapeDtypeStruct((S, LANES), BF16), jax.ShapeDtypeStruct((1, MLA_PAD), F32),
                   jax.ShapeDtypeStruct((1, MLA_PAD), F32)],
        scratch_shapes=[pltpu.VMEM((tm, LANES), F32)],
        compiler_params=_params(("arbitrary", "arbitrary"), 24 * tm * MLA_PAD * 4),
    )(dqn, dkn, dv, qraw, kv, proj, gq, gk, cos, sgn)


def swa_kv_prep(proj, gk64, *, k_off, v_off, name, tm=512):
    S = proj.shape[0]
    tm = _tile(S, tm)
    W = SWA_KV * HEAD64

    def body(k_ref, v_ref, g_ref, ko_ref, vo_ref):
        first = _first_half((tm, LANES))

        def dup(n):
            nr = pltpu.roll(n, HEAD64, 1)
            return jnp.where(first, n, nr), jnp.where(first, nr, n)

        for t in range(W // LANES):
            x = k_ref[:, t * LANES:(t + 1) * LANES]
            n = x * lax.rsqrt(_segmean64(x * x) + NORM_EPS) * g_ref[...]
            d0, d1 = dup(n)
            ko_ref[:, 2 * t * LANES:(2 * t + 1) * LANES] = d0.astype(BF16)
            ko_ref[:, (2 * t + 1) * LANES:(2 * t + 2) * LANES] = d1.astype(BF16)
            d0, d1 = dup(v_ref[:, t * LANES:(t + 1) * LANES])
            vo_ref[:, 2 * t * LANES:(2 * t + 1) * LANES] = d0.astype(BF16)
            vo_ref[:, (2 * t + 1) * LANES:(2 * t + 2) * LANES] = d1.astype(BF16)

    out = pl.BlockSpec((tm, SWA_KV * LANES), lambda i: (i, 0))
    shape = jax.ShapeDtypeStruct((S, SWA_KV * LANES), BF16)
    return pl.pallas_call(
        body, name=name, grid=(S // tm,),
        in_specs=[pl.BlockSpec((tm, W), lambda i: (i, k_off // W)), pl.BlockSpec((tm, W), lambda i: (i, v_off // W)),
                  pl.BlockSpec((1, LANES), lambda i: (0, 0))],
        out_specs=[out, out], out_shape=[shape, shape],
        compiler_params=_params(("parallel",), 12 * tm * W * 4),
    )(proj, proj, jnp.tile(gk64, 2).reshape(1, LANES))


def swa_kv_prep_bwd(dkdup, dvdup, proj, gk64, *, k_off, name, tm=512):
    S = proj.shape[0]
    tm = _tile(S, tm)
    W = SWA_KV * HEAD64

    def body(dk_ref, dv_ref, k_ref, g_ref, dko_ref, dvo_ref, dg_ref):
        @pl.when(pl.program_id(0) == 0)
        def _():
            dg_ref[...] = jnp.zeros_like(dg_ref)

        first = _first_half((tm, LANES))

        def fold(ref, t):
            d0 = ref[:, 2 * t * LANES:(2 * t + 1) * LANES]
            d1 = ref[:, (2 * t + 1) * LANES:(2 * t + 2) * LANES]
            return jnp.where(first, d0 + pltpu.roll(d0, HEAD64, 1), d1 + pltpu.roll(d1, HEAD64, 1))

        for t in range(W // LANES):
            dvo_ref[:, t * LANES:(t + 1) * LANES] = fold(dv_ref, t).astype(BF16)
            dh = fold(dk_ref, t)
            x = k_ref[:, t * LANES:(t + 1) * LANES]
            r = lax.rsqrt(_segmean64(x * x) + NORM_EPS)
            y = x * r
            dy = dh * g_ref[...]
            dko_ref[:, t * LANES:(t + 1) * LANES] = (r * (dy - y * _segmean64(dy * y))).astype(BF16)
            dg_ref[...] += jnp.sum(dh * y, axis=0, keepdims=True)

    dup = pl.BlockSpec((tm, SWA_KV * LANES), lambda i: (i, 0))
    out = pl.BlockSpec((tm, W), lambda i: (i, 0))
    shape = jax.ShapeDtypeStruct((S, W), BF16)
    return pl.pallas_call(
        body, name=name, grid=(S // tm,),
        in_specs=[dup, dup, pl.BlockSpec((tm, W), lambda i: (i, k_off // W)), pl.BlockSpec((1, LANES), lambda i: (0, 0))],
        out_specs=[out, out, pl.BlockSpec((1, LANES), lambda i: (0, 0))],
        out_shape=[shape, shape, jax.ShapeDtypeStruct((1, LANES), F32)],
        compiler_params=_params(("arbitrary",), 16 * tm * W * 4),
    )(dkdup, dvdup, proj, jnp.tile(gk64, 2).reshape(1, LANES))


def _swa_geometry(i):
    r = lax.broadcasted_iota(I32, (BLK, 2 * BLK), 0)
    c = lax.broadcasted_iota(I32, (BLK, 2 * BLK), 1)
    rel = r + BLK - c
    valid = (rel >= 0) & (rel < BLK) & ((c >= BLK) | (i > 0))
    return valid, rel.astype(F32)


def _swa_slope(h):
    return 2.0 ** (-8.0 * (h + 1) / SWA_HEADS)


def swa_attn_fwd(qn, kdup, vdup, sinks, *, name):
    S = qn.shape[0]
    nq = S // BLK
    group = SWA_HEADS // SWA_KV

    def body(q_ref, kp_ref, kc_ref, vp_ref, vc_ref, sink_ref, o_ref, lse_ref):
        i = pl.program_id(0)
        valid, rel = _swa_geometry(i)
        first = _first_half((BLK, LANES))
        lane = lax.broadcasted_iota(I32, (BLK, LANES), 1)
        lse_all = jnp.zeros((BLK, LANES), F32)
        for g in range(SWA_KV):
            cols = slice(g * LANES, (g + 1) * LANES)
            kk = jnp.concatenate([kp_ref[:, cols], kc_ref[:, cols]], axis=0)
            vv = jnp.concatenate([vp_ref[:, cols], vc_ref[:, cols]], axis=0)
            for pair in range(group // 2):
                tile = (g * group) // 2 + pair
                q2 = q_ref[:, tile * LANES:(tile + 1) * LANES]
                outs = []
                for a in range(2):
                    h = 2 * tile + a
                    qh = jnp.where(first if a == 0 else ~first, q2, jnp.zeros_like(q2))
                    s = _dot(qh, kk, NT) * (1.0 / math.sqrt(HEAD64)) - _swa_slope(h) * rel
                    s = jnp.where(valid, s, -1e30)
                    sink = sink_ref[h]
                    m = jnp.maximum(jnp.max(s, axis=1, keepdims=True), sink)
                    e = jnp.exp(s - m)
                    den = jnp.sum(e, axis=1, keepdims=True) + jnp.exp(sink - m)
                    outs.append(_dot((e / den).astype(BF16), vv))
                    lse_all = jnp.where(lane == h, m + jnp.log(den), lse_all)
                o_ref[:, tile * LANES:(tile + 1) * LANES] = jnp.where(first, outs[0], outs[1])
        lse_ref[...] = lse_all

    prev = lambda i: (jnp.maximum(i - 1, 0), 0)
    cur = lambda i: (i, 0)
    kvw = SWA_KV * LANES
    return pl.pallas_call(
        body, name=name, grid=(nq,),
        in_specs=[pl.BlockSpec((BLK, 1024), cur), pl.BlockSpec((BLK, kvw), prev), pl.BlockSpec((BLK, kvw), cur),
                  pl.BlockSpec((BLK, kvw), prev), pl.BlockSpec((BLK, kvw), cur),
                  pl.BlockSpec(memory_space=pltpu.SMEM)],
        out_specs=[pl.BlockSpec((BLK, 1024), cur), pl.BlockSpec((BLK, LANES), cur)],
        out_shape=[jax.ShapeDtypeStruct((S, 1024), F32), jax.ShapeDtypeStruct((S, LANES), F32)],
        compiler_params=_params(("parallel",), 16 * BLK * 1024 * 4),
    )(qn, kdup, kdup, vdup, vdup, sinks)


def swa_attn_bwd(qn, kdup, vdup, sinks, o, lse, do, *, name):
    S = qn.shape[0]
    nq = S // BLK
    group = SWA_HEADS // SWA_KV
    scale = 1.0 / math.sqrt(HEAD64)

    def body(q_ref, kp_ref, kc_ref, vp_ref, vc_ref, sink_ref, o_ref, lse_ref, do_ref,
             dq_ref, dk_ref, dv_ref, ds_ref):
        i = pl.program_id(0)

        @pl.when(i == 0)
        def _():
            dk_ref[...] = jnp.zeros_like(dk_ref)
            dv_ref[...] = jnp.zeros_like(dv_ref)
            ds_ref[...] = jnp.zeros_like(ds_ref)

        valid, rel = _swa_geometry(i)
        first = _first_half((BLK, LANES))
        lane1 = lax.broadcasted_iota(I32, (1, LANES), 1)
        lane = lax.broadcasted_iota(I32, (BLK, LANES), 1)
        lse_all = lse_ref[...]
        prow = pl.ds(pl.multiple_of(jnp.maximum(i - 1, 0) * BLK, BLK), BLK)
        crow = pl.ds(pl.multiple_of(i * BLK, BLK), BLK)
        dsink = jnp.zeros((1, LANES), F32)
        for g in range(SWA_KV):
            cols = slice(g * LANES, (g + 1) * LANES)
            kk = jnp.concatenate([kp_ref[:, cols], kc_ref[:, cols]], axis=0)
            vv = jnp.concatenate([vp_ref[:, cols], vc_ref[:, cols]], axis=0)
            dkk = jnp.zeros((2 * BLK, LANES), F32)
            dvv = jnp.zeros((2 * BLK, LANES), F32)
            for pair in range(group // 2):
                tile = (g * group) // 2 + pair
                tcols = slice(tile * LANES, (tile + 1) * LANES)
                q2 = q_ref[:, tcols]
                do2 = do_ref[:, tcols]
                prod = do2 * o_ref[:, tcols]
                dqs = []
                for a in range(2):
                    h = 2 * tile + a
                    mine = first if a == 0 else ~first
                    qh = jnp.where(mine, q2, jnp.zeros_like(q2))
                    doh = jnp.where(mine, do2, 0.0).astype(BF16)
                    delta = jnp.sum(jnp.where(mine, prod, 0.0), axis=1, keepdims=True)
                    lse_h = jnp.sum(jnp.where(lane == h, lse_all, 0.0), axis=1, keepdims=True)
                    s = _dot(qh, kk, NT) * scale - _swa_slope(h) * rel
                    p = jnp.where(valid, jnp.exp(s - lse_h), 0.0)
                    dsc = (p * (_dot(doh, vv, NT) - delta) * scale).astype(BF16)
                    dqs.append(_dot(dsc, kk))
                    dkk = dkk + _dot(dsc, qh, TN)
                    dvv = dvv + _dot(p.astype(BF16), doh, TN)
                    psink = jnp.exp(sink_ref[h] - lse_h)
                    dsink = dsink + jnp.where(lane1 == h, -jnp.sum(psink * delta), 0.0)
                dq_ref[:, tcols] = jnp.where(first, dqs[0], dqs[1])
            dk_ref[prow, cols] += dkk[:BLK]
            dv_ref[prow, cols] += dvv[:BLK]
            dk_ref[crow, cols] += dkk[BLK:]
            dv_ref[crow, cols] += dvv[BLK:]
        ds_ref[...] += dsink

    prev = lambda i: (jnp.maximum(i - 1, 0), 0)
    cur = lambda i: (i, 0)
    kvw = SWA_KV * LANES
    whole = pl.BlockSpec((S, kvw), lambda i: (0, 0))
    return pl.pallas_call(
        body, name=name, grid=(nq,),
        in_specs=[pl.BlockSpec((BLK, 1024), cur), pl.BlockSpec((BLK, kvw), prev), pl.BlockSpec((BLK, kvw), cur),
                  pl.BlockSpec((BLK, kvw), prev), pl.BlockSpec((BLK, kvw), cur),
                  pl.BlockSpec(memory_space=pltpu.SMEM),
                  pl.BlockSpec((BLK, 1024), cur), pl.BlockSpec((BLK, LANES), cur), pl.BlockSpec((BLK, 1024), cur)],
        out_specs=[pl.BlockSpec((BLK, 1024), cur), whole, whole, pl.BlockSpec((1, LANES), lambda i: (0, 0))],
        out_shape=[jax.ShapeDtypeStruct((S, 1024), F32), jax.ShapeDtypeStruct((S, kvw), F32),
                   jax.ShapeDtypeStruct((S, kvw), F32), jax.ShapeDtypeStruct((1, LANES), F32)],
        compiler_params=_params(("arbitrary",), 4 * S * kvw * 4 + 24 * BLK * 1024 * 4),
    )(qn, kdup, kdup, vdup, vdup, sinks, o, lse, do)


def _in_bwd(x, h, dproj, dy, g, w_in, tag):
    dh = matmul(dproj, w_in, "nt", name=f"{tag}_dh")
    dw_in = matmul(h, dproj, "tn", name=f"{tag}_dwin")
    dx, dg = rmsnorm_bwd(x, g, dh, col_off=0, width=D_MODEL, out_dtype=F32, res=dy, name=f"{tag}_dnorm")
    return dx, dw_in, dg


def _out_bwd(dy, og, o, proj, w_out, gate_off, tag):
    dog = matmul(dy, w_out, "nt", name=f"{tag}_dog")
    dw_out = matmul(og, dy, "tn", name=f"{tag}_dwout")
    do, dgate = gate_bwd(dog, o, proj, gate_off=gate_off, name=f"{tag}_dgate")
    return do, dgate, dw_out


def sb_fwd(x, p, tag):
    h = rmsnorm_fwd(x, p["norm"], col_off=0, width=D_MODEL, out_dtype=BF16, name=f"{tag}_norm")
    proj, projb = matmul(h, p["w_in"], "nn", out_dtype=(F32, BF16), name=f"{tag}_proj")
    o = sb_attn_fwd(projb, name=f"{tag}_attn")
    og = gate_fwd(o, proj, gate_off=3 * D_MODEL, name=f"{tag}_gate")
    y = matmul(og, p["w_out"], "nn", add=x, name=f"{tag}_out")
    return y, (x, h, proj, projb, o, og)


def sb_bwd(dy, saved, p, tag):
    x, h, proj, projb, o, og = saved
    do, dgate, dw_out = _out_bwd(dy, og, o, proj, p["w_out"], 3 * D_MODEL, tag)
    dq, dk, dv = sb_attn_bwd(projb, o, do, name=f"{tag}_dattn")
    dproj = jnp.concatenate([dq, dk.astype(BF16), dv.astype(BF16), dgate], axis=1)
    dx, dw_in, dg = _in_bwd(x, h, dproj, dy, p["norm"], p["w_in"], tag)
    return dx, {"norm": dg[0], "w_in": dw_in, "w_out": dw_out}


MLA_GATE, MLA_QLAT, MLA_KVLAT, MLA_KPE, MLA_IN = 0, 1024, 1280, 1408, 1536


def mla_fwd(x, p, tabs, tag):
    cos, sgn = tabs
    h = rmsnorm_fwd(x, p["norm"], col_off=0, width=D_MODEL, out_dtype=BF16, name=f"{tag}_norm")
    proj = matmul(h, p["w_in"], "nn", name=f"{tag}_proj")
    ql = rmsnorm_fwd(proj, p["q_a_norm"], col_off=MLA_QLAT, width=256, out_dtype=BF16, name=f"{tag}_qanorm")
    kvl = rmsnorm_fwd(proj, p["kv_a_norm"], col_off=MLA_KVLAT, width=128, out_dtype=BF16, name=f"{tag}_kvanorm")
    qraw = matmul(ql, p["w_uq"], "nn", name=f"{tag}_uq")
    kv = matmul(kvl, p["w_ukv"], "nn", name=f"{tag}_ukv")
    qn, kn, vb = mla_prep_fwd(qraw, kv, proj, p["gq"], p["gk"], cos, sgn, kpe_off=MLA_KPE, name=f"{tag}_prep")
    o, lse = flash_fwd(qn, kn, vb, name=f"{tag}_attn")
    og = gate_fwd(o, proj, gate_off=MLA_GATE, name=f"{tag}_gate")
    y = matmul(og, p["w_out"], "nn", add=x, name=f"{tag}_out")
    return y, (x, h, proj, ql, kvl, qraw, kv, qn, kn, vb, o, lse, og)


def mla_bwd(dy, saved, p, tabs, tag):
    cos, sgn = tabs
    x, h, proj, ql, kvl, qraw, kv, qn, kn, vb, o, lse, og = saved
    do, dgate, dw_out = _out_bwd(dy, og, o, proj, p["w_out"], MLA_GATE, tag)
    dqn, dkn, dv = flash_bwd(qn, kn, vb, o, lse, do, name=f"{tag}_dattn")
    dqraw, dkv, dkpe, dgq, dgk = mla_prep_bwd(dqn, dkn, dv, qraw, kv, proj, p["gq"], p["gk"], cos, sgn,
                                              kpe_off=MLA_KPE, name=f"{tag}_dprep")
    dql = matmul(dqraw, p["w_uq"], "nt", name=f"{tag}_dql")
    dw_uq = matmul(ql, dqraw, "tn", name=f"{tag}_dwuq")
    dkvl = matmul(dkv, p["w_ukv"], "nt", name=f"{tag}_dkvl")
    dw_ukv = matmul(kvl, dkv, "tn", name=f"{tag}_dwukv")
    dqlat, dgqa = rmsnorm_bwd(proj, p["q_a_norm"], dql, col_off=MLA_QLAT, width=256, out_dtype=BF16, name=f"{tag}_dqanorm")
    dkvlat, dgkva = rmsnorm_bwd(proj, p["kv_a_norm"], dkvl, col_off=MLA_KVLAT, width=128, out_dtype=BF16,
                                name=f"{tag}_dkvanorm")
    dproj = jnp.concatenate([dgate, dqlat, dkvlat, dkpe], axis=1)
    dx, dw_in, dg = _in_bwd(x, h, dproj, dy, p["norm"], p["w_in"], tag)
    return dx, {"norm": dg[0], "w_in": dw_in, "q_a_norm": dgqa[0], "w_uq": dw_uq, "kv_a_norm": dgkva[0],
                "w_ukv": dw_ukv, "gq": dgq[0], "gk": dgk[0], "w_out": dw_out}


SWA_Q, SWA_GATE, SWA_K, SWA_V = 0, 1024, 2048, 2304


def swa_fwd(x, p, tag):
    h = rmsnorm_fwd(x, p["norm"], col_off=0, width=D_MODEL, out_dtype=BF16, name=f"{tag}_norm")
    proj = matmul(h, p["w_in"], "nn", name=f"{tag}_proj")
    qn = headnorm_fwd(proj, p["q_head_norm"], col_off=SWA_Q, width=1024, name=f"{tag}_qnorm")
    kdup, vdup = swa_kv_prep(proj, p["k_head_norm"], k_off=SWA_K, v_off=SWA_V, name=f"{tag}_kvprep")
    o, lse = swa_attn_fwd(qn, kdup, vdup, p["sinks"], name=f"{tag}_attn")
    og = gate_fwd(o, proj, gate_off=SWA_GATE, name=f"{tag}_gate")
    y = matmul(og, p["w_out"], "nn", add=x, name=f"{tag}_out")
    return y, (x, h, proj, qn, kdup, vdup, o, lse, og)


def swa_bwd(dy, saved, p, tag):
    x, h, proj, qn, kdup, vdup, o, lse, og = saved
    do, dgate, dw_out = _out_bwd(dy, og, o, proj, p["w_out"], SWA_GATE, tag)
    dqn, dkdup, dvdup, dsinks = swa_attn_bwd(qn, kdup, vdup, p["sinks"], o, lse, do, name=f"{tag}_dattn")
    dq, dgq = headnorm_bwd(proj, p["q_head_norm"], dqn, col_off=SWA_Q, width=1024, name=f"{tag}_dqnorm")
    dk, dv, dgk = swa_kv_prep_bwd(dkdup, dvdup, proj, p["k_head_norm"], k_off=SWA_K, name=f"{tag}_dkvprep")
    dproj = jnp.concatenate([dq, dgate, dk, dv], axis=1)
    dx, dw_in, dg = _in_bwd(x, h, dproj, dy, p["norm"], p["w_in"], tag)
    return dx, {"norm": dg[0], "w_in": dw_in, "q_head_norm": dgq[0, :HEAD64] + dgq[0, HEAD64:],
                "k_head_norm": dgk[0, :HEAD64] + dgk[0, HEAD64:], "sinks": dsinks[0, :SWA_HEADS], "w_out": dw_out}


def prepare_weights(w):
    l1_in, l2_in = w["l1_w_in"], w["l2_w_in"]
    pad64 = lambda v: jnp.pad(v, (0, MLA_PAD - MLA_QK)).reshape(1, MLA_PAD)
    return [
        {"norm": w["l0_norm"], "w_in": w["l0_w_in"], "w_out": w["l0_w_out"]},
        {"norm": w["l1_norm"],
         "w_in": jnp.concatenate([l1_in[:, 448:], l1_in[:, :448], jnp.zeros((D_MODEL, 64), l1_in.dtype)], axis=1),
         "q_a_norm": w["l1_q_a_norm"], "kv_a_norm": w["l1_kv_a_norm"],
         "w_uq": jnp.pad(w["l1_w_uq"].reshape(256, MLA_HEADS, MLA_QK), ((0, 0), (0, 0), (0, MLA_PAD - MLA_QK))
                         ).reshape(256, MLA_HEADS * MLA_PAD),
         "w_ukv": w["l1_w_ukv"], "gq": pad64(w["l1_q_head_norm"]), "gk": pad64(w["l1_k_head_norm"]),
         "w_out": w["l1_w_out"]},
        {"norm": w["l2_norm"],
         "w_in": jnp.concatenate([l2_in[:, :1024], l2_in[:, 1536:], l2_in[:, 1024:1536]], axis=1),
         "q_head_norm": w["l2_q_head_norm"], "k_head_norm": w["l2_k_head_norm"], "sinks": w["l2_sinks"],
         "w_out": w["l2_w_out"]},
        {"norm": w["l3_norm"], "w_in": w["l3_w_in"], "w_out": w["l3_w_out"]},
    ]


def unprepare_grads(gs):
    g0, g1, g2, g3 = gs
    d1, d2 = g1["w_in"], g2["w_in"]
    return {
        "l0_norm": g0["norm"], "l0_w_in": g0["w_in"], "l0_w_out": g0["w_out"],
        "l1_norm": g1["norm"], "l1_w_in": jnp.concatenate([d1[:, 1024:1472], d1[:, :1024]], axis=1),
        "l1_q_a_norm": g1["q_a_norm"],
        "l1_w_uq": g1["w_uq"].reshape(256, MLA_HEADS, MLA_PAD)[:, :, :MLA_QK].reshape(256, MLA_HEADS * MLA_QK),
        "l1_kv_a_norm": g1["kv_a_norm"], "l1_w_ukv": g1["w_ukv"],
        "l1_q_head_norm": g1["gq"][:MLA_QK], "l1_k_head_norm": g1["gk"][:MLA_QK], "l1_w_out": g1["w_out"],
        "l2_norm": g2["norm"], "l2_w_in": jnp.concatenate([d2[:, :1024], d2[:, 2048:], d2[:, 1024:2048]], axis=1),
        "l2_q_head_norm": g2["q_head_norm"], "l2_k_head_norm": g2["k_head_norm"], "l2_sinks": g2["sinks"],
        "l2_w_out": g2["w_out"],
        "l3_norm": g3["norm"], "l3_w_in": g3["w_in"], "l3_w_out": g3["w_out"],
    }


def local_step(x, target, w):
    ps = prepare_weights(w)
    tabs = _rope_tables(x.shape[0])
    y0, s0 = sb_fwd(x, ps[0], "l0")
    y1, s1 = mla_fwd(y0, ps[1], tabs, "l1")
    y2, s2 = swa_fwd(y1, ps[2], "l2")
    y3, s3 = sb_fwd(y2, ps[3], "l3")
    dy, loss = loss_head(y3, target, name="loss")
    d3, g3 = sb_bwd(dy, s3, ps[3], "l3")
    d2, g2 = swa_bwd(d3, s2, ps[2], "l2")
    d1, g1 = mla_bwd(d2, s1, ps[1], tabs, "l1")
    d0, g0 = sb_bwd(d1, s0, ps[0], "l0")
    return loss, d0, unprepare_grads([g0, g1, g2, g3])


MATS = (("l0_w_in", "col", 1024, 4096), ("l0_w_out", "row", 1024, 1024), ("l1_w_in", "col", 1024, 1472),
        ("l1_w_uq", "col", 256, 1536), ("l1_w_ukv", "col", 128, 2048), ("l1_w_out", "row", 1024, 1024),
        ("l2_w_in", "col", 1024, 2560), ("l2_w_out", "row", 1024, 1024), ("l3_w_in", "col", 1024, 4096),
        ("l3_w_out", "row", 1024, 1024))
N_CHIPS = 4
PACK_W = 1024
HALF_ROWS = 2176
PACK_ROWS = 2 * HALF_ROWS
VECS = (("l0_norm", 0, 0, 1024), ("l1_norm", 1, 0, 1024), ("l2_norm", 2, 0, 1024), ("l3_norm", 3, 0, 1024),
        ("l1_q_a_norm", 4, 0, 256), ("l1_kv_a_norm", 4, 256, 128), ("l1_q_head_norm", 4, 384, 192),
        ("l1_k_head_norm", 4, 576, 192), ("l2_q_head_norm", 4, 768, 64), ("l2_k_head_norm", 4, 832, 64),
        ("l2_sinks", 4, 896, 16))
LOSS_SLOT = (4, 912)
VEC_ROWS = 8


def _shard_rows(k, n):
    return k * n // N_CHIPS // PACK_W


def pack_shards(shards):
    parts = [shards[name].reshape(-1, PACK_W) for name, _, _, _ in MATS]
    used = sum(p.shape[0] for p in parts)
    return jnp.concatenate(parts + [jnp.zeros((PACK_ROWS - used, PACK_W), parts[0].dtype)], axis=0)


def unpack_shards(flat):
    out, r0 = {}, 0
    for name, kind, k, n in MATS:
        rows = _shard_rows(k, n)
        shape = (k, n // N_CHIPS) if kind == "col" else (k // N_CHIPS, n)
        out[name] = flat[r0:r0 + rows].reshape(shape)
        r0 += rows
    return out


def pack_full(mats):
    parts = []
    for name, kind, k, n in MATS:
        m = mats[name]
        if kind == "col":
            m = m.reshape(k, N_CHIPS, n // N_CHIPS).transpose(1, 0, 2)
        parts.append(m.reshape(N_CHIPS, -1, PACK_W))
    used = sum(p.shape[1] for p in parts)
    return jnp.concatenate(parts + [jnp.zeros((N_CHIPS, PACK_ROWS - used, PACK_W), parts[0].dtype)], axis=1)


def unpack_full(stacked):
    out, r0 = {}, 0
    for name, kind, k, n in MATS:
        rows = _shard_rows(k, n)
        seg = stacked[:, r0:r0 + rows]
        if kind == "col":
            out[name] = seg.reshape(N_CHIPS, k, n // N_CHIPS).transpose(1, 0, 2).reshape(k, n)
        else:
            out[name] = seg.reshape(k, n)
        r0 += rows
    return out


def pack_vecs(vecs, loss=None):
    rows = []
    for r in range(VEC_ROWS):
        items = [(off, vecs[name]) for name, rr, off, _ in VECS if rr == r]
        if loss is not None and r == LOSS_SLOT[0]:
            items.append((LOSS_SLOT[1], loss.reshape(1)))
        pos, parts = 0, []
        for off, v in sorted(items, key=lambda t: t[0]):
            assert off == pos
            parts.append(v.astype(F32))
            pos += v.shape[0]
        parts.append(jnp.zeros((PACK_W - pos,), F32))
        rows.append(jnp.concatenate(parts))
    return jnp.stack(rows)


def unpack_vecs(block):
    return {name: block[r, off:off + n] for name, r, off, n in VECS}


def _me():
    return lax.axis_index("x"), lax.axis_index("y"), lax.axis_index("c")


OTHER_CHIPS = ((1, 0), (0, 1), (1, 1))


def _remote(src, dst, send_sem, recv_sem, to):
    return pltpu.make_async_remote_copy(src_ref=src, dst_ref=dst, send_sem=send_sem, recv_sem=recv_sem,
                                        device_id=to, device_id_type=MESH)


def gather_weights(block):
    def body(in_ref, out_ref, send_sems, recv_sems, local_sem):
        x, y, c = _me()
        sibling = (x, y, 1 - c)

        def half(px, py, pc):
            return out_ref.at[2 * px + py, pl.ds(pc * HALF_ROWS, HALF_ROWS), :]

        mine = pltpu.make_async_copy(in_ref, out_ref.at[2 * x + y], local_sem)
        mine.start()
        chips = [(x ^ dx, y ^ dy) for dx, dy in OTHER_CHIPS]
        first = [_remote(in_ref.at[pl.ds(c * HALF_ROWS, HALF_ROWS), :], half(x, y, c), send_sems.at[j], recv_sems.at[j],
                         (*chip, c)) for j, chip in enumerate(chips)]
        for cp in first:
            cp.start()
        passed = [_remote(half(*chip, c), half(*chip, c), send_sems.at[3 + j], recv_sems.at[3 + j], sibling)
                  for j, chip in enumerate(chips)]
        for j, chip in enumerate(chips):
            _remote(half(*chip, c), half(*chip, c), send_sems.at[j], recv_sems.at[j], (*chip, c)).wait_recv()
            passed[j].start()
        for j, chip in enumerate(chips):
            _remote(half(*chip, 1 - c), half(*chip, 1 - c), send_sems.at[3 + j], recv_sems.at[3 + j], sibling).wait_recv()
        for cp in first + passed:
            cp.wait_send()
        mine.wait()

    hbm = pl.BlockSpec(memory_space=pltpu.HBM)
    return pl.pallas_call(
        body, name="gather_weights",
        out_shape=jax.ShapeDtypeStruct((N_CHIPS, PACK_ROWS, PACK_W), block.dtype),
        in_specs=[hbm], out_specs=hbm,
        scratch_shapes=[pltpu.SemaphoreType.DMA((6,)), pltpu.SemaphoreType.DMA((6,)), pltpu.SemaphoreType.DMA],
    )(block)


def pair_exchange(g):
    def body(g_ref, out_ref, send_sems, recv_sems):
        x, y, c = _me()
        sibling = (x, y, 1 - c)
        copies = [_remote(g_ref.at[k, pl.ds((1 - c) * HALF_ROWS, HALF_ROWS), :], out_ref.at[k], send_sems.at[k],
                          recv_sems.at[k], sibling) for k in range(N_CHIPS)]
        for cp in copies:
            cp.start()
        for cp in copies:
            cp.wait_recv()
        for cp in copies:
            cp.wait_send()

    hbm = pl.BlockSpec(memory_space=pltpu.HBM)
    return pl.pallas_call(
        body, name="pair_exchange",
        out_shape=jax.ShapeDtypeStruct((N_CHIPS, HALF_ROWS, PACK_W), g.dtype),
        in_specs=[hbm], out_specs=hbm,
        scratch_shapes=[pltpu.SemaphoreType.DMA((N_CHIPS,)), pltpu.SemaphoreType.DMA((N_CHIPS,))],
    )(g)


def chip_exchange(part):
    def body(p_ref, out_ref, send_sems, recv_sems):
        x, y, c = _me()
        copies = [_remote(p_ref.at[2 * (x ^ dx) + (y ^ dy)], out_ref.at[j], send_sems.at[j], recv_sems.at[j],
                          (x ^ dx, y ^ dy, c)) for j, (dx, dy) in enumerate(OTHER_CHIPS)]
        for cp in copies:
            cp.start()
        for cp in copies:
            cp.wait_recv()
        for cp in copies:
            cp.wait_send()

    hbm = pl.BlockSpec(memory_space=pltpu.HBM)
    return pl.pallas_call(
        body, name="chip_exchange",
        out_shape=jax.ShapeDtypeStruct((len(OTHER_CHIPS), HALF_ROWS, PACK_W), part.dtype),
        in_specs=[hbm], out_specs=hbm,
        scratch_shapes=[pltpu.SemaphoreType.DMA((3,)), pltpu.SemaphoreType.DMA((3,))],
    )(part)


def join_halves(half):
    def body(h_ref, out_ref, send_sem, recv_sem, local_sem):
        x, y, c = _me()
        mine = pltpu.make_async_copy(h_ref, out_ref.at[pl.ds(c * HALF_ROWS, HALF_ROWS), :], local_sem)
        mine.start()
        cp = _remote(h_ref, out_ref.at[pl.ds(c * HALF_ROWS, HALF_ROWS), :], send_sem, recv_sem, (x, y, 1 - c))
        cp.start()
        _remote(h_ref, out_ref.at[pl.ds((1 - c) * HALF_ROWS, HALF_ROWS), :], send_sem, recv_sem, (x, y, 1 - c)).wait_recv()
        cp.wait_send()
        mine.wait()

    hbm = pl.BlockSpec(memory_space=pltpu.HBM)
    return pl.pallas_call(
        body, name="join_halves",
        out_shape=jax.ShapeDtypeStruct((PACK_ROWS, PACK_W), half.dtype),
        in_specs=[hbm], out_specs=hbm,
        scratch_shapes=[pltpu.SemaphoreType.DMA, pltpu.SemaphoreType.DMA, pltpu.SemaphoreType.DMA],
    )(half)


def sum_over_devices(block):
    def body(in_ref, out_ref, all_ref, send_sems, recv_sems):
        x, y, c = _me()
        me = 4 * x + 2 * y + c
        all_ref[me] = in_ref[...]
        copies = []
        for r in range(1, 8):
            to = (x ^ (r >> 2), y ^ ((r >> 1) & 1), c ^ (r & 1))
            copies.append(_remote(in_ref, all_ref.at[me], send_sems.at[r - 1], recv_sems.at[r - 1], to))
        for cp in copies:
            cp.start()
        for r in range(1, 8):
            frm = (x ^ (r >> 2), y ^ ((r >> 1) & 1), c ^ (r & 1))
            _remote(in_ref, all_ref.at[4 * frm[0] + 2 * frm[1] + frm[2]], send_sems.at[r - 1], recv_sems.at[r - 1],
                    frm).wait_recv()
        for cp in copies:
            cp.wait_send()
        acc = all_ref[0]
        for d in range(1, 8):
            acc = acc + all_ref[d]
        out_ref[...] = acc

    vmem = pl.BlockSpec(memory_space=pltpu.VMEM)
    return pl.pallas_call(
        body, name="sum_over_devices",
        out_shape=jax.ShapeDtypeStruct(block.shape, F32),
        in_specs=[vmem], out_specs=vmem,
        scratch_shapes=[pltpu.VMEM((8,) + block.shape, F32), pltpu.SemaphoreType.DMA((7,)), pltpu.SemaphoreType.DMA((7,))],
    )(block)


SUM_ROWS = 272


def pair_sum(g, got, core):
    steps = HALF_ROWS // SUM_ROWS

    def body(s_ref, g_ref, r_ref, o_ref):
        o_ref[...] = g_ref[...] + r_ref[...]

    blk = pl.BlockSpec((1, SUM_ROWS, PACK_W), lambda k, i, s: (k, i, 0))
    return pl.pallas_call(
        body, name="pair_sum",
        grid_spec=pltpu.PrefetchScalarGridSpec(
            num_scalar_prefetch=1, grid=(N_CHIPS, steps),
            in_specs=[pl.BlockSpec((1, SUM_ROWS, PACK_W), lambda k, i, s: (k, s[0] * steps + i, 0)), blk],
            out_specs=blk),
        out_shape=jax.ShapeDtypeStruct((N_CHIPS, HALF_ROWS, PACK_W), F32),
        compiler_params=_params(("parallel", "parallel"), 6 * SUM_ROWS * PACK_W * 4),
    )(core, g, got)


def chip_sum(part, got, chip):
    steps = HALF_ROWS // SUM_ROWS

    def body(s_ref, p_ref, a_ref, b_ref, c_ref, o_ref):
        o_ref[...] = ((p_ref[0] + a_ref[0]) + b_ref[0]) + c_ref[0]

    def got_spec(j):
        return pl.BlockSpec((1, SUM_ROWS, PACK_W), lambda i, s: (j, i, 0))

    return pl.pallas_call(
        body, name="chip_sum",
        grid_spec=pltpu.PrefetchScalarGridSpec(
            num_scalar_prefetch=1, grid=(steps,),
            in_specs=[pl.BlockSpec((1, SUM_ROWS, PACK_W), lambda i, s: (s[0], i, 0)), got_spec(0), got_spec(1), got_spec(2)],
            out_specs=pl.BlockSpec((SUM_ROWS, PACK_W), lambda i, s: (i, 0))),
        out_shape=jax.ShapeDtypeStruct((HALF_ROWS, PACK_W), F32),
        compiler_params=_params(("parallel",), 10 * SUM_ROWS * PACK_W * 4),
    )(chip, part, got, got, got)


def reduce_scatter(g):
    x, y, c = _me()
    part = pair_sum(g, pair_exchange(g), jnp.reshape(c, (1,)).astype(I32))
    half = chip_sum(part, chip_exchange(part), jnp.reshape(2 * x + y, (1,)).astype(I32))
    return join_halves(half)


def adamw(w, g, m, v, *, name):
    rows, cols = w.shape
    tm = _tile(rows, 256)
    c1 = 1.0 - ADAM_B1 ** ADAM_STEP
    c2 = 1.0 - ADAM_B2 ** ADAM_STEP

    def body(w_ref, g_ref, m_ref, v_ref, d_ref, mo_ref, vo_ref):
        gv = g_ref[...]
        mn = ADAM_B1 * m_ref[...] + (1.0 - ADAM_B1) * gv
        vn = ADAM_B2 * v_ref[...] + (1.0 - ADAM_B2) * (gv * gv)
        d_ref[...] = -ADAM_LR * ((mn / c1) / (jnp.sqrt(vn / c2) + ADAM_EPS) + ADAM_WD * w_ref[...])
        mo_ref[...] = mn
        vo_ref[...] = vn

    blk = pl.BlockSpec((tm, cols), lambda i: (i, 0))
    shape = jax.ShapeDtypeStruct(w.shape, F32)
    return pl.pallas_call(
        body, name=name, grid=(rows // tm,),
        in_specs=[blk] * 4, out_specs=[blk] * 3, out_shape=[shape] * 3,
        compiler_params=_params(("parallel",), 16 * tm * cols * 4),
    )(w, g, m, v)


WEIGHTS = ("l0_norm", "l0_w_in", "l0_w_out", "l1_norm", "l1_w_in", "l1_q_a_norm", "l1_w_uq", "l1_kv_a_norm", "l1_w_ukv",
           "l1_q_head_norm", "l1_k_head_norm", "l1_w_out", "l2_norm", "l2_w_in", "l2_q_head_norm", "l2_k_head_norm",
           "l2_sinks", "l2_w_out", "l3_norm", "l3_w_in", "l3_w_out")


def kernel(x, l0_norm, l0_w_in, l0_w_out, l1_norm, l1_w_in, l1_q_a_norm, l1_w_uq, l1_kv_a_norm, l1_w_ukv, l1_q_head_norm, l1_k_head_norm, l1_w_out, l2_norm, l2_w_in, l2_q_head_norm, l2_k_head_norm, l2_sinks, l2_w_out, l3_norm, l3_w_in, l3_w_out, loss_target, m_l0_norm, m_l0_w_in, m_l0_w_out, m_l1_norm, m_l1_w_in, m_l1_q_a_norm, m_l1_w_uq, m_l1_kv_a_norm, m_l1_w_ukv, m_l1_q_head_norm, m_l1_k_head_norm, m_l1_w_out, m_l2_norm, m_l2_w_in, m_l2_q_head_norm, m_l2_k_head_norm, m_l2_sinks, m_l2_w_out, m_l3_norm, m_l3_w_in, m_l3_w_out, v_l0_norm, v_l0_w_in, v_l0_w_out, v_l1_norm, v_l1_w_in, v_l1_q_a_norm, v_l1_w_uq, v_l1_kv_a_norm, v_l1_w_ukv, v_l1_q_head_norm, v_l1_k_head_norm, v_l1_w_out, v_l2_norm, v_l2_w_in, v_l2_q_head_norm, v_l2_k_head_norm, v_l2_sinks, v_l2_w_out, v_l3_norm, v_l3_w_in, v_l3_w_out):
    given = dict(locals())
    w = {n: given[n] for n in WEIGHTS}
    m = {n: given["m_" + n] for n in WEIGHTS}
    v = {n: given["v_" + n] for n in WEIGHTS}
    mat_names = [t[0] for t in MATS]
    vec_names = [t[0] for t in VECS]

    gathered = gather_weights(pack_shards({n: w[n] for n in mat_names}).astype(BF16))
    full = unpack_full(gathered)
    full.update({n: w[n] for n in vec_names})
    loss_tile, grad_x, grads = local_step(x[0], loss_target[0], full)

    block = reduce_scatter(pack_full({n: grads[n] for n in mat_names}))
    g = unpack_shards(block)
    vec_sum = sum_over_devices(pack_vecs({n: grads[n] for n in vec_names}, loss=loss_tile[0, 0]))
    loss = vec_sum[LOSS_SLOT[0], LOSS_SLOT[1]]

    delta, new_m, new_v = {}, {}, {}
    for n in mat_names:
        delta[n], new_m[n], new_v[n] = adamw(w[n], g[n], m[n], v[n], name=f"adamw_{n}")
    dv, mv, vv = adamw(pack_vecs(w), vec_sum, pack_vecs(m), pack_vecs(v), name="adamw_vecs")
    g.update(unpack_vecs(vec_sum))
    delta.update(unpack_vecs(dv))
    new_m.update(unpack_vecs(mv))
    new_v.update(unpack_vecs(vv))
    return (loss, grad_x[None], *[g[n] for n in WEIGHTS], *[delta[n] for n in WEIGHTS],
            *[new_m[n] for n in WEIGHTS], *[new_v[n] for n in WEIGHTS])
```

```python
import math

import jax
import jax.numpy as jnp
from jax import lax
from jax.experimental import pallas as pl
from jax.experimental.pallas import tpu as pltpu

F32 = jnp.float32
BF16 = jnp.bfloat16
I32 = jnp.int32
MESH = pl.DeviceIdType.MESH

NORM_EPS = 1e-6
D_MODEL = 1024
HEAD64 = 64
LANES = 128
BLK = 128
MLA_HEADS = 8
MLA_QK = 192
MLA_PAD = 256
ROPE_THETA = 10000.0
SWA_HEADS = 16
SWA_KV = 4
VMEM_CAP = 56 * 1024 * 1024

ADAM_LR, ADAM_B1, ADAM_B2, ADAM_EPS, ADAM_WD, ADAM_STEP = 0.001, 0.9, 0.999, 1e-08, 0.01, 10

NT = (((1,), (1,)), ((), ()))
NN = (((1,), (0,)), ((), ()))
TN = (((0,), (0,)), ((), ()))


def _dot(a, b, dims=NN):
    return lax.dot_general(a, b, dims, preferred_element_type=F32)


def _tile(n, pref):
    for t in (pref, 512, 256, 128):
        if t <= pref and n % t == 0:
            return t
    return n


def _params(sem, vmem_bytes):
    limit = int(min(max(2 * vmem_bytes, 24 * 1024 * 1024), VMEM_CAP))
    return pltpu.CompilerParams(dimension_semantics=sem, vmem_limit_bytes=limit)


def _pallas(*args, **kwargs):
    call = pl.pallas_call(*args, **kwargs)

    def run(*operands):
        return call(*[pltpu.with_memory_space_constraint(a, pltpu.HBM) if a.ndim >= 2 else a for a in operands])

    return run


def _split(v):
    hi = v.astype(BF16)
    return hi, (v - hi.astype(F32)).astype(BF16)


def _dot2(v, m):
    hi, lo = _split(v)
    return _dot(hi, m) + _dot(lo, m)


def _first_half(shape):
    return lax.broadcasted_iota(I32, shape, 1) < HEAD64


def _sigmoid(g):
    return 1.0 / (1.0 + jnp.exp(-g))


def matmul(a, b, mode, *, name, out_dtype=F32, add=None, tm=512, tn=1024, tk=512):
    if mode == "nn":
        (M, K), (K2, N) = a.shape, b.shape
    elif mode == "nt":
        (M, K), (N, K2) = a.shape, b.shape
    else:
        (K, M), (K2, N) = a.shape, b.shape
    assert K == K2, (a.shape, b.shape, mode)
    tm, tn, tk = _tile(M, tm), _tile(N, tn), _tile(K, tk)
    nk = K // tk
    dims = {"nn": NN, "nt": NT, "tn": TN}[mode]

    n_in = 2 if add is None else 3

    def body(*refs):
        a_ref, b_ref = refs[:2]
        out_refs, acc_ref = refs[n_in:-1], refs[-1]
        k = pl.program_id(2)

        @pl.when(k == 0)
        def _():
            acc_ref[...] = jnp.zeros_like(acc_ref)

        acc_ref[...] += _dot(a_ref[...].astype(BF16), b_ref[...].astype(BF16), dims)

        @pl.when(k == nk - 1)
        def _():
            r = acc_ref[...]
            if add is not None:
                r = r + refs[2][...]
            for o_ref in out_refs:
                o_ref[...] = r.astype(o_ref.dtype)

    a_spec = (pl.BlockSpec((tk, tm), lambda i, j, k: (k, i)) if mode == "tn"
              else pl.BlockSpec((tm, tk), lambda i, j, k: (i, k)))
    b_spec = (pl.BlockSpec((tn, tk), lambda i, j, k: (j, k)) if mode == "nt"
              else pl.BlockSpec((tk, tn), lambda i, j, k: (k, j)))
    o_spec = pl.BlockSpec((tm, tn), lambda i, j, k: (i, j))
    in_specs, args = [a_spec, b_spec], [a, b]
    if add is not None:
        in_specs.append(o_spec)
        args.append(add)
    vm = 2 * (tm * tk * a.dtype.itemsize + tk * tn * b.dtype.itemsize) + 5 * tm * tn * 4
    out_dtypes = list(out_dtype) if isinstance(out_dtype, (tuple, list)) else [out_dtype]
    res = _pallas(
        body, name=name, grid=(M // tm, N // tn, nk),
        in_specs=in_specs, out_specs=[o_spec] * len(out_dtypes),
        out_shape=[jax.ShapeDtypeStruct((M, N), d) for d in out_dtypes],
        scratch_shapes=[pltpu.VMEM((tm, tn), F32)],
        compiler_params=_params(("parallel", "parallel", "arbitrary"), vm),
    )(*args)
    return res[0] if len(out_dtypes) == 1 else res


def rmsnorm_fwd(x, g, *, col_off, width, out_dtype, name, tm=256):
    S = x.shape[0]
    assert col_off % width == 0
    cb = col_off // width
    tm = _tile(S, tm)

    def body(x_ref, g_ref, o_ref):
        v = x_ref[...]
        r = lax.rsqrt(jnp.mean(v * v, axis=1, keepdims=True) + NORM_EPS)
        o_ref[...] = (v * r * g_ref[...]).astype(out_dtype)

    return _pallas(
        body, name=name, grid=(S // tm,),
        in_specs=[pl.BlockSpec((tm, width), lambda i: (i, cb)), pl.BlockSpec((1, width), lambda i: (0, 0))],
        out_specs=pl.BlockSpec((tm, width), lambda i: (i, 0)),
        out_shape=jax.ShapeDtypeStruct((S, width), out_dtype),
        compiler_params=_params(("parallel",), 4 * tm * width * 4),
    )(x, g.reshape(1, width))


def rmsnorm_bwd(x, g, dh, *, col_off, width, out_dtype, name, res=None, tm=256):
    S = x.shape[0]
    cb = col_off // width
    tm = _tile(S, tm)

    def body(*refs):
        if res is None:
            x_ref, g_ref, dh_ref, dx_ref, dg_ref = refs
        else:
            x_ref, g_ref, dh_ref, res_ref, dx_ref, dg_ref = refs

        @pl.when(pl.program_id(0) == 0)
        def _():
            dg_ref[...] = jnp.zeros_like(dg_ref)

        v = x_ref[...]
        dhv = dh_ref[...].astype(F32)
        r = lax.rsqrt(jnp.mean(v * v, axis=1, keepdims=True) + NORM_EPS)
        y = v * r
        dy = dhv * g_ref[...]
        dx = r * (dy - y * jnp.mean(dy * y, axis=1, keepdims=True))
        if res is not None:
            dx = dx + res_ref[...]
        dx_ref[...] = dx.astype(out_dtype)
        dg_ref[...] += jnp.sum(dhv * y, axis=0, keepdims=True)

    row = pl.BlockSpec((tm, width), lambda i: (i, 0))
    in_specs = [pl.BlockSpec((tm, width), lambda i: (i, cb)), pl.BlockSpec((1, width), lambda i: (0, 0)), row]
    args = [x, g.reshape(1, width), dh]
    if res is not None:
        in_specs.append(row)
        args.append(res)
    return _pallas(
        body, name=name, grid=(S // tm,),
        in_specs=in_specs,
        out_specs=[row, pl.BlockSpec((1, width), lambda i: (0, 0))],
        out_shape=[jax.ShapeDtypeStruct((S, width), out_dtype), jax.ShapeDtypeStruct((1, width), F32)],
        compiler_params=_params(("arbitrary",), 8 * tm * width * 4),
    )(*args)


def _group_ones():
    r = lax.broadcasted_iota(I32, (LANES, LANES), 0) // HEAD64
    c = lax.broadcasted_iota(I32, (LANES, LANES), 1) // HEAD64
    return (r == c).astype(BF16)


def _segmean64(v):
    return _dot2(v, _group_ones()) * (1.0 / HEAD64)


def headnorm_fwd(x, g64, *, col_off, width, name, tm=512):
    S = x.shape[0]
    cb = col_off // LANES
    tm = _tile(S, tm)

    def body(x_ref, g_ref, o_ref):
        v = x_ref[...]
        r = lax.rsqrt(_segmean64(v * v) + NORM_EPS)
        o_ref[...] = (v * r * g_ref[...]).astype(BF16)

    return _pallas(
        body, name=name, grid=(S // tm, width // LANES),
        in_specs=[pl.BlockSpec((tm, LANES), lambda i, j: (i, cb + j)), pl.BlockSpec((1, LANES), lambda i, j: (0, 0))],
        out_specs=pl.BlockSpec((tm, LANES), lambda i, j: (i, j)),
        out_shape=jax.ShapeDtypeStruct((S, width), BF16),
        compiler_params=_params(("parallel", "parallel"), 8 * tm * LANES * 4),
    )(x, jnp.tile(g64, 2).reshape(1, LANES))


def headnorm_bwd(x, g64, dh, *, col_off, width, name, tm=512):
    S = x.shape[0]
    cb = col_off // LANES
    tm = _tile(S, tm)

    def body(x_ref, g_ref, dh_ref, dx_ref, dg_ref):
        @pl.when((pl.program_id(0) == 0) & (pl.program_id(1) == 0))
        def _():
            dg_ref[...] = jnp.zeros_like(dg_ref)

        v = x_ref[...]
        dhv = dh_ref[...]
        r = lax.rsqrt(_segmean64(v * v) + NORM_EPS)
        y = v * r
        dy = dhv * g_ref[...]
        dx_ref[...] = (r * (dy - y * _segmean64(dy * y))).astype(BF16)
        dg_ref[...] += jnp.sum(dhv * y, axis=0, keepdims=True)

    return _pallas(
        body, name=name, grid=(width // LANES, S // tm),
        in_specs=[pl.BlockSpec((tm, LANES), lambda j, i: (i, cb + j)), pl.BlockSpec((1, LANES), lambda j, i: (0, 0)),
                  pl.BlockSpec((tm, LANES), lambda j, i: (i, j))],
        out_specs=[pl.BlockSpec((tm, LANES), lambda j, i: (i, j)), pl.BlockSpec((1, LANES), lambda j, i: (0, 0))],
        out_shape=[jax.ShapeDtypeStruct((S, width), BF16), jax.ShapeDtypeStruct((1, LANES), F32)],
        compiler_params=_params(("arbitrary", "arbitrary"), 10 * tm * LANES * 4),
    )(x, jnp.tile(g64, 2).reshape(1, LANES), dh)


def gate_fwd(o, proj, *, gate_off, name, tm=256):
    S, W = o.shape
    cb = gate_off // W
    tm = _tile(S, tm)

    def body(o_ref, g_ref, out_ref):
        g = g_ref[...]
        out_ref[...] = (o_ref[...] * (g * _sigmoid(g))).astype(BF16)

    return _pallas(
        body, name=name, grid=(S // tm,),
        in_specs=[pl.BlockSpec((tm, W), lambda i: (i, 0)), pl.BlockSpec((tm, W), lambda i: (i, cb))],
        out_specs=pl.BlockSpec((tm, W), lambda i: (i, 0)),
        out_shape=jax.ShapeDtypeStruct((S, W), BF16),
        compiler_params=_params(("parallel",), 6 * tm * W * 4),
    )(o, proj)


def gate_bwd(dog, o, proj, *, gate_off, name, tm=256):
    S, W = o.shape
    cb = gate_off // W
    tm = _tile(S, tm)

    def body(d_ref, o_ref, g_ref, do_ref, dg_ref):
        g = g_ref[...]
        d = d_ref[...]
        s = _sigmoid(g)
        do_ref[...] = d * (g * s)
        dg_ref[...] = (d * o_ref[...] * (s * (1.0 + g * (1.0 - s)))).astype(BF16)

    row = pl.BlockSpec((tm, W), lambda i: (i, 0))
    return _pallas(
        body, name=name, grid=(S // tm,),
        in_specs=[row, row, pl.BlockSpec((tm, W), lambda i: (i, cb))],
        out_specs=[row, row],
        out_shape=[jax.ShapeDtypeStruct((S, W), F32), jax.ShapeDtypeStruct((S, W), BF16)],
        compiler_params=_params(("parallel",), 10 * tm * W * 4),
    )(dog, o, proj)


def loss_head(y, target, *, name, tm=256):
    S, W = y.shape
    tm = _tile(S, tm)
    n = S // tm

    def body(y_ref, t_ref, dy_ref, l_ref, acc_ref):
        i = pl.program_id(0)

        @pl.when(i == 0)
        def _():
            acc_ref[...] = jnp.zeros_like(acc_ref)

        e = y_ref[...] - t_ref[...]
        dy_ref[...] = e * (1.0 / W)
        acc_ref[...] += jnp.sum(e * e, axis=0, keepdims=True)

        @pl.when(i == n - 1)
        def _():
            l_ref[...] = jnp.full(l_ref.shape, (0.5 / W) * jnp.sum(acc_ref[...]), F32)

    row = pl.BlockSpec((tm, W), lambda i: (i, 0))
    return _pallas(
        body, name=name, grid=(n,),
        in_specs=[row, row],
        out_specs=[row, pl.BlockSpec((8, LANES), lambda i: (0, 0))],
        out_shape=[jax.ShapeDtypeStruct((S, W), F32), jax.ShapeDtypeStruct((8, LANES), F32)],
        scratch_shapes=[pltpu.VMEM((1, W), F32)],
        compiler_params=_params(("arbitrary",), 8 * tm * W * 4),
    )(y, target)


def _stack_heads(t, zero):
    first = _first_half(t.shape)
    return jnp.concatenate([jnp.where(first, t, zero), jnp.where(first, zero, t)], axis=0)


def _sb_weights(qs, ks, mask, upper, rss):
    zs = [_dot(q, k, NT) for q, k in zip(qs, ks)]
    sps = [jnp.maximum(z, 0.0) + jnp.log(1.0 + jnp.exp(-jnp.abs(z))) for z in zs]
    gs = [z - sp for z, sp in zip(zs, sps)]
    if mask is not None:
        sps = [jnp.where(mask, sp, 0.0) for sp in sps]
    splits = [_split(sp) for sp in sps]
    his = [_dot(hi, upper) for hi, _ in splits]
    los = [_dot(lo, upper) for _, lo in splits]
    avs = [jnp.exp(g - ((hi + lo) + rs)) for g, hi, lo, rs in zip(gs, his, los, rss)]
    if mask is not None:
        avs = [jnp.where(mask, a, 0.0) for a in avs]
    return avs, sps, gs


def _sb_consts():
    row = lax.broadcasted_iota(I32, (BLK, BLK), 0)
    col = lax.broadcasted_iota(I32, (BLK, BLK), 1)
    diag = col < row
    return row, col, jnp.concatenate([diag, diag], axis=0)


SB_DEAD = 105.0


def _sb_walk_left(block, i, carry):
    def least(c):
        m = c[0][1]
        for pair in c[1:]:
            m = jnp.minimum(m, pair[1])
        return jnp.min(m)

    def cond(state):
        jj, _, low = state
        return (jj < i) & (low < SB_DEAD)

    def body(state):
        jj, c, _ = state
        c = block(i - 1 - jj, c, None)
        return jj + 1, c, least(c)

    return lax.while_loop(cond, body, (jnp.int32(0), carry, least(carry)))[1]


SB_W = 1024


def sb_attn_fwd(qkv, *, name, pairs=8):
    S = qkv.shape[0]
    W = SB_W
    PW = pairs * LANES
    ngrp, nq = W // PW, S // BLK

    def body(q_ref, k_ref, v_ref, o_ref):
        i = pl.program_id(1)
        row, col, diag = _sb_consts()
        upper = (row > col).astype(BF16)
        zero = jnp.zeros((BLK, LANES), BF16)
        qs = [_stack_heads(q_ref[:, p * LANES:(p + 1) * LANES] * 0.125, zero) for p in range(pairs)]

        def block(j, carry, mask):
            rows = pl.ds(pl.multiple_of(j * BLK, BLK), BLK)
            cols = [slice(p * LANES, (p + 1) * LANES) for p in range(pairs)]
            avs, sps, _ = _sb_weights(qs, [k_ref[rows, c] for c in cols], mask, upper, [c[1] for c in carry])
            abs_ = [a.astype(BF16) for a in avs]
            outs = [_dot(jnp.concatenate([ab[:BLK], ab[BLK:]], axis=1), _stack_heads(v_ref[rows, c], zero))
                    for ab, c in zip(abs_, cols)]
            return tuple((carry[p][0] + outs[p], carry[p][1] + jnp.sum(sps[p], axis=1, keepdims=True))
                         for p in range(pairs))

        init = tuple((jnp.zeros((BLK, LANES), F32), jnp.zeros((2 * BLK, 1), F32)) for _ in range(pairs))
        carry = _sb_walk_left(block, i, block(i, init, diag))
        for p in range(pairs):
            o_ref[:, p * LANES:(p + 1) * LANES] = carry[p][0]

    once = pl.Buffered(1)
    return _pallas(
        body, name=name, grid=(ngrp, nq),
        in_specs=[pl.BlockSpec((BLK, PW), lambda p, i: (i, p)),
                  pl.BlockSpec((S, PW), lambda p, i: (0, ngrp + p), pipeline_mode=once),
                  pl.BlockSpec((S, PW), lambda p, i: (0, 2 * ngrp + p), pipeline_mode=once)],
        out_specs=pl.BlockSpec((BLK, PW), lambda p, i: (i, p)),
        out_shape=jax.ShapeDtypeStruct((S, W), F32),
        compiler_params=_params(("parallel", "arbitrary"), 2 * S * PW * 2 + 16 * BLK * PW * 4),
    )(qkv, qkv, qkv)


def sb_attn_bwd(qkv, o, do, *, name, pairs=4):
    S = qkv.shape[0]
    W = SB_W
    PW = pairs * LANES
    ngrp, nq = W // PW, S // BLK

    def body(q_ref, k_ref, v_ref, o_ref, do_ref, dq_ref, dk_ref, dv_ref):
        i = pl.program_id(1)

        @pl.when(i == 0)
        def _():
            dk_ref[...] = jnp.zeros_like(dk_ref)
            dv_ref[...] = jnp.zeros_like(dv_ref)

        row, col, diag = _sb_consts()
        upper = (row > col).astype(BF16)
        upper_incl = (row >= col).astype(BF16)
        first = _first_half((BLK, LANES))
        zero = jnp.zeros((BLK, LANES), BF16)
        qs, dos, tots = [], [], []
        for p in range(pairs):
            cols = slice(p * LANES, (p + 1) * LANES)
            qs.append(_stack_heads(q_ref[:, cols] * 0.125, zero))
            dob = do_ref[:, cols].astype(BF16)
            dos.append(_stack_heads(dob, zero))
            prod = dob.astype(F32) * o_ref[:, cols]
            tots.append(jnp.concatenate([jnp.sum(jnp.where(first, prod, 0.0), axis=1, keepdims=True),
                                         jnp.sum(jnp.where(first, 0.0, prod), axis=1, keepdims=True)], axis=0))

        def block(j, carry, mask):
            rows = pl.ds(pl.multiple_of(j * BLK, BLK), BLK)
            P = range(pairs)
            cols = [slice(p * LANES, (p + 1) * LANES) for p in P]
            ks = [k_ref[rows, c] for c in cols]
            das = [_dot(dos[p], v_ref[rows, cols[p]], NT) for p in P]
            avs, sps, gs = _sb_weights(qs, ks, mask, upper, [c[1] for c in carry])
            abs_ = [a.astype(BF16) for a in avs]
            es = [ab.astype(F32) * da for ab, da in zip(abs_, das)]
            splits = [_split(e) for e in es]
            his = [_dot(hi, upper_incl) for hi, _ in splits]
            los = [_dot(lo, upper_incl) for _, lo in splits]
            lefts = [tots[p] - ((his[p] + los[p]) + carry[p][2]) for p in P]
            dzs = [es[p] - jnp.exp(gs[p]) * (es[p] + lefts[p]) for p in P]
            if mask is not None:
                dzs = [jnp.where(mask, dz, 0.0) for dz in dzs]
            dzbs = [dz.astype(BF16) for dz in dzs]
            dks = [_dot(dzbs[p], qs[p], TN) for p in P]
            dvs = [_dot(abs_[p], dos[p], TN) for p in P]
            dqs = [_dot(jnp.concatenate([dzbs[p][:BLK], dzbs[p][BLK:]], axis=1), _stack_heads(ks[p], zero)) for p in P]
            for p in P:
                dk_ref[rows, cols[p]] += dks[p]
                dv_ref[rows, cols[p]] += dvs[p]
            return tuple((carry[p][0] + dqs[p], carry[p][1] + jnp.sum(sps[p], axis=1, keepdims=True),
                          carry[p][2] + jnp.sum(es[p], axis=1, keepdims=True)) for p in P)

        col0 = jnp.zeros((2 * BLK, 1), F32)
        init = tuple((jnp.zeros((BLK, LANES), F32), col0, col0) for _ in range(pairs))
        carry = _sb_walk_left(block, i, block(i, init, diag))
        for p in range(pairs):
            dq_ref[:, p * LANES:(p + 1) * LANES] = (carry[p][0] * 0.125).astype(BF16)

    once = pl.Buffered(1)
    tile = pl.BlockSpec((BLK, PW), lambda p, i: (i, p))
    full = pl.BlockSpec((S, PW), lambda p, i: (0, p), pipeline_mode=once)
    shape = jax.ShapeDtypeStruct((S, W), F32)
    return _pallas(
        body, name=name, grid=(ngrp, nq),
        in_specs=[tile,
                  pl.BlockSpec((S, PW), lambda p, i: (0, ngrp + p), pipeline_mode=once),
                  pl.BlockSpec((S, PW), lambda p, i: (0, 2 * ngrp + p), pipeline_mode=once),
                  tile, tile],
        out_specs=[tile, full, full],
        out_shape=[jax.ShapeDtypeStruct((S, W), BF16), shape, shape],
        compiler_params=_params(("parallel", "arbitrary"), 2 * S * PW * 2 + 2 * S * PW * 4 + 16 * BLK * PW * 4),
    )(qkv, qkv, qkv, o, do)


FB = 256


def flash_fwd(qn, kn, vb, *, name, heads=2):
    S = qn.shape[0]
    H, DK, DV = MLA_HEADS, MLA_PAD, LANES
    nq, ngrp = S // FB, H // heads
    scale = 1.0 / math.sqrt(MLA_QK)

    def body(q_ref, k_ref, v_ref, o_ref, lse_ref):
        i = pl.program_id(1)
        diag = lax.broadcasted_iota(I32, (FB, FB), 1) <= lax.broadcasted_iota(I32, (FB, FB), 0)
        qs = [q_ref[:, h * DK:(h + 1) * DK] for h in range(heads)]

        def block(j, carry, mask):
            rows = pl.ds(pl.multiple_of(j * FB, FB), FB)
            out = []
            for h in range(heads):
                acc, m, l = carry[h]
                s = _dot(qs[h], k_ref[rows, h * DK:(h + 1) * DK], NT) * scale
                if mask is not None:
                    s = jnp.where(mask, s, -1e30)
                m_new = jnp.maximum(m, jnp.max(s, axis=1, keepdims=True))
                p = jnp.exp(s - m_new)
                w = jnp.exp(m - m_new)
                acc = acc * w + _dot(p.astype(BF16), v_ref[rows, h * DV:(h + 1) * DV])
                out.append((acc, m_new, l * w + jnp.sum(p, axis=1, keepdims=True)))
            return tuple(out)

        init = tuple((jnp.zeros((FB, DV), F32), jnp.full((FB, 1), -1e30, F32), jnp.zeros((FB, 1), F32))
                     for _ in range(heads))
        carry = lax.fori_loop(0, i, lambda j, c: block(j, c, None), init)
        carry = block(i, carry, diag)
        for h in range(heads):
            acc, m, l = carry[h]
            o_ref[:, h * DV:(h + 1) * DV] = acc / l
            lse_ref[h] = m + jnp.log(l)

    return _pallas(
        body, name=name, grid=(ngrp, nq),
        in_specs=[pl.BlockSpec((FB, heads * DK), lambda g, i: (i, g)),
                  pl.BlockSpec((S, heads * DK), lambda g, i: (0, g)),
                  pl.BlockSpec((S, heads * DV), lambda g, i: (0, g))],
        out_specs=[pl.BlockSpec((FB, heads * DV), lambda g, i: (i, g)), pl.BlockSpec((heads, FB, 1), lambda g, i: (g, i, 0))],
        out_shape=[jax.ShapeDtypeStruct((S, H * DV), F32), jax.ShapeDtypeStruct((H, S, 1), F32)],
        compiler_params=_params(("parallel", "arbitrary"), 2 * S * heads * (DK + DV) * 2 + 16 * FB * FB * 4),
    )(qn, kn, vb)


def flash_bwd(qn, kn, vb, o, lse, do, *, name, heads=2):
    S = qn.shape[0]
    H, DK, DV = MLA_HEADS, MLA_PAD, LANES
    nq, ngrp = S // FB, H // heads
    scale = 1.0 / math.sqrt(MLA_QK)

    def body(q_ref, k_ref, v_ref, o_ref, lse_ref, do_ref, dq_ref, dk_ref, dv_ref):
        i = pl.program_id(1)

        @pl.when(i == 0)
        def _():
            dk_ref[...] = jnp.zeros_like(dk_ref)
            dv_ref[...] = jnp.zeros_like(dv_ref)

        diag = lax.broadcasted_iota(I32, (FB, FB), 1) <= lax.broadcasted_iota(I32, (FB, FB), 0)
        qs, dobs, deltas, lses = [], [], [], []
        for h in range(heads):
            do = do_ref[:, h * DV:(h + 1) * DV]
            qs.append(q_ref[:, h * DK:(h + 1) * DK])
            dobs.append(do.astype(BF16))
            deltas.append(jnp.sum(do * o_ref[:, h * DV:(h + 1) * DV], axis=1, keepdims=True))
            lses.append(lse_ref[h])

        def block(j, carry, mask):
            rows = pl.ds(pl.multiple_of(j * FB, FB), FB)
            out = []
            for h in range(heads):
                kc, vc = slice(h * DK, (h + 1) * DK), slice(h * DV, (h + 1) * DV)
                k = k_ref[rows, kc]
                p = jnp.exp(_dot(qs[h], k, NT) * scale - lses[h])
                if mask is not None:
                    p = jnp.where(mask, p, 0.0)
                ds = (p * (_dot(dobs[h], v_ref[rows, vc], NT) - deltas[h]) * scale).astype(BF16)
                dv_ref[rows, vc] += _dot(p.astype(BF16), dobs[h], TN)
                dk_ref[rows, kc] += _dot(ds, qs[h], TN)
                out.append(carry[h] + _dot(ds, k))
            return tuple(out)

        carry = lax.fori_loop(0, i, lambda j, c: block(j, c, None), tuple(jnp.zeros((FB, DK), F32) for _ in range(heads)))
        carry = block(i, carry, diag)
        for h in range(heads):
            dq_ref[:, h * DK:(h + 1) * DK] = carry[h]

    qtile = pl.BlockSpec((FB, heads * DK), lambda g, i: (i, g))
    otile = pl.BlockSpec((FB, heads * DV), lambda g, i: (i, g))
    kfull = pl.BlockSpec((S, heads * DK), lambda g, i: (0, g))
    vfull = pl.BlockSpec((S, heads * DV), lambda g, i: (0, g))
    return _pallas(
        body, name=name, grid=(ngrp, nq),
        in_specs=[qtile, kfull, vfull, otile, pl.BlockSpec((heads, FB, 1), lambda g, i: (g, i, 0)), otile],
        out_specs=[qtile, kfull, vfull],
        out_shape=[jax.ShapeDtypeStruct((S, H * DK), F32), jax.ShapeDtypeStruct((S, H * DK), F32),
                   jax.ShapeDtypeStruct((S, H * DV), F32)],
        compiler_params=_params(("parallel", "arbitrary"), S * heads * (DK + DV) * 6 + 16 * FB * FB * 4),
    )(qn, kn, vb, o, lse, do)


def _rope_tables(S):
    half = 32
    inv_freq = ROPE_THETA ** (-jnp.arange(half, dtype=F32) / half)
    ang = jnp.arange(S).astype(F32)[:, None] * inv_freq[None, :]
    cos, sin = jnp.cos(ang), jnp.sin(ang)
    ones, zeros = jnp.ones((S, LANES), F32), jnp.zeros((S, LANES), F32)
    pad = jnp.zeros((S, 64), F32)
    return (jnp.concatenate([ones, cos, cos, pad + 1.0], axis=1),
            jnp.concatenate([zeros, -sin, sin, pad], axis=1))


def _rope_partner(u):
    lane = lax.broadcasted_iota(I32, u.shape, 1)
    return jnp.where((lane % HEAD64) < 32, pltpu.roll(u, LANES - 32, 1), pltpu.roll(u, 32, 1))


def _normrope(raw, g, cos, sgn):
    r = lax.rsqrt(jnp.sum(raw * raw, axis=1, keepdims=True) * (1.0 / MLA_QK) + NORM_EPS)
    y = raw * r
    u = y * g
    pe = u[:, LANES:]
    out = jnp.concatenate([u[:, :LANES], pe * cos[:, LANES:] + _rope_partner(pe) * sgn[:, LANES:]], axis=1)
    return out, y, r


def _normrope_bwd(dout, g, cos, sgn, y, r):
    dpe = dout[:, LANES:]
    du = jnp.concatenate([dout[:, :LANES], dpe * cos[:, LANES:] + _rope_partner(dpe * sgn[:, LANES:])], axis=1)
    dy = du * g
    draw = r * (dy - y * (jnp.sum(dy * y, axis=1, keepdims=True) * (1.0 / MLA_QK)))
    return draw, jnp.sum(du * y, axis=0, keepdims=True)


def mla_prep_fwd(qraw, kv, proj, gq, gk, cos, sgn, *, kpe_off, name, tm=256):
    S = qraw.shape[0]
    tm = _tile(S, tm)
    kb = kpe_off // LANES

    def body(q_ref, kn_ref, v_ref, kpe_ref, gq_ref, gk_ref, c_ref, s_ref, qo_ref, ko_ref, vo_ref):
        cos, sgn = c_ref[...], s_ref[...]
        qo_ref[...] = _normrope(q_ref[...], gq_ref[...], cos, sgn)[0].astype(BF16)
        kraw = jnp.concatenate([kn_ref[...], kpe_ref[...]], axis=1)
        ko_ref[...] = _normrope(kraw, gk_ref[...], cos, sgn)[0].astype(BF16)
        vo_ref[...] = v_ref[...].astype(BF16)

    head = pl.BlockSpec((tm, MLA_PAD), lambda i, h: (i, h))
    gain = pl.BlockSpec((1, MLA_PAD), lambda i, h: (0, 0))
    tab = pl.BlockSpec((tm, MLA_PAD), lambda i, h: (i, 0))
    return _pallas(
        body, name=name, grid=(S // tm, MLA_HEADS),
        in_specs=[head, pl.BlockSpec((tm, LANES), lambda i, h: (i, 2 * h)), pl.BlockSpec((tm, LANES), lambda i, h: (i, 2 * h + 1)),
                  pl.BlockSpec((tm, LANES), lambda i, h: (i, kb)), gain, gain, tab, tab],
        out_specs=[head, head, pl.BlockSpec((tm, LANES), lambda i, h: (i, h))],
        out_shape=[jax.ShapeDtypeStruct(qraw.shape, BF16), jax.ShapeDtypeStruct(qraw.shape, BF16),
                   jax.ShapeDtypeStruct((S, MLA_HEADS * LANES), BF16)],
        compiler_params=_params(("parallel", "arbitrary"), 16 * tm * MLA_PAD * 4),
    )(qraw, kv, kv, proj, gq, gk, cos, sgn)


def mla_prep_bwd(dqn, dkn, dv, qraw, kv, proj, gq, gk, cos, sgn, *, kpe_off, name, tm=256):
    S = qraw.shape[0]
    tm = _tile(S, tm)
    kb = kpe_off // LANES

    def body(dq_ref, dk_ref, dv_ref, q_ref, kn_ref, kpe_ref, gq_ref, gk_ref, c_ref, s_ref,
             dqo_ref, dkv_ref, dkpe_ref, dgq_ref, dgk_ref, acc_ref):
        i, h = pl.program_id(0), pl.program_id(1)

        @pl.when((i == 0) & (h == 0))
        def _():
            dgq_ref[...] = jnp.zeros_like(dgq_ref)
            dgk_ref[...] = jnp.zeros_like(dgk_ref)

        @pl.when(h == 0)
        def _():
            acc_ref[...] = jnp.zeros_like(acc_ref)

        cos, sgn = c_ref[...], s_ref[...]
        _, yq, rq = _normrope(q_ref[...], gq_ref[...], cos, sgn)
        dq, dgq = _normrope_bwd(dq_ref[...], gq_ref[...], cos, sgn, yq, rq)
        dqo_ref[...] = dq.astype(BF16)
        dgq_ref[...] += dgq
        kraw = jnp.concatenate([kn_ref[...], kpe_ref[...]], axis=1)
        _, yk, rk = _normrope(kraw, gk_ref[...], cos, sgn)
        dk, dgk = _normrope_bwd(dk_ref[...], gk_ref[...], cos, sgn, yk, rk)
        dgk_ref[...] += dgk
        dkv_ref[...] = jnp.concatenate([dk[:, :LANES], dv_ref[...]], axis=1).astype(BF16)
        acc_ref[...] += dk[:, LANES:]

        @pl.when(h == MLA_HEADS - 1)
        def _():
            dkpe_ref[...] = acc_ref[...].astype(BF16)

    head = pl.BlockSpec((tm, MLA_PAD), lambda i, h: (i, h))
    gain = pl.BlockSpec((1, MLA_PAD), lambda i, h: (0, 0))
    tab = pl.BlockSpec((tm, MLA_PAD), lambda i, h: (i, 0))
    return _pallas(
        body, name=name, grid=(S // tm, MLA_HEADS),
        in_specs=[head, head, pl.BlockSpec((tm, LANES), lambda i, h: (i, h)), head,
                  pl.BlockSpec((tm, LANES), lambda i, h: (i, 2 * h)), pl.BlockSpec((tm, LANES), lambda i, h: (i, kb)),
                  gain, gain, tab, tab],
        out_specs=[head, head, pl.BlockSpec((tm, LANES), lambda i, h: (i, 0)), gain, gain],
        out_shape=[jax.ShapeDtypeStruct(qraw.shape, BF16), jax.ShapeDtypeStruct(qraw.shape, BF16),
                   jax.ShapeDtypeStruct((S, LANES), BF16), jax.ShapeDtypeStruct((1, MLA_PAD), F32),
                   jax.ShapeDtypeStruct((1, MLA_PAD), F32)],
        scratch_shapes=[pltpu.VMEM((tm, LANES), F32)],
        compiler_params=_params(("arbitrary", "arbitrary"), 24 * tm * MLA_PAD * 4),
    )(dqn, dkn, dv, qraw, kv, proj, gq, gk, cos, sgn)


def swa_kv_prep(proj, gk64, *, k_off, v_off, name, tm=512):
    S = proj.shape[0]
    tm = _tile(S, tm)
    W = SWA_KV * HEAD64

    def body(k_ref, v_ref, g_ref, ko_ref, vo_ref):
        first = _first_half((tm, LANES))

        def dup(n):
            nr = pltpu.roll(n, HEAD64, 1)
            return jnp.where(first, n, nr), jnp.where(first, nr, n)

        for t in range(W // LANES):
            x = k_ref[:, t * LANES:(t + 1) * LANES]
            n = x * lax.rsqrt(_segmean64(x * x) + NORM_EPS) * g_ref[...]
            d0, d1 = dup(n)
            ko_ref[:, 2 * t * LANES:(2 * t + 1) * LANES] = d0.astype(BF16)
            ko_ref[:, (2 * t + 1) * LANES:(2 * t + 2) * LANES] = d1.astype(BF16)
            d0, d1 = dup(v_ref[:, t * LANES:(t + 1) * LANES])
            vo_ref[:, 2 * t * LANES:(2 * t + 1) * LANES] = d0.astype(BF16)
            vo_ref[:, (2 * t + 1) * LANES:(2 * t + 2) * LANES] = d1.astype(BF16)

    out = pl.BlockSpec((tm, SWA_KV * LANES), lambda i: (i, 0))
    shape = jax.ShapeDtypeStruct((S, SWA_KV * LANES), BF16)
    return _pallas(
        body, name=name, grid=(S // tm,),
        in_specs=[pl.BlockSpec((tm, W), lambda i: (i, k_off // W)), pl.BlockSpec((tm, W), lambda i: (i, v_off // W)),
                  pl.BlockSpec((1, LANES), lambda i: (0, 0))],
        out_specs=[out, out], out_shape=[shape, shape],
        compiler_params=_params(("parallel",), 12 * tm * W * 4),
    )(proj, proj, jnp.tile(gk64, 2).reshape(1, LANES))


def swa_kv_prep_bwd(dkdup, dvdup, proj, gk64, *, k_off, name, tm=512):
    S = proj.shape[0]
    tm = _tile(S, tm)
    W = SWA_KV * HEAD64

    def body(dk_ref, dv_ref, k_ref, g_ref, dko_ref, dvo_ref, dg_ref):
        @pl.when(pl.program_id(0) == 0)
        def _():
            dg_ref[...] = jnp.zeros_like(dg_ref)

        first = _first_half((tm, LANES))

        def fold(ref, t):
            d0 = ref[:, 2 * t * LANES:(2 * t + 1) * LANES]
            d1 = ref[:, (2 * t + 1) * LANES:(2 * t + 2) * LANES]
            return jnp.where(first, d0 + pltpu.roll(d0, HEAD64, 1), d1 + pltpu.roll(d1, HEAD64, 1))

        for t in range(W // LANES):
            dvo_ref[:, t * LANES:(t + 1) * LANES] = fold(dv_ref, t).astype(BF16)
            dh = fold(dk_ref, t)
            x = k_ref[:, t * LANES:(t + 1) * LANES]
            r = lax.rsqrt(_segmean64(x * x) + NORM_EPS)
            y = x * r
            dy = dh * g_ref[...]
            dko_ref[:, t * LANES:(t + 1) * LANES] = (r * (dy - y * _segmean64(dy * y))).astype(BF16)
            dg_ref[...] += jnp.sum(dh * y, axis=0, keepdims=True)

    dup = pl.BlockSpec((tm, SWA_KV * LANES), lambda i: (i, 0))
    out = pl.BlockSpec((tm, W), lambda i: (i, 0))
    shape = jax.ShapeDtypeStruct((S, W), BF16)
    return _pallas(
        body, name=name, grid=(S // tm,),
        in_specs=[dup, dup, pl.BlockSpec((tm, W), lambda i: (i, k_off // W)), pl.BlockSpec((1, LANES), lambda i: (0, 0))],
        out_specs=[out, out, pl.BlockSpec((1, LANES), lambda i: (0, 0))],
        out_shape=[shape, shape, jax.ShapeDtypeStruct((1, LANES), F32)],
        compiler_params=_params(("arbitrary",), 16 * tm * W * 4),
    )(dkdup, dvdup, proj, jnp.tile(gk64, 2).reshape(1, LANES))


def _swa_geometry(i):
    r = lax.broadcasted_iota(I32, (BLK, 2 * BLK), 0)
    c = lax.broadcasted_iota(I32, (BLK, 2 * BLK), 1)
    rel = r + BLK - c
    valid = (rel >= 0) & (rel < BLK) & ((c >= BLK) | (i > 0))
    return valid, rel.astype(F32)


def _swa_slope(h):
    return 2.0 ** (-8.0 * (h + 1) / SWA_HEADS)


def swa_attn_fwd(qn, kdup, vdup, sinks, *, name):
    S = qn.shape[0]
    nq = S // BLK
    group = SWA_HEADS // SWA_KV

    def body(q_ref, kp_ref, kc_ref, vp_ref, vc_ref, sink_ref, o_ref, lse_ref):
        i = pl.program_id(0)
        valid, rel = _swa_geometry(i)
        first = _first_half((BLK, LANES))
        lane = lax.broadcasted_iota(I32, (BLK, LANES), 1)
        lse_all = jnp.zeros((BLK, LANES), F32)
        for g in range(SWA_KV):
            cols = slice(g * LANES, (g + 1) * LANES)
            kk = jnp.concatenate([kp_ref[:, cols], kc_ref[:, cols]], axis=0)
            vv = jnp.concatenate([vp_ref[:, cols], vc_ref[:, cols]], axis=0)
            for pair in range(group // 2):
                tile = (g * group) // 2 + pair
                q2 = q_ref[:, tile * LANES:(tile + 1) * LANES]
                outs = []
                for a in range(2):
                    h = 2 * tile + a
                    qh = jnp.where(first if a == 0 else ~first, q2, jnp.zeros_like(q2))
                    s = _dot(qh, kk, NT) * (1.0 / math.sqrt(HEAD64)) - _swa_slope(h) * rel
                    s = jnp.where(valid, s, -1e30)
                    sink = sink_ref[h]
                    m = jnp.maximum(jnp.max(s, axis=1, keepdims=True), sink)
                    e = jnp.exp(s - m)
                    den = jnp.sum(e, axis=1, keepdims=True) + jnp.exp(sink - m)
                    outs.append(_dot((e / den).astype(BF16), vv))
                    lse_all = jnp.where(lane == h, m + jnp.log(den), lse_all)
                o_ref[:, tile * LANES:(tile + 1) * LANES] = jnp.where(first, outs[0], outs[1])
        lse_ref[...] = lse_all

    prev = lambda i: (jnp.maximum(i - 1, 0), 0)
    cur = lambda i: (i, 0)
    kvw = SWA_KV * LANES
    return _pallas(
        body, name=name, grid=(nq,),
        in_specs=[pl.BlockSpec((BLK, 1024), cur), pl.BlockSpec((BLK, kvw), prev), pl.BlockSpec((BLK, kvw), cur),
                  pl.BlockSpec((BLK, kvw), prev), pl.BlockSpec((BLK, kvw), cur),
                  pl.BlockSpec(memory_space=pltpu.SMEM)],
        out_specs=[pl.BlockSpec((BLK, 1024), cur), pl.BlockSpec((BLK, LANES), cur)],
        out_shape=[jax.ShapeDtypeStruct((S, 1024), F32), jax.ShapeDtypeStruct((S, LANES), F32)],
        compiler_params=_params(("parallel",), 16 * BLK * 1024 * 4),
    )(qn, kdup, kdup, vdup, vdup, sinks)


def swa_attn_bwd(qn, kdup, vdup, sinks, o, lse, do, *, name):
    S = qn.shape[0]
    nq = S // BLK
    group = SWA_HEADS // SWA_KV
    scale = 1.0 / math.sqrt(HEAD64)

    def body(q_ref, kp_ref, kc_ref, vp_ref, vc_ref, sink_ref, o_ref, lse_ref, do_ref,
             dq_ref, dk_ref, dv_ref, ds_ref):
        i = pl.program_id(0)

        @pl.when(i == 0)
        def _():
            dk_ref[...] = jnp.zeros_like(dk_ref)
            dv_ref[...] = jnp.zeros_like(dv_ref)
            ds_ref[...] = jnp.zeros_like(ds_ref)

        valid, rel = _swa_geometry(i)
        first = _first_half((BLK, LANES))
        lane1 = lax.broadcasted_iota(I32, (1, LANES), 1)
        lane = lax.broadcasted_iota(I32, (BLK, LANES), 1)
        lse_all = lse_ref[...]
        prow = pl.ds(pl.multiple_of(jnp.maximum(i - 1, 0) * BLK, BLK), BLK)
        crow = pl.ds(pl.multiple_of(i * BLK, BLK), BLK)
        dsink = jnp.zeros((1, LANES), F32)
        for g in range(SWA_KV):
            cols = slice(g * LANES, (g + 1) * LANES)
            kk = jnp.concatenate([kp_ref[:, cols], kc_ref[:, cols]], axis=0)
            vv = jnp.concatenate([vp_ref[:, cols], vc_ref[:, cols]], axis=0)
            dkk = jnp.zeros((2 * BLK, LANES), F32)
            dvv = jnp.zeros((2 * BLK, LANES), F32)
            for pair in range(group // 2):
                tile = (g * group) // 2 + pair
                tcols = slice(tile * LANES, (tile + 1) * LANES)
                q2 = q_ref[:, tcols]
                do2 = do_ref[:, tcols]
                prod = do2 * o_ref[:, tcols]
                dqs = []
                for a in range(2):
                    h = 2 * tile + a
                    mine = first if a == 0 else ~first
                    qh = jnp.where(mine, q2, jnp.zeros_like(q2))
                    doh = jnp.where(mine, do2, 0.0).astype(BF16)
                    delta = jnp.sum(jnp.where(mine, prod, 0.0), axis=1, keepdims=True)
                    lse_h = jnp.sum(jnp.where(lane == h, lse_all, 0.0), axis=1, keepdims=True)
                    s = _dot(qh, kk, NT) * scale - _swa_slope(h) * rel
                    p = jnp.where(valid, jnp.exp(s - lse_h), 0.0)
                    dsc = (p * (_dot(doh, vv, NT) - delta) * scale).astype(BF16)
                    dqs.append(_dot(dsc, kk))
                    dkk = dkk + _dot(dsc, qh, TN)
                    dvv = dvv + _dot(p.astype(BF16), doh, TN)
                    psink = jnp.exp(sink_ref[h] - lse_h)
                    dsink = dsink + jnp.where(lane1 == h, -jnp.sum(psink * delta), 0.0)
                dq_ref[:, tcols] = jnp.where(first, dqs[0], dqs[1])
            dk_ref[prow, cols] += dkk[:BLK]
            dv_ref[prow, cols] += dvv[:BLK]
            dk_ref[crow, cols] += dkk[BLK:]
            dv_ref[crow, cols] += dvv[BLK:]
        ds_ref[...] += dsink

    prev = lambda i: (jnp.maximum(i - 1, 0), 0)
    cur = lambda i: (i, 0)
    kvw = SWA_KV * LANES
    whole = pl.BlockSpec((S, kvw), lambda i: (0, 0))
    return _pallas(
        body, name=name, grid=(nq,),
        in_specs=[pl.BlockSpec((BLK, 1024), cur), pl.BlockSpec((BLK, kvw), prev), pl.BlockSpec((BLK, kvw), cur),
                  pl.BlockSpec((BLK, kvw), prev), pl.BlockSpec((BLK, kvw), cur),
                  pl.BlockSpec(memory_space=pltpu.SMEM),
                  pl.BlockSpec((BLK, 1024), cur), pl.BlockSpec((BLK, LANES), cur), pl.BlockSpec((BLK, 1024), cur)],
        out_specs=[pl.BlockSpec((BLK, 1024), cur), whole, whole, pl.BlockSpec((1, LANES), lambda i: (0, 0))],
        out_shape=[jax.ShapeDtypeStruct((S, 1024), F32), jax.ShapeDtypeStruct((S, kvw), F32),
                   jax.ShapeDtypeStruct((S, kvw), F32), jax.ShapeDtypeStruct((1, LANES), F32)],
        compiler_params=_params(("arbitrary",), 4 * S * kvw * 4 + 24 * BLK * 1024 * 4),
    )(qn, kdup, kdup, vdup, vdup, sinks, o, lse, do)


def _in_bwd(x, h, dproj, dy, g, w_in, tag):
    dh = matmul(dproj, w_in, "nt", name=f"{tag}_dh")
    dw_in = matmul(h, dproj, "tn", name=f"{tag}_dwin")
    dx, dg = rmsnorm_bwd(x, g, dh, col_off=0, width=D_MODEL, out_dtype=F32, res=dy, name=f"{tag}_dnorm")
    return dx, dw_in, dg


def _out_bwd(dy, og, o, proj, w_out, gate_off, tag):
    dog = matmul(dy, w_out, "nt", name=f"{tag}_dog")
    dw_out = matmul(og, dy, "tn", name=f"{tag}_dwout")
    do, dgate = gate_bwd(dog, o, proj, gate_off=gate_off, name=f"{tag}_dgate")
    return do, dgate, dw_out


def sb_fwd(x, p, tag):
    h = rmsnorm_fwd(x, p["norm"], col_off=0, width=D_MODEL, out_dtype=BF16, name=f"{tag}_norm")
    proj, projb = matmul(h, p["w_in"], "nn", out_dtype=(F32, BF16), name=f"{tag}_proj")
    o = sb_attn_fwd(projb, name=f"{tag}_attn")
    og = gate_fwd(o, proj, gate_off=3 * D_MODEL, name=f"{tag}_gate")
    y = matmul(og, p["w_out"], "nn", add=x, name=f"{tag}_out")
    return y, (x, h, proj, projb, o, og)


def sb_bwd(dy, saved, p, tag):
    x, h, proj, projb, o, og = saved
    do, dgate, dw_out = _out_bwd(dy, og, o, proj, p["w_out"], 3 * D_MODEL, tag)
    dq, dk, dv = sb_attn_bwd(projb, o, do, name=f"{tag}_dattn")
    dproj = jnp.concatenate([dq, dk.astype(BF16), dv.astype(BF16), dgate], axis=1)
    dx, dw_in, dg = _in_bwd(x, h, dproj, dy, p["norm"], p["w_in"], tag)
    return dx, {"norm": dg[0], "w_in": dw_in, "w_out": dw_out}


MLA_GATE, MLA_QLAT, MLA_KVLAT, MLA_KPE, MLA_IN = 0, 1024, 1280, 1408, 1536


def mla_fwd(x, p, tabs, tag):
    cos, sgn = tabs
    h = rmsnorm_fwd(x, p["norm"], col_off=0, width=D_MODEL, out_dtype=BF16, name=f"{tag}_norm")
    proj = matmul(h, p["w_in"], "nn", name=f"{tag}_proj")
    ql = rmsnorm_fwd(proj, p["q_a_norm"], col_off=MLA_QLAT, width=256, out_dtype=BF16, name=f"{tag}_qanorm")
    kvl = rmsnorm_fwd(proj, p["kv_a_norm"], col_off=MLA_KVLAT, width=128, out_dtype=BF16, name=f"{tag}_kvanorm")
    qraw = matmul(ql, p["w_uq"], "nn", name=f"{tag}_uq")
    kv = matmul(kvl, p["w_ukv"], "nn", name=f"{tag}_ukv")
    qn, kn, vb = mla_prep_fwd(qraw, kv, proj, p["gq"], p["gk"], cos, sgn, kpe_off=MLA_KPE, name=f"{tag}_prep")
    o, lse = flash_fwd(qn, kn, vb, name=f"{tag}_attn")
    og = gate_fwd(o, proj, gate_off=MLA_GATE, name=f"{tag}_gate")
    y = matmul(og, p["w_out"], "nn", add=x, name=f"{tag}_out")
    return y, (x, h, proj, ql, kvl, qraw, kv, qn, kn, vb, o, lse, og)


def mla_bwd(dy, saved, p, tabs, tag):
    cos, sgn = tabs
    x, h, proj, ql, kvl, qraw, kv, qn, kn, vb, o, lse, og = saved
    do, dgate, dw_out = _out_bwd(dy, og, o, proj, p["w_out"], MLA_GATE, tag)
    dqn, dkn, dv = flash_bwd(qn, kn, vb, o, lse, do, name=f"{tag}_dattn")
    dqraw, dkv, dkpe, dgq, dgk = mla_prep_bwd(dqn, dkn, dv, qraw, kv, proj, p["gq"], p["gk"], cos, sgn,
                                              kpe_off=MLA_KPE, name=f"{tag}_dprep")
    dql = matmul(dqraw, p["w_uq"], "nt", name=f"{tag}_dql")
    dw_uq = matmul(ql, dqraw, "tn", name=f"{tag}_dwuq")
    dkvl = matmul(dkv, p["w_ukv"], "nt", name=f"{tag}_dkvl")
    dw_ukv = matmul(kvl, dkv, "tn", name=f"{tag}_dwukv")
    dqlat, dgqa = rmsnorm_bwd(proj, p["q_a_norm"], dql, col_off=MLA_QLAT, width=256, out_dtype=BF16, name=f"{tag}_dqanorm")
    dkvlat, dgkva = rmsnorm_bwd(proj, p["kv_a_norm"], dkvl, col_off=MLA_KVLAT, width=128, out_dtype=BF16,
                                name=f"{tag}_dkvanorm")
    dproj = jnp.concatenate([dgate, dqlat, dkvlat, dkpe], axis=1)
    dx, dw_in, dg = _in_bwd(x, h, dproj, dy, p["norm"], p["w_in"], tag)
    return dx, {"norm": dg[0], "w_in": dw_in, "q_a_norm": dgqa[0], "w_uq": dw_uq, "kv_a_norm": dgkva[0],
                "w_ukv": dw_ukv, "gq": dgq[0], "gk": dgk[0], "w_out": dw_out}


SWA_Q, SWA_GATE, SWA_K, SWA_V = 0, 1024, 2048, 2304


def swa_fwd(x, p, tag):
    h = rmsnorm_fwd(x, p["norm"], col_off=0, width=D_MODEL, out_dtype=BF16, name=f"{tag}_norm")
    proj = matmul(h, p["w_in"], "nn", name=f"{tag}_proj")
    qn = headnorm_fwd(proj, p["q_head_norm"], col_off=SWA_Q, width=1024, name=f"{tag}_qnorm")
    kdup, vdup = swa_kv_prep(proj, p["k_head_norm"], k_off=SWA_K, v_off=SWA_V, name=f"{tag}_kvprep")
    o, lse = swa_attn_fwd(qn, kdup, vdup, p["sinks"], name=f"{tag}_attn")
    og = gate_fwd(o, proj, gate_off=SWA_GATE, name=f"{tag}_gate")
    y = matmul(og, p["w_out"], "nn", add=x, name=f"{tag}_out")
    return y, (x, h, proj, qn, kdup, vdup, o, lse, og)


def swa_bwd(dy, saved, p, tag):
    x, h, proj, qn, kdup, vdup, o, lse, og = saved
    do, dgate, dw_out = _out_bwd(dy, og, o, proj, p["w_out"], SWA_GATE, tag)
    dqn, dkdup, dvdup, dsinks = swa_attn_bwd(qn, kdup, vdup, p["sinks"], o, lse, do, name=f"{tag}_dattn")
    dq, dgq = headnorm_bwd(proj, p["q_head_norm"], dqn, col_off=SWA_Q, width=1024, name=f"{tag}_dqnorm")
    dk, dv, dgk = swa_kv_prep_bwd(dkdup, dvdup, proj, p["k_head_norm"], k_off=SWA_K, name=f"{tag}_dkvprep")
    dproj = jnp.concatenate([dq, dgate, dk, dv], axis=1)
    dx, dw_in, dg = _in_bwd(x, h, dproj, dy, p["norm"], p["w_in"], tag)
    return dx, {"norm": dg[0], "w_in": dw_in, "q_head_norm": dgq[0, :HEAD64] + dgq[0, HEAD64:],
                "k_head_norm": dgk[0, :HEAD64] + dgk[0, HEAD64:], "sinks": dsinks[0, :SWA_HEADS], "w_out": dw_out}


def prepare_weights(w):
    l1_in, l2_in = w["l1_w_in"], w["l2_w_in"]
    pad64 = lambda v: jnp.pad(v, (0, MLA_PAD - MLA_QK)).reshape(1, MLA_PAD)
    return [
        {"norm": w["l0_norm"], "w_in": w["l0_w_in"], "w_out": w["l0_w_out"]},
        {"norm": w["l1_norm"],
         "w_in": jnp.concatenate([l1_in[:, 448:], l1_in[:, :448], jnp.zeros((D_MODEL, 64), l1_in.dtype)], axis=1),
         "q_a_norm": w["l1_q_a_norm"], "kv_a_norm": w["l1_kv_a_norm"],
         "w_uq": jnp.pad(w["l1_w_uq"].reshape(256, MLA_HEADS, MLA_QK), ((0, 0), (0, 0), (0, MLA_PAD - MLA_QK))
                         ).reshape(256, MLA_HEADS * MLA_PAD),
         "w_ukv": w["l1_w_ukv"], "gq": pad64(w["l1_q_head_norm"]), "gk": pad64(w["l1_k_head_norm"]),
         "w_out": w["l1_w_out"]},
        {"norm": w["l2_norm"],
         "w_in": jnp.concatenate([l2_in[:, :1024], l2_in[:, 1536:], l2_in[:, 1024:1536]], axis=1),
         "q_head_norm": w["l2_q_head_norm"], "k_head_norm": w["l2_k_head_norm"], "sinks": w["l2_sinks"],
         "w_out": w["l2_w_out"]},
        {"norm": w["l3_norm"], "w_in": w["l3_w_in"], "w_out": w["l3_w_out"]},
    ]


def unprepare_grads(gs):
    g0, g1, g2, g3 = gs
    d1, d2 = g1["w_in"], g2["w_in"]
    return {
        "l0_norm": g0["norm"], "l0_w_in": g0["w_in"], "l0_w_out": g0["w_out"],
        "l1_norm": g1["norm"], "l1_w_in": jnp.concatenate([d1[:, 1024:1472], d1[:, :1024]], axis=1),
        "l1_q_a_norm": g1["q_a_norm"],
        "l1_w_uq": g1["w_uq"].reshape(256, MLA_HEADS, MLA_PAD)[:, :, :MLA_QK].reshape(256, MLA_HEADS * MLA_QK),
        "l1_kv_a_norm": g1["kv_a_norm"], "l1_w_ukv": g1["w_ukv"],
        "l1_q_head_norm": g1["gq"][:MLA_QK], "l1_k_head_norm": g1["gk"][:MLA_QK], "l1_w_out": g1["w_out"],
        "l2_norm": g2["norm"], "l2_w_in": jnp.concatenate([d2[:, :1024], d2[:, 2048:], d2[:, 1024:2048]], axis=1),
        "l2_q_head_norm": g2["q_head_norm"], "l2_k_head_norm": g2["k_head_norm"], "l2_sinks": g2["sinks"],
        "l2_w_out": g2["w_out"],
        "l3_norm": g3["norm"], "l3_w_in": g3["w_in"], "l3_w_out": g3["w_out"],
    }


def local_step(x, target, w):
    ps = prepare_weights(w)
    tabs = _rope_tables(x.shape[0])
    y0, s0 = sb_fwd(x, ps[0], "l0")
    y1, s1 = mla_fwd(y0, ps[1], tabs, "l1")
    y2, s2 = swa_fwd(y1, ps[2], "l2")
    y3, s3 = sb_fwd(y2, ps[3], "l3")
    dy, loss = loss_head(y3, target, name="loss")
    d3, g3 = sb_bwd(dy, s3, ps[3], "l3")
    d2, g2 = swa_bwd(d3, s2, ps[2], "l2")
    d1, g1 = mla_bwd(d2, s1, ps[1], tabs, "l1")
    d0, g0 = sb_bwd(d1, s0, ps[0], "l0")
    return loss, d0, unprepare_grads([g0, g1, g2, g3])


MATS = (("l0_w_in", "col", 1024, 4096), ("l0_w_out", "row", 1024, 1024), ("l1_w_in", "col", 1024, 1472),
        ("l1_w_uq", "col", 256, 1536), ("l1_w_ukv", "col", 128, 2048), ("l1_w_out", "row", 1024, 1024),
        ("l2_w_in", "col", 1024, 2560), ("l2_w_out", "row", 1024, 1024), ("l3_w_in", "col", 1024, 4096),
        ("l3_w_out", "row", 1024, 1024))
N_CHIPS = 4
PACK_W = 1024
HALF_ROWS = 2176
PACK_ROWS = 2 * HALF_ROWS
VECS = (("l0_norm", 0, 0, 1024), ("l1_norm", 1, 0, 1024), ("l2_norm", 2, 0, 1024), ("l3_norm", 3, 0, 1024),
        ("l1_q_a_norm", 4, 0, 256), ("l1_kv_a_norm", 4, 256, 128), ("l1_q_head_norm", 4, 384, 192),
        ("l1_k_head_norm", 4, 576, 192), ("l2_q_head_norm", 4, 768, 64), ("l2_k_head_norm", 4, 832, 64),
        ("l2_sinks", 4, 896, 16))
LOSS_SLOT = (4, 912)
VEC_ROWS = 8


def _shard_rows(k, n):
    return k * n // N_CHIPS // PACK_W


def pack_shards(shards):
    parts = [shards[name].reshape(-1, PACK_W) for name, _, _, _ in MATS]
    used = sum(p.shape[0] for p in parts)
    return jnp.concatenate(parts + [jnp.zeros((PACK_ROWS - used, PACK_W), parts[0].dtype)], axis=0)


def unpack_shards(flat):
    out, r0 = {}, 0
    for name, kind, k, n in MATS:
        rows = _shard_rows(k, n)
        shape = (k, n // N_CHIPS) if kind == "col" else (k // N_CHIPS, n)
        out[name] = flat[r0:r0 + rows].reshape(shape)
        r0 += rows
    return out


def pack_full(mats):
    parts = []
    for name, kind, k, n in MATS:
        m = mats[name]
        if kind == "col":
            m = m.reshape(k, N_CHIPS, n // N_CHIPS).transpose(1, 0, 2)
        parts.append(m.reshape(N_CHIPS, -1, PACK_W))
    used = sum(p.shape[1] for p in parts)
    return jnp.concatenate(parts + [jnp.zeros((N_CHIPS, PACK_ROWS - used, PACK_W), parts[0].dtype)], axis=1)


def unpack_full(stacked):
    out, r0 = {}, 0
    for name, kind, k, n in MATS:
        rows = _shard_rows(k, n)
        seg = stacked[:, r0:r0 + rows]
        if kind == "col":
            out[name] = seg.reshape(N_CHIPS, k, n // N_CHIPS).transpose(1, 0, 2).reshape(k, n)
        else:
            out[name] = seg.reshape(k, n)
        r0 += rows
    return out


def pack_vecs(vecs, loss=None):
    rows = []
    for r in range(VEC_ROWS):
        items = [(off, vecs[name]) for name, rr, off, _ in VECS if rr == r]
        if loss is not None and r == LOSS_SLOT[0]:
            items.append((LOSS_SLOT[1], loss.reshape(1)))
        pos, parts = 0, []
        for off, v in sorted(items, key=lambda t: t[0]):
            assert off == pos
            parts.append(v.astype(F32))
            pos += v.shape[0]
        parts.append(jnp.zeros((PACK_W - pos,), F32))
        rows.append(jnp.concatenate(parts))
    return jnp.stack(rows)


def unpack_vecs(block):
    return {name: block[r, off:off + n] for name, r, off, n in VECS}


def _me():
    return lax.axis_index("x"), lax.axis_index("y"), lax.axis_index("c")


OTHER_CHIPS = ((1, 0), (0, 1), (1, 1))


def _remote(src, dst, send_sem, recv_sem, to):
    return pltpu.make_async_remote_copy(src_ref=src, dst_ref=dst, send_sem=send_sem, recv_sem=recv_sem,
                                        device_id=to, device_id_type=MESH)


def gather_weights(block):
    def body(in_ref, out_ref, send_sems, recv_sems, local_sem):
        x, y, c = _me()
        sibling = (x, y, 1 - c)

        def half(px, py, pc):
            return out_ref.at[2 * px + py, pl.ds(pc * HALF_ROWS, HALF_ROWS), :]

        mine = pltpu.make_async_copy(in_ref, out_ref.at[2 * x + y], local_sem)
        mine.start()
        chips = [(x ^ dx, y ^ dy) for dx, dy in OTHER_CHIPS]
        first = [_remote(in_ref.at[pl.ds(c * HALF_ROWS, HALF_ROWS), :], half(x, y, c), send_sems.at[j], recv_sems.at[j],
                         (*chip, c)) for j, chip in enumerate(chips)]
        for cp in first:
            cp.start()
        passed = [_remote(half(*chip, c), half(*chip, c), send_sems.at[3 + j], recv_sems.at[3 + j], sibling)
                  for j, chip in enumerate(chips)]
        for j, chip in enumerate(chips):
            _remote(half(*chip, c), half(*chip, c), send_sems.at[j], recv_sems.at[j], (*chip, c)).wait_recv()
            passed[j].start()
        for j, chip in enumerate(chips):
            _remote(half(*chip, 1 - c), half(*chip, 1 - c), send_sems.at[3 + j], recv_sems.at[3 + j], sibling).wait_recv()
        for cp in first + passed:
            cp.wait_send()
        mine.wait()

    hbm = pl.BlockSpec(memory_space=pltpu.HBM)
    return pl.pallas_call(
        body, name="gather_weights",
        out_shape=jax.ShapeDtypeStruct((N_CHIPS, PACK_ROWS, PACK_W), block.dtype),
        in_specs=[hbm], out_specs=hbm,
        scratch_shapes=[pltpu.SemaphoreType.DMA((6,)), pltpu.SemaphoreType.DMA((6,)), pltpu.SemaphoreType.DMA],
    )(block)


def pair_exchange(g):
    def body(g_ref, out_ref, send_sems, recv_sems):
        x, y, c = _me()
        sibling = (x, y, 1 - c)
        copies = [_remote(g_ref.at[k, pl.ds((1 - c) * HALF_ROWS, HALF_ROWS), :], out_ref.at[k], send_sems.at[k],
                          recv_sems.at[k], sibling) for k in range(N_CHIPS)]
        for cp in copies:
            cp.start()
        for cp in copies:
            cp.wait_recv()
        for cp in copies:
            cp.wait_send()

    hbm = pl.BlockSpec(memory_space=pltpu.HBM)
    return pl.pallas_call(
        body, name="pair_exchange",
        out_shape=jax.ShapeDtypeStruct((N_CHIPS, HALF_ROWS, PACK_W), g.dtype),
        in_specs=[hbm], out_specs=hbm,
        scratch_shapes=[pltpu.SemaphoreType.DMA((N_CHIPS,)), pltpu.SemaphoreType.DMA((N_CHIPS,))],
    )(g)


def chip_exchange(part):
    def body(p_ref, out_ref, send_sems, recv_sems):
        x, y, c = _me()
        copies = [_remote(p_ref.at[2 * (x ^ dx) + (y ^ dy)], out_ref.at[j], send_sems.at[j], recv_sems.at[j],
                          (x ^ dx, y ^ dy, c)) for j, (dx, dy) in enumerate(OTHER_CHIPS)]
        for cp in copies:
            cp.start()
        for cp in copies:
            cp.wait_recv()
        for cp in copies:
            cp.wait_send()

    hbm = pl.BlockSpec(memory_space=pltpu.HBM)
    return pl.pallas_call(
        body, name="chip_exchange",
        out_shape=jax.ShapeDtypeStruct((len(OTHER_CHIPS), HALF_ROWS, PACK_W), part.dtype),
        in_specs=[hbm], out_specs=hbm,
        scratch_shapes=[pltpu.SemaphoreType.DMA((3,)), pltpu.SemaphoreType.DMA((3,))],
    )(part)


def join_halves(half):
    def body(h_ref, out_ref, send_sem, recv_sem, local_sem):
        x, y, c = _me()
        mine = pltpu.make_async_copy(h_ref, out_ref.at[pl.ds(c * HALF_ROWS, HALF_ROWS), :], local_sem)
        mine.start()
        cp = _remote(h_ref, out_ref.at[pl.ds(c * HALF_ROWS, HALF_ROWS), :], send_sem, recv_sem, (x, y, 1 - c))
        cp.start()
        _remote(h_ref, out_ref.at[pl.ds((1 - c) * HALF_ROWS, HALF_ROWS), :], send_sem, recv_sem, (x, y, 1 - c)).wait_recv()
        cp.wait_send()
        mine.wait()

    hbm = pl.BlockSpec(memory_space=pltpu.HBM)
    return pl.pallas_call(
        body, name="join_halves",
        out_shape=jax.ShapeDtypeStruct((PACK_ROWS, PACK_W), half.dtype),
        in_specs=[hbm], out_specs=hbm,
        scratch_shapes=[pltpu.SemaphoreType.DMA, pltpu.SemaphoreType.DMA, pltpu.SemaphoreType.DMA],
    )(half)


def sum_over_devices(block):
    def body(in_ref, out_ref, all_ref, send_sems, recv_sems):
        x, y, c = _me()
        me = 4 * x + 2 * y + c
        all_ref[me] = in_ref[...]
        copies = []
        for r in range(1, 8):
            to = (x ^ (r >> 2), y ^ ((r >> 1) & 1), c ^ (r & 1))
            copies.append(_remote(in_ref, all_ref.at[me], send_sems.at[r - 1], recv_sems.at[r - 1], to))
        for cp in copies:
            cp.start()
        for r in range(1, 8):
            frm = (x ^ (r >> 2), y ^ ((r >> 1) & 1), c ^ (r & 1))
            _remote(in_ref, all_ref.at[4 * frm[0] + 2 * frm[1] + frm[2]], send_sems.at[r - 1], recv_sems.at[r - 1],
                    frm).wait_recv()
        for cp in copies:
            cp.wait_send()
        acc = all_ref[0]
        for d in range(1, 8):
            acc = acc + all_ref[d]
        out_ref[...] = acc

    vmem = pl.BlockSpec(memory_space=pltpu.VMEM)
    return pl.pallas_call(
        body, name="sum_over_devices",
        out_shape=jax.ShapeDtypeStruct(block.shape, F32),
        in_specs=[vmem], out_specs=vmem,
        scratch_shapes=[pltpu.VMEM((8,) + block.shape, F32), pltpu.SemaphoreType.DMA((7,)), pltpu.SemaphoreType.DMA((7,))],
    )(block)


SUM_ROWS = 272


def pair_sum(g, got, core):
    steps = HALF_ROWS // SUM_ROWS

    def body(s_ref, g_ref, r_ref, o_ref, ob_ref):
        t = g_ref[...] + r_ref[...]
        o_ref[...] = t
        ob_ref[...] = t.astype(BF16)

    blk = pl.BlockSpec((1, SUM_ROWS, PACK_W), lambda k, i, s: (k, i, 0))
    return pl.pallas_call(
        body, name="pair_sum",
        grid_spec=pltpu.PrefetchScalarGridSpec(
            num_scalar_prefetch=1, grid=(N_CHIPS, steps),
            in_specs=[pl.BlockSpec((1, SUM_ROWS, PACK_W), lambda k, i, s: (k, s[0] * steps + i, 0)), blk],
            out_specs=[blk, blk]),
        out_shape=[jax.ShapeDtypeStruct((N_CHIPS, HALF_ROWS, PACK_W), F32),
                   jax.ShapeDtypeStruct((N_CHIPS, HALF_ROWS, PACK_W), BF16)],
        compiler_params=_params(("parallel", "parallel"), 8 * SUM_ROWS * PACK_W * 4),
    )(core, g, got)


def chip_sum(part, got, chip):
    steps = HALF_ROWS // SUM_ROWS

    def body(s_ref, p_ref, a_ref, b_ref, c_ref, o_ref):
        o_ref[...] = ((p_ref[0] + a_ref[0].astype(F32)) + b_ref[0].astype(F32)) + c_ref[0].astype(F32)

    def got_spec(j):
        return pl.BlockSpec((1, SUM_ROWS, PACK_W), lambda i, s: (j, i, 0))

    return pl.pallas_call(
        body, name="chip_sum",
        grid_spec=pltpu.PrefetchScalarGridSpec(
            num_scalar_prefetch=1, grid=(steps,),
            in_specs=[pl.BlockSpec((1, SUM_ROWS, PACK_W), lambda i, s: (s[0], i, 0)), got_spec(0), got_spec(1), got_spec(2)],
            out_specs=pl.BlockSpec((SUM_ROWS, PACK_W), lambda i, s: (i, 0))),
        out_shape=jax.ShapeDtypeStruct((HALF_ROWS, PACK_W), F32),
        compiler_params=_params(("parallel",), 10 * SUM_ROWS * PACK_W * 4),
    )(chip, part, got, got, got)


def reduce_scatter(g):
    x, y, c = _me()
    part, part_bf16 = pair_sum(g, pair_exchange(g), jnp.reshape(c, (1,)).astype(I32))
    half = chip_sum(part, chip_exchange(part_bf16), jnp.reshape(2 * x + y, (1,)).astype(I32))
    return join_halves(half)


def adamw(w, g, m, v, *, name):
    rows, cols = w.shape
    tm = _tile(rows, 256)
    c1 = 1.0 - ADAM_B1 ** ADAM_STEP
    c2 = 1.0 - ADAM_B2 ** ADAM_STEP

    def body(w_ref, g_ref, m_ref, v_ref, d_ref, mo_ref, vo_ref):
        gv = g_ref[...]
        mn = ADAM_B1 * m_ref[...] + (1.0 - ADAM_B1) * gv
        vn = ADAM_B2 * v_ref[...] + (1.0 - ADAM_B2) * (gv * gv)
        d_ref[...] = -ADAM_LR * ((mn / c1) / (jnp.sqrt(vn / c2) + ADAM_EPS) + ADAM_WD * w_ref[...])
        mo_ref[...] = mn
        vo_ref[...] = vn

    blk = pl.BlockSpec((tm, cols), lambda i: (i, 0))
    shape = jax.ShapeDtypeStruct(w.shape, F32)
    return _pallas(
        body, name=name, grid=(rows // tm,),
        in_specs=[blk] * 4, out_specs=[blk] * 3, out_shape=[shape] * 3,
        compiler_params=_params(("parallel",), 16 * tm * cols * 4),
    )(w, g, m, v)


WEIGHTS = ("l0_norm", "l0_w_in", "l0_w_out", "l1_norm", "l1_w_in", "l1_q_a_norm", "l1_w_uq", "l1_kv_a_norm", "l1_w_ukv",
           "l1_q_head_norm", "l1_k_head_norm", "l1_w_out", "l2_norm", "l2_w_in", "l2_q_head_norm", "l2_k_head_norm",
           "l2_sinks", "l2_w_out", "l3_norm", "l3_w_in", "l3_w_out")


def kernel(x, l0_norm, l0_w_in, l0_w_out, l1_norm, l1_w_in, l1_q_a_norm, l1_w_uq, l1_kv_a_norm, l1_w_ukv, l1_q_head_norm, l1_k_head_norm, l1_w_out, l2_norm, l2_w_in, l2_q_head_norm, l2_k_head_norm, l2_sinks, l2_w_out, l3_norm, l3_w_in, l3_w_out, loss_target, m_l0_norm, m_l0_w_in, m_l0_w_out, m_l1_norm, m_l1_w_in, m_l1_q_a_norm, m_l1_w_uq, m_l1_kv_a_norm, m_l1_w_ukv, m_l1_q_head_norm, m_l1_k_head_norm, m_l1_w_out, m_l2_norm, m_l2_w_in, m_l2_q_head_norm, m_l2_k_head_norm, m_l2_sinks, m_l2_w_out, m_l3_norm, m_l3_w_in, m_l3_w_out, v_l0_norm, v_l0_w_in, v_l0_w_out, v_l1_norm, v_l1_w_in, v_l1_q_a_norm, v_l1_w_uq, v_l1_kv_a_norm, v_l1_w_ukv, v_l1_q_head_norm, v_l1_k_head_norm, v_l1_w_out, v_l2_norm, v_l2_w_in, v_l2_q_head_norm, v_l2_k_head_norm, v_l2_sinks, v_l2_w_out, v_l3_norm, v_l3_w_in, v_l3_w_out):
    given = dict(locals())
    w = {n: given[n] for n in WEIGHTS}
    m = {n: given["m_" + n] for n in WEIGHTS}
    v = {n: given["v_" + n] for n in WEIGHTS}
    mat_names = [t[0] for t in MATS]
    vec_names = [t[0] for t in VECS]

    gathered = gather_weights(pack_shards({n: w[n] for n in mat_names}).astype(BF16))
    full = unpack_full(gathered)
    full.update({n: w[n] for n in vec_names})
    loss_tile, grad_x, grads = local_step(x[0], loss_target[0], full)

    block = reduce_scatter(pack_full({n: grads[n] for n in mat_names}))
    g = unpack_shards(block)
    vec_sum = sum_over_devices(pack_vecs({n: grads[n] for n in vec_names}, loss=loss_tile[0, 0]))
    loss = vec_sum[LOSS_SLOT[0], LOSS_SLOT[1]]

    delta, new_m, new_v = {}, {}, {}
    for n in mat_names:
        delta[n], new_m[n], new_v[n] = adamw(w[n], g[n], m[n], v[n], name=f"adamw_{n}")
    dv, mv, vv = adamw(pack_vecs(w), vec_sum, pack_vecs(m), pack_vecs(v), name="adamw_vecs")
    g.update(unpack_vecs(vec_sum))
    delta.update(unpack_vecs(dv))
    new_m.update(unpack_vecs(mv))
    new_v.update(unpack_vecs(vv))
    return (loss, grad_x[None], *[g[n] for n in WEIGHTS], *[delta[n] for n in WEIGHTS],
            *[new_m[n] for n in WEIGHTS], *[new_v[n] for n in WEIGHTS])
```

```python
import math

import jax
import jax.numpy as jnp
from jax import lax
from jax.experimental import pallas as pl
from jax.experimental.pallas import tpu as pltpu

F32 = jnp.float32
BF16 = jnp.bfloat16
I32 = jnp.int32
MESH = pl.DeviceIdType.MESH

NORM_EPS = 1e-6
D_MODEL = 1024
HEAD64 = 64
LANES = 128
BLK = 128
MLA_HEADS = 8
MLA_QK = 192
MLA_PAD = 256
ROPE_THETA = 10000.0
SWA_HEADS = 16
SWA_KV = 4
VMEM_CAP = 56 * 1024 * 1024

ADAM_LR, ADAM_B1, ADAM_B2, ADAM_EPS, ADAM_WD, ADAM_STEP = 0.001, 0.9, 0.999, 1e-08, 0.01, 10

NT = (((1,), (1,)), ((), ()))
NN = (((1,), (0,)), ((), ()))
TN = (((0,), (0,)), ((), ()))


def _dot(a, b, dims=NN):
    return lax.dot_general(a, b, dims, preferred_element_type=F32)


def _tile(n, pref):
    for t in (pref, 512, 256, 128):
        if t <= pref and n % t == 0:
            return t
    return n


def _params(sem, vmem_bytes):
    limit = int(min(max(2 * vmem_bytes, 24 * 1024 * 1024), VMEM_CAP))
    return pltpu.CompilerParams(dimension_semantics=sem, vmem_limit_bytes=limit)


def _in_hbm(s):
    return pltpu.HBM(s.shape, s.dtype) if len(s.shape) >= 2 else s


def _pallas(*args, out_shape, **kwargs):
    out_shape = [_in_hbm(s) for s in out_shape] if isinstance(out_shape, (list, tuple)) else _in_hbm(out_shape)
    call = pl.pallas_call(*args, out_shape=out_shape, **kwargs)

    def run(*operands):
        return call(*[pltpu.with_memory_space_constraint(a, pltpu.HBM) if a.ndim >= 2 else a for a in operands])

    return run


def _split(v):
    hi = v.astype(BF16)
    return hi, (v - hi.astype(F32)).astype(BF16)


def _dot2(v, m):
    hi, lo = _split(v)
    return _dot(hi, m) + _dot(lo, m)


def _first_half(shape):
    return lax.broadcasted_iota(I32, shape, 1) < HEAD64


def _sigmoid(g):
    return 1.0 / (1.0 + jnp.exp(-g))


def matmul(a, b, mode, *, name, out_dtype=F32, add=None, tm=512, tn=1024, tk=1024):
    if mode == "nn":
        (M, K), (K2, N) = a.shape, b.shape
    elif mode == "nt":
        (M, K), (N, K2) = a.shape, b.shape
    else:
        (K, M), (K2, N) = a.shape, b.shape
    assert K == K2, (a.shape, b.shape, mode)
    tm, tn, tk = _tile(M, tm), _tile(N, tn), _tile(K, tk)
    nk = K // tk
    dims = {"nn": NN, "nt": NT, "tn": TN}[mode]

    n_in = 2 if add is None else 3

    def body(*refs):
        a_ref, b_ref = refs[:2]
        out_refs, acc_ref = refs[n_in:-1], refs[-1]
        k = pl.program_id(2)
        part = _dot(a_ref[...].astype(BF16), b_ref[...].astype(BF16), dims)

        def finish(r):
            if add is not None:
                r = r + refs[2][...]
            for o_ref in out_refs:
                o_ref[...] = r.astype(o_ref.dtype)

        if nk == 1:
            finish(part)
            return

        @pl.when(k == 0)
        def _():
            acc_ref[...] = part

        @pl.when((k > 0) & (k < nk - 1))
        def _():
            acc_ref[...] += part

        @pl.when(k == nk - 1)
        def _():
            finish(acc_ref[...] + part)

    a_spec = (pl.BlockSpec((tk, tm), lambda i, j, k: (k, i)) if mode == "tn"
              else pl.BlockSpec((tm, tk), lambda i, j, k: (i, k)))
    b_spec = (pl.BlockSpec((tn, tk), lambda i, j, k: (j, k)) if mode == "nt"
              else pl.BlockSpec((tk, tn), lambda i, j, k: (k, j)))
    o_spec = pl.BlockSpec((tm, tn), lambda i, j, k: (i, j))
    in_specs, args = [a_spec, b_spec], [a, b]
    if add is not None:
        in_specs.append(o_spec)
        args.append(add)
    vm = 2 * (tm * tk * a.dtype.itemsize + tk * tn * b.dtype.itemsize) + 5 * tm * tn * 4
    out_dtypes = list(out_dtype) if isinstance(out_dtype, (tuple, list)) else [out_dtype]
    res = _pallas(
        body, name=name, grid=(M // tm, N // tn, nk),
        in_specs=in_specs, out_specs=[o_spec] * len(out_dtypes),
        out_shape=[jax.ShapeDtypeStruct((M, N), d) for d in out_dtypes],
        scratch_shapes=[pltpu.VMEM((tm, tn), F32)],
        compiler_params=_params(("parallel", "parallel", "arbitrary"), vm),
    )(*args)
    return res[0] if len(out_dtypes) == 1 else res


def rmsnorm_fwd(x, g, *, col_off, width, out_dtype, name, tm=256):
    S = x.shape[0]
    assert col_off % width == 0
    cb = col_off // width
    tm = _tile(S, tm)

    def body(x_ref, g_ref, o_ref):
        v = x_ref[...]
        r = lax.rsqrt(jnp.mean(v * v, axis=1, keepdims=True) + NORM_EPS)
        o_ref[...] = (v * r * g_ref[...]).astype(out_dtype)

    return _pallas(
        body, name=name, grid=(S // tm,),
        in_specs=[pl.BlockSpec((tm, width), lambda i: (i, cb)), pl.BlockSpec((1, width), lambda i: (0, 0))],
        out_specs=pl.BlockSpec((tm, width), lambda i: (i, 0)),
        out_shape=jax.ShapeDtypeStruct((S, width), out_dtype),
        compiler_params=_params(("parallel",), 4 * tm * width * 4),
    )(x, g.reshape(1, width))


def rmsnorm_bwd(x, g, dh, *, col_off, width, out_dtype, name, res=None, tm=256):
    S = x.shape[0]
    cb = col_off // width
    tm = _tile(S, tm)

    def body(*refs):
        if res is None:
            x_ref, g_ref, dh_ref, dx_ref, dg_ref = refs
        else:
            x_ref, g_ref, dh_ref, res_ref, dx_ref, dg_ref = refs

        @pl.when(pl.program_id(0) == 0)
        def _():
            dg_ref[...] = jnp.zeros_like(dg_ref)

        v = x_ref[...]
        dhv = dh_ref[...].astype(F32)
        r = lax.rsqrt(jnp.mean(v * v, axis=1, keepdims=True) + NORM_EPS)
        y = v * r
        dy = dhv * g_ref[...]
        dx = r * (dy - y * jnp.mean(dy * y, axis=1, keepdims=True))
        if res is not None:
            dx = dx + res_ref[...]
        dx_ref[...] = dx.astype(out_dtype)
        dg_ref[...] += jnp.sum(dhv * y, axis=0, keepdims=True)

    row = pl.BlockSpec((tm, width), lambda i: (i, 0))
    in_specs = [pl.BlockSpec((tm, width), lambda i: (i, cb)), pl.BlockSpec((1, width), lambda i: (0, 0)), row]
    args = [x, g.reshape(1, width), dh]
    if res is not None:
        in_specs.append(row)
        args.append(res)
    return _pallas(
        body, name=name, grid=(S // tm,),
        in_specs=in_specs,
        out_specs=[row, pl.BlockSpec((1, width), lambda i: (0, 0))],
        out_shape=[jax.ShapeDtypeStruct((S, width), out_dtype), jax.ShapeDtypeStruct((1, width), F32)],
        compiler_params=_params(("arbitrary",), 8 * tm * width * 4),
    )(*args)


def _group_ones():
    r = lax.broadcasted_iota(I32, (LANES, LANES), 0) // HEAD64
    c = lax.broadcasted_iota(I32, (LANES, LANES), 1) // HEAD64
    return (r == c).astype(BF16)


def _segmean64(v):
    return _dot2(v, _group_ones()) * (1.0 / HEAD64)


def headnorm_fwd(x, g64, *, col_off, width, name, tm=512):
    S = x.shape[0]
    cb = col_off // LANES
    tm = _tile(S, tm)

    def body(x_ref, g_ref, o_ref):
        v = x_ref[...]
        r = lax.rsqrt(_segmean64(v * v) + NORM_EPS)
        o_ref[...] = (v * r * g_ref[...]).astype(BF16)

    return _pallas(
        body, name=name, grid=(S // tm, width // LANES),
        in_specs=[pl.BlockSpec((tm, LANES), lambda i, j: (i, cb + j)), pl.BlockSpec((1, LANES), lambda i, j: (0, 0))],
        out_specs=pl.BlockSpec((tm, LANES), lambda i, j: (i, j)),
        out_shape=jax.ShapeDtypeStruct((S, width), BF16),
        compiler_params=_params(("parallel", "parallel"), 8 * tm * LANES * 4),
    )(x, jnp.tile(g64, 2).reshape(1, LANES))


def headnorm_bwd(x, g64, dh, *, col_off, width, name, tm=512):
    S = x.shape[0]
    cb = col_off // LANES
    tm = _tile(S, tm)

    def body(x_ref, g_ref, dh_ref, dx_ref, dg_ref):
        @pl.when((pl.program_id(0) == 0) & (pl.program_id(1) == 0))
        def _():
            dg_ref[...] = jnp.zeros_like(dg_ref)

        v = x_ref[...]
        dhv = dh_ref[...]
        r = lax.rsqrt(_segmean64(v * v) + NORM_EPS)
        y = v * r
        dy = dhv * g_ref[...]
        dx_ref[...] = (r * (dy - y * _segmean64(dy * y))).astype(BF16)
        dg_ref[...] += jnp.sum(dhv * y, axis=0, keepdims=True)

    return _pallas(
        body, name=name, grid=(width // LANES, S // tm),
        in_specs=[pl.BlockSpec((tm, LANES), lambda j, i: (i, cb + j)), pl.BlockSpec((1, LANES), lambda j, i: (0, 0)),
                  pl.BlockSpec((tm, LANES), lambda j, i: (i, j))],
        out_specs=[pl.BlockSpec((tm, LANES), lambda j, i: (i, j)), pl.BlockSpec((1, LANES), lambda j, i: (0, 0))],
        out_shape=[jax.ShapeDtypeStruct((S, width), BF16), jax.ShapeDtypeStruct((1, LANES), F32)],
        compiler_params=_params(("arbitrary", "arbitrary"), 10 * tm * LANES * 4),
    )(x, jnp.tile(g64, 2).reshape(1, LANES), dh)


def gate_fwd(o, proj, *, gate_off, name, tm=256):
    S, W = o.shape
    cb = gate_off // W
    tm = _tile(S, tm)

    def body(o_ref, g_ref, out_ref):
        g = g_ref[...]
        out_ref[...] = (o_ref[...] * (g * _sigmoid(g))).astype(BF16)

    return _pallas(
        body, name=name, grid=(S // tm,),
        in_specs=[pl.BlockSpec((tm, W), lambda i: (i, 0)), pl.BlockSpec((tm, W), lambda i: (i, cb))],
        out_specs=pl.BlockSpec((tm, W), lambda i: (i, 0)),
        out_shape=jax.ShapeDtypeStruct((S, W), BF16),
        compiler_params=_params(("parallel",), 6 * tm * W * 4),
    )(o, proj)


def gate_bwd(dog, o, proj, *, gate_off, name, tm=256):
    S, W = o.shape
    cb = gate_off // W
    tm = _tile(S, tm)

    def body(d_ref, o_ref, g_ref, do_ref, dg_ref):
        g = g_ref[...]
        d = d_ref[...]
        s = _sigmoid(g)
        do_ref[...] = d * (g * s)
        dg_ref[...] = (d * o_ref[...] * (s * (1.0 + g * (1.0 - s)))).astype(BF16)

    row = pl.BlockSpec((tm, W), lambda i: (i, 0))
    return _pallas(
        body, name=name, grid=(S // tm,),
        in_specs=[row, row, pl.BlockSpec((tm, W), lambda i: (i, cb))],
        out_specs=[row, row],
        out_shape=[jax.ShapeDtypeStruct((S, W), F32), jax.ShapeDtypeStruct((S, W), BF16)],
        compiler_params=_params(("parallel",), 10 * tm * W * 4),
    )(dog, o, proj)


def loss_head(y, target, *, name, tm=256):
    S, W = y.shape
    tm = _tile(S, tm)
    n = S // tm

    def body(y_ref, t_ref, dy_ref, l_ref, acc_ref):
        i = pl.program_id(0)

        @pl.when(i == 0)
        def _():
            acc_ref[...] = jnp.zeros_like(acc_ref)

        e = y_ref[...] - t_ref[...]
        dy_ref[...] = e * (1.0 / W)
        acc_ref[...] += jnp.sum(e * e, axis=0, keepdims=True)

        @pl.when(i == n - 1)
        def _():
            l_ref[...] = jnp.full(l_ref.shape, (0.5 / W) * jnp.sum(acc_ref[...]), F32)

    row = pl.BlockSpec((tm, W), lambda i: (i, 0))
    return _pallas(
        body, name=name, grid=(n,),
        in_specs=[row, row],
        out_specs=[row, pl.BlockSpec((8, LANES), lambda i: (0, 0))],
        out_shape=[jax.ShapeDtypeStruct((S, W), F32), jax.ShapeDtypeStruct((8, LANES), F32)],
        scratch_shapes=[pltpu.VMEM((1, W), F32)],
        compiler_params=_params(("arbitrary",), 8 * tm * W * 4),
    )(y, target)


def _stack_heads(t, zero):
    first = _first_half(t.shape)
    return jnp.concatenate([jnp.where(first, t, zero), jnp.where(first, zero, t)], axis=0)


def _sb_weights(qs, ks, mask, upper, rss):
    zs = [_dot(q, k, NT) for q, k in zip(qs, ks)]
    sps = [jnp.maximum(z, 0.0) + jnp.log(1.0 + jnp.exp(-jnp.abs(z))) for z in zs]
    gs = [z - sp for z, sp in zip(zs, sps)]
    if mask is not None:
        sps = [jnp.where(mask, sp, 0.0) for sp in sps]
    splits = [_split(sp) for sp in sps]
    his = [_dot(hi, upper) for hi, _ in splits]
    los = [_dot(lo, upper) for _, lo in splits]
    avs = [jnp.exp(g - ((hi + lo) + rs)) for g, hi, lo, rs in zip(gs, his, los, rss)]
    if mask is not None:
        avs = [jnp.where(mask, a, 0.0) for a in avs]
    return avs, sps, gs


def _sb_consts():
    row = lax.broadcasted_iota(I32, (BLK, BLK), 0)
    col = lax.broadcasted_iota(I32, (BLK, BLK), 1)
    diag = col < row
    return row, col, jnp.concatenate([diag, diag], axis=0)


SB_DEAD = 105.0


def _sb_walk_left(block, i, carry):
    def least(c):
        m = c[0][1]
        for pair in c[1:]:
            m = jnp.minimum(m, pair[1])
        return jnp.min(m)

    def cond(state):
        jj, _, low = state
        return (jj < i) & (low < SB_DEAD)

    def body(state):
        jj, c, _ = state
        c = block(i - 1 - jj, c, None)
        return jj + 1, c, least(c)

    return lax.while_loop(cond, body, (jnp.int32(0), carry, least(carry)))[1]


SB_W = 1024


def sb_attn_fwd(qkv, *, name, pairs=8):
    S = qkv.shape[0]
    W = SB_W
    PW = pairs * LANES
    ngrp, nq = W // PW, S // BLK

    def body(q_ref, k_ref, v_ref, o_ref):
        i = pl.program_id(1)
        row, col, diag = _sb_consts()
        upper = (row > col).astype(BF16)
        zero = jnp.zeros((BLK, LANES), BF16)
        qs = [_stack_heads(q_ref[:, p * LANES:(p + 1) * LANES] * 0.125, zero) for p in range(pairs)]

        def block(j, carry, mask):
            rows = pl.ds(pl.multiple_of(j * BLK, BLK), BLK)
            cols = [slice(p * LANES, (p + 1) * LANES) for p in range(pairs)]
            avs, sps, _ = _sb_weights(qs, [k_ref[rows, c] for c in cols], mask, upper, [c[1] for c in carry])
            abs_ = [a.astype(BF16) for a in avs]
            outs = [_dot(jnp.concatenate([ab[:BLK], ab[BLK:]], axis=1), _stack_heads(v_ref[rows, c], zero))
                    for ab, c in zip(abs_, cols)]
            return tuple((carry[p][0] + outs[p], carry[p][1] + jnp.sum(sps[p], axis=1, keepdims=True))
                         for p in range(pairs))

        init = tuple((jnp.zeros((BLK, LANES), F32), jnp.zeros((2 * BLK, 1), F32)) for _ in range(pairs))
        carry = _sb_walk_left(block, i, block(i, init, diag))
        for p in range(pairs):
            o_ref[:, p * LANES:(p + 1) * LANES] = carry[p][0]

    once = pl.Buffered(1)
    return _pallas(
        body, name=name, grid=(ngrp, nq),
        in_specs=[pl.BlockSpec((BLK, PW), lambda p, i: (i, p)),
                  pl.BlockSpec((S, PW), lambda p, i: (0, ngrp + p), pipeline_mode=once),
                  pl.BlockSpec((S, PW), lambda p, i: (0, 2 * ngrp + p), pipeline_mode=once)],
        out_specs=pl.BlockSpec((BLK, PW), lambda p, i: (i, p)),
        out_shape=jax.ShapeDtypeStruct((S, W), F32),
        compiler_params=_params(("parallel", "arbitrary"), 2 * S * PW * 2 + 16 * BLK * PW * 4),
    )(qkv, qkv, qkv)


def sb_attn_bwd(qkv, o, do, *, name, pairs=4):
    S = qkv.shape[0]
    W = SB_W
    PW = pairs * LANES
    ngrp, nq = W // PW, S // BLK

    def body(q_ref, k_ref, v_ref, o_ref, do_ref, dq_ref, dk_ref, dv_ref):
        i = pl.program_id(1)

        @pl.when(i == 0)
        def _():
            dk_ref[...] = jnp.zeros_like(dk_ref)
            dv_ref[...] = jnp.zeros_like(dv_ref)

        row, col, diag = _sb_consts()
        upper = (row > col).astype(BF16)
        upper_incl = (row >= col).astype(BF16)
        first = _first_half((BLK, LANES))
        zero = jnp.zeros((BLK, LANES), BF16)
        qs, dos, tots = [], [], []
        for p in range(pairs):
            cols = slice(p * LANES, (p + 1) * LANES)
            qs.append(_stack_heads(q_ref[:, cols] * 0.125, zero))
            dob = do_ref[:, cols].astype(BF16)
            dos.append(_stack_heads(dob, zero))
            prod = dob.astype(F32) * o_ref[:, cols]
            tots.append(jnp.concatenate([jnp.sum(jnp.where(first, prod, 0.0), axis=1, keepdims=True),
                                         jnp.sum(jnp.where(first, 0.0, prod), axis=1, keepdims=True)], axis=0))

        def block(j, carry, mask):
            rows = pl.ds(pl.multiple_of(j * BLK, BLK), BLK)
            P = range(pairs)
            cols = [slice(p * LANES, (p + 1) * LANES) for p in P]
            ks = [k_ref[rows, c] for c in cols]
            das = [_dot(dos[p], v_ref[rows, cols[p]], NT) for p in P]
            avs, sps, gs = _sb_weights(qs, ks, mask, upper, [c[1] for c in carry])
            abs_ = [a.astype(BF16) for a in avs]
            es = [ab.astype(F32) * da for ab, da in zip(abs_, das)]
            splits = [_split(e) for e in es]
            his = [_dot(hi, upper_incl) for hi, _ in splits]
            los = [_dot(lo, upper_incl) for _, lo in splits]
            lefts = [tots[p] - ((his[p] + los[p]) + carry[p][2]) for p in P]
            dzs = [es[p] - jnp.exp(gs[p]) * (es[p] + lefts[p]) for p in P]
            if mask is not None:
                dzs = [jnp.where(mask, dz, 0.0) for dz in dzs]
            dzbs = [dz.astype(BF16) for dz in dzs]
            dks = [_dot(dzbs[p], qs[p], TN) for p in P]
            dvs = [_dot(abs_[p], dos[p], TN) for p in P]
            dqs = [_dot(jnp.concatenate([dzbs[p][:BLK], dzbs[p][BLK:]], axis=1), _stack_heads(ks[p], zero)) for p in P]
            for p in P:
                dk_ref[rows, cols[p]] += dks[p]
                dv_ref[rows, cols[p]] += dvs[p]
            return tuple((carry[p][0] + dqs[p], carry[p][1] + jnp.sum(sps[p], axis=1, keepdims=True),
                          carry[p][2] + jnp.sum(es[p], axis=1, keepdims=True)) for p in P)

        col0 = jnp.zeros((2 * BLK, 1), F32)
        init = tuple((jnp.zeros((BLK, LANES), F32), col0, col0) for _ in range(pairs))
        carry = _sb_walk_left(block, i, block(i, init, diag))
        for p in range(pairs):
            dq_ref[:, p * LANES:(p + 1) * LANES] = (carry[p][0] * 0.125).astype(BF16)

    once = pl.Buffered(1)
    tile = pl.BlockSpec((BLK, PW), lambda p, i: (i, p))
    full = pl.BlockSpec((S, PW), lambda p, i: (0, p), pipeline_mode=once)
    shape = jax.ShapeDtypeStruct((S, W), F32)
    return _pallas(
        body, name=name, grid=(ngrp, nq),
        in_specs=[tile,
                  pl.BlockSpec((S, PW), lambda p, i: (0, ngrp + p), pipeline_mode=once),
                  pl.BlockSpec((S, PW), lambda p, i: (0, 2 * ngrp + p), pipeline_mode=once),
                  tile, tile],
        out_specs=[tile, full, full],
        out_shape=[jax.ShapeDtypeStruct((S, W), BF16), shape, shape],
        compiler_params=_params(("parallel", "arbitrary"), 2 * S * PW * 2 + 2 * S * PW * 4 + 16 * BLK * PW * 4),
    )(qkv, qkv, qkv, o, do)


FB = 256


def flash_fwd(qn, kn, vb, *, name, heads=4):
    S = qn.shape[0]
    H, DK, DV = MLA_HEADS, MLA_PAD, LANES
    nq, ngrp = S // FB, H // heads
    scale = 1.0 / math.sqrt(MLA_QK)

    def body(q_ref, k_ref, v_ref, o_ref, lse_ref):
        i = pl.program_id(1)
        diag = lax.broadcasted_iota(I32, (FB, FB), 1) <= lax.broadcasted_iota(I32, (FB, FB), 0)
        qs = [q_ref[:, h * DK:(h + 1) * DK] for h in range(heads)]

        def block(j, carry, mask):
            rows = pl.ds(pl.multiple_of(j * FB, FB), FB)
            H = range(heads)
            ss = [_dot(qs[h], k_ref[rows, h * DK:(h + 1) * DK], NT) * scale for h in H]
            if mask is not None:
                ss = [jnp.where(mask, s, -1e30) for s in ss]
            ms = [jnp.maximum(carry[h][1], jnp.max(ss[h], axis=1, keepdims=True)) for h in H]
            ps = [jnp.exp(ss[h] - ms[h]) for h in H]
            ws = [jnp.exp(carry[h][1] - ms[h]) for h in H]
            pvs = [_dot(ps[h].astype(BF16), v_ref[rows, h * DV:(h + 1) * DV]) for h in H]
            return tuple((carry[h][0] * ws[h] + pvs[h], ms[h], carry[h][2] * ws[h] + jnp.sum(ps[h], axis=1, keepdims=True))
                         for h in H)

        init = tuple((jnp.zeros((FB, DV), F32), jnp.full((FB, 1), -1e30, F32), jnp.zeros((FB, 1), F32))
                     for _ in range(heads))
        carry = lax.fori_loop(0, i, lambda j, c: block(j, c, None), init)
        carry = block(i, carry, diag)
        for h in range(heads):
            acc, m, l = carry[h]
            o_ref[:, h * DV:(h + 1) * DV] = acc / l
            lse_ref[h] = m + jnp.log(l)

    return _pallas(
        body, name=name, grid=(ngrp, nq),
        in_specs=[pl.BlockSpec((FB, heads * DK), lambda g, i: (i, g)),
                  pl.BlockSpec((S, heads * DK), lambda g, i: (0, g), pipeline_mode=pl.Buffered(1)),
                  pl.BlockSpec((S, heads * DV), lambda g, i: (0, g), pipeline_mode=pl.Buffered(1))],
        out_specs=[pl.BlockSpec((FB, heads * DV), lambda g, i: (i, g)), pl.BlockSpec((heads, FB, 1), lambda g, i: (g, i, 0))],
        out_shape=[jax.ShapeDtypeStruct((S, H * DV), F32), jax.ShapeDtypeStruct((H, S, 1), F32)],
        compiler_params=_params(("parallel", "arbitrary"), 2 * S * heads * (DK + DV) * 2 + 16 * FB * FB * 4),
    )(qn, kn, vb)


def flash_bwd(qn, kn, vb, o, lse, do, *, name, heads=4):
    S = qn.shape[0]
    H, DK, DV = MLA_HEADS, MLA_PAD, LANES
    nq, ngrp = S // FB, H // heads
    scale = 1.0 / math.sqrt(MLA_QK)

    def body(q_ref, k_ref, v_ref, o_ref, lse_ref, do_ref, dq_ref, dk_ref, dv_ref):
        i = pl.program_id(1)

        @pl.when(i == 0)
        def _():
            dk_ref[...] = jnp.zeros_like(dk_ref)
            dv_ref[...] = jnp.zeros_like(dv_ref)

        diag = lax.broadcasted_iota(I32, (FB, FB), 1) <= lax.broadcasted_iota(I32, (FB, FB), 0)
        qs, dobs, deltas, lses = [], [], [], []
        for h in range(heads):
            do = do_ref[:, h * DV:(h + 1) * DV]
            qs.append(q_ref[:, h * DK:(h + 1) * DK])
            dobs.append(do.astype(BF16))
            deltas.append(jnp.sum(do * o_ref[:, h * DV:(h + 1) * DV], axis=1, keepdims=True))
            lses.append(lse_ref[h])

        def block(j, carry, mask):
            rows = pl.ds(pl.multiple_of(j * FB, FB), FB)
            H = range(heads)
            kcs = [slice(h * DK, (h + 1) * DK) for h in H]
            vcs = [slice(h * DV, (h + 1) * DV) for h in H]
            ks = [k_ref[rows, kcs[h]] for h in H]
            ss = [_dot(qs[h], ks[h], NT) for h in H]
            dps = [_dot(dobs[h], v_ref[rows, vcs[h]], NT) for h in H]
            ps = [jnp.exp(ss[h] * scale - lses[h]) for h in H]
            if mask is not None:
                ps = [jnp.where(mask, p, 0.0) for p in ps]
            pbs = [p.astype(BF16) for p in ps]
            dss = [(ps[h] * (dps[h] - deltas[h]) * scale).astype(BF16) for h in H]
            dvs = [_dot(pbs[h], dobs[h], TN) for h in H]
            dks = [_dot(dss[h], qs[h], TN) for h in H]
            dqs = [_dot(dss[h], ks[h]) for h in H]
            for h in H:
                dv_ref[rows, vcs[h]] += dvs[h]
                dk_ref[rows, kcs[h]] += dks[h]
            return tuple(carry[h] + dqs[h] for h in H)

        carry = lax.fori_loop(0, i, lambda j, c: block(j, c, None), tuple(jnp.zeros((FB, DK), F32) for _ in range(heads)))
        carry = block(i, carry, diag)
        for h in range(heads):
            dq_ref[:, h * DK:(h + 1) * DK] = carry[h]

    qtile = pl.BlockSpec((FB, heads * DK), lambda g, i: (i, g))
    otile = pl.BlockSpec((FB, heads * DV), lambda g, i: (i, g))
    once = pl.Buffered(1)
    kfull = pl.BlockSpec((S, heads * DK), lambda g, i: (0, g), pipeline_mode=once)
    vfull = pl.BlockSpec((S, heads * DV), lambda g, i: (0, g), pipeline_mode=once)
    return _pallas(
        body, name=name, grid=(ngrp, nq),
        in_specs=[qtile, kfull, vfull, otile, pl.BlockSpec((heads, FB, 1), lambda g, i: (g, i, 0)), otile],
        out_specs=[qtile, kfull, vfull],
        out_shape=[jax.ShapeDtypeStruct((S, H * DK), F32), jax.ShapeDtypeStruct((S, H * DK), F32),
                   jax.ShapeDtypeStruct((S, H * DV), F32)],
        compiler_params=_params(("parallel", "arbitrary"), S * heads * (DK + DV) * 6 + 16 * FB * FB * 4),
    )(qn, kn, vb, o, lse, do)


def _rope_tables(S):
    half = 32
    inv_freq = ROPE_THETA ** (-jnp.arange(half, dtype=F32) / half)
    ang = jnp.arange(S).astype(F32)[:, None] * inv_freq[None, :]
    cos, sin = jnp.cos(ang), jnp.sin(ang)
    ones, zeros = jnp.ones((S, LANES), F32), jnp.zeros((S, LANES), F32)
    pad = jnp.zeros((S, 64), F32)
    return (jnp.concatenate([ones, cos, cos, pad + 1.0], axis=1),
            jnp.concatenate([zeros, -sin, sin, pad], axis=1))


def _rope_partner(u):
    lane = lax.broadcasted_iota(I32, u.shape, 1)
    return jnp.where((lane % HEAD64) < 32, pltpu.roll(u, LANES - 32, 1), pltpu.roll(u, 32, 1))


def _normrope(raw, g, cos, sgn):
    r = lax.rsqrt(jnp.sum(raw * raw, axis=1, keepdims=True) * (1.0 / MLA_QK) + NORM_EPS)
    y = raw * r
    u = y * g
    pe = u[:, LANES:]
    out = jnp.concatenate([u[:, :LANES], pe * cos[:, LANES:] + _rope_partner(pe) * sgn[:, LANES:]], axis=1)
    return out, y, r


def _normrope_bwd(dout, g, cos, sgn, y, r):
    dpe = dout[:, LANES:]
    du = jnp.concatenate([dout[:, :LANES], dpe * cos[:, LANES:] + _rope_partner(dpe * sgn[:, LANES:])], axis=1)
    dy = du * g
    draw = r * (dy - y * (jnp.sum(dy * y, axis=1, keepdims=True) * (1.0 / MLA_QK)))
    return draw, jnp.sum(du * y, axis=0, keepdims=True)


def mla_prep_fwd(qraw, kv, proj, gq, gk, cos, sgn, *, kpe_off, name, tm=256):
    S = qraw.shape[0]
    tm = _tile(S, tm)
    kb = kpe_off // LANES

    def body(q_ref, kn_ref, v_ref, kpe_ref, gq_ref, gk_ref, c_ref, s_ref, qo_ref, ko_ref, vo_ref):
        cos, sgn = c_ref[...], s_ref[...]
        qo_ref[...] = _normrope(q_ref[...], gq_ref[...], cos, sgn)[0].astype(BF16)
        kraw = jnp.concatenate([kn_ref[...], kpe_ref[...]], axis=1)
        ko_ref[...] = _normrope(kraw, gk_ref[...], cos, sgn)[0].astype(BF16)
        vo_ref[...] = v_ref[...].astype(BF16)

    head = pl.BlockSpec((tm, MLA_PAD), lambda i, h: (i, h))
    gain = pl.BlockSpec((1, MLA_PAD), lambda i, h: (0, 0))
    tab = pl.BlockSpec((tm, MLA_PAD), lambda i, h: (i, 0))
    return _pallas(
        body, name=name, grid=(S // tm, MLA_HEADS),
        in_specs=[head, pl.BlockSpec((tm, LANES), lambda i, h: (i, 2 * h)), pl.BlockSpec((tm, LANES), lambda i, h: (i, 2 * h + 1)),
                  pl.BlockSpec((tm, LANES), lambda i, h: (i, kb)), gain, gain, tab, tab],
        out_specs=[head, head, pl.BlockSpec((tm, LANES), lambda i, h: (i, h))],
        out_shape=[jax.ShapeDtypeStruct(qraw.shape, BF16), jax.ShapeDtypeStruct(qraw.shape, BF16),
                   jax.ShapeDtypeStruct((S, MLA_HEADS * LANES), BF16)],
        compiler_params=_params(("parallel", "arbitrary"), 16 * tm * MLA_PAD * 4),
    )(qraw, kv, kv, proj, gq, gk, cos, sgn)


def mla_prep_bwd(dqn, dkn, dv, qraw, kv, proj, gq, gk, cos, sgn, *, kpe_off, name, tm=256):
    S = qraw.shape[0]
    tm = _tile(S, tm)
    kb = kpe_off // LANES

    def body(dq_ref, dk_ref, dv_ref, q_ref, kn_ref, kpe_ref, gq_ref, gk_ref, c_ref, s_ref,
             dqo_ref, dkv_ref, dkpe_ref, dgq_ref, dgk_ref, acc_ref):
        i, h = pl.program_id(0), pl.program_id(1)

        @pl.when((i == 0) & (h == 0))
        def _():
            dgq_ref[...] = jnp.zeros_like(dgq_ref)
            dgk_ref[...] = jnp.zeros_like(dgk_ref)

        @pl.when(h == 0)
        def _():
            acc_ref[...] = jnp.zeros_like(acc_ref)

        cos, sgn = c_ref[...], s_ref[...]
        _, yq, rq = _normrope(q_ref[...], gq_ref[...], cos, sgn)
        dq, dgq = _normrope_bwd(dq_ref[...], gq_ref[...], cos, sgn, yq, rq)
        dqo_ref[...] = dq.astype(BF16)
        dgq_ref[...] += dgq
        kraw = jnp.concatenate([kn_ref[...], kpe_ref[...]], axis=1)
        _, yk, rk = _normrope(kraw, gk_ref[...], cos, sgn)
        dk, dgk = _normrope_bwd(dk_ref[...], gk_ref[...], cos, sgn, yk, rk)
        dgk_ref[...] += dgk
        dkv_ref[...] = jnp.concatenate([dk[:, :LANES], dv_ref[...]], axis=1).astype(BF16)
        acc_ref[...] += dk[:, LANES:]

        @pl.when(h == MLA_HEADS - 1)
        def _():
            dkpe_ref[...] = acc_ref[...].astype(BF16)

    head = pl.BlockSpec((tm, MLA_PAD), lambda i, h: (i, h))
    gain = pl.BlockSpec((1, MLA_PAD), lambda i, h: (0, 0))
    tab = pl.BlockSpec((tm, MLA_PAD), lambda i, h: (i, 0))
    return _pallas(
        body, name=name, grid=(S // tm, MLA_HEADS),
        in_specs=[head, head, pl.BlockSpec((tm, LANES), lambda i, h: (i, h)), head,
                  pl.BlockSpec((tm, LANES), lambda i, h: (i, 2 * h)), pl.BlockSpec((tm, LANES), lambda i, h: (i, kb)),
                  gain, gain, tab, tab],
        out_specs=[head, head, pl.BlockSpec((tm, LANES), lambda i, h: (i, 0)), gain, gain],
        out_shape=[jax.ShapeDtypeStruct(qraw.shape, BF16), jax.ShapeDtypeStruct(qraw.shape, BF16),
                   jax.ShapeDtypeStruct((S, LANES), BF16), jax.ShapeDtypeStruct((1, MLA_PAD), F32),
                   jax.ShapeDtypeStruct((1, MLA_PAD), F32)],
        scratch_shapes=[pltpu.VMEM((tm, LANES), F32)],
        compiler_params=_params(("arbitrary", "arbitrary"), 24 * tm * MLA_PAD * 4),
    )(dqn, dkn, dv, qraw, kv, proj, gq, gk, cos, sgn)


def swa_kv_prep(proj, gk64, *, k_off, v_off, name, tm=512):
    S = proj.shape[0]
    tm = _tile(S, tm)
    W = SWA_KV * HEAD64

    def body(k_ref, v_ref, g_ref, ko_ref, vo_ref):
        first = _first_half((tm, LANES))

        def dup(n):
            nr = pltpu.roll(n, HEAD64, 1)
            return jnp.where(first, n, nr), jnp.where(first, nr, n)

        for t in range(W // LANES):
            x = k_ref[:, t * LANES:(t + 1) * LANES]
            n = x * lax.rsqrt(_segmean64(x * x) + NORM_EPS) * g_ref[...]
            d0, d1 = dup(n)
            ko_ref[:, 2 * t * LANES:(2 * t + 1) * LANES] = d0.astype(BF16)
            ko_ref[:, (2 * t + 1) * LANES:(2 * t + 2) * LANES] = d1.astype(BF16)
            d0, d1 = dup(v_ref[:, t * LANES:(t + 1) * LANES])
            vo_ref[:, 2 * t * LANES:(2 * t + 1) * LANES] = d0.astype(BF16)
            vo_ref[:, (2 * t + 1) * LANES:(2 * t + 2) * LANES] = d1.astype(BF16)

    out = pl.BlockSpec((tm, SWA_KV * LANES), lambda i: (i, 0))
    shape = jax.ShapeDtypeStruct((S, SWA_KV * LANES), BF16)
    return _pallas(
        body, name=name, grid=(S // tm,),
        in_specs=[pl.BlockSpec((tm, W), lambda i: (i, k_off // W)), pl.BlockSpec((tm, W), lambda i: (i, v_off // W)),
                  pl.BlockSpec((1, LANES), lambda i: (0, 0))],
        out_specs=[out, out], out_shape=[shape, shape],
        compiler_params=_params(("parallel",), 12 * tm * W * 4),
    )(proj, proj, jnp.tile(gk64, 2).reshape(1, LANES))


def swa_kv_prep_bwd(dkdup, dvdup, proj, gk64, *, k_off, name, tm=512):
    S = proj.shape[0]
    tm = _tile(S, tm)
    W = SWA_KV * HEAD64

    def body(dk_ref, dv_ref, k_ref, g_ref, dko_ref, dvo_ref, dg_ref):
        @pl.when(pl.program_id(0) == 0)
        def _():
            dg_ref[...] = jnp.zeros_like(dg_ref)

        first = _first_half((tm, LANES))

        def fold(ref, t):
            d0 = ref[:, 2 * t * LANES:(2 * t + 1) * LANES]
            d1 = ref[:, (2 * t + 1) * LANES:(2 * t + 2) * LANES]
            return jnp.where(first, d0 + pltpu.roll(d0, HEAD64, 1), d1 + pltpu.roll(d1, HEAD64, 1))

        for t in range(W // LANES):
            dvo_ref[:, t * LANES:(t + 1) * LANES] = fold(dv_ref, t).astype(BF16)
            dh = fold(dk_ref, t)
            x = k_ref[:, t * LANES:(t + 1) * LANES]
            r = lax.rsqrt(_segmean64(x * x) + NORM_EPS)
            y = x * r
            dy = dh * g_ref[...]
            dko_ref[:, t * LANES:(t + 1) * LANES] = (r * (dy - y * _segmean64(dy * y))).astype(BF16)
            dg_ref[...] += jnp.sum(dh * y, axis=0, keepdims=True)

    dup = pl.BlockSpec((tm, SWA_KV * LANES), lambda i: (i, 0))
    out = pl.BlockSpec((tm, W), lambda i: (i, 0))
    shape = jax.ShapeDtypeStruct((S, W), BF16)
    return _pallas(
        body, name=name, grid=(S // tm,),
        in_specs=[dup, dup, pl.BlockSpec((tm, W), lambda i: (i, k_off // W)), pl.BlockSpec((1, LANES), lambda i: (0, 0))],
        out_specs=[out, out, pl.BlockSpec((1, LANES), lambda i: (0, 0))],
        out_shape=[shape, shape, jax.ShapeDtypeStruct((1, LANES), F32)],
        compiler_params=_params(("arbitrary",), 16 * tm * W * 4),
    )(dkdup, dvdup, proj, jnp.tile(gk64, 2).reshape(1, LANES))


def _swa_geometry(i):
    r = lax.broadcasted_iota(I32, (BLK, 2 * BLK), 0)
    c = lax.broadcasted_iota(I32, (BLK, 2 * BLK), 1)
    rel = r + BLK - c
    valid = (rel >= 0) & (rel < BLK) & ((c >= BLK) | (i > 0))
    return valid, rel.astype(F32)


def _swa_slope(h):
    return 2.0 ** (-8.0 * (h + 1) / SWA_HEADS)


def swa_attn_fwd(qn, kdup, vdup, sinks, *, name):
    S = qn.shape[0]
    nq = S // BLK
    group = SWA_HEADS // SWA_KV

    def body(q_ref, kp_ref, kc_ref, vp_ref, vc_ref, sink_ref, o_ref, lse_ref):
        i = pl.program_id(0)
        valid, rel = _swa_geometry(i)
        first = _first_half((BLK, LANES))
        lane = lax.broadcasted_iota(I32, (BLK, LANES), 1)
        lse_all = jnp.zeros((BLK, LANES), F32)
        for g in range(SWA_KV):
            cols = slice(g * LANES, (g + 1) * LANES)
            kk = jnp.concatenate([kp_ref[:, cols], kc_ref[:, cols]], axis=0)
            vv = jnp.concatenate([vp_ref[:, cols], vc_ref[:, cols]], axis=0)
            for pair in range(group // 2):
                tile = (g * group) // 2 + pair
                q2 = q_ref[:, tile * LANES:(tile + 1) * LANES]
                outs = []
                for a in range(2):
                    h = 2 * tile + a
                    qh = jnp.where(first if a == 0 else ~first, q2, jnp.zeros_like(q2))
                    s = _dot(qh, kk, NT) * (1.0 / math.sqrt(HEAD64)) - _swa_slope(h) * rel
                    s = jnp.where(valid, s, -1e30)
                    sink = sink_ref[h]
                    m = jnp.maximum(jnp.max(s, axis=1, keepdims=True), sink)
                    e = jnp.exp(s - m)
                    den = jnp.sum(e, axis=1, keepdims=True) + jnp.exp(sink - m)
                    outs.append(_dot((e / den).astype(BF16), vv))
                    lse_all = jnp.where(lane == h, m + jnp.log(den), lse_all)
                o_ref[:, tile * LANES:(tile + 1) * LANES] = jnp.where(first, outs[0], outs[1])
        lse_ref[...] = lse_all

    prev = lambda i: (jnp.maximum(i - 1, 0), 0)
    cur = lambda i: (i, 0)
    kvw = SWA_KV * LANES
    return _pallas(
        body, name=name, grid=(nq,),
        in_specs=[pl.BlockSpec((BLK, 1024), cur), pl.BlockSpec((BLK, kvw), prev), pl.BlockSpec((BLK, kvw), cur),
                  pl.BlockSpec((BLK, kvw), prev), pl.BlockSpec((BLK, kvw), cur),
                  pl.BlockSpec(memory_space=pltpu.SMEM)],
        out_specs=[pl.BlockSpec((BLK, 1024), cur), pl.BlockSpec((BLK, LANES), cur)],
        out_shape=[jax.ShapeDtypeStruct((S, 1024), F32), jax.ShapeDtypeStruct((S, LANES), F32)],
        compiler_params=_params(("parallel",), 16 * BLK * 1024 * 4),
    )(qn, kdup, kdup, vdup, vdup, sinks)


def swa_attn_bwd(qn, kdup, vdup, sinks, o, lse, do, *, name):
    S = qn.shape[0]
    nq = S // BLK
    group = SWA_HEADS // SWA_KV
    scale = 1.0 / math.sqrt(HEAD64)

    def body(q_ref, kp_ref, kc_ref, vp_ref, vc_ref, sink_ref, o_ref, lse_ref, do_ref,
             dq_ref, dk_ref, dv_ref, ds_ref):
        i = pl.program_id(0)

        @pl.when(i == 0)
        def _():
            dk_ref[...] = jnp.zeros_like(dk_ref)
            dv_ref[...] = jnp.zeros_like(dv_ref)
            ds_ref[...] = jnp.zeros_like(ds_ref)

        valid, rel = _swa_geometry(i)
        first = _first_half((BLK, LANES))
        lane1 = lax.broadcasted_iota(I32, (1, LANES), 1)
        lane = lax.broadcasted_iota(I32, (BLK, LANES), 1)
        lse_all = lse_ref[...]
        prow = pl.ds(pl.multiple_of(jnp.maximum(i - 1, 0) * BLK, BLK), BLK)
        crow = pl.ds(pl.multiple_of(i * BLK, BLK), BLK)
        dsink = jnp.zeros((1, LANES), F32)
        for g in range(SWA_KV):
            cols = slice(g * LANES, (g + 1) * LANES)
            kk = jnp.concatenate([kp_ref[:, cols], kc_ref[:, cols]], axis=0)
            vv = jnp.concatenate([vp_ref[:, cols], vc_ref[:, cols]], axis=0)
            dkk = jnp.zeros((2 * BLK, LANES), F32)
            dvv = jnp.zeros((2 * BLK, LANES), F32)
            for pair in range(group // 2):
                tile = (g * group) // 2 + pair
                tcols = slice(tile * LANES, (tile + 1) * LANES)
                q2 = q_ref[:, tcols]
                do2 = do_ref[:, tcols]
                prod = do2 * o_ref[:, tcols]
                dqs = []
                for a in range(2):
                    h = 2 * tile + a
                    mine = first if a == 0 else ~first
                    qh = jnp.where(mine, q2, jnp.zeros_like(q2))
                    doh = jnp.where(mine, do2, 0.0).astype(BF16)
                    delta = jnp.sum(jnp.where(mine, prod, 0.0), axis=1, keepdims=True)
                    lse_h = jnp.sum(jnp.where(lane == h, lse_all, 0.0), axis=1, keepdims=True)
                    s = _dot(qh, kk, NT) * scale - _swa_slope(h) * rel
                    p = jnp.where(valid, jnp.exp(s - lse_h), 0.0)
                    dsc = (p * (_dot(doh, vv, NT) - delta) * scale).astype(BF16)
                    dqs.append(_dot(dsc, kk))
                    dkk = dkk + _dot(dsc, qh, TN)
                    dvv = dvv + _dot(p.astype(BF16), doh, TN)
                    psink = jnp.exp(sink_ref[h] - lse_h)
                    dsink = dsink + jnp.where(lane1 == h, -jnp.sum(psink * delta), 0.0)
                dq_ref[:, tcols] = jnp.where(first, dqs[0], dqs[1])
            dk_ref[prow, cols] += dkk[:BLK]
            dv_ref[prow, cols] += dvv[:BLK]
            dk_ref[crow, cols] += dkk[BLK:]
            dv_ref[crow, cols] += dvv[BLK:]
        ds_ref[...] += dsink

    prev = lambda i: (jnp.maximum(i - 1, 0), 0)
    cur = lambda i: (i, 0)
    kvw = SWA_KV * LANES
    whole = pl.BlockSpec((S, kvw), lambda i: (0, 0))
    return _pallas(
        body, name=name, grid=(nq,),
        in_specs=[pl.BlockSpec((BLK, 1024), cur), pl.BlockSpec((BLK, kvw), prev), pl.BlockSpec((BLK, kvw), cur),
                  pl.BlockSpec((BLK, kvw), prev), pl.BlockSpec((BLK, kvw), cur),
                  pl.BlockSpec(memory_space=pltpu.SMEM),
                  pl.BlockSpec((BLK, 1024), cur), pl.BlockSpec((BLK, LANES), cur), pl.BlockSpec((BLK, 1024), cur)],
        out_specs=[pl.BlockSpec((BLK, 1024), cur), whole, whole, pl.BlockSpec((1, LANES), lambda i: (0, 0))],
        out_shape=[jax.ShapeDtypeStruct((S, 1024), F32), jax.ShapeDtypeStruct((S, kvw), F32),
                   jax.ShapeDtypeStruct((S, kvw), F32), jax.ShapeDtypeStruct((1, LANES), F32)],
        compiler_params=_params(("arbitrary",), 4 * S * kvw * 4 + 24 * BLK * 1024 * 4),
    )(qn, kdup, kdup, vdup, vdup, sinks, o, lse, do)


def _in_bwd(x, h, dproj, dy, g, w_in, tag):
    dh = matmul(dproj, w_in, "nt", name=f"{tag}_dh")
    dw_in = matmul(h, dproj, "tn", name=f"{tag}_dwin")
    dx, dg = rmsnorm_bwd(x, g, dh, col_off=0, width=D_MODEL, out_dtype=F32, res=dy, name=f"{tag}_dnorm")
    return dx, dw_in, dg


def _out_bwd(dy, og, o, proj, w_out, gate_off, tag):
    dog = matmul(dy, w_out, "nt", name=f"{tag}_dog")
    dw_out = matmul(og, dy, "tn", name=f"{tag}_dwout")
    do, dgate = gate_bwd(dog, o, proj, gate_off=gate_off, name=f"{tag}_dgate")
    return do, dgate, dw_out


def sb_fwd(x, p, tag):
    h = rmsnorm_fwd(x, p["norm"], col_off=0, width=D_MODEL, out_dtype=BF16, name=f"{tag}_norm")
    proj, projb = matmul(h, p["w_in"], "nn", out_dtype=(F32, BF16), name=f"{tag}_proj")
    o = sb_attn_fwd(projb, name=f"{tag}_attn")
    og = gate_fwd(o, proj, gate_off=3 * D_MODEL, name=f"{tag}_gate")
    y = matmul(og, p["w_out"], "nn", add=x, name=f"{tag}_out")
    return y, (x, h, proj, projb, o, og)


def sb_bwd(dy, saved, p, tag):
    x, h, proj, projb, o, og = saved
    do, dgate, dw_out = _out_bwd(dy, og, o, proj, p["w_out"], 3 * D_MODEL, tag)
    dq, dk, dv = sb_attn_bwd(projb, o, do, name=f"{tag}_dattn")
    dproj = jnp.concatenate([dq, dk.astype(BF16), dv.astype(BF16), dgate], axis=1)
    dx, dw_in, dg = _in_bwd(x, h, dproj, dy, p["norm"], p["w_in"], tag)
    return dx, {"norm": dg[0], "w_in": dw_in, "w_out": dw_out}


MLA_GATE, MLA_QLAT, MLA_KVLAT, MLA_KPE, MLA_IN = 0, 1024, 1280, 1408, 1536


def mla_fwd(x, p, tabs, tag):
    cos, sgn = tabs
    h = rmsnorm_fwd(x, p["norm"], col_off=0, width=D_MODEL, out_dtype=BF16, name=f"{tag}_norm")
    proj = matmul(h, p["w_in"], "nn", name=f"{tag}_proj")
    ql = rmsnorm_fwd(proj, p["q_a_norm"], col_off=MLA_QLAT, width=256, out_dtype=BF16, name=f"{tag}_qanorm")
    kvl = rmsnorm_fwd(proj, p["kv_a_norm"], col_off=MLA_KVLAT, width=128, out_dtype=BF16, name=f"{tag}_kvanorm")
    qraw = matmul(ql, p["w_uq"], "nn", name=f"{tag}_uq")
    kv = matmul(kvl, p["w_ukv"], "nn", name=f"{tag}_ukv")
    qn, kn, vb = mla_prep_fwd(qraw, kv, proj, p["gq"], p["gk"], cos, sgn, kpe_off=MLA_KPE, name=f"{tag}_prep")
    o, lse = flash_fwd(qn, kn, vb, name=f"{tag}_attn")
    og = gate_fwd(o, proj, gate_off=MLA_GATE, name=f"{tag}_gate")
    y = matmul(og, p["w_out"], "nn", add=x, name=f"{tag}_out")
    return y, (x, h, proj, ql, kvl, qraw, kv, qn, kn, vb, o, lse, og)


def mla_bwd(dy, saved, p, tabs, tag):
    cos, sgn = tabs
    x, h, proj, ql, kvl, qraw, kv, qn, kn, vb, o, lse, og = saved
    do, dgate, dw_out = _out_bwd(dy, og, o, proj, p["w_out"], MLA_GATE, tag)
    dqn, dkn, dv = flash_bwd(qn, kn, vb, o, lse, do, name=f"{tag}_dattn")
    dqraw, dkv, dkpe, dgq, dgk = mla_prep_bwd(dqn, dkn, dv, qraw, kv, proj, p["gq"], p["gk"], cos, sgn,
                                              kpe_off=MLA_KPE, name=f"{tag}_dprep")
    dql = matmul(dqraw, p["w_uq"], "nt", name=f"{tag}_dql")
    dw_uq = matmul(ql, dqraw, "tn", name=f"{tag}_dwuq")
    dkvl = matmul(dkv, p["w_ukv"], "nt", name=f"{tag}_dkvl")
    dw_ukv = matmul(kvl, dkv, "tn", name=f"{tag}_dwukv")
    dqlat, dgqa = rmsnorm_bwd(proj, p["q_a_norm"], dql, col_off=MLA_QLAT, width=256, out_dtype=BF16, name=f"{tag}_dqanorm")
    dkvlat, dgkva = rmsnorm_bwd(proj, p["kv_a_norm"], dkvl, col_off=MLA_KVLAT, width=128, out_dtype=BF16,
                                name=f"{tag}_dkvanorm")
    dproj = jnp.concatenate([dgate, dqlat, dkvlat, dkpe], axis=1)
    dx, dw_in, dg = _in_bwd(x, h, dproj, dy, p["norm"], p["w_in"], tag)
    return dx, {"norm": dg[0], "w_in": dw_in, "q_a_norm": dgqa[0], "w_uq": dw_uq, "kv_a_norm": dgkva[0],
                "w_ukv": dw_ukv, "gq": dgq[0], "gk": dgk[0], "w_out": dw_out}


SWA_Q, SWA_GATE, SWA_K, SWA_V = 0, 1024, 2048, 2304


def swa_fwd(x, p, tag):
    h = rmsnorm_fwd(x, p["norm"], col_off=0, width=D_MODEL, out_dtype=BF16, name=f"{tag}_norm")
    proj = matmul(h, p["w_in"], "nn", name=f"{tag}_proj")
    qn = headnorm_fwd(proj, p["q_head_norm"], col_off=SWA_Q, width=1024, name=f"{tag}_qnorm")
    kdup, vdup = swa_kv_prep(proj, p["k_head_norm"], k_off=SWA_K, v_off=SWA_V, name=f"{tag}_kvprep")
    o, lse = swa_attn_fwd(qn, kdup, vdup, p["sinks"], name=f"{tag}_attn")
    og = gate_fwd(o, proj, gate_off=SWA_GATE, name=f"{tag}_gate")
    y = matmul(og, p["w_out"], "nn", add=x, name=f"{tag}_out")
    return y, (x, h, proj, qn, kdup, vdup, o, lse, og)


def swa_bwd(dy, saved, p, tag):
    x, h, proj, qn, kdup, vdup, o, lse, og = saved
    do, dgate, dw_out = _out_bwd(dy, og, o, proj, p["w_out"], SWA_GATE, tag)
    dqn, dkdup, dvdup, dsinks = swa_attn_bwd(qn, kdup, vdup, p["sinks"], o, lse, do, name=f"{tag}_dattn")
    dq, dgq = headnorm_bwd(proj, p["q_head_norm"], dqn, col_off=SWA_Q, width=1024, name=f"{tag}_dqnorm")
    dk, dv, dgk = swa_kv_prep_bwd(dkdup, dvdup, proj, p["k_head_norm"], k_off=SWA_K, name=f"{tag}_dkvprep")
    dproj = jnp.concatenate([dq, dgate, dk, dv], axis=1)
    dx, dw_in, dg = _in_bwd(x, h, dproj, dy, p["norm"], p["w_in"], tag)
    return dx, {"norm": dg[0], "w_in": dw_in, "q_head_norm": dgq[0, :HEAD64] + dgq[0, HEAD64:],
                "k_head_norm": dgk[0, :HEAD64] + dgk[0, HEAD64:], "sinks": dsinks[0, :SWA_HEADS], "w_out": dw_out}


def prepare_weights(w):
    l1_in, l2_in = w["l1_w_in"], w["l2_w_in"]
    pad64 = lambda v: jnp.pad(v, (0, MLA_PAD - MLA_QK)).reshape(1, MLA_PAD)
    return [
        {"norm": w["l0_norm"], "w_in": w["l0_w_in"], "w_out": w["l0_w_out"]},
        {"norm": w["l1_norm"],
         "w_in": jnp.concatenate([l1_in[:, 448:], l1_in[:, :448], jnp.zeros((D_MODEL, 64), l1_in.dtype)], axis=1),
         "q_a_norm": w["l1_q_a_norm"], "kv_a_norm": w["l1_kv_a_norm"],
         "w_uq": jnp.pad(w["l1_w_uq"].reshape(256, MLA_HEADS, MLA_QK), ((0, 0), (0, 0), (0, MLA_PAD - MLA_QK))
                         ).reshape(256, MLA_HEADS * MLA_PAD),
         "w_ukv": w["l1_w_ukv"], "gq": pad64(w["l1_q_head_norm"]), "gk": pad64(w["l1_k_head_norm"]),
         "w_out": w["l1_w_out"]},
        {"norm": w["l2_norm"],
         "w_in": jnp.concatenate([l2_in[:, :1024], l2_in[:, 1536:], l2_in[:, 1024:1536]], axis=1),
         "q_head_norm": w["l2_q_head_norm"], "k_head_norm": w["l2_k_head_norm"], "sinks": w["l2_sinks"],
         "w_out": w["l2_w_out"]},
        {"norm": w["l3_norm"], "w_in": w["l3_w_in"], "w_out": w["l3_w_out"]},
    ]


def unprepare_grads(gs):
    g0, g1, g2, g3 = gs
    d1, d2 = g1["w_in"], g2["w_in"]
    return {
        "l0_norm": g0["norm"], "l0_w_in": g0["w_in"], "l0_w_out": g0["w_out"],
        "l1_norm": g1["norm"], "l1_w_in": jnp.concatenate([d1[:, 1024:1472], d1[:, :1024]], axis=1),
        "l1_q_a_norm": g1["q_a_norm"],
        "l1_w_uq": g1["w_uq"].reshape(256, MLA_HEADS, MLA_PAD)[:, :, :MLA_QK].reshape(256, MLA_HEADS * MLA_QK),
        "l1_kv_a_norm": g1["kv_a_norm"], "l1_w_ukv": g1["w_ukv"],
        "l1_q_head_norm": g1["gq"][:MLA_QK], "l1_k_head_norm": g1["gk"][:MLA_QK], "l1_w_out": g1["w_out"],
        "l2_norm": g2["norm"], "l2_w_in": jnp.concatenate([d2[:, :1024], d2[:, 2048:], d2[:, 1024:2048]], axis=1),
        "l2_q_head_norm": g2["q_head_norm"], "l2_k_head_norm": g2["k_head_norm"], "l2_sinks": g2["sinks"],
        "l2_w_out": g2["w_out"],
        "l3_norm": g3["norm"], "l3_w_in": g3["w_in"], "l3_w_out": g3["w_out"],
    }


def local_step(x, target, w):
    ps = prepare_weights(w)
    tabs = _rope_tables(x.shape[0])
    y0, s0 = sb_fwd(x, ps[0], "l0")
    y1, s1 = mla_fwd(y0, ps[1], tabs, "l1")
    y2, s2 = swa_fwd(y1, ps[2], "l2")
    y3, s3 = sb_fwd(y2, ps[3], "l3")
    dy, loss = loss_head(y3, target, name="loss")
    d3, g3 = sb_bwd(dy, s3, ps[3], "l3")
    d2, g2 = swa_bwd(d3, s2, ps[2], "l2")
    d1, g1 = mla_bwd(d2, s1, ps[1], tabs, "l1")
    d0, g0 = sb_bwd(d1, s0, ps[0], "l0")
    return loss, d0, unprepare_grads([g0, g1, g2, g3])


MATS = (("l0_w_in", "col", 1024, 4096), ("l0_w_out", "row", 1024, 1024), ("l1_w_in", "col", 1024, 1472),
        ("l1_w_uq", "col", 256, 1536), ("l1_w_ukv", "col", 128, 2048), ("l1_w_out", "row", 1024, 1024),
        ("l2_w_in", "col", 1024, 2560), ("l2_w_out", "row", 1024, 1024), ("l3_w_in", "col", 1024, 4096),
        ("l3_w_out", "row", 1024, 1024))
N_CHIPS = 4
PACK_W = 1024
HALF_ROWS = 2176
PACK_ROWS = 2 * HALF_ROWS
VECS = (("l0_norm", 0, 0, 1024), ("l1_norm", 1, 0, 1024), ("l2_norm", 2, 0, 1024), ("l3_norm", 3, 0, 1024),
        ("l1_q_a_norm", 4, 0, 256), ("l1_kv_a_norm", 4, 256, 128), ("l1_q_head_norm", 4, 384, 192),
        ("l1_k_head_norm", 4, 576, 192), ("l2_q_head_norm", 4, 768, 64), ("l2_k_head_norm", 4, 832, 64),
        ("l2_sinks", 4, 896, 16))
LOSS_SLOT = (4, 912)
VEC_ROWS = 8


def _shard_rows(k, n):
    return k * n // N_CHIPS // PACK_W


def pack_shards(shards):
    parts = [shards[name].reshape(-1, PACK_W) for name, _, _, _ in MATS]
    used = sum(p.shape[0] for p in parts)
    return jnp.concatenate(parts + [jnp.zeros((PACK_ROWS - used, PACK_W), parts[0].dtype)], axis=0)


def unpack_shards(flat):
    out, r0 = {}, 0
    for name, kind, k, n in MATS:
        rows = _shard_rows(k, n)
        shape = (k, n // N_CHIPS) if kind == "col" else (k // N_CHIPS, n)
        out[name] = flat[r0:r0 + rows].reshape(shape)
        r0 += rows
    return out


def pack_full(mats):
    parts = []
    for name, kind, k, n in MATS:
        m = mats[name]
        if kind == "col":
            m = m.reshape(k, N_CHIPS, n // N_CHIPS).transpose(1, 0, 2)
        parts.append(m.reshape(N_CHIPS, -1, PACK_W))
    used = sum(p.shape[1] for p in parts)
    return jnp.concatenate(parts + [jnp.zeros((N_CHIPS, PACK_ROWS - used, PACK_W), parts[0].dtype)], axis=1)


def unpack_full(stacked):
    out, r0 = {}, 0
    for name, kind, k, n in MATS:
        rows = _shard_rows(k, n)
        seg = stacked[:, r0:r0 + rows]
        if kind == "col":
            out[name] = seg.reshape(N_CHIPS, k, n // N_CHIPS).transpose(1, 0, 2).reshape(k, n)
        else:
            out[name] = seg.reshape(k, n)
        r0 += rows
    return out


def pack_vecs(vecs, loss=None):
    rows = []
    for r in range(VEC_ROWS):
        items = [(off, vecs[name]) for name, rr, off, _ in VECS if rr == r]
        if loss is not None and r == LOSS_SLOT[0]:
            items.append((LOSS_SLOT[1], loss.reshape(1)))
        pos, parts = 0, []
        for off, v in sorted(items, key=lambda t: t[0]):
            assert off == pos
            parts.append(v.astype(F32))
            pos += v.shape[0]
        parts.append(jnp.zeros((PACK_W - pos,), F32))
        rows.append(jnp.concatenate(parts))
    return jnp.stack(rows)


def unpack_vecs(block):
    return {name: block[r, off:off + n] for name, r, off, n in VECS}


def _me():
    return lax.axis_index("x"), lax.axis_index("y"), lax.axis_index("c")


OTHER_CHIPS = ((1, 0), (0, 1), (1, 1))


def _remote(src, dst, send_sem, recv_sem, to):
    return pltpu.make_async_remote_copy(src_ref=src, dst_ref=dst, send_sem=send_sem, recv_sem=recv_sem,
                                        device_id=to, device_id_type=MESH)


def gather_weights(block):
    def body(in_ref, out_ref, send_sems, recv_sems):
        x, y, c = _me()
        sibling = (x, y, 1 - c)

        def half(px, py, pc):
            return out_ref.at[2 * px + py, pl.ds(pc * HALF_ROWS, HALF_ROWS), :]

        chips = [(x ^ dx, y ^ dy) for dx, dy in OTHER_CHIPS]
        first = [_remote(in_ref.at[pl.ds(c * HALF_ROWS, HALF_ROWS), :], half(x, y, c), send_sems.at[j], recv_sems.at[j],
                         (*chip, c)) for j, chip in enumerate(chips)]
        for cp in first:
            cp.start()
        passed = [_remote(half(*chip, c), half(*chip, c), send_sems.at[3 + j], recv_sems.at[3 + j], sibling)
                  for j, chip in enumerate(chips)]
        for j, chip in enumerate(chips):
            _remote(half(*chip, c), half(*chip, c), send_sems.at[j], recv_sems.at[j], (*chip, c)).wait_recv()
            passed[j].start()
        for j, chip in enumerate(chips):
            _remote(half(*chip, 1 - c), half(*chip, 1 - c), send_sems.at[3 + j], recv_sems.at[3 + j], sibling).wait_recv()
        for cp in first + passed:
            cp.wait_send()

    hbm = pl.BlockSpec(memory_space=pltpu.HBM)
    others = pl.pallas_call(
        body, name="gather_weights",
        out_shape=jax.ShapeDtypeStruct((N_CHIPS, PACK_ROWS, PACK_W), block.dtype),
        in_specs=[hbm], out_specs=hbm,
        scratch_shapes=[pltpu.SemaphoreType.DMA((6,)), pltpu.SemaphoreType.DMA((6,))],
    )(block)
    return lax.dynamic_update_slice(others, block[None], (2 * lax.axis_index("x") + lax.axis_index("y"), 0, 0))


def pair_exchange(g):
    def body(g_ref, out_ref, send_sems, recv_sems):
        x, y, c = _me()
        sibling = (x, y, 1 - c)
        copies = [_remote(g_ref.at[k, pl.ds((1 - c) * HALF_ROWS, HALF_ROWS), :], out_ref.at[k], send_sems.at[k],
                          recv_sems.at[k], sibling) for k in range(N_CHIPS)]
        for cp in copies:
            cp.start()
        for cp in copies:
            cp.wait_recv()
        for cp in copies:
            cp.wait_send()

    hbm = pl.BlockSpec(memory_space=pltpu.HBM)
    return pl.pallas_call(
        body, name="pair_exchange",
        out_shape=jax.ShapeDtypeStruct((N_CHIPS, HALF_ROWS, PACK_W), g.dtype),
        in_specs=[hbm], out_specs=hbm,
        scratch_shapes=[pltpu.SemaphoreType.DMA((N_CHIPS,)), pltpu.SemaphoreType.DMA((N_CHIPS,))],
    )(g)


def chip_exchange(part):
    def body(p_ref, out_ref, send_sems, recv_sems):
        x, y, c = _me()
        copies = [_remote(p_ref.at[2 * (x ^ dx) + (y ^ dy)], out_ref.at[j], send_sems.at[j], recv_sems.at[j],
                          (x ^ dx, y ^ dy, c)) for j, (dx, dy) in enumerate(OTHER_CHIPS)]
        for cp in copies:
            cp.start()
        for cp in copies:
            cp.wait_recv()
        for cp in copies:
            cp.wait_send()

    hbm = pl.BlockSpec(memory_space=pltpu.HBM)
    return pl.pallas_call(
        body, name="chip_exchange",
        out_shape=jax.ShapeDtypeStruct((len(OTHER_CHIPS), HALF_ROWS, PACK_W), part.dtype),
        in_specs=[hbm], out_specs=hbm,
        scratch_shapes=[pltpu.SemaphoreType.DMA((3,)), pltpu.SemaphoreType.DMA((3,))],
    )(part)


def join_halves(half):
    def body(h_ref, out_ref, send_sem, recv_sem):
        x, y, c = _me()
        cp = _remote(h_ref, out_ref.at[pl.ds(c * HALF_ROWS, HALF_ROWS), :], send_sem, recv_sem, (x, y, 1 - c))
        cp.start()
        _remote(h_ref, out_ref.at[pl.ds((1 - c) * HALF_ROWS, HALF_ROWS), :], send_sem, recv_sem, (x, y, 1 - c)).wait_recv()
        cp.wait_send()

    hbm = pl.BlockSpec(memory_space=pltpu.HBM)
    other = pl.pallas_call(
        body, name="join_halves",
        out_shape=jax.ShapeDtypeStruct((PACK_ROWS, PACK_W), half.dtype),
        in_specs=[hbm], out_specs=hbm,
        scratch_shapes=[pltpu.SemaphoreType.DMA, pltpu.SemaphoreType.DMA],
    )(half)
    return lax.dynamic_update_slice(other, half, (lax.axis_index("c") * HALF_ROWS, 0))


def sum_over_devices(block):
    def body(in_ref, out_ref, all_ref, send_sems, recv_sems):
        x, y, c = _me()
        me = 4 * x + 2 * y + c
        all_ref[me] = in_ref[...]
        copies = []
        for r in range(1, 8):
            to = (x ^ (r >> 2), y ^ ((r >> 1) & 1), c ^ (r & 1))
            copies.append(_remote(in_ref, all_ref.at[me], send_sems.at[r - 1], recv_sems.at[r - 1], to))
        for cp in copies:
            cp.start()
        for r in range(1, 8):
            frm = (x ^ (r >> 2), y ^ ((r >> 1) & 1), c ^ (r & 1))
            _remote(in_ref, all_ref.at[4 * frm[0] + 2 * frm[1] + frm[2]], send_sems.at[r - 1], recv_sems.at[r - 1],
                    frm).wait_recv()
        for cp in copies:
            cp.wait_send()
        acc = all_ref[0]
        for d in range(1, 8):
            acc = acc + all_ref[d]
        out_ref[...] = acc

    vmem = pl.BlockSpec(memory_space=pltpu.VMEM)
    return pl.pallas_call(
        body, name="sum_over_devices",
        out_shape=jax.ShapeDtypeStruct(block.shape, F32),
        in_specs=[vmem], out_specs=vmem,
        scratch_shapes=[pltpu.VMEM((8,) + block.shape, F32), pltpu.SemaphoreType.DMA((7,)), pltpu.SemaphoreType.DMA((7,))],
    )(block)


SUM_ROWS = 272


def pair_sum(g, got, core):
    steps = HALF_ROWS // SUM_ROWS

    def body(s_ref, g_ref, r_ref, o_ref, ob_ref):
        t = g_ref[...] + r_ref[...]
        o_ref[...] = t
        ob_ref[...] = t.astype(BF16)

    blk = pl.BlockSpec((1, SUM_ROWS, PACK_W), lambda k, i, s: (k, i, 0))
    return pl.pallas_call(
        body, name="pair_sum",
        grid_spec=pltpu.PrefetchScalarGridSpec(
            num_scalar_prefetch=1, grid=(N_CHIPS, steps),
            in_specs=[pl.BlockSpec((1, SUM_ROWS, PACK_W), lambda k, i, s: (k, s[0] * steps + i, 0)), blk],
            out_specs=[blk, blk]),
        out_shape=[jax.ShapeDtypeStruct((N_CHIPS, HALF_ROWS, PACK_W), F32),
                   jax.ShapeDtypeStruct((N_CHIPS, HALF_ROWS, PACK_W), BF16)],
        compiler_params=_params(("parallel", "parallel"), 8 * SUM_ROWS * PACK_W * 4),
    )(core, g, got)


def chip_sum(part, got, chip):
    steps = HALF_ROWS // SUM_ROWS

    def body(s_ref, p_ref, a_ref, b_ref, c_ref, o_ref):
        o_ref[...] = ((p_ref[0] + a_ref[0].astype(F32)) + b_ref[0].astype(F32)) + c_ref[0].astype(F32)

    def got_spec(j):
        return pl.BlockSpec((1, SUM_ROWS, PACK_W), lambda i, s: (j, i, 0))

    return pl.pallas_call(
        body, name="chip_sum",
        grid_spec=pltpu.PrefetchScalarGridSpec(
            num_scalar_prefetch=1, grid=(steps,),
            in_specs=[pl.BlockSpec((1, SUM_ROWS, PACK_W), lambda i, s: (s[0], i, 0)), got_spec(0), got_spec(1), got_spec(2)],
            out_specs=pl.BlockSpec((SUM_ROWS, PACK_W), lambda i, s: (i, 0))),
        out_shape=jax.ShapeDtypeStruct((HALF_ROWS, PACK_W), F32),
        compiler_params=_params(("parallel",), 10 * SUM_ROWS * PACK_W * 4),
    )(chip, part, got, got, got)


def reduce_scatter(g):
    x, y, c = _me()
    part, part_bf16 = pair_sum(g, pair_exchange(g), jnp.reshape(c, (1,)).astype(I32))
    half = chip_sum(part, chip_exchange(part_bf16), jnp.reshape(2 * x + y, (1,)).astype(I32))
    return join_halves(half)


def adamw(w, g, m, v, *, name):
    rows, cols = w.shape
    tm = _tile(rows, 256)
    c1 = 1.0 - ADAM_B1 ** ADAM_STEP
    c2 = 1.0 - ADAM_B2 ** ADAM_STEP

    def body(w_ref, g_ref, m_ref, v_ref, d_ref, mo_ref, vo_ref):
        gv = g_ref[...]
        mn = ADAM_B1 * m_ref[...] + (1.0 - ADAM_B1) * gv
        vn = ADAM_B2 * v_ref[...] + (1.0 - ADAM_B2) * (gv * gv)
        d_ref[...] = -ADAM_LR * ((mn / c1) / (jnp.sqrt(vn / c2) + ADAM_EPS) + ADAM_WD * w_ref[...])
        mo_ref[...] = mn
        vo_ref[...] = vn

    blk = pl.BlockSpec((tm, cols), lambda i: (i, 0))
    shape = jax.ShapeDtypeStruct(w.shape, F32)
    return _pallas(
        body, name=name, grid=(rows // tm,),
        in_specs=[blk] * 4, out_specs=[blk] * 3, out_shape=[shape] * 3,
        compiler_params=_params(("parallel",), 16 * tm * cols * 4),
    )(w, g, m, v)


WEIGHTS = ("l0_norm", "l0_w_in", "l0_w_out", "l1_norm", "l1_w_in", "l1_q_a_norm", "l1_w_uq", "l1_kv_a_norm", "l1_w_ukv",
           "l1_q_head_norm", "l1_k_head_norm", "l1_w_out", "l2_norm", "l2_w_in", "l2_q_head_norm", "l2_k_head_norm",
           "l2_sinks", "l2_w_out", "l3_norm", "l3_w_in", "l3_w_out")


def kernel(x, l0_norm, l0_w_in, l0_w_out, l1_norm, l1_w_in, l1_q_a_norm, l1_w_uq, l1_kv_a_norm, l1_w_ukv, l1_q_head_norm, l1_k_head_norm, l1_w_out, l2_norm, l2_w_in, l2_q_head_norm, l2_k_head_norm, l2_sinks, l2_w_out, l3_norm, l3_w_in, l3_w_out, loss_target, m_l0_norm, m_l0_w_in, m_l0_w_out, m_l1_norm, m_l1_w_in, m_l1_q_a_norm, m_l1_w_uq, m_l1_kv_a_norm, m_l1_w_ukv, m_l1_q_head_norm, m_l1_k_head_norm, m_l1_w_out, m_l2_norm, m_l2_w_in, m_l2_q_head_norm, m_l2_k_head_norm, m_l2_sinks, m_l2_w_out, m_l3_norm, m_l3_w_in, m_l3_w_out, v_l0_norm, v_l0_w_in, v_l0_w_out, v_l1_norm, v_l1_w_in, v_l1_q_a_norm, v_l1_w_uq, v_l1_kv_a_norm, v_l1_w_ukv, v_l1_q_head_norm, v_l1_k_head_norm, v_l1_w_out, v_l2_norm, v_l2_w_in, v_l2_q_head_norm, v_l2_k_head_norm, v_l2_sinks, v_l2_w_out, v_l3_norm, v_l3_w_in, v_l3_w_out):
    given = dict(locals())
    w = {n: given[n] for n in WEIGHTS}
    m = {n: given["m_" + n] for n in WEIGHTS}
    v = {n: given["v_" + n] for n in WEIGHTS}
    mat_names = [t[0] for t in MATS]
    vec_names = [t[0] for t in VECS]

    gathered = gather_weights(pack_shards({n: w[n] for n in mat_names}).astype(BF16))
    full = unpack_full(gathered)
    full.update({n: w[n] for n in vec_names})
    loss_tile, grad_x, grads = local_step(x[0], loss_target[0], full)

    block = reduce_scatter(pack_full({n: grads[n] for n in mat_names}))
    g = unpack_shards(block)
    vec_sum = sum_over_devices(pack_vecs({n: grads[n] for n in vec_names}, loss=loss_tile[0, 0]))
    loss = vec_sum[LOSS_SLOT[0], LOSS_SLOT[1]]

    delta, new_m, new_v = {}, {}, {}
    for n in mat_names:
        delta[n], new_m[n], new_v[n] = adamw(w[n], g[n], m[n], v[n], name=f"adamw_{n}")
    dv, mv, vv = adamw(pack_vecs(w), vec_sum, pack_vecs(m), pack_vecs(v), name="adamw_vecs")
    g.update(unpack_vecs(vec_sum))
    delta.update(unpack_vecs(dv))
    new_m.update(unpack_vecs(mv))
    new_v.update(unpack_vecs(vv))
    return (loss, grad_x[None], *[g[n] for n in WEIGHTS], *[delta[n] for n in WEIGHTS],
            *[new_m[n] for n in WEIGHTS], *[new_v[n] for n in WEIGHTS])
```

```python
import math

import jax
import jax.numpy as jnp
from jax import lax
from jax.experimental import pallas as pl
from jax.experimental.pallas import tpu as pltpu

F32 = jnp.float32
BF16 = jnp.bfloat16
I32 = jnp.int32
MESH = pl.DeviceIdType.MESH

NORM_EPS = 1e-6
D_MODEL = 1024
HEAD64 = 64
LANES = 128
BLK = 128
MLA_HEADS = 8
MLA_QK = 192
MLA_PAD = 256
ROPE_THETA = 10000.0
SWA_HEADS = 16
SWA_KV = 4
VMEM_CAP = 56 * 1024 * 1024
MATMUL_TILE_BYTES = 8 * 1024 * 1024

ADAM_LR, ADAM_B1, ADAM_B2, ADAM_EPS, ADAM_WD, ADAM_STEP = 0.001, 0.9, 0.999, 1e-08, 0.01, 10

NT = (((1,), (1,)), ((), ()))
NN = (((1,), (0,)), ((), ()))
TN = (((0,), (0,)), ((), ()))


def _dot(a, b, dims=NN):
    return lax.dot_general(a, b, dims, preferred_element_type=F32)


def _tile(n, pref):
    for t in (pref, 512, 256, 128):
        if t <= pref and n % t == 0:
            return t
    return n


def _params(sem, vmem_bytes):
    limit = int(min(max(2 * vmem_bytes, 24 * 1024 * 1024), VMEM_CAP))
    return pltpu.CompilerParams(dimension_semantics=sem, vmem_limit_bytes=limit)


def _in_hbm(s):
    return pltpu.HBM(s.shape, s.dtype) if len(s.shape) >= 2 else s


def _pallas(*args, out_shape, **kwargs):
    out_shape = [_in_hbm(s) for s in out_shape] if isinstance(out_shape, (list, tuple)) else _in_hbm(out_shape)
    call = pl.pallas_call(*args, out_shape=out_shape, **kwargs)

    def run(*operands):
        return call(*[pltpu.with_memory_space_constraint(a, pltpu.HBM) if a.ndim >= 2 else a for a in operands])

    return run


def _split(v):
    hi = v.astype(BF16)
    return hi, (v - hi.astype(F32)).astype(BF16)


def _dot2(v, m):
    hi, lo = _split(v)
    return _dot(hi, m) + _dot(lo, m)


def _dot_split(v, m2):
    return _dot(jnp.concatenate(_split(v), axis=1), m2)


def _first_half(shape):
    return lax.broadcasted_iota(I32, shape, 1) < HEAD64


def _sigmoid(g):
    return 1.0 / (1.0 + jnp.exp(-g))


def matmul(a, b, mode, *, name, out_dtype=F32, add=None, tm=512, tn=1024):
    if mode == "nn":
        (M, K), (K2, N) = a.shape, b.shape
    elif mode == "nt":
        (M, K), (N, K2) = a.shape, b.shape
    else:
        (K, M), (K2, N) = a.shape, b.shape
    assert K == K2, (a.shape, b.shape, mode)
    tm, tn = _tile(M, tm), _tile(N, tn)
    while K * tm * a.dtype.itemsize > MATMUL_TILE_BYTES:
        tm //= 2
    while K * tn * b.dtype.itemsize > MATMUL_TILE_BYTES:
        tn //= 2
    dims = {"nn": NN, "nt": NT, "tn": TN}[mode]
    n_in = 2 if add is None else 3

    def body(*refs):
        a_ref, b_ref = refs[:2]
        r = _dot(a_ref[...].astype(BF16), b_ref[...].astype(BF16), dims)
        if add is not None:
            r = r + refs[2][...]
        for o_ref in refs[n_in:]:
            o_ref[...] = r.astype(o_ref.dtype)

    a_spec = pl.BlockSpec((K, tm), lambda i, j: (0, i)) if mode == "tn" else pl.BlockSpec((tm, K), lambda i, j: (i, 0))
    b_spec = pl.BlockSpec((tn, K), lambda i, j: (j, 0)) if mode == "nt" else pl.BlockSpec((K, tn), lambda i, j: (0, j))
    o_spec = pl.BlockSpec((tm, tn), lambda i, j: (i, j))
    in_specs, args = [a_spec, b_spec], [a, b]
    if add is not None:
        in_specs.append(o_spec)
        args.append(add)
    vm = 2 * (tm * K * a.dtype.itemsize + K * tn * b.dtype.itemsize) + 5 * tm * tn * 4
    out_dtypes = list(out_dtype) if isinstance(out_dtype, (tuple, list)) else [out_dtype]
    res = _pallas(
        body, name=name, grid=(M // tm, N // tn),
        in_specs=in_specs, out_specs=[o_spec] * len(out_dtypes),
        out_shape=[jax.ShapeDtypeStruct((M, N), d) for d in out_dtypes],
        compiler_params=_params(("parallel", "parallel"), vm),
    )(*args)
    return res[0] if len(out_dtypes) == 1 else res


def rmsnorm_fwd(x, g, *, col_off, width, out_dtype, name, tm=256):
    S = x.shape[0]
    assert col_off % width == 0
    cb = col_off // width
    tm = _tile(S, tm)

    def body(x_ref, g_ref, o_ref):
        v = x_ref[...]
        r = lax.rsqrt(jnp.mean(v * v, axis=1, keepdims=True) + NORM_EPS)
        o_ref[...] = (v * r * g_ref[...]).astype(out_dtype)

    return _pallas(
        body, name=name, grid=(S // tm,),
        in_specs=[pl.BlockSpec((tm, width), lambda i: (i, cb)), pl.BlockSpec((1, width), lambda i: (0, 0))],
        out_specs=pl.BlockSpec((tm, width), lambda i: (i, 0)),
        out_shape=jax.ShapeDtypeStruct((S, width), out_dtype),
        compiler_params=_params(("parallel",), 4 * tm * width * 4),
    )(x, g.reshape(1, width))


def rmsnorm_bwd(x, g, dh, *, col_off, width, out_dtype, name, res=None, tm=256):
    S = x.shape[0]
    cb = col_off // width
    tm = _tile(S, tm)

    def body(*refs):
        if res is None:
            x_ref, g_ref, dh_ref, dx_ref, dg_ref = refs
        else:
            x_ref, g_ref, dh_ref, res_ref, dx_ref, dg_ref = refs

        @pl.when(pl.program_id(0) == 0)
        def _():
            dg_ref[...] = jnp.zeros_like(dg_ref)

        v = x_ref[...]
        dhv = dh_ref[...].astype(F32)
        r = lax.rsqrt(jnp.mean(v * v, axis=1, keepdims=True) + NORM_EPS)
        y = v * r
        dy = dhv * g_ref[...]
        dx = r * (dy - y * jnp.mean(dy * y, axis=1, keepdims=True))
        if res is not None:
            dx = dx + res_ref[...]
        dx_ref[...] = dx.astype(out_dtype)
        dg_ref[...] += jnp.sum(dhv * y, axis=0, keepdims=True)

    row = pl.BlockSpec((tm, width), lambda i: (i, 0))
    in_specs = [pl.BlockSpec((tm, width), lambda i: (i, cb)), pl.BlockSpec((1, width), lambda i: (0, 0)), row]
    args = [x, g.reshape(1, width), dh]
    if res is not None:
        in_specs.append(row)
        args.append(res)
    return _pallas(
        body, name=name, grid=(S // tm,),
        in_specs=in_specs,
        out_specs=[row, pl.BlockSpec((1, width), lambda i: (0, 0))],
        out_shape=[jax.ShapeDtypeStruct((S, width), out_dtype), jax.ShapeDtypeStruct((1, width), F32)],
        compiler_params=_params(("arbitrary",), 8 * tm * width * 4),
    )(*args)


def _group_ones():
    r = lax.broadcasted_iota(I32, (LANES, LANES), 0) // HEAD64
    c = lax.broadcasted_iota(I32, (LANES, LANES), 1) // HEAD64
    return (r == c).astype(BF16)


def _segmean64(v):
    return _dot2(v, _group_ones()) * (1.0 / HEAD64)


def headnorm_fwd(x, g64, *, col_off, width, name, tm=512):
    S = x.shape[0]
    cb = col_off // LANES
    tm = _tile(S, tm)

    def body(x_ref, g_ref, o_ref):
        v = x_ref[...]
        r = lax.rsqrt(_segmean64(v * v) + NORM_EPS)
        o_ref[...] = (v * r * g_ref[...]).astype(BF16)

    return _pallas(
        body, name=name, grid=(S // tm, width // LANES),
        in_specs=[pl.BlockSpec((tm, LANES), lambda i, j: (i, cb + j)), pl.BlockSpec((1, LANES), lambda i, j: (0, 0))],
        out_specs=pl.BlockSpec((tm, LANES), lambda i, j: (i, j)),
        out_shape=jax.ShapeDtypeStruct((S, width), BF16),
        compiler_params=_params(("parallel", "parallel"), 8 * tm * LANES * 4),
    )(x, jnp.tile(g64, 2).reshape(1, LANES))


def headnorm_bwd(x, g64, dh, *, col_off, width, name, tm=512):
    S = x.shape[0]
    cb = col_off // LANES
    tm = _tile(S, tm)

    def body(x_ref, g_ref, dh_ref, dx_ref, dg_ref):
        @pl.when((pl.program_id(0) == 0) & (pl.program_id(1) == 0))
        def _():
            dg_ref[...] = jnp.zeros_like(dg_ref)

        v = x_ref[...]
        dhv = dh_ref[...]
        r = lax.rsqrt(_segmean64(v * v) + NORM_EPS)
        y = v * r
        dy = dhv * g_ref[...]
        dx_ref[...] = (r * (dy - y * _segmean64(dy * y))).astype(BF16)
        dg_ref[...] += jnp.sum(dhv * y, axis=0, keepdims=True)

    return _pallas(
        body, name=name, grid=(width // LANES, S // tm),
        in_specs=[pl.BlockSpec((tm, LANES), lambda j, i: (i, cb + j)), pl.BlockSpec((1, LANES), lambda j, i: (0, 0)),
                  pl.BlockSpec((tm, LANES), lambda j, i: (i, j))],
        out_specs=[pl.BlockSpec((tm, LANES), lambda j, i: (i, j)), pl.BlockSpec((1, LANES), lambda j, i: (0, 0))],
        out_shape=[jax.ShapeDtypeStruct((S, width), BF16), jax.ShapeDtypeStruct((1, LANES), F32)],
        compiler_params=_params(("arbitrary", "arbitrary"), 10 * tm * LANES * 4),
    )(x, jnp.tile(g64, 2).reshape(1, LANES), dh)


def gate_fwd(o, proj, *, gate_off, name, tm=256):
    S, W = o.shape
    cb = gate_off // W
    tm = _tile(S, tm)

    def body(o_ref, g_ref, out_ref):
        g = g_ref[...]
        out_ref[...] = (o_ref[...] * (g * _sigmoid(g))).astype(BF16)

    return _pallas(
        body, name=name, grid=(S // tm,),
        in_specs=[pl.BlockSpec((tm, W), lambda i: (i, 0)), pl.BlockSpec((tm, W), lambda i: (i, cb))],
        out_specs=pl.BlockSpec((tm, W), lambda i: (i, 0)),
        out_shape=jax.ShapeDtypeStruct((S, W), BF16),
        compiler_params=_params(("parallel",), 6 * tm * W * 4),
    )(o, proj)


def gate_bwd(dog, o, proj, *, gate_off, name, tm=256):
    S, W = o.shape
    cb = gate_off // W
    tm = _tile(S, tm)

    def body(d_ref, o_ref, g_ref, do_ref, dg_ref):
        g = g_ref[...]
        d = d_ref[...]
        s = _sigmoid(g)
        do_ref[...] = d * (g * s)
        dg_ref[...] = (d * o_ref[...] * (s * (1.0 + g * (1.0 - s)))).astype(BF16)

    row = pl.BlockSpec((tm, W), lambda i: (i, 0))
    return _pallas(
        body, name=name, grid=(S // tm,),
        in_specs=[row, row, pl.BlockSpec((tm, W), lambda i: (i, cb))],
        out_specs=[row, row],
        out_shape=[jax.ShapeDtypeStruct((S, W), F32), jax.ShapeDtypeStruct((S, W), BF16)],
        compiler_params=_params(("parallel",), 10 * tm * W * 4),
    )(dog, o, proj)


def loss_head(y, target, *, name, tm=256):
    S, W = y.shape
    tm = _tile(S, tm)
    n = S // tm

    def body(y_ref, t_ref, dy_ref, l_ref, acc_ref):
        i = pl.program_id(0)

        @pl.when(i == 0)
        def _():
            acc_ref[...] = jnp.zeros_like(acc_ref)

        e = y_ref[...] - t_ref[...]
        dy_ref[...] = e * (1.0 / W)
        acc_ref[...] += jnp.sum(e * e, axis=0, keepdims=True)

        @pl.when(i == n - 1)
        def _():
            l_ref[...] = jnp.full(l_ref.shape, (0.5 / W) * jnp.sum(acc_ref[...]), F32)

    row = pl.BlockSpec((tm, W), lambda i: (i, 0))
    return _pallas(
        body, name=name, grid=(n,),
        in_specs=[row, row],
        out_specs=[row, pl.BlockSpec((8, LANES), lambda i: (0, 0))],
        out_shape=[jax.ShapeDtypeStruct((S, W), F32), jax.ShapeDtypeStruct((8, LANES), F32)],
        scratch_shapes=[pltpu.VMEM((1, W), F32)],
        compiler_params=_params(("arbitrary",), 8 * tm * W * 4),
    )(y, target)


def _stack_heads(t, zero):
    first = _first_half(t.shape)
    return jnp.concatenate([jnp.where(first, t, zero), jnp.where(first, zero, t)], axis=0)


def _sb_weights(qs, ks, mask, upper, rss):
    zs = [_dot(q, k, NT) for q, k in zip(qs, ks)]
    sps = [jnp.maximum(z, 0.0) + jnp.log(1.0 + jnp.exp(-jnp.abs(z))) for z in zs]
    gs = [z - sp for z, sp in zip(zs, sps)]
    if mask is not None:
        sps = [jnp.where(mask, sp, 0.0) for sp in sps]
    cums = [_dot_split(sp, upper) for sp in sps]
    avs = [jnp.exp(g - (cum + rs)) for g, cum, rs in zip(gs, cums, rss)]
    if mask is not None:
        avs = [jnp.where(mask, a, 0.0) for a in avs]
    return avs, sps, gs


def _sb_consts():
    row = lax.broadcasted_iota(I32, (BLK, BLK), 0)
    col = lax.broadcasted_iota(I32, (BLK, BLK), 1)
    diag = col < row
    return row, col, jnp.concatenate([diag, diag], axis=0)


SB_DEAD = 105.0


def _sb_walk_left(block, i, carry):
    def least(c):
        m = c[0][1]
        for pair in c[1:]:
            m = jnp.minimum(m, pair[1])
        return jnp.min(m)

    def cond(state):
        jj, _, low = state
        return (jj < i) & (low < SB_DEAD)

    def body(state):
        jj, c, _ = state
        c = block(i - 1 - jj, c, None)
        return jj + 1, c, least(c)

    return lax.while_loop(cond, body, (jnp.int32(0), carry, least(carry)))[1]


SB_W = 1024


def sb_attn_fwd(qkv, *, name, pairs=8):
    S = qkv.shape[0]
    W = SB_W
    PW = pairs * LANES
    ngrp, nq = W // PW, S // BLK

    def body(q_ref, k_ref, v_ref, o_ref):
        i = pl.program_id(1)
        row, col, diag = _sb_consts()
        upper = jnp.tile((row > col).astype(BF16), (2, 1))
        zero = jnp.zeros((BLK, LANES), BF16)
        qs = [_stack_heads(q_ref[:, p * LANES:(p + 1) * LANES] * 0.125, zero) for p in range(pairs)]

        def block(j, carry, mask):
            rows = pl.ds(pl.multiple_of(j * BLK, BLK), BLK)
            cols = [slice(p * LANES, (p + 1) * LANES) for p in range(pairs)]
            avs, sps, _ = _sb_weights(qs, [k_ref[rows, c] for c in cols], mask, upper, [c[1] for c in carry])
            abs_ = [a.astype(BF16) for a in avs]
            outs = [_dot(jnp.concatenate([ab[:BLK], ab[BLK:]], axis=1), _stack_heads(v_ref[rows, c], zero))
                    for ab, c in zip(abs_, cols)]
            return tuple((carry[p][0] + outs[p], carry[p][1] + jnp.sum(sps[p], axis=1, keepdims=True))
                         for p in range(pairs))

        init = tuple((jnp.zeros((BLK, LANES), F32), jnp.zeros((2 * BLK, 1), F32)) for _ in range(pairs))
        carry = _sb_walk_left(block, i, block(i, init, diag))
        for p in range(pairs):
            o_ref[:, p * LANES:(p + 1) * LANES] = carry[p][0]

    once = pl.Buffered(1)
    return _pallas(
        body, name=name, grid=(ngrp, nq),
        in_specs=[pl.BlockSpec((BLK, PW), lambda p, i: (i, p)),
                  pl.BlockSpec((S, PW), lambda p, i: (0, ngrp + p), pipeline_mode=once),
                  pl.BlockSpec((S, PW), lambda p, i: (0, 2 * ngrp + p), pipeline_mode=once)],
        out_specs=pl.BlockSpec((BLK, PW), lambda p, i: (i, p)),
        out_shape=jax.ShapeDtypeStruct((S, W), F32),
        compiler_params=_params(("parallel", "arbitrary"), 2 * S * PW * 2 + 16 * BLK * PW * 4),
    )(qkv, qkv, qkv)


def sb_attn_bwd(qkv, o, do, *, name, pairs=4):
    S = qkv.shape[0]
    W = SB_W
    PW = pairs * LANES
    ngrp, nq = W // PW, S // BLK

    def body(q_ref, k_ref, v_ref, o_ref, do_ref, dq_ref, dk_ref, dv_ref):
        i = pl.program_id(1)

        @pl.when(i == 0)
        def _():
            dk_ref[...] = jnp.zeros_like(dk_ref)
            dv_ref[...] = jnp.zeros_like(dv_ref)

        row, col, diag = _sb_consts()
        upper = jnp.tile((row > col).astype(BF16), (2, 1))
        upper_incl = jnp.tile((row >= col).astype(BF16), (2, 1))
        first = _first_half((BLK, LANES))
        zero = jnp.zeros((BLK, LANES), BF16)
        qs, dos, tots = [], [], []
        for p in range(pairs):
            cols = slice(p * LANES, (p + 1) * LANES)
            qs.append(_stack_heads(q_ref[:, cols] * 0.125, zero))
            dob = do_ref[:, cols].astype(BF16)
            dos.append(_stack_heads(dob, zero))
            prod = dob.astype(F32) * o_ref[:, cols]
            tots.append(jnp.concatenate([jnp.sum(jnp.where(first, prod, 0.0), axis=1, keepdims=True),
                                         jnp.sum(jnp.where(first, 0.0, prod), axis=1, keepdims=True)], axis=0))

        def block(j, carry, mask):
            rows = pl.ds(pl.multiple_of(j * BLK, BLK), BLK)
            P = range(pairs)
            cols = [slice(p * LANES, (p + 1) * LANES) for p in P]
            ks = [k_ref[rows, c] for c in cols]
            das = [_dot(dos[p], v_ref[rows, cols[p]], NT) for p in P]
            avs, sps, gs = _sb_weights(qs, ks, mask, upper, [c[1] for c in carry])
            abs_ = [a.astype(BF16) for a in avs]
            es = [ab.astype(F32) * da for ab, da in zip(abs_, das)]
            sufs = [_dot_split(e, upper_incl) for e in es]
            lefts = [tots[p] - (sufs[p] + carry[p][2]) for p in P]
            dzs = [es[p] - jnp.exp(gs[p]) * (es[p] + lefts[p]) for p in P]
            if mask is not None:
                dzs = [jnp.where(mask, dz, 0.0) for dz in dzs]
            dzbs = [dz.astype(BF16) for dz in dzs]
            dks = [_dot(dzbs[p], qs[p], TN) for p in P]
            dvs = [_dot(abs_[p], dos[p], TN) for p in P]
            dqs = [_dot(jnp.concatenate([dzbs[p][:BLK], dzbs[p][BLK:]], axis=1), _stack_heads(ks[p], zero)) for p in P]
            for p in P:
                dk_ref[rows, cols[p]] += dks[p]
                dv_ref[rows, cols[p]] += dvs[p]
            return tuple((carry[p][0] + dqs[p], carry[p][1] + jnp.sum(sps[p], axis=1, keepdims=True),
                          carry[p][2] + jnp.sum(es[p], axis=1, keepdims=True)) for p in P)

        col0 = jnp.zeros((2 * BLK, 1), F32)
        init = tuple((jnp.zeros((BLK, LANES), F32), col0, col0) for _ in range(pairs))
        carry = _sb_walk_left(block, i, block(i, init, diag))
        for p in range(pairs):
            dq_ref[:, p * LANES:(p + 1) * LANES] = (carry[p][0] * 0.125).astype(BF16)

    once = pl.Buffered(1)
    tile = pl.BlockSpec((BLK, PW), lambda p, i: (i, p))
    full = pl.BlockSpec((S, PW), lambda p, i: (0, p), pipeline_mode=once)
    shape = jax.ShapeDtypeStruct((S, W), F32)
    return _pallas(
        body, name=name, grid=(ngrp, nq),
        in_specs=[tile,
                  pl.BlockSpec((S, PW), lambda p, i: (0, ngrp + p), pipeline_mode=once),
                  pl.BlockSpec((S, PW), lambda p, i: (0, 2 * ngrp + p), pipeline_mode=once),
                  tile, tile],
        out_specs=[tile, full, full],
        out_shape=[jax.ShapeDtypeStruct((S, W), BF16), shape, shape],
        compiler_params=_params(("parallel", "arbitrary"), 2 * S * PW * 2 + 2 * S * PW * 4 + 16 * BLK * PW * 4),
    )(qkv, qkv, qkv, o, do)


FB = 256


def flash_fwd(qn, kn, vb, *, name, heads=4):
    S = qn.shape[0]
    H, DK, DV = MLA_HEADS, MLA_PAD, LANES
    nq, ngrp = S // FB, H // heads
    scale = 1.0 / math.sqrt(MLA_QK)

    def body(q_ref, k_ref, v_ref, o_ref, lse_ref):
        i = pl.program_id(1)
        diag = lax.broadcasted_iota(I32, (FB, FB), 1) <= lax.broadcasted_iota(I32, (FB, FB), 0)
        qs = [q_ref[:, h * DK:(h + 1) * DK] for h in range(heads)]

        def block(j, carry, mask):
            rows = pl.ds(pl.multiple_of(j * FB, FB), FB)
            H = range(heads)
            ss = [_dot(qs[h], k_ref[rows, h * DK:(h + 1) * DK], NT) * scale for h in H]
            if mask is not None:
                ss = [jnp.where(mask, s, -1e30) for s in ss]
            ms = [jnp.maximum(carry[h][1], jnp.max(ss[h], axis=1, keepdims=True)) for h in H]
            ps = [jnp.exp(ss[h] - ms[h]) for h in H]
            ws = [jnp.exp(carry[h][1] - ms[h]) for h in H]
            pvs = [_dot(ps[h].astype(BF16), v_ref[rows, h * DV:(h + 1) * DV]) for h in H]
            return tuple((carry[h][0] * ws[h] + pvs[h], ms[h], carry[h][2] * ws[h] + jnp.sum(ps[h], axis=1, keepdims=True))
                         for h in H)

        init = tuple((jnp.zeros((FB, DV), F32), jnp.full((FB, 1), -1e30, F32), jnp.zeros((FB, 1), F32))
                     for _ in range(heads))
        carry = lax.fori_loop(0, i, lambda j, c: block(j, c, None), init)
        carry = block(i, carry, diag)
        for h in range(heads):
            acc, m, l = carry[h]
            o_ref[:, h * DV:(h + 1) * DV] = acc / l
            lse_ref[h] = m + jnp.log(l)

    return _pallas(
        body, name=name, grid=(ngrp, nq),
        in_specs=[pl.BlockSpec((FB, heads * DK), lambda g, i: (i, g)),
                  pl.BlockSpec((S, heads * DK), lambda g, i: (0, g), pipeline_mode=pl.Buffered(1)),
                  pl.BlockSpec((S, heads * DV), lambda g, i: (0, g), pipeline_mode=pl.Buffered(1))],
        out_specs=[pl.BlockSpec((FB, heads * DV), lambda g, i: (i, g)), pl.BlockSpec((heads, FB, 1), lambda g, i: (g, i, 0))],
        out_shape=[jax.ShapeDtypeStruct((S, H * DV), F32), jax.ShapeDtypeStruct((H, S, 1), F32)],
        compiler_params=_params(("parallel", "arbitrary"), 2 * S * heads * (DK + DV) * 2 + 16 * FB * FB * 4),
    )(qn, kn, vb)


def flash_bwd(qn, kn, vb, o, lse, do, *, name, heads=4):
    S = qn.shape[0]
    H, DK, DV = MLA_HEADS, MLA_PAD, LANES
    nq, ngrp = S // FB, H // heads
    scale = 1.0 / math.sqrt(MLA_QK)

    def body(q_ref, k_ref, v_ref, o_ref, lse_ref, do_ref, dq_ref, dk_ref, dv_ref):
        i = pl.program_id(1)

        @pl.when(i == 0)
        def _():
            dk_ref[...] = jnp.zeros_like(dk_ref)
            dv_ref[...] = jnp.zeros_like(dv_ref)

        diag = lax.broadcasted_iota(I32, (FB, FB), 1) <= lax.broadcasted_iota(I32, (FB, FB), 0)
        qs, dobs, deltas, lses = [], [], [], []
        for h in range(heads):
            do = do_ref[:, h * DV:(h + 1) * DV]
            qs.append(q_ref[:, h * DK:(h + 1) * DK])
            dobs.append(do.astype(BF16))
            deltas.append(jnp.sum(do * o_ref[:, h * DV:(h + 1) * DV], axis=1, keepdims=True))
            lses.append(lse_ref[h])

        def block(j, carry, mask):
            rows = pl.ds(pl.multiple_of(j * FB, FB), FB)
            H = range(heads)
            kcs = [slice(h * DK, (h + 1) * DK) for h in H]
            vcs = [slice(h * DV, (h + 1) * DV) for h in H]
            ks = [k_ref[rows, kcs[h]] for h in H]
            ss = [_dot(qs[h], ks[h], NT) for h in H]
            dps = [_dot(dobs[h], v_ref[rows, vcs[h]], NT) for h in H]
            ps = [jnp.exp(ss[h] * scale - lses[h]) for h in H]
            if mask is not None:
                ps = [jnp.where(mask, p, 0.0) for p in ps]
            pbs = [p.astype(BF16) for p in ps]
            dss = [(ps[h] * (dps[h] - deltas[h]) * scale).astype(BF16) for h in H]
            dvs = [_dot(pbs[h], dobs[h], TN) for h in H]
            dks = [_dot(dss[h], qs[h], TN) for h in H]
            dqs = [_dot(dss[h], ks[h]) for h in H]
            for h in H:
                dv_ref[rows, vcs[h]] += dvs[h]
                dk_ref[rows, kcs[h]] += dks[h]
            return tuple(carry[h] + dqs[h] for h in H)

        carry = lax.fori_loop(0, i, lambda j, c: block(j, c, None), tuple(jnp.zeros((FB, DK), F32) for _ in range(heads)))
        carry = block(i, carry, diag)
        for h in range(heads):
            dq_ref[:, h * DK:(h + 1) * DK] = carry[h]

    qtile = pl.BlockSpec((FB, heads * DK), lambda g, i: (i, g))
    otile = pl.BlockSpec((FB, heads * DV), lambda g, i: (i, g))
    once = pl.Buffered(1)
    kfull = pl.BlockSpec((S, heads * DK), lambda g, i: (0, g), pipeline_mode=once)
    vfull = pl.BlockSpec((S, heads * DV), lambda g, i: (0, g), pipeline_mode=once)
    return _pallas(
        body, name=name, grid=(ngrp, nq),
        in_specs=[qtile, kfull, vfull, otile, pl.BlockSpec((heads, FB, 1), lambda g, i: (g, i, 0)), otile],
        out_specs=[qtile, kfull, vfull],
        out_shape=[jax.ShapeDtypeStruct((S, H * DK), F32), jax.ShapeDtypeStruct((S, H * DK), F32),
                   jax.ShapeDtypeStruct((S, H * DV), F32)],
        compiler_params=_params(("parallel", "arbitrary"), S * heads * (DK + DV) * 6 + 16 * FB * FB * 4),
    )(qn, kn, vb, o, lse, do)


def _rope_tables(S):
    half = 32
    inv_freq = ROPE_THETA ** (-jnp.arange(half, dtype=F32) / half)
    ang = jnp.arange(S).astype(F32)[:, None] * inv_freq[None, :]
    cos, sin = jnp.cos(ang), jnp.sin(ang)
    ones, zeros = jnp.ones((S, LANES), F32), jnp.zeros((S, LANES), F32)
    pad = jnp.zeros((S, 64), F32)
    return (jnp.concatenate([ones, cos, cos, pad + 1.0], axis=1),
            jnp.concatenate([zeros, -sin, sin, pad], axis=1))


def _rope_partner(u):
    lane = lax.broadcasted_iota(I32, u.shape, 1)
    return jnp.where((lane % HEAD64) < 32, pltpu.roll(u, LANES - 32, 1), pltpu.roll(u, 32, 1))


def _normrope(raw, g, cos, sgn):
    r = lax.rsqrt(jnp.sum(raw * raw, axis=1, keepdims=True) * (1.0 / MLA_QK) + NORM_EPS)
    y = raw * r
    u = y * g
    pe = u[:, LANES:]
    out = jnp.concatenate([u[:, :LANES], pe * cos[:, LANES:] + _rope_partner(pe) * sgn[:, LANES:]], axis=1)
    return out, y, r


def _normrope_bwd(dout, g, cos, sgn, y, r):
    dpe = dout[:, LANES:]
    du = jnp.concatenate([dout[:, :LANES], dpe * cos[:, LANES:] + _rope_partner(dpe * sgn[:, LANES:])], axis=1)
    dy = du * g
    draw = r * (dy - y * (jnp.sum(dy * y, axis=1, keepdims=True) * (1.0 / MLA_QK)))
    return draw, jnp.sum(du * y, axis=0, keepdims=True)


def mla_prep_fwd(qraw, kv, proj, gq, gk, cos, sgn, *, kpe_off, name, tm=1024):
    S = qraw.shape[0]
    tm = _tile(S, tm)
    kb = kpe_off // LANES

    def body(q_ref, kn_ref, v_ref, kpe_ref, gq_ref, gk_ref, c_ref, s_ref, qo_ref, ko_ref, vo_ref):
        cos, sgn = c_ref[...], s_ref[...]
        qo_ref[...] = _normrope(q_ref[...], gq_ref[...], cos, sgn)[0].astype(BF16)
        kraw = jnp.concatenate([kn_ref[...], kpe_ref[...]], axis=1)
        ko_ref[...] = _normrope(kraw, gk_ref[...], cos, sgn)[0].astype(BF16)
        vo_ref[...] = v_ref[...].astype(BF16)

    head = pl.BlockSpec((tm, MLA_PAD), lambda i, h: (i, h))
    gain = pl.BlockSpec((1, MLA_PAD), lambda i, h: (0, 0))
    tab = pl.BlockSpec((tm, MLA_PAD), lambda i, h: (i, 0))
    return _pallas(
        body, name=name, grid=(S // tm, MLA_HEADS),
        in_specs=[head, pl.BlockSpec((tm, LANES), lambda i, h: (i, 2 * h)), pl.BlockSpec((tm, LANES), lambda i, h: (i, 2 * h + 1)),
                  pl.BlockSpec((tm, LANES), lambda i, h: (i, kb)), gain, gain, tab, tab],
        out_specs=[head, head, pl.BlockSpec((tm, LANES), lambda i, h: (i, h))],
        out_shape=[jax.ShapeDtypeStruct(qraw.shape, BF16), jax.ShapeDtypeStruct(qraw.shape, BF16),
                   jax.ShapeDtypeStruct((S, MLA_HEADS * LANES), BF16)],
        compiler_params=_params(("parallel", "arbitrary"), 16 * tm * MLA_PAD * 4),
    )(qraw, kv, kv, proj, gq, gk, cos, sgn)


def mla_prep_bwd(dqn, dkn, dv, qraw, kv, proj, gq, gk, cos, sgn, *, kpe_off, name, tm=512):
    S = qraw.shape[0]
    tm = _tile(S, tm)
    kb = kpe_off // LANES

    def body(dq_ref, dk_ref, dv_ref, q_ref, kn_ref, kpe_ref, gq_ref, gk_ref, c_ref, s_ref,
             dqo_ref, dkv_ref, dkpe_ref, dgq_ref, dgk_ref, acc_ref):
        i, h = pl.program_id(0), pl.program_id(1)

        @pl.when((i == 0) & (h == 0))
        def _():
            dgq_ref[...] = jnp.zeros_like(dgq_ref)
            dgk_ref[...] = jnp.zeros_like(dgk_ref)

        @pl.when(h == 0)
        def _():
            acc_ref[...] = jnp.zeros_like(acc_ref)

        cos, sgn = c_ref[...], s_ref[...]
        _, yq, rq = _normrope(q_ref[...], gq_ref[...], cos, sgn)
        dq, dgq = _normrope_bwd(dq_ref[...], gq_ref[...], cos, sgn, yq, rq)
        dqo_ref[...] = dq.astype(BF16)
        dgq_ref[...] += dgq
        kraw = jnp.concatenate([kn_ref[...], kpe_ref[...]], axis=1)
        _, yk, rk = _normrope(kraw, gk_ref[...], cos, sgn)
        dk, dgk = _normrope_bwd(dk_ref[...], gk_ref[...], cos, sgn, yk, rk)
        dgk_ref[...] += dgk
        dkv_ref[...] = jnp.concatenate([dk[:, :LANES], dv_ref[...]], axis=1).astype(BF16)
        acc_ref[...] += dk[:, LANES:]

        @pl.when(h == MLA_HEADS - 1)
        def _():
            dkpe_ref[...] = acc_ref[...].astype(BF16)

    head = pl.BlockSpec((tm, MLA_PAD), lambda i, h: (i, h))
    gain = pl.BlockSpec((1, MLA_PAD), lambda i, h: (0, 0))
    tab = pl.BlockSpec((tm, MLA_PAD), lambda i, h: (i, 0))
    return _pallas(
        body, name=name, grid=(S // tm, MLA_HEADS),
        in_specs=[head, head, pl.BlockSpec((tm, LANES), lambda i, h: (i, h)), head,
                  pl.BlockSpec((tm, LANES), lambda i, h: (i, 2 * h)), pl.BlockSpec((tm, LANES), lambda i, h: (i, kb)),
                  gain, gain, tab, tab],
        out_specs=[head, head, pl.BlockSpec((tm, LANES), lambda i, h: (i, 0)), gain, gain],
        out_shape=[jax.ShapeDtypeStruct(qraw.shape, BF16), jax.ShapeDtypeStruct(qraw.shape, BF16),
                   jax.ShapeDtypeStruct((S, LANES), BF16), jax.ShapeDtypeStruct((1, MLA_PAD), F32),
                   jax.ShapeDtypeStruct((1, MLA_PAD), F32)],
        scratch_shapes=[pltpu.VMEM((tm, LANES), F32)],
        compiler_params=_params(("arbitrary", "arbitrary"), 24 * tm * MLA_PAD * 4),
    )(dqn, dkn, dv, qraw, kv, proj, gq, gk, cos, sgn)


def swa_kv_prep(proj, gk64, *, k_off, v_off, name, tm=512):
    S = proj.shape[0]
    tm = _tile(S, tm)
    W = SWA_KV * HEAD64

    def body(k_ref, v_ref, g_ref, ko_ref, vo_ref):
        first = _first_half((tm, LANES))

        def dup(n):
            nr = pltpu.roll(n, HEAD64, 1)
            return jnp.where(first, n, nr), jnp.where(first, nr, n)

        for t in range(W // LANES):
            x = k_ref[:, t * LANES:(t + 1) * LANES]
            n = x * lax.rsqrt(_segmean64(x * x) + NORM_EPS) * g_ref[...]
            d0, d1 = dup(n)
            ko_ref[:, 2 * t * LANES:(2 * t + 1) * LANES] = d0.astype(BF16)
            ko_ref[:, (2 * t + 1) * LANES:(2 * t + 2) * LANES] = d1.astype(BF16)
            d0, d1 = dup(v_ref[:, t * LANES:(t + 1) * LANES])
            vo_ref[:, 2 * t * LANES:(2 * t + 1) * LANES] = d0.astype(BF16)
            vo_ref[:, (2 * t + 1) * LANES:(2 * t + 2) * LANES] = d1.astype(BF16)

    out = pl.BlockSpec((tm, SWA_KV * LANES), lambda i: (i, 0))
    shape = jax.ShapeDtypeStruct((S, SWA_KV * LANES), BF16)
    return _pallas(
        body, name=name, grid=(S // tm,),
        in_specs=[pl.BlockSpec((tm, W), lambda i: (i, k_off // W)), pl.BlockSpec((tm, W), lambda i: (i, v_off // W)),
                  pl.BlockSpec((1, LANES), lambda i: (0, 0))],
        out_specs=[out, out], out_shape=[shape, shape],
        compiler_params=_params(("parallel",), 12 * tm * W * 4),
    )(proj, proj, jnp.tile(gk64, 2).reshape(1, LANES))


def swa_kv_prep_bwd(dkdup, dvdup, proj, gk64, *, k_off, name, tm=512):
    S = proj.shape[0]
    tm = _tile(S, tm)
    W = SWA_KV * HEAD64

    def body(dk_ref, dv_ref, k_ref, g_ref, dko_ref, dvo_ref, dg_ref):
        @pl.when(pl.program_id(0) == 0)
        def _():
            dg_ref[...] = jnp.zeros_like(dg_ref)

        first = _first_half((tm, LANES))

        def fold(ref, t):
            d0 = ref[:, 2 * t * LANES:(2 * t + 1) * LANES]
            d1 = ref[:, (2 * t + 1) * LANES:(2 * t + 2) * LANES]
            return jnp.where(first, d0 + pltpu.roll(d0, HEAD64, 1), d1 + pltpu.roll(d1, HEAD64, 1))

        for t in range(W // LANES):
            dvo_ref[:, t * LANES:(t + 1) * LANES] = fold(dv_ref, t).astype(BF16)
            dh = fold(dk_ref, t)
            x = k_ref[:, t * LANES:(t + 1) * LANES]
            r = lax.rsqrt(_segmean64(x * x) + NORM_EPS)
            y = x * r
            dy = dh * g_ref[...]
            dko_ref[:, t * LANES:(t + 1) * LANES] = (r * (dy - y * _segmean64(dy * y))).astype(BF16)
            dg_ref[...] += jnp.sum(dh * y, axis=0, keepdims=True)

    dup = pl.BlockSpec((tm, SWA_KV * LANES), lambda i: (i, 0))
    out = pl.BlockSpec((tm, W), lambda i: (i, 0))
    shape = jax.ShapeDtypeStruct((S, W), BF16)
    return _pallas(
        body, name=name, grid=(S // tm,),
        in_specs=[dup, dup, pl.BlockSpec((tm, W), lambda i: (i, k_off // W)), pl.BlockSpec((1, LANES), lambda i: (0, 0))],
        out_specs=[out, out, pl.BlockSpec((1, LANES), lambda i: (0, 0))],
        out_shape=[shape, shape, jax.ShapeDtypeStruct((1, LANES), F32)],
        compiler_params=_params(("arbitrary",), 16 * tm * W * 4),
    )(dkdup, dvdup, proj, jnp.tile(gk64, 2).reshape(1, LANES))


def _swa_geometry(i):
    r = lax.broadcasted_iota(I32, (BLK, 2 * BLK), 0)
    c = lax.broadcasted_iota(I32, (BLK, 2 * BLK), 1)
    rel = r + BLK - c
    valid = (rel >= 0) & (rel < BLK) & ((c >= BLK) | (i > 0))
    return valid, rel.astype(F32)


def _swa_slope(h):
    return 2.0 ** (-8.0 * (h + 1) / SWA_HEADS)


def swa_attn_fwd(qn, kdup, vdup, sinks, *, name):
    S = qn.shape[0]
    nq = S // BLK
    group = SWA_HEADS // SWA_KV

    def body(q_ref, kp_ref, kc_ref, vp_ref, vc_ref, sink_ref, o_ref, lse_ref):
        i = pl.program_id(0)
        valid, rel = _swa_geometry(i)
        first = _first_half((BLK, LANES))
        lane = lax.broadcasted_iota(I32, (BLK, LANES), 1)
        lse_all = jnp.zeros((BLK, LANES), F32)
        for g in range(SWA_KV):
            cols = slice(g * LANES, (g + 1) * LANES)
            kk = jnp.concatenate([kp_ref[:, cols], kc_ref[:, cols]], axis=0)
            vv = jnp.concatenate([vp_ref[:, cols], vc_ref[:, cols]], axis=0)
            for pair in range(group // 2):
                tile = (g * group) // 2 + pair
                q2 = q_ref[:, tile * LANES:(tile + 1) * LANES]
                outs = []
                for a in range(2):
                    h = 2 * tile + a
                    qh = jnp.where(first if a == 0 else ~first, q2, jnp.zeros_like(q2))
                    s = _dot(qh, kk, NT) * (1.0 / math.sqrt(HEAD64)) - _swa_slope(h) * rel
                    s = jnp.where(valid, s, -1e30)
                    sink = sink_ref[h]
                    m = jnp.maximum(jnp.max(s, axis=1, keepdims=True), sink)
                    e = jnp.exp(s - m)
                    den = jnp.sum(e, axis=1, keepdims=True) + jnp.exp(sink - m)
                    outs.append(_dot((e / den).astype(BF16), vv))
                    lse_all = jnp.where(lane == h, m + jnp.log(den), lse_all)
                o_ref[:, tile * LANES:(tile + 1) * LANES] = jnp.where(first, outs[0], outs[1])
        lse_ref[...] = lse_all

    prev = lambda i: (jnp.maximum(i - 1, 0), 0)
    cur = lambda i: (i, 0)
    kvw = SWA_KV * LANES
    return _pallas(
        body, name=name, grid=(nq,),
        in_specs=[pl.BlockSpec((BLK, 1024), cur), pl.BlockSpec((BLK, kvw), prev), pl.BlockSpec((BLK, kvw), cur),
                  pl.BlockSpec((BLK, kvw), prev), pl.BlockSpec((BLK, kvw), cur),
                  pl.BlockSpec(memory_space=pltpu.SMEM)],
        out_specs=[pl.BlockSpec((BLK, 1024), cur), pl.BlockSpec((BLK, LANES), cur)],
        out_shape=[jax.ShapeDtypeStruct((S, 1024), F32), jax.ShapeDtypeStruct((S, LANES), F32)],
        compiler_params=_params(("parallel",), 16 * BLK * 1024 * 4),
    )(qn, kdup, kdup, vdup, vdup, sinks)


def swa_attn_bwd(qn, kdup, vdup, sinks, o, lse, do, *, name):
    S = qn.shape[0]
    nq = S // BLK
    group = SWA_HEADS // SWA_KV
    scale = 1.0 / math.sqrt(HEAD64)

    def body(q_ref, kp_ref, kc_ref, vp_ref, vc_ref, sink_ref, o_ref, lse_ref, do_ref,
             dq_ref, dk_ref, dv_ref, ds_ref):
        i = pl.program_id(0)

        @pl.when(i == 0)
        def _():
            dk_ref[...] = jnp.zeros_like(dk_ref)
            dv_ref[...] = jnp.zeros_like(dv_ref)
            ds_ref[...] = jnp.zeros_like(ds_ref)

        valid, rel = _swa_geometry(i)
        first = _first_half((BLK, LANES))
        lane1 = lax.broadcasted_iota(I32, (1, LANES), 1)
        lane = lax.broadcasted_iota(I32, (BLK, LANES), 1)
        lse_all = lse_ref[...]
        prow = pl.ds(pl.multiple_of(jnp.maximum(i - 1, 0) * BLK, BLK), BLK)
        crow = pl.ds(pl.multiple_of(i * BLK, BLK), BLK)
        dsink = jnp.zeros((1, LANES), F32)
        for g in range(SWA_KV):
            cols = slice(g * LANES, (g + 1) * LANES)
            kk = jnp.concatenate([kp_ref[:, cols], kc_ref[:, cols]], axis=0)
            vv = jnp.concatenate([vp_ref[:, cols], vc_ref[:, cols]], axis=0)
            dkk = jnp.zeros((2 * BLK, LANES), F32)
            dvv = jnp.zeros((2 * BLK, LANES), F32)
            for pair in range(group // 2):
                tile = (g * group) // 2 + pair
                tcols = slice(tile * LANES, (tile + 1) * LANES)
                q2 = q_ref[:, tcols]
                do2 = do_ref[:, tcols]
                prod = do2 * o_ref[:, tcols]
                dqs = []
                for a in range(2):
                    h = 2 * tile + a
                    mine = first if a == 0 else ~first
                    qh = jnp.where(mine, q2, jnp.zeros_like(q2))
                    doh = jnp.where(mine, do2, 0.0).astype(BF16)
                    delta = jnp.sum(jnp.where(mine, prod, 0.0), axis=1, keepdims=True)
                    lse_h = jnp.sum(jnp.where(lane == h, lse_all, 0.0), axis=1, keepdims=True)
                    s = _dot(qh, kk, NT) * scale - _swa_slope(h) * rel
                    p = jnp.where(valid, jnp.exp(s - lse_h), 0.0)
                    dsc = (p * (_dot(doh, vv, NT) - delta) * scale).astype(BF16)
                    dqs.append(_dot(dsc, kk))
                    dkk = dkk + _dot(dsc, qh, TN)
                    dvv = dvv + _dot(p.astype(BF16), doh, TN)
                    psink = jnp.exp(sink_ref[h] - lse_h)
                    dsink = dsink + jnp.where(lane1 == h, -jnp.sum(psink * delta), 0.0)
                dq_ref[:, tcols] = jnp.where(first, dqs[0], dqs[1])
            dk_ref[prow, cols] += dkk[:BLK]
            dv_ref[prow, cols] += dvv[:BLK]
            dk_ref[crow, cols] += dkk[BLK:]
            dv_ref[crow, cols] += dvv[BLK:]
        ds_ref[...] += dsink

    prev = lambda i: (jnp.maximum(i - 1, 0), 0)
    cur = lambda i: (i, 0)
    kvw = SWA_KV * LANES
    whole = pl.BlockSpec((S, kvw), lambda i: (0, 0))
    return _pallas(
        body, name=name, grid=(nq,),
        in_specs=[pl.BlockSpec((BLK, 1024), cur), pl.BlockSpec((BLK, kvw), prev), pl.BlockSpec((BLK, kvw), cur),
                  pl.BlockSpec((BLK, kvw), prev), pl.BlockSpec((BLK, kvw), cur),
                  pl.BlockSpec(memory_space=pltpu.SMEM),
                  pl.BlockSpec((BLK, 1024), cur), pl.BlockSpec((BLK, LANES), cur), pl.BlockSpec((BLK, 1024), cur)],
        out_specs=[pl.BlockSpec((BLK, 1024), cur), whole, whole, pl.BlockSpec((1, LANES), lambda i: (0, 0))],
        out_shape=[jax.ShapeDtypeStruct((S, 1024), F32), jax.ShapeDtypeStruct((S, kvw), F32),
                   jax.ShapeDtypeStruct((S, kvw), F32), jax.ShapeDtypeStruct((1, LANES), F32)],
        compiler_params=_params(("arbitrary",), 4 * S * kvw * 4 + 24 * BLK * 1024 * 4),
    )(qn, kdup, kdup, vdup, vdup, sinks, o, lse, do)


def _in_bwd(x, h, dproj, dy, g, w_in, tag):
    dh = matmul(dproj, w_in, "nt", name=f"{tag}_dh")
    dw_in = matmul(h, dproj, "tn", name=f"{tag}_dwin")
    dx, dg = rmsnorm_bwd(x, g, dh, col_off=0, width=D_MODEL, out_dtype=F32, res=dy, name=f"{tag}_dnorm")
    return dx, dw_in, dg


def _out_bwd(dy, og, o, proj, w_out, gate_off, tag):
    dog = matmul(dy, w_out, "nt", name=f"{tag}_dog")
    dw_out = matmul(og, dy, "tn", name=f"{tag}_dwout")
    do, dgate = gate_bwd(dog, o, proj, gate_off=gate_off, name=f"{tag}_dgate")
    return do, dgate, dw_out


def sb_fwd(x, p, tag):
    h = rmsnorm_fwd(x, p["norm"], col_off=0, width=D_MODEL, out_dtype=BF16, name=f"{tag}_norm")
    proj, projb = matmul(h, p["w_in"], "nn", out_dtype=(F32, BF16), name=f"{tag}_proj")
    o = sb_attn_fwd(projb, name=f"{tag}_attn")
    og = gate_fwd(o, proj, gate_off=3 * D_MODEL, name=f"{tag}_gate")
    y = matmul(og, p["w_out"], "nn", add=x, name=f"{tag}_out")
    return y, (x, h, proj, projb, o, og)


def sb_bwd(dy, saved, p, tag):
    x, h, proj, projb, o, og = saved
    do, dgate, dw_out = _out_bwd(dy, og, o, proj, p["w_out"], 3 * D_MODEL, tag)
    dq, dk, dv = sb_attn_bwd(projb, o, do, name=f"{tag}_dattn")
    dproj = jnp.concatenate([dq, dk.astype(BF16), dv.astype(BF16), dgate], axis=1)
    dx, dw_in, dg = _in_bwd(x, h, dproj, dy, p["norm"], p["w_in"], tag)
    return dx, {"norm": dg[0], "w_in": dw_in, "w_out": dw_out}


MLA_GATE, MLA_QLAT, MLA_KVLAT, MLA_KPE, MLA_IN = 0, 1024, 1280, 1408, 1536


def mla_fwd(x, p, tabs, tag):
    cos, sgn = tabs
    h = rmsnorm_fwd(x, p["norm"], col_off=0, width=D_MODEL, out_dtype=BF16, name=f"{tag}_norm")
    proj = matmul(h, p["w_in"], "nn", name=f"{tag}_proj")
    ql = rmsnorm_fwd(proj, p["q_a_norm"], col_off=MLA_QLAT, width=256, out_dtype=BF16, name=f"{tag}_qanorm")
    kvl = rmsnorm_fwd(proj, p["kv_a_norm"], col_off=MLA_KVLAT, width=128, out_dtype=BF16, name=f"{tag}_kvanorm")
    qraw = matmul(ql, p["w_uq"], "nn", name=f"{tag}_uq")
    kv = matmul(kvl, p["w_ukv"], "nn", name=f"{tag}_ukv")
    qn, kn, vb = mla_prep_fwd(qraw, kv, proj, p["gq"], p["gk"], cos, sgn, kpe_off=MLA_KPE, name=f"{tag}_prep")
    o, lse = flash_fwd(qn, kn, vb, name=f"{tag}_attn")
    og = gate_fwd(o, proj, gate_off=MLA_GATE, name=f"{tag}_gate")
    y = matmul(og, p["w_out"], "nn", add=x, name=f"{tag}_out")
    return y, (x, h, proj, ql, kvl, qraw, kv, qn, kn, vb, o, lse, og)


def mla_bwd(dy, saved, p, tabs, tag):
    cos, sgn = tabs
    x, h, proj, ql, kvl, qraw, kv, qn, kn, vb, o, lse, og = saved
    do, dgate, dw_out = _out_bwd(dy, og, o, proj, p["w_out"], MLA_GATE, tag)
    dqn, dkn, dv = flash_bwd(qn, kn, vb, o, lse, do, name=f"{tag}_dattn")
    dqraw, dkv, dkpe, dgq, dgk = mla_prep_bwd(dqn, dkn, dv, qraw, kv, proj, p["gq"], p["gk"], cos, sgn,
                                              kpe_off=MLA_KPE, name=f"{tag}_dprep")
    dql = matmul(dqraw, p["w_uq"], "nt", name=f"{tag}_dql")
    dw_uq = matmul(ql, dqraw, "tn", name=f"{tag}_dwuq")
    dkvl = matmul(dkv, p["w_ukv"], "nt", name=f"{tag}_dkvl")
    dw_ukv = matmul(kvl, dkv, "tn", name=f"{tag}_dwukv")
    dqlat, dgqa = rmsnorm_bwd(proj, p["q_a_norm"], dql, col_off=MLA_QLAT, width=256, out_dtype=BF16, name=f"{tag}_dqanorm")
    dkvlat, dgkva = rmsnorm_bwd(proj, p["kv_a_norm"], dkvl, col_off=MLA_KVLAT, width=128, out_dtype=BF16,
                                name=f"{tag}_dkvanorm")
    dproj = jnp.concatenate([dgate, dqlat, dkvlat, dkpe], axis=1)
    dx, dw_in, dg = _in_bwd(x, h, dproj, dy, p["norm"], p["w_in"], tag)
    return dx, {"norm": dg[0], "w_in": dw_in, "q_a_norm": dgqa[0], "w_uq": dw_uq, "kv_a_norm": dgkva[0],
                "w_ukv": dw_ukv, "gq": dgq[0], "gk": dgk[0], "w_out": dw_out}


SWA_Q, SWA_GATE, SWA_K, SWA_V = 0, 1024, 2048, 2304


def swa_fwd(x, p, tag):
    h = rmsnorm_fwd(x, p["norm"], col_off=0, width=D_MODEL, out_dtype=BF16, name=f"{tag}_norm")
    proj = matmul(h, p["w_in"], "nn", name=f"{tag}_proj")
    qn = headnorm_fwd(proj, p["q_head_norm"], col_off=SWA_Q, width=1024, name=f"{tag}_qnorm")
    kdup, vdup = swa_kv_prep(proj, p["k_head_norm"], k_off=SWA_K, v_off=SWA_V, name=f"{tag}_kvprep")
    o, lse = swa_attn_fwd(qn, kdup, vdup, p["sinks"], name=f"{tag}_attn")
    og = gate_fwd(o, proj, gate_off=SWA_GATE, name=f"{tag}_gate")
    y = matmul(og, p["w_out"], "nn", add=x, name=f"{tag}_out")
    return y, (x, h, proj, qn, kdup, vdup, o, lse, og)


def swa_bwd(dy, saved, p, tag):
    x, h, proj, qn, kdup, vdup, o, lse, og = saved
    do, dgate, dw_out = _out_bwd(dy, og, o, proj, p["w_out"], SWA_GATE, tag)
    dqn, dkdup, dvdup, dsinks = swa_attn_bwd(qn, kdup, vdup, p["sinks"], o, lse, do, name=f"{tag}_dattn")
    dq, dgq = headnorm_bwd(proj, p["q_head_norm"], dqn, col_off=SWA_Q, width=1024, name=f"{tag}_dqnorm")
    dk, dv, dgk = swa_kv_prep_bwd(dkdup, dvdup, proj, p["k_head_norm"], k_off=SWA_K, name=f"{tag}_dkvprep")
    dproj = jnp.concatenate([dq, dgate, dk, dv], axis=1)
    dx, dw_in, dg = _in_bwd(x, h, dproj, dy, p["norm"], p["w_in"], tag)
    return dx, {"norm": dg[0], "w_in": dw_in, "q_head_norm": dgq[0, :HEAD64] + dgq[0, HEAD64:],
                "k_head_norm": dgk[0, :HEAD64] + dgk[0, HEAD64:], "sinks": dsinks[0, :SWA_HEADS], "w_out": dw_out}


def prepare_weights(w):
    l1_in, l2_in = w["l1_w_in"], w["l2_w_in"]
    pad64 = lambda v: jnp.pad(v, (0, MLA_PAD - MLA_QK)).reshape(1, MLA_PAD)
    return [
        {"norm": w["l0_norm"], "w_in": w["l0_w_in"], "w_out": w["l0_w_out"]},
        {"norm": w["l1_norm"],
         "w_in": jnp.concatenate([l1_in[:, 448:], l1_in[:, :448], jnp.zeros((D_MODEL, 64), l1_in.dtype)], axis=1),
         "q_a_norm": w["l1_q_a_norm"], "kv_a_norm": w["l1_kv_a_norm"],
         "w_uq": jnp.pad(w["l1_w_uq"].reshape(256, MLA_HEADS, MLA_QK), ((0, 0), (0, 0), (0, MLA_PAD - MLA_QK))
                         ).reshape(256, MLA_HEADS * MLA_PAD),
         "w_ukv": w["l1_w_ukv"], "gq": pad64(w["l1_q_head_norm"]), "gk": pad64(w["l1_k_head_norm"]),
         "w_out": w["l1_w_out"]},
        {"norm": w["l2_norm"],
         "w_in": jnp.concatenate([l2_in[:, :1024], l2_in[:, 1536:], l2_in[:, 1024:1536]], axis=1),
         "q_head_norm": w["l2_q_head_norm"], "k_head_norm": w["l2_k_head_norm"], "sinks": w["l2_sinks"],
         "w_out": w["l2_w_out"]},
        {"norm": w["l3_norm"], "w_in": w["l3_w_in"], "w_out": w["l3_w_out"]},
    ]


def unprepare_grads(gs):
    g0, g1, g2, g3 = gs
    d1, d2 = g1["w_in"], g2["w_in"]
    return {
        "l0_norm": g0["norm"], "l0_w_in": g0["w_in"], "l0_w_out": g0["w_out"],
        "l1_norm": g1["norm"], "l1_w_in": jnp.concatenate([d1[:, 1024:1472], d1[:, :1024]], axis=1),
        "l1_q_a_norm": g1["q_a_norm"],
        "l1_w_uq": g1["w_uq"].reshape(256, MLA_HEADS, MLA_PAD)[:, :, :MLA_QK].reshape(256, MLA_HEADS * MLA_QK),
        "l1_kv_a_norm": g1["kv_a_norm"], "l1_w_ukv": g1["w_ukv"],
        "l1_q_head_norm": g1["gq"][:MLA_QK], "l1_k_head_norm": g1["gk"][:MLA_QK], "l1_w_out": g1["w_out"],
        "l2_norm": g2["norm"], "l2_w_in": jnp.concatenate([d2[:, :1024], d2[:, 2048:], d2[:, 1024:2048]], axis=1),
        "l2_q_head_norm": g2["q_head_norm"], "l2_k_head_norm": g2["k_head_norm"], "l2_sinks": g2["sinks"],
        "l2_w_out": g2["w_out"],
        "l3_norm": g3["norm"], "l3_w_in": g3["w_in"], "l3_w_out": g3["w_out"],
    }


def local_step(x, target, w):
    ps = prepare_weights(w)
    tabs = _rope_tables(x.shape[0])
    y0, s0 = sb_fwd(x, ps[0], "l0")
    y1, s1 = mla_fwd(y0, ps[1], tabs, "l1")
    y2, s2 = swa_fwd(y1, ps[2], "l2")
    y3, s3 = sb_fwd(y2, ps[3], "l3")
    dy, loss = loss_head(y3, target, name="loss")
    d3, g3 = sb_bwd(dy, s3, ps[3], "l3")
    d2, g2 = swa_bwd(d3, s2, ps[2], "l2")
    d1, g1 = mla_bwd(d2, s1, ps[1], tabs, "l1")
    d0, g0 = sb_bwd(d1, s0, ps[0], "l0")
    return loss, d0, unprepare_grads([g0, g1, g2, g3])


MATS = (("l0_w_in", "col", 1024, 4096), ("l0_w_out", "row", 1024, 1024), ("l1_w_in", "col", 1024, 1472),
        ("l1_w_uq", "col", 256, 1536), ("l1_w_ukv", "col", 128, 2048), ("l1_w_out", "row", 1024, 1024),
        ("l2_w_in", "col", 1024, 2560), ("l2_w_out", "row", 1024, 1024), ("l3_w_in", "col", 1024, 4096),
        ("l3_w_out", "row", 1024, 1024))
N_CHIPS = 4
PACK_W = 1024
HALF_ROWS = 2176
PACK_ROWS = 2 * HALF_ROWS
VECS = (("l0_norm", 0, 0, 1024), ("l1_norm", 1, 0, 1024), ("l2_norm", 2, 0, 1024), ("l3_norm", 3, 0, 1024),
        ("l1_q_a_norm", 4, 0, 256), ("l1_kv_a_norm", 4, 256, 128), ("l1_q_head_norm", 4, 384, 192),
        ("l1_k_head_norm", 4, 576, 192), ("l2_q_head_norm", 4, 768, 64), ("l2_k_head_norm", 4, 832, 64),
        ("l2_sinks", 4, 896, 16))
LOSS_SLOT = (4, 912)
VEC_ROWS = 8


def _shard_rows(k, n):
    return k * n // N_CHIPS // PACK_W


def pack_shards(shards):
    parts = [shards[name].reshape(-1, PACK_W) for name, _, _, _ in MATS]
    used = sum(p.shape[0] for p in parts)
    return jnp.concatenate(parts + [jnp.zeros((PACK_ROWS - used, PACK_W), parts[0].dtype)], axis=0)


def unpack_shards(flat):
    out, r0 = {}, 0
    for name, kind, k, n in MATS:
        rows = _shard_rows(k, n)
        shape = (k, n // N_CHIPS) if kind == "col" else (k // N_CHIPS, n)
        out[name] = flat[r0:r0 + rows].reshape(shape)
        r0 += rows
    return out


def pack_full(mats):
    parts = []
    for name, kind, k, n in MATS:
        m = mats[name]
        if kind == "col":
            m = m.reshape(k, N_CHIPS, n // N_CHIPS).transpose(1, 0, 2)
        parts.append(m.reshape(N_CHIPS, -1, PACK_W))
    used = sum(p.shape[1] for p in parts)
    return jnp.concatenate(parts + [jnp.zeros((N_CHIPS, PACK_ROWS - used, PACK_W), parts[0].dtype)], axis=1)


def unpack_full(stacked):
    out, r0 = {}, 0
    for name, kind, k, n in MATS:
        rows = _shard_rows(k, n)
        seg = stacked[:, r0:r0 + rows]
        if kind == "col":
            out[name] = seg.reshape(N_CHIPS, k, n // N_CHIPS).transpose(1, 0, 2).reshape(k, n)
        else:
            out[name] = seg.reshape(k, n)
        r0 += rows
    return out


def pack_vecs(vecs, loss=None):
    rows = []
    for r in range(VEC_ROWS):
        items = [(off, vecs[name]) for name, rr, off, _ in VECS if rr == r]
        if loss is not None and r == LOSS_SLOT[0]:
            items.append((LOSS_SLOT[1], loss.reshape(1)))
        pos, parts = 0, []
        for off, v in sorted(items, key=lambda t: t[0]):
            assert off == pos
            parts.append(v.astype(F32))
            pos += v.shape[0]
        parts.append(jnp.zeros((PACK_W - pos,), F32))
        rows.append(jnp.concatenate(parts))
    return jnp.stack(rows)


def unpack_vecs(block):
    return {name: block[r, off:off + n] for name, r, off, n in VECS}


def _me():
    return lax.axis_index("x"), lax.axis_index("y"), lax.axis_index("c")


OTHER_CHIPS = ((1, 0), (0, 1), (1, 1))


def _remote(src, dst, send_sem, recv_sem, to):
    return pltpu.make_async_remote_copy(src_ref=src, dst_ref=dst, send_sem=send_sem, recv_sem=recv_sem,
                                        device_id=to, device_id_type=MESH)


def gather_weights(block):
    def body(in_ref, out_ref, send_sems, recv_sems):
        x, y, c = _me()
        sibling = (x, y, 1 - c)

        def half(px, py, pc):
            return out_ref.at[2 * px + py, pl.ds(pc * HALF_ROWS, HALF_ROWS), :]

        chips = [(x ^ dx, y ^ dy) for dx, dy in OTHER_CHIPS]
        first = [_remote(in_ref.at[pl.ds(c * HALF_ROWS, HALF_ROWS), :], half(x, y, c), send_sems.at[j], recv_sems.at[j],
                         (*chip, c)) for j, chip in enumerate(chips)]
        for cp in first:
            cp.start()
        passed = [_remote(half(*chip, c), half(*chip, c), send_sems.at[3 + j], recv_sems.at[3 + j], sibling)
                  for j, chip in enumerate(chips)]
        for j, chip in enumerate(chips):
            _remote(half(*chip, c), half(*chip, c), send_sems.at[j], recv_sems.at[j], (*chip, c)).wait_recv()
            passed[j].start()
        for j, chip in enumerate(chips):
            _remote(half(*chip, 1 - c), half(*chip, 1 - c), send_sems.at[3 + j], recv_sems.at[3 + j], sibling).wait_recv()
        for cp in first + passed:
            cp.wait_send()

    hbm = pl.BlockSpec(memory_space=pltpu.HBM)
    others = pl.pallas_call(
        body, name="gather_weights",
        out_shape=jax.ShapeDtypeStruct((N_CHIPS, PACK_ROWS, PACK_W), block.dtype),
        in_specs=[hbm], out_specs=hbm,
        scratch_shapes=[pltpu.SemaphoreType.DMA((6,)), pltpu.SemaphoreType.DMA((6,))],
    )(block)
    return lax.dynamic_update_slice(others, block[None], (2 * lax.axis_index("x") + lax.axis_index("y"), 0, 0))


def pair_exchange(g):
    def body(g_ref, out_ref, send_sems, recv_sems):
        x, y, c = _me()
        sibling = (x, y, 1 - c)
        copies = [_remote(g_ref.at[k, pl.ds((1 - c) * HALF_ROWS, HALF_ROWS), :], out_ref.at[k], send_sems.at[k],
                          recv_sems.at[k], sibling) for k in range(N_CHIPS)]
        for cp in copies:
            cp.start()
        for cp in copies:
            cp.wait_recv()
        for cp in copies:
            cp.wait_send()

    hbm = pl.BlockSpec(memory_space=pltpu.HBM)
    return pl.pallas_call(
        body, name="pair_exchange",
        out_shape=jax.ShapeDtypeStruct((N_CHIPS, HALF_ROWS, PACK_W), g.dtype),
        in_specs=[hbm], out_specs=hbm,
        scratch_shapes=[pltpu.SemaphoreType.DMA((N_CHIPS,)), pltpu.SemaphoreType.DMA((N_CHIPS,))],
    )(g)


def chip_exchange(part):
    def body(p_ref, out_ref, send_sems, recv_sems):
        x, y, c = _me()
        copies = [_remote(p_ref.at[2 * (x ^ dx) + (y ^ dy)], out_ref.at[j], send_sems.at[j], recv_sems.at[j],
                          (x ^ dx, y ^ dy, c)) for j, (dx, dy) in enumerate(OTHER_CHIPS)]
        for cp in copies:
            cp.start()
        for cp in copies:
            cp.wait_recv()
        for cp in copies:
            cp.wait_send()

    hbm = pl.BlockSpec(memory_space=pltpu.HBM)
    return pl.pallas_call(
        body, name="chip_exchange",
        out_shape=jax.ShapeDtypeStruct((len(OTHER_CHIPS), HALF_ROWS, PACK_W), part.dtype),
        in_specs=[hbm], out_specs=hbm,
        scratch_shapes=[pltpu.SemaphoreType.DMA((3,)), pltpu.SemaphoreType.DMA((3,))],
    )(part)


def join_halves(half):
    def body(h_ref, out_ref, send_sem, recv_sem):
        x, y, c = _me()
        cp = _remote(h_ref, out_ref.at[pl.ds(c * HALF_ROWS, HALF_ROWS), :], send_sem, recv_sem, (x, y, 1 - c))
        cp.start()
        _remote(h_ref, out_ref.at[pl.ds((1 - c) * HALF_ROWS, HALF_ROWS), :], send_sem, recv_sem, (x, y, 1 - c)).wait_recv()
        cp.wait_send()

    hbm = pl.BlockSpec(memory_space=pltpu.HBM)
    other = pl.pallas_call(
        body, name="join_halves",
        out_shape=jax.ShapeDtypeStruct((PACK_ROWS, PACK_W), half.dtype),
        in_specs=[hbm], out_specs=hbm,
        scratch_shapes=[pltpu.SemaphoreType.DMA, pltpu.SemaphoreType.DMA],
    )(half)
    return lax.dynamic_update_slice(other, half, (lax.axis_index("c") * HALF_ROWS, 0))


def sum_over_devices(block):
    def body(in_ref, out_ref, all_ref, send_sems, recv_sems):
        x, y, c = _me()
        me = 4 * x + 2 * y + c
        all_ref[me] = in_ref[...]
        copies = []
        for r in range(1, 8):
            to = (x ^ (r >> 2), y ^ ((r >> 1) & 1), c ^ (r & 1))
            copies.append(_remote(in_ref, all_ref.at[me], send_sems.at[r - 1], recv_sems.at[r - 1], to))
        for cp in copies:
            cp.start()
        for r in range(1, 8):
            frm = (x ^ (r >> 2), y ^ ((r >> 1) & 1), c ^ (r & 1))
            _remote(in_ref, all_ref.at[4 * frm[0] + 2 * frm[1] + frm[2]], send_sems.at[r - 1], recv_sems.at[r - 1],
                    frm).wait_recv()
        for cp in copies:
            cp.wait_send()
        acc = all_ref[0]
        for d in range(1, 8):
            acc = acc + all_ref[d]
        out_ref[...] = acc

    vmem = pl.BlockSpec(memory_space=pltpu.VMEM)
    return pl.pallas_call(
        body, name="sum_over_devices",
        out_shape=jax.ShapeDtypeStruct(block.shape, F32),
        in_specs=[vmem], out_specs=vmem,
        scratch_shapes=[pltpu.VMEM((8,) + block.shape, F32), pltpu.SemaphoreType.DMA((7,)), pltpu.SemaphoreType.DMA((7,))],
    )(block)


SUM_ROWS = 272


def pair_sum(g, got, core):
    steps = HALF_ROWS // SUM_ROWS

    def body(s_ref, g_ref, r_ref, o_ref, ob_ref):
        t = g_ref[...] + r_ref[...]
        o_ref[...] = t
        ob_ref[...] = t.astype(BF16)

    blk = pl.BlockSpec((1, SUM_ROWS, PACK_W), lambda k, i, s: (k, i, 0))
    return pl.pallas_call(
        body, name="pair_sum",
        grid_spec=pltpu.PrefetchScalarGridSpec(
            num_scalar_prefetch=1, grid=(N_CHIPS, steps),
            in_specs=[pl.BlockSpec((1, SUM_ROWS, PACK_W), lambda k, i, s: (k, s[0] * steps + i, 0)), blk],
            out_specs=[blk, blk]),
        out_shape=[jax.ShapeDtypeStruct((N_CHIPS, HALF_ROWS, PACK_W), F32),
                   jax.ShapeDtypeStruct((N_CHIPS, HALF_ROWS, PACK_W), BF16)],
        compiler_params=_params(("parallel", "parallel"), 8 * SUM_ROWS * PACK_W * 4),
    )(core, g, got)


def chip_sum(part, got, chip):
    steps = HALF_ROWS // SUM_ROWS

    def body(s_ref, p_ref, a_ref, b_ref, c_ref, o_ref):
        o_ref[...] = ((p_ref[0] + a_ref[0].astype(F32)) + b_ref[0].astype(F32)) + c_ref[0].astype(F32)

    def got_spec(j):
        return pl.BlockSpec((1, SUM_ROWS, PACK_W), lambda i, s: (j, i, 0))

    return pl.pallas_call(
        body, name="chip_sum",
        grid_spec=pltpu.PrefetchScalarGridSpec(
            num_scalar_prefetch=1, grid=(steps,),
            in_specs=[pl.BlockSpec((1, SUM_ROWS, PACK_W), lambda i, s: (s[0], i, 0)), got_spec(0), got_spec(1), got_spec(2)],
            out_specs=pl.BlockSpec((SUM_ROWS, PACK_W), lambda i, s: (i, 0))),
        out_shape=jax.ShapeDtypeStruct((HALF_ROWS, PACK_W), F32),
        compiler_params=_params(("parallel",), 10 * SUM_ROWS * PACK_W * 4),
    )(chip, part, got, got, got)


def reduce_scatter(g):
    x, y, c = _me()
    part, part_bf16 = pair_sum(g, pair_exchange(g), jnp.reshape(c, (1,)).astype(I32))
    half = chip_sum(part, chip_exchange(part_bf16), jnp.reshape(2 * x + y, (1,)).astype(I32))
    return join_halves(half)


def adamw(w, g, m, v, *, name):
    rows, cols = w.shape
    tm = _tile(rows, 256)
    c1 = 1.0 - ADAM_B1 ** ADAM_STEP
    c2 = 1.0 - ADAM_B2 ** ADAM_STEP

    def body(w_ref, g_ref, m_ref, v_ref, d_ref, mo_ref, vo_ref):
        gv = g_ref[...]
        mn = ADAM_B1 * m_ref[...] + (1.0 - ADAM_B1) * gv
        vn = ADAM_B2 * v_ref[...] + (1.0 - ADAM_B2) * (gv * gv)
        d_ref[...] = -ADAM_LR * ((mn / c1) / (jnp.sqrt(vn / c2) + ADAM_EPS) + ADAM_WD * w_ref[...])
        mo_ref[...] = mn
        vo_ref[...] = vn

    blk = pl.BlockSpec((tm, cols), lambda i: (i, 0))
    shape = jax.ShapeDtypeStruct(w.shape, F32)
    return _pallas(
        body, name=name, grid=(rows // tm,),
        in_specs=[blk] * 4, out_specs=[blk] * 3, out_shape=[shape] * 3,
        compiler_params=_params(("parallel",), 16 * tm * cols * 4),
    )(w, g, m, v)


WEIGHTS = ("l0_norm", "l0_w_in", "l0_w_out", "l1_norm", "l1_w_in", "l1_q_a_norm", "l1_w_uq", "l1_kv_a_norm", "l1_w_ukv",
           "l1_q_head_norm", "l1_k_head_norm", "l1_w_out", "l2_norm", "l2_w_in", "l2_q_head_norm", "l2_k_head_norm",
           "l2_sinks", "l2_w_out", "l3_norm", "l3_w_in", "l3_w_out")


def kernel(x, l0_norm, l0_w_in, l0_w_out, l1_norm, l1_w_in, l1_q_a_norm, l1_w_uq, l1_kv_a_norm, l1_w_ukv, l1_q_head_norm, l1_k_head_norm, l1_w_out, l2_norm, l2_w_in, l2_q_head_norm, l2_k_head_norm, l2_sinks, l2_w_out, l3_norm, l3_w_in, l3_w_out, loss_target, m_l0_norm, m_l0_w_in, m_l0_w_out, m_l1_norm, m_l1_w_in, m_l1_q_a_norm, m_l1_w_uq, m_l1_kv_a_norm, m_l1_w_ukv, m_l1_q_head_norm, m_l1_k_head_norm, m_l1_w_out, m_l2_norm, m_l2_w_in, m_l2_q_head_norm, m_l2_k_head_norm, m_l2_sinks, m_l2_w_out, m_l3_norm, m_l3_w_in, m_l3_w_out, v_l0_norm, v_l0_w_in, v_l0_w_out, v_l1_norm, v_l1_w_in, v_l1_q_a_norm, v_l1_w_uq, v_l1_kv_a_norm, v_l1_w_ukv, v_l1_q_head_norm, v_l1_k_head_norm, v_l1_w_out, v_l2_norm, v_l2_w_in, v_l2_q_head_norm, v_l2_k_head_norm, v_l2_sinks, v_l2_w_out, v_l3_norm, v_l3_w_in, v_l3_w_out):
    given = dict(locals())
    w = {n: given[n] for n in WEIGHTS}
    m = {n: given["m_" + n] for n in WEIGHTS}
    v = {n: given["v_" + n] for n in WEIGHTS}
    mat_names = [t[0] for t in MATS]
    vec_names = [t[0] for t in VECS]

    gathered = gather_weights(pack_shards({n: w[n] for n in mat_names}).astype(BF16))
    full = unpack_full(gathered)
    full.update({n: w[n] for n in vec_names})
    loss_tile, grad_x, grads = local_step(x[0], loss_target[0], full)

    block = reduce_scatter(pack_full({n: grads[n] for n in mat_names}))
    g = unpack_shards(block)
    vec_sum = sum_over_devices(pack_vecs({n: grads[n] for n in vec_names}, loss=loss_tile[0, 0]))
    loss = vec_sum[LOSS_SLOT[0], LOSS_SLOT[1]]

    delta, new_m, new_v = {}, {}, {}
    for n in mat_names:
        delta[n], new_m[n], new_v[n] = adamw(w[n], g[n], m[n], v[n], name=f"adamw_{n}")
    dv, mv, vv = adamw(pack_vecs(w), vec_sum, pack_vecs(m), pack_vecs(v), name="adamw_vecs")
    g.update(unpack_vecs(vec_sum))
    delta.update(unpack_vecs(dv))
    new_m.update(unpack_vecs(mv))
    new_v.update(unpack_vecs(vv))
    return (loss, grad_x[None], *[g[n] for n in WEIGHTS], *[delta[n] for n in WEIGHTS],
            *[new_m[n] for n in WEIGHTS], *[new_v[n] for n in WEIGHTS])
```

```python
import math

import jax
import jax.numpy as jnp
from jax import lax
from jax.experimental import pallas as pl
from jax.experimental.pallas import tpu as pltpu

F32 = jnp.float32
BF16 = jnp.bfloat16
I32 = jnp.int32
MESH = pl.DeviceIdType.MESH

NORM_EPS = 1e-6
D_MODEL = 1024
HEAD64 = 64
LANES = 128
BLK = 128
MLA_HEADS = 8
MLA_QK = 192
MLA_PAD = 256
ROPE_THETA = 10000.0
SWA_HEADS = 16
SWA_KV = 4
VMEM_CAP = 56 * 1024 * 1024
MATMUL_TILE_BYTES = 8 * 1024 * 1024

ADAM_LR, ADAM_B1, ADAM_B2, ADAM_EPS, ADAM_WD, ADAM_STEP = 0.001, 0.9, 0.999, 1e-08, 0.01, 10

NT = (((1,), (1,)), ((), ()))
NN = (((1,), (0,)), ((), ()))
TN = (((0,), (0,)), ((), ()))


def _dot(a, b, dims=NN):
    return lax.dot_general(a, b, dims, preferred_element_type=F32)


def _tile(n, pref):
    for t in (pref, 512, 256, 128):
        if t <= pref and n % t == 0:
            return t
    return n


def _params(sem, vmem_bytes):
    limit = int(min(max(2 * vmem_bytes, 24 * 1024 * 1024), VMEM_CAP))
    return pltpu.CompilerParams(dimension_semantics=sem, vmem_limit_bytes=limit)


def _in_hbm(s):
    return pltpu.HBM(s.shape, s.dtype) if len(s.shape) >= 2 else s


def _pallas(*args, out_shape, **kwargs):
    out_shape = [_in_hbm(s) for s in out_shape] if isinstance(out_shape, (list, tuple)) else _in_hbm(out_shape)
    call = pl.pallas_call(*args, out_shape=out_shape, **kwargs)

    def run(*operands):
        return call(*[pltpu.with_memory_space_constraint(a, pltpu.HBM) if a.ndim >= 2 else a for a in operands])

    return run


def _split(v):
    hi = v.astype(BF16)
    return hi, (v - hi.astype(F32)).astype(BF16)


def _dot2(v, m):
    hi, lo = _split(v)
    return _dot(hi, m) + _dot(lo, m)


def _dot_split(v, m2):
    return _dot(jnp.concatenate(_split(v), axis=1), m2)


def _first_half(shape):
    return lax.broadcasted_iota(I32, shape, 1) < HEAD64


def _sigmoid(g):
    return 1.0 / (1.0 + jnp.exp(-g))


def matmul(a, b, mode, *, name, out_dtype=F32, add=None, b_cols=None, tm=512, tn=1024):
    if mode == "nn":
        (M, K), (K2, N) = a.shape, b.shape
        if b_cols is not None:
            N = b_cols[1]
    elif mode == "nt":
        (M, K), (N, K2) = a.shape, b.shape
    else:
        (K, M), (K2, N) = a.shape, b.shape
    assert K == K2, (a.shape, b.shape, mode)
    tm, tn = _tile(M, tm), _tile(N, tn)
    while K * tm * a.dtype.itemsize > MATMUL_TILE_BYTES:
        tm //= 2
    while K * tn * b.dtype.itemsize > MATMUL_TILE_BYTES:
        tn //= 2
    dims = {"nn": NN, "nt": NT, "tn": TN}[mode]
    n_in = 2 if add is None else 3

    def body(*refs):
        a_ref, b_ref = refs[:2]
        r = _dot(a_ref[...].astype(BF16), b_ref[...].astype(BF16), dims)
        if add is not None:
            r = r + refs[2][...]
        for o_ref in refs[n_in:]:
            o_ref[...] = r.astype(o_ref.dtype)

    a_spec = pl.BlockSpec((K, tm), lambda i, j: (0, i)) if mode == "tn" else pl.BlockSpec((tm, K), lambda i, j: (i, 0))
    jb = 0 if b_cols is None else b_cols[0] // tn
    assert b_cols is None or (mode == "nn" and b_cols[0] % tn == 0)
    b_spec = pl.BlockSpec((tn, K), lambda i, j: (j, 0)) if mode == "nt" else pl.BlockSpec((K, tn), lambda i, j: (0, jb + j))
    o_spec = pl.BlockSpec((tm, tn), lambda i, j: (i, j))
    in_specs, args = [a_spec, b_spec], [a, b]
    if add is not None:
        in_specs.append(o_spec)
        args.append(add)
    vm = 2 * (tm * K * a.dtype.itemsize + K * tn * b.dtype.itemsize) + 5 * tm * tn * 4
    out_dtypes = list(out_dtype) if isinstance(out_dtype, (tuple, list)) else [out_dtype]
    res = _pallas(
        body, name=name, grid=(M // tm, N // tn),
        in_specs=in_specs, out_specs=[o_spec] * len(out_dtypes),
        out_shape=[jax.ShapeDtypeStruct((M, N), d) for d in out_dtypes],
        compiler_params=_params(("parallel", "parallel"), vm),
    )(*args)
    return res[0] if len(out_dtypes) == 1 else res


def rmsnorm_fwd(x, g, *, col_off, width, out_dtype, name, tm=256):
    S = x.shape[0]
    assert col_off % width == 0
    cb = col_off // width
    tm = _tile(S, tm)

    def body(x_ref, g_ref, o_ref):
        v = x_ref[...]
        r = lax.rsqrt(jnp.mean(v * v, axis=1, keepdims=True) + NORM_EPS)
        o_ref[...] = (v * r * g_ref[...]).astype(out_dtype)

    return _pallas(
        body, name=name, grid=(S // tm,),
        in_specs=[pl.BlockSpec((tm, width), lambda i: (i, cb)), pl.BlockSpec((1, width), lambda i: (0, 0))],
        out_specs=pl.BlockSpec((tm, width), lambda i: (i, 0)),
        out_shape=jax.ShapeDtypeStruct((S, width), out_dtype),
        compiler_params=_params(("parallel",), 4 * tm * width * 4),
    )(x, g.reshape(1, width))


def rmsnorm_bwd(x, g, dh, *, col_off, width, out_dtype, name, res=None, tm=256):
    S = x.shape[0]
    cb = col_off // width
    tm = _tile(S, tm)

    def body(*refs):
        if res is None:
            x_ref, g_ref, dh_ref, dx_ref, dg_ref = refs
        else:
            x_ref, g_ref, dh_ref, res_ref, dx_ref, dg_ref = refs

        @pl.when(pl.program_id(0) == 0)
        def _():
            dg_ref[...] = jnp.zeros_like(dg_ref)

        v = x_ref[...]
        dhv = dh_ref[...].astype(F32)
        r = lax.rsqrt(jnp.mean(v * v, axis=1, keepdims=True) + NORM_EPS)
        y = v * r
        dy = dhv * g_ref[...]
        dx = r * (dy - y * jnp.mean(dy * y, axis=1, keepdims=True))
        if res is not None:
            dx = dx + res_ref[...]
        dx_ref[...] = dx.astype(out_dtype)
        dg_ref[...] += jnp.sum(dhv * y, axis=0, keepdims=True)

    row = pl.BlockSpec((tm, width), lambda i: (i, 0))
    in_specs = [pl.BlockSpec((tm, width), lambda i: (i, cb)), pl.BlockSpec((1, width), lambda i: (0, 0)), row]
    args = [x, g.reshape(1, width), dh]
    if res is not None:
        in_specs.append(row)
        args.append(res)
    return _pallas(
        body, name=name, grid=(S // tm,),
        in_specs=in_specs,
        out_specs=[row, pl.BlockSpec((1, width), lambda i: (0, 0))],
        out_shape=[jax.ShapeDtypeStruct((S, width), out_dtype), jax.ShapeDtypeStruct((1, width), F32)],
        compiler_params=_params(("arbitrary",), 8 * tm * width * 4),
    )(*args)


def _group_ones():
    r = lax.broadcasted_iota(I32, (LANES, LANES), 0) // HEAD64
    c = lax.broadcasted_iota(I32, (LANES, LANES), 1) // HEAD64
    return (r == c).astype(BF16)


def _segmean64(v):
    return _dot2(v, _group_ones()) * (1.0 / HEAD64)


def headnorm_fwd(x, g64, *, col_off, width, name, tm=512):
    S = x.shape[0]
    cb = col_off // LANES
    tm = _tile(S, tm)

    def body(x_ref, g_ref, o_ref):
        v = x_ref[...]
        r = lax.rsqrt(_segmean64(v * v) + NORM_EPS)
        o_ref[...] = (v * r * g_ref[...]).astype(BF16)

    return _pallas(
        body, name=name, grid=(S // tm, width // LANES),
        in_specs=[pl.BlockSpec((tm, LANES), lambda i, j: (i, cb + j)), pl.BlockSpec((1, LANES), lambda i, j: (0, 0))],
        out_specs=pl.BlockSpec((tm, LANES), lambda i, j: (i, j)),
        out_shape=jax.ShapeDtypeStruct((S, width), BF16),
        compiler_params=_params(("parallel", "parallel"), 8 * tm * LANES * 4),
    )(x, jnp.tile(g64, 2).reshape(1, LANES))


def headnorm_bwd(x, g64, dh, *, col_off, width, name, tm=512):
    S = x.shape[0]
    cb = col_off // LANES
    tm = _tile(S, tm)

    def body(x_ref, g_ref, dh_ref, dx_ref, dg_ref):
        @pl.when((pl.program_id(0) == 0) & (pl.program_id(1) == 0))
        def _():
            dg_ref[...] = jnp.zeros_like(dg_ref)

        v = x_ref[...]
        dhv = dh_ref[...]
        r = lax.rsqrt(_segmean64(v * v) + NORM_EPS)
        y = v * r
        dy = dhv * g_ref[...]
        dx_ref[...] = (r * (dy - y * _segmean64(dy * y))).astype(BF16)
        dg_ref[...] += jnp.sum(dhv * y, axis=0, keepdims=True)

    return _pallas(
        body, name=name, grid=(width // LANES, S // tm),
        in_specs=[pl.BlockSpec((tm, LANES), lambda j, i: (i, cb + j)), pl.BlockSpec((1, LANES), lambda j, i: (0, 0)),
                  pl.BlockSpec((tm, LANES), lambda j, i: (i, j))],
        out_specs=[pl.BlockSpec((tm, LANES), lambda j, i: (i, j)), pl.BlockSpec((1, LANES), lambda j, i: (0, 0))],
        out_shape=[jax.ShapeDtypeStruct((S, width), BF16), jax.ShapeDtypeStruct((1, LANES), F32)],
        compiler_params=_params(("arbitrary", "arbitrary"), 10 * tm * LANES * 4),
    )(x, jnp.tile(g64, 2).reshape(1, LANES), dh)


def gate_fwd(o, proj, *, gate_off, name, tm=256):
    S, W = o.shape
    cb = gate_off // W
    tm = _tile(S, tm)

    def body(o_ref, g_ref, out_ref):
        g = g_ref[...]
        out_ref[...] = (o_ref[...] * (g * _sigmoid(g))).astype(BF16)

    return _pallas(
        body, name=name, grid=(S // tm,),
        in_specs=[pl.BlockSpec((tm, W), lambda i: (i, 0)), pl.BlockSpec((tm, W), lambda i: (i, cb))],
        out_specs=pl.BlockSpec((tm, W), lambda i: (i, 0)),
        out_shape=jax.ShapeDtypeStruct((S, W), BF16),
        compiler_params=_params(("parallel",), 6 * tm * W * 4),
    )(o, proj)


def gate_bwd(dog, o, proj, *, gate_off, name, tm=256):
    S, W = o.shape
    cb = gate_off // W
    tm = _tile(S, tm)

    def body(d_ref, o_ref, g_ref, do_ref, dg_ref):
        g = g_ref[...]
        d = d_ref[...]
        s = _sigmoid(g)
        do_ref[...] = d * (g * s)
        dg_ref[...] = (d * o_ref[...] * (s * (1.0 + g * (1.0 - s)))).astype(BF16)

    row = pl.BlockSpec((tm, W), lambda i: (i, 0))
    return _pallas(
        body, name=name, grid=(S // tm,),
        in_specs=[row, row, pl.BlockSpec((tm, W), lambda i: (i, cb))],
        out_specs=[row, row],
        out_shape=[jax.ShapeDtypeStruct((S, W), F32), jax.ShapeDtypeStruct((S, W), BF16)],
        compiler_params=_params(("parallel",), 10 * tm * W * 4),
    )(dog, o, proj)


def loss_head(y, target, *, name, tm=256):
    S, W = y.shape
    tm = _tile(S, tm)
    n = S // tm

    def body(y_ref, t_ref, dy_ref, l_ref, acc_ref):
        i = pl.program_id(0)

        @pl.when(i == 0)
        def _():
            acc_ref[...] = jnp.zeros_like(acc_ref)

        e = y_ref[...] - t_ref[...]
        dy_ref[...] = e * (1.0 / W)
        acc_ref[...] += jnp.sum(e * e, axis=0, keepdims=True)

        @pl.when(i == n - 1)
        def _():
            l_ref[...] = jnp.full(l_ref.shape, (0.5 / W) * jnp.sum(acc_ref[...]), F32)

    row = pl.BlockSpec((tm, W), lambda i: (i, 0))
    return _pallas(
        body, name=name, grid=(n,),
        in_specs=[row, row],
        out_specs=[row, pl.BlockSpec((8, LANES), lambda i: (0, 0))],
        out_shape=[jax.ShapeDtypeStruct((S, W), F32), jax.ShapeDtypeStruct((8, LANES), F32)],
        scratch_shapes=[pltpu.VMEM((1, W), F32)],
        compiler_params=_params(("arbitrary",), 8 * tm * W * 4),
    )(y, target)


def _stack_heads(t, zero):
    first = _first_half(t.shape)
    return jnp.concatenate([jnp.where(first, t, zero), jnp.where(first, zero, t)], axis=0)


def _sb_weights(qs, ks, mask, upper, rss):
    zs = [_dot(q, k, NT) for q, k in zip(qs, ks)]
    sps = [jnp.maximum(z, 0.0) + jnp.log(1.0 + jnp.exp(-jnp.abs(z))) for z in zs]
    gs = [z - sp for z, sp in zip(zs, sps)]
    if mask is not None:
        sps = [jnp.where(mask, sp, 0.0) for sp in sps]
    cums = [_dot_split(sp, upper) for sp in sps]
    avs = [jnp.exp(g - (cum + rs)) for g, cum, rs in zip(gs, cums, rss)]
    if mask is not None:
        avs = [jnp.where(mask, a, 0.0) for a in avs]
    return avs, sps, gs


def _sb_consts():
    row = lax.broadcasted_iota(I32, (BLK, BLK), 0)
    col = lax.broadcasted_iota(I32, (BLK, BLK), 1)
    diag = col < row
    return row, col, jnp.concatenate([diag, diag], axis=0)


SB_DEAD = 88.0


def _sb_walk_left(block, i, carry):
    def least(c):
        m = c[0][1]
        for pair in c[1:]:
            m = jnp.minimum(m, pair[1])
        return jnp.min(m)

    def cond(state):
        jj, _, low = state
        return (jj < i) & (low < SB_DEAD)

    def body(state):
        jj, c, _ = state
        c = block(i - 1 - jj, c, None)
        return jj + 1, c, least(c)

    return lax.while_loop(cond, body, (jnp.int32(0), carry, least(carry)))[1]


SB_W = 1024


def sb_attn_fwd(qkv, *, name, pairs=8):
    S = qkv.shape[0]
    W = SB_W
    PW = pairs * LANES
    ngrp, nq = W // PW, S // BLK

    def body(q_ref, k_ref, v_ref, o_ref):
        i = pl.program_id(1)
        row, col, diag = _sb_consts()
        upper = jnp.tile((row > col).astype(BF16), (2, 1))
        zero = jnp.zeros((BLK, LANES), BF16)
        qs = [_stack_heads(q_ref[:, p * LANES:(p + 1) * LANES] * 0.125, zero) for p in range(pairs)]

        def block(j, carry, mask):
            rows = pl.ds(pl.multiple_of(j * BLK, BLK), BLK)
            cols = [slice(p * LANES, (p + 1) * LANES) for p in range(pairs)]
            avs, sps, _ = _sb_weights(qs, [k_ref[rows, c] for c in cols], mask, upper, [c[1] for c in carry])
            abs_ = [a.astype(BF16) for a in avs]
            outs = [_dot(jnp.concatenate([ab[:BLK], ab[BLK:]], axis=1), _stack_heads(v_ref[rows, c], zero))
                    for ab, c in zip(abs_, cols)]
            return tuple((carry[p][0] + outs[p], carry[p][1] + jnp.sum(sps[p], axis=1, keepdims=True))
                         for p in range(pairs))

        init = tuple((jnp.zeros((BLK, LANES), F32), jnp.zeros((2 * BLK, 1), F32)) for _ in range(pairs))
        carry = _sb_walk_left(block, i, block(i, init, diag))
        for p in range(pairs):
            o_ref[:, p * LANES:(p + 1) * LANES] = carry[p][0]

    once = pl.Buffered(1)
    return _pallas(
        body, name=name, grid=(ngrp, nq),
        in_specs=[pl.BlockSpec((BLK, PW), lambda p, i: (i, p)),
                  pl.BlockSpec((S, PW), lambda p, i: (0, ngrp + p), pipeline_mode=once),
                  pl.BlockSpec((S, PW), lambda p, i: (0, 2 * ngrp + p), pipeline_mode=once)],
        out_specs=pl.BlockSpec((BLK, PW), lambda p, i: (i, p)),
        out_shape=jax.ShapeDtypeStruct((S, W), F32),
        compiler_params=_params(("parallel", "arbitrary"), 2 * S * PW * 2 + 16 * BLK * PW * 4),
    )(qkv, qkv, qkv)


def sb_attn_bwd(qkv, o, do, *, name, pairs=4):
    S = qkv.shape[0]
    W = SB_W
    PW = pairs * LANES
    ngrp, nq = W // PW, S // BLK

    def body(q_ref, k_ref, v_ref, o_ref, do_ref, dq_ref, dk_ref, dv_ref):
        i = pl.program_id(1)

        @pl.when(i == 0)
        def _():
            dk_ref[...] = jnp.zeros_like(dk_ref)
            dv_ref[...] = jnp.zeros_like(dv_ref)

        row, col, diag = _sb_consts()
        upper = jnp.tile((row > col).astype(BF16), (2, 1))
        upper_incl = jnp.tile((row >= col).astype(BF16), (2, 1))
        first = _first_half((BLK, LANES))
        zero = jnp.zeros((BLK, LANES), BF16)
        qs, dos, tots = [], [], []
        for p in range(pairs):
            cols = slice(p * LANES, (p + 1) * LANES)
            qs.append(_stack_heads(q_ref[:, cols] * 0.125, zero))
            dob = do_ref[:, cols].astype(BF16)
            dos.append(_stack_heads(dob, zero))
            prod = dob.astype(F32) * o_ref[:, cols]
            tots.append(jnp.concatenate([jnp.sum(jnp.where(first, prod, 0.0), axis=1, keepdims=True),
                                         jnp.sum(jnp.where(first, 0.0, prod), axis=1, keepdims=True)], axis=0))

        def block(j, carry, mask):
            rows = pl.ds(pl.multiple_of(j * BLK, BLK), BLK)
            P = range(pairs)
            cols = [slice(p * LANES, (p + 1) * LANES) for p in P]
            ks = [k_ref[rows, c] for c in cols]
            das = [_dot(dos[p], v_ref[rows, cols[p]], NT) for p in P]
            avs, sps, gs = _sb_weights(qs, ks, mask, upper, [c[1] for c in carry])
            abs_ = [a.astype(BF16) for a in avs]
            es = [ab.astype(F32) * da for ab, da in zip(abs_, das)]
            sufs = [_dot_split(e, upper_incl) for e in es]
            lefts = [tots[p] - (sufs[p] + carry[p][2]) for p in P]
            dzs = [es[p] - jnp.exp(gs[p]) * (es[p] + lefts[p]) for p in P]
            if mask is not None:
                dzs = [jnp.where(mask, dz, 0.0) for dz in dzs]
            dzbs = [dz.astype(BF16) for dz in dzs]
            dks = [_dot(dzbs[p], qs[p], TN) for p in P]
            dvs = [_dot(abs_[p], dos[p], TN) for p in P]
            dqs = [_dot(jnp.concatenate([dzbs[p][:BLK], dzbs[p][BLK:]], axis=1), _stack_heads(ks[p], zero)) for p in P]
            for p in P:
                dk_ref[rows, cols[p]] += dks[p]
                dv_ref[rows, cols[p]] += dvs[p]
            return tuple((carry[p][0] + dqs[p], carry[p][1] + jnp.sum(sps[p], axis=1, keepdims=True),
                          carry[p][2] + jnp.sum(es[p], axis=1, keepdims=True)) for p in P)

        col0 = jnp.zeros((2 * BLK, 1), F32)
        init = tuple((jnp.zeros((BLK, LANES), F32), col0, col0) for _ in range(pairs))
        carry = _sb_walk_left(block, i, block(i, init, diag))
        for p in range(pairs):
            dq_ref[:, p * LANES:(p + 1) * LANES] = (carry[p][0] * 0.125).astype(BF16)

    once = pl.Buffered(1)
    tile = pl.BlockSpec((BLK, PW), lambda p, i: (i, p))
    full = pl.BlockSpec((S, PW), lambda p, i: (0, p), pipeline_mode=once)
    shape = jax.ShapeDtypeStruct((S, W), F32)
    return _pallas(
        body, name=name, grid=(ngrp, nq),
        in_specs=[tile,
                  pl.BlockSpec((S, PW), lambda p, i: (0, ngrp + p), pipeline_mode=once),
                  pl.BlockSpec((S, PW), lambda p, i: (0, 2 * ngrp + p), pipeline_mode=once),
                  tile, tile],
        out_specs=[tile, full, full],
        out_shape=[jax.ShapeDtypeStruct((S, W), BF16), shape, shape],
        compiler_params=_params(("parallel", "arbitrary"), 2 * S * PW * 2 + 2 * S * PW * 4 + 16 * BLK * PW * 4),
    )(qkv, qkv, qkv, o, do)


FB = 256


def flash_fwd(qn, kn, vb, *, name, heads=4):
    S = qn.shape[0]
    H, DK, DV = MLA_HEADS, MLA_PAD, LANES
    nq, ngrp = S // FB, H // heads
    scale = 1.0 / math.sqrt(MLA_QK)

    def body(q_ref, k_ref, v_ref, o_ref, lse_ref):
        i = pl.program_id(1)
        diag = lax.broadcasted_iota(I32, (FB, FB), 1) <= lax.broadcasted_iota(I32, (FB, FB), 0)
        qs = [q_ref[:, h * DK:(h + 1) * DK] for h in range(heads)]

        def block(j, carry, mask):
            rows = pl.ds(pl.multiple_of(j * FB, FB), FB)
            H = range(heads)
            ss = [_dot(qs[h], k_ref[rows, h * DK:(h + 1) * DK], NT) * scale for h in H]
            if mask is not None:
                ss = [jnp.where(mask, s, -1e30) for s in ss]
            ms = [jnp.maximum(carry[h][1], jnp.max(ss[h], axis=1, keepdims=True)) for h in H]
            ps = [jnp.exp(ss[h] - ms[h]) for h in H]
            ws = [jnp.exp(carry[h][1] - ms[h]) for h in H]
            pvs = [_dot(ps[h].astype(BF16), v_ref[rows, h * DV:(h + 1) * DV]) for h in H]
            return tuple((carry[h][0] * ws[h] + pvs[h], ms[h], carry[h][2] * ws[h] + jnp.sum(ps[h], axis=1, keepdims=True))
                         for h in H)

        init = tuple((jnp.zeros((FB, DV), F32), jnp.full((FB, 1), -1e30, F32), jnp.zeros((FB, 1), F32))
                     for _ in range(heads))
        carry = lax.fori_loop(0, i, lambda j, c: block(j, c, None), init)
        carry = block(i, carry, diag)
        for h in range(heads):
            acc, m, l = carry[h]
            o_ref[:, h * DV:(h + 1) * DV] = acc / l
            lse_ref[h] = m + jnp.log(l)

    return _pallas(
        body, name=name, grid=(ngrp, nq),
        in_specs=[pl.BlockSpec((FB, heads * DK), lambda g, i: (i, g)),
                  pl.BlockSpec((S, heads * DK), lambda g, i: (0, g), pipeline_mode=pl.Buffered(1)),
                  pl.BlockSpec((S, heads * DV), lambda g, i: (0, g), pipeline_mode=pl.Buffered(1))],
        out_specs=[pl.BlockSpec((FB, heads * DV), lambda g, i: (i, g)), pl.BlockSpec((heads, FB, 1), lambda g, i: (g, i, 0))],
        out_shape=[jax.ShapeDtypeStruct((S, H * DV), F32), jax.ShapeDtypeStruct((H, S, 1), F32)],
        compiler_params=_params(("parallel", "arbitrary"), 2 * S * heads * (DK + DV) * 2 + 16 * FB * FB * 4),
    )(qn, kn, vb)


def flash_bwd(qn, kn, vb, o, lse, do, *, name, heads=4):
    S = qn.shape[0]
    H, DK, DV = MLA_HEADS, MLA_PAD, LANES
    nq, ngrp = S // FB, H // heads
    scale = 1.0 / math.sqrt(MLA_QK)

    def body(q_ref, k_ref, v_ref, o_ref, lse_ref, do_ref, dq_ref, dk_ref, dv_ref):
        i = pl.program_id(1)

        @pl.when(i == 0)
        def _():
            dk_ref[...] = jnp.zeros_like(dk_ref)
            dv_ref[...] = jnp.zeros_like(dv_ref)

        diag = lax.broadcasted_iota(I32, (FB, FB), 1) <= lax.broadcasted_iota(I32, (FB, FB), 0)
        qs, dobs, deltas, lses = [], [], [], []
        for h in range(heads):
            do = do_ref[:, h * DV:(h + 1) * DV]
            qs.append(q_ref[:, h * DK:(h + 1) * DK])
            dobs.append(do.astype(BF16))
            deltas.append(jnp.sum(do * o_ref[:, h * DV:(h + 1) * DV], axis=1, keepdims=True))
            lses.append(lse_ref[h])

        def block(j, carry, mask):
            rows = pl.ds(pl.multiple_of(j * FB, FB), FB)
            H = range(heads)
            kcs = [slice(h * DK, (h + 1) * DK) for h in H]
            vcs = [slice(h * DV, (h + 1) * DV) for h in H]
            ks = [k_ref[rows, kcs[h]] for h in H]
            ss = [_dot(qs[h], ks[h], NT) for h in H]
            dps = [_dot(dobs[h], v_ref[rows, vcs[h]], NT) for h in H]
            ps = [jnp.exp(ss[h] * scale - lses[h]) for h in H]
            if mask is not None:
                ps = [jnp.where(mask, p, 0.0) for p in ps]
            pbs = [p.astype(BF16) for p in ps]
            dss = [(ps[h] * (dps[h] - deltas[h]) * scale).astype(BF16) for h in H]
            dvs = [_dot(pbs[h], dobs[h], TN) for h in H]
            dks = [_dot(dss[h], qs[h], TN) for h in H]
            dqs = [_dot(dss[h], ks[h]) for h in H]
            for h in H:
                dv_ref[rows, vcs[h]] += dvs[h]
                dk_ref[rows, kcs[h]] += dks[h]
            return tuple(carry[h] + dqs[h] for h in H)

        carry = lax.fori_loop(0, i, lambda j, c: block(j, c, None), tuple(jnp.zeros((FB, DK), F32) for _ in range(heads)))
        carry = block(i, carry, diag)
        for h in range(heads):
            dq_ref[:, h * DK:(h + 1) * DK] = carry[h]

    qtile = pl.BlockSpec((FB, heads * DK), lambda g, i: (i, g))
    otile = pl.BlockSpec((FB, heads * DV), lambda g, i: (i, g))
    once = pl.Buffered(1)
    kfull = pl.BlockSpec((S, heads * DK), lambda g, i: (0, g), pipeline_mode=once)
    vfull = pl.BlockSpec((S, heads * DV), lambda g, i: (0, g), pipeline_mode=once)
    return _pallas(
        body, name=name, grid=(ngrp, nq),
        in_specs=[qtile, kfull, vfull, otile, pl.BlockSpec((heads, FB, 1), lambda g, i: (g, i, 0)), otile],
        out_specs=[qtile, kfull, vfull],
        out_shape=[jax.ShapeDtypeStruct((S, H * DK), F32), jax.ShapeDtypeStruct((S, H * DK), F32),
                   jax.ShapeDtypeStruct((S, H * DV), F32)],
        compiler_params=_params(("parallel", "arbitrary"), S * heads * (DK + DV) * 6 + 16 * FB * FB * 4),
    )(qn, kn, vb, o, lse, do)


def _rope_tables(S):
    half = 32
    inv_freq = ROPE_THETA ** (-jnp.arange(half, dtype=F32) / half)
    ang = jnp.arange(S).astype(F32)[:, None] * inv_freq[None, :]
    cos, sin = jnp.cos(ang), jnp.sin(ang)
    ones, zeros = jnp.ones((S, LANES), F32), jnp.zeros((S, LANES), F32)
    pad = jnp.zeros((S, 64), F32)
    return (jnp.concatenate([ones, cos, cos, pad + 1.0], axis=1),
            jnp.concatenate([zeros, -sin, sin, pad], axis=1))


def _rope_partner(u):
    lane = lax.broadcasted_iota(I32, u.shape, 1)
    return jnp.where((lane % HEAD64) < 32, pltpu.roll(u, LANES - 32, 1), pltpu.roll(u, 32, 1))


def _normrope(raw, g, cos, sgn):
    r = lax.rsqrt(jnp.sum(raw * raw, axis=1, keepdims=True) * (1.0 / MLA_QK) + NORM_EPS)
    y = raw * r
    u = y * g
    pe = u[:, LANES:]
    out = jnp.concatenate([u[:, :LANES], pe * cos[:, LANES:] + _rope_partner(pe) * sgn[:, LANES:]], axis=1)
    return out, y, r


def _normrope_bwd(dout, g, cos, sgn, y, r):
    dpe = dout[:, LANES:]
    du = jnp.concatenate([dout[:, :LANES], dpe * cos[:, LANES:] + _rope_partner(dpe * sgn[:, LANES:])], axis=1)
    dy = du * g
    draw = r * (dy - y * (jnp.sum(dy * y, axis=1, keepdims=True) * (1.0 / MLA_QK)))
    return draw, jnp.sum(du * y, axis=0, keepdims=True)


def mla_prep_fwd(qraw, kv, proj, gq, gk, cos, sgn, *, kpe_off, name, tm=1024):
    S = qraw.shape[0]
    tm = _tile(S, tm)
    kb = kpe_off // LANES

    def body(q_ref, kn_ref, v_ref, kpe_ref, gq_ref, gk_ref, c_ref, s_ref, qo_ref, ko_ref, vo_ref):
        cos, sgn = c_ref[...], s_ref[...]
        qo_ref[...] = _normrope(q_ref[...], gq_ref[...], cos, sgn)[0].astype(BF16)
        kraw = jnp.concatenate([kn_ref[...], kpe_ref[...]], axis=1)
        ko_ref[...] = _normrope(kraw, gk_ref[...], cos, sgn)[0].astype(BF16)
        vo_ref[...] = v_ref[...].astype(BF16)

    head = pl.BlockSpec((tm, MLA_PAD), lambda i, h: (i, h))
    gain = pl.BlockSpec((1, MLA_PAD), lambda i, h: (0, 0))
    tab = pl.BlockSpec((tm, MLA_PAD), lambda i, h: (i, 0))
    return _pallas(
        body, name=name, grid=(S // tm, MLA_HEADS),
        in_specs=[head, pl.BlockSpec((tm, LANES), lambda i, h: (i, 2 * h)), pl.BlockSpec((tm, LANES), lambda i, h: (i, 2 * h + 1)),
                  pl.BlockSpec((tm, LANES), lambda i, h: (i, kb)), gain, gain, tab, tab],
        out_specs=[head, head, pl.BlockSpec((tm, LANES), lambda i, h: (i, h))],
        out_shape=[jax.ShapeDtypeStruct(qraw.shape, BF16), jax.ShapeDtypeStruct(qraw.shape, BF16),
                   jax.ShapeDtypeStruct((S, MLA_HEADS * LANES), BF16)],
        compiler_params=_params(("parallel", "arbitrary"), 16 * tm * MLA_PAD * 4),
    )(qraw, kv, kv, proj, gq, gk, cos, sgn)


def mla_prep_bwd(dqn, dkn, dv, qraw, kv, proj, gq, gk, cos, sgn, *, kpe_off, name, tm=512):
    S = qraw.shape[0]
    tm = _tile(S, tm)
    kb = kpe_off // LANES

    def body(dq_ref, dk_ref, dv_ref, q_ref, kn_ref, kpe_ref, gq_ref, gk_ref, c_ref, s_ref,
             dqo_ref, dkv_ref, dkpe_ref, dgq_ref, dgk_ref, acc_ref):
        i, h = pl.program_id(0), pl.program_id(1)

        @pl.when((i == 0) & (h == 0))
        def _():
            dgq_ref[...] = jnp.zeros_like(dgq_ref)
            dgk_ref[...] = jnp.zeros_like(dgk_ref)

        @pl.when(h == 0)
        def _():
            acc_ref[...] = jnp.zeros_like(acc_ref)

        cos, sgn = c_ref[...], s_ref[...]
        _, yq, rq = _normrope(q_ref[...], gq_ref[...], cos, sgn)
        dq, dgq = _normrope_bwd(dq_ref[...], gq_ref[...], cos, sgn, yq, rq)
        dqo_ref[...] = dq.astype(BF16)
        dgq_ref[...] += dgq
        kraw = jnp.concatenate([kn_ref[...], kpe_ref[...]], axis=1)
        _, yk, rk = _normrope(kraw, gk_ref[...], cos, sgn)
        dk, dgk = _normrope_bwd(dk_ref[...], gk_ref[...], cos, sgn, yk, rk)
        dgk_ref[...] += dgk
        dkv_ref[...] = jnp.concatenate([dk[:, :LANES], dv_ref[...]], axis=1).astype(BF16)
        acc_ref[...] += dk[:, LANES:]

        @pl.when(h == MLA_HEADS - 1)
        def _():
            dkpe_ref[...] = acc_ref[...].astype(BF16)

    head = pl.BlockSpec((tm, MLA_PAD), lambda i, h: (i, h))
    gain = pl.BlockSpec((1, MLA_PAD), lambda i, h: (0, 0))
    tab = pl.BlockSpec((tm, MLA_PAD), lambda i, h: (i, 0))
    return _pallas(
        body, name=name, grid=(S // tm, MLA_HEADS),
        in_specs=[head, head, pl.BlockSpec((tm, LANES), lambda i, h: (i, h)), head,
                  pl.BlockSpec((tm, LANES), lambda i, h: (i, 2 * h)), pl.BlockSpec((tm, LANES), lambda i, h: (i, kb)),
                  gain, gain, tab, tab],
        out_specs=[head, head, pl.BlockSpec((tm, LANES), lambda i, h: (i, 0)), gain, gain],
        out_shape=[jax.ShapeDtypeStruct(qraw.shape, BF16), jax.ShapeDtypeStruct(qraw.shape, BF16),
                   jax.ShapeDtypeStruct((S, LANES), BF16), jax.ShapeDtypeStruct((1, MLA_PAD), F32),
                   jax.ShapeDtypeStruct((1, MLA_PAD), F32)],
        scratch_shapes=[pltpu.VMEM((tm, LANES), F32)],
        compiler_params=_params(("arbitrary", "arbitrary"), 24 * tm * MLA_PAD * 4),
    )(dqn, dkn, dv, qraw, kv, proj, gq, gk, cos, sgn)


def swa_kv_prep(proj, gk64, *, k_off, v_off, name, tm=512):
    S = proj.shape[0]
    tm = _tile(S, tm)
    W = SWA_KV * HEAD64

    def body(k_ref, v_ref, g_ref, ko_ref, vo_ref):
        first = _first_half((tm, LANES))

        def dup(n):
            nr = pltpu.roll(n, HEAD64, 1)
            return jnp.where(first, n, nr), jnp.where(first, nr, n)

        for t in range(W // LANES):
            x = k_ref[:, t * LANES:(t + 1) * LANES]
            n = x * lax.rsqrt(_segmean64(x * x) + NORM_EPS) * g_ref[...]
            d0, d1 = dup(n)
            ko_ref[:, 2 * t * LANES:(2 * t + 1) * LANES] = d0.astype(BF16)
            ko_ref[:, (2 * t + 1) * LANES:(2 * t + 2) * LANES] = d1.astype(BF16)
            d0, d1 = dup(v_ref[:, t * LANES:(t + 1) * LANES])
            vo_ref[:, 2 * t * LANES:(2 * t + 1) * LANES] = d0.astype(BF16)
            vo_ref[:, (2 * t + 1) * LANES:(2 * t + 2) * LANES] = d1.astype(BF16)

    out = pl.BlockSpec((tm, SWA_KV * LANES), lambda i: (i, 0))
    shape = jax.ShapeDtypeStruct((S, SWA_KV * LANES), BF16)
    return _pallas(
        body, name=name, grid=(S // tm,),
        in_specs=[pl.BlockSpec((tm, W), lambda i: (i, k_off // W)), pl.BlockSpec((tm, W), lambda i: (i, v_off // W)),
                  pl.BlockSpec((1, LANES), lambda i: (0, 0))],
        out_specs=[out, out], out_shape=[shape, shape],
        compiler_params=_params(("parallel",), 12 * tm * W * 4),
    )(proj, proj, jnp.tile(gk64, 2).reshape(1, LANES))


def swa_kv_prep_bwd(dkdup, dvdup, proj, gk64, *, k_off, name, tm=512):
    S = proj.shape[0]
    tm = _tile(S, tm)
    W = SWA_KV * HEAD64

    def body(dk_ref, dv_ref, k_ref, g_ref, dko_ref, dvo_ref, dg_ref):
        @pl.when(pl.program_id(0) == 0)
        def _():
            dg_ref[...] = jnp.zeros_like(dg_ref)

        first = _first_half((tm, LANES))

        def fold(ref, t):
            d0 = ref[:, 2 * t * LANES:(2 * t + 1) * LANES]
            d1 = ref[:, (2 * t + 1) * LANES:(2 * t + 2) * LANES]
            return jnp.where(first, d0 + pltpu.roll(d0, HEAD64, 1), d1 + pltpu.roll(d1, HEAD64, 1))

        for t in range(W // LANES):
            dvo_ref[:, t * LANES:(t + 1) * LANES] = fold(dv_ref, t).astype(BF16)
            dh = fold(dk_ref, t)
            x = k_ref[:, t * LANES:(t + 1) * LANES]
            r = lax.rsqrt(_segmean64(x * x) + NORM_EPS)
            y = x * r
            dy = dh * g_ref[...]
            dko_ref[:, t * LANES:(t + 1) * LANES] = (r * (dy - y * _segmean64(dy * y))).astype(BF16)
            dg_ref[...] += jnp.sum(dh * y, axis=0, keepdims=True)

    dup = pl.BlockSpec((tm, SWA_KV * LANES), lambda i: (i, 0))
    out = pl.BlockSpec((tm, W), lambda i: (i, 0))
    shape = jax.ShapeDtypeStruct((S, W), BF16)
    return _pallas(
        body, name=name, grid=(S // tm,),
        in_specs=[dup, dup, pl.BlockSpec((tm, W), lambda i: (i, k_off // W)), pl.BlockSpec((1, LANES), lambda i: (0, 0))],
        out_specs=[out, out, pl.BlockSpec((1, LANES), lambda i: (0, 0))],
        out_shape=[shape, shape, jax.ShapeDtypeStruct((1, LANES), F32)],
        compiler_params=_params(("arbitrary",), 16 * tm * W * 4),
    )(dkdup, dvdup, proj, jnp.tile(gk64, 2).reshape(1, LANES))


def _swa_geometry(i):
    r = lax.broadcasted_iota(I32, (BLK, 2 * BLK), 0)
    c = lax.broadcasted_iota(I32, (BLK, 2 * BLK), 1)
    rel = r + BLK - c
    valid = (rel >= 0) & (rel < BLK) & ((c >= BLK) | (i > 0))
    return valid, rel.astype(F32)


def _swa_slope(h):
    return 2.0 ** (-8.0 * (h + 1) / SWA_HEADS)


def swa_attn_fwd(qn, kdup, vdup, sinks, *, name):
    S = qn.shape[0]
    nq = S // BLK
    group = SWA_HEADS // SWA_KV

    def body(q_ref, kp_ref, kc_ref, vp_ref, vc_ref, sink_ref, o_ref, lse_ref):
        i = pl.program_id(0)
        valid, rel = _swa_geometry(i)
        first = _first_half((BLK, LANES))
        lane = lax.broadcasted_iota(I32, (BLK, LANES), 1)
        lse_all = jnp.zeros((BLK, LANES), F32)
        zero = jnp.zeros((BLK, LANES), BF16)
        H = range(SWA_HEADS)
        kks = [jnp.concatenate([kp_ref[:, g * LANES:(g + 1) * LANES], kc_ref[:, g * LANES:(g + 1) * LANES]], axis=0)
               for g in range(SWA_KV)]
        vvs = [jnp.concatenate([vp_ref[:, g * LANES:(g + 1) * LANES], vc_ref[:, g * LANES:(g + 1) * LANES]], axis=0)
               for g in range(SWA_KV)]
        q2s = [q_ref[:, t * LANES:(t + 1) * LANES] for t in range(SWA_HEADS // 2)]
        qhs = [jnp.where(first if h % 2 == 0 else ~first, q2s[h // 2], zero) for h in H]
        ss = [_dot(qhs[h], kks[h // group], NT) * (1.0 / math.sqrt(HEAD64)) - _swa_slope(h) * rel for h in H]
        ss = [jnp.where(valid, s, -1e30) for s in ss]
        ms = [jnp.maximum(jnp.max(ss[h], axis=1, keepdims=True), sink_ref[h]) for h in H]
        es = [jnp.exp(ss[h] - ms[h]) for h in H]
        dens = [jnp.sum(es[h], axis=1, keepdims=True) + jnp.exp(sink_ref[h] - ms[h]) for h in H]
        outs = [_dot((es[h] / dens[h]).astype(BF16), vvs[h // group]) for h in H]
        for h in H:
            lse_all = jnp.where(lane == h, ms[h] + jnp.log(dens[h]), lse_all)
        for t in range(SWA_HEADS // 2):
            o_ref[:, t * LANES:(t + 1) * LANES] = jnp.where(first, outs[2 * t], outs[2 * t + 1])
        lse_ref[...] = lse_all

    prev = lambda i: (jnp.maximum(i - 1, 0), 0)
    cur = lambda i: (i, 0)
    kvw = SWA_KV * LANES
    return _pallas(
        body, name=name, grid=(nq,),
        in_specs=[pl.BlockSpec((BLK, 1024), cur), pl.BlockSpec((BLK, kvw), prev), pl.BlockSpec((BLK, kvw), cur),
                  pl.BlockSpec((BLK, kvw), prev), pl.BlockSpec((BLK, kvw), cur),
                  pl.BlockSpec(memory_space=pltpu.SMEM)],
        out_specs=[pl.BlockSpec((BLK, 1024), cur), pl.BlockSpec((BLK, LANES), cur)],
        out_shape=[jax.ShapeDtypeStruct((S, 1024), F32), jax.ShapeDtypeStruct((S, LANES), F32)],
        compiler_params=_params(("parallel",), 16 * BLK * 1024 * 4),
    )(qn, kdup, kdup, vdup, vdup, sinks)


def swa_attn_bwd(qn, kdup, vdup, sinks, o, lse, do, *, name):
    S = qn.shape[0]
    nq = S // BLK
    group = SWA_HEADS // SWA_KV
    scale = 1.0 / math.sqrt(HEAD64)

    def body(q_ref, kp_ref, kc_ref, vp_ref, vc_ref, sink_ref, o_ref, lse_ref, do_ref,
             dq_ref, dk_ref, dv_ref, ds_ref):
        i = pl.program_id(0)

        @pl.when(i == 0)
        def _():
            dk_ref[...] = jnp.zeros_like(dk_ref)
            dv_ref[...] = jnp.zeros_like(dv_ref)
            ds_ref[...] = jnp.zeros_like(ds_ref)

        valid, rel = _swa_geometry(i)
        first = _first_half((BLK, LANES))
        lane1 = lax.broadcasted_iota(I32, (1, LANES), 1)
        lane = lax.broadcasted_iota(I32, (BLK, LANES), 1)
        lse_all = lse_ref[...]
        prow = pl.ds(pl.multiple_of(jnp.maximum(i - 1, 0) * BLK, BLK), BLK)
        crow = pl.ds(pl.multiple_of(i * BLK, BLK), BLK)
        dsink = jnp.zeros((1, LANES), F32)
        zero = jnp.zeros((BLK, LANES), BF16)
        H, T = range(SWA_HEADS), range(SWA_HEADS // 2)
        kks = [jnp.concatenate([kp_ref[:, g * LANES:(g + 1) * LANES], kc_ref[:, g * LANES:(g + 1) * LANES]], axis=0)
               for g in range(SWA_KV)]
        vvs = [jnp.concatenate([vp_ref[:, g * LANES:(g + 1) * LANES], vc_ref[:, g * LANES:(g + 1) * LANES]], axis=0)
               for g in range(SWA_KV)]
        q2s = [q_ref[:, t * LANES:(t + 1) * LANES] for t in T]
        do2s = [do_ref[:, t * LANES:(t + 1) * LANES] for t in T]
        prods = [do2s[t] * o_ref[:, t * LANES:(t + 1) * LANES] for t in T]
        mine = [first if h % 2 == 0 else ~first for h in H]
        qhs = [jnp.where(mine[h], q2s[h // 2], zero) for h in H]
        dohs = [jnp.where(mine[h], do2s[h // 2], 0.0).astype(BF16) for h in H]
        deltas = [jnp.sum(jnp.where(mine[h], prods[h // 2], 0.0), axis=1, keepdims=True) for h in H]
        lses = [jnp.sum(jnp.where(lane == h, lse_all, 0.0), axis=1, keepdims=True) for h in H]
        ss = [_dot(qhs[h], kks[h // group], NT) * scale - _swa_slope(h) * rel for h in H]
        dps = [_dot(dohs[h], vvs[h // group], NT) for h in H]
        ps = [jnp.where(valid, jnp.exp(ss[h] - lses[h]), 0.0) for h in H]
        dscs = [(ps[h] * (dps[h] - deltas[h]) * scale).astype(BF16) for h in H]
        dqs = [_dot(dscs[h], kks[h // group]) for h in H]
        dks = [_dot(dscs[h], qhs[h], TN) for h in H]
        dvs = [_dot(ps[h].astype(BF16), dohs[h], TN) for h in H]
        for h in H:
            psink = jnp.exp(sink_ref[h] - lses[h])
            dsink = dsink + jnp.where(lane1 == h, -jnp.sum(psink * deltas[h]), 0.0)
        for t in T:
            dq_ref[:, t * LANES:(t + 1) * LANES] = jnp.where(first, dqs[2 * t], dqs[2 * t + 1])
        for g in range(SWA_KV):
            cols = slice(g * LANES, (g + 1) * LANES)
            b = g * group
            dkk = (dks[b] + dks[b + 1]) + (dks[b + 2] + dks[b + 3])
            dvv = (dvs[b] + dvs[b + 1]) + (dvs[b + 2] + dvs[b + 3])
            dk_ref[prow, cols] += dkk[:BLK]
            dv_ref[prow, cols] += dvv[:BLK]
            dk_ref[crow, cols] += dkk[BLK:]
            dv_ref[crow, cols] += dvv[BLK:]
        ds_ref[...] += dsink

    prev = lambda i: (jnp.maximum(i - 1, 0), 0)
    cur = lambda i: (i, 0)
    kvw = SWA_KV * LANES
    whole = pl.BlockSpec((S, kvw), lambda i: (0, 0))
    return _pallas(
        body, name=name, grid=(nq,),
        in_specs=[pl.BlockSpec((BLK, 1024), cur), pl.BlockSpec((BLK, kvw), prev), pl.BlockSpec((BLK, kvw), cur),
                  pl.BlockSpec((BLK, kvw), prev), pl.BlockSpec((BLK, kvw), cur),
                  pl.BlockSpec(memory_space=pltpu.SMEM),
                  pl.BlockSpec((BLK, 1024), cur), pl.BlockSpec((BLK, LANES), cur), pl.BlockSpec((BLK, 1024), cur)],
        out_specs=[pl.BlockSpec((BLK, 1024), cur), whole, whole, pl.BlockSpec((1, LANES), lambda i: (0, 0))],
        out_shape=[jax.ShapeDtypeStruct((S, 1024), F32), jax.ShapeDtypeStruct((S, kvw), F32),
                   jax.ShapeDtypeStruct((S, kvw), F32), jax.ShapeDtypeStruct((1, LANES), F32)],
        compiler_params=_params(("arbitrary",), 4 * S * kvw * 4 + 24 * BLK * 1024 * 4),
    )(qn, kdup, kdup, vdup, vdup, sinks, o, lse, do)


def _in_bwd(x, h, dproj, dy, g, w_in, tag):
    dh = matmul(dproj, w_in, "nt", name=f"{tag}_dh")
    dw_in = matmul(h, dproj, "tn", name=f"{tag}_dwin")
    dx, dg = rmsnorm_bwd(x, g, dh, col_off=0, width=D_MODEL, out_dtype=F32, res=dy, name=f"{tag}_dnorm")
    return dx, dw_in, dg


def _out_bwd(dy, og, o, proj, w_out, gate_off, tag):
    dog = matmul(dy, w_out, "nt", name=f"{tag}_dog")
    dw_out = matmul(og, dy, "tn", name=f"{tag}_dwout")
    do, dgate = gate_bwd(dog, o, proj, gate_off=gate_off, name=f"{tag}_dgate")
    return do, dgate, dw_out


def sb_fwd(x, p, tag):
    h = rmsnorm_fwd(x, p["norm"], col_off=0, width=D_MODEL, out_dtype=BF16, name=f"{tag}_norm")
    qkv = matmul(h, p["w_in"], "nn", out_dtype=BF16, b_cols=(0, 3 * SB_W), name=f"{tag}_proj")
    gate = matmul(h, p["w_in"], "nn", b_cols=(3 * SB_W, SB_W), name=f"{tag}_gproj")
    o = sb_attn_fwd(qkv, name=f"{tag}_attn")
    og = gate_fwd(o, gate, gate_off=0, name=f"{tag}_gate")
    y = matmul(og, p["w_out"], "nn", add=x, name=f"{tag}_out")
    return y, (x, h, gate, qkv, o, og)


def sb_bwd(dy, saved, p, tag):
    x, h, gate, qkv, o, og = saved
    do, dgate, dw_out = _out_bwd(dy, og, o, gate, p["w_out"], 0, tag)
    dq, dk, dv = sb_attn_bwd(qkv, o, do, name=f"{tag}_dattn")
    dproj = jnp.concatenate([dq, dk.astype(BF16), dv.astype(BF16), dgate], axis=1)
    dx, dw_in, dg = _in_bwd(x, h, dproj, dy, p["norm"], p["w_in"], tag)
    return dx, {"norm": dg[0], "w_in": dw_in, "w_out": dw_out}


MLA_GATE, MLA_QLAT, MLA_KVLAT, MLA_KPE, MLA_IN = 0, 1024, 1280, 1408, 1536


def mla_fwd(x, p, tabs, tag):
    cos, sgn = tabs
    h = rmsnorm_fwd(x, p["norm"], col_off=0, width=D_MODEL, out_dtype=BF16, name=f"{tag}_norm")
    proj = matmul(h, p["w_in"], "nn", name=f"{tag}_proj")
    ql = rmsnorm_fwd(proj, p["q_a_norm"], col_off=MLA_QLAT, width=256, out_dtype=BF16, name=f"{tag}_qanorm")
    kvl = rmsnorm_fwd(proj, p["kv_a_norm"], col_off=MLA_KVLAT, width=128, out_dtype=BF16, name=f"{tag}_kvanorm")
    qraw = matmul(ql, p["w_uq"], "nn", name=f"{tag}_uq")
    kv = matmul(kvl, p["w_ukv"], "nn", name=f"{tag}_ukv")
    qn, kn, vb = mla_prep_fwd(qraw, kv, proj, p["gq"], p["gk"], cos, sgn, kpe_off=MLA_KPE, name=f"{tag}_prep")
    o, lse = flash_fwd(qn, kn, vb, name=f"{tag}_attn")
    og = gate_fwd(o, proj, gate_off=MLA_GATE, name=f"{tag}_gate")
    y = matmul(og, p["w_out"], "nn", add=x, name=f"{tag}_out")
    return y, (x, h, proj, ql, kvl, qraw, kv, qn, kn, vb, o, lse, og)


def mla_bwd(dy, saved, p, tabs, tag):
    cos, sgn = tabs
    x, h, proj, ql, kvl, qraw, kv, qn, kn, vb, o, lse, og = saved
    do, dgate, dw_out = _out_bwd(dy, og, o, proj, p["w_out"], MLA_GATE, tag)
    dqn, dkn, dv = flash_bwd(qn, kn, vb, o, lse, do, name=f"{tag}_dattn")
    dqraw, dkv, dkpe, dgq, dgk = mla_prep_bwd(dqn, dkn, dv, qraw, kv, proj, p["gq"], p["gk"], cos, sgn,
                                              kpe_off=MLA_KPE, name=f"{tag}_dprep")
    dql = matmul(dqraw, p["w_uq"], "nt", name=f"{tag}_dql")
    dw_uq = matmul(ql, dqraw, "tn", name=f"{tag}_dwuq")
    dkvl = matmul(dkv, p["w_ukv"], "nt", name=f"{tag}_dkvl")
    dw_ukv = matmul(kvl, dkv, "tn", name=f"{tag}_dwukv")
    dqlat, dgqa = rmsnorm_bwd(proj, p["q_a_norm"], dql, col_off=MLA_QLAT, width=256, out_dtype=BF16, name=f"{tag}_dqanorm")
    dkvlat, dgkva = rmsnorm_bwd(proj, p["kv_a_norm"], dkvl, col_off=MLA_KVLAT, width=128, out_dtype=BF16,
                                name=f"{tag}_dkvanorm")
    dproj = jnp.concatenate([dgate, dqlat, dkvlat, dkpe], axis=1)
    dx, dw_in, dg = _in_bwd(x, h, dproj, dy, p["norm"], p["w_in"], tag)
    return dx, {"norm": dg[0], "w_in": dw_in, "q_a_norm": dgqa[0], "w_uq": dw_uq, "kv_a_norm": dgkva[0],
                "w_ukv": dw_ukv, "gq": dgq[0], "gk": dgk[0], "w_out": dw_out}


SWA_Q, SWA_GATE, SWA_K, SWA_V = 0, 1024, 2048, 2304


def swa_fwd(x, p, tag):
    h = rmsnorm_fwd(x, p["norm"], col_off=0, width=D_MODEL, out_dtype=BF16, name=f"{tag}_norm")
    proj = matmul(h, p["w_in"], "nn", name=f"{tag}_proj")
    qn = headnorm_fwd(proj, p["q_head_norm"], col_off=SWA_Q, width=1024, name=f"{tag}_qnorm")
    kdup, vdup = swa_kv_prep(proj, p["k_head_norm"], k_off=SWA_K, v_off=SWA_V, name=f"{tag}_kvprep")
    o, lse = swa_attn_fwd(qn, kdup, vdup, p["sinks"], name=f"{tag}_attn")
    og = gate_fwd(o, proj, gate_off=SWA_GATE, name=f"{tag}_gate")
    y = matmul(og, p["w_out"], "nn", add=x, name=f"{tag}_out")
    return y, (x, h, proj, qn, kdup, vdup, o, lse, og)


def swa_bwd(dy, saved, p, tag):
    x, h, proj, qn, kdup, vdup, o, lse, og = saved
    do, dgate, dw_out = _out_bwd(dy, og, o, proj, p["w_out"], SWA_GATE, tag)
    dqn, dkdup, dvdup, dsinks = swa_attn_bwd(qn, kdup, vdup, p["sinks"], o, lse, do, name=f"{tag}_dattn")
    dq, dgq = headnorm_bwd(proj, p["q_head_norm"], dqn, col_off=SWA_Q, width=1024, name=f"{tag}_dqnorm")
    dk, dv, dgk = swa_kv_prep_bwd(dkdup, dvdup, proj, p["k_head_norm"], k_off=SWA_K, name=f"{tag}_dkvprep")
    dproj = jnp.concatenate([dq, dgate, dk, dv], axis=1)
    dx, dw_in, dg = _in_bwd(x, h, dproj, dy, p["norm"], p["w_in"], tag)
    return dx, {"norm": dg[0], "w_in": dw_in, "q_head_norm": dgq[0, :HEAD64] + dgq[0, HEAD64:],
                "k_head_norm": dgk[0, :HEAD64] + dgk[0, HEAD64:], "sinks": dsinks[0, :SWA_HEADS], "w_out": dw_out}


def prepare_weights(w):
    l1_in, l2_in = w["l1_w_in"], w["l2_w_in"]
    pad64 = lambda v: jnp.pad(v, (0, MLA_PAD - MLA_QK)).reshape(1, MLA_PAD)
    return [
        {"norm": w["l0_norm"], "w_in": w["l0_w_in"], "w_out": w["l0_w_out"]},
        {"norm": w["l1_norm"],
         "w_in": jnp.concatenate([l1_in[:, 448:], l1_in[:, :448], jnp.zeros((D_MODEL, 64), l1_in.dtype)], axis=1),
         "q_a_norm": w["l1_q_a_norm"], "kv_a_norm": w["l1_kv_a_norm"],
         "w_uq": jnp.pad(w["l1_w_uq"].reshape(256, MLA_HEADS, MLA_QK), ((0, 0), (0, 0), (0, MLA_PAD - MLA_QK))
                         ).reshape(256, MLA_HEADS * MLA_PAD),
         "w_ukv": w["l1_w_ukv"], "gq": pad64(w["l1_q_head_norm"]), "gk": pad64(w["l1_k_head_norm"]),
         "w_out": w["l1_w_out"]},
        {"norm": w["l2_norm"],
         "w_in": jnp.concatenate([l2_in[:, :1024], l2_in[:, 1536:], l2_in[:, 1024:1536]], axis=1),
         "q_head_norm": w["l2_q_head_norm"], "k_head_norm": w["l2_k_head_norm"], "sinks": w["l2_sinks"],
         "w_out": w["l2_w_out"]},
        {"norm": w["l3_norm"], "w_in": w["l3_w_in"], "w_out": w["l3_w_out"]},
    ]


def unprepare_grads(gs):
    g0, g1, g2, g3 = gs
    d1, d2 = g1["w_in"], g2["w_in"]
    return {
        "l0_norm": g0["norm"], "l0_w_in": g0["w_in"], "l0_w_out": g0["w_out"],
        "l1_norm": g1["norm"], "l1_w_in": jnp.concatenate([d1[:, 1024:1472], d1[:, :1024]], axis=1),
        "l1_q_a_norm": g1["q_a_norm"],
        "l1_w_uq": g1["w_uq"].reshape(256, MLA_HEADS, MLA_PAD)[:, :, :MLA_QK].reshape(256, MLA_HEADS * MLA_QK),
        "l1_kv_a_norm": g1["kv_a_norm"], "l1_w_ukv": g1["w_ukv"],
        "l1_q_head_norm": g1["gq"][:MLA_QK], "l1_k_head_norm": g1["gk"][:MLA_QK], "l1_w_out": g1["w_out"],
        "l2_norm": g2["norm"], "l2_w_in": jnp.concatenate([d2[:, :1024], d2[:, 2048:], d2[:, 1024:2048]], axis=1),
        "l2_q_head_norm": g2["q_head_norm"], "l2_k_head_norm": g2["k_head_norm"], "l2_sinks": g2["sinks"],
        "l2_w_out": g2["w_out"],
        "l3_norm": g3["norm"], "l3_w_in": g3["w_in"], "l3_w_out": g3["w_out"],
    }


def local_step(x, target, w):
    ps = prepare_weights(w)
    tabs = _rope_tables(x.shape[0])
    y0, s0 = sb_fwd(x, ps[0], "l0")
    y1, s1 = mla_fwd(y0, ps[1], tabs, "l1")
    y2, s2 = swa_fwd(y1, ps[2], "l2")
    y3, s3 = sb_fwd(y2, ps[3], "l3")
    dy, loss = loss_head(y3, target, name="loss")
    d3, g3 = sb_bwd(dy, s3, ps[3], "l3")
    d2, g2 = swa_bwd(d3, s2, ps[2], "l2")
    d1, g1 = mla_bwd(d2, s1, ps[1], tabs, "l1")
    d0, g0 = sb_bwd(d1, s0, ps[0], "l0")
    return loss, d0, unprepare_grads([g0, g1, g2, g3])


MATS = (("l0_w_in", "col", 1024, 4096), ("l0_w_out", "row", 1024, 1024), ("l1_w_in", "col", 1024, 1472),
        ("l1_w_uq", "col", 256, 1536), ("l1_w_ukv", "col", 128, 2048), ("l1_w_out", "row", 1024, 1024),
        ("l2_w_in", "col", 1024, 2560), ("l2_w_out", "row", 1024, 1024), ("l3_w_in", "col", 1024, 4096),
        ("l3_w_out", "row", 1024, 1024))
N_CHIPS = 4
PACK_W = 1024
HALF_ROWS = 2176
PACK_ROWS = 2 * HALF_ROWS
VECS = (("l0_norm", 0, 0, 1024), ("l1_norm", 1, 0, 1024), ("l2_norm", 2, 0, 1024), ("l3_norm", 3, 0, 1024),
        ("l1_q_a_norm", 4, 0, 256), ("l1_kv_a_norm", 4, 256, 128), ("l1_q_head_norm", 4, 384, 192),
        ("l1_k_head_norm", 4, 576, 192), ("l2_q_head_norm", 4, 768, 64), ("l2_k_head_norm", 4, 832, 64),
        ("l2_sinks", 4, 896, 16))
LOSS_SLOT = (4, 912)
VEC_ROWS = 8


def _shard_rows(k, n):
    return k * n // N_CHIPS // PACK_W


def pack_shards(shards):
    parts = [shards[name].reshape(-1, PACK_W) for name, _, _, _ in MATS]
    used = sum(p.shape[0] for p in parts)
    return jnp.concatenate(parts + [jnp.zeros((PACK_ROWS - used, PACK_W), parts[0].dtype)], axis=0)


def unpack_shards(flat):
    out, r0 = {}, 0
    for name, kind, k, n in MATS:
        rows = _shard_rows(k, n)
        shape = (k, n // N_CHIPS) if kind == "col" else (k // N_CHIPS, n)
        out[name] = flat[r0:r0 + rows].reshape(shape)
        r0 += rows
    return out


def pack_full(mats):
    parts = []
    for name, kind, k, n in MATS:
        m = mats[name]
        if kind == "col":
            m = m.reshape(k, N_CHIPS, n // N_CHIPS).transpose(1, 0, 2)
        parts.append(m.reshape(N_CHIPS, -1, PACK_W))
    used = sum(p.shape[1] for p in parts)
    return jnp.concatenate(parts + [jnp.zeros((N_CHIPS, PACK_ROWS - used, PACK_W), parts[0].dtype)], axis=1)


def unpack_full(stacked):
    out, r0 = {}, 0
    for name, kind, k, n in MATS:
        rows = _shard_rows(k, n)
        seg = stacked[:, r0:r0 + rows]
        if kind == "col":
            out[name] = seg.reshape(N_CHIPS, k, n // N_CHIPS).transpose(1, 0, 2).reshape(k, n)
        else:
            out[name] = seg.reshape(k, n)
        r0 += rows
    return out


def pack_vecs(vecs, loss=None):
    rows = []
    for r in range(VEC_ROWS):
        items = [(off, vecs[name]) for name, rr, off, _ in VECS if rr == r]
        if loss is not None and r == LOSS_SLOT[0]:
            items.append((LOSS_SLOT[1], loss.reshape(1)))
        pos, parts = 0, []
        for off, v in sorted(items, key=lambda t: t[0]):
            assert off == pos
            parts.append(v.astype(F32))
            pos += v.shape[0]
        parts.append(jnp.zeros((PACK_W - pos,), F32))
        rows.append(jnp.concatenate(parts))
    return jnp.stack(rows)


def unpack_vecs(block):
    return {name: block[r, off:off + n] for name, r, off, n in VECS}


def _me():
    return lax.axis_index("x"), lax.axis_index("y"), lax.axis_index("c")


OTHER_CHIPS = ((1, 0), (0, 1), (1, 1))


def _remote(src, dst, send_sem, recv_sem, to):
    return pltpu.make_async_remote_copy(src_ref=src, dst_ref=dst, send_sem=send_sem, recv_sem=recv_sem,
                                        device_id=to, device_id_type=MESH)


def gather_weights(block):
    def body(in_ref, out_ref, send_sems, recv_sems):
        x, y, c = _me()
        sibling = (x, y, 1 - c)

        def half(px, py, pc):
            return out_ref.at[2 * px + py, pl.ds(pc * HALF_ROWS, HALF_ROWS), :]

        chips = [(x ^ dx, y ^ dy) for dx, dy in OTHER_CHIPS]
        first = [_remote(in_ref.at[pl.ds(c * HALF_ROWS, HALF_ROWS), :], half(x, y, c), send_sems.at[j], recv_sems.at[j],
                         (*chip, c)) for j, chip in enumerate(chips)]
        for cp in first:
            cp.start()
        passed = [_remote(half(*chip, c), half(*chip, c), send_sems.at[3 + j], recv_sems.at[3 + j], sibling)
                  for j, chip in enumerate(chips)]
        for j, chip in enumerate(chips):
            _remote(half(*chip, c), half(*chip, c), send_sems.at[j], recv_sems.at[j], (*chip, c)).wait_recv()
            passed[j].start()
        for j, chip in enumerate(chips):
            _remote(half(*chip, 1 - c), half(*chip, 1 - c), send_sems.at[3 + j], recv_sems.at[3 + j], sibling).wait_recv()
        for cp in first + passed:
            cp.wait_send()

    hbm = pl.BlockSpec(memory_space=pltpu.HBM)
    others = pl.pallas_call(
        body, name="gather_weights",
        out_shape=jax.ShapeDtypeStruct((N_CHIPS, PACK_ROWS, PACK_W), block.dtype),
        in_specs=[hbm], out_specs=hbm,
        scratch_shapes=[pltpu.SemaphoreType.DMA((6,)), pltpu.SemaphoreType.DMA((6,))],
    )(block)
    return lax.dynamic_update_slice(others, block[None], (2 * lax.axis_index("x") + lax.axis_index("y"), 0, 0))


def pair_exchange(g):
    def body(g_ref, out_ref, send_sems, recv_sems):
        x, y, c = _me()
        sibling = (x, y, 1 - c)
        copies = [_remote(g_ref.at[k, pl.ds((1 - c) * HALF_ROWS, HALF_ROWS), :], out_ref.at[k], send_sems.at[k],
                          recv_sems.at[k], sibling) for k in range(N_CHIPS)]
        for cp in copies:
            cp.start()
        for cp in copies:
            cp.wait_recv()
        for cp in copies:
            cp.wait_send()

    hbm = pl.BlockSpec(memory_space=pltpu.HBM)
    return pl.pallas_call(
        body, name="pair_exchange",
        out_shape=jax.ShapeDtypeStruct((N_CHIPS, HALF_ROWS, PACK_W), g.dtype),
        in_specs=[hbm], out_specs=hbm,
        scratch_shapes=[pltpu.SemaphoreType.DMA((N_CHIPS,)), pltpu.SemaphoreType.DMA((N_CHIPS,))],
    )(g)


def chip_exchange(part):
    def body(p_ref, out_ref, send_sems, recv_sems):
        x, y, c = _me()
        copies = [_remote(p_ref.at[2 * (x ^ dx) + (y ^ dy)], out_ref.at[j], send_sems.at[j], recv_sems.at[j],
                          (x ^ dx, y ^ dy, c)) for j, (dx, dy) in enumerate(OTHER_CHIPS)]
        for cp in copies:
            cp.start()
        for cp in copies:
            cp.wait_recv()
        for cp in copies:
            cp.wait_send()

    hbm = pl.BlockSpec(memory_space=pltpu.HBM)
    return pl.pallas_call(
        body, name="chip_exchange",
        out_shape=jax.ShapeDtypeStruct((len(OTHER_CHIPS), HALF_ROWS, PACK_W), part.dtype),
        in_specs=[hbm], out_specs=hbm,
        scratch_shapes=[pltpu.SemaphoreType.DMA((3,)), pltpu.SemaphoreType.DMA((3,))],
    )(part)


def join_halves(half):
    def body(h_ref, out_ref, send_sem, recv_sem):
        x, y, c = _me()
        cp = _remote(h_ref, out_ref.at[pl.ds(c * HALF_ROWS, HALF_ROWS), :], send_sem, recv_sem, (x, y, 1 - c))
        cp.start()
        _remote(h_ref, out_ref.at[pl.ds((1 - c) * HALF_ROWS, HALF_ROWS), :], send_sem, recv_sem, (x, y, 1 - c)).wait_recv()
        cp.wait_send()

    hbm = pl.BlockSpec(memory_space=pltpu.HBM)
    other = pl.pallas_call(
        body, name="join_halves",
        out_shape=jax.ShapeDtypeStruct((PACK_ROWS, PACK_W), half.dtype),
        in_specs=[hbm], out_specs=hbm,
        scratch_shapes=[pltpu.SemaphoreType.DMA, pltpu.SemaphoreType.DMA],
    )(half)
    return lax.dynamic_update_slice(other, half, (lax.axis_index("c") * HALF_ROWS, 0))


def sum_over_devices(block):
    def body(in_ref, out_ref, all_ref, send_sems, recv_sems):
        x, y, c = _me()
        me = 4 * x + 2 * y + c
        all_ref[me] = in_ref[...]
        copies = []
        for r in range(1, 8):
            to = (x ^ (r >> 2), y ^ ((r >> 1) & 1), c ^ (r & 1))
            copies.append(_remote(in_ref, all_ref.at[me], send_sems.at[r - 1], recv_sems.at[r - 1], to))
        for cp in copies:
            cp.start()
        for r in range(1, 8):
            frm = (x ^ (r >> 2), y ^ ((r >> 1) & 1), c ^ (r & 1))
            _remote(in_ref, all_ref.at[4 * frm[0] + 2 * frm[1] + frm[2]], send_sems.at[r - 1], recv_sems.at[r - 1],
                    frm).wait_recv()
        for cp in copies:
            cp.wait_send()
        acc = all_ref[0]
        for d in range(1, 8):
            acc = acc + all_ref[d]
        out_ref[...] = acc

    vmem = pl.BlockSpec(memory_space=pltpu.VMEM)
    return pl.pallas_call(
        body, name="sum_over_devices",
        out_shape=jax.ShapeDtypeStruct(block.shape, F32),
        in_specs=[vmem], out_specs=vmem,
        scratch_shapes=[pltpu.VMEM((8,) + block.shape, F32), pltpu.SemaphoreType.DMA((7,)), pltpu.SemaphoreType.DMA((7,))],
    )(block)


SUM_ROWS = 272


def pair_sum(g, got, core):
    steps = HALF_ROWS // SUM_ROWS

    def body(s_ref, g_ref, r_ref, o_ref, ob_ref):
        t = g_ref[...] + r_ref[...]
        o_ref[...] = t
        ob_ref[...] = t.astype(BF16)

    blk = pl.BlockSpec((1, SUM_ROWS, PACK_W), lambda k, i, s: (k, i, 0))
    return pl.pallas_call(
        body, name="pair_sum",
        grid_spec=pltpu.PrefetchScalarGridSpec(
            num_scalar_prefetch=1, grid=(N_CHIPS, steps),
            in_specs=[pl.BlockSpec((1, SUM_ROWS, PACK_W), lambda k, i, s: (k, s[0] * steps + i, 0)), blk],
            out_specs=[blk, blk]),
        out_shape=[jax.ShapeDtypeStruct((N_CHIPS, HALF_ROWS, PACK_W), F32),
                   jax.ShapeDtypeStruct((N_CHIPS, HALF_ROWS, PACK_W), BF16)],
        compiler_params=_params(("parallel", "parallel"), 8 * SUM_ROWS * PACK_W * 4),
    )(core, g, got)


def chip_sum(part, got, chip):
    steps = HALF_ROWS // SUM_ROWS

    def body(s_ref, p_ref, a_ref, b_ref, c_ref, o_ref):
        o_ref[...] = ((p_ref[0] + a_ref[0].astype(F32)) + b_ref[0].astype(F32)) + c_ref[0].astype(F32)

    def got_spec(j):
        return pl.BlockSpec((1, SUM_ROWS, PACK_W), lambda i, s: (j, i, 0))

    return pl.pallas_call(
        body, name="chip_sum",
        grid_spec=pltpu.PrefetchScalarGridSpec(
            num_scalar_prefetch=1, grid=(steps,),
            in_specs=[pl.BlockSpec((1, SUM_ROWS, PACK_W), lambda i, s: (s[0], i, 0)), got_spec(0), got_spec(1), got_spec(2)],
            out_specs=pl.BlockSpec((SUM_ROWS, PACK_W), lambda i, s: (i, 0))),
        out_shape=jax.ShapeDtypeStruct((HALF_ROWS, PACK_W), F32),
        compiler_params=_params(("parallel",), 10 * SUM_ROWS * PACK_W * 4),
    )(chip, part, got, got, got)


def reduce_scatter(g):
    x, y, c = _me()
    part, part_bf16 = pair_sum(g, pair_exchange(g), jnp.reshape(c, (1,)).astype(I32))
    half = chip_sum(part, chip_exchange(part_bf16), jnp.reshape(2 * x + y, (1,)).astype(I32))
    return join_halves(half)


def adamw(w, g, m, v, *, name):
    rows, cols = w.shape
    tm = _tile(rows, 256)
    c1 = 1.0 - ADAM_B1 ** ADAM_STEP
    c2 = 1.0 - ADAM_B2 ** ADAM_STEP

    def body(w_ref, g_ref, m_ref, v_ref, d_ref, mo_ref, vo_ref):
        gv = g_ref[...]
        mn = ADAM_B1 * m_ref[...] + (1.0 - ADAM_B1) * gv
        vn = ADAM_B2 * v_ref[...] + (1.0 - ADAM_B2) * (gv * gv)
        d_ref[...] = -ADAM_LR * ((mn / c1) / (jnp.sqrt(vn / c2) + ADAM_EPS) + ADAM_WD * w_ref[...])
        mo_ref[...] = mn
        vo_ref[...] = vn

    blk = pl.BlockSpec((tm, cols), lambda i: (i, 0))
    shape = jax.ShapeDtypeStruct(w.shape, F32)
    return _pallas(
        body, name=name, grid=(rows // tm,),
        in_specs=[blk] * 4, out_specs=[blk] * 3, out_shape=[shape] * 3,
        compiler_params=_params(("parallel",), 16 * tm * cols * 4),
    )(w, g, m, v)


WEIGHTS = ("l0_norm", "l0_w_in", "l0_w_out", "l1_norm", "l1_w_in", "l1_q_a_norm", "l1_w_uq", "l1_kv_a_norm", "l1_w_ukv",
           "l1_q_head_norm", "l1_k_head_norm", "l1_w_out", "l2_norm", "l2_w_in", "l2_q_head_norm", "l2_k_head_norm",
           "l2_sinks", "l2_w_out", "l3_norm", "l3_w_in", "l3_w_out")


def kernel(x, l0_norm, l0_w_in, l0_w_out, l1_norm, l1_w_in, l1_q_a_norm, l1_w_uq, l1_kv_a_norm, l1_w_ukv, l1_q_head_norm, l1_k_head_norm, l1_w_out, l2_norm, l2_w_in, l2_q_head_norm, l2_k_head_norm, l2_sinks, l2_w_out, l3_norm, l3_w_in, l3_w_out, loss_target, m_l0_norm, m_l0_w_in, m_l0_w_out, m_l1_norm, m_l1_w_in, m_l1_q_a_norm, m_l1_w_uq, m_l1_kv_a_norm, m_l1_w_ukv, m_l1_q_head_norm, m_l1_k_head_norm, m_l1_w_out, m_l2_norm, m_l2_w_in, m_l2_q_head_norm, m_l2_k_head_norm, m_l2_sinks, m_l2_w_out, m_l3_norm, m_l3_w_in, m_l3_w_out, v_l0_norm, v_l0_w_in, v_l0_w_out, v_l1_norm, v_l1_w_in, v_l1_q_a_norm, v_l1_w_uq, v_l1_kv_a_norm, v_l1_w_ukv, v_l1_q_head_norm, v_l1_k_head_norm, v_l1_w_out, v_l2_norm, v_l2_w_in, v_l2_q_head_norm, v_l2_k_head_norm, v_l2_sinks, v_l2_w_out, v_l3_norm, v_l3_w_in, v_l3_w_out):
    given = dict(locals())
    w = {n: given[n] for n in WEIGHTS}
    m = {n: given["m_" + n] for n in WEIGHTS}
    v = {n: given["v_" + n] for n in WEIGHTS}
    mat_names = [t[0] for t in MATS]
    vec_names = [t[0] for t in VECS]

    gathered = gather_weights(pack_shards({n: w[n] for n in mat_names}).astype(BF16))
    full = unpack_full(gathered)
    full.update({n: w[n] for n in vec_names})
    loss_tile, grad_x, grads = local_step(x[0], loss_target[0], full)

    block = reduce_scatter(pack_full({n: grads[n] for n in mat_names}))
    g = unpack_shards(block)
    vec_sum = sum_over_devices(pack_vecs({n: grads[n] for n in vec_names}, loss=loss_tile[0, 0]))
    loss = vec_sum[LOSS_SLOT[0], LOSS_SLOT[1]]

    delta, new_m, new_v = {}, {}, {}
    for n in mat_names:
        delta[n], new_m[n], new_v[n] = adamw(w[n], g[n], m[n], v[n], name=f"adamw_{n}")
    dv, mv, vv = adamw(pack_vecs(w), vec_sum, pack_vecs(m), pack_vecs(v), name="adamw_vecs")
    g.update(unpack_vecs(vec_sum))
    delta.update(unpack_vecs(dv))
    new_m.update(unpack_vecs(mv))
    new_v.update(unpack_vecs(vv))
    return (loss, grad_x[None], *[g[n] for n in WEIGHTS], *[delta[n] for n in WEIGHTS],
            *[new_m[n] for n in WEIGHTS], *[new_v[n] for n in WEIGHTS])
```

```python
import math

import jax
import jax.numpy as jnp
from jax import lax
from jax.experimental import pallas as pl
from jax.experimental.pallas import tpu as pltpu

F32 = jnp.float32
BF16 = jnp.bfloat16
I32 = jnp.int32
MESH = pl.DeviceIdType.MESH

NORM_EPS = 1e-6
D_MODEL = 1024
HEAD64 = 64
LANES = 128
BLK = 128
MLA_HEADS = 8
MLA_QK = 192
MLA_PAD = 256
ROPE_THETA = 10000.0
SWA_HEADS = 16
SWA_KV = 4
VMEM_CAP = 56 * 1024 * 1024
MATMUL_TILE_BYTES = 8 * 1024 * 1024

ADAM_LR, ADAM_B1, ADAM_B2, ADAM_EPS, ADAM_WD, ADAM_STEP = 0.001, 0.9, 0.999, 1e-08, 0.01, 10

NT = (((1,), (1,)), ((), ()))
NN = (((1,), (0,)), ((), ()))
TN = (((0,), (0,)), ((), ()))


def _dot(a, b, dims=NN):
    return lax.dot_general(a, b, dims, preferred_element_type=F32)


def _tile(n, pref):
    for t in (pref, 512, 256, 128):
        if t <= pref and n % t == 0:
            return t
    return n


def _params(sem, vmem_bytes):
    limit = int(min(max(2 * vmem_bytes, 24 * 1024 * 1024), VMEM_CAP))
    return pltpu.CompilerParams(dimension_semantics=sem, vmem_limit_bytes=limit)


def _in_hbm(s):
    return pltpu.HBM(s.shape, s.dtype) if len(s.shape) >= 2 else s


def _pallas(*args, out_shape, **kwargs):
    out_shape = [_in_hbm(s) for s in out_shape] if isinstance(out_shape, (list, tuple)) else _in_hbm(out_shape)
    call = pl.pallas_call(*args, out_shape=out_shape, **kwargs)

    def run(*operands):
        return call(*[pltpu.with_memory_space_constraint(a, pltpu.HBM) if a.ndim >= 2 else a for a in operands])

    return run


def _split(v):
    hi = v.astype(BF16)
    return hi, (v - hi.astype(F32)).astype(BF16)


def _dot2(v, m):
    hi, lo = _split(v)
    return _dot(hi, m) + _dot(lo, m)


def _dot_split(v, m2):
    return _dot(jnp.concatenate(_split(v), axis=1), m2)


def _first_half(shape):
    return lax.broadcasted_iota(I32, shape, 1) < HEAD64


def _sigmoid(g):
    return 1.0 / (1.0 + jnp.exp(-g))


def matmul(a, b, mode, *, name, out_dtype=F32, add=None, b_cols=None, tm=512, tn=1024):
    if mode == "nn":
        (M, K), (K2, N) = a.shape, b.shape
        if b_cols is not None:
            N = b_cols[1]
    elif mode == "nt":
        (M, K), (N, K2) = a.shape, b.shape
    else:
        (K, M), (K2, N) = a.shape, b.shape
    assert K == K2, (a.shape, b.shape, mode)
    tm, tn = _tile(M, tm), _tile(N, tn)
    while K * tm * a.dtype.itemsize > MATMUL_TILE_BYTES:
        tm //= 2
    while K * tn * b.dtype.itemsize > MATMUL_TILE_BYTES:
        tn //= 2
    dims = {"nn": NN, "nt": NT, "tn": TN}[mode]
    n_in = 2 if add is None else 3

    def body(*refs):
        a_ref, b_ref = refs[:2]
        r = _dot(a_ref[...].astype(BF16), b_ref[...].astype(BF16), dims)
        if add is not None:
            r = r + refs[2][...]
        for o_ref in refs[n_in:]:
            o_ref[...] = r.astype(o_ref.dtype)

    a_spec = pl.BlockSpec((K, tm), lambda i, j: (0, i)) if mode == "tn" else pl.BlockSpec((tm, K), lambda i, j: (i, 0))
    jb = 0 if b_cols is None else b_cols[0] // tn
    assert b_cols is None or (mode == "nn" and b_cols[0] % tn == 0)
    b_spec = pl.BlockSpec((tn, K), lambda i, j: (j, 0)) if mode == "nt" else pl.BlockSpec((K, tn), lambda i, j: (0, jb + j))
    o_spec = pl.BlockSpec((tm, tn), lambda i, j: (i, j))
    in_specs, args = [a_spec, b_spec], [a, b]
    if add is not None:
        in_specs.append(o_spec)
        args.append(add)
    vm = 2 * (tm * K * a.dtype.itemsize + K * tn * b.dtype.itemsize) + 5 * tm * tn * 4
    out_dtypes = list(out_dtype) if isinstance(out_dtype, (tuple, list)) else [out_dtype]
    res = _pallas(
        body, name=name, grid=(M // tm, N // tn),
        in_specs=in_specs, out_specs=[o_spec] * len(out_dtypes),
        out_shape=[jax.ShapeDtypeStruct((M, N), d) for d in out_dtypes],
        compiler_params=_params(("parallel", "parallel"), vm),
    )(*args)
    return res[0] if len(out_dtypes) == 1 else res


def rmsnorm_fwd(x, g, *, col_off, width, out_dtype, name, tm=256):
    S = x.shape[0]
    assert col_off % width == 0
    cb = col_off // width
    tm = _tile(S, tm)

    def body(x_ref, g_ref, o_ref):
        v = x_ref[...]
        r = lax.rsqrt(jnp.mean(v * v, axis=1, keepdims=True) + NORM_EPS)
        o_ref[...] = (v * r * g_ref[...]).astype(out_dtype)

    return _pallas(
        body, name=name, grid=(S // tm,),
        in_specs=[pl.BlockSpec((tm, width), lambda i: (i, cb)), pl.BlockSpec((1, width), lambda i: (0, 0))],
        out_specs=pl.BlockSpec((tm, width), lambda i: (i, 0)),
        out_shape=jax.ShapeDtypeStruct((S, width), out_dtype),
        compiler_params=_params(("parallel",), 4 * tm * width * 4),
    )(x, g.reshape(1, width))


def rmsnorm_bwd(x, g, dh, *, col_off, width, out_dtype, name, res=None, tm=256):
    S = x.shape[0]
    cb = col_off // width
    tm = _tile(S, tm)

    def body(*refs):
        if res is None:
            x_ref, g_ref, dh_ref, dx_ref, dg_ref = refs
        else:
            x_ref, g_ref, dh_ref, res_ref, dx_ref, dg_ref = refs

        @pl.when(pl.program_id(0) == 0)
        def _():
            dg_ref[...] = jnp.zeros_like(dg_ref)

        v = x_ref[...]
        dhv = dh_ref[...].astype(F32)
        r = lax.rsqrt(jnp.mean(v * v, axis=1, keepdims=True) + NORM_EPS)
        y = v * r
        dy = dhv * g_ref[...]
        dx = r * (dy - y * jnp.mean(dy * y, axis=1, keepdims=True))
        if res is not None:
            dx = dx + res_ref[...]
        dx_ref[...] = dx.astype(out_dtype)
        dg_ref[...] += jnp.sum(dhv * y, axis=0, keepdims=True)

    row = pl.BlockSpec((tm, width), lambda i: (i, 0))
    in_specs = [pl.BlockSpec((tm, width), lambda i: (i, cb)), pl.BlockSpec((1, width), lambda i: (0, 0)), row]
    args = [x, g.reshape(1, width), dh]
    if res is not None:
        in_specs.append(row)
        args.append(res)
    return _pallas(
        body, name=name, grid=(S // tm,),
        in_specs=in_specs,
        out_specs=[row, pl.BlockSpec((1, width), lambda i: (0, 0))],
        out_shape=[jax.ShapeDtypeStruct((S, width), out_dtype), jax.ShapeDtypeStruct((1, width), F32)],
        compiler_params=_params(("arbitrary",), 8 * tm * width * 4),
    )(*args)


def _group_ones():
    r = lax.broadcasted_iota(I32, (LANES, LANES), 0) // HEAD64
    c = lax.broadcasted_iota(I32, (LANES, LANES), 1) // HEAD64
    return (r == c).astype(BF16)


def _segmean64(v):
    return _dot2(v, _group_ones()) * (1.0 / HEAD64)


def headnorm_fwd(x, g64, *, col_off, width, name, tm=512):
    S = x.shape[0]
    cb = col_off // LANES
    tm = _tile(S, tm)

    def body(x_ref, g_ref, o_ref):
        v = x_ref[...]
        r = lax.rsqrt(_segmean64(v * v) + NORM_EPS)
        o_ref[...] = (v * r * g_ref[...]).astype(BF16)

    return _pallas(
        body, name=name, grid=(S // tm, width // LANES),
        in_specs=[pl.BlockSpec((tm, LANES), lambda i, j: (i, cb + j)), pl.BlockSpec((1, LANES), lambda i, j: (0, 0))],
        out_specs=pl.BlockSpec((tm, LANES), lambda i, j: (i, j)),
        out_shape=jax.ShapeDtypeStruct((S, width), BF16),
        compiler_params=_params(("parallel", "parallel"), 8 * tm * LANES * 4),
    )(x, jnp.tile(g64, 2).reshape(1, LANES))


def headnorm_bwd(x, g64, dh, *, col_off, width, name, tm=512):
    S = x.shape[0]
    cb = col_off // LANES
    tm = _tile(S, tm)

    def body(x_ref, g_ref, dh_ref, dx_ref, dg_ref):
        @pl.when((pl.program_id(0) == 0) & (pl.program_id(1) == 0))
        def _():
            dg_ref[...] = jnp.zeros_like(dg_ref)

        v = x_ref[...]
        dhv = dh_ref[...]
        r = lax.rsqrt(_segmean64(v * v) + NORM_EPS)
        y = v * r
        dy = dhv * g_ref[...]
        dx_ref[...] = (r * (dy - y * _segmean64(dy * y))).astype(BF16)
        dg_ref[...] += jnp.sum(dhv * y, axis=0, keepdims=True)

    return _pallas(
        body, name=name, grid=(width // LANES, S // tm),
        in_specs=[pl.BlockSpec((tm, LANES), lambda j, i: (i, cb + j)), pl.BlockSpec((1, LANES), lambda j, i: (0, 0)),
                  pl.BlockSpec((tm, LANES), lambda j, i: (i, j))],
        out_specs=[pl.BlockSpec((tm, LANES), lambda j, i: (i, j)), pl.BlockSpec((1, LANES), lambda j, i: (0, 0))],
        out_shape=[jax.ShapeDtypeStruct((S, width), BF16), jax.ShapeDtypeStruct((1, LANES), F32)],
        compiler_params=_params(("arbitrary", "arbitrary"), 10 * tm * LANES * 4),
    )(x, jnp.tile(g64, 2).reshape(1, LANES), dh)


def gate_fwd(o, proj, *, gate_off, name, tm=256):
    S, W = o.shape
    cb = gate_off // W
    tm = _tile(S, tm)

    def body(o_ref, g_ref, out_ref):
        g = g_ref[...]
        out_ref[...] = (o_ref[...] * (g * _sigmoid(g))).astype(BF16)

    return _pallas(
        body, name=name, grid=(S // tm,),
        in_specs=[pl.BlockSpec((tm, W), lambda i: (i, 0)), pl.BlockSpec((tm, W), lambda i: (i, cb))],
        out_specs=pl.BlockSpec((tm, W), lambda i: (i, 0)),
        out_shape=jax.ShapeDtypeStruct((S, W), BF16),
        compiler_params=_params(("parallel",), 6 * tm * W * 4),
    )(o, proj)


def gate_bwd(dog, o, proj, *, gate_off, name, tm=256):
    S, W = o.shape
    cb = gate_off // W
    tm = _tile(S, tm)

    def body(d_ref, o_ref, g_ref, do_ref, dg_ref):
        g = g_ref[...]
        d = d_ref[...]
        s = _sigmoid(g)
        do_ref[...] = d * (g * s)
        dg_ref[...] = (d * o_ref[...] * (s * (1.0 + g * (1.0 - s)))).astype(BF16)

    row = pl.BlockSpec((tm, W), lambda i: (i, 0))
    return _pallas(
        body, name=name, grid=(S // tm,),
        in_specs=[row, row, pl.BlockSpec((tm, W), lambda i: (i, cb))],
        out_specs=[row, row],
        out_shape=[jax.ShapeDtypeStruct((S, W), F32), jax.ShapeDtypeStruct((S, W), BF16)],
        compiler_params=_params(("parallel",), 10 * tm * W * 4),
    )(dog, o, proj)


def loss_head(y, target, *, name, tm=256):
    S, W = y.shape
    tm = _tile(S, tm)
    n = S // tm

    def body(y_ref, t_ref, dy_ref, l_ref, acc_ref):
        i = pl.program_id(0)

        @pl.when(i == 0)
        def _():
            acc_ref[...] = jnp.zeros_like(acc_ref)

        e = y_ref[...] - t_ref[...]
        dy_ref[...] = e * (1.0 / W)
        acc_ref[...] += jnp.sum(e * e, axis=0, keepdims=True)

        @pl.when(i == n - 1)
        def _():
            l_ref[...] = jnp.full(l_ref.shape, (0.5 / W) * jnp.sum(acc_ref[...]), F32)

    row = pl.BlockSpec((tm, W), lambda i: (i, 0))
    return _pallas(
        body, name=name, grid=(n,),
        in_specs=[row, row],
        out_specs=[row, pl.BlockSpec((8, LANES), lambda i: (0, 0))],
        out_shape=[jax.ShapeDtypeStruct((S, W), F32), jax.ShapeDtypeStruct((8, LANES), F32)],
        scratch_shapes=[pltpu.VMEM((1, W), F32)],
        compiler_params=_params(("arbitrary",), 8 * tm * W * 4),
    )(y, target)


def _stack_heads(t, zero):
    first = _first_half(t.shape)
    return jnp.concatenate([jnp.where(first, t, zero), jnp.where(first, zero, t)], axis=0)


def _sb_weights(qs, ks, mask, upper, rss):
    zs = [_dot(q, k, NT) for q, k in zip(qs, ks)]
    sps = [jnp.maximum(z, 0.0) + jnp.log(1.0 + jnp.exp(-jnp.abs(z))) for z in zs]
    gs = [z - sp for z, sp in zip(zs, sps)]
    if mask is not None:
        sps = [jnp.where(mask, sp, 0.0) for sp in sps]
    cums = [_dot_split(sp, upper) for sp in sps]
    avs = [jnp.exp(g - (cum + rs)) for g, cum, rs in zip(gs, cums, rss)]
    if mask is not None:
        avs = [jnp.where(mask, a, 0.0) for a in avs]
    return avs, sps, gs


def _sb_consts():
    row = lax.broadcasted_iota(I32, (BLK, BLK), 0)
    col = lax.broadcasted_iota(I32, (BLK, BLK), 1)
    diag = col < row
    return row, col, jnp.concatenate([diag, diag], axis=0)


SB_DEAD = 88.0


def _sb_walk_left(block, i, carry):
    def least(c):
        m = c[0][1]
        for pair in c[1:]:
            m = jnp.minimum(m, pair[1])
        return jnp.min(m)

    def cond(state):
        jj, _, low = state
        return (jj < i) & (low < SB_DEAD)

    def body(state):
        jj, c, _ = state
        c = block(i - 1 - jj, c, None)
        return jj + 1, c, least(c)

    return lax.while_loop(cond, body, (jnp.int32(0), carry, least(carry)))[1]


SB_W = 1024


def sb_attn_fwd(qkv, *, name, pairs=8, ride=None):
    S = qkv.shape[0]
    W = SB_W
    PW = pairs * LANES
    ngrp, nq = W // PW, S // BLK
    assert ride is None or (ngrp == 1 and nq >= 3)

    def body(*refs):
        if ride is None:
            q_ref, k_ref, v_ref, o_ref = refs
        else:
            q_ref, k_ref, v_ref, o_ref = refs[0], refs[1], refs[2], refs[4]
            send, forward, finish = _gather_steps(refs[3], refs[5], refs[6], refs[7])
        i = pl.program_id(1)
        if ride is not None:
            pl.when(i == 0)(send)
            pl.when(i == nq // 2)(forward)
        row, col, diag = _sb_consts()
        upper = jnp.tile((row > col).astype(BF16), (2, 1))
        zero = jnp.zeros((BLK, LANES), BF16)
        qs = [_stack_heads(q_ref[:, p * LANES:(p + 1) * LANES] * 0.125, zero) for p in range(pairs)]

        def block(j, carry, mask):
            rows = pl.ds(pl.multiple_of(j * BLK, BLK), BLK)
            cols = [slice(p * LANES, (p + 1) * LANES) for p in range(pairs)]
            avs, sps, _ = _sb_weights(qs, [k_ref[rows, c] for c in cols], mask, upper, [c[1] for c in carry])
            abs_ = [a.astype(BF16) for a in avs]
            outs = [_dot(jnp.concatenate([ab[:BLK], ab[BLK:]], axis=1), _stack_heads(v_ref[rows, c], zero))
                    for ab, c in zip(abs_, cols)]
            return tuple((carry[p][0] + outs[p], carry[p][1] + jnp.sum(sps[p], axis=1, keepdims=True))
                         for p in range(pairs))

        init = tuple((jnp.zeros((BLK, LANES), F32), jnp.zeros((2 * BLK, 1), F32)) for _ in range(pairs))
        carry = _sb_walk_left(block, i, block(i, init, diag))
        for p in range(pairs):
            o_ref[:, p * LANES:(p + 1) * LANES] = carry[p][0]
        if ride is not None:
            pl.when(i == nq - 1)(finish)

    once = pl.Buffered(1)
    hbm = pl.BlockSpec(memory_space=pltpu.HBM)
    in_specs = [pl.BlockSpec((BLK, PW), lambda p, i: (i, p)),
                pl.BlockSpec((S, PW), lambda p, i: (0, ngrp + p), pipeline_mode=once),
                pl.BlockSpec((S, PW), lambda p, i: (0, 2 * ngrp + p), pipeline_mode=once)]
    out_specs = [pl.BlockSpec((BLK, PW), lambda p, i: (i, p))]
    out_shape = [jax.ShapeDtypeStruct((S, W), F32)]
    args = [qkv, qkv, qkv]
    if ride is not None:
        in_specs.append(hbm)
        out_specs.append(hbm)
        out_shape.append(jax.ShapeDtypeStruct((N_CHIPS,) + ride.shape, ride.dtype))
        args.append(ride)
    res = _pallas(
        body, name=name, grid=(ngrp, nq),
        in_specs=in_specs, out_specs=out_specs, out_shape=out_shape,
        scratch_shapes=[] if ride is None else list(GATHER_SEMS),
        compiler_params=_params(("parallel", "arbitrary"), 2 * S * PW * 2 + 16 * BLK * PW * 4),
    )(*args)
    return res[0] if ride is None else (res[0], place_own_block(res[1], ride))


def sb_attn_bwd(qkv, o, do, *, name, pairs=4):
    S = qkv.shape[0]
    W = SB_W
    PW = pairs * LANES
    ngrp, nq = W // PW, S // BLK

    def body(q_ref, k_ref, v_ref, o_ref, do_ref, dq_ref, dk_ref, dv_ref):
        i = pl.program_id(1)

        @pl.when(i == 0)
        def _():
            dk_ref[...] = jnp.zeros_like(dk_ref)
            dv_ref[...] = jnp.zeros_like(dv_ref)

        row, col, diag = _sb_consts()
        upper = jnp.tile((row > col).astype(BF16), (2, 1))
        upper_incl = jnp.tile((row >= col).astype(BF16), (2, 1))
        first = _first_half((BLK, LANES))
        zero = jnp.zeros((BLK, LANES), BF16)
        qs, dos, tots = [], [], []
        for p in range(pairs):
            cols = slice(p * LANES, (p + 1) * LANES)
            qs.append(_stack_heads(q_ref[:, cols] * 0.125, zero))
            dob = do_ref[:, cols].astype(BF16)
            dos.append(_stack_heads(dob, zero))
            prod = dob.astype(F32) * o_ref[:, cols]
            tots.append(jnp.concatenate([jnp.sum(jnp.where(first, prod, 0.0), axis=1, keepdims=True),
                                         jnp.sum(jnp.where(first, 0.0, prod), axis=1, keepdims=True)], axis=0))

        def block(j, carry, mask):
            rows = pl.ds(pl.multiple_of(j * BLK, BLK), BLK)
            P = range(pairs)
            cols = [slice(p * LANES, (p + 1) * LANES) for p in P]
            ks = [k_ref[rows, c] for c in cols]
            das = [_dot(dos[p], v_ref[rows, cols[p]], NT) for p in P]
            avs, sps, gs = _sb_weights(qs, ks, mask, upper, [c[1] for c in carry])
            abs_ = [a.astype(BF16) for a in avs]
            es = [ab.astype(F32) * da for ab, da in zip(abs_, das)]
            sufs = [_dot_split(e, upper_incl) for e in es]
            lefts = [tots[p] - (sufs[p] + carry[p][2]) for p in P]
            dzs = [es[p] - jnp.exp(gs[p]) * (es[p] + lefts[p]) for p in P]
            if mask is not None:
                dzs = [jnp.where(mask, dz, 0.0) for dz in dzs]
            dzbs = [dz.astype(BF16) for dz in dzs]
            dks = [_dot(dzbs[p], qs[p], TN) for p in P]
            dvs = [_dot(abs_[p], dos[p], TN) for p in P]
            dqs = [_dot(jnp.concatenate([dzbs[p][:BLK], dzbs[p][BLK:]], axis=1), _stack_heads(ks[p], zero)) for p in P]
            for p in P:
                dk_ref[rows, cols[p]] += dks[p]
                dv_ref[rows, cols[p]] += dvs[p]
            return tuple((carry[p][0] + dqs[p], carry[p][1] + jnp.sum(sps[p], axis=1, keepdims=True),
                          carry[p][2] + jnp.sum(es[p], axis=1, keepdims=True)) for p in P)

        col0 = jnp.zeros((2 * BLK, 1), F32)
        init = tuple((jnp.zeros((BLK, LANES), F32), col0, col0) for _ in range(pairs))
        carry = _sb_walk_left(block, i, block(i, init, diag))
        for p in range(pairs):
            dq_ref[:, p * LANES:(p + 1) * LANES] = (carry[p][0] * 0.125).astype(BF16)

    once = pl.Buffered(1)
    tile = pl.BlockSpec((BLK, PW), lambda p, i: (i, p))
    full = pl.BlockSpec((S, PW), lambda p, i: (0, p), pipeline_mode=once)
    shape = jax.ShapeDtypeStruct((S, W), F32)
    return _pallas(
        body, name=name, grid=(ngrp, nq),
        in_specs=[tile,
                  pl.BlockSpec((S, PW), lambda p, i: (0, ngrp + p), pipeline_mode=once),
                  pl.BlockSpec((S, PW), lambda p, i: (0, 2 * ngrp + p), pipeline_mode=once),
                  tile, tile],
        out_specs=[tile, full, full],
        out_shape=[jax.ShapeDtypeStruct((S, W), BF16), shape, shape],
        compiler_params=_params(("parallel", "arbitrary"), 2 * S * PW * 2 + 2 * S * PW * 4 + 16 * BLK * PW * 4),
    )(qkv, qkv, qkv, o, do)


FB = 256


def flash_fwd(qn, kn, vb, *, name, heads=4):
    S = qn.shape[0]
    H, DK, DV = MLA_HEADS, MLA_PAD, LANES
    nq, ngrp = S // FB, H // heads
    scale = 1.0 / math.sqrt(MLA_QK)

    def body(q_ref, k_ref, v_ref, o_ref, lse_ref):
        i = pl.program_id(1)
        diag = lax.broadcasted_iota(I32, (FB, FB), 1) <= lax.broadcasted_iota(I32, (FB, FB), 0)
        qs = [q_ref[:, h * DK:(h + 1) * DK] for h in range(heads)]

        def block(j, carry, mask):
            rows = pl.ds(pl.multiple_of(j * FB, FB), FB)
            H = range(heads)
            ss = [_dot(qs[h], k_ref[rows, h * DK:(h + 1) * DK], NT) * scale for h in H]
            if mask is not None:
                ss = [jnp.where(mask, s, -1e30) for s in ss]
            ms = [jnp.maximum(carry[h][1], jnp.max(ss[h], axis=1, keepdims=True)) for h in H]
            ps = [jnp.exp(ss[h] - ms[h]) for h in H]
            ws = [jnp.exp(carry[h][1] - ms[h]) for h in H]
            pvs = [_dot(ps[h].astype(BF16), v_ref[rows, h * DV:(h + 1) * DV]) for h in H]
            return tuple((carry[h][0] * ws[h] + pvs[h], ms[h], carry[h][2] * ws[h] + jnp.sum(ps[h], axis=1, keepdims=True))
                         for h in H)

        init = tuple((jnp.zeros((FB, DV), F32), jnp.full((FB, 1), -1e30, F32), jnp.zeros((FB, 1), F32))
                     for _ in range(heads))
        carry = lax.fori_loop(0, i, lambda j, c: block(j, c, None), init)
        carry = block(i, carry, diag)
        for h in range(heads):
            acc, m, l = carry[h]
            o_ref[:, h * DV:(h + 1) * DV] = acc / l
            lse_ref[h] = m + jnp.log(l)

    return _pallas(
        body, name=name, grid=(ngrp, nq),
        in_specs=[pl.BlockSpec((FB, heads * DK), lambda g, i: (i, g)),
                  pl.BlockSpec((S, heads * DK), lambda g, i: (0, g), pipeline_mode=pl.Buffered(1)),
                  pl.BlockSpec((S, heads * DV), lambda g, i: (0, g), pipeline_mode=pl.Buffered(1))],
        out_specs=[pl.BlockSpec((FB, heads * DV), lambda g, i: (i, g)), pl.BlockSpec((heads, FB, 1), lambda g, i: (g, i, 0))],
        out_shape=[jax.ShapeDtypeStruct((S, H * DV), F32), jax.ShapeDtypeStruct((H, S, 1), F32)],
        compiler_params=_params(("parallel", "arbitrary"), 2 * S * heads * (DK + DV) * 2 + 16 * FB * FB * 4),
    )(qn, kn, vb)


def flash_bwd(qn, kn, vb, o, lse, do, *, name, heads=4):
    S = qn.shape[0]
    H, DK, DV = MLA_HEADS, MLA_PAD, LANES
    nq, ngrp = S // FB, H // heads
    scale = 1.0 / math.sqrt(MLA_QK)

    def body(q_ref, k_ref, v_ref, o_ref, lse_ref, do_ref, dq_ref, dk_ref, dv_ref):
        i = pl.program_id(1)

        @pl.when(i == 0)
        def _():
            dk_ref[...] = jnp.zeros_like(dk_ref)
            dv_ref[...] = jnp.zeros_like(dv_ref)

        diag = lax.broadcasted_iota(I32, (FB, FB), 1) <= lax.broadcasted_iota(I32, (FB, FB), 0)
        qs, dobs, deltas, lses = [], [], [], []
        for h in range(heads):
            do = do_ref[:, h * DV:(h + 1) * DV]
            qs.append(q_ref[:, h * DK:(h + 1) * DK])
            dobs.append(do.astype(BF16))
            deltas.append(jnp.sum(do * o_ref[:, h * DV:(h + 1) * DV], axis=1, keepdims=True))
            lses.append(lse_ref[h])

        def block(j, carry, mask):
            rows = pl.ds(pl.multiple_of(j * FB, FB), FB)
            H = range(heads)
            kcs = [slice(h * DK, (h + 1) * DK) for h in H]
            vcs = [slice(h * DV, (h + 1) * DV) for h in H]
            ks = [k_ref[rows, kcs[h]] for h in H]
            ss = [_dot(qs[h], ks[h], NT) for h in H]
            dps = [_dot(dobs[h], v_ref[rows, vcs[h]], NT) for h in H]
            ps = [jnp.exp(ss[h] * scale - lses[h]) for h in H]
            if mask is not None:
                ps = [jnp.where(mask, p, 0.0) for p in ps]
            pbs = [p.astype(BF16) for p in ps]
            dss = [(ps[h] * (dps[h] - deltas[h]) * scale).astype(BF16) for h in H]
            dvs = [_dot(pbs[h], dobs[h], TN) for h in H]
            dks = [_dot(dss[h], qs[h], TN) for h in H]
            dqs = [_dot(dss[h], ks[h]) for h in H]
            for h in H:
                dv_ref[rows, vcs[h]] += dvs[h]
                dk_ref[rows, kcs[h]] += dks[h]
            return tuple(carry[h] + dqs[h] for h in H)

        carry = lax.fori_loop(0, i, lambda j, c: block(j, c, None), tuple(jnp.zeros((FB, DK), F32) for _ in range(heads)))
        carry = block(i, carry, diag)
        for h in range(heads):
            dq_ref[:, h * DK:(h + 1) * DK] = carry[h]

    qtile = pl.BlockSpec((FB, heads * DK), lambda g, i: (i, g))
    otile = pl.BlockSpec((FB, heads * DV), lambda g, i: (i, g))
    once = pl.Buffered(1)
    kfull = pl.BlockSpec((S, heads * DK), lambda g, i: (0, g), pipeline_mode=once)
    vfull = pl.BlockSpec((S, heads * DV), lambda g, i: (0, g), pipeline_mode=once)
    return _pallas(
        body, name=name, grid=(ngrp, nq),
        in_specs=[qtile, kfull, vfull, otile, pl.BlockSpec((heads, FB, 1), lambda g, i: (g, i, 0)), otile],
        out_specs=[qtile, kfull, vfull],
        out_shape=[jax.ShapeDtypeStruct((S, H * DK), F32), jax.ShapeDtypeStruct((S, H * DK), F32),
                   jax.ShapeDtypeStruct((S, H * DV), F32)],
        compiler_params=_params(("parallel", "arbitrary"), S * heads * (DK + DV) * 6 + 16 * FB * FB * 4),
    )(qn, kn, vb, o, lse, do)


def _rope_tables(S):
    half = 32
    inv_freq = ROPE_THETA ** (-jnp.arange(half, dtype=F32) / half)
    ang = jnp.arange(S).astype(F32)[:, None] * inv_freq[None, :]
    cos, sin = jnp.cos(ang), jnp.sin(ang)
    ones, zeros = jnp.ones((S, LANES), F32), jnp.zeros((S, LANES), F32)
    pad = jnp.zeros((S, 64), F32)
    return (jnp.concatenate([ones, cos, cos, pad + 1.0], axis=1),
            jnp.concatenate([zeros, -sin, sin, pad], axis=1))


def _rope_partner(u):
    lane = lax.broadcasted_iota(I32, u.shape, 1)
    return jnp.where((lane % HEAD64) < 32, pltpu.roll(u, LANES - 32, 1), pltpu.roll(u, 32, 1))


def _normrope(raw, g, cos, sgn):
    r = lax.rsqrt(jnp.sum(raw * raw, axis=1, keepdims=True) * (1.0 / MLA_QK) + NORM_EPS)
    y = raw * r
    u = y * g
    pe = u[:, LANES:]
    out = jnp.concatenate([u[:, :LANES], pe * cos[:, LANES:] + _rope_partner(pe) * sgn[:, LANES:]], axis=1)
    return out, y, r


def _normrope_bwd(dout, g, cos, sgn, y, r):
    dpe = dout[:, LANES:]
    du = jnp.concatenate([dout[:, :LANES], dpe * cos[:, LANES:] + _rope_partner(dpe * sgn[:, LANES:])], axis=1)
    dy = du * g
    draw = r * (dy - y * (jnp.sum(dy * y, axis=1, keepdims=True) * (1.0 / MLA_QK)))
    return draw, jnp.sum(du * y, axis=0, keepdims=True)


def mla_prep_fwd(qraw, kv, proj, gq, gk, cos, sgn, *, kpe_off, name, tm=1024):
    S = qraw.shape[0]
    tm = _tile(S, tm)
    kb = kpe_off // LANES

    def body(q_ref, kn_ref, v_ref, kpe_ref, gq_ref, gk_ref, c_ref, s_ref, qo_ref, ko_ref, vo_ref):
        cos, sgn = c_ref[...], s_ref[...]
        qo_ref[...] = _normrope(q_ref[...], gq_ref[...], cos, sgn)[0].astype(BF16)
        kraw = jnp.concatenate([kn_ref[...], kpe_ref[...]], axis=1)
        ko_ref[...] = _normrope(kraw, gk_ref[...], cos, sgn)[0].astype(BF16)
        vo_ref[...] = v_ref[...].astype(BF16)

    head = pl.BlockSpec((tm, MLA_PAD), lambda i, h: (i, h))
    gain = pl.BlockSpec((1, MLA_PAD), lambda i, h: (0, 0))
    tab = pl.BlockSpec((tm, MLA_PAD), lambda i, h: (i, 0))
    return _pallas(
        body, name=name, grid=(S // tm, MLA_HEADS),
        in_specs=[head, pl.BlockSpec((tm, LANES), lambda i, h: (i, 2 * h)), pl.BlockSpec((tm, LANES), lambda i, h: (i, 2 * h + 1)),
                  pl.BlockSpec((tm, LANES), lambda i, h: (i, kb)), gain, gain, tab, tab],
        out_specs=[head, head, pl.BlockSpec((tm, LANES), lambda i, h: (i, h))],
        out_shape=[jax.ShapeDtypeStruct(qraw.shape, BF16), jax.ShapeDtypeStruct(qraw.shape, BF16),
                   jax.ShapeDtypeStruct((S, MLA_HEADS * LANES), BF16)],
        compiler_params=_params(("parallel", "arbitrary"), 16 * tm * MLA_PAD * 4),
    )(qraw, kv, kv, proj, gq, gk, cos, sgn)


def mla_prep_bwd(dqn, dkn, dv, qraw, kv, proj, gq, gk, cos, sgn, *, kpe_off, name, tm=512):
    S = qraw.shape[0]
    tm = _tile(S, tm)
    kb = kpe_off // LANES

    def body(dq_ref, dk_ref, dv_ref, q_ref, kn_ref, kpe_ref, gq_ref, gk_ref, c_ref, s_ref,
             dqo_ref, dkv_ref, dkpe_ref, dgq_ref, dgk_ref, acc_ref):
        i, h = pl.program_id(0), pl.program_id(1)

        @pl.when((i == 0) & (h == 0))
        def _():
            dgq_ref[...] = jnp.zeros_like(dgq_ref)
            dgk_ref[...] = jnp.zeros_like(dgk_ref)

        @pl.when(h == 0)
        def _():
            acc_ref[...] = jnp.zeros_like(acc_ref)

        cos, sgn = c_ref[...], s_ref[...]
        _, yq, rq = _normrope(q_ref[...], gq_ref[...], cos, sgn)
        dq, dgq = _normrope_bwd(dq_ref[...], gq_ref[...], cos, sgn, yq, rq)
        dqo_ref[...] = dq.astype(BF16)
        dgq_ref[...] += dgq
        kraw = jnp.concatenate([kn_ref[...], kpe_ref[...]], axis=1)
        _, yk, rk = _normrope(kraw, gk_ref[...], cos, sgn)
        dk, dgk = _normrope_bwd(dk_ref[...], gk_ref[...], cos, sgn, yk, rk)
        dgk_ref[...] += dgk
        dkv_ref[...] = jnp.concatenate([dk[:, :LANES], dv_ref[...]], axis=1).astype(BF16)
        acc_ref[...] += dk[:, LANES:]

        @pl.when(h == MLA_HEADS - 1)
        def _():
            dkpe_ref[...] = acc_ref[...].astype(BF16)

    head = pl.BlockSpec((tm, MLA_PAD), lambda i, h: (i, h))
    gain = pl.BlockSpec((1, MLA_PAD), lambda i, h: (0, 0))
    tab = pl.BlockSpec((tm, MLA_PAD), lambda i, h: (i, 0))
    return _pallas(
        body, name=name, grid=(S // tm, MLA_HEADS),
        in_specs=[head, head, pl.BlockSpec((tm, LANES), lambda i, h: (i, h)), head,
                  pl.BlockSpec((tm, LANES), lambda i, h: (i, 2 * h)), pl.BlockSpec((tm, LANES), lambda i, h: (i, kb)),
                  gain, gain, tab, tab],
        out_specs=[head, head, pl.BlockSpec((tm, LANES), lambda i, h: (i, 0)), gain, gain],
        out_shape=[jax.ShapeDtypeStruct(qraw.shape, BF16), jax.ShapeDtypeStruct(qraw.shape, BF16),
                   jax.ShapeDtypeStruct((S, LANES), BF16), jax.ShapeDtypeStruct((1, MLA_PAD), F32),
                   jax.ShapeDtypeStruct((1, MLA_PAD), F32)],
        scratch_shapes=[pltpu.VMEM((tm, LANES), F32)],
        compiler_params=_params(("arbitrary", "arbitrary"), 24 * tm * MLA_PAD * 4),
    )(dqn, dkn, dv, qraw, kv, proj, gq, gk, cos, sgn)


def swa_kv_prep(proj, gk64, *, k_off, v_off, name, tm=512):
    S = proj.shape[0]
    tm = _tile(S, tm)
    W = SWA_KV * HEAD64

    def body(k_ref, v_ref, g_ref, ko_ref, vo_ref):
        first = _first_half((tm, LANES))

        def dup(n):
            nr = pltpu.roll(n, HEAD64, 1)
            return jnp.where(first, n, nr), jnp.where(first, nr, n)

        for t in range(W // LANES):
            x = k_ref[:, t * LANES:(t + 1) * LANES]
            n = x * lax.rsqrt(_segmean64(x * x) + NORM_EPS) * g_ref[...]
            d0, d1 = dup(n)
            ko_ref[:, 2 * t * LANES:(2 * t + 1) * LANES] = d0.astype(BF16)
            ko_ref[:, (2 * t + 1) * LANES:(2 * t + 2) * LANES] = d1.astype(BF16)
            d0, d1 = dup(v_ref[:, t * LANES:(t + 1) * LANES])
            vo_ref[:, 2 * t * LANES:(2 * t + 1) * LANES] = d0.astype(BF16)
            vo_ref[:, (2 * t + 1) * LANES:(2 * t + 2) * LANES] = d1.astype(BF16)

    out = pl.BlockSpec((tm, SWA_KV * LANES), lambda i: (i, 0))
    shape = jax.ShapeDtypeStruct((S, SWA_KV * LANES), BF16)
    return _pallas(
        body, name=name, grid=(S // tm,),
        in_specs=[pl.BlockSpec((tm, W), lambda i: (i, k_off // W)), pl.BlockSpec((tm, W), lambda i: (i, v_off // W)),
                  pl.BlockSpec((1, LANES), lambda i: (0, 0))],
        out_specs=[out, out], out_shape=[shape, shape],
        compiler_params=_params(("parallel",), 12 * tm * W * 4),
    )(proj, proj, jnp.tile(gk64, 2).reshape(1, LANES))


def swa_kv_prep_bwd(dkdup, dvdup, proj, gk64, *, k_off, name, tm=512):
    S = proj.shape[0]
    tm = _tile(S, tm)
    W = SWA_KV * HEAD64

    def body(dk_ref, dv_ref, k_ref, g_ref, dko_ref, dvo_ref, dg_ref):
        @pl.when(pl.program_id(0) == 0)
        def _():
            dg_ref[...] = jnp.zeros_like(dg_ref)

        first = _first_half((tm, LANES))

        def fold(ref, t):
            d0 = ref[:, 2 * t * LANES:(2 * t + 1) * LANES]
            d1 = ref[:, (2 * t + 1) * LANES:(2 * t + 2) * LANES]
            return jnp.where(first, d0 + pltpu.roll(d0, HEAD64, 1), d1 + pltpu.roll(d1, HEAD64, 1))

        for t in range(W // LANES):
            dvo_ref[:, t * LANES:(t + 1) * LANES] = fold(dv_ref, t).astype(BF16)
            dh = fold(dk_ref, t)
            x = k_ref[:, t * LANES:(t + 1) * LANES]
            r = lax.rsqrt(_segmean64(x * x) + NORM_EPS)
            y = x * r
            dy = dh * g_ref[...]
            dko_ref[:, t * LANES:(t + 1) * LANES] = (r * (dy - y * _segmean64(dy * y))).astype(BF16)
            dg_ref[...] += jnp.sum(dh * y, axis=0, keepdims=True)

    dup = pl.BlockSpec((tm, SWA_KV * LANES), lambda i: (i, 0))
    out = pl.BlockSpec((tm, W), lambda i: (i, 0))
    shape = jax.ShapeDtypeStruct((S, W), BF16)
    return _pallas(
        body, name=name, grid=(S // tm,),
        in_specs=[dup, dup, pl.BlockSpec((tm, W), lambda i: (i, k_off // W)), pl.BlockSpec((1, LANES), lambda i: (0, 0))],
        out_specs=[out, out, pl.BlockSpec((1, LANES), lambda i: (0, 0))],
        out_shape=[shape, shape, jax.ShapeDtypeStruct((1, LANES), F32)],
        compiler_params=_params(("arbitrary",), 16 * tm * W * 4),
    )(dkdup, dvdup, proj, jnp.tile(gk64, 2).reshape(1, LANES))


def _swa_geometry(i):
    r = lax.broadcasted_iota(I32, (BLK, 2 * BLK), 0)
    c = lax.broadcasted_iota(I32, (BLK, 2 * BLK), 1)
    rel = r + BLK - c
    valid = (rel >= 0) & (rel < BLK) & ((c >= BLK) | (i > 0))
    return valid, rel.astype(F32)


def _swa_slope(h):
    return 2.0 ** (-8.0 * (h + 1) / SWA_HEADS)


def swa_attn_fwd(qn, kdup, vdup, sinks, *, name):
    S = qn.shape[0]
    nq = S // BLK
    group = SWA_HEADS // SWA_KV

    def body(q_ref, kp_ref, kc_ref, vp_ref, vc_ref, sink_ref, o_ref, lse_ref):
        i = pl.program_id(0)
        valid, rel = _swa_geometry(i)
        first = _first_half((BLK, LANES))
        lane = lax.broadcasted_iota(I32, (BLK, LANES), 1)
        lse_all = jnp.zeros((BLK, LANES), F32)
        zero = jnp.zeros((BLK, LANES), BF16)
        H = range(SWA_HEADS)
        kks = [jnp.concatenate([kp_ref[:, g * LANES:(g + 1) * LANES], kc_ref[:, g * LANES:(g + 1) * LANES]], axis=0)
               for g in range(SWA_KV)]
        vvs = [jnp.concatenate([vp_ref[:, g * LANES:(g + 1) * LANES], vc_ref[:, g * LANES:(g + 1) * LANES]], axis=0)
               for g in range(SWA_KV)]
        q2s = [q_ref[:, t * LANES:(t + 1) * LANES] for t in range(SWA_HEADS // 2)]
        qhs = [jnp.where(first if h % 2 == 0 else ~first, q2s[h // 2], zero) for h in H]
        ss = [_dot(qhs[h], kks[h // group], NT) * (1.0 / math.sqrt(HEAD64)) - _swa_slope(h) * rel for h in H]
        ss = [jnp.where(valid, s, -1e30) for s in ss]
        ms = [jnp.maximum(jnp.max(ss[h], axis=1, keepdims=True), sink_ref[h]) for h in H]
        es = [jnp.exp(ss[h] - ms[h]) for h in H]
        dens = [jnp.sum(es[h], axis=1, keepdims=True) + jnp.exp(sink_ref[h] - ms[h]) for h in H]
        outs = [_dot((es[h] / dens[h]).astype(BF16), vvs[h // group]) for h in H]
        for h in H:
            lse_all = jnp.where(lane == h, ms[h] + jnp.log(dens[h]), lse_all)
        for t in range(SWA_HEADS // 2):
            o_ref[:, t * LANES:(t + 1) * LANES] = jnp.where(first, outs[2 * t], outs[2 * t + 1])
        lse_ref[...] = lse_all

    prev = lambda i: (jnp.maximum(i - 1, 0), 0)
    cur = lambda i: (i, 0)
    kvw = SWA_KV * LANES
    return _pallas(
        body, name=name, grid=(nq,),
        in_specs=[pl.BlockSpec((BLK, 1024), cur), pl.BlockSpec((BLK, kvw), prev), pl.BlockSpec((BLK, kvw), cur),
                  pl.BlockSpec((BLK, kvw), prev), pl.BlockSpec((BLK, kvw), cur),
                  pl.BlockSpec(memory_space=pltpu.SMEM)],
        out_specs=[pl.BlockSpec((BLK, 1024), cur), pl.BlockSpec((BLK, LANES), cur)],
        out_shape=[jax.ShapeDtypeStruct((S, 1024), F32), jax.ShapeDtypeStruct((S, LANES), F32)],
        compiler_params=_params(("parallel",), 16 * BLK * 1024 * 4),
    )(qn, kdup, kdup, vdup, vdup, sinks)


def swa_attn_bwd(qn, kdup, vdup, sinks, o, lse, do, *, name):
    S = qn.shape[0]
    nq = S // BLK
    group = SWA_HEADS // SWA_KV
    scale = 1.0 / math.sqrt(HEAD64)

    def body(q_ref, kp_ref, kc_ref, vp_ref, vc_ref, sink_ref, o_ref, lse_ref, do_ref,
             dq_ref, dk_ref, dv_ref, ds_ref):
        i = pl.program_id(0)

        @pl.when(i == 0)
        def _():
            dk_ref[...] = jnp.zeros_like(dk_ref)
            dv_ref[...] = jnp.zeros_like(dv_ref)
            ds_ref[...] = jnp.zeros_like(ds_ref)

        valid, rel = _swa_geometry(i)
        first = _first_half((BLK, LANES))
        lane1 = lax.broadcasted_iota(I32, (1, LANES), 1)
        lane = lax.broadcasted_iota(I32, (BLK, LANES), 1)
        lse_all = lse_ref[...]
        prow = pl.ds(pl.multiple_of(jnp.maximum(i - 1, 0) * BLK, BLK), BLK)
        crow = pl.ds(pl.multiple_of(i * BLK, BLK), BLK)
        dsink = jnp.zeros((1, LANES), F32)
        zero = jnp.zeros((BLK, LANES), BF16)
        H, T = range(SWA_HEADS), range(SWA_HEADS // 2)
        kks = [jnp.concatenate([kp_ref[:, g * LANES:(g + 1) * LANES], kc_ref[:, g * LANES:(g + 1) * LANES]], axis=0)
               for g in range(SWA_KV)]
        vvs = [jnp.concatenate([vp_ref[:, g * LANES:(g + 1) * LANES], vc_ref[:, g * LANES:(g + 1) * LANES]], axis=0)
               for g in range(SWA_KV)]
        q2s = [q_ref[:, t * LANES:(t + 1) * LANES] for t in T]
        do2s = [do_ref[:, t * LANES:(t + 1) * LANES] for t in T]
        prods = [do2s[t] * o_ref[:, t * LANES:(t + 1) * LANES] for t in T]
        mine = [first if h % 2 == 0 else ~first for h in H]
        qhs = [jnp.where(mine[h], q2s[h // 2], zero) for h in H]
        dohs = [jnp.where(mine[h], do2s[h // 2], 0.0).astype(BF16) for h in H]
        deltas = [jnp.sum(jnp.where(mine[h], prods[h // 2], 0.0), axis=1, keepdims=True) for h in H]
        lses = [jnp.sum(jnp.where(lane == h, lse_all, 0.0), axis=1, keepdims=True) for h in H]
        ss = [_dot(qhs[h], kks[h // group], NT) * scale - _swa_slope(h) * rel for h in H]
        dps = [_dot(dohs[h], vvs[h // group], NT) for h in H]
        ps = [jnp.where(valid, jnp.exp(ss[h] - lses[h]), 0.0) for h in H]
        dscs = [(ps[h] * (dps[h] - deltas[h]) * scale).astype(BF16) for h in H]
        dqs = [_dot(dscs[h], kks[h // group]) for h in H]
        dks = [_dot(dscs[h], qhs[h], TN) for h in H]
        dvs = [_dot(ps[h].astype(BF16), dohs[h], TN) for h in H]
        for h in H:
            psink = jnp.exp(sink_ref[h] - lses[h])
            dsink = dsink + jnp.where(lane1 == h, -jnp.sum(psink * deltas[h]), 0.0)
        for t in T:
            dq_ref[:, t * LANES:(t + 1) * LANES] = jnp.where(first, dqs[2 * t], dqs[2 * t + 1])
        for g in range(SWA_KV):
            cols = slice(g * LANES, (g + 1) * LANES)
            b = g * group
            dkk = (dks[b] + dks[b + 1]) + (dks[b + 2] + dks[b + 3])
            dvv = (dvs[b] + dvs[b + 1]) + (dvs[b + 2] + dvs[b + 3])
            dk_ref[prow, cols] += dkk[:BLK]
            dv_ref[prow, cols] += dvv[:BLK]
            dk_ref[crow, cols] += dkk[BLK:]
            dv_ref[crow, cols] += dvv[BLK:]
        ds_ref[...] += dsink

    prev = lambda i: (jnp.maximum(i - 1, 0), 0)
    cur = lambda i: (i, 0)
    kvw = SWA_KV * LANES
    whole = pl.BlockSpec((S, kvw), lambda i: (0, 0))
    return _pallas(
        body, name=name, grid=(nq,),
        in_specs=[pl.BlockSpec((BLK, 1024), cur), pl.BlockSpec((BLK, kvw), prev), pl.BlockSpec((BLK, kvw), cur),
                  pl.BlockSpec((BLK, kvw), prev), pl.BlockSpec((BLK, kvw), cur),
                  pl.BlockSpec(memory_space=pltpu.SMEM),
                  pl.BlockSpec((BLK, 1024), cur), pl.BlockSpec((BLK, LANES), cur), pl.BlockSpec((BLK, 1024), cur)],
        out_specs=[pl.BlockSpec((BLK, 1024), cur), whole, whole, pl.BlockSpec((1, LANES), lambda i: (0, 0))],
        out_shape=[jax.ShapeDtypeStruct((S, 1024), F32), jax.ShapeDtypeStruct((S, kvw), F32),
                   jax.ShapeDtypeStruct((S, kvw), F32), jax.ShapeDtypeStruct((1, LANES), F32)],
        compiler_params=_params(("arbitrary",), 4 * S * kvw * 4 + 24 * BLK * 1024 * 4),
    )(qn, kdup, kdup, vdup, vdup, sinks, o, lse, do)


def _in_bwd(x, h, dproj, dy, g, w_in, tag):
    dh = matmul(dproj, w_in, "nt", name=f"{tag}_dh")
    dw_in = matmul(h, dproj, "tn", name=f"{tag}_dwin")
    dx, dg = rmsnorm_bwd(x, g, dh, col_off=0, width=D_MODEL, out_dtype=F32, res=dy, name=f"{tag}_dnorm")
    return dx, dw_in, dg


def _out_bwd(dy, og, o, proj, w_out, gate_off, tag):
    dog = matmul(dy, w_out, "nt", name=f"{tag}_dog")
    dw_out = matmul(og, dy, "tn", name=f"{tag}_dwout")
    do, dgate = gate_bwd(dog, o, proj, gate_off=gate_off, name=f"{tag}_dgate")
    return do, dgate, dw_out


def sb_fwd(x, p, tag, ride=None):
    h = rmsnorm_fwd(x, p["norm"], col_off=0, width=D_MODEL, out_dtype=BF16, name=f"{tag}_norm")
    qkv = matmul(h, p["w_in"], "nn", out_dtype=BF16, b_cols=(0, 3 * SB_W), name=f"{tag}_proj")
    gate = matmul(h, p["w_in"], "nn", b_cols=(3 * SB_W, SB_W), name=f"{tag}_gproj")
    o = sb_attn_fwd(qkv, name=f"{tag}_attn", ride=ride)
    if ride is not None:
        o, gathered = o
    og = gate_fwd(o, gate, gate_off=0, name=f"{tag}_gate")
    y = matmul(og, p["w_out"], "nn", add=x, name=f"{tag}_out")
    saved = (x, h, gate, qkv, o, og)
    return (y, saved) if ride is None else (y, saved, gathered)


def sb_bwd(dy, saved, p, tag):
    x, h, gate, qkv, o, og = saved
    do, dgate, dw_out = _out_bwd(dy, og, o, gate, p["w_out"], 0, tag)
    dq, dk, dv = sb_attn_bwd(qkv, o, do, name=f"{tag}_dattn")
    dproj = jnp.concatenate([dq, dk.astype(BF16), dv.astype(BF16), dgate], axis=1)
    dx, dw_in, dg = _in_bwd(x, h, dproj, dy, p["norm"], p["w_in"], tag)
    return dx, {"norm": dg[0], "w_in": dw_in, "w_out": dw_out}


MLA_GATE, MLA_QLAT, MLA_KVLAT, MLA_KPE, MLA_IN = 0, 1024, 1280, 1408, 1536


def mla_fwd(x, p, tabs, tag):
    cos, sgn = tabs
    h = rmsnorm_fwd(x, p["norm"], col_off=0, width=D_MODEL, out_dtype=BF16, name=f"{tag}_norm")
    proj = matmul(h, p["w_in"], "nn", name=f"{tag}_proj")
    ql = rmsnorm_fwd(proj, p["q_a_norm"], col_off=MLA_QLAT, width=256, out_dtype=BF16, name=f"{tag}_qanorm")
    kvl = rmsnorm_fwd(proj, p["kv_a_norm"], col_off=MLA_KVLAT, width=128, out_dtype=BF16, name=f"{tag}_kvanorm")
    qraw = matmul(ql, p["w_uq"], "nn", name=f"{tag}_uq")
    kv = matmul(kvl, p["w_ukv"], "nn", name=f"{tag}_ukv")
    qn, kn, vb = mla_prep_fwd(qraw, kv, proj, p["gq"], p["gk"], cos, sgn, kpe_off=MLA_KPE, name=f"{tag}_prep")
    o, lse = flash_fwd(qn, kn, vb, name=f"{tag}_attn")
    og = gate_fwd(o, proj, gate_off=MLA_GATE, name=f"{tag}_gate")
    y = matmul(og, p["w_out"], "nn", add=x, name=f"{tag}_out")
    return y, (x, h, proj, ql, kvl, qraw, kv, qn, kn, vb, o, lse, og)


def mla_bwd(dy, saved, p, tabs, tag):
    cos, sgn = tabs
    x, h, proj, ql, kvl, qraw, kv, qn, kn, vb, o, lse, og = saved
    do, dgate, dw_out = _out_bwd(dy, og, o, proj, p["w_out"], MLA_GATE, tag)
    dqn, dkn, dv = flash_bwd(qn, kn, vb, o, lse, do, name=f"{tag}_dattn")
    dqraw, dkv, dkpe, dgq, dgk = mla_prep_bwd(dqn, dkn, dv, qraw, kv, proj, p["gq"], p["gk"], cos, sgn,
                                              kpe_off=MLA_KPE, name=f"{tag}_dprep")
    dql = matmul(dqraw, p["w_uq"], "nt", name=f"{tag}_dql")
    dw_uq = matmul(ql, dqraw, "tn", name=f"{tag}_dwuq")
    dkvl = matmul(dkv, p["w_ukv"], "nt", name=f"{tag}_dkvl")
    dw_ukv = matmul(kvl, dkv, "tn", name=f"{tag}_dwukv")
    dqlat, dgqa = rmsnorm_bwd(proj, p["q_a_norm"], dql, col_off=MLA_QLAT, width=256, out_dtype=BF16, name=f"{tag}_dqanorm")
    dkvlat, dgkva = rmsnorm_bwd(proj, p["kv_a_norm"], dkvl, col_off=MLA_KVLAT, width=128, out_dtype=BF16,
                                name=f"{tag}_dkvanorm")
    dproj = jnp.concatenate([dgate, dqlat, dkvlat, dkpe], axis=1)
    dx, dw_in, dg = _in_bwd(x, h, dproj, dy, p["norm"], p["w_in"], tag)
    return dx, {"norm": dg[0], "w_in": dw_in, "q_a_norm": dgqa[0], "w_uq": dw_uq, "kv_a_norm": dgkva[0],
                "w_ukv": dw_ukv, "gq": dgq[0], "gk": dgk[0], "w_out": dw_out}


SWA_Q, SWA_GATE, SWA_K, SWA_V = 0, 1024, 2048, 2304


def swa_fwd(x, p, tag):
    h = rmsnorm_fwd(x, p["norm"], col_off=0, width=D_MODEL, out_dtype=BF16, name=f"{tag}_norm")
    proj = matmul(h, p["w_in"], "nn", name=f"{tag}_proj")
    qn = headnorm_fwd(proj, p["q_head_norm"], col_off=SWA_Q, width=1024, name=f"{tag}_qnorm")
    kdup, vdup = swa_kv_prep(proj, p["k_head_norm"], k_off=SWA_K, v_off=SWA_V, name=f"{tag}_kvprep")
    o, lse = swa_attn_fwd(qn, kdup, vdup, p["sinks"], name=f"{tag}_attn")
    og = gate_fwd(o, proj, gate_off=SWA_GATE, name=f"{tag}_gate")
    y = matmul(og, p["w_out"], "nn", add=x, name=f"{tag}_out")
    return y, (x, h, proj, qn, kdup, vdup, o, lse, og)


def swa_bwd(dy, saved, p, tag):
    x, h, proj, qn, kdup, vdup, o, lse, og = saved
    do, dgate, dw_out = _out_bwd(dy, og, o, proj, p["w_out"], SWA_GATE, tag)
    dqn, dkdup, dvdup, dsinks = swa_attn_bwd(qn, kdup, vdup, p["sinks"], o, lse, do, name=f"{tag}_dattn")
    dq, dgq = headnorm_bwd(proj, p["q_head_norm"], dqn, col_off=SWA_Q, width=1024, name=f"{tag}_dqnorm")
    dk, dv, dgk = swa_kv_prep_bwd(dkdup, dvdup, proj, p["k_head_norm"], k_off=SWA_K, name=f"{tag}_dkvprep")
    dproj = jnp.concatenate([dq, dgate, dk, dv], axis=1)
    dx, dw_in, dg = _in_bwd(x, h, dproj, dy, p["norm"], p["w_in"], tag)
    return dx, {"norm": dg[0], "w_in": dw_in, "q_head_norm": dgq[0, :HEAD64] + dgq[0, HEAD64:],
                "k_head_norm": dgk[0, :HEAD64] + dgk[0, HEAD64:], "sinks": dsinks[0, :SWA_HEADS], "w_out": dw_out}


def prepare_layer(w, i):
    if i in (0, 3):
        return {"norm": w[f"l{i}_norm"], "w_in": w[f"l{i}_w_in"], "w_out": w[f"l{i}_w_out"]}
    if i == 1:
        l1_in = w["l1_w_in"]
        pad64 = lambda v: jnp.pad(v, (0, MLA_PAD - MLA_QK)).reshape(1, MLA_PAD)
        return {"norm": w["l1_norm"],
                "w_in": jnp.concatenate([l1_in[:, 448:], l1_in[:, :448], jnp.zeros((D_MODEL, 64), l1_in.dtype)], axis=1),
                "q_a_norm": w["l1_q_a_norm"], "kv_a_norm": w["l1_kv_a_norm"],
                "w_uq": jnp.pad(w["l1_w_uq"].reshape(256, MLA_HEADS, MLA_QK), ((0, 0), (0, 0), (0, MLA_PAD - MLA_QK))
                                ).reshape(256, MLA_HEADS * MLA_PAD),
                "w_ukv": w["l1_w_ukv"], "gq": pad64(w["l1_q_head_norm"]), "gk": pad64(w["l1_k_head_norm"]),
                "w_out": w["l1_w_out"]}
    l2_in = w["l2_w_in"]
    return {"norm": w["l2_norm"],
            "w_in": jnp.concatenate([l2_in[:, :1024], l2_in[:, 1536:], l2_in[:, 1024:1536]], axis=1),
            "q_head_norm": w["l2_q_head_norm"], "k_head_norm": w["l2_k_head_norm"], "sinks": w["l2_sinks"],
            "w_out": w["l2_w_out"]}


def unprepare_grads(gs):
    g0, g1, g2, g3 = gs
    d1, d2 = g1["w_in"], g2["w_in"]
    return {
        "l0_norm": g0["norm"], "l0_w_in": g0["w_in"], "l0_w_out": g0["w_out"],
        "l1_norm": g1["norm"], "l1_w_in": jnp.concatenate([d1[:, 1024:1472], d1[:, :1024]], axis=1),
        "l1_q_a_norm": g1["q_a_norm"],
        "l1_w_uq": g1["w_uq"].reshape(256, MLA_HEADS, MLA_PAD)[:, :, :MLA_QK].reshape(256, MLA_HEADS * MLA_QK),
        "l1_kv_a_norm": g1["kv_a_norm"], "l1_w_ukv": g1["w_ukv"],
        "l1_q_head_norm": g1["gq"][:MLA_QK], "l1_k_head_norm": g1["gk"][:MLA_QK], "l1_w_out": g1["w_out"],
        "l2_norm": g2["norm"], "l2_w_in": jnp.concatenate([d2[:, :1024], d2[:, 2048:], d2[:, 1024:2048]], axis=1),
        "l2_q_head_norm": g2["q_head_norm"], "l2_k_head_norm": g2["k_head_norm"], "l2_sinks": g2["sinks"],
        "l2_w_out": g2["w_out"],
        "l3_norm": g3["norm"], "l3_w_in": g3["w_in"], "l3_w_out": g3["w_out"],
    }


def local_step(x, target, w, late=None):
    tabs = _rope_tables(x.shape[0])
    p0 = prepare_layer(w, 0)
    if late is None:
        y0, s0 = sb_fwd(x, p0, "l0")
    else:
        y0, s0, gathered = sb_fwd(x, p0, "l0", ride=late[0])
        w = {**w, **late[1](gathered)}
    ps = [p0] + [prepare_layer(w, i) for i in (1, 2, 3)]
    y1, s1 = mla_fwd(y0, ps[1], tabs, "l1")
    y2, s2 = swa_fwd(y1, ps[2], "l2")
    y3, s3 = sb_fwd(y2, ps[3], "l3")
    dy, loss = loss_head(y3, target, name="loss")
    d3, g3 = sb_bwd(dy, s3, ps[3], "l3")
    d2, g2 = swa_bwd(d3, s2, ps[2], "l2")
    d1, g1 = mla_bwd(d2, s1, ps[1], tabs, "l1")
    d0, g0 = sb_bwd(d1, s0, ps[0], "l0")
    return loss, d0, unprepare_grads([g0, g1, g2, g3])


MATS = (("l0_w_in", "col", 1024, 4096), ("l0_w_out", "row", 1024, 1024), ("l1_w_in", "col", 1024, 1472),
        ("l1_w_uq", "col", 256, 1536), ("l1_w_ukv", "col", 128, 2048), ("l1_w_out", "row", 1024, 1024),
        ("l2_w_in", "col", 1024, 2560), ("l2_w_out", "row", 1024, 1024), ("l3_w_in", "col", 1024, 4096),
        ("l3_w_out", "row", 1024, 1024))
FIRST_MATS = 2
N_CHIPS = 4
PACK_W = 1024
HALF_ROWS = 2176
PACK_ROWS = 2 * HALF_ROWS
VECS = (("l0_norm", 0, 0, 1024), ("l1_norm", 1, 0, 1024), ("l2_norm", 2, 0, 1024), ("l3_norm", 3, 0, 1024),
        ("l1_q_a_norm", 4, 0, 256), ("l1_kv_a_norm", 4, 256, 128), ("l1_q_head_norm", 4, 384, 192),
        ("l1_k_head_norm", 4, 576, 192), ("l2_q_head_norm", 4, 768, 64), ("l2_k_head_norm", 4, 832, 64),
        ("l2_sinks", 4, 896, 16))
LOSS_SLOT = (4, 912)
VEC_ROWS = 8


def _shard_rows(k, n):
    return k * n // N_CHIPS // PACK_W


def pack_shards(shards):
    parts = [shards[name].reshape(-1, PACK_W) for name, _, _, _ in MATS]
    used = sum(p.shape[0] for p in parts)
    return jnp.concatenate(parts + [jnp.zeros((PACK_ROWS - used, PACK_W), parts[0].dtype)], axis=0)


def unpack_shards(flat):
    out, r0 = {}, 0
    for name, kind, k, n in MATS:
        rows = _shard_rows(k, n)
        shape = (k, n // N_CHIPS) if kind == "col" else (k // N_CHIPS, n)
        out[name] = flat[r0:r0 + rows].reshape(shape)
        r0 += rows
    return out


def pack_full(mats):
    parts = []
    for name, kind, k, n in MATS:
        m = mats[name]
        if kind == "col":
            m = m.reshape(k, N_CHIPS, n // N_CHIPS).transpose(1, 0, 2)
        parts.append(m.reshape(N_CHIPS, -1, PACK_W))
    used = sum(p.shape[1] for p in parts)
    return jnp.concatenate(parts + [jnp.zeros((N_CHIPS, PACK_ROWS - used, PACK_W), parts[0].dtype)], axis=1)


def unpack_full(stacked, mats=MATS):
    out, r0 = {}, 0
    for name, kind, k, n in mats:
        rows = _shard_rows(k, n)
        seg = stacked[:, r0:r0 + rows]
        if kind == "col":
            out[name] = seg.reshape(N_CHIPS, k, n // N_CHIPS).transpose(1, 0, 2).reshape(k, n)
        else:
            out[name] = seg.reshape(k, n)
        r0 += rows
    return out


def pack_vecs(vecs, loss=None):
    rows = []
    for r in range(VEC_ROWS):
        items = [(off, vecs[name]) for name, rr, off, _ in VECS if rr == r]
        if loss is not None and r == LOSS_SLOT[0]:
            items.append((LOSS_SLOT[1], loss.reshape(1)))
        pos, parts = 0, []
        for off, v in sorted(items, key=lambda t: t[0]):
            assert off == pos
            parts.append(v.astype(F32))
            pos += v.shape[0]
        parts.append(jnp.zeros((PACK_W - pos,), F32))
        rows.append(jnp.concatenate(parts))
    return jnp.stack(rows)


def unpack_vecs(block):
    return {name: block[r, off:off + n] for name, r, off, n in VECS}


def _me():
    return lax.axis_index("x"), lax.axis_index("y"), lax.axis_index("c")


OTHER_CHIPS = ((1, 0), (0, 1), (1, 1))


def _remote(src, dst, send_sem, recv_sem, to):
    return pltpu.make_async_remote_copy(src_ref=src, dst_ref=dst, send_sem=send_sem, recv_sem=recv_sem,
                                        device_id=to, device_id_type=MESH)


def gather_weights(block):
    def body(in_ref, out_ref, send_sems, recv_sems):
        send, forward, finish = _gather_steps(in_ref, out_ref, send_sems, recv_sems)
        send()
        forward()
        finish()

    hbm = pl.BlockSpec(memory_space=pltpu.HBM)
    others = pl.pallas_call(
        body, name="gather_weights",
        out_shape=jax.ShapeDtypeStruct((N_CHIPS,) + block.shape, block.dtype),
        in_specs=[hbm], out_specs=hbm,
        scratch_shapes=list(GATHER_SEMS),
    )(block)
    return place_own_block(others, block)


GATHER_SEMS = (pltpu.SemaphoreType.DMA((6,)), pltpu.SemaphoreType.DMA((6,)))


def place_own_block(others, block):
    return lax.dynamic_update_slice(others, block[None], (2 * lax.axis_index("x") + lax.axis_index("y"), 0, 0))


def _gather_steps(in_ref, out_ref, send_sems, recv_sems):
    hr = in_ref.shape[0] // 2
    x, y, c = _me()
    sibling = (x, y, 1 - c)
    chips = [(x ^ dx, y ^ dy) for dx, dy in OTHER_CHIPS]

    def half(px, py, pc):
        return out_ref.at[2 * px + py, pl.ds(pc * hr, hr), :]

    def firsts():
        return [_remote(in_ref.at[pl.ds(c * hr, hr), :], half(x, y, c), send_sems.at[j], recv_sems.at[j], (*chip, c))
                for j, chip in enumerate(chips)]

    def passes():
        return [_remote(half(*chip, c), half(*chip, c), send_sems.at[3 + j], recv_sems.at[3 + j], sibling)
                for j, chip in enumerate(chips)]

    def send():
        for cp in firsts():
            cp.start()

    def forward():
        for j, (chip, cp) in enumerate(zip(chips, passes())):
            _remote(half(*chip, c), half(*chip, c), send_sems.at[j], recv_sems.at[j], (*chip, c)).wait_recv()
            cp.start()

    def finish():
        for j, chip in enumerate(chips):
            _remote(half(*chip, 1 - c), half(*chip, 1 - c), send_sems.at[3 + j], recv_sems.at[3 + j], sibling).wait_recv()
        for cp in firsts() + passes():
            cp.wait_send()

    return send, forward, finish


def pair_exchange(g):
    def body(g_ref, out_ref, send_sems, recv_sems):
        x, y, c = _me()
        sibling = (x, y, 1 - c)
        copies = [_remote(g_ref.at[k, pl.ds((1 - c) * HALF_ROWS, HALF_ROWS), :], out_ref.at[k], send_sems.at[k],
                          recv_sems.at[k], sibling) for k in range(N_CHIPS)]
        for cp in copies:
            cp.start()
        for cp in copies:
            cp.wait_recv()
        for cp in copies:
            cp.wait_send()

    hbm = pl.BlockSpec(memory_space=pltpu.HBM)
    return pl.pallas_call(
        body, name="pair_exchange",
        out_shape=jax.ShapeDtypeStruct((N_CHIPS, HALF_ROWS, PACK_W), g.dtype),
        in_specs=[hbm], out_specs=hbm,
        scratch_shapes=[pltpu.SemaphoreType.DMA((N_CHIPS,)), pltpu.SemaphoreType.DMA((N_CHIPS,))],
    )(g)


def chip_exchange(part):
    def body(p_ref, out_ref, send_sems, recv_sems):
        x, y, c = _me()
        copies = [_remote(p_ref.at[2 * (x ^ dx) + (y ^ dy)], out_ref.at[j], send_sems.at[j], recv_sems.at[j],
                          (x ^ dx, y ^ dy, c)) for j, (dx, dy) in enumerate(OTHER_CHIPS)]
        for cp in copies:
            cp.start()
        for cp in copies:
            cp.wait_recv()
        for cp in copies:
            cp.wait_send()

    hbm = pl.BlockSpec(memory_space=pltpu.HBM)
    return pl.pallas_call(
        body, name="chip_exchange",
        out_shape=jax.ShapeDtypeStruct((len(OTHER_CHIPS), HALF_ROWS, PACK_W), part.dtype),
        in_specs=[hbm], out_specs=hbm,
        scratch_shapes=[pltpu.SemaphoreType.DMA((3,)), pltpu.SemaphoreType.DMA((3,))],
    )(part)


def join_halves(half):
    def body(h_ref, out_ref, send_sem, recv_sem):
        x, y, c = _me()
        cp = _remote(h_ref, out_ref.at[pl.ds(c * HALF_ROWS, HALF_ROWS), :], send_sem, recv_sem, (x, y, 1 - c))
        cp.start()
        _remote(h_ref, out_ref.at[pl.ds((1 - c) * HALF_ROWS, HALF_ROWS), :], send_sem, recv_sem, (x, y, 1 - c)).wait_recv()
        cp.wait_send()

    hbm = pl.BlockSpec(memory_space=pltpu.HBM)
    other = pl.pallas_call(
        body, name="join_halves",
        out_shape=jax.ShapeDtypeStruct((PACK_ROWS, PACK_W), half.dtype),
        in_specs=[hbm], out_specs=hbm,
        scratch_shapes=[pltpu.SemaphoreType.DMA, pltpu.SemaphoreType.DMA],
    )(half)
    return lax.dynamic_update_slice(other, half, (lax.axis_index("c") * HALF_ROWS, 0))


def sum_over_devices(block):
    def body(in_ref, out_ref, all_ref, send_sems, recv_sems):
        x, y, c = _me()
        me = 4 * x + 2 * y + c
        all_ref[me] = in_ref[...]
        copies = []
        for r in range(1, 8):
            to = (x ^ (r >> 2), y ^ ((r >> 1) & 1), c ^ (r & 1))
            copies.append(_remote(in_ref, all_ref.at[me], send_sems.at[r - 1], recv_sems.at[r - 1], to))
        for cp in copies:
            cp.start()
        for r in range(1, 8):
            frm = (x ^ (r >> 2), y ^ ((r >> 1) & 1), c ^ (r & 1))
            _remote(in_ref, all_ref.at[4 * frm[0] + 2 * frm[1] + frm[2]], send_sems.at[r - 1], recv_sems.at[r - 1],
                    frm).wait_recv()
        for cp in copies:
            cp.wait_send()
        acc = all_ref[0]
        for d in range(1, 8):
            acc = acc + all_ref[d]
        out_ref[...] = acc

    vmem = pl.BlockSpec(memory_space=pltpu.VMEM)
    return pl.pallas_call(
        body, name="sum_over_devices",
        out_shape=jax.ShapeDtypeStruct(block.shape, F32),
        in_specs=[vmem], out_specs=vmem,
        scratch_shapes=[pltpu.VMEM((8,) + block.shape, F32), pltpu.SemaphoreType.DMA((7,)), pltpu.SemaphoreType.DMA((7,))],
    )(block)


SUM_ROWS = 272


def pair_sum(g, got, core):
    steps = HALF_ROWS // SUM_ROWS

    def body(s_ref, g_ref, r_ref, o_ref, ob_ref):
        t = g_ref[...] + r_ref[...]
        o_ref[...] = t
        ob_ref[...] = t.astype(BF16)

    blk = pl.BlockSpec((1, SUM_ROWS, PACK_W), lambda k, i, s: (k, i, 0))
    return pl.pallas_call(
        body, name="pair_sum",
        grid_spec=pltpu.PrefetchScalarGridSpec(
            num_scalar_prefetch=1, grid=(N_CHIPS, steps),
            in_specs=[pl.BlockSpec((1, SUM_ROWS, PACK_W), lambda k, i, s: (k, s[0] * steps + i, 0)), blk],
            out_specs=[blk, blk]),
        out_shape=[jax.ShapeDtypeStruct((N_CHIPS, HALF_ROWS, PACK_W), F32),
                   jax.ShapeDtypeStruct((N_CHIPS, HALF_ROWS, PACK_W), BF16)],
        compiler_params=_params(("parallel", "parallel"), 8 * SUM_ROWS * PACK_W * 4),
    )(core, g, got)


def chip_sum(part, got, chip):
    steps = HALF_ROWS // SUM_ROWS

    def body(s_ref, p_ref, a_ref, b_ref, c_ref, o_ref):
        o_ref[...] = ((p_ref[0] + a_ref[0].astype(F32)) + b_ref[0].astype(F32)) + c_ref[0].astype(F32)

    def got_spec(j):
        return pl.BlockSpec((1, SUM_ROWS, PACK_W), lambda i, s: (j, i, 0))

    return pl.pallas_call(
        body, name="chip_sum",
        grid_spec=pltpu.PrefetchScalarGridSpec(
            num_scalar_prefetch=1, grid=(steps,),
            in_specs=[pl.BlockSpec((1, SUM_ROWS, PACK_W), lambda i, s: (s[0], i, 0)), got_spec(0), got_spec(1), got_spec(2)],
            out_specs=pl.BlockSpec((SUM_ROWS, PACK_W), lambda i, s: (i, 0))),
        out_shape=jax.ShapeDtypeStruct((HALF_ROWS, PACK_W), F32),
        compiler_params=_params(("parallel",), 10 * SUM_ROWS * PACK_W * 4),
    )(chip, part, got, got, got)


def reduce_scatter(g):
    x, y, c = _me()
    part, part_bf16 = pair_sum(g, pair_exchange(g), jnp.reshape(c, (1,)).astype(I32))
    half = chip_sum(part, chip_exchange(part_bf16), jnp.reshape(2 * x + y, (1,)).astype(I32))
    return join_halves(half)


def adamw(w, g, m, v, *, name):
    rows, cols = w.shape
    tm = _tile(rows, 256)
    c1 = 1.0 - ADAM_B1 ** ADAM_STEP
    c2 = 1.0 - ADAM_B2 ** ADAM_STEP

    def body(w_ref, g_ref, m_ref, v_ref, d_ref, mo_ref, vo_ref):
        gv = g_ref[...]
        mn = ADAM_B1 * m_ref[...] + (1.0 - ADAM_B1) * gv
        vn = ADAM_B2 * v_ref[...] + (1.0 - ADAM_B2) * (gv * gv)
        d_ref[...] = -ADAM_LR * ((mn / c1) / (jnp.sqrt(vn / c2) + ADAM_EPS) + ADAM_WD * w_ref[...])
        mo_ref[...] = mn
        vo_ref[...] = vn

    blk = pl.BlockSpec((tm, cols), lambda i: (i, 0))
    shape = jax.ShapeDtypeStruct(w.shape, F32)
    return _pallas(
        body, name=name, grid=(rows // tm,),
        in_specs=[blk] * 4, out_specs=[blk] * 3, out_shape=[shape] * 3,
        compiler_params=_params(("parallel",), 16 * tm * cols * 4),
    )(w, g, m, v)


WEIGHTS = ("l0_norm", "l0_w_in", "l0_w_out", "l1_norm", "l1_w_in", "l1_q_a_norm", "l1_w_uq", "l1_kv_a_norm", "l1_w_ukv",
           "l1_q_head_norm", "l1_k_head_norm", "l1_w_out", "l2_norm", "l2_w_in", "l2_q_head_norm", "l2_k_head_norm",
           "l2_sinks", "l2_w_out", "l3_norm", "l3_w_in", "l3_w_out")


def kernel(x, l0_norm, l0_w_in, l0_w_out, l1_norm, l1_w_in, l1_q_a_norm, l1_w_uq, l1_kv_a_norm, l1_w_ukv, l1_q_head_norm, l1_k_head_norm, l1_w_out, l2_norm, l2_w_in, l2_q_head_norm, l2_k_head_norm, l2_sinks, l2_w_out, l3_norm, l3_w_in, l3_w_out, loss_target, m_l0_norm, m_l0_w_in, m_l0_w_out, m_l1_norm, m_l1_w_in, m_l1_q_a_norm, m_l1_w_uq, m_l1_kv_a_norm, m_l1_w_ukv, m_l1_q_head_norm, m_l1_k_head_norm, m_l1_w_out, m_l2_norm, m_l2_w_in, m_l2_q_head_norm, m_l2_k_head_norm, m_l2_sinks, m_l2_w_out, m_l3_norm, m_l3_w_in, m_l3_w_out, v_l0_norm, v_l0_w_in, v_l0_w_out, v_l1_norm, v_l1_w_in, v_l1_q_a_norm, v_l1_w_uq, v_l1_kv_a_norm, v_l1_w_ukv, v_l1_q_head_norm, v_l1_k_head_norm, v_l1_w_out, v_l2_norm, v_l2_w_in, v_l2_q_head_norm, v_l2_k_head_norm, v_l2_sinks, v_l2_w_out, v_l3_norm, v_l3_w_in, v_l3_w_out):
    given = dict(locals())
    w = {n: given[n] for n in WEIGHTS}
    m = {n: given["m_" + n] for n in WEIGHTS}
    v = {n: given["v_" + n] for n in WEIGHTS}
    mat_names = [t[0] for t in MATS]
    vec_names = [t[0] for t in VECS]

    packed = pack_shards({n: w[n] for n in mat_names}).astype(BF16)
    first = sum(_shard_rows(k, n) for _, _, k, n in MATS[:FIRST_MATS])
    full = unpack_full(gather_weights(packed[:first]), MATS[:FIRST_MATS])
    full.update({n: w[n] for n in vec_names})
    late = (packed[first:], lambda gathered: unpack_full(gathered, MATS[FIRST_MATS:]))
    loss_tile, grad_x, grads = local_step(x[0], loss_target[0], full, late)

    block = reduce_scatter(pack_full({n: grads[n] for n in mat_names}))
    g = unpack_shards(block)
    vec_sum = sum_over_devices(pack_vecs({n: grads[n] for n in vec_names}, loss=loss_tile[0, 0]))
    loss = vec_sum[LOSS_SLOT[0], LOSS_SLOT[1]]

    delta, new_m, new_v = {}, {}, {}
    for n in mat_names:
        delta[n], new_m[n], new_v[n] = adamw(w[n], g[n], m[n], v[n], name=f"adamw_{n}")
    dv, mv, vv = adamw(pack_vecs(w), vec_sum, pack_vecs(m), pack_vecs(v), name="adamw_vecs")
    g.update(unpack_vecs(vec_sum))
    delta.update(unpack_vecs(dv))
    new_m.update(unpack_vecs(mv))
    new_v.update(unpack_vecs(vv))
    return (loss, grad_x[None], *[g[n] for n in WEIGHTS], *[delta[n] for n in WEIGHTS],
            *[new_m[n] for n in WEIGHTS], *[new_v[n] for n in WEIGHTS])
```

```python
import math

import jax
import jax.numpy as jnp
from jax import lax
from jax.experimental import pallas as pl
from jax.experimental.pallas import tpu as pltpu

F32 = jnp.float32
BF16 = jnp.bfloat16
I32 = jnp.int32
MESH = pl.DeviceIdType.MESH

NORM_EPS = 1e-6
D_MODEL = 1024
HEAD64 = 64
LANES = 128
BLK = 128
MLA_HEADS = 8
MLA_QK = 192
MLA_PAD = 256
ROPE_THETA = 10000.0
SWA_HEADS = 16
SWA_KV = 4
VMEM_CAP = 56 * 1024 * 1024
MATMUL_TILE_BYTES = 8 * 1024 * 1024

ADAM_LR, ADAM_B1, ADAM_B2, ADAM_EPS, ADAM_WD, ADAM_STEP = 0.001, 0.9, 0.999, 1e-08, 0.01, 10

NT = (((1,), (1,)), ((), ()))
NN = (((1,), (0,)), ((), ()))
TN = (((0,), (0,)), ((), ()))


def _dot(a, b, dims=NN):
    return lax.dot_general(a, b, dims, preferred_element_type=F32)


def _tile(n, pref):
    for t in (pref, 512, 256, 128):
        if t <= pref and n % t == 0:
            return t
    return n


def _params(sem, vmem_bytes):
    limit = int(min(max(2 * vmem_bytes, 24 * 1024 * 1024), VMEM_CAP))
    return pltpu.CompilerParams(dimension_semantics=sem, vmem_limit_bytes=limit)


def _in_hbm(s):
    return pltpu.HBM(s.shape, s.dtype) if len(s.shape) >= 2 else s


def _pallas(*args, out_shape, **kwargs):
    out_shape = [_in_hbm(s) for s in out_shape] if isinstance(out_shape, (list, tuple)) else _in_hbm(out_shape)
    call = pl.pallas_call(*args, out_shape=out_shape, **kwargs)

    def run(*operands):
        return call(*[pltpu.with_memory_space_constraint(a, pltpu.HBM) if a.ndim >= 2 else a for a in operands])

    return run


def _split(v):
    hi = v.astype(BF16)
    return hi, (v - hi.astype(F32)).astype(BF16)


def _dot2(v, m):
    hi, lo = _split(v)
    return _dot(hi, m) + _dot(lo, m)


def _dot_split(v, m2):
    return _dot(jnp.concatenate(_split(v), axis=1), m2)


def _first_half(shape):
    return lax.broadcasted_iota(I32, shape, 1) < HEAD64


def _sigmoid(g):
    return 1.0 / (1.0 + jnp.exp(-g))


def matmul(a, b, mode, *, name, out_dtype=F32, add=None, b_cols=None, tm=512, tn=1024):
    if mode == "nn":
        (M, K), (K2, N) = a.shape, b.shape
        if b_cols is not None:
            N = b_cols[1]
    elif mode == "nt":
        (M, K), (N, K2) = a.shape, b.shape
    else:
        (K, M), (K2, N) = a.shape, b.shape
    assert K == K2, (a.shape, b.shape, mode)
    tm, tn = _tile(M, tm), _tile(N, tn)
    while K * tm * a.dtype.itemsize > MATMUL_TILE_BYTES:
        tm //= 2
    while K * tn * b.dtype.itemsize > MATMUL_TILE_BYTES:
        tn //= 2
    dims = {"nn": NN, "nt": NT, "tn": TN}[mode]
    n_in = 2 if add is None else 3

    def body(*refs):
        a_ref, b_ref = refs[:2]
        r = _dot(a_ref[...].astype(BF16), b_ref[...].astype(BF16), dims)
        if add is not None:
            r = r + refs[2][...]
        for o_ref in refs[n_in:]:
            o_ref[...] = r.astype(o_ref.dtype)

    a_spec = pl.BlockSpec((K, tm), lambda i, j: (0, i)) if mode == "tn" else pl.BlockSpec((tm, K), lambda i, j: (i, 0))
    jb = 0 if b_cols is None else b_cols[0] // tn
    assert b_cols is None or (mode == "nn" and b_cols[0] % tn == 0)
    b_spec = pl.BlockSpec((tn, K), lambda i, j: (j, 0)) if mode == "nt" else pl.BlockSpec((K, tn), lambda i, j: (0, jb + j))
    o_spec = pl.BlockSpec((tm, tn), lambda i, j: (i, j))
    in_specs, args = [a_spec, b_spec], [a, b]
    if add is not None:
        in_specs.append(o_spec)
        args.append(add)
    vm = 2 * (tm * K * a.dtype.itemsize + K * tn * b.dtype.itemsize) + 5 * tm * tn * 4
    out_dtypes = list(out_dtype) if isinstance(out_dtype, (tuple, list)) else [out_dtype]
    res = _pallas(
        body, name=name, grid=(M // tm, N // tn),
        in_specs=in_specs, out_specs=[o_spec] * len(out_dtypes),
        out_shape=[jax.ShapeDtypeStruct((M, N), d) for d in out_dtypes],
        compiler_params=_params(("parallel", "parallel"), vm),
    )(*args)
    return res[0] if len(out_dtypes) == 1 else res


def rmsnorm_fwd(x, g, *, col_off, width, out_dtype, name, tm=256):
    S = x.shape[0]
    assert col_off % width == 0
    cb = col_off // width
    tm = _tile(S, tm)

    def body(x_ref, g_ref, o_ref):
        v = x_ref[...]
        r = lax.rsqrt(jnp.mean(v * v, axis=1, keepdims=True) + NORM_EPS)
        o_ref[...] = (v * r * g_ref[...]).astype(out_dtype)

    return _pallas(
        body, name=name, grid=(S // tm,),
        in_specs=[pl.BlockSpec((tm, width), lambda i: (i, cb)), pl.BlockSpec((1, width), lambda i: (0, 0))],
        out_specs=pl.BlockSpec((tm, width), lambda i: (i, 0)),
        out_shape=jax.ShapeDtypeStruct((S, width), out_dtype),
        compiler_params=_params(("parallel",), 4 * tm * width * 4),
    )(x, g.reshape(1, width))


def rmsnorm_bwd(x, g, dh, *, col_off, width, out_dtype, name, res=None, tm=256):
    S = x.shape[0]
    cb = col_off // width
    tm = _tile(S, tm)

    def body(*refs):
        if res is None:
            x_ref, g_ref, dh_ref, dx_ref, dg_ref = refs
        else:
            x_ref, g_ref, dh_ref, res_ref, dx_ref, dg_ref = refs

        @pl.when(pl.program_id(0) == 0)
        def _():
            dg_ref[...] = jnp.zeros_like(dg_ref)

        v = x_ref[...]
        dhv = dh_ref[...].astype(F32)
        r = lax.rsqrt(jnp.mean(v * v, axis=1, keepdims=True) + NORM_EPS)
        y = v * r
        dy = dhv * g_ref[...]
        dx = r * (dy - y * jnp.mean(dy * y, axis=1, keepdims=True))
        if res is not None:
            dx = dx + res_ref[...]
        dx_ref[...] = dx.astype(out_dtype)
        dg_ref[...] += jnp.sum(dhv * y, axis=0, keepdims=True)

    row = pl.BlockSpec((tm, width), lambda i: (i, 0))
    in_specs = [pl.BlockSpec((tm, width), lambda i: (i, cb)), pl.BlockSpec((1, width), lambda i: (0, 0)), row]
    args = [x, g.reshape(1, width), dh]
    if res is not None:
        in_specs.append(row)
        args.append(res)
    return _pallas(
        body, name=name, grid=(S // tm,),
        in_specs=in_specs,
        out_specs=[row, pl.BlockSpec((1, width), lambda i: (0, 0))],
        out_shape=[jax.ShapeDtypeStruct((S, width), out_dtype), jax.ShapeDtypeStruct((1, width), F32)],
        compiler_params=_params(("arbitrary",), 8 * tm * width * 4),
    )(*args)


def _group_ones():
    r = lax.broadcasted_iota(I32, (LANES, LANES), 0) // HEAD64
    c = lax.broadcasted_iota(I32, (LANES, LANES), 1) // HEAD64
    return (r == c).astype(BF16)


def _segmean64(v):
    return _dot2(v, _group_ones()) * (1.0 / HEAD64)


def headnorm_fwd(x, g64, *, col_off, width, name, tm=512):
    S = x.shape[0]
    cb = col_off // LANES
    tm = _tile(S, tm)

    def body(x_ref, g_ref, o_ref):
        v = x_ref[...]
        r = lax.rsqrt(_segmean64(v * v) + NORM_EPS)
        o_ref[...] = (v * r * g_ref[...]).astype(BF16)

    return _pallas(
        body, name=name, grid=(S // tm, width // LANES),
        in_specs=[pl.BlockSpec((tm, LANES), lambda i, j: (i, cb + j)), pl.BlockSpec((1, LANES), lambda i, j: (0, 0))],
        out_specs=pl.BlockSpec((tm, LANES), lambda i, j: (i, j)),
        out_shape=jax.ShapeDtypeStruct((S, width), BF16),
        compiler_params=_params(("parallel", "parallel"), 8 * tm * LANES * 4),
    )(x, jnp.tile(g64, 2).reshape(1, LANES))


def headnorm_bwd(x, g64, dh, *, col_off, width, name, tm=512):
    S = x.shape[0]
    cb = col_off // LANES
    tm = _tile(S, tm)

    def body(x_ref, g_ref, dh_ref, dx_ref, dg_ref):
        @pl.when((pl.program_id(0) == 0) & (pl.program_id(1) == 0))
        def _():
            dg_ref[...] = jnp.zeros_like(dg_ref)

        v = x_ref[...]
        dhv = dh_ref[...]
        r = lax.rsqrt(_segmean64(v * v) + NORM_EPS)
        y = v * r
        dy = dhv * g_ref[...]
        dx_ref[...] = (r * (dy - y * _segmean64(dy * y))).astype(BF16)
        dg_ref[...] += jnp.sum(dhv * y, axis=0, keepdims=True)

    return _pallas(
        body, name=name, grid=(width // LANES, S // tm),
        in_specs=[pl.BlockSpec((tm, LANES), lambda j, i: (i, cb + j)), pl.BlockSpec((1, LANES), lambda j, i: (0, 0)),
                  pl.BlockSpec((tm, LANES), lambda j, i: (i, j))],
        out_specs=[pl.BlockSpec((tm, LANES), lambda j, i: (i, j)), pl.BlockSpec((1, LANES), lambda j, i: (0, 0))],
        out_shape=[jax.ShapeDtypeStruct((S, width), BF16), jax.ShapeDtypeStruct((1, LANES), F32)],
        compiler_params=_params(("arbitrary", "arbitrary"), 10 * tm * LANES * 4),
    )(x, jnp.tile(g64, 2).reshape(1, LANES), dh)


def gate_fwd(o, proj, *, gate_off, name, tm=256):
    S, W = o.shape
    cb = gate_off // W
    tm = _tile(S, tm)

    def body(o_ref, g_ref, out_ref):
        g = g_ref[...]
        out_ref[...] = (o_ref[...] * (g * _sigmoid(g))).astype(BF16)

    return _pallas(
        body, name=name, grid=(S // tm,),
        in_specs=[pl.BlockSpec((tm, W), lambda i: (i, 0)), pl.BlockSpec((tm, W), lambda i: (i, cb))],
        out_specs=pl.BlockSpec((tm, W), lambda i: (i, 0)),
        out_shape=jax.ShapeDtypeStruct((S, W), BF16),
        compiler_params=_params(("parallel",), 6 * tm * W * 4),
    )(o, proj)


def gate_bwd(dog, o, proj, *, gate_off, name, tm=256):
    S, W = o.shape
    cb = gate_off // W
    tm = _tile(S, tm)

    def body(d_ref, o_ref, g_ref, do_ref, dg_ref):
        g = g_ref[...]
        d = d_ref[...]
        s = _sigmoid(g)
        do_ref[...] = d * (g * s)
        dg_ref[...] = (d * o_ref[...] * (s * (1.0 + g * (1.0 - s)))).astype(BF16)

    row = pl.BlockSpec((tm, W), lambda i: (i, 0))
    return _pallas(
        body, name=name, grid=(S // tm,),
        in_specs=[row, row, pl.BlockSpec((tm, W), lambda i: (i, cb))],
        out_specs=[row, row],
        out_shape=[jax.ShapeDtypeStruct((S, W), F32), jax.ShapeDtypeStruct((S, W), BF16)],
        compiler_params=_params(("parallel",), 10 * tm * W * 4),
    )(dog, o, proj)


def loss_head(y, target, *, name, tm=256):
    S, W = y.shape
    tm = _tile(S, tm)
    n = S // tm

    def body(y_ref, t_ref, dy_ref, l_ref, acc_ref):
        i = pl.program_id(0)

        @pl.when(i == 0)
        def _():
            acc_ref[...] = jnp.zeros_like(acc_ref)

        e = y_ref[...] - t_ref[...]
        dy_ref[...] = e * (1.0 / W)
        acc_ref[...] += jnp.sum(e * e, axis=0, keepdims=True)

        @pl.when(i == n - 1)
        def _():
            l_ref[...] = jnp.full(l_ref.shape, (0.5 / W) * jnp.sum(acc_ref[...]), F32)

    row = pl.BlockSpec((tm, W), lambda i: (i, 0))
    return _pallas(
        body, name=name, grid=(n,),
        in_specs=[row, row],
        out_specs=[row, pl.BlockSpec((8, LANES), lambda i: (0, 0))],
        out_shape=[jax.ShapeDtypeStruct((S, W), F32), jax.ShapeDtypeStruct((8, LANES), F32)],
        scratch_shapes=[pltpu.VMEM((1, W), F32)],
        compiler_params=_params(("arbitrary",), 8 * tm * W * 4),
    )(y, target)


def _stack_heads(t, zero):
    first = _first_half(t.shape)
    return jnp.concatenate([jnp.where(first, t, zero), jnp.where(first, zero, t)], axis=0)


def _sb_weights(qs, ks, mask, upper, rss):
    zs = [_dot(q, k, NT) for q, k in zip(qs, ks)]
    sps = [jnp.maximum(z, 0.0) + jnp.log(1.0 + jnp.exp(-jnp.abs(z))) for z in zs]
    gs = [z - sp for z, sp in zip(zs, sps)]
    if mask is not None:
        sps = [jnp.where(mask, sp, 0.0) for sp in sps]
    cums = [_dot_split(sp, upper) for sp in sps]
    avs = [jnp.exp(g - (cum + rs)) for g, cum, rs in zip(gs, cums, rss)]
    if mask is not None:
        avs = [jnp.where(mask, a, 0.0) for a in avs]
    return avs, sps, gs


def _sb_consts():
    row = lax.broadcasted_iota(I32, (BLK, BLK), 0)
    col = lax.broadcasted_iota(I32, (BLK, BLK), 1)
    diag = col < row
    return row, col, jnp.concatenate([diag, diag], axis=0)


SB_DEAD = 88.0


def _sb_walk_left(block, i, carry):
    def least(c):
        m = c[0][1]
        for pair in c[1:]:
            m = jnp.minimum(m, pair[1])
        return jnp.min(m)

    def cond(state):
        jj, _, low = state
        return (jj < i) & (low < SB_DEAD)

    def body(state):
        jj, c, _ = state
        c = block(i - 1 - jj, c, None)
        return jj + 1, c, least(c)

    return lax.while_loop(cond, body, (jnp.int32(0), carry, least(carry)))[1]


SB_W = 1024


def sb_attn_fwd(qkv, *, name, pairs=8, ride=None):
    S = qkv.shape[0]
    W = SB_W
    PW = pairs * LANES
    ngrp, nq = W // PW, S // BLK
    assert ride is None or (ngrp == 1 and nq >= 3)

    def body(*refs):
        if ride is None:
            q_ref, k_ref, v_ref, o_ref = refs
        else:
            q_ref, k_ref, v_ref, o_ref = refs[0], refs[1], refs[2], refs[4]
            send, forward, finish = _gather_steps(refs[3], refs[5], refs[6], refs[7])
        i = pl.program_id(1)
        if ride is not None:
            pl.when(i == 0)(send)
            pl.when(i == nq // 2)(forward)
        row, col, diag = _sb_consts()
        upper = jnp.tile((row > col).astype(BF16), (2, 1))
        zero = jnp.zeros((BLK, LANES), BF16)
        qs = [_stack_heads(q_ref[:, p * LANES:(p + 1) * LANES] * 0.125, zero) for p in range(pairs)]

        def block(j, carry, mask):
            rows = pl.ds(pl.multiple_of(j * BLK, BLK), BLK)
            cols = [slice(p * LANES, (p + 1) * LANES) for p in range(pairs)]
            avs, sps, _ = _sb_weights(qs, [k_ref[rows, c] for c in cols], mask, upper, [c[1] for c in carry])
            abs_ = [a.astype(BF16) for a in avs]
            outs = [_dot(jnp.concatenate([ab[:BLK], ab[BLK:]], axis=1), _stack_heads(v_ref[rows, c], zero))
                    for ab, c in zip(abs_, cols)]
            return tuple((carry[p][0] + outs[p], carry[p][1] + jnp.sum(sps[p], axis=1, keepdims=True))
                         for p in range(pairs))

        init = tuple((jnp.zeros((BLK, LANES), F32), jnp.zeros((2 * BLK, 1), F32)) for _ in range(pairs))
        carry = _sb_walk_left(block, i, block(i, init, diag))
        for p in range(pairs):
            o_ref[:, p * LANES:(p + 1) * LANES] = carry[p][0]
        if ride is not None:
            pl.when(i == nq - 1)(finish)

    once = pl.Buffered(1)
    hbm = pl.BlockSpec(memory_space=pltpu.HBM)
    in_specs = [pl.BlockSpec((BLK, PW), lambda p, i: (i, p)),
                pl.BlockSpec((S, PW), lambda p, i: (0, ngrp + p), pipeline_mode=once),
                pl.BlockSpec((S, PW), lambda p, i: (0, 2 * ngrp + p), pipeline_mode=once)]
    out_specs = [pl.BlockSpec((BLK, PW), lambda p, i: (i, p))]
    out_shape = [jax.ShapeDtypeStruct((S, W), F32)]
    args = [qkv, qkv, qkv]
    if ride is not None:
        in_specs.append(hbm)
        out_specs.append(hbm)
        out_shape.append(jax.ShapeDtypeStruct((N_CHIPS,) + ride.shape, ride.dtype))
        args.append(ride)
    res = _pallas(
        body, name=name, grid=(ngrp, nq),
        in_specs=in_specs, out_specs=out_specs, out_shape=out_shape,
        scratch_shapes=[] if ride is None else list(GATHER_SEMS),
        compiler_params=_params(("parallel", "arbitrary"), 2 * S * PW * 2 + 16 * BLK * PW * 4),
    )(*args)
    return res[0] if ride is None else (res[0], place_own_block(res[1], ride))


def sb_attn_bwd(qkv, o, do, *, name, pairs=4):
    S = qkv.shape[0]
    W = SB_W
    PW = pairs * LANES
    ngrp, nq = W // PW, S // BLK

    def body(q_ref, k_ref, v_ref, o_ref, do_ref, dq_ref, dk_ref, dv_ref):
        i = pl.program_id(1)

        @pl.when(i == 0)
        def _():
            dk_ref[...] = jnp.zeros_like(dk_ref)
            dv_ref[...] = jnp.zeros_like(dv_ref)

        row, col, diag = _sb_consts()
        upper = jnp.tile((row > col).astype(BF16), (2, 1))
        upper_incl = jnp.tile((row >= col).astype(BF16), (2, 1))
        first = _first_half((BLK, LANES))
        zero = jnp.zeros((BLK, LANES), BF16)
        qs, dos, tots = [], [], []
        for p in range(pairs):
            cols = slice(p * LANES, (p + 1) * LANES)
            qs.append(_stack_heads(q_ref[:, cols] * 0.125, zero))
            dob = do_ref[:, cols].astype(BF16)
            dos.append(_stack_heads(dob, zero))
            prod = dob.astype(F32) * o_ref[:, cols]
            tots.append(jnp.concatenate([jnp.sum(jnp.where(first, prod, 0.0), axis=1, keepdims=True),
                                         jnp.sum(jnp.where(first, 0.0, prod), axis=1, keepdims=True)], axis=0))

        def block(j, carry, mask):
            rows = pl.ds(pl.multiple_of(j * BLK, BLK), BLK)
            P = range(pairs)
            cols = [slice(p * LANES, (p + 1) * LANES) for p in P]
            ks = [k_ref[rows, c] for c in cols]
            das = [_dot(dos[p], v_ref[rows, cols[p]], NT) for p in P]
            avs, sps, gs = _sb_weights(qs, ks, mask, upper, [c[1] for c in carry])
            abs_ = [a.astype(BF16) for a in avs]
            es = [ab.astype(F32) * da for ab, da in zip(abs_, das)]
            sufs = [_dot_split(e, upper_incl) for e in es]
            lefts = [tots[p] - (sufs[p] + carry[p][2]) for p in P]
            dzs = [es[p] - jnp.exp(gs[p]) * (es[p] + lefts[p]) for p in P]
            if mask is not None:
                dzs = [jnp.where(mask, dz, 0.0) for dz in dzs]
            dzbs = [dz.astype(BF16) for dz in dzs]
            dks = [_dot(dzbs[p], qs[p], TN) for p in P]
            dvs = [_dot(abs_[p], dos[p], TN) for p in P]
            dqs = [_dot(jnp.concatenate([dzbs[p][:BLK], dzbs[p][BLK:]], axis=1), _stack_heads(ks[p], zero)) for p in P]
            for p in P:
                dk_ref[rows, cols[p]] += dks[p]
                dv_ref[rows, cols[p]] += dvs[p]
            return tuple((carry[p][0] + dqs[p], carry[p][1] + jnp.sum(sps[p], axis=1, keepdims=True),
                          carry[p][2] + jnp.sum(es[p], axis=1, keepdims=True)) for p in P)

        col0 = jnp.zeros((2 * BLK, 1), F32)
        init = tuple((jnp.zeros((BLK, LANES), F32), col0, col0) for _ in range(pairs))
        carry = _sb_walk_left(block, i, block(i, init, diag))
        for p in range(pairs):
            dq_ref[:, p * LANES:(p + 1) * LANES] = (carry[p][0] * 0.125).astype(BF16)

    once = pl.Buffered(1)
    tile = pl.BlockSpec((BLK, PW), lambda p, i: (i, p))
    full = pl.BlockSpec((S, PW), lambda p, i: (0, p), pipeline_mode=once)
    shape = jax.ShapeDtypeStruct((S, W), F32)
    return _pallas(
        body, name=name, grid=(ngrp, nq),
        in_specs=[tile,
                  pl.BlockSpec((S, PW), lambda p, i: (0, ngrp + p), pipeline_mode=once),
                  pl.BlockSpec((S, PW), lambda p, i: (0, 2 * ngrp + p), pipeline_mode=once),
                  tile, tile],
        out_specs=[tile, full, full],
        out_shape=[jax.ShapeDtypeStruct((S, W), BF16), shape, shape],
        compiler_params=_params(("parallel", "arbitrary"), 2 * S * PW * 2 + 2 * S * PW * 4 + 16 * BLK * PW * 4),
    )(qkv, qkv, qkv, o, do)


FB = 256


def flash_fwd(qn, kn, vb, *, name, heads=4, ride=None):
    S = qn.shape[0]
    H, DK, DV = MLA_HEADS, MLA_PAD, LANES
    nq, ngrp = S // FB, H // heads
    scale = 1.0 / math.sqrt(MLA_QK)
    assert ride is None or (ngrp >= 2 and nq >= 2)

    def body(*refs):
        if ride is None:
            q_ref, k_ref, v_ref, o_ref, lse_ref = refs
        else:
            q_ref, k_ref, v_ref, o_ref, lse_ref = refs[0], refs[1], refs[2], refs[4], refs[5]
            send, forward, finish = _gather_steps(refs[3], refs[6], refs[7], refs[8])
        g, i = pl.program_id(0), pl.program_id(1)
        if ride is not None:
            pl.when((g == 0) & (i == 0))(send)
            pl.when((g == 1) & (i == 0))(forward)
        diag = lax.broadcasted_iota(I32, (FB, FB), 1) <= lax.broadcasted_iota(I32, (FB, FB), 0)
        qs = [q_ref[:, h * DK:(h + 1) * DK] for h in range(heads)]

        def block(j, carry, mask):
            rows = pl.ds(pl.multiple_of(j * FB, FB), FB)
            H = range(heads)
            ss = [_dot(qs[h], k_ref[rows, h * DK:(h + 1) * DK], NT) * scale for h in H]
            if mask is not None:
                ss = [jnp.where(mask, s, -1e30) for s in ss]
            ms = [jnp.maximum(carry[h][1], jnp.max(ss[h], axis=1, keepdims=True)) for h in H]
            ps = [jnp.exp(ss[h] - ms[h]) for h in H]
            ws = [jnp.exp(carry[h][1] - ms[h]) for h in H]
            pvs = [_dot(ps[h].astype(BF16), v_ref[rows, h * DV:(h + 1) * DV]) for h in H]
            return tuple((carry[h][0] * ws[h] + pvs[h], ms[h], carry[h][2] * ws[h] + jnp.sum(ps[h], axis=1, keepdims=True))
                         for h in H)

        init = tuple((jnp.zeros((FB, DV), F32), jnp.full((FB, 1), -1e30, F32), jnp.zeros((FB, 1), F32))
                     for _ in range(heads))
        carry = lax.fori_loop(0, i, lambda j, c: block(j, c, None), init)
        carry = block(i, carry, diag)
        for h in range(heads):
            acc, m, l = carry[h]
            o_ref[:, h * DV:(h + 1) * DV] = acc / l
            lse_ref[h] = m + jnp.log(l)
        if ride is not None:
            pl.when((g == ngrp - 1) & (i == nq - 1))(finish)

    hbm = pl.BlockSpec(memory_space=pltpu.HBM)
    in_specs = [pl.BlockSpec((FB, heads * DK), lambda g, i: (i, g)),
                pl.BlockSpec((S, heads * DK), lambda g, i: (0, g), pipeline_mode=pl.Buffered(1)),
                pl.BlockSpec((S, heads * DV), lambda g, i: (0, g), pipeline_mode=pl.Buffered(1))]
    out_specs = [pl.BlockSpec((FB, heads * DV), lambda g, i: (i, g)), pl.BlockSpec((heads, FB, 1), lambda g, i: (g, i, 0))]
    out_shape = [jax.ShapeDtypeStruct((S, H * DV), F32), jax.ShapeDtypeStruct((H, S, 1), F32)]
    args = [qn, kn, vb]
    if ride is not None:
        in_specs.append(hbm)
        out_specs.append(hbm)
        out_shape.append(jax.ShapeDtypeStruct((N_CHIPS,) + ride.shape, ride.dtype))
        args.append(ride)
    order = ("parallel", "arbitrary") if ride is None else ("arbitrary", "arbitrary")
    res = _pallas(
        body, name=name, grid=(ngrp, nq),
        in_specs=in_specs, out_specs=out_specs, out_shape=out_shape,
        scratch_shapes=[] if ride is None else list(GATHER_SEMS),
        compiler_params=_params(order, 2 * S * heads * (DK + DV) * 2 + 16 * FB * FB * 4),
    )(*args)
    return tuple(res) if ride is None else (res[0], res[1], place_own_block(res[2], ride))


def flash_bwd(qn, kn, vb, o, lse, do, *, name, heads=4):
    S = qn.shape[0]
    H, DK, DV = MLA_HEADS, MLA_PAD, LANES
    nq, ngrp = S // FB, H // heads
    scale = 1.0 / math.sqrt(MLA_QK)

    def body(q_ref, k_ref, v_ref, o_ref, lse_ref, do_ref, dq_ref, dk_ref, dv_ref):
        i = pl.program_id(1)

        @pl.when(i == 0)
        def _():
            dk_ref[...] = jnp.zeros_like(dk_ref)
            dv_ref[...] = jnp.zeros_like(dv_ref)

        diag = lax.broadcasted_iota(I32, (FB, FB), 1) <= lax.broadcasted_iota(I32, (FB, FB), 0)
        qs, dobs, deltas, lses = [], [], [], []
        for h in range(heads):
            do = do_ref[:, h * DV:(h + 1) * DV]
            qs.append(q_ref[:, h * DK:(h + 1) * DK])
            dobs.append(do.astype(BF16))
            deltas.append(jnp.sum(do * o_ref[:, h * DV:(h + 1) * DV], axis=1, keepdims=True))
            lses.append(lse_ref[h])

        def block(j, carry, mask):
            rows = pl.ds(pl.multiple_of(j * FB, FB), FB)
            H = range(heads)
            kcs = [slice(h * DK, (h + 1) * DK) for h in H]
            vcs = [slice(h * DV, (h + 1) * DV) for h in H]
            ks = [k_ref[rows, kcs[h]] for h in H]
            ss = [_dot(qs[h], ks[h], NT) for h in H]
            dps = [_dot(dobs[h], v_ref[rows, vcs[h]], NT) for h in H]
            ps = [jnp.exp(ss[h] * scale - lses[h]) for h in H]
            if mask is not None:
                ps = [jnp.where(mask, p, 0.0) for p in ps]
            pbs = [p.astype(BF16) for p in ps]
            dss = [(ps[h] * (dps[h] - deltas[h]) * scale).astype(BF16) for h in H]
            dvs = [_dot(pbs[h], dobs[h], TN) for h in H]
            dks = [_dot(dss[h], qs[h], TN) for h in H]
            dqs = [_dot(dss[h], ks[h]) for h in H]
            for h in H:
                dv_ref[rows, vcs[h]] += dvs[h]
                dk_ref[rows, kcs[h]] += dks[h]
            return tuple(carry[h] + dqs[h] for h in H)

        carry = lax.fori_loop(0, i, lambda j, c: block(j, c, None), tuple(jnp.zeros((FB, DK), F32) for _ in range(heads)))
        carry = block(i, carry, diag)
        for h in range(heads):
            dq_ref[:, h * DK:(h + 1) * DK] = carry[h]

    qtile = pl.BlockSpec((FB, heads * DK), lambda g, i: (i, g))
    otile = pl.BlockSpec((FB, heads * DV), lambda g, i: (i, g))
    once = pl.Buffered(1)
    kfull = pl.BlockSpec((S, heads * DK), lambda g, i: (0, g), pipeline_mode=once)
    vfull = pl.BlockSpec((S, heads * DV), lambda g, i: (0, g), pipeline_mode=once)
    return _pallas(
        body, name=name, grid=(ngrp, nq),
        in_specs=[qtile, kfull, vfull, otile, pl.BlockSpec((heads, FB, 1), lambda g, i: (g, i, 0)), otile],
        out_specs=[qtile, kfull, vfull],
        out_shape=[jax.ShapeDtypeStruct((S, H * DK), F32), jax.ShapeDtypeStruct((S, H * DK), F32),
                   jax.ShapeDtypeStruct((S, H * DV), F32)],
        compiler_params=_params(("parallel", "arbitrary"), S * heads * (DK + DV) * 6 + 16 * FB * FB * 4),
    )(qn, kn, vb, o, lse, do)


def _rope_tables(S):
    half = 32
    inv_freq = ROPE_THETA ** (-jnp.arange(half, dtype=F32) / half)
    ang = jnp.arange(S).astype(F32)[:, None] * inv_freq[None, :]
    cos, sin = jnp.cos(ang), jnp.sin(ang)
    ones, zeros = jnp.ones((S, LANES), F32), jnp.zeros((S, LANES), F32)
    pad = jnp.zeros((S, 64), F32)
    return (jnp.concatenate([ones, cos, cos, pad + 1.0], axis=1),
            jnp.concatenate([zeros, -sin, sin, pad], axis=1))


def _rope_partner(u):
    lane = lax.broadcasted_iota(I32, u.shape, 1)
    return jnp.where((lane % HEAD64) < 32, pltpu.roll(u, LANES - 32, 1), pltpu.roll(u, 32, 1))


def _normrope(raw, g, cos, sgn):
    r = lax.rsqrt(jnp.sum(raw * raw, axis=1, keepdims=True) * (1.0 / MLA_QK) + NORM_EPS)
    y = raw * r
    u = y * g
    pe = u[:, LANES:]
    out = jnp.concatenate([u[:, :LANES], pe * cos[:, LANES:] + _rope_partner(pe) * sgn[:, LANES:]], axis=1)
    return out, y, r


def _normrope_bwd(dout, g, cos, sgn, y, r):
    dpe = dout[:, LANES:]
    du = jnp.concatenate([dout[:, :LANES], dpe * cos[:, LANES:] + _rope_partner(dpe * sgn[:, LANES:])], axis=1)
    dy = du * g
    draw = r * (dy - y * (jnp.sum(dy * y, axis=1, keepdims=True) * (1.0 / MLA_QK)))
    return draw, jnp.sum(du * y, axis=0, keepdims=True)


def mla_prep_fwd(qraw, kv, proj, gq, gk, cos, sgn, *, kpe_off, name, tm=1024):
    S = qraw.shape[0]
    tm = _tile(S, tm)
    kb = kpe_off // LANES

    def body(q_ref, kn_ref, v_ref, kpe_ref, gq_ref, gk_ref, c_ref, s_ref, qo_ref, ko_ref, vo_ref):
        cos, sgn = c_ref[...], s_ref[...]
        qo_ref[...] = _normrope(q_ref[...], gq_ref[...], cos, sgn)[0].astype(BF16)
        kraw = jnp.concatenate([kn_ref[...], kpe_ref[...]], axis=1)
        ko_ref[...] = _normrope(kraw, gk_ref[...], cos, sgn)[0].astype(BF16)
        vo_ref[...] = v_ref[...].astype(BF16)

    head = pl.BlockSpec((tm, MLA_PAD), lambda i, h: (i, h))
    gain = pl.BlockSpec((1, MLA_PAD), lambda i, h: (0, 0))
    tab = pl.BlockSpec((tm, MLA_PAD), lambda i, h: (i, 0))
    return _pallas(
        body, name=name, grid=(S // tm, MLA_HEADS),
        in_specs=[head, pl.BlockSpec((tm, LANES), lambda i, h: (i, 2 * h)), pl.BlockSpec((tm, LANES), lambda i, h: (i, 2 * h + 1)),
                  pl.BlockSpec((tm, LANES), lambda i, h: (i, kb)), gain, gain, tab, tab],
        out_specs=[head, head, pl.BlockSpec((tm, LANES), lambda i, h: (i, h))],
        out_shape=[jax.ShapeDtypeStruct(qraw.shape, BF16), jax.ShapeDtypeStruct(qraw.shape, BF16),
                   jax.ShapeDtypeStruct((S, MLA_HEADS * LANES), BF16)],
        compiler_params=_params(("parallel", "arbitrary"), 16 * tm * MLA_PAD * 4),
    )(qraw, kv, kv, proj, gq, gk, cos, sgn)


def mla_prep_bwd(dqn, dkn, dv, qraw, kv, proj, gq, gk, cos, sgn, *, kpe_off, name, tm=512):
    S = qraw.shape[0]
    tm = _tile(S, tm)
    kb = kpe_off // LANES

    def body(dq_ref, dk_ref, dv_ref, q_ref, kn_ref, kpe_ref, gq_ref, gk_ref, c_ref, s_ref,
             dqo_ref, dkv_ref, dkpe_ref, dgq_ref, dgk_ref, acc_ref):
        i, h = pl.program_id(0), pl.program_id(1)

        @pl.when((i == 0) & (h == 0))
        def _():
            dgq_ref[...] = jnp.zeros_like(dgq_ref)
            dgk_ref[...] = jnp.zeros_like(dgk_ref)

        @pl.when(h == 0)
        def _():
            acc_ref[...] = jnp.zeros_like(acc_ref)

        cos, sgn = c_ref[...], s_ref[...]
        _, yq, rq = _normrope(q_ref[...], gq_ref[...], cos, sgn)
        dq, dgq = _normrope_bwd(dq_ref[...], gq_ref[...], cos, sgn, yq, rq)
        dqo_ref[...] = dq.astype(BF16)
        dgq_ref[...] += dgq
        kraw = jnp.concatenate([kn_ref[...], kpe_ref[...]], axis=1)
        _, yk, rk = _normrope(kraw, gk_ref[...], cos, sgn)
        dk, dgk = _normrope_bwd(dk_ref[...], gk_ref[...], cos, sgn, yk, rk)
        dgk_ref[...] += dgk
        dkv_ref[...] = jnp.concatenate([dk[:, :LANES], dv_ref[...]], axis=1).astype(BF16)
        acc_ref[...] += dk[:, LANES:]

        @pl.when(h == MLA_HEADS - 1)
        def _():
            dkpe_ref[...] = acc_ref[...].astype(BF16)

    head = pl.BlockSpec((tm, MLA_PAD), lambda i, h: (i, h))
    gain = pl.BlockSpec((1, MLA_PAD), lambda i, h: (0, 0))
    tab = pl.BlockSpec((tm, MLA_PAD), lambda i, h: (i, 0))
    return _pallas(
        body, name=name, grid=(S // tm, MLA_HEADS),
        in_specs=[head, head, pl.BlockSpec((tm, LANES), lambda i, h: (i, h)), head,
                  pl.BlockSpec((tm, LANES), lambda i, h: (i, 2 * h)), pl.BlockSpec((tm, LANES), lambda i, h: (i, kb)),
                  gain, gain, tab, tab],
        out_specs=[head, head, pl.BlockSpec((tm, LANES), lambda i, h: (i, 0)), gain, gain],
        out_shape=[jax.ShapeDtypeStruct(qraw.shape, BF16), jax.ShapeDtypeStruct(qraw.shape, BF16),
                   jax.ShapeDtypeStruct((S, LANES), BF16), jax.ShapeDtypeStruct((1, MLA_PAD), F32),
                   jax.ShapeDtypeStruct((1, MLA_PAD), F32)],
        scratch_shapes=[pltpu.VMEM((tm, LANES), F32)],
        compiler_params=_params(("arbitrary", "arbitrary"), 24 * tm * MLA_PAD * 4),
    )(dqn, dkn, dv, qraw, kv, proj, gq, gk, cos, sgn)


def swa_kv_prep(proj, gk64, *, k_off, v_off, name, tm=512):
    S = proj.shape[0]
    tm = _tile(S, tm)
    W = SWA_KV * HEAD64

    def body(k_ref, v_ref, g_ref, ko_ref, vo_ref):
        first = _first_half((tm, LANES))

        def dup(n):
            nr = pltpu.roll(n, HEAD64, 1)
            return jnp.where(first, n, nr), jnp.where(first, nr, n)

        for t in range(W // LANES):
            x = k_ref[:, t * LANES:(t + 1) * LANES]
            n = x * lax.rsqrt(_segmean64(x * x) + NORM_EPS) * g_ref[...]
            d0, d1 = dup(n)
            ko_ref[:, 2 * t * LANES:(2 * t + 1) * LANES] = d0.astype(BF16)
            ko_ref[:, (2 * t + 1) * LANES:(2 * t + 2) * LANES] = d1.astype(BF16)
            d0, d1 = dup(v_ref[:, t * LANES:(t + 1) * LANES])
            vo_ref[:, 2 * t * LANES:(2 * t + 1) * LANES] = d0.astype(BF16)
            vo_ref[:, (2 * t + 1) * LANES:(2 * t + 2) * LANES] = d1.astype(BF16)

    out = pl.BlockSpec((tm, SWA_KV * LANES), lambda i: (i, 0))
    shape = jax.ShapeDtypeStruct((S, SWA_KV * LANES), BF16)
    return _pallas(
        body, name=name, grid=(S // tm,),
        in_specs=[pl.BlockSpec((tm, W), lambda i: (i, k_off // W)), pl.BlockSpec((tm, W), lambda i: (i, v_off // W)),
                  pl.BlockSpec((1, LANES), lambda i: (0, 0))],
        out_specs=[out, out], out_shape=[shape, shape],
        compiler_params=_params(("parallel",), 12 * tm * W * 4),
    )(proj, proj, jnp.tile(gk64, 2).reshape(1, LANES))


def swa_kv_prep_bwd(dkdup, dvdup, proj, gk64, *, k_off, name, tm=512):
    S = proj.shape[0]
    tm = _tile(S, tm)
    W = SWA_KV * HEAD64

    def body(dk_ref, dv_ref, k_ref, g_ref, dko_ref, dvo_ref, dg_ref):
        @pl.when(pl.program_id(0) == 0)
        def _():
            dg_ref[...] = jnp.zeros_like(dg_ref)

        first = _first_half((tm, LANES))

        def fold(ref, t):
            d0 = ref[:, 2 * t * LANES:(2 * t + 1) * LANES]
            d1 = ref[:, (2 * t + 1) * LANES:(2 * t + 2) * LANES]
            return jnp.where(first, d0 + pltpu.roll(d0, HEAD64, 1), d1 + pltpu.roll(d1, HEAD64, 1))

        for t in range(W // LANES):
            dvo_ref[:, t * LANES:(t + 1) * LANES] = fold(dv_ref, t).astype(BF16)
            dh = fold(dk_ref, t)
            x = k_ref[:, t * LANES:(t + 1) * LANES]
            r = lax.rsqrt(_segmean64(x * x) + NORM_EPS)
            y = x * r
            dy = dh * g_ref[...]
            dko_ref[:, t * LANES:(t + 1) * LANES] = (r * (dy - y * _segmean64(dy * y))).astype(BF16)
            dg_ref[...] += jnp.sum(dh * y, axis=0, keepdims=True)

    dup = pl.BlockSpec((tm, SWA_KV * LANES), lambda i: (i, 0))
    out = pl.BlockSpec((tm, W), lambda i: (i, 0))
    shape = jax.ShapeDtypeStruct((S, W), BF16)
    return _pallas(
        body, name=name, grid=(S // tm,),
        in_specs=[dup, dup, pl.BlockSpec((tm, W), lambda i: (i, k_off // W)), pl.BlockSpec((1, LANES), lambda i: (0, 0))],
        out_specs=[out, out, pl.BlockSpec((1, LANES), lambda i: (0, 0))],
        out_shape=[shape, shape, jax.ShapeDtypeStruct((1, LANES), F32)],
        compiler_params=_params(("arbitrary",), 16 * tm * W * 4),
    )(dkdup, dvdup, proj, jnp.tile(gk64, 2).reshape(1, LANES))


def _swa_geometry(i):
    r = lax.broadcasted_iota(I32, (BLK, 2 * BLK), 0)
    c = lax.broadcasted_iota(I32, (BLK, 2 * BLK), 1)
    rel = r + BLK - c
    valid = (rel >= 0) & (rel < BLK) & ((c >= BLK) | (i > 0))
    return valid, rel.astype(F32)


def _swa_slope(h):
    return 2.0 ** (-8.0 * (h + 1) / SWA_HEADS)


def swa_attn_fwd(qn, kdup, vdup, sinks, *, name):
    S = qn.shape[0]
    nq = S // BLK
    group = SWA_HEADS // SWA_KV

    def body(q_ref, kp_ref, kc_ref, vp_ref, vc_ref, sink_ref, o_ref, lse_ref):
        i = pl.program_id(0)
        valid, rel = _swa_geometry(i)
        first = _first_half((BLK, LANES))
        lane = lax.broadcasted_iota(I32, (BLK, LANES), 1)
        lse_all = jnp.zeros((BLK, LANES), F32)
        zero = jnp.zeros((BLK, LANES), BF16)
        H = range(SWA_HEADS)
        kks = [jnp.concatenate([kp_ref[:, g * LANES:(g + 1) * LANES], kc_ref[:, g * LANES:(g + 1) * LANES]], axis=0)
               for g in range(SWA_KV)]
        vvs = [jnp.concatenate([vp_ref[:, g * LANES:(g + 1) * LANES], vc_ref[:, g * LANES:(g + 1) * LANES]], axis=0)
               for g in range(SWA_KV)]
        q2s = [q_ref[:, t * LANES:(t + 1) * LANES] for t in range(SWA_HEADS // 2)]
        qhs = [jnp.where(first if h % 2 == 0 else ~first, q2s[h // 2], zero) for h in H]
        ss = [_dot(qhs[h], kks[h // group], NT) * (1.0 / math.sqrt(HEAD64)) - _swa_slope(h) * rel for h in H]
        ss = [jnp.where(valid, s, -1e30) for s in ss]
        ms = [jnp.maximum(jnp.max(ss[h], axis=1, keepdims=True), sink_ref[h]) for h in H]
        es = [jnp.exp(ss[h] - ms[h]) for h in H]
        dens = [jnp.sum(es[h], axis=1, keepdims=True) + jnp.exp(sink_ref[h] - ms[h]) for h in H]
        outs = [_dot((es[h] / dens[h]).astype(BF16), vvs[h // group]) for h in H]
        for h in H:
            lse_all = jnp.where(lane == h, ms[h] + jnp.log(dens[h]), lse_all)
        for t in range(SWA_HEADS // 2):
            o_ref[:, t * LANES:(t + 1) * LANES] = jnp.where(first, outs[2 * t], outs[2 * t + 1])
        lse_ref[...] = lse_all

    prev = lambda i: (jnp.maximum(i - 1, 0), 0)
    cur = lambda i: (i, 0)
    kvw = SWA_KV * LANES
    return _pallas(
        body, name=name, grid=(nq,),
        in_specs=[pl.BlockSpec((BLK, 1024), cur), pl.BlockSpec((BLK, kvw), prev), pl.BlockSpec((BLK, kvw), cur),
                  pl.BlockSpec((BLK, kvw), prev), pl.BlockSpec((BLK, kvw), cur),
                  pl.BlockSpec(memory_space=pltpu.SMEM)],
        out_specs=[pl.BlockSpec((BLK, 1024), cur), pl.BlockSpec((BLK, LANES), cur)],
        out_shape=[jax.ShapeDtypeStruct((S, 1024), F32), jax.ShapeDtypeStruct((S, LANES), F32)],
        compiler_params=_params(("parallel",), 16 * BLK * 1024 * 4),
    )(qn, kdup, kdup, vdup, vdup, sinks)


def swa_attn_bwd(qn, kdup, vdup, sinks, o, lse, do, *, name):
    S = qn.shape[0]
    nq = S // BLK
    group = SWA_HEADS // SWA_KV
    scale = 1.0 / math.sqrt(HEAD64)

    def body(q_ref, kp_ref, kc_ref, vp_ref, vc_ref, sink_ref, o_ref, lse_ref, do_ref,
             dq_ref, dk_ref, dv_ref, ds_ref):
        i = pl.program_id(0)

        @pl.when(i == 0)
        def _():
            dk_ref[...] = jnp.zeros_like(dk_ref)
            dv_ref[...] = jnp.zeros_like(dv_ref)
            ds_ref[...] = jnp.zeros_like(ds_ref)

        valid, rel = _swa_geometry(i)
        first = _first_half((BLK, LANES))
        lane1 = lax.broadcasted_iota(I32, (1, LANES), 1)
        lane = lax.broadcasted_iota(I32, (BLK, LANES), 1)
        lse_all = lse_ref[...]
        prow = pl.ds(pl.multiple_of(jnp.maximum(i - 1, 0) * BLK, BLK), BLK)
        crow = pl.ds(pl.multiple_of(i * BLK, BLK), BLK)
        dsink = jnp.zeros((1, LANES), F32)
        zero = jnp.zeros((BLK, LANES), BF16)
        H, T = range(SWA_HEADS), range(SWA_HEADS // 2)
        kks = [jnp.concatenate([kp_ref[:, g * LANES:(g + 1) * LANES], kc_ref[:, g * LANES:(g + 1) * LANES]], axis=0)
               for g in range(SWA_KV)]
        vvs = [jnp.concatenate([vp_ref[:, g * LANES:(g + 1) * LANES], vc_ref[:, g * LANES:(g + 1) * LANES]], axis=0)
               for g in range(SWA_KV)]
        q2s = [q_ref[:, t * LANES:(t + 1) * LANES] for t in T]
        do2s = [do_ref[:, t * LANES:(t + 1) * LANES] for t in T]
        prods = [do2s[t] * o_ref[:, t * LANES:(t + 1) * LANES] for t in T]
        mine = [first if h % 2 == 0 else ~first for h in H]
        qhs = [jnp.where(mine[h], q2s[h // 2], zero) for h in H]
        dohs = [jnp.where(mine[h], do2s[h // 2], 0.0).astype(BF16) for h in H]
        deltas = [jnp.sum(jnp.where(mine[h], prods[h // 2], 0.0), axis=1, keepdims=True) for h in H]
        lses = [jnp.sum(jnp.where(lane == h, lse_all, 0.0), axis=1, keepdims=True) for h in H]
        ss = [_dot(qhs[h], kks[h // group], NT) * scale - _swa_slope(h) * rel for h in H]
        dps = [_dot(dohs[h], vvs[h // group], NT) for h in H]
        ps = [jnp.where(valid, jnp.exp(ss[h] - lses[h]), 0.0) for h in H]
        dscs = [(ps[h] * (dps[h] - deltas[h]) * scale).astype(BF16) for h in H]
        dqs = [_dot(dscs[h], kks[h // group]) for h in H]
        dks = [_dot(dscs[h], qhs[h], TN) for h in H]
        dvs = [_dot(ps[h].astype(BF16), dohs[h], TN) for h in H]
        for h in H:
            psink = jnp.exp(sink_ref[h] - lses[h])
            dsink = dsink + jnp.where(lane1 == h, -jnp.sum(psink * deltas[h]), 0.0)
        for t in T:
            dq_ref[:, t * LANES:(t + 1) * LANES] = jnp.where(first, dqs[2 * t], dqs[2 * t + 1])
        for g in range(SWA_KV):
            cols = slice(g * LANES, (g + 1) * LANES)
            b = g * group
            dkk = (dks[b] + dks[b + 1]) + (dks[b + 2] + dks[b + 3])
            dvv = (dvs[b] + dvs[b + 1]) + (dvs[b + 2] + dvs[b + 3])
            dk_ref[prow, cols] += dkk[:BLK]
            dv_ref[prow, cols] += dvv[:BLK]
            dk_ref[crow, cols] += dkk[BLK:]
            dv_ref[crow, cols] += dvv[BLK:]
        ds_ref[...] += dsink

    prev = lambda i: (jnp.maximum(i - 1, 0), 0)
    cur = lambda i: (i, 0)
    kvw = SWA_KV * LANES
    whole = pl.BlockSpec((S, kvw), lambda i: (0, 0))
    return _pallas(
        body, name=name, grid=(nq,),
        in_specs=[pl.BlockSpec((BLK, 1024), cur), pl.BlockSpec((BLK, kvw), prev), pl.BlockSpec((BLK, kvw), cur),
                  pl.BlockSpec((BLK, kvw), prev), pl.BlockSpec((BLK, kvw), cur),
                  pl.BlockSpec(memory_space=pltpu.SMEM),
                  pl.BlockSpec((BLK, 1024), cur), pl.BlockSpec((BLK, LANES), cur), pl.BlockSpec((BLK, 1024), cur)],
        out_specs=[pl.BlockSpec((BLK, 1024), cur), whole, whole, pl.BlockSpec((1, LANES), lambda i: (0, 0))],
        out_shape=[jax.ShapeDtypeStruct((S, 1024), F32), jax.ShapeDtypeStruct((S, kvw), F32),
                   jax.ShapeDtypeStruct((S, kvw), F32), jax.ShapeDtypeStruct((1, LANES), F32)],
        compiler_params=_params(("arbitrary",), 4 * S * kvw * 4 + 24 * BLK * 1024 * 4),
    )(qn, kdup, kdup, vdup, vdup, sinks, o, lse, do)


def _in_bwd(x, h, dproj, dy, g, w_in, tag):
    dh = matmul(dproj, w_in, "nt", name=f"{tag}_dh")
    dw_in = matmul(h, dproj, "tn", name=f"{tag}_dwin")
    dx, dg = rmsnorm_bwd(x, g, dh, col_off=0, width=D_MODEL, out_dtype=F32, res=dy, name=f"{tag}_dnorm")
    return dx, dw_in, dg


def _out_bwd(dy, og, o, proj, w_out, gate_off, tag):
    dog = matmul(dy, w_out, "nt", name=f"{tag}_dog")
    dw_out = matmul(og, dy, "tn", name=f"{tag}_dwout")
    do, dgate = gate_bwd(dog, o, proj, gate_off=gate_off, name=f"{tag}_dgate")
    return do, dgate, dw_out


def sb_fwd(x, p, tag, ride=None):
    h = rmsnorm_fwd(x, p["norm"], col_off=0, width=D_MODEL, out_dtype=BF16, name=f"{tag}_norm")
    qkv = matmul(h, p["w_in"], "nn", out_dtype=BF16, b_cols=(0, 3 * SB_W), name=f"{tag}_proj")
    gate = matmul(h, p["w_in"], "nn", b_cols=(3 * SB_W, SB_W), name=f"{tag}_gproj")
    o = sb_attn_fwd(qkv, name=f"{tag}_attn", ride=ride)
    if ride is not None:
        o, gathered = o
    og = gate_fwd(o, gate, gate_off=0, name=f"{tag}_gate")
    y = matmul(og, p["w_out"], "nn", add=x, name=f"{tag}_out")
    saved = (x, h, gate, qkv, o, og)
    return (y, saved) if ride is None else (y, saved, gathered)


def sb_bwd(dy, saved, p, tag):
    x, h, gate, qkv, o, og = saved
    do, dgate, dw_out = _out_bwd(dy, og, o, gate, p["w_out"], 0, tag)
    dq, dk, dv = sb_attn_bwd(qkv, o, do, name=f"{tag}_dattn")
    dproj = jnp.concatenate([dq, dk.astype(BF16), dv.astype(BF16), dgate], axis=1)
    dx, dw_in, dg = _in_bwd(x, h, dproj, dy, p["norm"], p["w_in"], tag)
    return dx, {"norm": dg[0], "w_in": dw_in, "w_out": dw_out}


MLA_GATE, MLA_QLAT, MLA_KVLAT, MLA_KPE, MLA_IN = 0, 1024, 1280, 1408, 1536


def mla_fwd(x, p, tabs, tag, ride=None):
    cos, sgn = tabs
    h = rmsnorm_fwd(x, p["norm"], col_off=0, width=D_MODEL, out_dtype=BF16, name=f"{tag}_norm")
    proj = matmul(h, p["w_in"], "nn", name=f"{tag}_proj")
    ql = rmsnorm_fwd(proj, p["q_a_norm"], col_off=MLA_QLAT, width=256, out_dtype=BF16, name=f"{tag}_qanorm")
    kvl = rmsnorm_fwd(proj, p["kv_a_norm"], col_off=MLA_KVLAT, width=128, out_dtype=BF16, name=f"{tag}_kvanorm")
    qraw = matmul(ql, p["w_uq"], "nn", name=f"{tag}_uq")
    kv = matmul(kvl, p["w_ukv"], "nn", name=f"{tag}_ukv")
    qn, kn, vb = mla_prep_fwd(qraw, kv, proj, p["gq"], p["gk"], cos, sgn, kpe_off=MLA_KPE, name=f"{tag}_prep")
    o, lse, *gathered = flash_fwd(qn, kn, vb, name=f"{tag}_attn", ride=ride)
    og = gate_fwd(o, proj, gate_off=MLA_GATE, name=f"{tag}_gate")
    y = matmul(og, p["w_out"], "nn", add=x, name=f"{tag}_out")
    saved = (x, h, proj, ql, kvl, qraw, kv, qn, kn, vb, o, lse, og)
    return (y, saved) if ride is None else (y, saved, gathered[0])


def mla_bwd(dy, saved, p, tabs, tag):
    cos, sgn = tabs
    x, h, proj, ql, kvl, qraw, kv, qn, kn, vb, o, lse, og = saved
    do, dgate, dw_out = _out_bwd(dy, og, o, proj, p["w_out"], MLA_GATE, tag)
    dqn, dkn, dv = flash_bwd(qn, kn, vb, o, lse, do, name=f"{tag}_dattn")
    dqraw, dkv, dkpe, dgq, dgk = mla_prep_bwd(dqn, dkn, dv, qraw, kv, proj, p["gq"], p["gk"], cos, sgn,
                                              kpe_off=MLA_KPE, name=f"{tag}_dprep")
    dql = matmul(dqraw, p["w_uq"], "nt", name=f"{tag}_dql")
    dw_uq = matmul(ql, dqraw, "tn", name=f"{tag}_dwuq")
    dkvl = matmul(dkv, p["w_ukv"], "nt", name=f"{tag}_dkvl")
    dw_ukv = matmul(kvl, dkv, "tn", name=f"{tag}_dwukv")
    dqlat, dgqa = rmsnorm_bwd(proj, p["q_a_norm"], dql, col_off=MLA_QLAT, width=256, out_dtype=BF16, name=f"{tag}_dqanorm")
    dkvlat, dgkva = rmsnorm_bwd(proj, p["kv_a_norm"], dkvl, col_off=MLA_KVLAT, width=128, out_dtype=BF16,
                                name=f"{tag}_dkvanorm")
    dproj = jnp.concatenate([dgate, dqlat, dkvlat, dkpe], axis=1)
    dx, dw_in, dg = _in_bwd(x, h, dproj, dy, p["norm"], p["w_in"], tag)
    return dx, {"norm": dg[0], "w_in": dw_in, "q_a_norm": dgqa[0], "w_uq": dw_uq, "kv_a_norm": dgkva[0],
                "w_ukv": dw_ukv, "gq": dgq[0], "gk": dgk[0], "w_out": dw_out}


SWA_Q, SWA_GATE, SWA_K, SWA_V = 0, 1024, 2048, 2304


def swa_fwd(x, p, tag):
    h = rmsnorm_fwd(x, p["norm"], col_off=0, width=D_MODEL, out_dtype=BF16, name=f"{tag}_norm")
    proj = matmul(h, p["w_in"], "nn", name=f"{tag}_proj")
    qn = headnorm_fwd(proj, p["q_head_norm"], col_off=SWA_Q, width=1024, name=f"{tag}_qnorm")
    kdup, vdup = swa_kv_prep(proj, p["k_head_norm"], k_off=SWA_K, v_off=SWA_V, name=f"{tag}_kvprep")
    o, lse = swa_attn_fwd(qn, kdup, vdup, p["sinks"], name=f"{tag}_attn")
    og = gate_fwd(o, proj, gate_off=SWA_GATE, name=f"{tag}_gate")
    y = matmul(og, p["w_out"], "nn", add=x, name=f"{tag}_out")
    return y, (x, h, proj, qn, kdup, vdup, o, lse, og)


def swa_bwd(dy, saved, p, tag):
    x, h, proj, qn, kdup, vdup, o, lse, og = saved
    do, dgate, dw_out = _out_bwd(dy, og, o, proj, p["w_out"], SWA_GATE, tag)
    dqn, dkdup, dvdup, dsinks = swa_attn_bwd(qn, kdup, vdup, p["sinks"], o, lse, do, name=f"{tag}_dattn")
    dq, dgq = headnorm_bwd(proj, p["q_head_norm"], dqn, col_off=SWA_Q, width=1024, name=f"{tag}_dqnorm")
    dk, dv, dgk = swa_kv_prep_bwd(dkdup, dvdup, proj, p["k_head_norm"], k_off=SWA_K, name=f"{tag}_dkvprep")
    dproj = jnp.concatenate([dq, dgate, dk, dv], axis=1)
    dx, dw_in, dg = _in_bwd(x, h, dproj, dy, p["norm"], p["w_in"], tag)
    return dx, {"norm": dg[0], "w_in": dw_in, "q_head_norm": dgq[0, :HEAD64] + dgq[0, HEAD64:],
                "k_head_norm": dgk[0, :HEAD64] + dgk[0, HEAD64:], "sinks": dsinks[0, :SWA_HEADS], "w_out": dw_out}


def prepare_layer(w, i):
    if i in (0, 3):
        return {"norm": w[f"l{i}_norm"], "w_in": w[f"l{i}_w_in"], "w_out": w[f"l{i}_w_out"]}
    if i == 1:
        l1_in = w["l1_w_in"]
        pad64 = lambda v: jnp.pad(v, (0, MLA_PAD - MLA_QK)).reshape(1, MLA_PAD)
        return {"norm": w["l1_norm"],
                "w_in": jnp.concatenate([l1_in[:, 448:], l1_in[:, :448], jnp.zeros((D_MODEL, 64), l1_in.dtype)], axis=1),
                "q_a_norm": w["l1_q_a_norm"], "kv_a_norm": w["l1_kv_a_norm"],
                "w_uq": jnp.pad(w["l1_w_uq"].reshape(256, MLA_HEADS, MLA_QK), ((0, 0), (0, 0), (0, MLA_PAD - MLA_QK))
                                ).reshape(256, MLA_HEADS * MLA_PAD),
                "w_ukv": w["l1_w_ukv"], "gq": pad64(w["l1_q_head_norm"]), "gk": pad64(w["l1_k_head_norm"]),
                "w_out": w["l1_w_out"]}
    l2_in = w["l2_w_in"]
    return {"norm": w["l2_norm"],
            "w_in": jnp.concatenate([l2_in[:, :1024], l2_in[:, 1536:], l2_in[:, 1024:1536]], axis=1),
            "q_head_norm": w["l2_q_head_norm"], "k_head_norm": w["l2_k_head_norm"], "sinks": w["l2_sinks"],
            "w_out": w["l2_w_out"]}


def unprepare_grads(gs):
    g0, g1, g2, g3 = gs
    d1, d2 = g1["w_in"], g2["w_in"]
    return {
        "l0_norm": g0["norm"], "l0_w_in": g0["w_in"], "l0_w_out": g0["w_out"],
        "l1_norm": g1["norm"], "l1_w_in": jnp.concatenate([d1[:, 1024:1472], d1[:, :1024]], axis=1),
        "l1_q_a_norm": g1["q_a_norm"],
        "l1_w_uq": g1["w_uq"].reshape(256, MLA_HEADS, MLA_PAD)[:, :, :MLA_QK].reshape(256, MLA_HEADS * MLA_QK),
        "l1_kv_a_norm": g1["kv_a_norm"], "l1_w_ukv": g1["w_ukv"],
        "l1_q_head_norm": g1["gq"][:MLA_QK], "l1_k_head_norm": g1["gk"][:MLA_QK], "l1_w_out": g1["w_out"],
        "l2_norm": g2["norm"], "l2_w_in": jnp.concatenate([d2[:, :1024], d2[:, 2048:], d2[:, 1024:2048]], axis=1),
        "l2_q_head_norm": g2["q_head_norm"], "l2_k_head_norm": g2["k_head_norm"], "l2_sinks": g2["sinks"],
        "l2_w_out": g2["w_out"],
        "l3_norm": g3["norm"], "l3_w_in": g3["w_in"], "l3_w_out": g3["w_out"],
    }


def local_step(x, target, w, late=None):
    tabs = _rope_tables(x.shape[0])
    p0 = prepare_layer(w, 0)
    if late is None:
        y0, s0 = sb_fwd(x, p0, "l0")
        p1 = prepare_layer(w, 1)
        y1, s1 = mla_fwd(y0, p1, tabs, "l1")
    else:
        y0, s0, gathered = sb_fwd(x, p0, "l0", ride=late[0][0])
        w = {**w, **late[0][1](gathered)}
        p1 = prepare_layer(w, 1)
        y1, s1, gathered = mla_fwd(y0, p1, tabs, "l1", ride=late[1][0])
        w = {**w, **late[1][1](gathered)}
    ps = [p0, p1, prepare_layer(w, 2), prepare_layer(w, 3)]
    y2, s2 = swa_fwd(y1, ps[2], "l2")
    y3, s3 = sb_fwd(y2, ps[3], "l3")
    dy, loss = loss_head(y3, target, name="loss")
    d3, g3 = sb_bwd(dy, s3, ps[3], "l3")
    d2, g2 = swa_bwd(d3, s2, ps[2], "l2")
    d1, g1 = mla_bwd(d2, s1, ps[1], tabs, "l1")
    d0, g0 = sb_bwd(d1, s0, ps[0], "l0")
    return loss, d0, unprepare_grads([g0, g1, g2, g3])


MATS = (("l0_w_in", "col", 1024, 4096), ("l0_w_out", "row", 1024, 1024), ("l1_w_in", "col", 1024, 1472),
        ("l1_w_uq", "col", 256, 1536), ("l1_w_ukv", "col", 128, 2048), ("l1_w_out", "row", 1024, 1024),
        ("l2_w_in", "col", 1024, 2560), ("l2_w_out", "row", 1024, 1024), ("l3_w_in", "col", 1024, 4096),
        ("l3_w_out", "row", 1024, 1024))
PACK_GROUPS = ((MATS[0:2], 1280), (MATS[2:6], 800), (MATS[6:10], 2272))
N_CHIPS = 4
PACK_W = 1024
HALF_ROWS = 2176
PACK_ROWS = 2 * HALF_ROWS
VECS = (("l0_norm", 0, 0, 1024), ("l1_norm", 1, 0, 1024), ("l2_norm", 2, 0, 1024), ("l3_norm", 3, 0, 1024),
        ("l1_q_a_norm", 4, 0, 256), ("l1_kv_a_norm", 4, 256, 128), ("l1_q_head_norm", 4, 384, 192),
        ("l1_k_head_norm", 4, 576, 192), ("l2_q_head_norm", 4, 768, 64), ("l2_k_head_norm", 4, 832, 64),
        ("l2_sinks", 4, 896, 16))
LOSS_SLOT = (4, 912)
VEC_ROWS = 8


def _shard_rows(k, n):
    return k * n // N_CHIPS // PACK_W


def _group_start(gi):
    return sum(rows for _, rows in PACK_GROUPS[:gi])


def pack_shards(shards):
    parts = []
    for mats, rows in PACK_GROUPS:
        group = [shards[name].reshape(-1, PACK_W) for name, _, _, _ in mats]
        used = sum(p.shape[0] for p in group)
        parts += group + [jnp.zeros((rows - used, PACK_W), group[0].dtype)]
    return jnp.concatenate(parts, axis=0)


def unpack_shards(flat):
    out = {}
    for gi, (mats, _) in enumerate(PACK_GROUPS):
        r0 = _group_start(gi)
        for name, kind, k, n in mats:
            rows = _shard_rows(k, n)
            shape = (k, n // N_CHIPS) if kind == "col" else (k // N_CHIPS, n)
            out[name] = flat[r0:r0 + rows].reshape(shape)
            r0 += rows
    return out


def pack_full(full):
    parts = []
    for mats, rows in PACK_GROUPS:
        group = []
        for name, kind, k, n in mats:
            m = full[name]
            if kind == "col":
                m = m.reshape(k, N_CHIPS, n // N_CHIPS).transpose(1, 0, 2)
            group.append(m.reshape(N_CHIPS, -1, PACK_W))
        used = sum(p.shape[1] for p in group)
        parts += group + [jnp.zeros((N_CHIPS, rows - used, PACK_W), group[0].dtype)]
    return jnp.concatenate(parts, axis=1)


def unpack_full(stacked, gi):
    out, r0 = {}, 0
    for name, kind, k, n in PACK_GROUPS[gi][0]:
        rows = _shard_rows(k, n)
        seg = stacked[:, r0:r0 + rows]
        if kind == "col":
            out[name] = seg.reshape(N_CHIPS, k, n // N_CHIPS).transpose(1, 0, 2).reshape(k, n)
        else:
            out[name] = seg.reshape(k, n)
        r0 += rows
    return out


def pack_vecs(vecs, loss=None):
    rows = []
    for r in range(VEC_ROWS):
        items = [(off, vecs[name]) for name, rr, off, _ in VECS if rr == r]
        if loss is not None and r == LOSS_SLOT[0]:
            items.append((LOSS_SLOT[1], loss.reshape(1)))
        pos, parts = 0, []
        for off, v in sorted(items, key=lambda t: t[0]):
            assert off == pos
            parts.append(v.astype(F32))
            pos += v.shape[0]
        parts.append(jnp.zeros((PACK_W - pos,), F32))
        rows.append(jnp.concatenate(parts))
    return jnp.stack(rows)


def unpack_vecs(block):
    return {name: block[r, off:off + n] for name, r, off, n in VECS}


def _me():
    return lax.axis_index("x"), lax.axis_index("y"), lax.axis_index("c")


OTHER_CHIPS = ((1, 0), (0, 1), (1, 1))


def _remote(src, dst, send_sem, recv_sem, to):
    return pltpu.make_async_remote_copy(src_ref=src, dst_ref=dst, send_sem=send_sem, recv_sem=recv_sem,
                                        device_id=to, device_id_type=MESH)


def gather_weights(block):
    def body(in_ref, out_ref, send_sems, recv_sems):
        send, forward, finish = _gather_steps(in_ref, out_ref, send_sems, recv_sems)
        send()
        forward()
        finish()

    hbm = pl.BlockSpec(memory_space=pltpu.HBM)
    others = pl.pallas_call(
        body, name="gather_weights",
        out_shape=jax.ShapeDtypeStruct((N_CHIPS,) + block.shape, block.dtype),
        in_specs=[hbm], out_specs=hbm,
        scratch_shapes=list(GATHER_SEMS),
    )(block)
    return place_own_block(others, block)


GATHER_SEMS = (pltpu.SemaphoreType.DMA((6,)), pltpu.SemaphoreType.DMA((6,)))


def place_own_block(others, block):
    return lax.dynamic_update_slice(others, block[None], (2 * lax.axis_index("x") + lax.axis_index("y"), 0, 0))


def _gather_steps(in_ref, out_ref, send_sems, recv_sems):
    hr = in_ref.shape[0] // 2
    x, y, c = _me()
    sibling = (x, y, 1 - c)
    chips = [(x ^ dx, y ^ dy) for dx, dy in OTHER_CHIPS]

    def half(px, py, pc):
        return out_ref.at[2 * px + py, pl.ds(pc * hr, hr), :]

    def firsts():
        return [_remote(in_ref.at[pl.ds(c * hr, hr), :], half(x, y, c), send_sems.at[j], recv_sems.at[j], (*chip, c))
                for j, chip in enumerate(chips)]

    def passes():
        return [_remote(half(*chip, c), half(*chip, c), send_sems.at[3 + j], recv_sems.at[3 + j], sibling)
                for j, chip in enumerate(chips)]

    def send():
        for cp in firsts():
            cp.start()

    def forward():
        for j, (chip, cp) in enumerate(zip(chips, passes())):
            _remote(half(*chip, c), half(*chip, c), send_sems.at[j], recv_sems.at[j], (*chip, c)).wait_recv()
            cp.start()

    def finish():
        for j, chip in enumerate(chips):
            _remote(half(*chip, 1 - c), half(*chip, 1 - c), send_sems.at[3 + j], recv_sems.at[3 + j], sibling).wait_recv()
        for cp in firsts() + passes():
            cp.wait_send()

    return send, forward, finish


def pair_exchange(g):
    def body(g_ref, out_ref, send_sems, recv_sems):
        x, y, c = _me()
        sibling = (x, y, 1 - c)
        copies = [_remote(g_ref.at[k, pl.ds((1 - c) * HALF_ROWS, HALF_ROWS), :], out_ref.at[k], send_sems.at[k],
                          recv_sems.at[k], sibling) for k in range(N_CHIPS)]
        for cp in copies:
            cp.start()
        for cp in copies:
            cp.wait_recv()
        for cp in copies:
            cp.wait_send()

    hbm = pl.BlockSpec(memory_space=pltpu.HBM)
    return pl.pallas_call(
        body, name="pair_exchange",
        out_shape=jax.ShapeDtypeStruct((N_CHIPS, HALF_ROWS, PACK_W), g.dtype),
        in_specs=[hbm], out_specs=hbm,
        scratch_shapes=[pltpu.SemaphoreType.DMA((N_CHIPS,)), pltpu.SemaphoreType.DMA((N_CHIPS,))],
    )(g)


def chip_exchange(part):
    def body(p_ref, out_ref, send_sems, recv_sems):
        x, y, c = _me()
        copies = [_remote(p_ref.at[2 * (x ^ dx) + (y ^ dy)], out_ref.at[j], send_sems.at[j], recv_sems.at[j],
                          (x ^ dx, y ^ dy, c)) for j, (dx, dy) in enumerate(OTHER_CHIPS)]
        for cp in copies:
            cp.start()
        for cp in copies:
            cp.wait_recv()
        for cp in copies:
            cp.wait_send()

    hbm = pl.BlockSpec(memory_space=pltpu.HBM)
    return pl.pallas_call(
        body, name="chip_exchange",
        out_shape=jax.ShapeDtypeStruct((len(OTHER_CHIPS), HALF_ROWS, PACK_W), part.dtype),
        in_specs=[hbm], out_specs=hbm,
        scratch_shapes=[pltpu.SemaphoreType.DMA((3,)), pltpu.SemaphoreType.DMA((3,))],
    )(part)


def join_halves(half):
    def body(h_ref, out_ref, send_sem, recv_sem):
        x, y, c = _me()
        cp = _remote(h_ref, out_ref.at[pl.ds(c * HALF_ROWS, HALF_ROWS), :], send_sem, recv_sem, (x, y, 1 - c))
        cp.start()
        _remote(h_ref, out_ref.at[pl.ds((1 - c) * HALF_ROWS, HALF_ROWS), :], send_sem, recv_sem, (x, y, 1 - c)).wait_recv()
        cp.wait_send()

    hbm = pl.BlockSpec(memory_space=pltpu.HBM)
    other = pl.pallas_call(
        body, name="join_halves",
        out_shape=jax.ShapeDtypeStruct((PACK_ROWS, PACK_W), half.dtype),
        in_specs=[hbm], out_specs=hbm,
        scratch_shapes=[pltpu.SemaphoreType.DMA, pltpu.SemaphoreType.DMA],
    )(half)
    return lax.dynamic_update_slice(other, half, (lax.axis_index("c") * HALF_ROWS, 0))


def sum_over_devices(block):
    def body(in_ref, out_ref, all_ref, send_sems, recv_sems):
        x, y, c = _me()
        me = 4 * x + 2 * y + c
        all_ref[me] = in_ref[...]
        copies = []
        for r in range(1, 8):
            to = (x ^ (r >> 2), y ^ ((r >> 1) & 1), c ^ (r & 1))
            copies.append(_remote(in_ref, all_ref.at[me], send_sems.at[r - 1], recv_sems.at[r - 1], to))
        for cp in copies:
            cp.start()
        for r in range(1, 8):
            frm = (x ^ (r >> 2), y ^ ((r >> 1) & 1), c ^ (r & 1))
            _remote(in_ref, all_ref.at[4 * frm[0] + 2 * frm[1] + frm[2]], send_sems.at[r - 1], recv_sems.at[r - 1],
                    frm).wait_recv()
        for cp in copies:
            cp.wait_send()
        acc = all_ref[0]
        for d in range(1, 8):
            acc = acc + all_ref[d]
        out_ref[...] = acc

    vmem = pl.BlockSpec(memory_space=pltpu.VMEM)
    return pl.pallas_call(
        body, name="sum_over_devices",
        out_shape=jax.ShapeDtypeStruct(block.shape, F32),
        in_specs=[vmem], out_specs=vmem,
        scratch_shapes=[pltpu.VMEM((8,) + block.shape, F32), pltpu.SemaphoreType.DMA((7,)), pltpu.SemaphoreType.DMA((7,))],
    )(block)


SUM_ROWS = 272


def pair_sum(g, got, core):
    steps = HALF_ROWS // SUM_ROWS

    def body(s_ref, g_ref, r_ref, o_ref, ob_ref):
        t = g_ref[...] + r_ref[...]
        o_ref[...] = t
        ob_ref[...] = t.astype(BF16)

    blk = pl.BlockSpec((1, SUM_ROWS, PACK_W), lambda k, i, s: (k, i, 0))
    return pl.pallas_call(
        body, name="pair_sum",
        grid_spec=pltpu.PrefetchScalarGridSpec(
            num_scalar_prefetch=1, grid=(N_CHIPS, steps),
            in_specs=[pl.BlockSpec((1, SUM_ROWS, PACK_W), lambda k, i, s: (k, s[0] * steps + i, 0)), blk],
            out_specs=[blk, blk]),
        out_shape=[jax.ShapeDtypeStruct((N_CHIPS, HALF_ROWS, PACK_W), F32),
                   jax.ShapeDtypeStruct((N_CHIPS, HALF_ROWS, PACK_W), BF16)],
        compiler_params=_params(("parallel", "parallel"), 8 * SUM_ROWS * PACK_W * 4),
    )(core, g, got)


def chip_sum(part, got, chip):
    steps = HALF_ROWS // SUM_ROWS

    def body(s_ref, p_ref, a_ref, b_ref, c_ref, o_ref):
        o_ref[...] = ((p_ref[0] + a_ref[0].astype(F32)) + b_ref[0].astype(F32)) + c_ref[0].astype(F32)

    def got_spec(j):
        return pl.BlockSpec((1, SUM_ROWS, PACK_W), lambda i, s: (j, i, 0))

    return pl.pallas_call(
        body, name="chip_sum",
        grid_spec=pltpu.PrefetchScalarGridSpec(
            num_scalar_prefetch=1, grid=(steps,),
            in_specs=[pl.BlockSpec((1, SUM_ROWS, PACK_W), lambda i, s: (s[0], i, 0)), got_spec(0), got_spec(1), got_spec(2)],
            out_specs=pl.BlockSpec((SUM_ROWS, PACK_W), lambda i, s: (i, 0))),
        out_shape=jax.ShapeDtypeStruct((HALF_ROWS, PACK_W), F32),
        compiler_params=_params(("parallel",), 10 * SUM_ROWS * PACK_W * 4),
    )(chip, part, got, got, got)


def reduce_scatter(g):
    x, y, c = _me()
    part, part_bf16 = pair_sum(g, pair_exchange(g), jnp.reshape(c, (1,)).astype(I32))
    half = chip_sum(part, chip_exchange(part_bf16), jnp.reshape(2 * x + y, (1,)).astype(I32))
    return join_halves(half)


def adamw(w, g, m, v, *, name):
    rows, cols = w.shape
    tm = _tile(rows, 256)
    c1 = 1.0 - ADAM_B1 ** ADAM_STEP
    c2 = 1.0 - ADAM_B2 ** ADAM_STEP

    def body(w_ref, g_ref, m_ref, v_ref, d_ref, mo_ref, vo_ref):
        gv = g_ref[...]
        mn = ADAM_B1 * m_ref[...] + (1.0 - ADAM_B1) * gv
        vn = ADAM_B2 * v_ref[...] + (1.0 - ADAM_B2) * (gv * gv)
        d_ref[...] = -ADAM_LR * ((mn / c1) / (jnp.sqrt(vn / c2) + ADAM_EPS) + ADAM_WD * w_ref[...])
        mo_ref[...] = mn
        vo_ref[...] = vn

    blk = pl.BlockSpec((tm, cols), lambda i: (i, 0))
    shape = jax.ShapeDtypeStruct(w.shape, F32)
    return _pallas(
        body, name=name, grid=(rows // tm,),
        in_specs=[blk] * 4, out_specs=[blk] * 3, out_shape=[shape] * 3,
        compiler_params=_params(("parallel",), 16 * tm * cols * 4),
    )(w, g, m, v)


WEIGHTS = ("l0_norm", "l0_w_in", "l0_w_out", "l1_norm", "l1_w_in", "l1_q_a_norm", "l1_w_uq", "l1_kv_a_norm", "l1_w_ukv",
           "l1_q_head_norm", "l1_k_head_norm", "l1_w_out", "l2_norm", "l2_w_in", "l2_q_head_norm", "l2_k_head_norm",
           "l2_sinks", "l2_w_out", "l3_norm", "l3_w_in", "l3_w_out")


def kernel(x, l0_norm, l0_w_in, l0_w_out, l1_norm, l1_w_in, l1_q_a_norm, l1_w_uq, l1_kv_a_norm, l1_w_ukv, l1_q_head_norm, l1_k_head_norm, l1_w_out, l2_norm, l2_w_in, l2_q_head_norm, l2_k_head_norm, l2_sinks, l2_w_out, l3_norm, l3_w_in, l3_w_out, loss_target, m_l0_norm, m_l0_w_in, m_l0_w_out, m_l1_norm, m_l1_w_in, m_l1_q_a_norm, m_l1_w_uq, m_l1_kv_a_norm, m_l1_w_ukv, m_l1_q_head_norm, m_l1_k_head_norm, m_l1_w_out, m_l2_norm, m_l2_w_in, m_l2_q_head_norm, m_l2_k_head_norm, m_l2_sinks, m_l2_w_out, m_l3_norm, m_l3_w_in, m_l3_w_out, v_l0_norm, v_l0_w_in, v_l0_w_out, v_l1_norm, v_l1_w_in, v_l1_q_a_norm, v_l1_w_uq, v_l1_kv_a_norm, v_l1_w_ukv, v_l1_q_head_norm, v_l1_k_head_norm, v_l1_w_out, v_l2_norm, v_l2_w_in, v_l2_q_head_norm, v_l2_k_head_norm, v_l2_sinks, v_l2_w_out, v_l3_norm, v_l3_w_in, v_l3_w_out):
    given = dict(locals())
    w = {n: given[n] for n in WEIGHTS}
    m = {n: given["m_" + n] for n in WEIGHTS}
    v = {n: given["v_" + n] for n in WEIGHTS}
    mat_names = [t[0] for t in MATS]
    vec_names = [t[0] for t in VECS]

    packed = pack_shards({n: w[n] for n in mat_names}).astype(BF16)
    part = lambda gi: packed[_group_start(gi):_group_start(gi + 1)]
    full = unpack_full(gather_weights(part(0)), 0)
    full.update({n: w[n] for n in vec_names})
    late = [(part(gi), lambda gathered, gi=gi: unpack_full(gathered, gi)) for gi in (1, 2)]
    loss_tile, grad_x, grads = local_step(x[0], loss_target[0], full, late)

    block = reduce_scatter(pack_full({n: grads[n] for n in mat_names}))
    g = unpack_shards(block)
    vec_sum = sum_over_devices(pack_vecs({n: grads[n] for n in vec_names}, loss=loss_tile[0, 0]))
    loss = vec_sum[LOSS_SLOT[0], LOSS_SLOT[1]]

    delta, new_m, new_v = {}, {}, {}
    for n in mat_names:
        delta[n], new_m[n], new_v[n] = adamw(w[n], g[n], m[n], v[n], name=f"adamw_{n}")
    dv, mv, vv = adamw(pack_vecs(w), vec_sum, pack_vecs(m), pack_vecs(v), name="adamw_vecs")
    g.update(unpack_vecs(vec_sum))
    delta.update(unpack_vecs(dv))
    new_m.update(unpack_vecs(mv))
    new_v.update(unpack_vecs(vv))
    return (loss, grad_x[None], *[g[n] for n in WEIGHTS], *[delta[n] for n in WEIGHTS],
            *[new_m[n] for n in WEIGHTS], *[new_v[n] for n in WEIGHTS])
```

```python
import math

import jax
import jax.numpy as jnp
from jax import lax
from jax.experimental import pallas as pl
from jax.experimental.pallas import tpu as pltpu

F32 = jnp.float32
BF16 = jnp.bfloat16
I32 = jnp.int32
MESH = pl.DeviceIdType.MESH

NORM_EPS = 1e-6
D_MODEL = 1024
HEAD64 = 64
LANES = 128
BLK = 128
MLA_HEADS = 8
MLA_QK = 192
MLA_PAD = 256
ROPE_THETA = 10000.0
SWA_HEADS = 16
SWA_KV = 4
VMEM_CAP = 56 * 1024 * 1024
MATMUL_TILE_BYTES = 8 * 1024 * 1024

ADAM_LR, ADAM_B1, ADAM_B2, ADAM_EPS, ADAM_WD, ADAM_STEP = 0.001, 0.9, 0.999, 1e-08, 0.01, 10

NT = (((1,), (1,)), ((), ()))
NN = (((1,), (0,)), ((), ()))
TN = (((0,), (0,)), ((), ()))


def _dot(a, b, dims=NN):
    return lax.dot_general(a, b, dims, preferred_element_type=F32)


def _tile(n, pref):
    for t in (pref, 512, 256, 128):
        if t <= pref and n % t == 0:
            return t
    return n


def _params(sem, vmem_bytes):
    limit = int(min(max(2 * vmem_bytes, 24 * 1024 * 1024), VMEM_CAP))
    return pltpu.CompilerParams(dimension_semantics=sem, vmem_limit_bytes=limit)


def _in_hbm(s):
    return pltpu.HBM(s.shape, s.dtype) if len(s.shape) >= 2 else s


def _pallas(*args, out_shape, **kwargs):
    out_shape = [_in_hbm(s) for s in out_shape] if isinstance(out_shape, (list, tuple)) else _in_hbm(out_shape)
    call = pl.pallas_call(*args, out_shape=out_shape, **kwargs)

    def run(*operands):
        return call(*[pltpu.with_memory_space_constraint(a, pltpu.HBM) if a.ndim >= 2 else a for a in operands])

    return run


def _split(v):
    hi = v.astype(BF16)
    return hi, (v - hi.astype(F32)).astype(BF16)


def _dot2(v, m):
    hi, lo = _split(v)
    return _dot(hi, m) + _dot(lo, m)


def _dot_split(v, m2):
    return _dot(jnp.concatenate(_split(v), axis=1), m2)


def _first_half(shape):
    return lax.broadcasted_iota(I32, shape, 1) < HEAD64


def _sigmoid(g):
    return 1.0 / (1.0 + jnp.exp(-g))


def matmul(a, b, mode, *, name, out_dtype=F32, add=None, b_cols=None, tm=512, tn=1024):
    if mode == "nn":
        (M, K), (K2, N) = a.shape, b.shape
        if b_cols is not None:
            N = b_cols[1]
    elif mode == "nt":
        (M, K), (N, K2) = a.shape, b.shape
    else:
        (K, M), (K2, N) = a.shape, b.shape
    assert K == K2, (a.shape, b.shape, mode)
    tm, tn = _tile(M, tm), _tile(N, tn)
    while K * tm * a.dtype.itemsize > MATMUL_TILE_BYTES:
        tm //= 2
    while K * tn * b.dtype.itemsize > MATMUL_TILE_BYTES:
        tn //= 2
    dims = {"nn": NN, "nt": NT, "tn": TN}[mode]
    n_in = 2 if add is None else 3

    def body(*refs):
        a_ref, b_ref = refs[:2]
        r = _dot(a_ref[...].astype(BF16), b_ref[...].astype(BF16), dims)
        if add is not None:
            r = r + refs[2][...]
        for o_ref in refs[n_in:]:
            o_ref[...] = r.astype(o_ref.dtype)

    a_spec = pl.BlockSpec((K, tm), lambda i, j: (0, i)) if mode == "tn" else pl.BlockSpec((tm, K), lambda i, j: (i, 0))
    jb = 0 if b_cols is None else b_cols[0] // tn
    assert b_cols is None or (mode == "nn" and b_cols[0] % tn == 0)
    b_spec = pl.BlockSpec((tn, K), lambda i, j: (j, 0)) if mode == "nt" else pl.BlockSpec((K, tn), lambda i, j: (0, jb + j))
    o_spec = pl.BlockSpec((tm, tn), lambda i, j: (i, j))
    in_specs, args = [a_spec, b_spec], [a, b]
    if add is not None:
        in_specs.append(o_spec)
        args.append(add)
    vm = 2 * (tm * K * a.dtype.itemsize + K * tn * b.dtype.itemsize) + 5 * tm * tn * 4
    out_dtypes = list(out_dtype) if isinstance(out_dtype, (tuple, list)) else [out_dtype]
    res = _pallas(
        body, name=name, grid=(M // tm, N // tn),
        in_specs=in_specs, out_specs=[o_spec] * len(out_dtypes),
        out_shape=[jax.ShapeDtypeStruct((M, N), d) for d in out_dtypes],
        compiler_params=_params(("parallel", "parallel"), vm),
    )(*args)
    return res[0] if len(out_dtypes) == 1 else res


def rmsnorm_fwd(x, g, *, col_off, width, out_dtype, name, tm=256):
    S = x.shape[0]
    assert col_off % width == 0
    cb = col_off // width
    tm = _tile(S, tm)

    def body(x_ref, g_ref, o_ref):
        v = x_ref[...]
        r = lax.rsqrt(jnp.mean(v * v, axis=1, keepdims=True) + NORM_EPS)
        o_ref[...] = (v * r * g_ref[...]).astype(out_dtype)

    return _pallas(
        body, name=name, grid=(S // tm,),
        in_specs=[pl.BlockSpec((tm, width), lambda i: (i, cb)), pl.BlockSpec((1, width), lambda i: (0, 0))],
        out_specs=pl.BlockSpec((tm, width), lambda i: (i, 0)),
        out_shape=jax.ShapeDtypeStruct((S, width), out_dtype),
        compiler_params=_params(("parallel",), 4 * tm * width * 4),
    )(x, g.reshape(1, width))


def rmsnorm_bwd(x, g, dh, *, col_off, width, out_dtype, name, res=None, tm=256):
    S = x.shape[0]
    cb = col_off // width
    tm = _tile(S, tm)

    def body(*refs):
        if res is None:
            x_ref, g_ref, dh_ref, dx_ref, dg_ref = refs
        else:
            x_ref, g_ref, dh_ref, res_ref, dx_ref, dg_ref = refs

        @pl.when(pl.program_id(0) == 0)
        def _():
            dg_ref[...] = jnp.zeros_like(dg_ref)

        v = x_ref[...]
        dhv = dh_ref[...].astype(F32)
        r = lax.rsqrt(jnp.mean(v * v, axis=1, keepdims=True) + NORM_EPS)
        y = v * r
        dy = dhv * g_ref[...]
        dx = r * (dy - y * jnp.mean(dy * y, axis=1, keepdims=True))
        if res is not None:
            dx = dx + res_ref[...]
        dx_ref[...] = dx.astype(out_dtype)
        dg_ref[...] += jnp.sum(dhv * y, axis=0, keepdims=True)

    row = pl.BlockSpec((tm, width), lambda i: (i, 0))
    in_specs = [pl.BlockSpec((tm, width), lambda i: (i, cb)), pl.BlockSpec((1, width), lambda i: (0, 0)), row]
    args = [x, g.reshape(1, width), dh]
    if res is not None:
        in_specs.append(row)
        args.append(res)
    return _pallas(
        body, name=name, grid=(S // tm,),
        in_specs=in_specs,
        out_specs=[row, pl.BlockSpec((1, width), lambda i: (0, 0))],
        out_shape=[jax.ShapeDtypeStruct((S, width), out_dtype), jax.ShapeDtypeStruct((1, width), F32)],
        compiler_params=_params(("arbitrary",), 8 * tm * width * 4),
    )(*args)


def _group_ones():
    r = lax.broadcasted_iota(I32, (LANES, LANES), 0) // HEAD64
    c = lax.broadcasted_iota(I32, (LANES, LANES), 1) // HEAD64
    return (r == c).astype(BF16)


def _segmean64(v):
    return _dot2(v, _group_ones()) * (1.0 / HEAD64)


def headnorm_fwd(x, g64, *, col_off, width, name, tm=512):
    S = x.shape[0]
    cb = col_off // LANES
    tm = _tile(S, tm)

    def body(x_ref, g_ref, o_ref):
        v = x_ref[...]
        r = lax.rsqrt(_segmean64(v * v) + NORM_EPS)
        o_ref[...] = (v * r * g_ref[...]).astype(BF16)

    return _pallas(
        body, name=name, grid=(S // tm, width // LANES),
        in_specs=[pl.BlockSpec((tm, LANES), lambda i, j: (i, cb + j)), pl.BlockSpec((1, LANES), lambda i, j: (0, 0))],
        out_specs=pl.BlockSpec((tm, LANES), lambda i, j: (i, j)),
        out_shape=jax.ShapeDtypeStruct((S, width), BF16),
        compiler_params=_params(("parallel", "parallel"), 8 * tm * LANES * 4),
    )(x, jnp.tile(g64, 2).reshape(1, LANES))


def headnorm_bwd(x, g64, dh, *, col_off, width, name, tm=512):
    S = x.shape[0]
    cb = col_off // LANES
    tm = _tile(S, tm)

    def body(x_ref, g_ref, dh_ref, dx_ref, dg_ref):
        @pl.when((pl.program_id(0) == 0) & (pl.program_id(1) == 0))
        def _():
            dg_ref[...] = jnp.zeros_like(dg_ref)

        v = x_ref[...]
        dhv = dh_ref[...]
        r = lax.rsqrt(_segmean64(v * v) + NORM_EPS)
        y = v * r
        dy = dhv * g_ref[...]
        dx_ref[...] = (r * (dy - y * _segmean64(dy * y))).astype(BF16)
        dg_ref[...] += jnp.sum(dhv * y, axis=0, keepdims=True)

    return _pallas(
        body, name=name, grid=(width // LANES, S // tm),
        in_specs=[pl.BlockSpec((tm, LANES), lambda j, i: (i, cb + j)), pl.BlockSpec((1, LANES), lambda j, i: (0, 0)),
                  pl.BlockSpec((tm, LANES), lambda j, i: (i, j))],
        out_specs=[pl.BlockSpec((tm, LANES), lambda j, i: (i, j)), pl.BlockSpec((1, LANES), lambda j, i: (0, 0))],
        out_shape=[jax.ShapeDtypeStruct((S, width), BF16), jax.ShapeDtypeStruct((1, LANES), F32)],
        compiler_params=_params(("arbitrary", "arbitrary"), 10 * tm * LANES * 4),
    )(x, jnp.tile(g64, 2).reshape(1, LANES), dh)


def gate_fwd(o, proj, *, gate_off, name, tm=256):
    S, W = o.shape
    cb = gate_off // W
    tm = _tile(S, tm)

    def body(o_ref, g_ref, out_ref):
        g = g_ref[...]
        out_ref[...] = (o_ref[...] * (g * _sigmoid(g))).astype(BF16)

    return _pallas(
        body, name=name, grid=(S // tm,),
        in_specs=[pl.BlockSpec((tm, W), lambda i: (i, 0)), pl.BlockSpec((tm, W), lambda i: (i, cb))],
        out_specs=pl.BlockSpec((tm, W), lambda i: (i, 0)),
        out_shape=jax.ShapeDtypeStruct((S, W), BF16),
        compiler_params=_params(("parallel",), 6 * tm * W * 4),
    )(o, proj)


def gate_bwd(dog, o, proj, *, gate_off, name, tm=256):
    S, W = o.shape
    cb = gate_off // W
    tm = _tile(S, tm)

    def body(d_ref, o_ref, g_ref, do_ref, dg_ref):
        g = g_ref[...]
        d = d_ref[...]
        s = _sigmoid(g)
        do_ref[...] = d * (g * s)
        dg_ref[...] = (d * o_ref[...] * (s * (1.0 + g * (1.0 - s)))).astype(BF16)

    row = pl.BlockSpec((tm, W), lambda i: (i, 0))
    return _pallas(
        body, name=name, grid=(S // tm,),
        in_specs=[row, row, pl.BlockSpec((tm, W), lambda i: (i, cb))],
        out_specs=[row, row],
        out_shape=[jax.ShapeDtypeStruct((S, W), F32), jax.ShapeDtypeStruct((S, W), BF16)],
        compiler_params=_params(("parallel",), 10 * tm * W * 4),
    )(dog, o, proj)


def loss_head(y, target, *, name, tm=256):
    S, W = y.shape
    tm = _tile(S, tm)
    n = S // tm

    def body(y_ref, t_ref, dy_ref, l_ref, acc_ref):
        i = pl.program_id(0)

        @pl.when(i == 0)
        def _():
            acc_ref[...] = jnp.zeros_like(acc_ref)

        e = y_ref[...] - t_ref[...]
        dy_ref[...] = e * (1.0 / W)
        acc_ref[...] += jnp.sum(e * e, axis=0, keepdims=True)

        @pl.when(i == n - 1)
        def _():
            l_ref[...] = jnp.full(l_ref.shape, (0.5 / W) * jnp.sum(acc_ref[...]), F32)

    row = pl.BlockSpec((tm, W), lambda i: (i, 0))
    return _pallas(
        body, name=name, grid=(n,),
        in_specs=[row, row],
        out_specs=[row, pl.BlockSpec((8, LANES), lambda i: (0, 0))],
        out_shape=[jax.ShapeDtypeStruct((S, W), F32), jax.ShapeDtypeStruct((8, LANES), F32)],
        scratch_shapes=[pltpu.VMEM((1, W), F32)],
        compiler_params=_params(("arbitrary",), 8 * tm * W * 4),
    )(y, target)


def _stack_heads(t, zero):
    first = _first_half(t.shape)
    return jnp.concatenate([jnp.where(first, t, zero), jnp.where(first, zero, t)], axis=0)


def _sb_weights(qs, ks, mask, upper, rss):
    zs = [_dot(q, k, NT) for q, k in zip(qs, ks)]
    sps = [jnp.maximum(z, 0.0) + jnp.log(1.0 + jnp.exp(-jnp.abs(z))) for z in zs]
    gs = [z - sp for z, sp in zip(zs, sps)]
    if mask is not None:
        sps = [jnp.where(mask, sp, 0.0) for sp in sps]
    cums = [_dot_split(sp, upper) for sp in sps]
    avs = [jnp.exp(g - (cum + rs)) for g, cum, rs in zip(gs, cums, rss)]
    if mask is not None:
        avs = [jnp.where(mask, a, 0.0) for a in avs]
    return avs, sps, gs


def _sb_consts():
    row = lax.broadcasted_iota(I32, (BLK, BLK), 0)
    col = lax.broadcasted_iota(I32, (BLK, BLK), 1)
    diag = col < row
    return row, col, jnp.concatenate([diag, diag], axis=0)


SB_DEAD = 88.0


def _sb_walk_left(block, i, carry):
    def least(c):
        m = c[0][1]
        for pair in c[1:]:
            m = jnp.minimum(m, pair[1])
        return jnp.min(m)

    def cond(state):
        jj, _, low = state
        return (jj < i) & (low < SB_DEAD)

    def body(state):
        jj, c, _ = state
        c = block(i - 1 - jj, c, None)
        return jj + 1, c, least(c)

    return lax.while_loop(cond, body, (jnp.int32(0), carry, least(carry)))[1]


SB_W = 1024


def sb_attn_fwd(qkv, *, name, pairs=8, ride=None):
    S = qkv.shape[0]
    W = SB_W
    PW = pairs * LANES
    ngrp, nq = W // PW, S // BLK
    assert ride is None or (ngrp == 1 and nq >= 3)

    def body(*refs):
        if ride is None:
            q_ref, k_ref, v_ref, o_ref = refs
        else:
            q_ref, k_ref, v_ref, o_ref = refs[0], refs[1], refs[2], refs[4]
            send, forward, finish = _gather_steps(refs[3], refs[5], refs[6], refs[7])
        i = pl.program_id(1)
        if ride is not None:
            pl.when(i == 0)(send)
            pl.when(i == nq // 2)(forward)
        row, col, diag = _sb_consts()
        upper = jnp.tile((row > col).astype(BF16), (2, 1))
        zero = jnp.zeros((BLK, LANES), BF16)
        qs = [_stack_heads(q_ref[:, p * LANES:(p + 1) * LANES] * 0.125, zero) for p in range(pairs)]

        def block(j, carry, mask):
            rows = pl.ds(pl.multiple_of(j * BLK, BLK), BLK)
            cols = [slice(p * LANES, (p + 1) * LANES) for p in range(pairs)]
            avs, sps, _ = _sb_weights(qs, [k_ref[rows, c] for c in cols], mask, upper, [c[1] for c in carry])
            abs_ = [a.astype(BF16) for a in avs]
            outs = [_dot(jnp.concatenate([ab[:BLK], ab[BLK:]], axis=1), _stack_heads(v_ref[rows, c], zero))
                    for ab, c in zip(abs_, cols)]
            return tuple((carry[p][0] + outs[p], carry[p][1] + jnp.sum(sps[p], axis=1, keepdims=True))
                         for p in range(pairs))

        init = tuple((jnp.zeros((BLK, LANES), F32), jnp.zeros((2 * BLK, 1), F32)) for _ in range(pairs))
        carry = _sb_walk_left(block, i, block(i, init, diag))
        for p in range(pairs):
            o_ref[:, p * LANES:(p + 1) * LANES] = carry[p][0]
        if ride is not None:
            pl.when(i == nq - 1)(finish)

    once = pl.Buffered(1)
    hbm = pl.BlockSpec(memory_space=pltpu.HBM)
    in_specs = [pl.BlockSpec((BLK, PW), lambda p, i: (i, p)),
                pl.BlockSpec((S, PW), lambda p, i: (0, ngrp + p), pipeline_mode=once),
                pl.BlockSpec((S, PW), lambda p, i: (0, 2 * ngrp + p), pipeline_mode=once)]
    out_specs = [pl.BlockSpec((BLK, PW), lambda p, i: (i, p))]
    out_shape = [jax.ShapeDtypeStruct((S, W), F32)]
    args = [qkv, qkv, qkv]
    if ride is not None:
        in_specs.append(hbm)
        out_specs.append(hbm)
        out_shape.append(jax.ShapeDtypeStruct((N_CHIPS,) + ride.shape, ride.dtype))
        args.append(ride)
    res = _pallas(
        body, name=name, grid=(ngrp, nq),
        in_specs=in_specs, out_specs=out_specs, out_shape=out_shape,
        scratch_shapes=[] if ride is None else list(GATHER_SEMS),
        compiler_params=_params(("parallel", "arbitrary"), 2 * S * PW * 2 + 16 * BLK * PW * 4),
    )(*args)
    return res[0] if ride is None else (res[0], place_own_block(res[1], ride))


def sb_attn_bwd(qkv, o, do, *, name, pairs=4, ride=None):
    S = qkv.shape[0]
    W = SB_W
    PW = pairs * LANES
    ngrp, nq = W // PW, S // BLK

    def body(*refs):
        q_ref, k_ref, v_ref, o_ref, do_ref = refs[:5]
        if ride is None:
            dq_ref, dk_ref, dv_ref = refs[5:]
        else:
            dq_ref, dk_ref, dv_ref = refs[6:9]
            send, finish = _chip_exchange_steps(refs[5], refs[9], refs[10], refs[11])
            pl.when((pl.program_id(0) == 0) & (pl.program_id(1) == 0))(send)
        i = pl.program_id(1)

        @pl.when(i == 0)
        def _():
            dk_ref[...] = jnp.zeros_like(dk_ref)
            dv_ref[...] = jnp.zeros_like(dv_ref)

        row, col, diag = _sb_consts()
        upper = jnp.tile((row > col).astype(BF16), (2, 1))
        upper_incl = jnp.tile((row >= col).astype(BF16), (2, 1))
        first = _first_half((BLK, LANES))
        zero = jnp.zeros((BLK, LANES), BF16)
        qs, dos, tots = [], [], []
        for p in range(pairs):
            cols = slice(p * LANES, (p + 1) * LANES)
            qs.append(_stack_heads(q_ref[:, cols] * 0.125, zero))
            dob = do_ref[:, cols].astype(BF16)
            dos.append(_stack_heads(dob, zero))
            prod = dob.astype(F32) * o_ref[:, cols]
            tots.append(jnp.concatenate([jnp.sum(jnp.where(first, prod, 0.0), axis=1, keepdims=True),
                                         jnp.sum(jnp.where(first, 0.0, prod), axis=1, keepdims=True)], axis=0))

        def block(j, carry, mask):
            rows = pl.ds(pl.multiple_of(j * BLK, BLK), BLK)
            P = range(pairs)
            cols = [slice(p * LANES, (p + 1) * LANES) for p in P]
            ks = [k_ref[rows, c] for c in cols]
            das = [_dot(dos[p], v_ref[rows, cols[p]], NT) for p in P]
            avs, sps, gs = _sb_weights(qs, ks, mask, upper, [c[1] for c in carry])
            abs_ = [a.astype(BF16) for a in avs]
            es = [ab.astype(F32) * da for ab, da in zip(abs_, das)]
            sufs = [_dot_split(e, upper_incl) for e in es]
            lefts = [tots[p] - (sufs[p] + carry[p][2]) for p in P]
            dzs = [es[p] - jnp.exp(gs[p]) * (es[p] + lefts[p]) for p in P]
            if mask is not None:
                dzs = [jnp.where(mask, dz, 0.0) for dz in dzs]
            dzbs = [dz.astype(BF16) for dz in dzs]
            dks = [_dot(dzbs[p], qs[p], TN) for p in P]
            dvs = [_dot(abs_[p], dos[p], TN) for p in P]
            dqs = [_dot(jnp.concatenate([dzbs[p][:BLK], dzbs[p][BLK:]], axis=1), _stack_heads(ks[p], zero)) for p in P]
            for p in P:
                dk_ref[rows, cols[p]] += dks[p]
                dv_ref[rows, cols[p]] += dvs[p]
            return tuple((carry[p][0] + dqs[p], carry[p][1] + jnp.sum(sps[p], axis=1, keepdims=True),
                          carry[p][2] + jnp.sum(es[p], axis=1, keepdims=True)) for p in P)

        col0 = jnp.zeros((2 * BLK, 1), F32)
        init = tuple((jnp.zeros((BLK, LANES), F32), col0, col0) for _ in range(pairs))
        carry = _sb_walk_left(block, i, block(i, init, diag))
        for p in range(pairs):
            dq_ref[:, p * LANES:(p + 1) * LANES] = (carry[p][0] * 0.125).astype(BF16)
        if ride is not None:
            pl.when((pl.program_id(0) == ngrp - 1) & (i == nq - 1))(finish)

    once = pl.Buffered(1)
    tile = pl.BlockSpec((BLK, PW), lambda p, i: (i, p))
    full = pl.BlockSpec((S, PW), lambda p, i: (0, p), pipeline_mode=once)
    shape = jax.ShapeDtypeStruct((S, W), F32)
    hbm = pl.BlockSpec(memory_space=pltpu.HBM)
    in_specs = [tile,
                pl.BlockSpec((S, PW), lambda p, i: (0, ngrp + p), pipeline_mode=once),
                pl.BlockSpec((S, PW), lambda p, i: (0, 2 * ngrp + p), pipeline_mode=once),
                tile, tile]
    out_specs = [tile, full, full]
    out_shape = [jax.ShapeDtypeStruct((S, W), BF16), shape, shape]
    args = [qkv, qkv, qkv, o, do]
    if ride is not None:
        in_specs.append(hbm)
        out_specs.append(hbm)
        out_shape.append(jax.ShapeDtypeStruct((len(OTHER_CHIPS),) + ride.shape[1:], ride.dtype))
        args.append(ride)
    order = ("parallel", "arbitrary") if ride is None else ("arbitrary", "arbitrary")
    return _pallas(
        body, name=name, grid=(ngrp, nq),
        in_specs=in_specs, out_specs=out_specs, out_shape=out_shape,
        scratch_shapes=[] if ride is None else list(CHIP_SEMS),
        compiler_params=_params(order, 2 * S * PW * 2 + 2 * S * PW * 4 + 16 * BLK * PW * 4),
    )(*args)


FB = 256


def flash_fwd(qn, kn, vb, *, name, heads=4, ride=None):
    S = qn.shape[0]
    H, DK, DV = MLA_HEADS, MLA_PAD, LANES
    nq, ngrp = S // FB, H // heads
    scale = 1.0 / math.sqrt(MLA_QK)
    assert ride is None or (ngrp >= 2 and nq >= 2)

    def body(*refs):
        if ride is None:
            q_ref, k_ref, v_ref, o_ref, lse_ref = refs
        else:
            q_ref, k_ref, v_ref, o_ref, lse_ref = refs[0], refs[1], refs[2], refs[4], refs[5]
            send, forward, finish = _gather_steps(refs[3], refs[6], refs[7], refs[8])
        g, i = pl.program_id(0), pl.program_id(1)
        if ride is not None:
            pl.when((g == 0) & (i == 0))(send)
            pl.when((g == 1) & (i == 0))(forward)
        diag = lax.broadcasted_iota(I32, (FB, FB), 1) <= lax.broadcasted_iota(I32, (FB, FB), 0)
        qs = [q_ref[:, h * DK:(h + 1) * DK] for h in range(heads)]

        def block(j, carry, mask):
            rows = pl.ds(pl.multiple_of(j * FB, FB), FB)
            H = range(heads)
            ss = [_dot(qs[h], k_ref[rows, h * DK:(h + 1) * DK], NT) * scale for h in H]
            if mask is not None:
                ss = [jnp.where(mask, s, -1e30) for s in ss]
            ms = [jnp.maximum(carry[h][1], jnp.max(ss[h], axis=1, keepdims=True)) for h in H]
            ps = [jnp.exp(ss[h] - ms[h]) for h in H]
            ws = [jnp.exp(carry[h][1] - ms[h]) for h in H]
            pvs = [_dot(ps[h].astype(BF16), v_ref[rows, h * DV:(h + 1) * DV]) for h in H]
            return tuple((carry[h][0] * ws[h] + pvs[h], ms[h], carry[h][2] * ws[h] + jnp.sum(ps[h], axis=1, keepdims=True))
                         for h in H)

        init = tuple((jnp.zeros((FB, DV), F32), jnp.full((FB, 1), -1e30, F32), jnp.zeros((FB, 1), F32))
                     for _ in range(heads))
        carry = lax.fori_loop(0, i, lambda j, c: block(j, c, None), init)
        carry = block(i, carry, diag)
        for h in range(heads):
            acc, m, l = carry[h]
            o_ref[:, h * DV:(h + 1) * DV] = acc / l
            lse_ref[h] = m + jnp.log(l)
        if ride is not None:
            pl.when((g == ngrp - 1) & (i == nq - 1))(finish)

    hbm = pl.BlockSpec(memory_space=pltpu.HBM)
    in_specs = [pl.BlockSpec((FB, heads * DK), lambda g, i: (i, g)),
                pl.BlockSpec((S, heads * DK), lambda g, i: (0, g), pipeline_mode=pl.Buffered(1)),
                pl.BlockSpec((S, heads * DV), lambda g, i: (0, g), pipeline_mode=pl.Buffered(1))]
    out_specs = [pl.BlockSpec((FB, heads * DV), lambda g, i: (i, g)), pl.BlockSpec((heads, FB, 1), lambda g, i: (g, i, 0))]
    out_shape = [jax.ShapeDtypeStruct((S, H * DV), F32), jax.ShapeDtypeStruct((H, S, 1), F32)]
    args = [qn, kn, vb]
    if ride is not None:
        in_specs.append(hbm)
        out_specs.append(hbm)
        out_shape.append(jax.ShapeDtypeStruct((N_CHIPS,) + ride.shape, ride.dtype))
        args.append(ride)
    order = ("parallel", "arbitrary") if ride is None else ("arbitrary", "arbitrary")
    res = _pallas(
        body, name=name, grid=(ngrp, nq),
        in_specs=in_specs, out_specs=out_specs, out_shape=out_shape,
        scratch_shapes=[] if ride is None else list(GATHER_SEMS),
        compiler_params=_params(order, 2 * S * heads * (DK + DV) * 2 + 16 * FB * FB * 4),
    )(*args)
    return tuple(res) if ride is None else (res[0], res[1], place_own_block(res[2], ride))


def flash_bwd(qn, kn, vb, o, lse, do, *, name, heads=4):
    S = qn.shape[0]
    H, DK, DV = MLA_HEADS, MLA_PAD, LANES
    nq, ngrp = S // FB, H // heads
    scale = 1.0 / math.sqrt(MLA_QK)

    def body(q_ref, k_ref, v_ref, o_ref, lse_ref, do_ref, dq_ref, dk_ref, dv_ref):
        i = pl.program_id(1)

        @pl.when(i == 0)
        def _():
            dk_ref[...] = jnp.zeros_like(dk_ref)
            dv_ref[...] = jnp.zeros_like(dv_ref)

        diag = lax.broadcasted_iota(I32, (FB, FB), 1) <= lax.broadcasted_iota(I32, (FB, FB), 0)
        qs, dobs, deltas, lses = [], [], [], []
        for h in range(heads):
            do = do_ref[:, h * DV:(h + 1) * DV]
            qs.append(q_ref[:, h * DK:(h + 1) * DK])
            dobs.append(do.astype(BF16))
            deltas.append(jnp.sum(do * o_ref[:, h * DV:(h + 1) * DV], axis=1, keepdims=True))
            lses.append(lse_ref[h])

        def block(j, carry, mask):
            rows = pl.ds(pl.multiple_of(j * FB, FB), FB)
            H = range(heads)
            kcs = [slice(h * DK, (h + 1) * DK) for h in H]
            vcs = [slice(h * DV, (h + 1) * DV) for h in H]
            ks = [k_ref[rows, kcs[h]] for h in H]
            ss = [_dot(qs[h], ks[h], NT) for h in H]
            dps = [_dot(dobs[h], v_ref[rows, vcs[h]], NT) for h in H]
            ps = [jnp.exp(ss[h] * scale - lses[h]) for h in H]
            if mask is not None:
                ps = [jnp.where(mask, p, 0.0) for p in ps]
            pbs = [p.astype(BF16) for p in ps]
            dss = [(ps[h] * (dps[h] - deltas[h]) * scale).astype(BF16) for h in H]
            dvs = [_dot(pbs[h], dobs[h], TN) for h in H]
            dks = [_dot(dss[h], qs[h], TN) for h in H]
            dqs = [_dot(dss[h], ks[h]) for h in H]
            for h in H:
                dv_ref[rows, vcs[h]] += dvs[h]
                dk_ref[rows, kcs[h]] += dks[h]
            return tuple(carry[h] + dqs[h] for h in H)

        carry = lax.fori_loop(0, i, lambda j, c: block(j, c, None), tuple(jnp.zeros((FB, DK), F32) for _ in range(heads)))
        carry = block(i, carry, diag)
        for h in range(heads):
            dq_ref[:, h * DK:(h + 1) * DK] = carry[h]

    qtile = pl.BlockSpec((FB, heads * DK), lambda g, i: (i, g))
    otile = pl.BlockSpec((FB, heads * DV), lambda g, i: (i, g))
    once = pl.Buffered(1)
    kfull = pl.BlockSpec((S, heads * DK), lambda g, i: (0, g), pipeline_mode=once)
    vfull = pl.BlockSpec((S, heads * DV), lambda g, i: (0, g), pipeline_mode=once)
    return _pallas(
        body, name=name, grid=(ngrp, nq),
        in_specs=[qtile, kfull, vfull, otile, pl.BlockSpec((heads, FB, 1), lambda g, i: (g, i, 0)), otile],
        out_specs=[qtile, kfull, vfull],
        out_shape=[jax.ShapeDtypeStruct((S, H * DK), F32), jax.ShapeDtypeStruct((S, H * DK), F32),
                   jax.ShapeDtypeStruct((S, H * DV), F32)],
        compiler_params=_params(("parallel", "arbitrary"), S * heads * (DK + DV) * 6 + 16 * FB * FB * 4),
    )(qn, kn, vb, o, lse, do)


def _rope_tables(S):
    half = 32
    inv_freq = ROPE_THETA ** (-jnp.arange(half, dtype=F32) / half)
    ang = jnp.arange(S).astype(F32)[:, None] * inv_freq[None, :]
    cos, sin = jnp.cos(ang), jnp.sin(ang)
    ones, zeros = jnp.ones((S, LANES), F32), jnp.zeros((S, LANES), F32)
    pad = jnp.zeros((S, 64), F32)
    return (jnp.concatenate([ones, cos, cos, pad + 1.0], axis=1),
            jnp.concatenate([zeros, -sin, sin, pad], axis=1))


def _rope_partner(u):
    lane = lax.broadcasted_iota(I32, u.shape, 1)
    return jnp.where((lane % HEAD64) < 32, pltpu.roll(u, LANES - 32, 1), pltpu.roll(u, 32, 1))


def _normrope(raw, g, cos, sgn):
    r = lax.rsqrt(jnp.sum(raw * raw, axis=1, keepdims=True) * (1.0 / MLA_QK) + NORM_EPS)
    y = raw * r
    u = y * g
    pe = u[:, LANES:]
    out = jnp.concatenate([u[:, :LANES], pe * cos[:, LANES:] + _rope_partner(pe) * sgn[:, LANES:]], axis=1)
    return out, y, r


def _normrope_bwd(dout, g, cos, sgn, y, r):
    dpe = dout[:, LANES:]
    du = jnp.concatenate([dout[:, :LANES], dpe * cos[:, LANES:] + _rope_partner(dpe * sgn[:, LANES:])], axis=1)
    dy = du * g
    draw = r * (dy - y * (jnp.sum(dy * y, axis=1, keepdims=True) * (1.0 / MLA_QK)))
    return draw, jnp.sum(du * y, axis=0, keepdims=True)


def mla_prep_fwd(qraw, kv, proj, gq, gk, cos, sgn, *, kpe_off, name, tm=1024):
    S = qraw.shape[0]
    tm = _tile(S, tm)
    kb = kpe_off // LANES

    def body(q_ref, kn_ref, v_ref, kpe_ref, gq_ref, gk_ref, c_ref, s_ref, qo_ref, ko_ref, vo_ref):
        cos, sgn = c_ref[...], s_ref[...]
        qo_ref[...] = _normrope(q_ref[...], gq_ref[...], cos, sgn)[0].astype(BF16)
        kraw = jnp.concatenate([kn_ref[...], kpe_ref[...]], axis=1)
        ko_ref[...] = _normrope(kraw, gk_ref[...], cos, sgn)[0].astype(BF16)
        vo_ref[...] = v_ref[...].astype(BF16)

    head = pl.BlockSpec((tm, MLA_PAD), lambda i, h: (i, h))
    gain = pl.BlockSpec((1, MLA_PAD), lambda i, h: (0, 0))
    tab = pl.BlockSpec((tm, MLA_PAD), lambda i, h: (i, 0))
    return _pallas(
        body, name=name, grid=(S // tm, MLA_HEADS),
        in_specs=[head, pl.BlockSpec((tm, LANES), lambda i, h: (i, 2 * h)), pl.BlockSpec((tm, LANES), lambda i, h: (i, 2 * h + 1)),
                  pl.BlockSpec((tm, LANES), lambda i, h: (i, kb)), gain, gain, tab, tab],
        out_specs=[head, head, pl.BlockSpec((tm, LANES), lambda i, h: (i, h))],
        out_shape=[jax.ShapeDtypeStruct(qraw.shape, BF16), jax.ShapeDtypeStruct(qraw.shape, BF16),
                   jax.ShapeDtypeStruct((S, MLA_HEADS * LANES), BF16)],
        compiler_params=_params(("parallel", "arbitrary"), 16 * tm * MLA_PAD * 4),
    )(qraw, kv, kv, proj, gq, gk, cos, sgn)


def mla_prep_bwd(dqn, dkn, dv, qraw, kv, proj, gq, gk, cos, sgn, *, kpe_off, name, tm=512):
    S = qraw.shape[0]
    tm = _tile(S, tm)
    kb = kpe_off // LANES

    def body(dq_ref, dk_ref, dv_ref, q_ref, kn_ref, kpe_ref, gq_ref, gk_ref, c_ref, s_ref,
             dqo_ref, dkv_ref, dkpe_ref, dgq_ref, dgk_ref, acc_ref):
        i, h = pl.program_id(0), pl.program_id(1)

        @pl.when((i == 0) & (h == 0))
        def _():
            dgq_ref[...] = jnp.zeros_like(dgq_ref)
            dgk_ref[...] = jnp.zeros_like(dgk_ref)

        @pl.when(h == 0)
        def _():
            acc_ref[...] = jnp.zeros_like(acc_ref)

        cos, sgn = c_ref[...], s_ref[...]
        _, yq, rq = _normrope(q_ref[...], gq_ref[...], cos, sgn)
        dq, dgq = _normrope_bwd(dq_ref[...], gq_ref[...], cos, sgn, yq, rq)
        dqo_ref[...] = dq.astype(BF16)
        dgq_ref[...] += dgq
        kraw = jnp.concatenate([kn_ref[...], kpe_ref[...]], axis=1)
        _, yk, rk = _normrope(kraw, gk_ref[...], cos, sgn)
        dk, dgk = _normrope_bwd(dk_ref[...], gk_ref[...], cos, sgn, yk, rk)
        dgk_ref[...] += dgk
        dkv_ref[...] = jnp.concatenate([dk[:, :LANES], dv_ref[...]], axis=1).astype(BF16)
        acc_ref[...] += dk[:, LANES:]

        @pl.when(h == MLA_HEADS - 1)
        def _():
            dkpe_ref[...] = acc_ref[...].astype(BF16)

    head = pl.BlockSpec((tm, MLA_PAD), lambda i, h: (i, h))
    gain = pl.BlockSpec((1, MLA_PAD), lambda i, h: (0, 0))
    tab = pl.BlockSpec((tm, MLA_PAD), lambda i, h: (i, 0))
    return _pallas(
        body, name=name, grid=(S // tm, MLA_HEADS),
        in_specs=[head, head, pl.BlockSpec((tm, LANES), lambda i, h: (i, h)), head,
                  pl.BlockSpec((tm, LANES), lambda i, h: (i, 2 * h)), pl.BlockSpec((tm, LANES), lambda i, h: (i, kb)),
                  gain, gain, tab, tab],
        out_specs=[head, head, pl.BlockSpec((tm, LANES), lambda i, h: (i, 0)), gain, gain],
        out_shape=[jax.ShapeDtypeStruct(qraw.shape, BF16), jax.ShapeDtypeStruct(qraw.shape, BF16),
                   jax.ShapeDtypeStruct((S, LANES), BF16), jax.ShapeDtypeStruct((1, MLA_PAD), F32),
                   jax.ShapeDtypeStruct((1, MLA_PAD), F32)],
        scratch_shapes=[pltpu.VMEM((tm, LANES), F32)],
        compiler_params=_params(("arbitrary", "arbitrary"), 24 * tm * MLA_PAD * 4),
    )(dqn, dkn, dv, qraw, kv, proj, gq, gk, cos, sgn)


def swa_kv_prep(proj, gk64, *, k_off, v_off, name, tm=512):
    S = proj.shape[0]
    tm = _tile(S, tm)
    W = SWA_KV * HEAD64

    def body(k_ref, v_ref, g_ref, ko_ref, vo_ref):
        first = _first_half((tm, LANES))

        def dup(n):
            nr = pltpu.roll(n, HEAD64, 1)
            return jnp.where(first, n, nr), jnp.where(first, nr, n)

        for t in range(W // LANES):
            x = k_ref[:, t * LANES:(t + 1) * LANES]
            n = x * lax.rsqrt(_segmean64(x * x) + NORM_EPS) * g_ref[...]
            d0, d1 = dup(n)
            ko_ref[:, 2 * t * LANES:(2 * t + 1) * LANES] = d0.astype(BF16)
            ko_ref[:, (2 * t + 1) * LANES:(2 * t + 2) * LANES] = d1.astype(BF16)
            d0, d1 = dup(v_ref[:, t * LANES:(t + 1) * LANES])
            vo_ref[:, 2 * t * LANES:(2 * t + 1) * LANES] = d0.astype(BF16)
            vo_ref[:, (2 * t + 1) * LANES:(2 * t + 2) * LANES] = d1.astype(BF16)

    out = pl.BlockSpec((tm, SWA_KV * LANES), lambda i: (i, 0))
    shape = jax.ShapeDtypeStruct((S, SWA_KV * LANES), BF16)
    return _pallas(
        body, name=name, grid=(S // tm,),
        in_specs=[pl.BlockSpec((tm, W), lambda i: (i, k_off // W)), pl.BlockSpec((tm, W), lambda i: (i, v_off // W)),
                  pl.BlockSpec((1, LANES), lambda i: (0, 0))],
        out_specs=[out, out], out_shape=[shape, shape],
        compiler_params=_params(("parallel",), 12 * tm * W * 4),
    )(proj, proj, jnp.tile(gk64, 2).reshape(1, LANES))


def swa_kv_prep_bwd(dkdup, dvdup, proj, gk64, *, k_off, name, tm=512):
    S = proj.shape[0]
    tm = _tile(S, tm)
    W = SWA_KV * HEAD64

    def body(dk_ref, dv_ref, k_ref, g_ref, dko_ref, dvo_ref, dg_ref):
        @pl.when(pl.program_id(0) == 0)
        def _():
            dg_ref[...] = jnp.zeros_like(dg_ref)

        first = _first_half((tm, LANES))

        def fold(ref, t):
            d0 = ref[:, 2 * t * LANES:(2 * t + 1) * LANES]
            d1 = ref[:, (2 * t + 1) * LANES:(2 * t + 2) * LANES]
            return jnp.where(first, d0 + pltpu.roll(d0, HEAD64, 1), d1 + pltpu.roll(d1, HEAD64, 1))

        for t in range(W // LANES):
            dvo_ref[:, t * LANES:(t + 1) * LANES] = fold(dv_ref, t).astype(BF16)
            dh = fold(dk_ref, t)
            x = k_ref[:, t * LANES:(t + 1) * LANES]
            r = lax.rsqrt(_segmean64(x * x) + NORM_EPS)
            y = x * r
            dy = dh * g_ref[...]
            dko_ref[:, t * LANES:(t + 1) * LANES] = (r * (dy - y * _segmean64(dy * y))).astype(BF16)
            dg_ref[...] += jnp.sum(dh * y, axis=0, keepdims=True)

    dup = pl.BlockSpec((tm, SWA_KV * LANES), lambda i: (i, 0))
    out = pl.BlockSpec((tm, W), lambda i: (i, 0))
    shape = jax.ShapeDtypeStruct((S, W), BF16)
    return _pallas(
        body, name=name, grid=(S // tm,),
        in_specs=[dup, dup, pl.BlockSpec((tm, W), lambda i: (i, k_off // W)), pl.BlockSpec((1, LANES), lambda i: (0, 0))],
        out_specs=[out, out, pl.BlockSpec((1, LANES), lambda i: (0, 0))],
        out_shape=[shape, shape, jax.ShapeDtypeStruct((1, LANES), F32)],
        compiler_params=_params(("arbitrary",), 16 * tm * W * 4),
    )(dkdup, dvdup, proj, jnp.tile(gk64, 2).reshape(1, LANES))


def _swa_geometry(i):
    r = lax.broadcasted_iota(I32, (BLK, 2 * BLK), 0)
    c = lax.broadcasted_iota(I32, (BLK, 2 * BLK), 1)
    rel = r + BLK - c
    valid = (rel >= 0) & (rel < BLK) & ((c >= BLK) | (i > 0))
    return valid, rel.astype(F32)


def _swa_slope(h):
    return 2.0 ** (-8.0 * (h + 1) / SWA_HEADS)


def swa_attn_fwd(qn, kdup, vdup, sinks, *, name):
    S = qn.shape[0]
    nq = S // BLK
    group = SWA_HEADS // SWA_KV

    def body(q_ref, kp_ref, kc_ref, vp_ref, vc_ref, sink_ref, o_ref, lse_ref):
        i = pl.program_id(0)
        valid, rel = _swa_geometry(i)
        first = _first_half((BLK, LANES))
        lane = lax.broadcasted_iota(I32, (BLK, LANES), 1)
        lse_all = jnp.zeros((BLK, LANES), F32)
        zero = jnp.zeros((BLK, LANES), BF16)
        H = range(SWA_HEADS)
        kks = [jnp.concatenate([kp_ref[:, g * LANES:(g + 1) * LANES], kc_ref[:, g * LANES:(g + 1) * LANES]], axis=0)
               for g in range(SWA_KV)]
        vvs = [jnp.concatenate([vp_ref[:, g * LANES:(g + 1) * LANES], vc_ref[:, g * LANES:(g + 1) * LANES]], axis=0)
               for g in range(SWA_KV)]
        q2s = [q_ref[:, t * LANES:(t + 1) * LANES] for t in range(SWA_HEADS // 2)]
        qhs = [jnp.where(first if h % 2 == 0 else ~first, q2s[h // 2], zero) for h in H]
        ss = [_dot(qhs[h], kks[h // group], NT) * (1.0 / math.sqrt(HEAD64)) - _swa_slope(h) * rel for h in H]
        ss = [jnp.where(valid, s, -1e30) for s in ss]
        ms = [jnp.maximum(jnp.max(ss[h], axis=1, keepdims=True), sink_ref[h]) for h in H]
        es = [jnp.exp(ss[h] - ms[h]) for h in H]
        dens = [jnp.sum(es[h], axis=1, keepdims=True) + jnp.exp(sink_ref[h] - ms[h]) for h in H]
        outs = [_dot((es[h] / dens[h]).astype(BF16), vvs[h // group]) for h in H]
        for h in H:
            lse_all = jnp.where(lane == h, ms[h] + jnp.log(dens[h]), lse_all)
        for t in range(SWA_HEADS // 2):
            o_ref[:, t * LANES:(t + 1) * LANES] = jnp.where(first, outs[2 * t], outs[2 * t + 1])
        lse_ref[...] = lse_all

    prev = lambda i: (jnp.maximum(i - 1, 0), 0)
    cur = lambda i: (i, 0)
    kvw = SWA_KV * LANES
    return _pallas(
        body, name=name, grid=(nq,),
        in_specs=[pl.BlockSpec((BLK, 1024), cur), pl.BlockSpec((BLK, kvw), prev), pl.BlockSpec((BLK, kvw), cur),
                  pl.BlockSpec((BLK, kvw), prev), pl.BlockSpec((BLK, kvw), cur),
                  pl.BlockSpec(memory_space=pltpu.SMEM)],
        out_specs=[pl.BlockSpec((BLK, 1024), cur), pl.BlockSpec((BLK, LANES), cur)],
        out_shape=[jax.ShapeDtypeStruct((S, 1024), F32), jax.ShapeDtypeStruct((S, LANES), F32)],
        compiler_params=_params(("parallel",), 16 * BLK * 1024 * 4),
    )(qn, kdup, kdup, vdup, vdup, sinks)


def swa_attn_bwd(qn, kdup, vdup, sinks, o, lse, do, *, name):
    S = qn.shape[0]
    nq = S // BLK
    group = SWA_HEADS // SWA_KV
    scale = 1.0 / math.sqrt(HEAD64)

    def body(q_ref, kp_ref, kc_ref, vp_ref, vc_ref, sink_ref, o_ref, lse_ref, do_ref,
             dq_ref, dk_ref, dv_ref, ds_ref):
        i = pl.program_id(0)

        @pl.when(i == 0)
        def _():
            dk_ref[...] = jnp.zeros_like(dk_ref)
            dv_ref[...] = jnp.zeros_like(dv_ref)
            ds_ref[...] = jnp.zeros_like(ds_ref)

        valid, rel = _swa_geometry(i)
        first = _first_half((BLK, LANES))
        lane1 = lax.broadcasted_iota(I32, (1, LANES), 1)
        lane = lax.broadcasted_iota(I32, (BLK, LANES), 1)
        lse_all = lse_ref[...]
        prow = pl.ds(pl.multiple_of(jnp.maximum(i - 1, 0) * BLK, BLK), BLK)
        crow = pl.ds(pl.multiple_of(i * BLK, BLK), BLK)
        dsink = jnp.zeros((1, LANES), F32)
        zero = jnp.zeros((BLK, LANES), BF16)
        H, T = range(SWA_HEADS), range(SWA_HEADS // 2)
        kks = [jnp.concatenate([kp_ref[:, g * LANES:(g + 1) * LANES], kc_ref[:, g * LANES:(g + 1) * LANES]], axis=0)
               for g in range(SWA_KV)]
        vvs = [jnp.concatenate([vp_ref[:, g * LANES:(g + 1) * LANES], vc_ref[:, g * LANES:(g + 1) * LANES]], axis=0)
               for g in range(SWA_KV)]
        q2s = [q_ref[:, t * LANES:(t + 1) * LANES] for t in T]
        do2s = [do_ref[:, t * LANES:(t + 1) * LANES] for t in T]
        prods = [do2s[t] * o_ref[:, t * LANES:(t + 1) * LANES] for t in T]
        mine = [first if h % 2 == 0 else ~first for h in H]
        qhs = [jnp.where(mine[h], q2s[h // 2], zero) for h in H]
        dohs = [jnp.where(mine[h], do2s[h // 2], 0.0).astype(BF16) for h in H]
        deltas = [jnp.sum(jnp.where(mine[h], prods[h // 2], 0.0), axis=1, keepdims=True) for h in H]
        lses = [jnp.sum(jnp.where(lane == h, lse_all, 0.0), axis=1, keepdims=True) for h in H]
        ss = [_dot(qhs[h], kks[h // group], NT) * scale - _swa_slope(h) * rel for h in H]
        dps = [_dot(dohs[h], vvs[h // group], NT) for h in H]
        ps = [jnp.where(valid, jnp.exp(ss[h] - lses[h]), 0.0) for h in H]
        dscs = [(ps[h] * (dps[h] - deltas[h]) * scale).astype(BF16) for h in H]
        dqs = [_dot(dscs[h], kks[h // group]) for h in H]
        dks = [_dot(dscs[h], qhs[h], TN) for h in H]
        dvs = [_dot(ps[h].astype(BF16), dohs[h], TN) for h in H]
        for h in H:
            psink = jnp.exp(sink_ref[h] - lses[h])
            dsink = dsink + jnp.where(lane1 == h, -jnp.sum(psink * deltas[h]), 0.0)
        for t in T:
            dq_ref[:, t * LANES:(t + 1) * LANES] = jnp.where(first, dqs[2 * t], dqs[2 * t + 1])
        for g in range(SWA_KV):
            cols = slice(g * LANES, (g + 1) * LANES)
            b = g * group
            dkk = (dks[b] + dks[b + 1]) + (dks[b + 2] + dks[b + 3])
            dvv = (dvs[b] + dvs[b + 1]) + (dvs[b + 2] + dvs[b + 3])
            dk_ref[prow, cols] += dkk[:BLK]
            dv_ref[prow, cols] += dvv[:BLK]
            dk_ref[crow, cols] += dkk[BLK:]
            dv_ref[crow, cols] += dvv[BLK:]
        ds_ref[...] += dsink

    prev = lambda i: (jnp.maximum(i - 1, 0), 0)
    cur = lambda i: (i, 0)
    kvw = SWA_KV * LANES
    whole = pl.BlockSpec((S, kvw), lambda i: (0, 0))
    return _pallas(
        body, name=name, grid=(nq,),
        in_specs=[pl.BlockSpec((BLK, 1024), cur), pl.BlockSpec((BLK, kvw), prev), pl.BlockSpec((BLK, kvw), cur),
                  pl.BlockSpec((BLK, kvw), prev), pl.BlockSpec((BLK, kvw), cur),
                  pl.BlockSpec(memory_space=pltpu.SMEM),
                  pl.BlockSpec((BLK, 1024), cur), pl.BlockSpec((BLK, LANES), cur), pl.BlockSpec((BLK, 1024), cur)],
        out_specs=[pl.BlockSpec((BLK, 1024), cur), whole, whole, pl.BlockSpec((1, LANES), lambda i: (0, 0))],
        out_shape=[jax.ShapeDtypeStruct((S, 1024), F32), jax.ShapeDtypeStruct((S, kvw), F32),
                   jax.ShapeDtypeStruct((S, kvw), F32), jax.ShapeDtypeStruct((1, LANES), F32)],
        compiler_params=_params(("arbitrary",), 4 * S * kvw * 4 + 24 * BLK * 1024 * 4),
    )(qn, kdup, kdup, vdup, vdup, sinks, o, lse, do)


def _in_bwd(x, h, dproj, dy, g, w_in, tag):
    dh = matmul(dproj, w_in, "nt", name=f"{tag}_dh")
    dw_in = matmul(h, dproj, "tn", name=f"{tag}_dwin")
    dx, dg = rmsnorm_bwd(x, g, dh, col_off=0, width=D_MODEL, out_dtype=F32, res=dy, name=f"{tag}_dnorm")
    return dx, dw_in, dg


def _out_bwd(dy, og, o, proj, w_out, gate_off, tag):
    dog = matmul(dy, w_out, "nt", name=f"{tag}_dog")
    dw_out = matmul(og, dy, "tn", name=f"{tag}_dwout")
    do, dgate = gate_bwd(dog, o, proj, gate_off=gate_off, name=f"{tag}_dgate")
    return do, dgate, dw_out


def sb_fwd(x, p, tag, ride=None):
    h = rmsnorm_fwd(x, p["norm"], col_off=0, width=D_MODEL, out_dtype=BF16, name=f"{tag}_norm")
    qkv = matmul(h, p["w_in"], "nn", out_dtype=BF16, b_cols=(0, 3 * SB_W), name=f"{tag}_proj")
    gate = matmul(h, p["w_in"], "nn", b_cols=(3 * SB_W, SB_W), name=f"{tag}_gproj")
    o = sb_attn_fwd(qkv, name=f"{tag}_attn", ride=ride)
    if ride is not None:
        o, gathered = o
    og = gate_fwd(o, gate, gate_off=0, name=f"{tag}_gate")
    y = matmul(og, p["w_out"], "nn", add=x, name=f"{tag}_out")
    saved = (x, h, gate, qkv, o, og)
    return (y, saved) if ride is None else (y, saved, gathered)


def sb_bwd(dy, saved, p, tag, ride=None):
    x, h, gate, qkv, o, og = saved
    do, dgate, dw_out = _out_bwd(dy, og, o, gate, p["w_out"], 0, tag)
    dq, dk, dv, *received = sb_attn_bwd(qkv, o, do, name=f"{tag}_dattn", ride=ride)
    dproj = jnp.concatenate([dq, dk.astype(BF16), dv.astype(BF16), dgate], axis=1)
    dx, dw_in, dg = _in_bwd(x, h, dproj, dy, p["norm"], p["w_in"], tag)
    grads = {"norm": dg[0], "w_in": dw_in, "w_out": dw_out}
    return (dx, grads) if ride is None else (dx, grads, received[0])


MLA_GATE, MLA_QLAT, MLA_KVLAT, MLA_KPE, MLA_IN = 0, 1024, 1280, 1408, 1536


def mla_fwd(x, p, tabs, tag, ride=None):
    cos, sgn = tabs
    h = rmsnorm_fwd(x, p["norm"], col_off=0, width=D_MODEL, out_dtype=BF16, name=f"{tag}_norm")
    proj = matmul(h, p["w_in"], "nn", name=f"{tag}_proj")
    ql = rmsnorm_fwd(proj, p["q_a_norm"], col_off=MLA_QLAT, width=256, out_dtype=BF16, name=f"{tag}_qanorm")
    kvl = rmsnorm_fwd(proj, p["kv_a_norm"], col_off=MLA_KVLAT, width=128, out_dtype=BF16, name=f"{tag}_kvanorm")
    qraw = matmul(ql, p["w_uq"], "nn", name=f"{tag}_uq")
    kv = matmul(kvl, p["w_ukv"], "nn", name=f"{tag}_ukv")
    qn, kn, vb = mla_prep_fwd(qraw, kv, proj, p["gq"], p["gk"], cos, sgn, kpe_off=MLA_KPE, name=f"{tag}_prep")
    o, lse, *gathered = flash_fwd(qn, kn, vb, name=f"{tag}_attn", ride=ride)
    og = gate_fwd(o, proj, gate_off=MLA_GATE, name=f"{tag}_gate")
    y = matmul(og, p["w_out"], "nn", add=x, name=f"{tag}_out")
    saved = (x, h, proj, ql, kvl, qraw, kv, qn, kn, vb, o, lse, og)
    return (y, saved) if ride is None else (y, saved, gathered[0])


def mla_bwd(dy, saved, p, tabs, tag):
    cos, sgn = tabs
    x, h, proj, ql, kvl, qraw, kv, qn, kn, vb, o, lse, og = saved
    do, dgate, dw_out = _out_bwd(dy, og, o, proj, p["w_out"], MLA_GATE, tag)
    dqn, dkn, dv = flash_bwd(qn, kn, vb, o, lse, do, name=f"{tag}_dattn")
    dqraw, dkv, dkpe, dgq, dgk = mla_prep_bwd(dqn, dkn, dv, qraw, kv, proj, p["gq"], p["gk"], cos, sgn,
                                              kpe_off=MLA_KPE, name=f"{tag}_dprep")
    dql = matmul(dqraw, p["w_uq"], "nt", name=f"{tag}_dql")
    dw_uq = matmul(ql, dqraw, "tn", name=f"{tag}_dwuq")
    dkvl = matmul(dkv, p["w_ukv"], "nt", name=f"{tag}_dkvl")
    dw_ukv = matmul(kvl, dkv, "tn", name=f"{tag}_dwukv")
    dqlat, dgqa = rmsnorm_bwd(proj, p["q_a_norm"], dql, col_off=MLA_QLAT, width=256, out_dtype=BF16, name=f"{tag}_dqanorm")
    dkvlat, dgkva = rmsnorm_bwd(proj, p["kv_a_norm"], dkvl, col_off=MLA_KVLAT, width=128, out_dtype=BF16,
                                name=f"{tag}_dkvanorm")
    dproj = jnp.concatenate([dgate, dqlat, dkvlat, dkpe], axis=1)
    dx, dw_in, dg = _in_bwd(x, h, dproj, dy, p["norm"], p["w_in"], tag)
    return dx, {"norm": dg[0], "w_in": dw_in, "q_a_norm": dgqa[0], "w_uq": dw_uq, "kv_a_norm": dgkva[0],
                "w_ukv": dw_ukv, "gq": dgq[0], "gk": dgk[0], "w_out": dw_out}


SWA_Q, SWA_GATE, SWA_K, SWA_V = 0, 1024, 2048, 2304


def swa_fwd(x, p, tag):
    h = rmsnorm_fwd(x, p["norm"], col_off=0, width=D_MODEL, out_dtype=BF16, name=f"{tag}_norm")
    proj = matmul(h, p["w_in"], "nn", name=f"{tag}_proj")
    qn = headnorm_fwd(proj, p["q_head_norm"], col_off=SWA_Q, width=1024, name=f"{tag}_qnorm")
    kdup, vdup = swa_kv_prep(proj, p["k_head_norm"], k_off=SWA_K, v_off=SWA_V, name=f"{tag}_kvprep")
    o, lse = swa_attn_fwd(qn, kdup, vdup, p["sinks"], name=f"{tag}_attn")
    og = gate_fwd(o, proj, gate_off=SWA_GATE, name=f"{tag}_gate")
    y = matmul(og, p["w_out"], "nn", add=x, name=f"{tag}_out")
    return y, (x, h, proj, qn, kdup, vdup, o, lse, og)


def swa_bwd(dy, saved, p, tag):
    x, h, proj, qn, kdup, vdup, o, lse, og = saved
    do, dgate, dw_out = _out_bwd(dy, og, o, proj, p["w_out"], SWA_GATE, tag)
    dqn, dkdup, dvdup, dsinks = swa_attn_bwd(qn, kdup, vdup, p["sinks"], o, lse, do, name=f"{tag}_dattn")
    dq, dgq = headnorm_bwd(proj, p["q_head_norm"], dqn, col_off=SWA_Q, width=1024, name=f"{tag}_dqnorm")
    dk, dv, dgk = swa_kv_prep_bwd(dkdup, dvdup, proj, p["k_head_norm"], k_off=SWA_K, name=f"{tag}_dkvprep")
    dproj = jnp.concatenate([dq, dgate, dk, dv], axis=1)
    dx, dw_in, dg = _in_bwd(x, h, dproj, dy, p["norm"], p["w_in"], tag)
    return dx, {"norm": dg[0], "w_in": dw_in, "q_head_norm": dgq[0, :HEAD64] + dgq[0, HEAD64:],
                "k_head_norm": dgk[0, :HEAD64] + dgk[0, HEAD64:], "sinks": dsinks[0, :SWA_HEADS], "w_out": dw_out}


def prepare_layer(w, i):
    if i in (0, 3):
        return {"norm": w[f"l{i}_norm"], "w_in": w[f"l{i}_w_in"], "w_out": w[f"l{i}_w_out"]}
    if i == 1:
        l1_in = w["l1_w_in"]
        pad64 = lambda v: jnp.pad(v, (0, MLA_PAD - MLA_QK)).reshape(1, MLA_PAD)
        return {"norm": w["l1_norm"],
                "w_in": jnp.concatenate([l1_in[:, 448:], l1_in[:, :448], jnp.zeros((D_MODEL, 64), l1_in.dtype)], axis=1),
                "q_a_norm": w["l1_q_a_norm"], "kv_a_norm": w["l1_kv_a_norm"],
                "w_uq": jnp.pad(w["l1_w_uq"].reshape(256, MLA_HEADS, MLA_QK), ((0, 0), (0, 0), (0, MLA_PAD - MLA_QK))
                                ).reshape(256, MLA_HEADS * MLA_PAD),
                "w_ukv": w["l1_w_ukv"], "gq": pad64(w["l1_q_head_norm"]), "gk": pad64(w["l1_k_head_norm"]),
                "w_out": w["l1_w_out"]}
    l2_in = w["l2_w_in"]
    return {"norm": w["l2_norm"],
            "w_in": jnp.concatenate([l2_in[:, :1024], l2_in[:, 1536:], l2_in[:, 1024:1536]], axis=1),
            "q_head_norm": w["l2_q_head_norm"], "k_head_norm": w["l2_k_head_norm"], "sinks": w["l2_sinks"],
            "w_out": w["l2_w_out"]}


def unprepare_grads(gs):
    g0, g1, g2, g3 = gs
    d1, d2 = g1["w_in"], g2["w_in"]
    first = {} if g0 is None else {"l0_norm": g0["norm"], "l0_w_in": g0["w_in"], "l0_w_out": g0["w_out"]}
    return {
        **first,
        "l1_norm": g1["norm"], "l1_w_in": jnp.concatenate([d1[:, 1024:1472], d1[:, :1024]], axis=1),
        "l1_q_a_norm": g1["q_a_norm"],
        "l1_w_uq": g1["w_uq"].reshape(256, MLA_HEADS, MLA_PAD)[:, :, :MLA_QK].reshape(256, MLA_HEADS * MLA_QK),
        "l1_kv_a_norm": g1["kv_a_norm"], "l1_w_ukv": g1["w_ukv"],
        "l1_q_head_norm": g1["gq"][:MLA_QK], "l1_k_head_norm": g1["gk"][:MLA_QK], "l1_w_out": g1["w_out"],
        "l2_norm": g2["norm"], "l2_w_in": jnp.concatenate([d2[:, :1024], d2[:, 2048:], d2[:, 1024:2048]], axis=1),
        "l2_q_head_norm": g2["q_head_norm"], "l2_k_head_norm": g2["k_head_norm"], "l2_sinks": g2["sinks"],
        "l2_w_out": g2["w_out"],
        "l3_norm": g3["norm"], "l3_w_in": g3["w_in"], "l3_w_out": g3["w_out"],
    }


def local_step(x, target, w, late=None, early=None):
    tabs = _rope_tables(x.shape[0])
    p0 = prepare_layer(w, 0)
    if late is None:
        y0, s0 = sb_fwd(x, p0, "l0")
        p1 = prepare_layer(w, 1)
        y1, s1 = mla_fwd(y0, p1, tabs, "l1")
    else:
        y0, s0, gathered = sb_fwd(x, p0, "l0", ride=late[0][0])
        w = {**w, **late[0][1](gathered)}
        p1 = prepare_layer(w, 1)
        y1, s1, gathered = mla_fwd(y0, p1, tabs, "l1", ride=late[1][0])
        w = {**w, **late[1][1](gathered)}
    ps = [p0, p1, prepare_layer(w, 2), prepare_layer(w, 3)]
    y2, s2 = swa_fwd(y1, ps[2], "l2")
    y3, s3 = sb_fwd(y2, ps[3], "l3")
    dy, loss = loss_head(y3, target, name="loss")
    d3, g3 = sb_bwd(dy, s3, ps[3], "l3")
    d2, g2 = swa_bwd(d3, s2, ps[2], "l2")
    d1, g1 = mla_bwd(d2, s1, ps[1], tabs, "l1")
    if early is None:
        d0, g0 = sb_bwd(d1, s0, ps[0], "l0")
        return loss, d0, unprepare_grads([g0, g1, g2, g3])
    d0, g0, received = sb_bwd(d1, s0, ps[0], "l0", ride=early(unprepare_grads([None, g1, g2, g3])))
    return loss, d0, unprepare_grads([g0, g1, g2, g3]), received


MATS = (("l0_w_in", "col", 1024, 4096), ("l0_w_out", "row", 1024, 1024), ("l1_w_in", "col", 1024, 1472),
        ("l1_w_uq", "col", 256, 1536), ("l1_w_ukv", "col", 128, 2048), ("l1_w_out", "row", 1024, 1024),
        ("l2_w_in", "col", 1024, 2560), ("l2_w_out", "row", 1024, 1024), ("l3_w_in", "col", 1024, 4096),
        ("l3_w_out", "row", 1024, 1024))
PACK_GROUPS = ((MATS[0:2], 1280), (MATS[2:6], 800), (MATS[6:10], 2272))
N_CHIPS = 4
PACK_W = 1024
HALF_ROWS = 2176
PACK_ROWS = 2 * HALF_ROWS
VECS = (("l0_norm", 0, 0, 1024), ("l1_norm", 1, 0, 1024), ("l2_norm", 2, 0, 1024), ("l3_norm", 3, 0, 1024),
        ("l1_q_a_norm", 4, 0, 256), ("l1_kv_a_norm", 4, 256, 128), ("l1_q_head_norm", 4, 384, 192),
        ("l1_k_head_norm", 4, 576, 192), ("l2_q_head_norm", 4, 768, 64), ("l2_k_head_norm", 4, 832, 64),
        ("l2_sinks", 4, 896, 16))
LOSS_SLOT = (4, 912)
VEC_ROWS = 8


def _shard_rows(k, n):
    return k * n // N_CHIPS // PACK_W


def _group_start(gi):
    return sum(rows for _, rows in PACK_GROUPS[:gi])


def pack_shards(shards):
    parts = []
    for mats, rows in PACK_GROUPS:
        group = [shards[name].reshape(-1, PACK_W) for name, _, _, _ in mats]
        used = sum(p.shape[0] for p in group)
        parts += group + [jnp.zeros((rows - used, PACK_W), group[0].dtype)]
    return jnp.concatenate(parts, axis=0)


def unpack_shards(flat):
    out = {}
    for gi, (mats, _) in enumerate(PACK_GROUPS):
        r0 = _group_start(gi)
        for name, kind, k, n in mats:
            rows = _shard_rows(k, n)
            shape = (k, n // N_CHIPS) if kind == "col" else (k // N_CHIPS, n)
            out[name] = flat[r0:r0 + rows].reshape(shape)
            r0 += rows
    return out


def pack_full(full, groups=(0, 1, 2)):
    parts = []
    for mats, rows in [PACK_GROUPS[gi] for gi in groups]:
        group = []
        for name, kind, k, n in mats:
            m = full[name]
            if kind == "col":
                m = m.reshape(k, N_CHIPS, n // N_CHIPS).transpose(1, 0, 2)
            group.append(m.reshape(N_CHIPS, -1, PACK_W))
        used = sum(p.shape[1] for p in group)
        parts += group + [jnp.zeros((N_CHIPS, rows - used, PACK_W), group[0].dtype)]
    return jnp.concatenate(parts, axis=1)


def unpack_full(stacked, gi):
    out, r0 = {}, 0
    for name, kind, k, n in PACK_GROUPS[gi][0]:
        rows = _shard_rows(k, n)
        seg = stacked[:, r0:r0 + rows]
        if kind == "col":
            out[name] = seg.reshape(N_CHIPS, k, n // N_CHIPS).transpose(1, 0, 2).reshape(k, n)
        else:
            out[name] = seg.reshape(k, n)
        r0 += rows
    return out


def pack_vecs(vecs, loss=None):
    rows = []
    for r in range(VEC_ROWS):
        items = [(off, vecs[name]) for name, rr, off, _ in VECS if rr == r]
        if loss is not None and r == LOSS_SLOT[0]:
            items.append((LOSS_SLOT[1], loss.reshape(1)))
        pos, parts = 0, []
        for off, v in sorted(items, key=lambda t: t[0]):
            assert off == pos
            parts.append(v.astype(F32))
            pos += v.shape[0]
        parts.append(jnp.zeros((PACK_W - pos,), F32))
        rows.append(jnp.concatenate(parts))
    return jnp.stack(rows)


def unpack_vecs(block):
    return {name: block[r, off:off + n] for name, r, off, n in VECS}


def _me():
    return lax.axis_index("x"), lax.axis_index("y"), lax.axis_index("c")


OTHER_CHIPS = ((1, 0), (0, 1), (1, 1))


def _remote(src, dst, send_sem, recv_sem, to):
    return pltpu.make_async_remote_copy(src_ref=src, dst_ref=dst, send_sem=send_sem, recv_sem=recv_sem,
                                        device_id=to, device_id_type=MESH)


def gather_weights(block):
    def body(in_ref, out_ref, send_sems, recv_sems):
        send, forward, finish = _gather_steps(in_ref, out_ref, send_sems, recv_sems)
        send()
        forward()
        finish()

    hbm = pl.BlockSpec(memory_space=pltpu.HBM)
    others = pl.pallas_call(
        body, name="gather_weights",
        out_shape=jax.ShapeDtypeStruct((N_CHIPS,) + block.shape, block.dtype),
        in_specs=[hbm], out_specs=hbm,
        scratch_shapes=list(GATHER_SEMS),
    )(block)
    return place_own_block(others, block)


GATHER_SEMS = (pltpu.SemaphoreType.DMA((6,)), pltpu.SemaphoreType.DMA((6,)))


def place_own_block(others, block):
    return lax.dynamic_update_slice(others, block[None], (2 * lax.axis_index("x") + lax.axis_index("y"), 0, 0))


def _gather_steps(in_ref, out_ref, send_sems, recv_sems):
    hr = in_ref.shape[0] // 2
    x, y, c = _me()
    sibling = (x, y, 1 - c)
    chips = [(x ^ dx, y ^ dy) for dx, dy in OTHER_CHIPS]

    def half(px, py, pc):
        return out_ref.at[2 * px + py, pl.ds(pc * hr, hr), :]

    def firsts():
        return [_remote(in_ref.at[pl.ds(c * hr, hr), :], half(x, y, c), send_sems.at[j], recv_sems.at[j], (*chip, c))
                for j, chip in enumerate(chips)]

    def passes():
        return [_remote(half(*chip, c), half(*chip, c), send_sems.at[3 + j], recv_sems.at[3 + j], sibling)
                for j, chip in enumerate(chips)]

    def send():
        for cp in firsts():
            cp.start()

    def forward():
        for j, (chip, cp) in enumerate(zip(chips, passes())):
            _remote(half(*chip, c), half(*chip, c), send_sems.at[j], recv_sems.at[j], (*chip, c)).wait_recv()
            cp.start()

    def finish():
        for j, chip in enumerate(chips):
            _remote(half(*chip, 1 - c), half(*chip, 1 - c), send_sems.at[3 + j], recv_sems.at[3 + j], sibling).wait_recv()
        for cp in firsts() + passes():
            cp.wait_send()

    return send, forward, finish


def pair_exchange(g, tag):
    hr = g.shape[1] // 2

    def body(g_ref, out_ref, send_sems, recv_sems):
        x, y, c = _me()
        sibling = (x, y, 1 - c)
        copies = [_remote(g_ref.at[k, pl.ds((1 - c) * hr, hr), :], out_ref.at[k], send_sems.at[k],
                          recv_sems.at[k], sibling) for k in range(N_CHIPS)]
        for cp in copies:
            cp.start()
        for cp in copies:
            cp.wait_recv()
        for cp in copies:
            cp.wait_send()

    hbm = pl.BlockSpec(memory_space=pltpu.HBM)
    return pl.pallas_call(
        body, name=f"pair_exchange_{tag}",
        out_shape=jax.ShapeDtypeStruct((N_CHIPS, hr, PACK_W), g.dtype),
        in_specs=[hbm], out_specs=hbm,
        scratch_shapes=[pltpu.SemaphoreType.DMA((N_CHIPS,)), pltpu.SemaphoreType.DMA((N_CHIPS,))],
    )(g)


CHIP_SEMS = (pltpu.SemaphoreType.DMA((3,)), pltpu.SemaphoreType.DMA((3,)))


def _chip_exchange_steps(p_ref, out_ref, send_sems, recv_sems):
    x, y, c = _me()

    def copies():
        return [_remote(p_ref.at[2 * (x ^ dx) + (y ^ dy)], out_ref.at[j], send_sems.at[j], recv_sems.at[j],
                        (x ^ dx, y ^ dy, c)) for j, (dx, dy) in enumerate(OTHER_CHIPS)]

    def send():
        for cp in copies():
            cp.start()

    def finish():
        for cp in copies():
            cp.wait_recv()
        for cp in copies():
            cp.wait_send()

    return send, finish


def chip_exchange(part, tag):
    def body(p_ref, out_ref, send_sems, recv_sems):
        send, finish = _chip_exchange_steps(p_ref, out_ref, send_sems, recv_sems)
        send()
        finish()

    hbm = pl.BlockSpec(memory_space=pltpu.HBM)
    return pl.pallas_call(
        body, name=f"chip_exchange_{tag}",
        out_shape=jax.ShapeDtypeStruct((len(OTHER_CHIPS),) + part.shape[1:], part.dtype),
        in_specs=[hbm], out_specs=hbm,
        scratch_shapes=list(CHIP_SEMS),
    )(part)


def join_halves(half, tag):
    hr = half.shape[0]

    def body(h_ref, out_ref, send_sem, recv_sem):
        x, y, c = _me()
        cp = _remote(h_ref, out_ref.at[pl.ds(c * hr, hr), :], send_sem, recv_sem, (x, y, 1 - c))
        cp.start()
        _remote(h_ref, out_ref.at[pl.ds((1 - c) * hr, hr), :], send_sem, recv_sem, (x, y, 1 - c)).wait_recv()
        cp.wait_send()

    hbm = pl.BlockSpec(memory_space=pltpu.HBM)
    other = pl.pallas_call(
        body, name=f"join_halves_{tag}",
        out_shape=jax.ShapeDtypeStruct((2 * hr, PACK_W), half.dtype),
        in_specs=[hbm], out_specs=hbm,
        scratch_shapes=[pltpu.SemaphoreType.DMA, pltpu.SemaphoreType.DMA],
    )(half)
    return lax.dynamic_update_slice(other, half, (lax.axis_index("c") * hr, 0))


def sum_over_devices(block):
    def body(in_ref, out_ref, all_ref, send_sems, recv_sems):
        x, y, c = _me()
        me = 4 * x + 2 * y + c
        all_ref[me] = in_ref[...]
        copies = []
        for r in range(1, 8):
            to = (x ^ (r >> 2), y ^ ((r >> 1) & 1), c ^ (r & 1))
            copies.append(_remote(in_ref, all_ref.at[me], send_sems.at[r - 1], recv_sems.at[r - 1], to))
        for cp in copies:
            cp.start()
        for r in range(1, 8):
            frm = (x ^ (r >> 2), y ^ ((r >> 1) & 1), c ^ (r & 1))
            _remote(in_ref, all_ref.at[4 * frm[0] + 2 * frm[1] + frm[2]], send_sems.at[r - 1], recv_sems.at[r - 1],
                    frm).wait_recv()
        for cp in copies:
            cp.wait_send()
        acc = all_ref[0]
        for d in range(1, 8):
            acc = acc + all_ref[d]
        out_ref[...] = acc

    vmem = pl.BlockSpec(memory_space=pltpu.VMEM)
    return pl.pallas_call(
        body, name="sum_over_devices",
        out_shape=jax.ShapeDtypeStruct(block.shape, F32),
        in_specs=[vmem], out_specs=vmem,
        scratch_shapes=[pltpu.VMEM((8,) + block.shape, F32), pltpu.SemaphoreType.DMA((7,)), pltpu.SemaphoreType.DMA((7,))],
    )(block)


def pair_sum(g, got, core, tag):
    hr = got.shape[1]
    rows = _tile(hr, 512)
    steps = hr // rows

    def body(s_ref, g_ref, r_ref, o_ref, ob_ref):
        t = g_ref[...] + r_ref[...]
        o_ref[...] = t
        ob_ref[...] = t.astype(BF16)

    blk = pl.BlockSpec((1, rows, PACK_W), lambda k, i, s: (k, i, 0))
    return pl.pallas_call(
        body, name=f"pair_sum_{tag}",
        grid_spec=pltpu.PrefetchScalarGridSpec(
            num_scalar_prefetch=1, grid=(N_CHIPS, steps),
            in_specs=[pl.BlockSpec((1, rows, PACK_W), lambda k, i, s: (k, s[0] * steps + i, 0)), blk],
            out_specs=[blk, blk]),
        out_shape=[jax.ShapeDtypeStruct((N_CHIPS, hr, PACK_W), F32), jax.ShapeDtypeStruct((N_CHIPS, hr, PACK_W), BF16)],
        compiler_params=_params(("parallel", "parallel"), 8 * rows * PACK_W * 4),
    )(core, g, got)


def chip_sum(part, got, chip, tag):
    hr = part.shape[1]
    rows = _tile(hr, 512)

    def body(s_ref, p_ref, a_ref, b_ref, c_ref, o_ref):
        o_ref[...] = ((p_ref[0] + a_ref[0].astype(F32)) + b_ref[0].astype(F32)) + c_ref[0].astype(F32)

    def got_spec(j):
        return pl.BlockSpec((1, rows, PACK_W), lambda i, s: (j, i, 0))

    return pl.pallas_call(
        body, name=f"chip_sum_{tag}",
        grid_spec=pltpu.PrefetchScalarGridSpec(
            num_scalar_prefetch=1, grid=(hr // rows,),
            in_specs=[pl.BlockSpec((1, rows, PACK_W), lambda i, s: (s[0], i, 0)), got_spec(0), got_spec(1), got_spec(2)],
            out_specs=pl.BlockSpec((rows, PACK_W), lambda i, s: (i, 0))),
        out_shape=jax.ShapeDtypeStruct((hr, PACK_W), F32),
        compiler_params=_params(("parallel",), 10 * rows * PACK_W * 4),
    )(chip, part, got, got, got)


def reduce_scatter_begin(g, tag):
    c = lax.axis_index("c")
    return pair_sum(g, pair_exchange(g, tag), jnp.reshape(c, (1,)).astype(I32), tag)


def reduce_scatter_end(part, got, tag):
    chip = 2 * lax.axis_index("x") + lax.axis_index("y")
    return join_halves(chip_sum(part, got, jnp.reshape(chip, (1,)).astype(I32), tag), tag)


def reduce_scatter(g, tag):
    part, part_bf16 = reduce_scatter_begin(g, tag)
    return reduce_scatter_end(part, chip_exchange(part_bf16, tag), tag)


def adamw(w, g, m, v, *, name):
    rows, cols = w.shape
    tm = _tile(rows, 256)
    c1 = 1.0 - ADAM_B1 ** ADAM_STEP
    c2 = 1.0 - ADAM_B2 ** ADAM_STEP

    def body(w_ref, g_ref, m_ref, v_ref, d_ref, mo_ref, vo_ref):
        gv = g_ref[...]
        mn = ADAM_B1 * m_ref[...] + (1.0 - ADAM_B1) * gv
        vn = ADAM_B2 * v_ref[...] + (1.0 - ADAM_B2) * (gv * gv)
        d_ref[...] = -ADAM_LR * ((mn / c1) / (jnp.sqrt(vn / c2) + ADAM_EPS) + ADAM_WD * w_ref[...])
        mo_ref[...] = mn
        vo_ref[...] = vn

    blk = pl.BlockSpec((tm, cols), lambda i: (i, 0))
    shape = jax.ShapeDtypeStruct(w.shape, F32)
    return _pallas(
        body, name=name, grid=(rows // tm,),
        in_specs=[blk] * 4, out_specs=[blk] * 3, out_shape=[shape] * 3,
        compiler_params=_params(("parallel",), 16 * tm * cols * 4),
    )(w, g, m, v)


WEIGHTS = ("l0_norm", "l0_w_in", "l0_w_out", "l1_norm", "l1_w_in", "l1_q_a_norm", "l1_w_uq", "l1_kv_a_norm", "l1_w_ukv",
           "l1_q_head_norm", "l1_k_head_norm", "l1_w_out", "l2_norm", "l2_w_in", "l2_q_head_norm", "l2_k_head_norm",
           "l2_sinks", "l2_w_out", "l3_norm", "l3_w_in", "l3_w_out")


def kernel(x, l0_norm, l0_w_in, l0_w_out, l1_norm, l1_w_in, l1_q_a_norm, l1_w_uq, l1_kv_a_norm, l1_w_ukv, l1_q_head_norm, l1_k_head_norm, l1_w_out, l2_norm, l2_w_in, l2_q_head_norm, l2_k_head_norm, l2_sinks, l2_w_out, l3_norm, l3_w_in, l3_w_out, loss_target, m_l0_norm, m_l0_w_in, m_l0_w_out, m_l1_norm, m_l1_w_in, m_l1_q_a_norm, m_l1_w_uq, m_l1_kv_a_norm, m_l1_w_ukv, m_l1_q_head_norm, m_l1_k_head_norm, m_l1_w_out, m_l2_norm, m_l2_w_in, m_l2_q_head_norm, m_l2_k_head_norm, m_l2_sinks, m_l2_w_out, m_l3_norm, m_l3_w_in, m_l3_w_out, v_l0_norm, v_l0_w_in, v_l0_w_out, v_l1_norm, v_l1_w_in, v_l1_q_a_norm, v_l1_w_uq, v_l1_kv_a_norm, v_l1_w_ukv, v_l1_q_head_norm, v_l1_k_head_norm, v_l1_w_out, v_l2_norm, v_l2_w_in, v_l2_q_head_norm, v_l2_k_head_norm, v_l2_sinks, v_l2_w_out, v_l3_norm, v_l3_w_in, v_l3_w_out):
    given = dict(locals())
    w = {n: given[n] for n in WEIGHTS}
    m = {n: given["m_" + n] for n in WEIGHTS}
    v = {n: given["v_" + n] for n in WEIGHTS}
    mat_names = [t[0] for t in MATS]
    vec_names = [t[0] for t in VECS]

    packed = pack_shards({n: w[n] for n in mat_names}).astype(BF16)
    part = lambda gi: packed[_group_start(gi):_group_start(gi + 1)]
    full = unpack_full(gather_weights(part(0)), 0)
    full.update({n: w[n] for n in vec_names})
    late = [(part(gi), lambda gathered, gi=gi: unpack_full(gathered, gi)) for gi in (1, 2)]
    pair_sums = []

    def early(named):
        part, part_bf16 = reduce_scatter_begin(pack_full(named, groups=(1, 2)), "late")
        pair_sums.append(part)
        return part_bf16

    loss_tile, grad_x, grads, received = local_step(x[0], loss_target[0], full, late, early)
    block = jnp.concatenate([reduce_scatter(pack_full(grads, groups=(0,)), "first"),
                             reduce_scatter_end(pair_sums[0], received, "late")], axis=0)
    g = unpack_shards(block)
    vec_sum = sum_over_devices(pack_vecs({n: grads[n] for n in vec_names}, loss=loss_tile[0, 0]))
    loss = vec_sum[LOSS_SLOT[0], LOSS_SLOT[1]]

    delta, new_m, new_v = {}, {}, {}
    for n in mat_names:
        delta[n], new_m[n], new_v[n] = adamw(w[n], g[n], m[n], v[n], name=f"adamw_{n}")
    dv, mv, vv = adamw(pack_vecs(w), vec_sum, pack_vecs(m), pack_vecs(v), name="adamw_vecs")
    g.update(unpack_vecs(vec_sum))
    delta.update(unpack_vecs(dv))
    new_m.update(unpack_vecs(mv))
    new_v.update(unpack_vecs(vv))
    return (loss, grad_x[None], *[g[n] for n in WEIGHTS], *[delta[n] for n in WEIGHTS],
            *[new_m[n] for n in WEIGHTS], *[new_v[n] for n in WEIGHTS])
```

```python
import math

import jax
import jax.numpy as jnp
from jax import lax
from jax.experimental import pallas as pl
from jax.experimental.pallas import tpu as pltpu

F32 = jnp.float32
BF16 = jnp.bfloat16
I32 = jnp.int32
MESH = pl.DeviceIdType.MESH

NORM_EPS = 1e-6
D_MODEL = 1024
HEAD64 = 64
LANES = 128
BLK = 128
MLA_HEADS = 8
MLA_QK = 192
MLA_PAD = 256
ROPE_THETA = 10000.0
SWA_HEADS = 16
SWA_KV = 4
VMEM_CAP = 56 * 1024 * 1024
MATMUL_TILE_BYTES = 8 * 1024 * 1024

ADAM_LR, ADAM_B1, ADAM_B2, ADAM_EPS, ADAM_WD, ADAM_STEP = 0.001, 0.9, 0.999, 1e-08, 0.01, 10

NT = (((1,), (1,)), ((), ()))
NN = (((1,), (0,)), ((), ()))
TN = (((0,), (0,)), ((), ()))


def _dot(a, b, dims=NN):
    return lax.dot_general(a, b, dims, preferred_element_type=F32)


def _tile(n, pref):
    for t in (pref, 512, 256, 128):
        if t <= pref and n % t == 0:
            return t
    return n


def _params(sem, vmem_bytes):
    limit = int(min(max(2 * vmem_bytes, 24 * 1024 * 1024), VMEM_CAP))
    return pltpu.CompilerParams(dimension_semantics=sem, vmem_limit_bytes=limit)


def _in_hbm(s):
    return pltpu.HBM(s.shape, s.dtype) if len(s.shape) >= 2 else s


def _pallas(*args, out_shape, **kwargs):
    out_shape = [_in_hbm(s) for s in out_shape] if isinstance(out_shape, (list, tuple)) else _in_hbm(out_shape)
    call = pl.pallas_call(*args, out_shape=out_shape, **kwargs)

    def run(*operands):
        return call(*[pltpu.with_memory_space_constraint(a, pltpu.HBM) if a.ndim >= 2 else a for a in operands])

    return run


def _split(v):
    hi = v.astype(BF16)
    return hi, (v - hi.astype(F32)).astype(BF16)


def _dot2(v, m):
    hi, lo = _split(v)
    return _dot(hi, m) + _dot(lo, m)


def _dot_split(v, m2):
    return _dot(jnp.concatenate(_split(v), axis=1), m2)


def _first_half(shape):
    return lax.broadcasted_iota(I32, shape, 1) < HEAD64


def _sigmoid(g):
    return 1.0 / (1.0 + jnp.exp(-g))


def matmul(a, b, mode, *, name, out_dtype=F32, add=None, b_cols=None, tm=512, tn=1024):
    if mode == "nn":
        (M, K), (K2, N) = a.shape, b.shape
        if b_cols is not None:
            N = b_cols[1]
    elif mode == "nt":
        (M, K), (N, K2) = a.shape, b.shape
    else:
        (K, M), (K2, N) = a.shape, b.shape
    assert K == K2, (a.shape, b.shape, mode)
    tm, tn = _tile(M, tm), _tile(N, tn)
    while K * tm * a.dtype.itemsize > MATMUL_TILE_BYTES:
        tm //= 2
    while K * tn * b.dtype.itemsize > MATMUL_TILE_BYTES:
        tn //= 2
    dims = {"nn": NN, "nt": NT, "tn": TN}[mode]
    n_in = 2 if add is None else 3

    def body(*refs):
        a_ref, b_ref = refs[:2]
        r = _dot(a_ref[...].astype(BF16), b_ref[...].astype(BF16), dims)
        if add is not None:
            r = r + refs[2][...]
        for o_ref in refs[n_in:]:
            o_ref[...] = r.astype(o_ref.dtype)

    a_spec = pl.BlockSpec((K, tm), lambda i, j: (0, i)) if mode == "tn" else pl.BlockSpec((tm, K), lambda i, j: (i, 0))
    jb = 0 if b_cols is None else b_cols[0] // tn
    assert b_cols is None or (mode == "nn" and b_cols[0] % tn == 0)
    b_spec = pl.BlockSpec((tn, K), lambda i, j: (j, 0)) if mode == "nt" else pl.BlockSpec((K, tn), lambda i, j: (0, jb + j))
    o_spec = pl.BlockSpec((tm, tn), lambda i, j: (i, j))
    in_specs, args = [a_spec, b_spec], [a, b]
    if add is not None:
        in_specs.append(o_spec)
        args.append(add)
    vm = 2 * (tm * K * a.dtype.itemsize + K * tn * b.dtype.itemsize) + 5 * tm * tn * 4
    out_dtypes = list(out_dtype) if isinstance(out_dtype, (tuple, list)) else [out_dtype]
    res = _pallas(
        body, name=name, grid=(M // tm, N // tn),
        in_specs=in_specs, out_specs=[o_spec] * len(out_dtypes),
        out_shape=[jax.ShapeDtypeStruct((M, N), d) for d in out_dtypes],
        compiler_params=_params(("parallel", "parallel"), vm),
    )(*args)
    return res[0] if len(out_dtypes) == 1 else res


def rmsnorm_fwd(x, g, *, col_off, width, out_dtype, name, tm=512):
    S = x.shape[0]
    assert col_off % width == 0
    cb = col_off // width
    tm = _tile(S, tm)

    def body(x_ref, g_ref, o_ref):
        v = x_ref[...]
        r = lax.rsqrt(jnp.mean(v * v, axis=1, keepdims=True) + NORM_EPS)
        o_ref[...] = (v * r * g_ref[...]).astype(out_dtype)

    return _pallas(
        body, name=name, grid=(S // tm,),
        in_specs=[pl.BlockSpec((tm, width), lambda i: (i, cb)), pl.BlockSpec((1, width), lambda i: (0, 0))],
        out_specs=pl.BlockSpec((tm, width), lambda i: (i, 0)),
        out_shape=jax.ShapeDtypeStruct((S, width), out_dtype),
        compiler_params=_params(("parallel",), 4 * tm * width * 4),
    )(x, g.reshape(1, width))


def rmsnorm_bwd(x, g, dh, *, col_off, width, out_dtype, name, res=None, tm=512):
    S = x.shape[0]
    cb = col_off // width
    tm = _tile(S, tm)

    def body(*refs):
        if res is None:
            x_ref, g_ref, dh_ref, dx_ref, dg_ref = refs
        else:
            x_ref, g_ref, dh_ref, res_ref, dx_ref, dg_ref = refs

        @pl.when(pl.program_id(0) == 0)
        def _():
            dg_ref[...] = jnp.zeros_like(dg_ref)

        v = x_ref[...]
        dhv = dh_ref[...].astype(F32)
        r = lax.rsqrt(jnp.mean(v * v, axis=1, keepdims=True) + NORM_EPS)
        y = v * r
        dy = dhv * g_ref[...]
        dx = r * (dy - y * jnp.mean(dy * y, axis=1, keepdims=True))
        if res is not None:
            dx = dx + res_ref[...]
        dx_ref[...] = dx.astype(out_dtype)
        dg_ref[...] += jnp.sum(dhv * y, axis=0, keepdims=True)

    row = pl.BlockSpec((tm, width), lambda i: (i, 0))
    in_specs = [pl.BlockSpec((tm, width), lambda i: (i, cb)), pl.BlockSpec((1, width), lambda i: (0, 0)), row]
    args = [x, g.reshape(1, width), dh]
    if res is not None:
        in_specs.append(row)
        args.append(res)
    return _pallas(
        body, name=name, grid=(S // tm,),
        in_specs=in_specs,
        out_specs=[row, pl.BlockSpec((1, width), lambda i: (0, 0))],
        out_shape=[jax.ShapeDtypeStruct((S, width), out_dtype), jax.ShapeDtypeStruct((1, width), F32)],
        compiler_params=_params(("arbitrary",), 8 * tm * width * 4),
    )(*args)


def _group_ones():
    r = lax.broadcasted_iota(I32, (LANES, LANES), 0) // HEAD64
    c = lax.broadcasted_iota(I32, (LANES, LANES), 1) // HEAD64
    return (r == c).astype(BF16)


def _segmean64(v):
    return _dot2(v, _group_ones()) * (1.0 / HEAD64)


def headnorm_fwd(x, g64, *, col_off, width, name, tm=2048):
    S = x.shape[0]
    cb = col_off // LANES
    tm = _tile(S, tm)

    def body(x_ref, g_ref, o_ref):
        v = x_ref[...]
        r = lax.rsqrt(_segmean64(v * v) + NORM_EPS)
        o_ref[...] = (v * r * g_ref[...]).astype(BF16)

    return _pallas(
        body, name=name, grid=(S // tm, width // LANES),
        in_specs=[pl.BlockSpec((tm, LANES), lambda i, j: (i, cb + j)), pl.BlockSpec((1, LANES), lambda i, j: (0, 0))],
        out_specs=pl.BlockSpec((tm, LANES), lambda i, j: (i, j)),
        out_shape=jax.ShapeDtypeStruct((S, width), BF16),
        compiler_params=_params(("parallel", "parallel"), 8 * tm * LANES * 4),
    )(x, jnp.tile(g64, 2).reshape(1, LANES))


def headnorm_bwd(x, g64, dh, *, col_off, width, name, tm=2048):
    S = x.shape[0]
    cb = col_off // LANES
    tm = _tile(S, tm)

    def body(x_ref, g_ref, dh_ref, dx_ref, dg_ref):
        @pl.when((pl.program_id(0) == 0) & (pl.program_id(1) == 0))
        def _():
            dg_ref[...] = jnp.zeros_like(dg_ref)

        v = x_ref[...]
        dhv = dh_ref[...]
        r = lax.rsqrt(_segmean64(v * v) + NORM_EPS)
        y = v * r
        dy = dhv * g_ref[...]
        dx_ref[...] = (r * (dy - y * _segmean64(dy * y))).astype(BF16)
        dg_ref[...] += jnp.sum(dhv * y, axis=0, keepdims=True)

    return _pallas(
        body, name=name, grid=(width // LANES, S // tm),
        in_specs=[pl.BlockSpec((tm, LANES), lambda j, i: (i, cb + j)), pl.BlockSpec((1, LANES), lambda j, i: (0, 0)),
                  pl.BlockSpec((tm, LANES), lambda j, i: (i, j))],
        out_specs=[pl.BlockSpec((tm, LANES), lambda j, i: (i, j)), pl.BlockSpec((1, LANES), lambda j, i: (0, 0))],
        out_shape=[jax.ShapeDtypeStruct((S, width), BF16), jax.ShapeDtypeStruct((1, LANES), F32)],
        compiler_params=_params(("arbitrary", "arbitrary"), 10 * tm * LANES * 4),
    )(x, jnp.tile(g64, 2).reshape(1, LANES), dh)


def gate_fwd(o, proj, *, gate_off, name, tm=512):
    S, W = o.shape
    cb = gate_off // W
    tm = _tile(S, tm)

    def body(o_ref, g_ref, out_ref):
        g = g_ref[...]
        out_ref[...] = (o_ref[...] * (g * _sigmoid(g))).astype(BF16)

    return _pallas(
        body, name=name, grid=(S // tm,),
        in_specs=[pl.BlockSpec((tm, W), lambda i: (i, 0)), pl.BlockSpec((tm, W), lambda i: (i, cb))],
        out_specs=pl.BlockSpec((tm, W), lambda i: (i, 0)),
        out_shape=jax.ShapeDtypeStruct((S, W), BF16),
        compiler_params=_params(("parallel",), 6 * tm * W * 4),
    )(o, proj)


def gate_bwd(dog, o, proj, *, gate_off, name, tm=512):
    S, W = o.shape
    cb = gate_off // W
    tm = _tile(S, tm)

    def body(d_ref, o_ref, g_ref, do_ref, dg_ref):
        g = g_ref[...]
        d = d_ref[...]
        s = _sigmoid(g)
        do_ref[...] = d * (g * s)
        dg_ref[...] = (d * o_ref[...] * (s * (1.0 + g * (1.0 - s)))).astype(BF16)

    row = pl.BlockSpec((tm, W), lambda i: (i, 0))
    return _pallas(
        body, name=name, grid=(S // tm,),
        in_specs=[row, row, pl.BlockSpec((tm, W), lambda i: (i, cb))],
        out_specs=[row, row],
        out_shape=[jax.ShapeDtypeStruct((S, W), F32), jax.ShapeDtypeStruct((S, W), BF16)],
        compiler_params=_params(("parallel",), 10 * tm * W * 4),
    )(dog, o, proj)


def loss_head(y, target, *, name, tm=256):
    S, W = y.shape
    tm = _tile(S, tm)
    n = S // tm

    def body(y_ref, t_ref, dy_ref, l_ref, acc_ref):
        i = pl.program_id(0)

        @pl.when(i == 0)
        def _():
            acc_ref[...] = jnp.zeros_like(acc_ref)

        e = y_ref[...] - t_ref[...]
        dy_ref[...] = e * (1.0 / W)
        acc_ref[...] += jnp.sum(e * e, axis=0, keepdims=True)

        @pl.when(i == n - 1)
        def _():
            l_ref[...] = jnp.full(l_ref.shape, (0.5 / W) * jnp.sum(acc_ref[...]), F32)

    row = pl.BlockSpec((tm, W), lambda i: (i, 0))
    return _pallas(
        body, name=name, grid=(n,),
        in_specs=[row, row],
        out_specs=[row, pl.BlockSpec((8, LANES), lambda i: (0, 0))],
        out_shape=[jax.ShapeDtypeStruct((S, W), F32), jax.ShapeDtypeStruct((8, LANES), F32)],
        scratch_shapes=[pltpu.VMEM((1, W), F32)],
        compiler_params=_params(("arbitrary",), 8 * tm * W * 4),
    )(y, target)


def _stack_heads(t, zero):
    first = _first_half(t.shape)
    return jnp.concatenate([jnp.where(first, t, zero), jnp.where(first, zero, t)], axis=0)


def _sb_weights(qs, ks, mask, upper, rss):
    zs = [_dot(q, k, NT) for q, k in zip(qs, ks)]
    sps = [jnp.maximum(z, 0.0) + jnp.log(1.0 + jnp.exp(-jnp.abs(z))) for z in zs]
    gs = [z - sp for z, sp in zip(zs, sps)]
    if mask is not None:
        sps = [jnp.where(mask, sp, 0.0) for sp in sps]
    cums = [_dot_split(sp, upper) for sp in sps]
    avs = [jnp.exp(g - (cum + rs)) for g, cum, rs in zip(gs, cums, rss)]
    if mask is not None:
        avs = [jnp.where(mask, a, 0.0) for a in avs]
    return avs, sps, gs


def _sb_consts():
    row = lax.broadcasted_iota(I32, (BLK, BLK), 0)
    col = lax.broadcasted_iota(I32, (BLK, BLK), 1)
    diag = col < row
    return row, col, jnp.concatenate([diag, diag], axis=0)


SB_DEAD = 88.0


def _sb_walk_left(block, i, carry):
    def least(c):
        m = c[0][1]
        for pair in c[1:]:
            m = jnp.minimum(m, pair[1])
        return jnp.min(m)

    def cond(state):
        jj, _, low = state
        return (jj < i) & (low < SB_DEAD)

    def body(state):
        jj, c, _ = state
        c = block(i - 1 - jj, c, None)
        return jj + 1, c, least(c)

    return lax.while_loop(cond, body, (jnp.int32(0), carry, least(carry)))[1]


SB_W = 1024


def sb_attn_fwd(qkv, *, name, pairs=8, ride=None):
    S = qkv.shape[0]
    W = SB_W
    PW = pairs * LANES
    ngrp, nq = W // PW, S // BLK
    assert ride is None or (ngrp == 1 and nq >= 3)

    def body(*refs):
        if ride is None:
            q_ref, k_ref, v_ref, o_ref = refs
        else:
            q_ref, k_ref, v_ref, o_ref = refs[0], refs[1], refs[2], refs[4]
            send, forward, finish = _gather_steps(refs[3], refs[5], refs[6], refs[7])
        i = pl.program_id(1)
        if ride is not None:
            pl.when(i == 0)(send)
            pl.when(i == nq // 2)(forward)
        row, col, diag = _sb_consts()
        upper = jnp.tile((row > col).astype(BF16), (2, 1))
        zero = jnp.zeros((BLK, LANES), BF16)
        qs = [_stack_heads(q_ref[:, p * LANES:(p + 1) * LANES] * 0.125, zero) for p in range(pairs)]

        def block(j, carry, mask):
            rows = pl.ds(pl.multiple_of(j * BLK, BLK), BLK)
            cols = [slice(p * LANES, (p + 1) * LANES) for p in range(pairs)]
            avs, sps, _ = _sb_weights(qs, [k_ref[rows, c] for c in cols], mask, upper, [c[1] for c in carry])
            abs_ = [a.astype(BF16) for a in avs]
            outs = [_dot(jnp.concatenate([ab[:BLK], ab[BLK:]], axis=1), _stack_heads(v_ref[rows, c], zero))
                    for ab, c in zip(abs_, cols)]
            return tuple((carry[p][0] + outs[p], carry[p][1] + jnp.sum(sps[p], axis=1, keepdims=True))
                         for p in range(pairs))

        init = tuple((jnp.zeros((BLK, LANES), F32), jnp.zeros((2 * BLK, 1), F32)) for _ in range(pairs))
        carry = _sb_walk_left(block, i, block(i, init, diag))
        for p in range(pairs):
            o_ref[:, p * LANES:(p + 1) * LANES] = carry[p][0]
        if ride is not None:
            pl.when(i == nq - 1)(finish)

    once = pl.Buffered(1)
    hbm = pl.BlockSpec(memory_space=pltpu.HBM)
    in_specs = [pl.BlockSpec((BLK, PW), lambda p, i: (i, p)),
                pl.BlockSpec((S, PW), lambda p, i: (0, ngrp + p), pipeline_mode=once),
                pl.BlockSpec((S, PW), lambda p, i: (0, 2 * ngrp + p), pipeline_mode=once)]
    out_specs = [pl.BlockSpec((BLK, PW), lambda p, i: (i, p))]
    out_shape = [jax.ShapeDtypeStruct((S, W), F32)]
    args = [qkv, qkv, qkv]
    if ride is not None:
        in_specs.append(hbm)
        out_specs.append(hbm)
        out_shape.append(jax.ShapeDtypeStruct((N_CHIPS,) + ride.shape, ride.dtype))
        args.append(ride)
    res = _pallas(
        body, name=name, grid=(ngrp, nq),
        in_specs=in_specs, out_specs=out_specs, out_shape=out_shape,
        scratch_shapes=[] if ride is None else list(GATHER_SEMS),
        compiler_params=_params(("parallel", "arbitrary"), 2 * S * PW * 2 + 16 * BLK * PW * 4),
    )(*args)
    return res[0] if ride is None else (res[0], place_own_block(res[1], ride))


def sb_attn_bwd(qkv, o, do, *, name, pairs=4, ride=None):
    S = qkv.shape[0]
    W = SB_W
    PW = pairs * LANES
    ngrp, nq = W // PW, S // BLK

    def body(*refs):
        q_ref, k_ref, v_ref, o_ref, do_ref = refs[:5]
        if ride is None:
            dq_ref, dk_ref, dv_ref = refs[5:]
        else:
            dq_ref, dk_ref, dv_ref = refs[6:9]
            send, finish = _chip_exchange_steps(refs[5], refs[9], refs[10], refs[11])
            pl.when((pl.program_id(0) == 0) & (pl.program_id(1) == 0))(send)
        i = pl.program_id(1)

        @pl.when(i == 0)
        def _():
            dk_ref[...] = jnp.zeros_like(dk_ref)
            dv_ref[...] = jnp.zeros_like(dv_ref)

        row, col, diag = _sb_consts()
        upper = jnp.tile((row > col).astype(BF16), (2, 1))
        upper_incl = jnp.tile((row >= col).astype(BF16), (2, 1))
        first = _first_half((BLK, LANES))
        zero = jnp.zeros((BLK, LANES), BF16)
        qs, dos, tots = [], [], []
        for p in range(pairs):
            cols = slice(p * LANES, (p + 1) * LANES)
            qs.append(_stack_heads(q_ref[:, cols] * 0.125, zero))
            dob = do_ref[:, cols].astype(BF16)
            dos.append(_stack_heads(dob, zero))
            prod = dob.astype(F32) * o_ref[:, cols]
            tots.append(jnp.concatenate([jnp.sum(jnp.where(first, prod, 0.0), axis=1, keepdims=True),
                                         jnp.sum(jnp.where(first, 0.0, prod), axis=1, keepdims=True)], axis=0))

        def block(j, carry, mask):
            rows = pl.ds(pl.multiple_of(j * BLK, BLK), BLK)
            P = range(pairs)
            cols = [slice(p * LANES, (p + 1) * LANES) for p in P]
            ks = [k_ref[rows, c] for c in cols]
            das = [_dot(dos[p], v_ref[rows, cols[p]], NT) for p in P]
            avs, sps, gs = _sb_weights(qs, ks, mask, upper, [c[1] for c in carry])
            abs_ = [a.astype(BF16) for a in avs]
            es = [ab.astype(F32) * da for ab, da in zip(abs_, das)]
            sufs = [_dot_split(e, upper_incl) for e in es]
            lefts = [tots[p] - (sufs[p] + carry[p][2]) for p in P]
            dzs = [es[p] - jnp.exp(gs[p]) * (es[p] + lefts[p]) for p in P]
            if mask is not None:
                dzs = [jnp.where(mask, dz, 0.0) for dz in dzs]
            dzbs = [dz.astype(BF16) for dz in dzs]
            dks = [_dot(dzbs[p], qs[p], TN) for p in P]
            dvs = [_dot(abs_[p], dos[p], TN) for p in P]
            dqs = [_dot(jnp.concatenate([dzbs[p][:BLK], dzbs[p][BLK:]], axis=1), _stack_heads(ks[p], zero)) for p in P]
            for p in P:
                dk_ref[rows, cols[p]] += dks[p]
                dv_ref[rows, cols[p]] += dvs[p]
            return tuple((carry[p][0] + dqs[p], carry[p][1] + jnp.sum(sps[p], axis=1, keepdims=True),
                          carry[p][2] + jnp.sum(es[p], axis=1, keepdims=True)) for p in P)

        col0 = jnp.zeros((2 * BLK, 1), F32)
        init = tuple((jnp.zeros((BLK, LANES), F32), col0, col0) for _ in range(pairs))
        carry = _sb_walk_left(block, i, block(i, init, diag))
        for p in range(pairs):
            dq_ref[:, p * LANES:(p + 1) * LANES] = (carry[p][0] * 0.125).astype(BF16)
        if ride is not None:
            pl.when((pl.program_id(0) == ngrp - 1) & (i == nq - 1))(finish)

    once = pl.Buffered(1)
    tile = pl.BlockSpec((BLK, PW), lambda p, i: (i, p))
    full = pl.BlockSpec((S, PW), lambda p, i: (0, p), pipeline_mode=once)
    shape = jax.ShapeDtypeStruct((S, W), F32)
    hbm = pl.BlockSpec(memory_space=pltpu.HBM)
    in_specs = [tile,
                pl.BlockSpec((S, PW), lambda p, i: (0, ngrp + p), pipeline_mode=once),
                pl.BlockSpec((S, PW), lambda p, i: (0, 2 * ngrp + p), pipeline_mode=once),
                tile, tile]
    out_specs = [tile, full, full]
    out_shape = [jax.ShapeDtypeStruct((S, W), BF16), shape, shape]
    args = [qkv, qkv, qkv, o, do]
    if ride is not None:
        in_specs.append(hbm)
        out_specs.append(hbm)
        out_shape.append(jax.ShapeDtypeStruct((len(OTHER_CHIPS),) + ride.shape[1:], ride.dtype))
        args.append(ride)
    order = ("parallel", "arbitrary") if ride is None else ("arbitrary", "arbitrary")
    return _pallas(
        body, name=name, grid=(ngrp, nq),
        in_specs=in_specs, out_specs=out_specs, out_shape=out_shape,
        scratch_shapes=[] if ride is None else list(CHIP_SEMS),
        compiler_params=_params(order, 2 * S * PW * 2 + 2 * S * PW * 4 + 16 * BLK * PW * 4),
    )(*args)


FB = 256


def flash_fwd(qn, kn, vb, *, name, heads=4, ride=None):
    S = qn.shape[0]
    H, DK, DV = MLA_HEADS, MLA_PAD, LANES
    nq, ngrp = S // FB, H // heads
    scale = 1.0 / math.sqrt(MLA_QK)
    assert ride is None or (ngrp >= 2 and nq >= 2)

    def body(*refs):
        if ride is None:
            q_ref, k_ref, v_ref, o_ref, lse_ref = refs
        else:
            q_ref, k_ref, v_ref, o_ref, lse_ref = refs[0], refs[1], refs[2], refs[4], refs[5]
            send, forward, finish = _gather_steps(refs[3], refs[6], refs[7], refs[8])
        g, i = pl.program_id(0), pl.program_id(1)
        if ride is not None:
            pl.when((g == 0) & (i == 0))(send)
            pl.when((g == 1) & (i == 0))(forward)
        diag = lax.broadcasted_iota(I32, (FB, FB), 1) <= lax.broadcasted_iota(I32, (FB, FB), 0)
        qs = [q_ref[:, h * DK:(h + 1) * DK] for h in range(heads)]

        def block(j, carry, mask):
            rows = pl.ds(pl.multiple_of(j * FB, FB), FB)
            H = range(heads)
            ss = [_dot(qs[h], k_ref[rows, h * DK:(h + 1) * DK], NT) * scale for h in H]
            if mask is not None:
                ss = [jnp.where(mask, s, -1e30) for s in ss]
            ms = [jnp.maximum(carry[h][1], jnp.max(ss[h], axis=1, keepdims=True)) for h in H]
            ps = [jnp.exp(ss[h] - ms[h]) for h in H]
            ws = [jnp.exp(carry[h][1] - ms[h]) for h in H]
            pvs = [_dot(ps[h].astype(BF16), v_ref[rows, h * DV:(h + 1) * DV]) for h in H]
            return tuple((carry[h][0] * ws[h] + pvs[h], ms[h], carry[h][2] * ws[h] + jnp.sum(ps[h], axis=1, keepdims=True))
                         for h in H)

        init = tuple((jnp.zeros((FB, DV), F32), jnp.full((FB, 1), -1e30, F32), jnp.zeros((FB, 1), F32))
                     for _ in range(heads))
        carry = lax.fori_loop(0, i, lambda j, c: block(j, c, None), init)
        carry = block(i, carry, diag)
        for h in range(heads):
            acc, m, l = carry[h]
            o_ref[:, h * DV:(h + 1) * DV] = acc / l
            lse_ref[h] = m + jnp.log(l)
        if ride is not None:
            pl.when((g == ngrp - 1) & (i == nq - 1))(finish)

    hbm = pl.BlockSpec(memory_space=pltpu.HBM)
    in_specs = [pl.BlockSpec((FB, heads * DK), lambda g, i: (i, g)),
                pl.BlockSpec((S, heads * DK), lambda g, i: (0, g), pipeline_mode=pl.Buffered(1)),
                pl.BlockSpec((S, heads * DV), lambda g, i: (0, g), pipeline_mode=pl.Buffered(1))]
    out_specs = [pl.BlockSpec((FB, heads * DV), lambda g, i: (i, g)), pl.BlockSpec((heads, FB, 1), lambda g, i: (g, i, 0))]
    out_shape = [jax.ShapeDtypeStruct((S, H * DV), F32), jax.ShapeDtypeStruct((H, S, 1), F32)]
    args = [qn, kn, vb]
    if ride is not None:
        in_specs.append(hbm)
        out_specs.append(hbm)
        out_shape.append(jax.ShapeDtypeStruct((N_CHIPS,) + ride.shape, ride.dtype))
        args.append(ride)
    order = ("parallel", "arbitrary") if ride is None else ("arbitrary", "arbitrary")
    res = _pallas(
        body, name=name, grid=(ngrp, nq),
        in_specs=in_specs, out_specs=out_specs, out_shape=out_shape,
        scratch_shapes=[] if ride is None else list(GATHER_SEMS),
        compiler_params=_params(order, 2 * S * heads * (DK + DV) * 2 + 16 * FB * FB * 4),
    )(*args)
    return tuple(res) if ride is None else (res[0], res[1], place_own_block(res[2], ride))


def flash_bwd(qn, kn, vb, o, lse, do, *, name, heads=4):
    S = qn.shape[0]
    H, DK, DV = MLA_HEADS, MLA_PAD, LANES
    nq, ngrp = S // FB, H // heads
    scale = 1.0 / math.sqrt(MLA_QK)

    def body(q_ref, k_ref, v_ref, o_ref, lse_ref, do_ref, dq_ref, dk_ref, dv_ref):
        i = pl.program_id(1)

        @pl.when(i == 0)
        def _():
            dk_ref[...] = jnp.zeros_like(dk_ref)
            dv_ref[...] = jnp.zeros_like(dv_ref)

        diag = lax.broadcasted_iota(I32, (FB, FB), 1) <= lax.broadcasted_iota(I32, (FB, FB), 0)
        qs, dobs, deltas, lses = [], [], [], []
        for h in range(heads):
            do = do_ref[:, h * DV:(h + 1) * DV]
            qs.append(q_ref[:, h * DK:(h + 1) * DK])
            dobs.append(do.astype(BF16))
            deltas.append(jnp.sum(do * o_ref[:, h * DV:(h + 1) * DV], axis=1, keepdims=True))
            lses.append(lse_ref[h])

        def block(j, carry, mask):
            rows = pl.ds(pl.multiple_of(j * FB, FB), FB)
            H = range(heads)
            kcs = [slice(h * DK, (h + 1) * DK) for h in H]
            vcs = [slice(h * DV, (h + 1) * DV) for h in H]
            ks = [k_ref[rows, kcs[h]] for h in H]
            ss = [_dot(qs[h], ks[h], NT) for h in H]
            dps = [_dot(dobs[h], v_ref[rows, vcs[h]], NT) for h in H]
            ps = [jnp.exp(ss[h] * scale - lses[h]) for h in H]
            if mask is not None:
                ps = [jnp.where(mask, p, 0.0) for p in ps]
            pbs = [p.astype(BF16) for p in ps]
            dss = [(ps[h] * (dps[h] - deltas[h]) * scale).astype(BF16) for h in H]
            dvs = [_dot(pbs[h], dobs[h], TN) for h in H]
            dks = [_dot(dss[h], qs[h], TN) for h in H]
            dqs = [_dot(dss[h], ks[h]) for h in H]
            for h in H:
                dv_ref[rows, vcs[h]] += dvs[h]
                dk_ref[rows, kcs[h]] += dks[h]
            return tuple(carry[h] + dqs[h] for h in H)

        carry = lax.fori_loop(0, i, lambda j, c: block(j, c, None), tuple(jnp.zeros((FB, DK), F32) for _ in range(heads)))
        carry = block(i, carry, diag)
        for h in range(heads):
            dq_ref[:, h * DK:(h + 1) * DK] = carry[h]

    qtile = pl.BlockSpec((FB, heads * DK), lambda g, i: (i, g))
    otile = pl.BlockSpec((FB, heads * DV), lambda g, i: (i, g))
    once = pl.Buffered(1)
    kfull = pl.BlockSpec((S, heads * DK), lambda g, i: (0, g), pipeline_mode=once)
    vfull = pl.BlockSpec((S, heads * DV), lambda g, i: (0, g), pipeline_mode=once)
    return _pallas(
        body, name=name, grid=(ngrp, nq),
        in_specs=[qtile, kfull, vfull, otile, pl.BlockSpec((heads, FB, 1), lambda g, i: (g, i, 0)), otile],
        out_specs=[qtile, kfull, vfull],
        out_shape=[jax.ShapeDtypeStruct((S, H * DK), F32), jax.ShapeDtypeStruct((S, H * DK), F32),
                   jax.ShapeDtypeStruct((S, H * DV), F32)],
        compiler_params=_params(("parallel", "arbitrary"), S * heads * (DK + DV) * 6 + 16 * FB * FB * 4),
    )(qn, kn, vb, o, lse, do)


def _rope_tables(S):
    half = 32
    inv_freq = ROPE_THETA ** (-jnp.arange(half, dtype=F32) / half)
    ang = jnp.arange(S).astype(F32)[:, None] * inv_freq[None, :]
    cos, sin = jnp.cos(ang), jnp.sin(ang)
    ones, zeros = jnp.ones((S, LANES), F32), jnp.zeros((S, LANES), F32)
    pad = jnp.zeros((S, 64), F32)
    return (jnp.concatenate([ones, cos, cos, pad + 1.0], axis=1),
            jnp.concatenate([zeros, -sin, sin, pad], axis=1))


def _rope_partner(u):
    lane = lax.broadcasted_iota(I32, u.shape, 1)
    return jnp.where((lane % HEAD64) < 32, pltpu.roll(u, LANES - 32, 1), pltpu.roll(u, 32, 1))


def _normrope(raw, g, cos, sgn):
    r = lax.rsqrt(jnp.sum(raw * raw, axis=1, keepdims=True) * (1.0 / MLA_QK) + NORM_EPS)
    y = raw * r
    u = y * g
    pe = u[:, LANES:]
    out = jnp.concatenate([u[:, :LANES], pe * cos[:, LANES:] + _rope_partner(pe) * sgn[:, LANES:]], axis=1)
    return out, y, r


def _normrope_bwd(dout, g, cos, sgn, y, r):
    dpe = dout[:, LANES:]
    du = jnp.concatenate([dout[:, :LANES], dpe * cos[:, LANES:] + _rope_partner(dpe * sgn[:, LANES:])], axis=1)
    dy = du * g
    draw = r * (dy - y * (jnp.sum(dy * y, axis=1, keepdims=True) * (1.0 / MLA_QK)))
    return draw, jnp.sum(du * y, axis=0, keepdims=True)


def mla_prep_fwd(qraw, kv, proj, gq, gk, cos, sgn, *, kpe_off, name, tm=1024):
    S = qraw.shape[0]
    tm = _tile(S, tm)
    kb = kpe_off // LANES

    def body(q_ref, kn_ref, v_ref, kpe_ref, gq_ref, gk_ref, c_ref, s_ref, qo_ref, ko_ref, vo_ref):
        cos, sgn = c_ref[...], s_ref[...]
        qo_ref[...] = _normrope(q_ref[...], gq_ref[...], cos, sgn)[0].astype(BF16)
        kraw = jnp.concatenate([kn_ref[...], kpe_ref[...]], axis=1)
        ko_ref[...] = _normrope(kraw, gk_ref[...], cos, sgn)[0].astype(BF16)
        vo_ref[...] = v_ref[...].astype(BF16)

    head = pl.BlockSpec((tm, MLA_PAD), lambda i, h: (i, h))
    gain = pl.BlockSpec((1, MLA_PAD), lambda i, h: (0, 0))
    tab = pl.BlockSpec((tm, MLA_PAD), lambda i, h: (i, 0))
    return _pallas(
        body, name=name, grid=(S // tm, MLA_HEADS),
        in_specs=[head, pl.BlockSpec((tm, LANES), lambda i, h: (i, 2 * h)), pl.BlockSpec((tm, LANES), lambda i, h: (i, 2 * h + 1)),
                  pl.BlockSpec((tm, LANES), lambda i, h: (i, kb)), gain, gain, tab, tab],
        out_specs=[head, head, pl.BlockSpec((tm, LANES), lambda i, h: (i, h))],
        out_shape=[jax.ShapeDtypeStruct(qraw.shape, BF16), jax.ShapeDtypeStruct(qraw.shape, BF16),
                   jax.ShapeDtypeStruct((S, MLA_HEADS * LANES), BF16)],
        compiler_params=_params(("parallel", "arbitrary"), 16 * tm * MLA_PAD * 4),
    )(qraw, kv, kv, proj, gq, gk, cos, sgn)


def mla_prep_bwd(dqn, dkn, dv, qraw, kv, proj, gq, gk, cos, sgn, *, kpe_off, name, tm=512):
    S = qraw.shape[0]
    tm = _tile(S, tm)
    kb = kpe_off // LANES

    def body(dq_ref, dk_ref, dv_ref, q_ref, kn_ref, kpe_ref, gq_ref, gk_ref, c_ref, s_ref,
             dqo_ref, dkv_ref, dkpe_ref, dgq_ref, dgk_ref, acc_ref):
        i, h = pl.program_id(0), pl.program_id(1)

        @pl.when((i == 0) & (h == 0))
        def _():
            dgq_ref[...] = jnp.zeros_like(dgq_ref)
            dgk_ref[...] = jnp.zeros_like(dgk_ref)

        @pl.when(h == 0)
        def _():
            acc_ref[...] = jnp.zeros_like(acc_ref)

        cos, sgn = c_ref[...], s_ref[...]
        _, yq, rq = _normrope(q_ref[...], gq_ref[...], cos, sgn)
        dq, dgq = _normrope_bwd(dq_ref[...], gq_ref[...], cos, sgn, yq, rq)
        dqo_ref[...] = dq.astype(BF16)
        dgq_ref[...] += dgq
        kraw = jnp.concatenate([kn_ref[...], kpe_ref[...]], axis=1)
        _, yk, rk = _normrope(kraw, gk_ref[...], cos, sgn)
        dk, dgk = _normrope_bwd(dk_ref[...], gk_ref[...], cos, sgn, yk, rk)
        dgk_ref[...] += dgk
        dkv_ref[...] = jnp.concatenate([dk[:, :LANES], dv_ref[...]], axis=1).astype(BF16)
        acc_ref[...] += dk[:, LANES:]

        @pl.when(h == MLA_HEADS - 1)
        def _():
            dkpe_ref[...] = acc_ref[...].astype(BF16)

    head = pl.BlockSpec((tm, MLA_PAD), lambda i, h: (i, h))
    gain = pl.BlockSpec((1, MLA_PAD), lambda i, h: (0, 0))
    tab = pl.BlockSpec((tm, MLA_PAD), lambda i, h: (i, 0))
    return _pallas(
        body, name=name, grid=(S // tm, MLA_HEADS),
        in_specs=[head, head, pl.BlockSpec((tm, LANES), lambda i, h: (i, h)), head,
                  pl.BlockSpec((tm, LANES), lambda i, h: (i, 2 * h)), pl.BlockSpec((tm, LANES), lambda i, h: (i, kb)),
                  gain, gain, tab, tab],
        out_specs=[head, head, pl.BlockSpec((tm, LANES), lambda i, h: (i, 0)), gain, gain],
        out_shape=[jax.ShapeDtypeStruct(qraw.shape, BF16), jax.ShapeDtypeStruct(qraw.shape, BF16),
                   jax.ShapeDtypeStruct((S, LANES), BF16), jax.ShapeDtypeStruct((1, MLA_PAD), F32),
                   jax.ShapeDtypeStruct((1, MLA_PAD), F32)],
        scratch_shapes=[pltpu.VMEM((tm, LANES), F32)],
        compiler_params=_params(("arbitrary", "arbitrary"), 24 * tm * MLA_PAD * 4),
    )(dqn, dkn, dv, qraw, kv, proj, gq, gk, cos, sgn)


def swa_kv_prep(proj, gk64, *, k_off, v_off, name, tm=512):
    S = proj.shape[0]
    tm = _tile(S, tm)
    W = SWA_KV * HEAD64

    def body(k_ref, v_ref, g_ref, ko_ref, vo_ref):
        first = _first_half((tm, LANES))

        def dup(n):
            nr = pltpu.roll(n, HEAD64, 1)
            return jnp.where(first, n, nr), jnp.where(first, nr, n)

        for t in range(W // LANES):
            x = k_ref[:, t * LANES:(t + 1) * LANES]
            n = x * lax.rsqrt(_segmean64(x * x) + NORM_EPS) * g_ref[...]
            d0, d1 = dup(n)
            ko_ref[:, 2 * t * LANES:(2 * t + 1) * LANES] = d0.astype(BF16)
            ko_ref[:, (2 * t + 1) * LANES:(2 * t + 2) * LANES] = d1.astype(BF16)
            d0, d1 = dup(v_ref[:, t * LANES:(t + 1) * LANES])
            vo_ref[:, 2 * t * LANES:(2 * t + 1) * LANES] = d0.astype(BF16)
            vo_ref[:, (2 * t + 1) * LANES:(2 * t + 2) * LANES] = d1.astype(BF16)

    out = pl.BlockSpec((tm, SWA_KV * LANES), lambda i: (i, 0))
    shape = jax.ShapeDtypeStruct((S, SWA_KV * LANES), BF16)
    return _pallas(
        body, name=name, grid=(S // tm,),
        in_specs=[pl.BlockSpec((tm, W), lambda i: (i, k_off // W)), pl.BlockSpec((tm, W), lambda i: (i, v_off // W)),
                  pl.BlockSpec((1, LANES), lambda i: (0, 0))],
        out_specs=[out, out], out_shape=[shape, shape],
        compiler_params=_params(("parallel",), 12 * tm * W * 4),
    )(proj, proj, jnp.tile(gk64, 2).reshape(1, LANES))


def swa_kv_prep_bwd(dkdup, dvdup, proj, gk64, *, k_off, name, tm=512):
    S = proj.shape[0]
    tm = _tile(S, tm)
    W = SWA_KV * HEAD64

    def body(dk_ref, dv_ref, k_ref, g_ref, dko_ref, dvo_ref, dg_ref):
        @pl.when(pl.program_id(0) == 0)
        def _():
            dg_ref[...] = jnp.zeros_like(dg_ref)

        first = _first_half((tm, LANES))

        def fold(ref, t):
            d0 = ref[:, 2 * t * LANES:(2 * t + 1) * LANES]
            d1 = ref[:, (2 * t + 1) * LANES:(2 * t + 2) * LANES]
            return jnp.where(first, d0 + pltpu.roll(d0, HEAD64, 1), d1 + pltpu.roll(d1, HEAD64, 1))

        for t in range(W // LANES):
            dvo_ref[:, t * LANES:(t + 1) * LANES] = fold(dv_ref, t).astype(BF16)
            dh = fold(dk_ref, t)
            x = k_ref[:, t * LANES:(t + 1) * LANES]
            r = lax.rsqrt(_segmean64(x * x) + NORM_EPS)
            y = x * r
            dy = dh * g_ref[...]
            dko_ref[:, t * LANES:(t + 1) * LANES] = (r * (dy - y * _segmean64(dy * y))).astype(BF16)
            dg_ref[...] += jnp.sum(dh * y, axis=0, keepdims=True)

    dup = pl.BlockSpec((tm, SWA_KV * LANES), lambda i: (i, 0))
    out = pl.BlockSpec((tm, W), lambda i: (i, 0))
    shape = jax.ShapeDtypeStruct((S, W), BF16)
    return _pallas(
        body, name=name, grid=(S // tm,),
        in_specs=[dup, dup, pl.BlockSpec((tm, W), lambda i: (i, k_off // W)), pl.BlockSpec((1, LANES), lambda i: (0, 0))],
        out_specs=[out, out, pl.BlockSpec((1, LANES), lambda i: (0, 0))],
        out_shape=[shape, shape, jax.ShapeDtypeStruct((1, LANES), F32)],
        compiler_params=_params(("arbitrary",), 16 * tm * W * 4),
    )(dkdup, dvdup, proj, jnp.tile(gk64, 2).reshape(1, LANES))


def _swa_geometry(i):
    r = lax.broadcasted_iota(I32, (BLK, 2 * BLK), 0)
    c = lax.broadcasted_iota(I32, (BLK, 2 * BLK), 1)
    rel = r + BLK - c
    valid = (rel >= 0) & (rel < BLK) & ((c >= BLK) | (i > 0))
    return valid, rel.astype(F32)


def _swa_slope(h):
    return 2.0 ** (-8.0 * (h + 1) / SWA_HEADS)


def swa_attn_fwd(qn, kdup, vdup, sinks, *, name):
    S = qn.shape[0]
    nq = S // BLK
    group = SWA_HEADS // SWA_KV

    def body(q_ref, kp_ref, kc_ref, vp_ref, vc_ref, sink_ref, o_ref, lse_ref):
        i = pl.program_id(0)
        valid, rel = _swa_geometry(i)
        first = _first_half((BLK, LANES))
        lane = lax.broadcasted_iota(I32, (BLK, LANES), 1)
        lse_all = jnp.zeros((BLK, LANES), F32)
        zero = jnp.zeros((BLK, LANES), BF16)
        H = range(SWA_HEADS)
        kks = [jnp.concatenate([kp_ref[:, g * LANES:(g + 1) * LANES], kc_ref[:, g * LANES:(g + 1) * LANES]], axis=0)
               for g in range(SWA_KV)]
        vvs = [jnp.concatenate([vp_ref[:, g * LANES:(g + 1) * LANES], vc_ref[:, g * LANES:(g + 1) * LANES]], axis=0)
               for g in range(SWA_KV)]
        q2s = [q_ref[:, t * LANES:(t + 1) * LANES] for t in range(SWA_HEADS // 2)]
        qhs = [jnp.where(first if h % 2 == 0 else ~first, q2s[h // 2], zero) for h in H]
        ss = [_dot(qhs[h], kks[h // group], NT) * (1.0 / math.sqrt(HEAD64)) - _swa_slope(h) * rel for h in H]
        ss = [jnp.where(valid, s, -1e30) for s in ss]
        ms = [jnp.maximum(jnp.max(ss[h], axis=1, keepdims=True), sink_ref[h]) for h in H]
        es = [jnp.exp(ss[h] - ms[h]) for h in H]
        dens = [jnp.sum(es[h], axis=1, keepdims=True) + jnp.exp(sink_ref[h] - ms[h]) for h in H]
        outs = [_dot((es[h] / dens[h]).astype(BF16), vvs[h // group]) for h in H]
        for h in H:
            lse_all = jnp.where(lane == h, ms[h] + jnp.log(dens[h]), lse_all)
        for t in range(SWA_HEADS // 2):
            o_ref[:, t * LANES:(t + 1) * LANES] = jnp.where(first, outs[2 * t], outs[2 * t + 1])
        lse_ref[...] = lse_all

    prev = lambda i: (jnp.maximum(i - 1, 0), 0)
    cur = lambda i: (i, 0)
    kvw = SWA_KV * LANES
    return _pallas(
        body, name=name, grid=(nq,),
        in_specs=[pl.BlockSpec((BLK, 1024), cur), pl.BlockSpec((BLK, kvw), prev), pl.BlockSpec((BLK, kvw), cur),
                  pl.BlockSpec((BLK, kvw), prev), pl.BlockSpec((BLK, kvw), cur),
                  pl.BlockSpec(memory_space=pltpu.SMEM)],
        out_specs=[pl.BlockSpec((BLK, 1024), cur), pl.BlockSpec((BLK, LANES), cur)],
        out_shape=[jax.ShapeDtypeStruct((S, 1024), F32), jax.ShapeDtypeStruct((S, LANES), F32)],
        compiler_params=_params(("parallel",), 16 * BLK * 1024 * 4),
    )(qn, kdup, kdup, vdup, vdup, sinks)


def swa_attn_bwd(qn, kdup, vdup, sinks, o, lse, do, *, name):
    S = qn.shape[0]
    nq = S // BLK
    group = SWA_HEADS // SWA_KV
    scale = 1.0 / math.sqrt(HEAD64)

    def body(q_ref, kp_ref, kc_ref, vp_ref, vc_ref, sink_ref, o_ref, lse_ref, do_ref,
             dq_ref, dk_ref, dv_ref, ds_ref):
        i = pl.program_id(0)

        @pl.when(i == 0)
        def _():
            dk_ref[...] = jnp.zeros_like(dk_ref)
            dv_ref[...] = jnp.zeros_like(dv_ref)
            ds_ref[...] = jnp.zeros_like(ds_ref)

        valid, rel = _swa_geometry(i)
        first = _first_half((BLK, LANES))
        lane1 = lax.broadcasted_iota(I32, (1, LANES), 1)
        lane = lax.broadcasted_iota(I32, (BLK, LANES), 1)
        lse_all = lse_ref[...]
        prow = pl.ds(pl.multiple_of(jnp.maximum(i - 1, 0) * BLK, BLK), BLK)
        crow = pl.ds(pl.multiple_of(i * BLK, BLK), BLK)
        dsink = jnp.zeros((1, LANES), F32)
        zero = jnp.zeros((BLK, LANES), BF16)
        H, T = range(SWA_HEADS), range(SWA_HEADS // 2)
        kks = [jnp.concatenate([kp_ref[:, g * LANES:(g + 1) * LANES], kc_ref[:, g * LANES:(g + 1) * LANES]], axis=0)
               for g in range(SWA_KV)]
        vvs = [jnp.concatenate([vp_ref[:, g * LANES:(g + 1) * LANES], vc_ref[:, g * LANES:(g + 1) * LANES]], axis=0)
               for g in range(SWA_KV)]
        q2s = [q_ref[:, t * LANES:(t + 1) * LANES] for t in T]
        do2s = [do_ref[:, t * LANES:(t + 1) * LANES] for t in T]
        prods = [do2s[t] * o_ref[:, t * LANES:(t + 1) * LANES] for t in T]
        mine = [first if h % 2 == 0 else ~first for h in H]
        qhs = [jnp.where(mine[h], q2s[h // 2], zero) for h in H]
        dohs = [jnp.where(mine[h], do2s[h // 2], 0.0).astype(BF16) for h in H]
        deltas = [jnp.sum(jnp.where(mine[h], prods[h // 2], 0.0), axis=1, keepdims=True) for h in H]
        lses = [jnp.sum(jnp.where(lane == h, lse_all, 0.0), axis=1, keepdims=True) for h in H]
        ss = [_dot(qhs[h], kks[h // group], NT) * scale - _swa_slope(h) * rel for h in H]
        dps = [_dot(dohs[h], vvs[h // group], NT) for h in H]
        ps = [jnp.where(valid, jnp.exp(ss[h] - lses[h]), 0.0) for h in H]
        dscs = [(ps[h] * (dps[h] - deltas[h]) * scale).astype(BF16) for h in H]
        dqs = [_dot(dscs[h], kks[h // group]) for h in H]
        dks = [_dot(dscs[h], qhs[h], TN) for h in H]
        dvs = [_dot(ps[h].astype(BF16), dohs[h], TN) for h in H]
        for h in H:
            psink = jnp.exp(sink_ref[h] - lses[h])
            dsink = dsink + jnp.where(lane1 == h, -jnp.sum(psink * deltas[h]), 0.0)
        for t in T:
            dq_ref[:, t * LANES:(t + 1) * LANES] = jnp.where(first, dqs[2 * t], dqs[2 * t + 1])
        for g in range(SWA_KV):
            cols = slice(g * LANES, (g + 1) * LANES)
            b = g * group
            dkk = (dks[b] + dks[b + 1]) + (dks[b + 2] + dks[b + 3])
            dvv = (dvs[b] + dvs[b + 1]) + (dvs[b + 2] + dvs[b + 3])
            dk_ref[prow, cols] += dkk[:BLK]
            dv_ref[prow, cols] += dvv[:BLK]
            dk_ref[crow, cols] += dkk[BLK:]
            dv_ref[crow, cols] += dvv[BLK:]
        ds_ref[...] += dsink

    prev = lambda i: (jnp.maximum(i - 1, 0), 0)
    cur = lambda i: (i, 0)
    kvw = SWA_KV * LANES
    whole = pl.BlockSpec((S, kvw), lambda i: (0, 0))
    return _pallas(
        body, name=name, grid=(nq,),
        in_specs=[pl.BlockSpec((BLK, 1024), cur), pl.BlockSpec((BLK, kvw), prev), pl.BlockSpec((BLK, kvw), cur),
                  pl.BlockSpec((BLK, kvw), prev), pl.BlockSpec((BLK, kvw), cur),
                  pl.BlockSpec(memory_space=pltpu.SMEM),
                  pl.BlockSpec((BLK, 1024), cur), pl.BlockSpec((BLK, LANES), cur), pl.BlockSpec((BLK, 1024), cur)],
        out_specs=[pl.BlockSpec((BLK, 1024), cur), whole, whole, pl.BlockSpec((1, LANES), lambda i: (0, 0))],
        out_shape=[jax.ShapeDtypeStruct((S, 1024), F32), jax.ShapeDtypeStruct((S, kvw), F32),
                   jax.ShapeDtypeStruct((S, kvw), F32), jax.ShapeDtypeStruct((1, LANES), F32)],
        compiler_params=_params(("arbitrary",), 4 * S * kvw * 4 + 24 * BLK * 1024 * 4),
    )(qn, kdup, kdup, vdup, vdup, sinks, o, lse, do)


def _in_bwd(x, h, dproj, dy, g, w_in, tag):
    dh = matmul(dproj, w_in, "nt", name=f"{tag}_dh")
    dw_in = matmul(h, dproj, "tn", name=f"{tag}_dwin")
    dx, dg = rmsnorm_bwd(x, g, dh, col_off=0, width=D_MODEL, out_dtype=F32, res=dy, name=f"{tag}_dnorm")
    return dx, dw_in, dg


def _out_bwd(dy, og, o, proj, w_out, gate_off, tag):
    dog = matmul(dy, w_out, "nt", name=f"{tag}_dog")
    dw_out = matmul(og, dy, "tn", name=f"{tag}_dwout")
    do, dgate = gate_bwd(dog, o, proj, gate_off=gate_off, name=f"{tag}_dgate")
    return do, dgate, dw_out


def sb_fwd(x, p, tag, ride=None):
    h = rmsnorm_fwd(x, p["norm"], col_off=0, width=D_MODEL, out_dtype=BF16, name=f"{tag}_norm")
    qkv = matmul(h, p["w_in"], "nn", out_dtype=BF16, b_cols=(0, 3 * SB_W), name=f"{tag}_proj")
    gate = matmul(h, p["w_in"], "nn", b_cols=(3 * SB_W, SB_W), name=f"{tag}_gproj")
    o = sb_attn_fwd(qkv, name=f"{tag}_attn", ride=ride)
    if ride is not None:
        o, gathered = o
    og = gate_fwd(o, gate, gate_off=0, name=f"{tag}_gate")
    y = matmul(og, p["w_out"], "nn", add=x, name=f"{tag}_out")
    saved = (x, h, gate, qkv, o, og)
    return (y, saved) if ride is None else (y, saved, gathered)


def sb_bwd(dy, saved, p, tag, ride=None):
    x, h, gate, qkv, o, og = saved
    do, dgate, dw_out = _out_bwd(dy, og, o, gate, p["w_out"], 0, tag)
    dq, dk, dv, *received = sb_attn_bwd(qkv, o, do, name=f"{tag}_dattn", ride=ride)
    dproj = jnp.concatenate([dq, dk.astype(BF16), dv.astype(BF16), dgate], axis=1)
    dx, dw_in, dg = _in_bwd(x, h, dproj, dy, p["norm"], p["w_in"], tag)
    grads = {"norm": dg[0], "w_in": dw_in, "w_out": dw_out}
    return (dx, grads) if ride is None else (dx, grads, received[0])


MLA_GATE, MLA_QLAT, MLA_KVLAT, MLA_KPE, MLA_IN = 0, 1024, 1280, 1408, 1536


def mla_fwd(x, p, tabs, tag, ride=None):
    cos, sgn = tabs
    h = rmsnorm_fwd(x, p["norm"], col_off=0, width=D_MODEL, out_dtype=BF16, name=f"{tag}_norm")
    proj = matmul(h, p["w_in"], "nn", name=f"{tag}_proj")
    ql = rmsnorm_fwd(proj, p["q_a_norm"], col_off=MLA_QLAT, width=256, out_dtype=BF16, name=f"{tag}_qanorm")
    kvl = rmsnorm_fwd(proj, p["kv_a_norm"], col_off=MLA_KVLAT, width=128, out_dtype=BF16, name=f"{tag}_kvanorm")
    qraw = matmul(ql, p["w_uq"], "nn", name=f"{tag}_uq")
    kv = matmul(kvl, p["w_ukv"], "nn", name=f"{tag}_ukv")
    qn, kn, vb = mla_prep_fwd(qraw, kv, proj, p["gq"], p["gk"], cos, sgn, kpe_off=MLA_KPE, name=f"{tag}_prep")
    o, lse, *gathered = flash_fwd(qn, kn, vb, name=f"{tag}_attn", ride=ride)
    og = gate_fwd(o, proj, gate_off=MLA_GATE, name=f"{tag}_gate")
    y = matmul(og, p["w_out"], "nn", add=x, name=f"{tag}_out")
    saved = (x, h, proj, ql, kvl, qraw, kv, qn, kn, vb, o, lse, og)
    return (y, saved) if ride is None else (y, saved, gathered[0])


def mla_bwd(dy, saved, p, tabs, tag):
    cos, sgn = tabs
    x, h, proj, ql, kvl, qraw, kv, qn, kn, vb, o, lse, og = saved
    do, dgate, dw_out = _out_bwd(dy, og, o, proj, p["w_out"], MLA_GATE, tag)
    dqn, dkn, dv = flash_bwd(qn, kn, vb, o, lse, do, name=f"{tag}_dattn")
    dqraw, dkv, dkpe, dgq, dgk = mla_prep_bwd(dqn, dkn, dv, qraw, kv, proj, p["gq"], p["gk"], cos, sgn,
                                              kpe_off=MLA_KPE, name=f"{tag}_dprep")
    dql = matmul(dqraw, p["w_uq"], "nt", name=f"{tag}_dql")
    dw_uq = matmul(ql, dqraw, "tn", name=f"{tag}_dwuq")
    dkvl = matmul(dkv, p["w_ukv"], "nt", name=f"{tag}_dkvl")
    dw_ukv = matmul(kvl, dkv, "tn", name=f"{tag}_dwukv")
    dqlat, dgqa = rmsnorm_bwd(proj, p["q_a_norm"], dql, col_off=MLA_QLAT, width=256, out_dtype=BF16, name=f"{tag}_dqanorm")
    dkvlat, dgkva = rmsnorm_bwd(proj, p["kv_a_norm"], dkvl, col_off=MLA_KVLAT, width=128, out_dtype=BF16,
                                name=f"{tag}_dkvanorm")
    dproj = jnp.concatenate([dgate, dqlat, dkvlat, dkpe], axis=1)
    dx, dw_in, dg = _in_bwd(x, h, dproj, dy, p["norm"], p["w_in"], tag)
    return dx, {"norm": dg[0], "w_in": dw_in, "q_a_norm": dgqa[0], "w_uq": dw_uq, "kv_a_norm": dgkva[0],
                "w_ukv": dw_ukv, "gq": dgq[0], "gk": dgk[0], "w_out": dw_out}


SWA_Q, SWA_GATE, SWA_K, SWA_V = 0, 1024, 2048, 2304


def swa_fwd(x, p, tag):
    h = rmsnorm_fwd(x, p["norm"], col_off=0, width=D_MODEL, out_dtype=BF16, name=f"{tag}_norm")
    proj = matmul(h, p["w_in"], "nn", name=f"{tag}_proj")
    qn = headnorm_fwd(proj, p["q_head_norm"], col_off=SWA_Q, width=1024, name=f"{tag}_qnorm")
    kdup, vdup = swa_kv_prep(proj, p["k_head_norm"], k_off=SWA_K, v_off=SWA_V, name=f"{tag}_kvprep")
    o, lse = swa_attn_fwd(qn, kdup, vdup, p["sinks"], name=f"{tag}_attn")
    og = gate_fwd(o, proj, gate_off=SWA_GATE, name=f"{tag}_gate")
    y = matmul(og, p["w_out"], "nn", add=x, name=f"{tag}_out")
    return y, (x, h, proj, qn, kdup, vdup, o, lse, og)


def swa_bwd(dy, saved, p, tag):
    x, h, proj, qn, kdup, vdup, o, lse, og = saved
    do, dgate, dw_out = _out_bwd(dy, og, o, proj, p["w_out"], SWA_GATE, tag)
    dqn, dkdup, dvdup, dsinks = swa_attn_bwd(qn, kdup, vdup, p["sinks"], o, lse, do, name=f"{tag}_dattn")
    dq, dgq = headnorm_bwd(proj, p["q_head_norm"], dqn, col_off=SWA_Q, width=1024, name=f"{tag}_dqnorm")
    dk, dv, dgk = swa_kv_prep_bwd(dkdup, dvdup, proj, p["k_head_norm"], k_off=SWA_K, name=f"{tag}_dkvprep")
    dproj = jnp.concatenate([dq, dgate, dk, dv], axis=1)
    dx, dw_in, dg = _in_bwd(x, h, dproj, dy, p["norm"], p["w_in"], tag)
    return dx, {"norm": dg[0], "w_in": dw_in, "q_head_norm": dgq[0, :HEAD64] + dgq[0, HEAD64:],
                "k_head_norm": dgk[0, :HEAD64] + dgk[0, HEAD64:], "sinks": dsinks[0, :SWA_HEADS], "w_out": dw_out}


def prepare_layer(w, i):
    if i in (0, 3):
        return {"norm": w[f"l{i}_norm"], "w_in": w[f"l{i}_w_in"], "w_out": w[f"l{i}_w_out"]}
    if i == 1:
        l1_in = w["l1_w_in"]
        pad64 = lambda v: jnp.pad(v, (0, MLA_PAD - MLA_QK)).reshape(1, MLA_PAD)
        return {"norm": w["l1_norm"],
                "w_in": jnp.concatenate([l1_in[:, 448:], l1_in[:, :448], jnp.zeros((D_MODEL, 64), l1_in.dtype)], axis=1),
                "q_a_norm": w["l1_q_a_norm"], "kv_a_norm": w["l1_kv_a_norm"],
                "w_uq": jnp.pad(w["l1_w_uq"].reshape(256, MLA_HEADS, MLA_QK), ((0, 0), (0, 0), (0, MLA_PAD - MLA_QK))
                                ).reshape(256, MLA_HEADS * MLA_PAD),
                "w_ukv": w["l1_w_ukv"], "gq": pad64(w["l1_q_head_norm"]), "gk": pad64(w["l1_k_head_norm"]),
                "w_out": w["l1_w_out"]}
    l2_in = w["l2_w_in"]
    return {"norm": w["l2_norm"],
            "w_in": jnp.concatenate([l2_in[:, :1024], l2_in[:, 1536:], l2_in[:, 1024:1536]], axis=1),
            "q_head_norm": w["l2_q_head_norm"], "k_head_norm": w["l2_k_head_norm"], "sinks": w["l2_sinks"],
            "w_out": w["l2_w_out"]}


def unprepare_grads(gs):
    g0, g1, g2, g3 = gs
    d1, d2 = g1["w_in"], g2["w_in"]
    first = {} if g0 is None else {"l0_norm": g0["norm"], "l0_w_in": g0["w_in"], "l0_w_out": g0["w_out"]}
    return {
        **first,
        "l1_norm": g1["norm"], "l1_w_in": jnp.concatenate([d1[:, 1024:1472], d1[:, :1024]], axis=1),
        "l1_q_a_norm": g1["q_a_norm"],
        "l1_w_uq": g1["w_uq"].reshape(256, MLA_HEADS, MLA_PAD)[:, :, :MLA_QK].reshape(256, MLA_HEADS * MLA_QK),
        "l1_kv_a_norm": g1["kv_a_norm"], "l1_w_ukv": g1["w_ukv"],
        "l1_q_head_norm": g1["gq"][:MLA_QK], "l1_k_head_norm": g1["gk"][:MLA_QK], "l1_w_out": g1["w_out"],
        "l2_norm": g2["norm"], "l2_w_in": jnp.concatenate([d2[:, :1024], d2[:, 2048:], d2[:, 1024:2048]], axis=1),
        "l2_q_head_norm": g2["q_head_norm"], "l2_k_head_norm": g2["k_head_norm"], "l2_sinks": g2["sinks"],
        "l2_w_out": g2["w_out"],
        "l3_norm": g3["norm"], "l3_w_in": g3["w_in"], "l3_w_out": g3["w_out"],
    }


def local_step(x, target, w, late=None, early=None):
    tabs = _rope_tables(x.shape[0])
    p0 = prepare_layer(w, 0)
    if late is None:
        y0, s0 = sb_fwd(x, p0, "l0")
        p1 = prepare_layer(w, 1)
        y1, s1 = mla_fwd(y0, p1, tabs, "l1")
    else:
        y0, s0, gathered = sb_fwd(x, p0, "l0", ride=late[0][0])
        w = {**w, **late[0][1](gathered)}
        p1 = prepare_layer(w, 1)
        y1, s1, gathered = mla_fwd(y0, p1, tabs, "l1", ride=late[1][0])
        w = {**w, **late[1][1](gathered)}
    ps = [p0, p1, prepare_layer(w, 2), prepare_layer(w, 3)]
    y2, s2 = swa_fwd(y1, ps[2], "l2")
    y3, s3 = sb_fwd(y2, ps[3], "l3")
    dy, loss = loss_head(y3, target, name="loss")
    d3, g3 = sb_bwd(dy, s3, ps[3], "l3")
    d2, g2 = swa_bwd(d3, s2, ps[2], "l2")
    d1, g1 = mla_bwd(d2, s1, ps[1], tabs, "l1")
    if early is None:
        d0, g0 = sb_bwd(d1, s0, ps[0], "l0")
        return loss, d0, unprepare_grads([g0, g1, g2, g3])
    d0, g0, received = sb_bwd(d1, s0, ps[0], "l0", ride=early(unprepare_grads([None, g1, g2, g3])))
    return loss, d0, unprepare_grads([g0, g1, g2, g3]), received


MATS = (("l0_w_in", "col", 1024, 4096), ("l0_w_out", "row", 1024, 1024), ("l1_w_in", "col", 1024, 1472),
        ("l1_w_uq", "col", 256, 1536), ("l1_w_ukv", "col", 128, 2048), ("l1_w_out", "row", 1024, 1024),
        ("l2_w_in", "col", 1024, 2560), ("l2_w_out", "row", 1024, 1024), ("l3_w_in", "col", 1024, 4096),
        ("l3_w_out", "row", 1024, 1024))
PACK_GROUPS = ((MATS[0:2], 1280), (MATS[2:6], 800), (MATS[6:10], 2272))
N_CHIPS = 4
PACK_W = 1024
HALF_ROWS = 2176
PACK_ROWS = 2 * HALF_ROWS
VECS = (("l0_norm", 0, 0, 1024), ("l1_norm", 1, 0, 1024), ("l2_norm", 2, 0, 1024), ("l3_norm", 3, 0, 1024),
        ("l1_q_a_norm", 4, 0, 256), ("l1_kv_a_norm", 4, 256, 128), ("l1_q_head_norm", 4, 384, 192),
        ("l1_k_head_norm", 4, 576, 192), ("l2_q_head_norm", 4, 768, 64), ("l2_k_head_norm", 4, 832, 64),
        ("l2_sinks", 4, 896, 16))
LOSS_SLOT = (4, 912)
VEC_ROWS = 8


def _shard_rows(k, n):
    return k * n // N_CHIPS // PACK_W


def _group_start(gi):
    return sum(rows for _, rows in PACK_GROUPS[:gi])


def pack_shards(shards):
    parts = []
    for mats, rows in PACK_GROUPS:
        group = [shards[name].reshape(-1, PACK_W) for name, _, _, _ in mats]
        used = sum(p.shape[0] for p in group)
        parts += group + [jnp.zeros((rows - used, PACK_W), group[0].dtype)]
    return jnp.concatenate(parts, axis=0)


def unpack_shards(flat):
    out = {}
    for gi, (mats, _) in enumerate(PACK_GROUPS):
        r0 = _group_start(gi)
        for name, kind, k, n in mats:
            rows = _shard_rows(k, n)
            shape = (k, n // N_CHIPS) if kind == "col" else (k // N_CHIPS, n)
            out[name] = flat[r0:r0 + rows].reshape(shape)
            r0 += rows
    return out


def pack_full(full, groups=(0, 1, 2)):
    parts = []
    for mats, rows in [PACK_GROUPS[gi] for gi in groups]:
        group = []
        for name, kind, k, n in mats:
            m = full[name]
            if kind == "col":
                m = m.reshape(k, N_CHIPS, n // N_CHIPS).transpose(1, 0, 2)
            group.append(m.reshape(N_CHIPS, -1, PACK_W))
        used = sum(p.shape[1] for p in group)
        parts += group + [jnp.zeros((N_CHIPS, rows - used, PACK_W), group[0].dtype)]
    return jnp.concatenate(parts, axis=1)


def unpack_full(stacked, gi):
    out, r0 = {}, 0
    for name, kind, k, n in PACK_GROUPS[gi][0]:
        rows = _shard_rows(k, n)
        seg = stacked[:, r0:r0 + rows]
        if kind == "col":
            out[name] = seg.reshape(N_CHIPS, k, n // N_CHIPS).transpose(1, 0, 2).reshape(k, n)
        else:
            out[name] = seg.reshape(k, n)
        r0 += rows
    return out


def pack_vecs(vecs, loss=None):
    rows = []
    for r in range(VEC_ROWS):
        items = [(off, vecs[name]) for name, rr, off, _ in VECS if rr == r]
        if loss is not None and r == LOSS_SLOT[0]:
            items.append((LOSS_SLOT[1], loss.reshape(1)))
        pos, parts = 0, []
        for off, v in sorted(items, key=lambda t: t[0]):
            assert off == pos
            parts.append(v.astype(F32))
            pos += v.shape[0]
        parts.append(jnp.zeros((PACK_W - pos,), F32))
        rows.append(jnp.concatenate(parts))
    return jnp.stack(rows)


def unpack_vecs(block):
    return {name: block[r, off:off + n] for name, r, off, n in VECS}


def _me():
    return lax.axis_index("x"), lax.axis_index("y"), lax.axis_index("c")


OTHER_CHIPS = ((1, 0), (0, 1), (1, 1))


def _remote(src, dst, send_sem, recv_sem, to):
    return pltpu.make_async_remote_copy(src_ref=src, dst_ref=dst, send_sem=send_sem, recv_sem=recv_sem,
                                        device_id=to, device_id_type=MESH)


def gather_weights(block):
    def body(in_ref, out_ref, send_sems, recv_sems):
        send, forward, finish = _gather_steps(in_ref, out_ref, send_sems, recv_sems)
        send()
        forward()
        finish()

    hbm = pl.BlockSpec(memory_space=pltpu.HBM)
    others = pl.pallas_call(
        body, name="gather_weights",
        out_shape=jax.ShapeDtypeStruct((N_CHIPS,) + block.shape, block.dtype),
        in_specs=[hbm], out_specs=hbm,
        scratch_shapes=list(GATHER_SEMS),
    )(block)
    return place_own_block(others, block)


GATHER_SEMS = (pltpu.SemaphoreType.DMA((6,)), pltpu.SemaphoreType.DMA((6,)))


def place_own_block(others, block):
    return lax.dynamic_update_slice(others, block[None], (2 * lax.axis_index("x") + lax.axis_index("y"), 0, 0))


def _gather_steps(in_ref, out_ref, send_sems, recv_sems):
    hr = in_ref.shape[0] // 2
    x, y, c = _me()
    sibling = (x, y, 1 - c)
    chips = [(x ^ dx, y ^ dy) for dx, dy in OTHER_CHIPS]

    def half(px, py, pc):
        return out_ref.at[2 * px + py, pl.ds(pc * hr, hr), :]

    def firsts():
        return [_remote(in_ref.at[pl.ds(c * hr, hr), :], half(x, y, c), send_sems.at[j], recv_sems.at[j], (*chip, c))
                for j, chip in enumerate(chips)]

    def passes():
        return [_remote(half(*chip, c), half(*chip, c), send_sems.at[3 + j], recv_sems.at[3 + j], sibling)
                for j, chip in enumerate(chips)]

    def send():
        for cp in firsts():
            cp.start()

    def forward():
        for j, (chip, cp) in enumerate(zip(chips, passes())):
            _remote(half(*chip, c), half(*chip, c), send_sems.at[j], recv_sems.at[j], (*chip, c)).wait_recv()
            cp.start()

    def finish():
        for j, chip in enumerate(chips):
            _remote(half(*chip, 1 - c), half(*chip, 1 - c), send_sems.at[3 + j], recv_sems.at[3 + j], sibling).wait_recv()
        for cp in firsts() + passes():
            cp.wait_send()

    return send, forward, finish


def pair_exchange(g, tag):
    hr = g.shape[1] // 2

    def body(g_ref, out_ref, send_sems, recv_sems):
        x, y, c = _me()
        sibling = (x, y, 1 - c)
        copies = [_remote(g_ref.at[k, pl.ds((1 - c) * hr, hr), :], out_ref.at[k], send_sems.at[k],
                          recv_sems.at[k], sibling) for k in range(N_CHIPS)]
        for cp in copies:
            cp.start()
        for cp in copies:
            cp.wait_recv()
        for cp in copies:
            cp.wait_send()

    hbm = pl.BlockSpec(memory_space=pltpu.HBM)
    return pl.pallas_call(
        body, name=f"pair_exchange_{tag}",
        out_shape=jax.ShapeDtypeStruct((N_CHIPS, hr, PACK_W), g.dtype),
        in_specs=[hbm], out_specs=hbm,
        scratch_shapes=[pltpu.SemaphoreType.DMA((N_CHIPS,)), pltpu.SemaphoreType.DMA((N_CHIPS,))],
    )(g)


CHIP_SEMS = (pltpu.SemaphoreType.DMA((3,)), pltpu.SemaphoreType.DMA((3,)))


def _chip_exchange_steps(p_ref, out_ref, send_sems, recv_sems):
    x, y, c = _me()

    def copies():
        return [_remote(p_ref.at[2 * (x ^ dx) + (y ^ dy)], out_ref.at[j], send_sems.at[j], recv_sems.at[j],
                        (x ^ dx, y ^ dy, c)) for j, (dx, dy) in enumerate(OTHER_CHIPS)]

    def send():
        for cp in copies():
            cp.start()

    def finish():
        for cp in copies():
            cp.wait_recv()
        for cp in copies():
            cp.wait_send()

    return send, finish


def chip_exchange(part, tag):
    def body(p_ref, out_ref, send_sems, recv_sems):
        send, finish = _chip_exchange_steps(p_ref, out_ref, send_sems, recv_sems)
        send()
        finish()

    hbm = pl.BlockSpec(memory_space=pltpu.HBM)
    return pl.pallas_call(
        body, name=f"chip_exchange_{tag}",
        out_shape=jax.ShapeDtypeStruct((len(OTHER_CHIPS),) + part.shape[1:], part.dtype),
        in_specs=[hbm], out_specs=hbm,
        scratch_shapes=list(CHIP_SEMS),
    )(part)


def join_halves(half, tag):
    hr = half.shape[0]

    def body(h_ref, out_ref, send_sem, recv_sem):
        x, y, c = _me()
        cp = _remote(h_ref, out_ref.at[pl.ds(c * hr, hr), :], send_sem, recv_sem, (x, y, 1 - c))
        cp.start()
        _remote(h_ref, out_ref.at[pl.ds((1 - c) * hr, hr), :], send_sem, recv_sem, (x, y, 1 - c)).wait_recv()
        cp.wait_send()

    hbm = pl.BlockSpec(memory_space=pltpu.HBM)
    other = pl.pallas_call(
        body, name=f"join_halves_{tag}",
        out_shape=jax.ShapeDtypeStruct((2 * hr, PACK_W), half.dtype),
        in_specs=[hbm], out_specs=hbm,
        scratch_shapes=[pltpu.SemaphoreType.DMA, pltpu.SemaphoreType.DMA],
    )(half)
    return lax.dynamic_update_slice(other, half, (lax.axis_index("c") * hr, 0))


def sum_over_devices(block):
    def body(in_ref, out_ref, all_ref, send_sems, recv_sems):
        x, y, c = _me()
        me = 4 * x + 2 * y + c
        all_ref[me] = in_ref[...]
        copies = []
        for r in range(1, 8):
            to = (x ^ (r >> 2), y ^ ((r >> 1) & 1), c ^ (r & 1))
            copies.append(_remote(in_ref, all_ref.at[me], send_sems.at[r - 1], recv_sems.at[r - 1], to))
        for cp in copies:
            cp.start()
        for r in range(1, 8):
            frm = (x ^ (r >> 2), y ^ ((r >> 1) & 1), c ^ (r & 1))
            _remote(in_ref, all_ref.at[4 * frm[0] + 2 * frm[1] + frm[2]], send_sems.at[r - 1], recv_sems.at[r - 1],
                    frm).wait_recv()
        for cp in copies:
            cp.wait_send()
        acc = all_ref[0]
        for d in range(1, 8):
            acc = acc + all_ref[d]
        out_ref[...] = acc

    vmem = pl.BlockSpec(memory_space=pltpu.VMEM)
    return pl.pallas_call(
        body, name="sum_over_devices",
        out_shape=jax.ShapeDtypeStruct(block.shape, F32),
        in_specs=[vmem], out_specs=vmem,
        scratch_shapes=[pltpu.VMEM((8,) + block.shape, F32), pltpu.SemaphoreType.DMA((7,)), pltpu.SemaphoreType.DMA((7,))],
    )(block)


def pair_sum(g, got, core, tag):
    hr = got.shape[1]
    rows = _tile(hr, 512)
    steps = hr // rows

    def body(s_ref, g_ref, r_ref, o_ref, ob_ref):
        t = g_ref[...] + r_ref[...]
        o_ref[...] = t
        ob_ref[...] = t.astype(BF16)

    blk = pl.BlockSpec((1, rows, PACK_W), lambda k, i, s: (k, i, 0))
    return pl.pallas_call(
        body, name=f"pair_sum_{tag}",
        grid_spec=pltpu.PrefetchScalarGridSpec(
            num_scalar_prefetch=1, grid=(N_CHIPS, steps),
            in_specs=[pl.BlockSpec((1, rows, PACK_W), lambda k, i, s: (k, s[0] * steps + i, 0)), blk],
            out_specs=[blk, blk]),
        out_shape=[jax.ShapeDtypeStruct((N_CHIPS, hr, PACK_W), F32), jax.ShapeDtypeStruct((N_CHIPS, hr, PACK_W), BF16)],
        compiler_params=_params(("parallel", "parallel"), 8 * rows * PACK_W * 4),
    )(core, g, got)


def chip_sum(part, got, chip, tag):
    hr = part.shape[1]
    rows = _tile(hr, 512)

    def body(s_ref, p_ref, a_ref, b_ref, c_ref, o_ref):
        o_ref[...] = ((p_ref[0] + a_ref[0].astype(F32)) + b_ref[0].astype(F32)) + c_ref[0].astype(F32)

    def got_spec(j):
        return pl.BlockSpec((1, rows, PACK_W), lambda i, s: (j, i, 0))

    return pl.pallas_call(
        body, name=f"chip_sum_{tag}",
        grid_spec=pltpu.PrefetchScalarGridSpec(
            num_scalar_prefetch=1, grid=(hr // rows,),
            in_specs=[pl.BlockSpec((1, rows, PACK_W), lambda i, s: (s[0], i, 0)), got_spec(0), got_spec(1), got_spec(2)],
            out_specs=pl.BlockSpec((rows, PACK_W), lambda i, s: (i, 0))),
        out_shape=jax.ShapeDtypeStruct((hr, PACK_W), F32),
        compiler_params=_params(("parallel",), 10 * rows * PACK_W * 4),
    )(chip, part, got, got, got)


def reduce_scatter_begin(g, tag):
    c = lax.axis_index("c")
    return pair_sum(g, pair_exchange(g, tag), jnp.reshape(c, (1,)).astype(I32), tag)


def reduce_scatter_end(part, got, tag):
    chip = 2 * lax.axis_index("x") + lax.axis_index("y")
    return join_halves(chip_sum(part, got, jnp.reshape(chip, (1,)).astype(I32), tag), tag)


def reduce_scatter(g, tag):
    part, part_bf16 = reduce_scatter_begin(g, tag)
    return reduce_scatter_end(part, chip_exchange(part_bf16, tag), tag)


def adamw(w, g, m, v, *, name):
    rows, cols = w.shape
    tm = _tile(rows, 256)
    c1 = 1.0 - ADAM_B1 ** ADAM_STEP
    c2 = 1.0 - ADAM_B2 ** ADAM_STEP

    def body(w_ref, g_ref, m_ref, v_ref, d_ref, mo_ref, vo_ref):
        gv = g_ref[...]
        mn = ADAM_B1 * m_ref[...] + (1.0 - ADAM_B1) * gv
        vn = ADAM_B2 * v_ref[...] + (1.0 - ADAM_B2) * (gv * gv)
        d_ref[...] = -ADAM_LR * ((mn / c1) / (jnp.sqrt(vn / c2) + ADAM_EPS) + ADAM_WD * w_ref[...])
        mo_ref[...] = mn
        vo_ref[...] = vn

    blk = pl.BlockSpec((tm, cols), lambda i: (i, 0))
    shape = jax.ShapeDtypeStruct(w.shape, F32)
    return _pallas(
        body, name=name, grid=(rows // tm,),
        in_specs=[blk] * 4, out_specs=[blk] * 3, out_shape=[shape] * 3,
        compiler_params=_params(("parallel",), 16 * tm * cols * 4),
    )(w, g, m, v)


WEIGHTS = ("l0_norm", "l0_w_in", "l0_w_out", "l1_norm", "l1_w_in", "l1_q_a_norm", "l1_w_uq", "l1_kv_a_norm", "l1_w_ukv",
           "l1_q_head_norm", "l1_k_head_norm", "l1_w_out", "l2_norm", "l2_w_in", "l2_q_head_norm", "l2_k_head_norm",
           "l2_sinks", "l2_w_out", "l3_norm", "l3_w_in", "l3_w_out")


def kernel(x, l0_norm, l0_w_in, l0_w_out, l1_norm, l1_w_in, l1_q_a_norm, l1_w_uq, l1_kv_a_norm, l1_w_ukv, l1_q_head_norm, l1_k_head_norm, l1_w_out, l2_norm, l2_w_in, l2_q_head_norm, l2_k_head_norm, l2_sinks, l2_w_out, l3_norm, l3_w_in, l3_w_out, loss_target, m_l0_norm, m_l0_w_in, m_l0_w_out, m_l1_norm, m_l1_w_in, m_l1_q_a_norm, m_l1_w_uq, m_l1_kv_a_norm, m_l1_w_ukv, m_l1_q_head_norm, m_l1_k_head_norm, m_l1_w_out, m_l2_norm, m_l2_w_in, m_l2_q_head_norm, m_l2_k_head_norm, m_l2_sinks, m_l2_w_out, m_l3_norm, m_l3_w_in, m_l3_w_out, v_l0_norm, v_l0_w_in, v_l0_w_out, v_l1_norm, v_l1_w_in, v_l1_q_a_norm, v_l1_w_uq, v_l1_kv_a_norm, v_l1_w_ukv, v_l1_q_head_norm, v_l1_k_head_norm, v_l1_w_out, v_l2_norm, v_l2_w_in, v_l2_q_head_norm, v_l2_k_head_norm, v_l2_sinks, v_l2_w_out, v_l3_norm, v_l3_w_in, v_l3_w_out):
    given = dict(locals())
    w = {n: given[n] for n in WEIGHTS}
    m = {n: given["m_" + n] for n in WEIGHTS}
    v = {n: given["v_" + n] for n in WEIGHTS}
    mat_names = [t[0] for t in MATS]
    vec_names = [t[0] for t in VECS]

    packed = pack_shards({n: w[n] for n in mat_names}).astype(BF16)
    part = lambda gi: packed[_group_start(gi):_group_start(gi + 1)]
    full = unpack_full(gather_weights(part(0)), 0)
    full.update({n: w[n] for n in vec_names})
    late = [(part(gi), lambda gathered, gi=gi: unpack_full(gathered, gi)) for gi in (1, 2)]
    pair_sums = []

    def early(named):
        part, part_bf16 = reduce_scatter_begin(pack_full(named, groups=(1, 2)), "late")
        pair_sums.append(part)
        return part_bf16

    loss_tile, grad_x, grads, received = local_step(x[0], loss_target[0], full, late, early)
    block = jnp.concatenate([reduce_scatter(pack_full(grads, groups=(0,)), "first"),
                             reduce_scatter_end(pair_sums[0], received, "late")], axis=0)
    g = unpack_shards(block)
    vec_sum = sum_over_devices(pack_vecs({n: grads[n] for n in vec_names}, loss=loss_tile[0, 0]))
    loss = vec_sum[LOSS_SLOT[0], LOSS_SLOT[1]]

    delta, new_m, new_v = {}, {}, {}
    for n in mat_names:
        delta[n], new_m[n], new_v[n] = adamw(w[n], g[n], m[n], v[n], name=f"adamw_{n}")
    dv, mv, vv = adamw(pack_vecs(w), vec_sum, pack_vecs(m), pack_vecs(v), name="adamw_vecs")
    g.update(unpack_vecs(vec_sum))
    delta.update(unpack_vecs(dv))
    new_m.update(unpack_vecs(mv))
    new_v.update(unpack_vecs(vv))
    return (loss, grad_x[None], *[g[n] for n in WEIGHTS], *[delta[n] for n in WEIGHTS],
            *[new_m[n] for n in WEIGHTS], *[new_v[n] for n in WEIGHTS])
```

```python
import math

import jax
import jax.numpy as jnp
from jax import lax
from jax.experimental import pallas as pl
from jax.experimental.pallas import tpu as pltpu

F32 = jnp.float32
BF16 = jnp.bfloat16
I32 = jnp.int32
MESH = pl.DeviceIdType.MESH

NORM_EPS = 1e-6
D_MODEL = 1024
HEAD64 = 64
LANES = 128
BLK = 128
MLA_HEADS = 8
MLA_QK = 192
MLA_PAD = 256
ROPE_THETA = 10000.0
SWA_HEADS = 16
SWA_KV = 4
VMEM_CAP = 56 * 1024 * 1024
MATMUL_TILE_BYTES = 8 * 1024 * 1024

ADAM_LR, ADAM_B1, ADAM_B2, ADAM_EPS, ADAM_WD, ADAM_STEP = 0.001, 0.9, 0.999, 1e-08, 0.01, 10

NT = (((1,), (1,)), ((), ()))
NN = (((1,), (0,)), ((), ()))
TN = (((0,), (0,)), ((), ()))


def _dot(a, b, dims=NN):
    return lax.dot_general(a, b, dims, preferred_element_type=F32)


def _tile(n, pref):
    for t in (pref, 512, 256, 128):
        if t <= pref and n % t == 0:
            return t
    return n


def _params(sem, vmem_bytes):
    limit = int(min(max(2 * vmem_bytes, 24 * 1024 * 1024), VMEM_CAP))
    return pltpu.CompilerParams(dimension_semantics=sem, vmem_limit_bytes=limit)


def _in_hbm(s):
    return pltpu.HBM(s.shape, s.dtype) if len(s.shape) >= 2 else s


def _pallas(*args, out_shape, **kwargs):
    out_shape = [_in_hbm(s) for s in out_shape] if isinstance(out_shape, (list, tuple)) else _in_hbm(out_shape)
    call = pl.pallas_call(*args, out_shape=out_shape, **kwargs)

    def run(*operands):
        return call(*[pltpu.with_memory_space_constraint(a, pltpu.HBM) if a.ndim >= 2 else a for a in operands])

    return run


def _split(v):
    hi = v.astype(BF16)
    return hi, (v - hi.astype(F32)).astype(BF16)


def _dot2(v, m):
    hi, lo = _split(v)
    return _dot(hi, m) + _dot(lo, m)


def _dot_split(v, m2):
    return _dot(jnp.concatenate(_split(v), axis=1), m2)


def _first_half(shape):
    return lax.broadcasted_iota(I32, shape, 1) < HEAD64


def _sigmoid(g):
    return 1.0 / (1.0 + jnp.exp(-g))


def matmul(a, b, mode, *, name, out_dtype=F32, add=None, b_cols=None, tm=512, tn=1024):
    if mode == "nn":
        (M, K), (K2, N) = a.shape, b.shape
        if b_cols is not None:
            N = b_cols[1]
    elif mode == "nt":
        (M, K), (N, K2) = a.shape, b.shape
    else:
        (K, M), (K2, N) = a.shape, b.shape
    assert K == K2, (a.shape, b.shape, mode)
    tm, tn = _tile(M, tm), _tile(N, tn)
    while K * tm * a.dtype.itemsize > MATMUL_TILE_BYTES:
        tm //= 2
    while K * tn * b.dtype.itemsize > MATMUL_TILE_BYTES:
        tn //= 2
    dims = {"nn": NN, "nt": NT, "tn": TN}[mode]
    n_in = 2 if add is None else 3

    def body(*refs):
        a_ref, b_ref = refs[:2]
        r = _dot(a_ref[...].astype(BF16), b_ref[...].astype(BF16), dims)
        if add is not None:
            r = r + refs[2][...]
        for o_ref in refs[n_in:]:
            o_ref[...] = r.astype(o_ref.dtype)

    a_spec = pl.BlockSpec((K, tm), lambda i, j: (0, i)) if mode == "tn" else pl.BlockSpec((tm, K), lambda i, j: (i, 0))
    jb = 0 if b_cols is None else b_cols[0] // tn
    assert b_cols is None or (mode == "nn" and b_cols[0] % tn == 0)
    b_spec = pl.BlockSpec((tn, K), lambda i, j: (j, 0)) if mode == "nt" else pl.BlockSpec((K, tn), lambda i, j: (0, jb + j))
    o_spec = pl.BlockSpec((tm, tn), lambda i, j: (i, j))
    in_specs, args = [a_spec, b_spec], [a, b]
    if add is not None:
        in_specs.append(o_spec)
        args.append(add)
    vm = 2 * (tm * K * a.dtype.itemsize + K * tn * b.dtype.itemsize) + 5 * tm * tn * 4
    out_dtypes = list(out_dtype) if isinstance(out_dtype, (tuple, list)) else [out_dtype]
    res = _pallas(
        body, name=name, grid=(M // tm, N // tn),
        in_specs=in_specs, out_specs=[o_spec] * len(out_dtypes),
        out_shape=[jax.ShapeDtypeStruct((M, N), d) for d in out_dtypes],
        compiler_params=_params(("parallel", "parallel"), vm),
    )(*args)
    return res[0] if len(out_dtypes) == 1 else res


def rmsnorm_fwd(x, g, *, col_off, width, out_dtype, name, tm=1024):
    S = x.shape[0]
    assert col_off % width == 0
    cb = col_off // width
    tm = _tile(S, tm)

    def body(x_ref, g_ref, o_ref):
        v = x_ref[...]
        r = lax.rsqrt(jnp.mean(v * v, axis=1, keepdims=True) + NORM_EPS)
        o_ref[...] = (v * r * g_ref[...]).astype(out_dtype)

    return _pallas(
        body, name=name, grid=(S // tm,),
        in_specs=[pl.BlockSpec((tm, width), lambda i: (i, cb)), pl.BlockSpec((1, width), lambda i: (0, 0))],
        out_specs=pl.BlockSpec((tm, width), lambda i: (i, 0)),
        out_shape=jax.ShapeDtypeStruct((S, width), out_dtype),
        compiler_params=_params(("parallel",), 4 * tm * width * 4),
    )(x, g.reshape(1, width))


def rmsnorm_bwd(x, g, dh, *, col_off, width, out_dtype, name, res=None, tm=512):
    S = x.shape[0]
    cb = col_off // width
    tm = _tile(S, tm)

    def body(*refs):
        if res is None:
            x_ref, g_ref, dh_ref, dx_ref, dg_ref = refs
        else:
            x_ref, g_ref, dh_ref, res_ref, dx_ref, dg_ref = refs

        @pl.when(pl.program_id(0) == 0)
        def _():
            dg_ref[...] = jnp.zeros_like(dg_ref)

        v = x_ref[...]
        dhv = dh_ref[...].astype(F32)
        r = lax.rsqrt(jnp.mean(v * v, axis=1, keepdims=True) + NORM_EPS)
        y = v * r
        dy = dhv * g_ref[...]
        dx = r * (dy - y * jnp.mean(dy * y, axis=1, keepdims=True))
        if res is not None:
            dx = dx + res_ref[...]
        dx_ref[...] = dx.astype(out_dtype)
        dg_ref[...] += jnp.sum(dhv * y, axis=0, keepdims=True)

    row = pl.BlockSpec((tm, width), lambda i: (i, 0))
    in_specs = [pl.BlockSpec((tm, width), lambda i: (i, cb)), pl.BlockSpec((1, width), lambda i: (0, 0)), row]
    args = [x, g.reshape(1, width), dh]
    if res is not None:
        in_specs.append(row)
        args.append(res)
    return _pallas(
        body, name=name, grid=(S // tm,),
        in_specs=in_specs,
        out_specs=[row, pl.BlockSpec((1, width), lambda i: (0, 0))],
        out_shape=[jax.ShapeDtypeStruct((S, width), out_dtype), jax.ShapeDtypeStruct((1, width), F32)],
        compiler_params=_params(("arbitrary",), 8 * tm * width * 4),
    )(*args)


def _group_ones():
    r = lax.broadcasted_iota(I32, (LANES, LANES), 0) // HEAD64
    c = lax.broadcasted_iota(I32, (LANES, LANES), 1) // HEAD64
    return (r == c).astype(BF16)


def _segmean64(v):
    return _dot2(v, _group_ones()) * (1.0 / HEAD64)


def headnorm_fwd(x, g64, *, col_off, width, name, tm=2048):
    S = x.shape[0]
    cb = col_off // LANES
    tm = _tile(S, tm)

    def body(x_ref, g_ref, o_ref):
        v = x_ref[...]
        r = lax.rsqrt(_segmean64(v * v) + NORM_EPS)
        o_ref[...] = (v * r * g_ref[...]).astype(BF16)

    return _pallas(
        body, name=name, grid=(S // tm, width // LANES),
        in_specs=[pl.BlockSpec((tm, LANES), lambda i, j: (i, cb + j)), pl.BlockSpec((1, LANES), lambda i, j: (0, 0))],
        out_specs=pl.BlockSpec((tm, LANES), lambda i, j: (i, j)),
        out_shape=jax.ShapeDtypeStruct((S, width), BF16),
        compiler_params=_params(("parallel", "parallel"), 8 * tm * LANES * 4),
    )(x, jnp.tile(g64, 2).reshape(1, LANES))


def headnorm_bwd(x, g64, dh, *, col_off, width, name, tm=2048):
    S = x.shape[0]
    cb = col_off // LANES
    tm = _tile(S, tm)

    def body(x_ref, g_ref, dh_ref, dx_ref, dg_ref):
        @pl.when((pl.program_id(0) == 0) & (pl.program_id(1) == 0))
        def _():
            dg_ref[...] = jnp.zeros_like(dg_ref)

        v = x_ref[...]
        dhv = dh_ref[...]
        r = lax.rsqrt(_segmean64(v * v) + NORM_EPS)
        y = v * r
        dy = dhv * g_ref[...]
        dx_ref[...] = (r * (dy - y * _segmean64(dy * y))).astype(BF16)
        dg_ref[...] += jnp.sum(dhv * y, axis=0, keepdims=True)

    return _pallas(
        body, name=name, grid=(width // LANES, S // tm),
        in_specs=[pl.BlockSpec((tm, LANES), lambda j, i: (i, cb + j)), pl.BlockSpec((1, LANES), lambda j, i: (0, 0)),
                  pl.BlockSpec((tm, LANES), lambda j, i: (i, j))],
        out_specs=[pl.BlockSpec((tm, LANES), lambda j, i: (i, j)), pl.BlockSpec((1, LANES), lambda j, i: (0, 0))],
        out_shape=[jax.ShapeDtypeStruct((S, width), BF16), jax.ShapeDtypeStruct((1, LANES), F32)],
        compiler_params=_params(("arbitrary", "arbitrary"), 10 * tm * LANES * 4),
    )(x, jnp.tile(g64, 2).reshape(1, LANES), dh)


def gate_fwd(o, proj, *, gate_off, name, tm=512):
    S, W = o.shape
    cb = gate_off // W
    tm = _tile(S, tm)

    def body(o_ref, g_ref, out_ref):
        g = g_ref[...]
        out_ref[...] = (o_ref[...] * (g * _sigmoid(g))).astype(BF16)

    return _pallas(
        body, name=name, grid=(S // tm,),
        in_specs=[pl.BlockSpec((tm, W), lambda i: (i, 0)), pl.BlockSpec((tm, W), lambda i: (i, cb))],
        out_specs=pl.BlockSpec((tm, W), lambda i: (i, 0)),
        out_shape=jax.ShapeDtypeStruct((S, W), BF16),
        compiler_params=_params(("parallel",), 6 * tm * W * 4),
    )(o, proj)


def gate_bwd(dog, o, proj, *, gate_off, name, tm=512):
    S, W = o.shape
    cb = gate_off // W
    tm = _tile(S, tm)

    def body(d_ref, o_ref, g_ref, do_ref, dg_ref):
        g = g_ref[...]
        d = d_ref[...]
        s = _sigmoid(g)
        do_ref[...] = d * (g * s)
        dg_ref[...] = (d * o_ref[...] * (s * (1.0 + g * (1.0 - s)))).astype(BF16)

    row = pl.BlockSpec((tm, W), lambda i: (i, 0))
    return _pallas(
        body, name=name, grid=(S // tm,),
        in_specs=[row, row, pl.BlockSpec((tm, W), lambda i: (i, cb))],
        out_specs=[row, row],
        out_shape=[jax.ShapeDtypeStruct((S, W), F32), jax.ShapeDtypeStruct((S, W), BF16)],
        compiler_params=_params(("parallel",), 10 * tm * W * 4),
    )(dog, o, proj)


def loss_head(y, target, *, name, tm=512):
    S, W = y.shape
    tm = _tile(S, tm)
    n = S // tm

    def body(y_ref, t_ref, dy_ref, l_ref, acc_ref):
        i = pl.program_id(0)

        @pl.when(i == 0)
        def _():
            acc_ref[...] = jnp.zeros_like(acc_ref)

        e = y_ref[...] - t_ref[...]
        dy_ref[...] = e * (1.0 / W)
        acc_ref[...] += jnp.sum(e * e, axis=0, keepdims=True)

        @pl.when(i == n - 1)
        def _():
            l_ref[...] = jnp.full(l_ref.shape, (0.5 / W) * jnp.sum(acc_ref[...]), F32)

    row = pl.BlockSpec((tm, W), lambda i: (i, 0))
    return _pallas(
        body, name=name, grid=(n,),
        in_specs=[row, row],
        out_specs=[row, pl.BlockSpec((8, LANES), lambda i: (0, 0))],
        out_shape=[jax.ShapeDtypeStruct((S, W), F32), jax.ShapeDtypeStruct((8, LANES), F32)],
        scratch_shapes=[pltpu.VMEM((1, W), F32)],
        compiler_params=_params(("arbitrary",), 8 * tm * W * 4),
    )(y, target)


def _stack_heads(t, zero):
    first = _first_half(t.shape)
    return jnp.concatenate([jnp.where(first, t, zero), jnp.where(first, zero, t)], axis=0)


def _sb_weights(qs, ks, mask, upper, rss):
    zs = [_dot(q, k, NT) for q, k in zip(qs, ks)]
    sps = [jnp.maximum(z, 0.0) + jnp.log(1.0 + jnp.exp(-jnp.abs(z))) for z in zs]
    gs = [z - sp for z, sp in zip(zs, sps)]
    if mask is not None:
        sps = [jnp.where(mask, sp, 0.0) for sp in sps]
    cums = [_dot_split(sp, upper) for sp in sps]
    avs = [jnp.exp(g - (cum + rs)) for g, cum, rs in zip(gs, cums, rss)]
    if mask is not None:
        avs = [jnp.where(mask, a, 0.0) for a in avs]
    return avs, sps, gs


def _sb_consts():
    row = lax.broadcasted_iota(I32, (BLK, BLK), 0)
    col = lax.broadcasted_iota(I32, (BLK, BLK), 1)
    diag = col < row
    return row, col, jnp.concatenate([diag, diag], axis=0)


SB_DEAD = 88.0


def _sb_walk_left(block, i, carry):
    def least(c):
        m = c[0][1]
        for pair in c[1:]:
            m = jnp.minimum(m, pair[1])
        return jnp.min(m)

    def cond(state):
        jj, _, low = state
        return (jj < i) & (low < SB_DEAD)

    def body(state):
        jj, c, _ = state
        c = block(i - 1 - jj, c, None)
        return jj + 1, c, least(c)

    return lax.while_loop(cond, body, (jnp.int32(0), carry, least(carry)))[1]


SB_W = 1024


def sb_attn_fwd(qkv, *, name, pairs=8, ride=None):
    S = qkv.shape[0]
    W = SB_W
    PW = pairs * LANES
    ngrp, nq = W // PW, S // BLK
    assert ride is None or (ngrp == 1 and nq >= 3)

    def body(*refs):
        if ride is None:
            q_ref, k_ref, v_ref, o_ref = refs
        else:
            q_ref, k_ref, v_ref, o_ref = refs[0], refs[1], refs[2], refs[4]
            send, forward, finish = _gather_steps(refs[3], refs[5], refs[6], refs[7])
        i = pl.program_id(1)
        if ride is not None:
            pl.when(i == 0)(send)
            pl.when(i == nq // 2)(forward)
        row, col, diag = _sb_consts()
        upper = jnp.tile((row > col).astype(BF16), (2, 1))
        zero = jnp.zeros((BLK, LANES), BF16)
        qs = [_stack_heads(q_ref[:, p * LANES:(p + 1) * LANES] * 0.125, zero) for p in range(pairs)]

        def block(j, carry, mask):
            rows = pl.ds(pl.multiple_of(j * BLK, BLK), BLK)
            cols = [slice(p * LANES, (p + 1) * LANES) for p in range(pairs)]
            avs, sps, _ = _sb_weights(qs, [k_ref[rows, c] for c in cols], mask, upper, [c[1] for c in carry])
            abs_ = [a.astype(BF16) for a in avs]
            outs = [_dot(jnp.concatenate([ab[:BLK], ab[BLK:]], axis=1), _stack_heads(v_ref[rows, c], zero))
                    for ab, c in zip(abs_, cols)]
            return tuple((carry[p][0] + outs[p], carry[p][1] + jnp.sum(sps[p], axis=1, keepdims=True))
                         for p in range(pairs))

        init = tuple((jnp.zeros((BLK, LANES), F32), jnp.zeros((2 * BLK, 1), F32)) for _ in range(pairs))
        carry = _sb_walk_left(block, i, block(i, init, diag))
        for p in range(pairs):
            o_ref[:, p * LANES:(p + 1) * LANES] = carry[p][0]
        if ride is not None:
            pl.when(i == nq - 1)(finish)

    once = pl.Buffered(1)
    hbm = pl.BlockSpec(memory_space=pltpu.HBM)
    in_specs = [pl.BlockSpec((BLK, PW), lambda p, i: (i, p)),
                pl.BlockSpec((S, PW), lambda p, i: (0, ngrp + p), pipeline_mode=once),
                pl.BlockSpec((S, PW), lambda p, i: (0, 2 * ngrp + p), pipeline_mode=once)]
    out_specs = [pl.BlockSpec((BLK, PW), lambda p, i: (i, p))]
    out_shape = [jax.ShapeDtypeStruct((S, W), F32)]
    args = [qkv, qkv, qkv]
    if ride is not None:
        in_specs.append(hbm)
        out_specs.append(hbm)
        out_shape.append(jax.ShapeDtypeStruct((N_CHIPS,) + ride.shape, ride.dtype))
        args.append(ride)
    res = _pallas(
        body, name=name, grid=(ngrp, nq),
        in_specs=in_specs, out_specs=out_specs, out_shape=out_shape,
        scratch_shapes=[] if ride is None else list(GATHER_SEMS),
        compiler_params=_params(("parallel", "arbitrary"), 2 * S * PW * 2 + 16 * BLK * PW * 4),
    )(*args)
    return res[0] if ride is None else (res[0], place_own_block(res[1], ride))


def sb_attn_bwd(qkv, o, do, *, name, pairs=4, ride=None):
    S = qkv.shape[0]
    W = SB_W
    PW = pairs * LANES
    ngrp, nq = W // PW, S // BLK

    def body(*refs):
        q_ref, k_ref, v_ref, o_ref, do_ref = refs[:5]
        if ride is None:
            dq_ref, dk_ref, dv_ref = refs[5:]
        else:
            dq_ref, dk_ref, dv_ref = refs[6:9]
            send, finish = _chip_exchange_steps(refs[5], refs[9], refs[10], refs[11])
            pl.when((pl.program_id(0) == 0) & (pl.program_id(1) == 0))(send)
        i = pl.program_id(1)

        @pl.when(i == 0)
        def _():
            dk_ref[...] = jnp.zeros_like(dk_ref)
            dv_ref[...] = jnp.zeros_like(dv_ref)

        row, col, diag = _sb_consts()
        upper = jnp.tile((row > col).astype(BF16), (2, 1))
        upper_incl = jnp.tile((row >= col).astype(BF16), (2, 1))
        first = _first_half((BLK, LANES))
        zero = jnp.zeros((BLK, LANES), BF16)
        qs, dos, tots = [], [], []
        for p in range(pairs):
            cols = slice(p * LANES, (p + 1) * LANES)
            qs.append(_stack_heads(q_ref[:, cols] * 0.125, zero))
            dob = do_ref[:, cols].astype(BF16)
            dos.append(_stack_heads(dob, zero))
            prod = dob.astype(F32) * o_ref[:, cols]
            tots.append(jnp.concatenate([jnp.sum(jnp.where(first, prod, 0.0), axis=1, keepdims=True),
                                         jnp.sum(jnp.where(first, 0.0, prod), axis=1, keepdims=True)], axis=0))

        def block(j, carry, mask):
            rows = pl.ds(pl.multiple_of(j * BLK, BLK), BLK)
            P = range(pairs)
            cols = [slice(p * LANES, (p + 1) * LANES) for p in P]
            ks = [k_ref[rows, c] for c in cols]
            das = [_dot(dos[p], v_ref[rows, cols[p]], NT) for p in P]
            avs, sps, gs = _sb_weights(qs, ks, mask, upper, [c[1] for c in carry])
            abs_ = [a.astype(BF16) for a in avs]
            es = [ab.astype(F32) * da for ab, da in zip(abs_, das)]
            sufs = [_dot_split(e, upper_incl) for e in es]
            lefts = [tots[p] - (sufs[p] + carry[p][2]) for p in P]
            dzs = [es[p] - jnp.exp(gs[p]) * (es[p] + lefts[p]) for p in P]
            if mask is not None:
                dzs = [jnp.where(mask, dz, 0.0) for dz in dzs]
            dzbs = [dz.astype(BF16) for dz in dzs]
            dks = [_dot(dzbs[p], qs[p], TN) for p in P]
            dvs = [_dot(abs_[p], dos[p], TN) for p in P]
            dqs = [_dot(jnp.concatenate([dzbs[p][:BLK], dzbs[p][BLK:]], axis=1), _stack_heads(ks[p], zero)) for p in P]
            for p in P:
                dk_ref[rows, cols[p]] += dks[p]
                dv_ref[rows, cols[p]] += dvs[p]
            return tuple((carry[p][0] + dqs[p], carry[p][1] + jnp.sum(sps[p], axis=1, keepdims=True),
                          carry[p][2] + jnp.sum(es[p], axis=1, keepdims=True)) for p in P)

        col0 = jnp.zeros((2 * BLK, 1), F32)
        init = tuple((jnp.zeros((BLK, LANES), F32), col0, col0) for _ in range(pairs))
        carry = _sb_walk_left(block, i, block(i, init, diag))
        for p in range(pairs):
            dq_ref[:, p * LANES:(p + 1) * LANES] = (carry[p][0] * 0.125).astype(BF16)
        if ride is not None:
            pl.when((pl.program_id(0) == ngrp - 1) & (i == nq - 1))(finish)

    once = pl.Buffered(1)
    tile = pl.BlockSpec((BLK, PW), lambda p, i: (i, p))
    full = pl.BlockSpec((S, PW), lambda p, i: (0, p), pipeline_mode=once)
    shape = jax.ShapeDtypeStruct((S, W), F32)
    hbm = pl.BlockSpec(memory_space=pltpu.HBM)
    in_specs = [tile,
                pl.BlockSpec((S, PW), lambda p, i: (0, ngrp + p), pipeline_mode=once),
                pl.BlockSpec((S, PW), lambda p, i: (0, 2 * ngrp + p), pipeline_mode=once),
                tile, tile]
    out_specs = [tile, full, full]
    out_shape = [jax.ShapeDtypeStruct((S, W), BF16), shape, shape]
    args = [qkv, qkv, qkv, o, do]
    if ride is not None:
        in_specs.append(hbm)
        out_specs.append(hbm)
        out_shape.append(jax.ShapeDtypeStruct((len(OTHER_CHIPS),) + ride.shape[1:], ride.dtype))
        args.append(ride)
    order = ("parallel", "arbitrary") if ride is None else ("arbitrary", "arbitrary")
    return _pallas(
        body, name=name, grid=(ngrp, nq),
        in_specs=in_specs, out_specs=out_specs, out_shape=out_shape,
        scratch_shapes=[] if ride is None else list(CHIP_SEMS),
        compiler_params=_params(order, 2 * S * PW * 2 + 2 * S * PW * 4 + 16 * BLK * PW * 4),
    )(*args)


FB = 256


def flash_fwd(qn, kn, vb, *, name, heads=4, ride=None):
    S = qn.shape[0]
    H, DK, DV = MLA_HEADS, MLA_PAD, LANES
    nq, ngrp = S // FB, H // heads
    scale = 1.0 / math.sqrt(MLA_QK)
    assert ride is None or (ngrp >= 2 and nq >= 2)

    def body(*refs):
        if ride is None:
            q_ref, k_ref, v_ref, o_ref, lse_ref = refs
        else:
            q_ref, k_ref, v_ref, o_ref, lse_ref = refs[0], refs[1], refs[2], refs[4], refs[5]
            send, forward, finish = _gather_steps(refs[3], refs[6], refs[7], refs[8])
        g, i = pl.program_id(0), pl.program_id(1)
        if ride is not None:
            pl.when((g == 0) & (i == 0))(send)
            pl.when((g == 1) & (i == 0))(forward)
        diag = lax.broadcasted_iota(I32, (FB, FB), 1) <= lax.broadcasted_iota(I32, (FB, FB), 0)
        qs = [q_ref[:, h * DK:(h + 1) * DK] for h in range(heads)]

        def block(j, carry, mask):
            rows = pl.ds(pl.multiple_of(j * FB, FB), FB)
            H = range(heads)
            ss = [_dot(qs[h], k_ref[rows, h * DK:(h + 1) * DK], NT) * scale for h in H]
            if mask is not None:
                ss = [jnp.where(mask, s, -1e30) for s in ss]
            ms = [jnp.maximum(carry[h][1], jnp.max(ss[h], axis=1, keepdims=True)) for h in H]
            ps = [jnp.exp(ss[h] - ms[h]) for h in H]
            ws = [jnp.exp(carry[h][1] - ms[h]) for h in H]
            pvs = [_dot(ps[h].astype(BF16), v_ref[rows, h * DV:(h + 1) * DV]) for h in H]
            return tuple((carry[h][0] * ws[h] + pvs[h], ms[h], carry[h][2] * ws[h] + jnp.sum(ps[h], axis=1, keepdims=True))
                         for h in H)

        init = tuple((jnp.zeros((FB, DV), F32), jnp.full((FB, 1), -1e30, F32), jnp.zeros((FB, 1), F32))
                     for _ in range(heads))
        carry = lax.fori_loop(0, i, lambda j, c: block(j, c, None), init)
        carry = block(i, carry, diag)
        for h in range(heads):
            acc, m, l = carry[h]
            o_ref[:, h * DV:(h + 1) * DV] = acc / l
            lse_ref[h] = m + jnp.log(l)
        if ride is not None:
            pl.when((g == ngrp - 1) & (i == nq - 1))(finish)

    hbm = pl.BlockSpec(memory_space=pltpu.HBM)
    in_specs = [pl.BlockSpec((FB, heads * DK), lambda g, i: (i, g)),
                pl.BlockSpec((S, heads * DK), lambda g, i: (0, g), pipeline_mode=pl.Buffered(1)),
                pl.BlockSpec((S, heads * DV), lambda g, i: (0, g), pipeline_mode=pl.Buffered(1))]
    out_specs = [pl.BlockSpec((FB, heads * DV), lambda g, i: (i, g)), pl.BlockSpec((heads, FB, 1), lambda g, i: (g, i, 0))]
    out_shape = [jax.ShapeDtypeStruct((S, H * DV), F32), jax.ShapeDtypeStruct((H, S, 1), F32)]
    args = [qn, kn, vb]
    if ride is not None:
        in_specs.append(hbm)
        out_specs.append(hbm)
        out_shape.append(jax.ShapeDtypeStruct((N_CHIPS,) + ride.shape, ride.dtype))
        args.append(ride)
    order = ("parallel", "arbitrary") if ride is None else ("arbitrary", "arbitrary")
    res = _pallas(
        body, name=name, grid=(ngrp, nq),
        in_specs=in_specs, out_specs=out_specs, out_shape=out_shape,
        scratch_shapes=[] if ride is None else list(GATHER_SEMS),
        compiler_params=_params(order, 2 * S * heads * (DK + DV) * 2 + 16 * FB * FB * 4),
    )(*args)
    return tuple(res) if ride is None else (res[0], res[1], place_own_block(res[2], ride))


def flash_bwd(qn, kn, vb, o, lse, do, *, name, heads=4):
    S = qn.shape[0]
    H, DK, DV = MLA_HEADS, MLA_PAD, LANES
    nq, ngrp = S // FB, H // heads
    scale = 1.0 / math.sqrt(MLA_QK)

    def body(q_ref, k_ref, v_ref, o_ref, lse_ref, do_ref, dq_ref, dk_ref, dv_ref):
        i = pl.program_id(1)

        @pl.when(i == 0)
        def _():
            dk_ref[...] = jnp.zeros_like(dk_ref)
            dv_ref[...] = jnp.zeros_like(dv_ref)

        diag = lax.broadcasted_iota(I32, (FB, FB), 1) <= lax.broadcasted_iota(I32, (FB, FB), 0)
        qs, dobs, deltas, lses = [], [], [], []
        for h in range(heads):
            do = do_ref[:, h * DV:(h + 1) * DV]
            qs.append(q_ref[:, h * DK:(h + 1) * DK])
            dobs.append(do.astype(BF16))
            deltas.append(jnp.sum(do * o_ref[:, h * DV:(h + 1) * DV], axis=1, keepdims=True))
            lses.append(lse_ref[h])

        def block(j, carry, mask):
            rows = pl.ds(pl.multiple_of(j * FB, FB), FB)
            H = range(heads)
            kcs = [slice(h * DK, (h + 1) * DK) for h in H]
            vcs = [slice(h * DV, (h + 1) * DV) for h in H]
            ks = [k_ref[rows, kcs[h]] for h in H]
            ss = [_dot(qs[h], ks[h], NT) for h in H]
            dps = [_dot(dobs[h], v_ref[rows, vcs[h]], NT) for h in H]
            ps = [jnp.exp(ss[h] * scale - lses[h]) for h in H]
            if mask is not None:
                ps = [jnp.where(mask, p, 0.0) for p in ps]
            pbs = [p.astype(BF16) for p in ps]
            dss = [(ps[h] * (dps[h] - deltas[h]) * scale).astype(BF16) for h in H]
            dvs = [_dot(pbs[h], dobs[h], TN) for h in H]
            dks = [_dot(dss[h], qs[h], TN) for h in H]
            dqs = [_dot(dss[h], ks[h]) for h in H]
            for h in H:
                dv_ref[rows, vcs[h]] += dvs[h]
                dk_ref[rows, kcs[h]] += dks[h]
            return tuple(carry[h] + dqs[h] for h in H)

        carry = lax.fori_loop(0, i, lambda j, c: block(j, c, None), tuple(jnp.zeros((FB, DK), F32) for _ in range(heads)))
        carry = block(i, carry, diag)
        for h in range(heads):
            dq_ref[:, h * DK:(h + 1) * DK] = carry[h]

    qtile = pl.BlockSpec((FB, heads * DK), lambda g, i: (i, g))
    otile = pl.BlockSpec((FB, heads * DV), lambda g, i: (i, g))
    once = pl.Buffered(1)
    kfull = pl.BlockSpec((S, heads * DK), lambda g, i: (0, g), pipeline_mode=once)
    vfull = pl.BlockSpec((S, heads * DV), lambda g, i: (0, g), pipeline_mode=once)
    return _pallas(
        body, name=name, grid=(ngrp, nq),
        in_specs=[qtile, kfull, vfull, otile, pl.BlockSpec((heads, FB, 1), lambda g, i: (g, i, 0)), otile],
        out_specs=[qtile, kfull, vfull],
        out_shape=[jax.ShapeDtypeStruct((S, H * DK), F32), jax.ShapeDtypeStruct((S, H * DK), F32),
                   jax.ShapeDtypeStruct((S, H * DV), F32)],
        compiler_params=_params(("parallel", "arbitrary"), S * heads * (DK + DV) * 6 + 16 * FB * FB * 4),
    )(qn, kn, vb, o, lse, do)


def _rope_tables(S):
    half = 32
    inv_freq = ROPE_THETA ** (-jnp.arange(half, dtype=F32) / half)
    ang = jnp.arange(S).astype(F32)[:, None] * inv_freq[None, :]
    cos, sin = jnp.cos(ang), jnp.sin(ang)
    ones, zeros = jnp.ones((S, LANES), F32), jnp.zeros((S, LANES), F32)
    pad = jnp.zeros((S, 64), F32)
    return (jnp.concatenate([ones, cos, cos, pad + 1.0], axis=1),
            jnp.concatenate([zeros, -sin, sin, pad], axis=1))


def _rope_partner(u):
    lane = lax.broadcasted_iota(I32, u.shape, 1)
    return jnp.where((lane % HEAD64) < 32, pltpu.roll(u, LANES - 32, 1), pltpu.roll(u, 32, 1))


def _normrope(raw, g, cos, sgn):
    r = lax.rsqrt(jnp.sum(raw * raw, axis=1, keepdims=True) * (1.0 / MLA_QK) + NORM_EPS)
    y = raw * r
    u = y * g
    pe = u[:, LANES:]
    out = jnp.concatenate([u[:, :LANES], pe * cos[:, LANES:] + _rope_partner(pe) * sgn[:, LANES:]], axis=1)
    return out, y, r


def _normrope_bwd(dout, g, cos, sgn, y, r):
    dpe = dout[:, LANES:]
    du = jnp.concatenate([dout[:, :LANES], dpe * cos[:, LANES:] + _rope_partner(dpe * sgn[:, LANES:])], axis=1)
    dy = du * g
    draw = r * (dy - y * (jnp.sum(dy * y, axis=1, keepdims=True) * (1.0 / MLA_QK)))
    return draw, jnp.sum(du * y, axis=0, keepdims=True)


def mla_prep_fwd(qraw, kv, proj, gq, gk, cos, sgn, *, kpe_off, name, tm=1024):
    S = qraw.shape[0]
    tm = _tile(S, tm)
    kb = kpe_off // LANES

    def body(q_ref, kn_ref, v_ref, kpe_ref, gq_ref, gk_ref, c_ref, s_ref, qo_ref, ko_ref, vo_ref):
        cos, sgn = c_ref[...], s_ref[...]
        qo_ref[...] = _normrope(q_ref[...], gq_ref[...], cos, sgn)[0].astype(BF16)
        kraw = jnp.concatenate([kn_ref[...], kpe_ref[...]], axis=1)
        ko_ref[...] = _normrope(kraw, gk_ref[...], cos, sgn)[0].astype(BF16)
        vo_ref[...] = v_ref[...].astype(BF16)

    head = pl.BlockSpec((tm, MLA_PAD), lambda i, h: (i, h))
    gain = pl.BlockSpec((1, MLA_PAD), lambda i, h: (0, 0))
    tab = pl.BlockSpec((tm, MLA_PAD), lambda i, h: (i, 0))
    return _pallas(
        body, name=name, grid=(S // tm, MLA_HEADS),
        in_specs=[head, pl.BlockSpec((tm, LANES), lambda i, h: (i, 2 * h)), pl.BlockSpec((tm, LANES), lambda i, h: (i, 2 * h + 1)),
                  pl.BlockSpec((tm, LANES), lambda i, h: (i, kb)), gain, gain, tab, tab],
        out_specs=[head, head, pl.BlockSpec((tm, LANES), lambda i, h: (i, h))],
        out_shape=[jax.ShapeDtypeStruct(qraw.shape, BF16), jax.ShapeDtypeStruct(qraw.shape, BF16),
                   jax.ShapeDtypeStruct((S, MLA_HEADS * LANES), BF16)],
        compiler_params=_params(("parallel", "arbitrary"), 16 * tm * MLA_PAD * 4),
    )(qraw, kv, kv, proj, gq, gk, cos, sgn)


def mla_prep_bwd(dqn, dkn, dv, qraw, kv, proj, gq, gk, cos, sgn, *, kpe_off, name, tm=512):
    S = qraw.shape[0]
    tm = _tile(S, tm)
    kb = kpe_off // LANES

    def body(dq_ref, dk_ref, dv_ref, q_ref, kn_ref, kpe_ref, gq_ref, gk_ref, c_ref, s_ref,
             dqo_ref, dkv_ref, dkpe_ref, dgq_ref, dgk_ref, acc_ref):
        i, h = pl.program_id(0), pl.program_id(1)

        @pl.when((i == 0) & (h == 0))
        def _():
            dgq_ref[...] = jnp.zeros_like(dgq_ref)
            dgk_ref[...] = jnp.zeros_like(dgk_ref)

        @pl.when(h == 0)
        def _():
            acc_ref[...] = jnp.zeros_like(acc_ref)

        cos, sgn = c_ref[...], s_ref[...]
        _, yq, rq = _normrope(q_ref[...], gq_ref[...], cos, sgn)
        dq, dgq = _normrope_bwd(dq_ref[...], gq_ref[...], cos, sgn, yq, rq)
        dqo_ref[...] = dq.astype(BF16)
        dgq_ref[...] += dgq
        kraw = jnp.concatenate([kn_ref[...], kpe_ref[...]], axis=1)
        _, yk, rk = _normrope(kraw, gk_ref[...], cos, sgn)
        dk, dgk = _normrope_bwd(dk_ref[...], gk_ref[...], cos, sgn, yk, rk)
        dgk_ref[...] += dgk
        dkv_ref[...] = jnp.concatenate([dk[:, :LANES], dv_ref[...]], axis=1).astype(BF16)
        acc_ref[...] += dk[:, LANES:]

        @pl.when(h == MLA_HEADS - 1)
        def _():
            dkpe_ref[...] = acc_ref[...].astype(BF16)

    head = pl.BlockSpec((tm, MLA_PAD), lambda i, h: (i, h))
    gain = pl.BlockSpec((1, MLA_PAD), lambda i, h: (0, 0))
    tab = pl.BlockSpec((tm, MLA_PAD), lambda i, h: (i, 0))
    return _pallas(
        body, name=name, grid=(S // tm, MLA_HEADS),
        in_specs=[head, head, pl.BlockSpec((tm, LANES), lambda i, h: (i, h)), head,
                  pl.BlockSpec((tm, LANES), lambda i, h: (i, 2 * h)), pl.BlockSpec((tm, LANES), lambda i, h: (i, kb)),
                  gain, gain, tab, tab],
        out_specs=[head, head, pl.BlockSpec((tm, LANES), lambda i, h: (i, 0)), gain, gain],
        out_shape=[jax.ShapeDtypeStruct(qraw.shape, BF16), jax.ShapeDtypeStruct(qraw.shape, BF16),
                   jax.ShapeDtypeStruct((S, LANES), BF16), jax.ShapeDtypeStruct((1, MLA_PAD), F32),
                   jax.ShapeDtypeStruct((1, MLA_PAD), F32)],
        scratch_shapes=[pltpu.VMEM((tm, LANES), F32)],
        compiler_params=_params(("arbitrary", "arbitrary"), 24 * tm * MLA_PAD * 4),
    )(dqn, dkn, dv, qraw, kv, proj, gq, gk, cos, sgn)


def swa_kv_prep(proj, gk64, *, k_off, v_off, name, tm=2048):
    S = proj.shape[0]
    tm = _tile(S, tm)
    W = SWA_KV * HEAD64

    def body(k_ref, v_ref, g_ref, ko_ref, vo_ref):
        first = _first_half((tm, LANES))

        def dup(n):
            nr = pltpu.roll(n, HEAD64, 1)
            return jnp.where(first, n, nr), jnp.where(first, nr, n)

        for t in range(W // LANES):
            x = k_ref[:, t * LANES:(t + 1) * LANES]
            n = x * lax.rsqrt(_segmean64(x * x) + NORM_EPS) * g_ref[...]
            d0, d1 = dup(n)
            ko_ref[:, 2 * t * LANES:(2 * t + 1) * LANES] = d0.astype(BF16)
            ko_ref[:, (2 * t + 1) * LANES:(2 * t + 2) * LANES] = d1.astype(BF16)
            d0, d1 = dup(v_ref[:, t * LANES:(t + 1) * LANES])
            vo_ref[:, 2 * t * LANES:(2 * t + 1) * LANES] = d0.astype(BF16)
            vo_ref[:, (2 * t + 1) * LANES:(2 * t + 2) * LANES] = d1.astype(BF16)

    out = pl.BlockSpec((tm, SWA_KV * LANES), lambda i: (i, 0))
    shape = jax.ShapeDtypeStruct((S, SWA_KV * LANES), BF16)
    return _pallas(
        body, name=name, grid=(S // tm,),
        in_specs=[pl.BlockSpec((tm, W), lambda i: (i, k_off // W)), pl.BlockSpec((tm, W), lambda i: (i, v_off // W)),
                  pl.BlockSpec((1, LANES), lambda i: (0, 0))],
        out_specs=[out, out], out_shape=[shape, shape],
        compiler_params=_params(("parallel",), 12 * tm * W * 4),
    )(proj, proj, jnp.tile(gk64, 2).reshape(1, LANES))


def swa_kv_prep_bwd(dkdup, dvdup, proj, gk64, *, k_off, name, tm=2048):
    S = proj.shape[0]
    tm = _tile(S, tm)
    W = SWA_KV * HEAD64

    def body(dk_ref, dv_ref, k_ref, g_ref, dko_ref, dvo_ref, dg_ref):
        @pl.when(pl.program_id(0) == 0)
        def _():
            dg_ref[...] = jnp.zeros_like(dg_ref)

        first = _first_half((tm, LANES))

        def fold(ref, t):
            d0 = ref[:, 2 * t * LANES:(2 * t + 1) * LANES]
            d1 = ref[:, (2 * t + 1) * LANES:(2 * t + 2) * LANES]
            return jnp.where(first, d0 + pltpu.roll(d0, HEAD64, 1), d1 + pltpu.roll(d1, HEAD64, 1))

        for t in range(W // LANES):
            dvo_ref[:, t * LANES:(t + 1) * LANES] = fold(dv_ref, t).astype(BF16)
            dh = fold(dk_ref, t)
            x = k_ref[:, t * LANES:(t + 1) * LANES]
            r = lax.rsqrt(_segmean64(x * x) + NORM_EPS)
            y = x * r
            dy = dh * g_ref[...]
            dko_ref[:, t * LANES:(t + 1) * LANES] = (r * (dy - y * _segmean64(dy * y))).astype(BF16)
            dg_ref[...] += jnp.sum(dh * y, axis=0, keepdims=True)

    dup = pl.BlockSpec((tm, SWA_KV * LANES), lambda i: (i, 0))
    out = pl.BlockSpec((tm, W), lambda i: (i, 0))
    shape = jax.ShapeDtypeStruct((S, W), BF16)
    return _pallas(
        body, name=name, grid=(S // tm,),
        in_specs=[dup, dup, pl.BlockSpec((tm, W), lambda i: (i, k_off // W)), pl.BlockSpec((1, LANES), lambda i: (0, 0))],
        out_specs=[out, out, pl.BlockSpec((1, LANES), lambda i: (0, 0))],
        out_shape=[shape, shape, jax.ShapeDtypeStruct((1, LANES), F32)],
        compiler_params=_params(("arbitrary",), 16 * tm * W * 4),
    )(dkdup, dvdup, proj, jnp.tile(gk64, 2).reshape(1, LANES))


def _swa_geometry(i):
    r = lax.broadcasted_iota(I32, (BLK, 2 * BLK), 0)
    c = lax.broadcasted_iota(I32, (BLK, 2 * BLK), 1)
    rel = r + BLK - c
    valid = (rel >= 0) & (rel < BLK) & ((c >= BLK) | (i > 0))
    return valid, rel.astype(F32)


def _swa_slope(h):
    return 2.0 ** (-8.0 * (h + 1) / SWA_HEADS)


def swa_attn_fwd(qn, kdup, vdup, sinks, *, name):
    S = qn.shape[0]
    nq = S // BLK
    group = SWA_HEADS // SWA_KV

    def body(q_ref, kp_ref, kc_ref, vp_ref, vc_ref, sink_ref, o_ref, lse_ref):
        i = pl.program_id(0)
        valid, rel = _swa_geometry(i)
        first = _first_half((BLK, LANES))
        lane = lax.broadcasted_iota(I32, (BLK, LANES), 1)
        lse_all = jnp.zeros((BLK, LANES), F32)
        zero = jnp.zeros((BLK, LANES), BF16)
        H = range(SWA_HEADS)
        kks = [jnp.concatenate([kp_ref[:, g * LANES:(g + 1) * LANES], kc_ref[:, g * LANES:(g + 1) * LANES]], axis=0)
               for g in range(SWA_KV)]
        vvs = [jnp.concatenate([vp_ref[:, g * LANES:(g + 1) * LANES], vc_ref[:, g * LANES:(g + 1) * LANES]], axis=0)
               for g in range(SWA_KV)]
        q2s = [q_ref[:, t * LANES:(t + 1) * LANES] for t in range(SWA_HEADS // 2)]
        qhs = [jnp.where(first if h % 2 == 0 else ~first, q2s[h // 2], zero) for h in H]
        ss = [_dot(qhs[h], kks[h // group], NT) * (1.0 / math.sqrt(HEAD64)) - _swa_slope(h) * rel for h in H]
        ss = [jnp.where(valid, s, -1e30) for s in ss]
        ms = [jnp.maximum(jnp.max(ss[h], axis=1, keepdims=True), sink_ref[h]) for h in H]
        es = [jnp.exp(ss[h] - ms[h]) for h in H]
        dens = [jnp.sum(es[h], axis=1, keepdims=True) + jnp.exp(sink_ref[h] - ms[h]) for h in H]
        outs = [_dot((es[h] / dens[h]).astype(BF16), vvs[h // group]) for h in H]
        for h in H:
            lse_all = jnp.where(lane == h, ms[h] + jnp.log(dens[h]), lse_all)
        for t in range(SWA_HEADS // 2):
            o_ref[:, t * LANES:(t + 1) * LANES] = jnp.where(first, outs[2 * t], outs[2 * t + 1])
        lse_ref[...] = lse_all

    prev = lambda i: (jnp.maximum(i - 1, 0), 0)
    cur = lambda i: (i, 0)
    kvw = SWA_KV * LANES
    return _pallas(
        body, name=name, grid=(nq,),
        in_specs=[pl.BlockSpec((BLK, 1024), cur), pl.BlockSpec((BLK, kvw), prev), pl.BlockSpec((BLK, kvw), cur),
                  pl.BlockSpec((BLK, kvw), prev), pl.BlockSpec((BLK, kvw), cur),
                  pl.BlockSpec(memory_space=pltpu.SMEM)],
        out_specs=[pl.BlockSpec((BLK, 1024), cur), pl.BlockSpec((BLK, LANES), cur)],
        out_shape=[jax.ShapeDtypeStruct((S, 1024), F32), jax.ShapeDtypeStruct((S, LANES), F32)],
        compiler_params=_params(("parallel",), 16 * BLK * 1024 * 4),
    )(qn, kdup, kdup, vdup, vdup, sinks)


def swa_attn_bwd(qn, kdup, vdup, sinks, o, lse, do, *, name):
    S = qn.shape[0]
    nq = S // BLK
    group = SWA_HEADS // SWA_KV
    scale = 1.0 / math.sqrt(HEAD64)

    def body(q_ref, kp_ref, kc_ref, vp_ref, vc_ref, sink_ref, o_ref, lse_ref, do_ref,
             dq_ref, dk_ref, dv_ref, ds_ref):
        i = pl.program_id(0)

        @pl.when(i == 0)
        def _():
            dk_ref[...] = jnp.zeros_like(dk_ref)
            dv_ref[...] = jnp.zeros_like(dv_ref)
            ds_ref[...] = jnp.zeros_like(ds_ref)

        valid, rel = _swa_geometry(i)
        first = _first_half((BLK, LANES))
        lane1 = lax.broadcasted_iota(I32, (1, LANES), 1)
        lane = lax.broadcasted_iota(I32, (BLK, LANES), 1)
        lse_all = lse_ref[...]
        prow = pl.ds(pl.multiple_of(jnp.maximum(i - 1, 0) * BLK, BLK), BLK)
        crow = pl.ds(pl.multiple_of(i * BLK, BLK), BLK)
        dsink = jnp.zeros((1, LANES), F32)
        zero = jnp.zeros((BLK, LANES), BF16)
        H, T = range(SWA_HEADS), range(SWA_HEADS // 2)
        kks = [jnp.concatenate([kp_ref[:, g * LANES:(g + 1) * LANES], kc_ref[:, g * LANES:(g + 1) * LANES]], axis=0)
               for g in range(SWA_KV)]
        vvs = [jnp.concatenate([vp_ref[:, g * LANES:(g + 1) * LANES], vc_ref[:, g * LANES:(g + 1) * LANES]], axis=0)
               for g in range(SWA_KV)]
        q2s = [q_ref[:, t * LANES:(t + 1) * LANES] for t in T]
        do2s = [do_ref[:, t * LANES:(t + 1) * LANES] for t in T]
        prods = [do2s[t] * o_ref[:, t * LANES:(t + 1) * LANES] for t in T]
        mine = [first if h % 2 == 0 else ~first for h in H]
        qhs = [jnp.where(mine[h], q2s[h // 2], zero) for h in H]
        dohs = [jnp.where(mine[h], do2s[h // 2], 0.0).astype(BF16) for h in H]
        deltas = [jnp.sum(jnp.where(mine[h], prods[h // 2], 0.0), axis=1, keepdims=True) for h in H]
        lses = [jnp.sum(jnp.where(lane == h, lse_all, 0.0), axis=1, keepdims=True) for h in H]
        ss = [_dot(qhs[h], kks[h // group], NT) * scale - _swa_slope(h) * rel for h in H]
        dps = [_dot(dohs[h], vvs[h // group], NT) for h in H]
        ps = [jnp.where(valid, jnp.exp(ss[h] - lses[h]), 0.0) for h in H]
        dscs = [(ps[h] * (dps[h] - deltas[h]) * scale).astype(BF16) for h in H]
        dqs = [_dot(dscs[h], kks[h // group]) for h in H]
        dks = [_dot(dscs[h], qhs[h], TN) for h in H]
        dvs = [_dot(ps[h].astype(BF16), dohs[h], TN) for h in H]
        for h in H:
            psink = jnp.exp(sink_ref[h] - lses[h])
            dsink = dsink + jnp.where(lane1 == h, -jnp.sum(psink * deltas[h]), 0.0)
        for t in T:
            dq_ref[:, t * LANES:(t + 1) * LANES] = jnp.where(first, dqs[2 * t], dqs[2 * t + 1])
        for g in range(SWA_KV):
            cols = slice(g * LANES, (g + 1) * LANES)
            b = g * group
            dkk = (dks[b] + dks[b + 1]) + (dks[b + 2] + dks[b + 3])
            dvv = (dvs[b] + dvs[b + 1]) + (dvs[b + 2] + dvs[b + 3])
            dk_ref[prow, cols] += dkk[:BLK]
            dv_ref[prow, cols] += dvv[:BLK]
            dk_ref[crow, cols] += dkk[BLK:]
            dv_ref[crow, cols] += dvv[BLK:]
        ds_ref[...] += dsink

    prev = lambda i: (jnp.maximum(i - 1, 0), 0)
    cur = lambda i: (i, 0)
    kvw = SWA_KV * LANES
    whole = pl.BlockSpec((S, kvw), lambda i: (0, 0))
    return _pallas(
        body, name=name, grid=(nq,),
        in_specs=[pl.BlockSpec((BLK, 1024), cur), pl.BlockSpec((BLK, kvw), prev), pl.BlockSpec((BLK, kvw), cur),
                  pl.BlockSpec((BLK, kvw), prev), pl.BlockSpec((BLK, kvw), cur),
                  pl.BlockSpec(memory_space=pltpu.SMEM),
                  pl.BlockSpec((BLK, 1024), cur), pl.BlockSpec((BLK, LANES), cur), pl.BlockSpec((BLK, 1024), cur)],
        out_specs=[pl.BlockSpec((BLK, 1024), cur), whole, whole, pl.BlockSpec((1, LANES), lambda i: (0, 0))],
        out_shape=[jax.ShapeDtypeStruct((S, 1024), F32), jax.ShapeDtypeStruct((S, kvw), F32),
                   jax.ShapeDtypeStruct((S, kvw), F32), jax.ShapeDtypeStruct((1, LANES), F32)],
        compiler_params=_params(("arbitrary",), 4 * S * kvw * 4 + 24 * BLK * 1024 * 4),
    )(qn, kdup, kdup, vdup, vdup, sinks, o, lse, do)


def _in_bwd(x, h, dproj, dy, g, w_in, tag):
    dh = matmul(dproj, w_in, "nt", name=f"{tag}_dh")
    dw_in = matmul(h, dproj, "tn", name=f"{tag}_dwin")
    dx, dg = rmsnorm_bwd(x, g, dh, col_off=0, width=D_MODEL, out_dtype=F32, res=dy, name=f"{tag}_dnorm")
    return dx, dw_in, dg


def _out_bwd(dy, og, o, proj, w_out, gate_off, tag):
    dog = matmul(dy, w_out, "nt", name=f"{tag}_dog")
    dw_out = matmul(og, dy, "tn", name=f"{tag}_dwout")
    do, dgate = gate_bwd(dog, o, proj, gate_off=gate_off, name=f"{tag}_dgate")
    return do, dgate, dw_out


def sb_fwd(x, p, tag, ride=None):
    h = rmsnorm_fwd(x, p["norm"], col_off=0, width=D_MODEL, out_dtype=BF16, name=f"{tag}_norm")
    qkv = matmul(h, p["w_in"], "nn", out_dtype=BF16, b_cols=(0, 3 * SB_W), name=f"{tag}_proj")
    gate = matmul(h, p["w_in"], "nn", b_cols=(3 * SB_W, SB_W), name=f"{tag}_gproj")
    o = sb_attn_fwd(qkv, name=f"{tag}_attn", ride=ride)
    if ride is not None:
        o, gathered = o
    og = gate_fwd(o, gate, gate_off=0, name=f"{tag}_gate")
    y = matmul(og, p["w_out"], "nn", add=x, name=f"{tag}_out")
    saved = (x, h, gate, qkv, o, og)
    return (y, saved) if ride is None else (y, saved, gathered)


def sb_bwd(dy, saved, p, tag, ride=None):
    x, h, gate, qkv, o, og = saved
    do, dgate, dw_out = _out_bwd(dy, og, o, gate, p["w_out"], 0, tag)
    dq, dk, dv, *received = sb_attn_bwd(qkv, o, do, name=f"{tag}_dattn", ride=ride)
    dproj = jnp.concatenate([dq, dk.astype(BF16), dv.astype(BF16), dgate], axis=1)
    dx, dw_in, dg = _in_bwd(x, h, dproj, dy, p["norm"], p["w_in"], tag)
    grads = {"norm": dg[0], "w_in": dw_in, "w_out": dw_out}
    return (dx, grads) if ride is None else (dx, grads, received[0])


MLA_GATE, MLA_QLAT, MLA_KVLAT, MLA_KPE, MLA_IN = 0, 1024, 1280, 1408, 1536


def mla_fwd(x, p, tabs, tag, ride=None):
    cos, sgn = tabs
    h = rmsnorm_fwd(x, p["norm"], col_off=0, width=D_MODEL, out_dtype=BF16, name=f"{tag}_norm")
    proj = matmul(h, p["w_in"], "nn", name=f"{tag}_proj")
    ql = rmsnorm_fwd(proj, p["q_a_norm"], col_off=MLA_QLAT, width=256, out_dtype=BF16, name=f"{tag}_qanorm")
    kvl = rmsnorm_fwd(proj, p["kv_a_norm"], col_off=MLA_KVLAT, width=128, out_dtype=BF16, name=f"{tag}_kvanorm")
    qraw = matmul(ql, p["w_uq"], "nn", name=f"{tag}_uq")
    kv = matmul(kvl, p["w_ukv"], "nn", name=f"{tag}_ukv")
    qn, kn, vb = mla_prep_fwd(qraw, kv, proj, p["gq"], p["gk"], cos, sgn, kpe_off=MLA_KPE, name=f"{tag}_prep")
    o, lse, *gathered = flash_fwd(qn, kn, vb, name=f"{tag}_attn", ride=ride)
    og = gate_fwd(o, proj, gate_off=MLA_GATE, name=f"{tag}_gate")
    y = matmul(og, p["w_out"], "nn", add=x, name=f"{tag}_out")
    saved = (x, h, proj, ql, kvl, qraw, kv, qn, kn, vb, o, lse, og)
    return (y, saved) if ride is None else (y, saved, gathered[0])


def mla_bwd(dy, saved, p, tabs, tag):
    cos, sgn = tabs
    x, h, proj, ql, kvl, qraw, kv, qn, kn, vb, o, lse, og = saved
    do, dgate, dw_out = _out_bwd(dy, og, o, proj, p["w_out"], MLA_GATE, tag)
    dqn, dkn, dv = flash_bwd(qn, kn, vb, o, lse, do, name=f"{tag}_dattn")
    dqraw, dkv, dkpe, dgq, dgk = mla_prep_bwd(dqn, dkn, dv, qraw, kv, proj, p["gq"], p["gk"], cos, sgn,
                                              kpe_off=MLA_KPE, name=f"{tag}_dprep")
    dql = matmul(dqraw, p["w_uq"], "nt", name=f"{tag}_dql")
    dw_uq = matmul(ql, dqraw, "tn", name=f"{tag}_dwuq")
    dkvl = matmul(dkv, p["w_ukv"], "nt", name=f"{tag}_dkvl")
    dw_ukv = matmul(kvl, dkv, "tn", name=f"{tag}_dwukv")
    dqlat, dgqa = rmsnorm_bwd(proj, p["q_a_norm"], dql, col_off=MLA_QLAT, width=256, out_dtype=BF16, name=f"{tag}_dqanorm")
    dkvlat, dgkva = rmsnorm_bwd(proj, p["kv_a_norm"], dkvl, col_off=MLA_KVLAT, width=128, out_dtype=BF16,
                                name=f"{tag}_dkvanorm")
    dproj = jnp.concatenate([dgate, dqlat, dkvlat, dkpe], axis=1)
    dx, dw_in, dg = _in_bwd(x, h, dproj, dy, p["norm"], p["w_in"], tag)
    return dx, {"norm": dg[0], "w_in": dw_in, "q_a_norm": dgqa[0], "w_uq": dw_uq, "kv_a_norm": dgkva[0],
                "w_ukv": dw_ukv, "gq": dgq[0], "gk": dgk[0], "w_out": dw_out}


SWA_Q, SWA_GATE, SWA_K, SWA_V = 0, 1024, 2048, 2304


def swa_fwd(x, p, tag):
    h = rmsnorm_fwd(x, p["norm"], col_off=0, width=D_MODEL, out_dtype=BF16, name=f"{tag}_norm")
    proj = matmul(h, p["w_in"], "nn", name=f"{tag}_proj")
    qn = headnorm_fwd(proj, p["q_head_norm"], col_off=SWA_Q, width=1024, name=f"{tag}_qnorm")
    kdup, vdup = swa_kv_prep(proj, p["k_head_norm"], k_off=SWA_K, v_off=SWA_V, name=f"{tag}_kvprep")
    o, lse = swa_attn_fwd(qn, kdup, vdup, p["sinks"], name=f"{tag}_attn")
    og = gate_fwd(o, proj, gate_off=SWA_GATE, name=f"{tag}_gate")
    y = matmul(og, p["w_out"], "nn", add=x, name=f"{tag}_out")
    return y, (x, h, proj, qn, kdup, vdup, o, lse, og)


def swa_bwd(dy, saved, p, tag):
    x, h, proj, qn, kdup, vdup, o, lse, og = saved
    do, dgate, dw_out = _out_bwd(dy, og, o, proj, p["w_out"], SWA_GATE, tag)
    dqn, dkdup, dvdup, dsinks = swa_attn_bwd(qn, kdup, vdup, p["sinks"], o, lse, do, name=f"{tag}_dattn")
    dq, dgq = headnorm_bwd(proj, p["q_head_norm"], dqn, col_off=SWA_Q, width=1024, name=f"{tag}_dqnorm")
    dk, dv, dgk = swa_kv_prep_bwd(dkdup, dvdup, proj, p["k_head_norm"], k_off=SWA_K, name=f"{tag}_dkvprep")
    dproj = jnp.concatenate([dq, dgate, dk, dv], axis=1)
    dx, dw_in, dg = _in_bwd(x, h, dproj, dy, p["norm"], p["w_in"], tag)
    return dx, {"norm": dg[0], "w_in": dw_in, "q_head_norm": dgq[0, :HEAD64] + dgq[0, HEAD64:],
                "k_head_norm": dgk[0, :HEAD64] + dgk[0, HEAD64:], "sinks": dsinks[0, :SWA_HEADS], "w_out": dw_out}


def prepare_layer(w, i):
    if i in (0, 3):
        return {"norm": w[f"l{i}_norm"], "w_in": w[f"l{i}_w_in"], "w_out": w[f"l{i}_w_out"]}
    if i == 1:
        l1_in = w["l1_w_in"]
        pad64 = lambda v: jnp.pad(v, (0, MLA_PAD - MLA_QK)).reshape(1, MLA_PAD)
        return {"norm": w["l1_norm"],
                "w_in": jnp.concatenate([l1_in[:, 448:], l1_in[:, :448], jnp.zeros((D_MODEL, 64), l1_in.dtype)], axis=1),
                "q_a_norm": w["l1_q_a_norm"], "kv_a_norm": w["l1_kv_a_norm"],
                "w_uq": jnp.pad(w["l1_w_uq"].reshape(256, MLA_HEADS, MLA_QK), ((0, 0), (0, 0), (0, MLA_PAD - MLA_QK))
                                ).reshape(256, MLA_HEADS * MLA_PAD),
                "w_ukv": w["l1_w_ukv"], "gq": pad64(w["l1_q_head_norm"]), "gk": pad64(w["l1_k_head_norm"]),
                "w_out": w["l1_w_out"]}
    l2_in = w["l2_w_in"]
    return {"norm": w["l2_norm"],
            "w_in": jnp.concatenate([l2_in[:, :1024], l2_in[:, 1536:], l2_in[:, 1024:1536]], axis=1),
            "q_head_norm": w["l2_q_head_norm"], "k_head_norm": w["l2_k_head_norm"], "sinks": w["l2_sinks"],
            "w_out": w["l2_w_out"]}


def unprepare_grads(gs):
    g0, g1, g2, g3 = gs
    d1, d2 = g1["w_in"], g2["w_in"]
    first = {} if g0 is None else {"l0_norm": g0["norm"], "l0_w_in": g0["w_in"], "l0_w_out": g0["w_out"]}
    return {
        **first,
        "l1_norm": g1["norm"], "l1_w_in": jnp.concatenate([d1[:, 1024:1472], d1[:, :1024]], axis=1),
        "l1_q_a_norm": g1["q_a_norm"],
        "l1_w_uq": g1["w_uq"].reshape(256, MLA_HEADS, MLA_PAD)[:, :, :MLA_QK].reshape(256, MLA_HEADS * MLA_QK),
        "l1_kv_a_norm": g1["kv_a_norm"], "l1_w_ukv": g1["w_ukv"],
        "l1_q_head_norm": g1["gq"][:MLA_QK], "l1_k_head_norm": g1["gk"][:MLA_QK], "l1_w_out": g1["w_out"],
        "l2_norm": g2["norm"], "l2_w_in": jnp.concatenate([d2[:, :1024], d2[:, 2048:], d2[:, 1024:2048]], axis=1),
        "l2_q_head_norm": g2["q_head_norm"], "l2_k_head_norm": g2["k_head_norm"], "l2_sinks": g2["sinks"],
        "l2_w_out": g2["w_out"],
        "l3_norm": g3["norm"], "l3_w_in": g3["w_in"], "l3_w_out": g3["w_out"],
    }


def local_step(x, target, w, late=None, early=None):
    tabs = _rope_tables(x.shape[0])
    p0 = prepare_layer(w, 0)
    if late is None:
        y0, s0 = sb_fwd(x, p0, "l0")
        p1 = prepare_layer(w, 1)
        y1, s1 = mla_fwd(y0, p1, tabs, "l1")
    else:
        y0, s0, gathered = sb_fwd(x, p0, "l0", ride=late[0][0])
        w = {**w, **late[0][1](gathered)}
        p1 = prepare_layer(w, 1)
        y1, s1, gathered = mla_fwd(y0, p1, tabs, "l1", ride=late[1][0])
        w = {**w, **late[1][1](gathered)}
    ps = [p0, p1, prepare_layer(w, 2), prepare_layer(w, 3)]
    y2, s2 = swa_fwd(y1, ps[2], "l2")
    y3, s3 = sb_fwd(y2, ps[3], "l3")
    dy, loss = loss_head(y3, target, name="loss")
    d3, g3 = sb_bwd(dy, s3, ps[3], "l3")
    d2, g2 = swa_bwd(d3, s2, ps[2], "l2")
    d1, g1 = mla_bwd(d2, s1, ps[1], tabs, "l1")
    if early is None:
        d0, g0 = sb_bwd(d1, s0, ps[0], "l0")
        return loss, d0, unprepare_grads([g0, g1, g2, g3])
    d0, g0, received = sb_bwd(d1, s0, ps[0], "l0", ride=early(unprepare_grads([None, g1, g2, g3])))
    return loss, d0, unprepare_grads([g0, g1, g2, g3]), received


MATS = (("l0_w_in", "col", 1024, 4096), ("l0_w_out", "row", 1024, 1024), ("l1_w_in", "col", 1024, 1472),
        ("l1_w_uq", "col", 256, 1536), ("l1_w_ukv", "col", 128, 2048), ("l1_w_out", "row", 1024, 1024),
        ("l2_w_in", "col", 1024, 2560), ("l2_w_out", "row", 1024, 1024), ("l3_w_in", "col", 1024, 4096),
        ("l3_w_out", "row", 1024, 1024))
PACK_GROUPS = ((MATS[0:2], 1280), (MATS[2:6], 800), (MATS[6:10], 2272))
N_CHIPS = 4
PACK_W = 1024
HALF_ROWS = 2176
PACK_ROWS = 2 * HALF_ROWS
VECS = (("l0_norm", 0, 0, 1024), ("l1_norm", 1, 0, 1024), ("l2_norm", 2, 0, 1024), ("l3_norm", 3, 0, 1024),
        ("l1_q_a_norm", 4, 0, 256), ("l1_kv_a_norm", 4, 256, 128), ("l1_q_head_norm", 4, 384, 192),
        ("l1_k_head_norm", 4, 576, 192), ("l2_q_head_norm", 4, 768, 64), ("l2_k_head_norm", 4, 832, 64),
        ("l2_sinks", 4, 896, 16))
LOSS_SLOT = (4, 912)
VEC_ROWS = 8


def _shard_rows(k, n):
    return k * n // N_CHIPS // PACK_W


def _group_start(gi):
    return sum(rows for _, rows in PACK_GROUPS[:gi])


def pack_shards(shards):
    parts = []
    for mats, rows in PACK_GROUPS:
        group = [shards[name].reshape(-1, PACK_W) for name, _, _, _ in mats]
        used = sum(p.shape[0] for p in group)
        parts += group + [jnp.zeros((rows - used, PACK_W), group[0].dtype)]
    return jnp.concatenate(parts, axis=0)


def unpack_shards(flat):
    out = {}
    for gi, (mats, _) in enumerate(PACK_GROUPS):
        r0 = _group_start(gi)
        for name, kind, k, n in mats:
            rows = _shard_rows(k, n)
            shape = (k, n // N_CHIPS) if kind == "col" else (k // N_CHIPS, n)
            out[name] = flat[r0:r0 + rows].reshape(shape)
            r0 += rows
    return out


def pack_full(full, groups=(0, 1, 2)):
    parts = []
    for mats, rows in [PACK_GROUPS[gi] for gi in groups]:
        group = []
        for name, kind, k, n in mats:
            m = full[name]
            if kind == "col":
                m = m.reshape(k, N_CHIPS, n // N_CHIPS).transpose(1, 0, 2)
            group.append(m.reshape(N_CHIPS, -1, PACK_W))
        used = sum(p.shape[1] for p in group)
        parts += group + [jnp.zeros((N_CHIPS, rows - used, PACK_W), group[0].dtype)]
    return jnp.concatenate(parts, axis=1)


def unpack_full(stacked, gi):
    out, r0 = {}, 0
    for name, kind, k, n in PACK_GROUPS[gi][0]:
        rows = _shard_rows(k, n)
        seg = stacked[:, r0:r0 + rows]
        if kind == "col":
            out[name] = seg.reshape(N_CHIPS, k, n // N_CHIPS).transpose(1, 0, 2).reshape(k, n)
        else:
            out[name] = seg.reshape(k, n)
        r0 += rows
    return out


def pack_vecs(vecs, loss=None):
    rows = []
    for r in range(VEC_ROWS):
        items = [(off, vecs[name]) for name, rr, off, _ in VECS if rr == r]
        if loss is not None and r == LOSS_SLOT[0]:
            items.append((LOSS_SLOT[1], loss.reshape(1)))
        pos, parts = 0, []
        for off, v in sorted(items, key=lambda t: t[0]):
            assert off == pos
            parts.append(v.astype(F32))
            pos += v.shape[0]
        parts.append(jnp.zeros((PACK_W - pos,), F32))
        rows.append(jnp.concatenate(parts))
    return jnp.stack(rows)


def unpack_vecs(block):
    return {name: block[r, off:off + n] for name, r, off, n in VECS}


def _me():
    return lax.axis_index("x"), lax.axis_index("y"), lax.axis_index("c")


OTHER_CHIPS = ((1, 0), (0, 1), (1, 1))


def _remote(src, dst, send_sem, recv_sem, to):
    return pltpu.make_async_remote_copy(src_ref=src, dst_ref=dst, send_sem=send_sem, recv_sem=recv_sem,
                                        device_id=to, device_id_type=MESH)


def gather_weights(block):
    def body(in_ref, out_ref, send_sems, recv_sems):
        send, forward, finish = _gather_steps(in_ref, out_ref, send_sems, recv_sems)
        send()
        forward()
        finish()

    hbm = pl.BlockSpec(memory_space=pltpu.HBM)
    others = pl.pallas_call(
        body, name="gather_weights",
        out_shape=jax.ShapeDtypeStruct((N_CHIPS,) + block.shape, block.dtype),
        in_specs=[hbm], out_specs=hbm,
        scratch_shapes=list(GATHER_SEMS),
    )(block)
    return place_own_block(others, block)


GATHER_SEMS = (pltpu.SemaphoreType.DMA((6,)), pltpu.SemaphoreType.DMA((6,)))


def place_own_block(others, block):
    return lax.dynamic_update_slice(others, block[None], (2 * lax.axis_index("x") + lax.axis_index("y"), 0, 0))


def _gather_steps(in_ref, out_ref, send_sems, recv_sems):
    hr = in_ref.shape[0] // 2
    x, y, c = _me()
    sibling = (x, y, 1 - c)
    chips = [(x ^ dx, y ^ dy) for dx, dy in OTHER_CHIPS]

    def half(px, py, pc):
        return out_ref.at[2 * px + py, pl.ds(pc * hr, hr), :]

    def firsts():
        return [_remote(in_ref.at[pl.ds(c * hr, hr), :], half(x, y, c), send_sems.at[j], recv_sems.at[j], (*chip, c))
                for j, chip in enumerate(chips)]

    def passes():
        return [_remote(half(*chip, c), half(*chip, c), send_sems.at[3 + j], recv_sems.at[3 + j], sibling)
                for j, chip in enumerate(chips)]

    def send():
        for cp in firsts():
            cp.start()

    def forward():
        for j, (chip, cp) in enumerate(zip(chips, passes())):
            _remote(half(*chip, c), half(*chip, c), send_sems.at[j], recv_sems.at[j], (*chip, c)).wait_recv()
            cp.start()

    def finish():
        for j, chip in enumerate(chips):
            _remote(half(*chip, 1 - c), half(*chip, 1 - c), send_sems.at[3 + j], recv_sems.at[3 + j], sibling).wait_recv()
        for cp in firsts() + passes():
            cp.wait_send()

    return send, forward, finish


def pair_exchange(g, tag):
    hr = g.shape[1] // 2

    def body(g_ref, out_ref, send_sems, recv_sems):
        x, y, c = _me()
        sibling = (x, y, 1 - c)
        copies = [_remote(g_ref.at[k, pl.ds((1 - c) * hr, hr), :], out_ref.at[k], send_sems.at[k],
                          recv_sems.at[k], sibling) for k in range(N_CHIPS)]
        for cp in copies:
            cp.start()
        for cp in copies:
            cp.wait_recv()
        for cp in copies:
            cp.wait_send()

    hbm = pl.BlockSpec(memory_space=pltpu.HBM)
    return pl.pallas_call(
        body, name=f"pair_exchange_{tag}",
        out_shape=jax.ShapeDtypeStruct((N_CHIPS, hr, PACK_W), g.dtype),
        in_specs=[hbm], out_specs=hbm,
        scratch_shapes=[pltpu.SemaphoreType.DMA((N_CHIPS,)), pltpu.SemaphoreType.DMA((N_CHIPS,))],
    )(g)


CHIP_SEMS = (pltpu.SemaphoreType.DMA((3,)), pltpu.SemaphoreType.DMA((3,)))


def _chip_exchange_steps(p_ref, out_ref, send_sems, recv_sems):
    x, y, c = _me()

    def copies():
        return [_remote(p_ref.at[2 * (x ^ dx) + (y ^ dy)], out_ref.at[j], send_sems.at[j], recv_sems.at[j],
                        (x ^ dx, y ^ dy, c)) for j, (dx, dy) in enumerate(OTHER_CHIPS)]

    def send():
        for cp in copies():
            cp.start()

    def finish():
        for cp in copies():
            cp.wait_recv()
        for cp in copies():
            cp.wait_send()

    return send, finish


def chip_exchange(part, tag):
    def body(p_ref, out_ref, send_sems, recv_sems):
        send, finish = _chip_exchange_steps(p_ref, out_ref, send_sems, recv_sems)
        send()
        finish()

    hbm = pl.BlockSpec(memory_space=pltpu.HBM)
    return pl.pallas_call(
        body, name=f"chip_exchange_{tag}",
        out_shape=jax.ShapeDtypeStruct((len(OTHER_CHIPS),) + part.shape[1:], part.dtype),
        in_specs=[hbm], out_specs=hbm,
        scratch_shapes=list(CHIP_SEMS),
    )(part)


def join_halves(half, tag):
    hr = half.shape[0]

    def body(h_ref, out_ref, send_sem, recv_sem):
        x, y, c = _me()
        cp = _remote(h_ref, out_ref.at[pl.ds(c * hr, hr), :], send_sem, recv_sem, (x, y, 1 - c))
        cp.start()
        _remote(h_ref, out_ref.at[pl.ds((1 - c) * hr, hr), :], send_sem, recv_sem, (x, y, 1 - c)).wait_recv()
        cp.wait_send()

    hbm = pl.BlockSpec(memory_space=pltpu.HBM)
    other = pl.pallas_call(
        body, name=f"join_halves_{tag}",
        out_shape=jax.ShapeDtypeStruct((2 * hr, PACK_W), half.dtype),
        in_specs=[hbm], out_specs=hbm,
        scratch_shapes=[pltpu.SemaphoreType.DMA, pltpu.SemaphoreType.DMA],
    )(half)
    return lax.dynamic_update_slice(other, half, (lax.axis_index("c") * hr, 0))


def sum_over_devices(block):
    def body(in_ref, out_ref, all_ref, send_sems, recv_sems):
        x, y, c = _me()
        me = 4 * x + 2 * y + c
        all_ref[me] = in_ref[...]
        copies = []
        for r in range(1, 8):
            to = (x ^ (r >> 2), y ^ ((r >> 1) & 1), c ^ (r & 1))
            copies.append(_remote(in_ref, all_ref.at[me], send_sems.at[r - 1], recv_sems.at[r - 1], to))
        for cp in copies:
            cp.start()
        for r in range(1, 8):
            frm = (x ^ (r >> 2), y ^ ((r >> 1) & 1), c ^ (r & 1))
            _remote(in_ref, all_ref.at[4 * frm[0] + 2 * frm[1] + frm[2]], send_sems.at[r - 1], recv_sems.at[r - 1],
                    frm).wait_recv()
        for cp in copies:
            cp.wait_send()
        acc = all_ref[0]
        for d in range(1, 8):
            acc = acc + all_ref[d]
        out_ref[...] = acc

    vmem = pl.BlockSpec(memory_space=pltpu.VMEM)
    return pl.pallas_call(
        body, name="sum_over_devices",
        out_shape=jax.ShapeDtypeStruct(block.shape, F32),
        in_specs=[vmem], out_specs=vmem,
        scratch_shapes=[pltpu.VMEM((8,) + block.shape, F32), pltpu.SemaphoreType.DMA((7,)), pltpu.SemaphoreType.DMA((7,))],
    )(block)


def pair_sum(g, got, core, tag):
    hr = got.shape[1]
    rows = _tile(hr, 512)
    steps = hr // rows

    def body(s_ref, g_ref, r_ref, o_ref, ob_ref):
        t = g_ref[...] + r_ref[...]
        o_ref[...] = t
        ob_ref[...] = t.astype(BF16)

    blk = pl.BlockSpec((1, rows, PACK_W), lambda k, i, s: (k, i, 0))
    return pl.pallas_call(
        body, name=f"pair_sum_{tag}",
        grid_spec=pltpu.PrefetchScalarGridSpec(
            num_scalar_prefetch=1, grid=(N_CHIPS, steps),
            in_specs=[pl.BlockSpec((1, rows, PACK_W), lambda k, i, s: (k, s[0] * steps + i, 0)), blk],
            out_specs=[blk, blk]),
        out_shape=[jax.ShapeDtypeStruct((N_CHIPS, hr, PACK_W), F32), jax.ShapeDtypeStruct((N_CHIPS, hr, PACK_W), BF16)],
        compiler_params=_params(("parallel", "parallel"), 8 * rows * PACK_W * 4),
    )(core, g, got)


def chip_sum(part, got, chip, tag):
    hr = part.shape[1]
    rows = _tile(hr, 512)

    def body(s_ref, p_ref, a_ref, b_ref, c_ref, o_ref):
        o_ref[...] = ((p_ref[0] + a_ref[0].astype(F32)) + b_ref[0].astype(F32)) + c_ref[0].astype(F32)

    def got_spec(j):
        return pl.BlockSpec((1, rows, PACK_W), lambda i, s: (j, i, 0))

    return pl.pallas_call(
        body, name=f"chip_sum_{tag}",
        grid_spec=pltpu.PrefetchScalarGridSpec(
            num_scalar_prefetch=1, grid=(hr // rows,),
            in_specs=[pl.BlockSpec((1, rows, PACK_W), lambda i, s: (s[0], i, 0)), got_spec(0), got_spec(1), got_spec(2)],
            out_specs=pl.BlockSpec((rows, PACK_W), lambda i, s: (i, 0))),
        out_shape=jax.ShapeDtypeStruct((hr, PACK_W), F32),
        compiler_params=_params(("parallel",), 10 * rows * PACK_W * 4),
    )(chip, part, got, got, got)


def reduce_scatter_begin(g, tag):
    c = lax.axis_index("c")
    return pair_sum(g, pair_exchange(g, tag), jnp.reshape(c, (1,)).astype(I32), tag)


def reduce_scatter_end(part, got, tag):
    chip = 2 * lax.axis_index("x") + lax.axis_index("y")
    return join_halves(chip_sum(part, got, jnp.reshape(chip, (1,)).astype(I32), tag), tag)


def reduce_scatter(g, tag):
    part, part_bf16 = reduce_scatter_begin(g, tag)
    return reduce_scatter_end(part, chip_exchange(part_bf16, tag), tag)


def adamw(w, g, m, v, *, name):
    rows, cols = w.shape
    tm = _tile(rows, 512)
    c1 = 1.0 - ADAM_B1 ** ADAM_STEP
    c2 = 1.0 - ADAM_B2 ** ADAM_STEP

    def body(w_ref, g_ref, m_ref, v_ref, d_ref, mo_ref, vo_ref):
        gv = g_ref[...]
        mn = ADAM_B1 * m_ref[...] + (1.0 - ADAM_B1) * gv
        vn = ADAM_B2 * v_ref[...] + (1.0 - ADAM_B2) * (gv * gv)
        d_ref[...] = -ADAM_LR * ((mn / c1) / (jnp.sqrt(vn / c2) + ADAM_EPS) + ADAM_WD * w_ref[...])
        mo_ref[...] = mn
        vo_ref[...] = vn

    blk = pl.BlockSpec((tm, cols), lambda i: (i, 0))
    shape = jax.ShapeDtypeStruct(w.shape, F32)
    return _pallas(
        body, name=name, grid=(rows // tm,),
        in_specs=[blk] * 4, out_specs=[blk] * 3, out_shape=[shape] * 3,
        compiler_params=_params(("parallel",), 16 * tm * cols * 4),
    )(w, g, m, v)


WEIGHTS = ("l0_norm", "l0_w_in", "l0_w_out", "l1_norm", "l1_w_in", "l1_q_a_norm", "l1_w_uq", "l1_kv_a_norm", "l1_w_ukv",
           "l1_q_head_norm", "l1_k_head_norm", "l1_w_out", "l2_norm", "l2_w_in", "l2_q_head_norm", "l2_k_head_norm",
           "l2_sinks", "l2_w_out", "l3_norm", "l3_w_in", "l3_w_out")


def kernel(x, l0_norm, l0_w_in, l0_w_out, l1_norm, l1_w_in, l1_q_a_norm, l1_w_uq, l1_kv_a_norm, l1_w_ukv, l1_q_head_norm, l1_k_head_norm, l1_w_out, l2_norm, l2_w_in, l2_q_head_norm, l2_k_head_norm, l2_sinks, l2_w_out, l3_norm, l3_w_in, l3_w_out, loss_target, m_l0_norm, m_l0_w_in, m_l0_w_out, m_l1_norm, m_l1_w_in, m_l1_q_a_norm, m_l1_w_uq, m_l1_kv_a_norm, m_l1_w_ukv, m_l1_q_head_norm, m_l1_k_head_norm, m_l1_w_out, m_l2_norm, m_l2_w_in, m_l2_q_head_norm, m_l2_k_head_norm, m_l2_sinks, m_l2_w_out, m_l3_norm, m_l3_w_in, m_l3_w_out, v_l0_norm, v_l0_w_in, v_l0_w_out, v_l1_norm, v_l1_w_in, v_l1_q_a_norm, v_l1_w_uq, v_l1_kv_a_norm, v_l1_w_ukv, v_l1_q_head_norm, v_l1_k_head_norm, v_l1_w_out, v_l2_norm, v_l2_w_in, v_l2_q_head_norm, v_l2_k_head_norm, v_l2_sinks, v_l2_w_out, v_l3_norm, v_l3_w_in, v_l3_w_out):
    given = dict(locals())
    w = {n: given[n] for n in WEIGHTS}
    m = {n: given["m_" + n] for n in WEIGHTS}
    v = {n: given["v_" + n] for n in WEIGHTS}
    mat_names = [t[0] for t in MATS]
    vec_names = [t[0] for t in VECS]

    packed = pack_shards({n: w[n] for n in mat_names}).astype(BF16)
    part = lambda gi: packed[_group_start(gi):_group_start(gi + 1)]
    full = unpack_full(gather_weights(part(0)), 0)
    full.update({n: w[n] for n in vec_names})
    late = [(part(gi), lambda gathered, gi=gi: unpack_full(gathered, gi)) for gi in (1, 2)]
    pair_sums = []

    def early(named):
        part, part_bf16 = reduce_scatter_begin(pack_full(named, groups=(1, 2)), "late")
        pair_sums.append(part)
        return part_bf16

    loss_tile, grad_x, grads, received = local_step(x[0], loss_target[0], full, late, early)
    block = jnp.concatenate([reduce_scatter(pack_full(grads, groups=(0,)), "first"),
                             reduce_scatter_end(pair_sums[0], received, "late")], axis=0)
    g = unpack_shards(block)
    vec_sum = sum_over_devices(pack_vecs({n: grads[n] for n in vec_names}, loss=loss_tile[0, 0]))
    loss = vec_sum[LOSS_SLOT[0], LOSS_SLOT[1]]

    delta, new_m, new_v = {}, {}, {}
    for n in mat_names:
        delta[n], new_m[n], new_v[n] = adamw(w[n], g[n], m[n], v[n], name=f"adamw_{n}")
    dv, mv, vv = adamw(pack_vecs(w), vec_sum, pack_vecs(m), pack_vecs(v), name="adamw_vecs")
    g.update(unpack_vecs(vec_sum))
    delta.update(unpack_vecs(dv))
    new_m.update(unpack_vecs(mv))
    new_v.update(unpack_vecs(vv))
    return (loss, grad_x[None], *[g[n] for n in WEIGHTS], *[delta[n] for n in WEIGHTS],
            *[new_m[n] for n in WEIGHTS], *[new_v[n] for n in WEIGHTS])
```
